```python
import math
import jax, jax.numpy as jnp
from jax import lax
import numpy as np

D_MODEL = 2048
BATCH = 2
SEQ = 4096
DEPTH = 1

D_MIX = D_MODEL
D_ATTN = D_MIX // 2
D_CONV = D_MIX - D_ATTN
HEAD_DIM = 128
N_Q_HEADS = D_ATTN // HEAD_DIM
N_KV_HEADS = 2
GQA = N_Q_HEADS // N_KV_HEADS
D_KV = N_KV_HEADS * HEAD_DIM
L_CMP = 32
STRIDE_CMP = 16
L_SEL = 64
N_SEL = 16
WINDOW = 512
Q_BLOCK = 128
FORCED_SCORE = float(GQA + 1)
CONV_WIDTH = 3
CONV_GROUPS = 8
NUM_BUCKETS = 32
MAX_DISTANCE = 128
PEER_HEADS = 8
N_KEYS = 128
N_EXPERTS = N_KEYS * N_KEYS
PEER_TOPK = 16
D_KEY = 256
D_KEY_HALF = D_KEY // 2
TOK_BLOCK = 128
EPS = 1e-6
IN_SPLITS = (D_ATTN, D_KV, D_KV, D_KV, D_KV, D_KV, D_KV, 3 * N_Q_HEADS, D_CONV, D_CONV, D_CONV)
D_IN_PROJ = sum(IN_SPLITS)

kernel_name = 'hymba_nsa_shortconv_peer_layer'


def rmsnorm(x, w):
    xf = x.astype(jnp.float32)
    y = xf * lax.rsqrt(jnp.mean(xf * xf, axis=-1, keepdims=True) + EPS)
    return y.astype(x.dtype) * w


def group_rmsnorm(x, w, n_groups):
    shp = x.shape
    xg = x.reshape(shp[:-1] + (n_groups, shp[-1] // n_groups)).astype(jnp.float32)
    y = xg * lax.rsqrt(jnp.mean(xg * xg, axis=-1, keepdims=True) + EPS)
    return y.reshape(shp).astype(x.dtype) * w


def masked_softmax(s, mask):
    s = jnp.where(mask, s, -jnp.inf)
    m = jnp.max(s, axis=-1, keepdims=True)
    m = jnp.where(jnp.isfinite(m), m, 0.0)
    e = jnp.exp(s - m)
    d = jnp.sum(e, axis=-1, keepdims=True)
    return e / jnp.where(d > 0, d, 1.0)


def t5_bucket(dist):
    n = jnp.maximum(dist, 0)
    max_exact = NUM_BUCKETS // 2
    nf = jnp.maximum(n, 1).astype(jnp.float32)
    large = max_exact + (jnp.log(nf / max_exact) / math.log(MAX_DISTANCE / max_exact)
                         * (NUM_BUCKETS - max_exact)).astype(jnp.int32)
    large = jnp.minimum(large, NUM_BUCKETS - 1)
    return jnp.where(n < max_exact, n, large)


def selection_overlap(S):
    n_c = (S - L_CMP) // STRIDE_CMP + 1
    n_sel = S // L_SEL
    pos = np.arange(n_c)[:, None] * STRIDE_CMP + np.arange(L_CMP)[None, :]
    m = np.zeros((n_c, n_sel), np.float32)
    np.add.at(m, (np.repeat(np.arange(n_c), L_CMP), (pos // L_SEL).reshape(-1)), 1.0 / L_CMP)
    return jnp.asarray(m)


def nsa_mixer(q, k_cmp_in, v_cmp_in, k_sel, v_sel, k_win, v_win, gate_logits,
              w_cmp_k, w_cmp_v, cmp_pos, rel_bias):
    B, S = q.shape[:2]
    scale = HEAD_DIM ** -0.5
    q = q.reshape(B, S, N_KV_HEADS, GQA, HEAD_DIM)
    kvs = lambda a: a.reshape(B, S, N_KV_HEADS, HEAD_DIM)
    k_cmp_in, v_cmp_in, k_sel, v_sel, k_win, v_win = map(kvs, (k_cmp_in, v_cmp_in, k_sel, v_sel, k_win, v_win))
    t = np.arange(S)

    n_c = (S - L_CMP) // STRIDE_CMP + 1
    blk_pos = np.arange(n_c)[:, None] * STRIDE_CMP + np.arange(L_CMP)[None, :]

    def compress(k, w):
        blocks = k[:, blk_pos] + cmp_pos[None, None, :, None, :]
        return jnp.einsum('bnlhd,lde->bnhe', blocks, w)

    kc = compress(k_cmp_in, w_cmp_k)
    vc = compress(v_cmp_in, w_cmp_v)
    dist_c = t[:, None] - blk_pos[:, -1][None, :]
    bias_c = rel_bias[t5_bucket(dist_c)].reshape(S, n_c, N_KV_HEADS, GQA).transpose(2, 3, 0, 1)
    s_c = jnp.einsum('bqhgd,bnhd->bhgqn', q, kc).astype(jnp.float32) * scale + bias_c
    p_c = masked_softmax(s_c, dist_c >= 0)
    o_cmp = jnp.einsum('bhgqn,bnhd->bqhgd', p_c.astype(vc.dtype), vc)

    n_sel = S // L_SEL
    n_top = min(N_SEL, n_sel)
    imp = jnp.einsum('bhgqn,nj->bhqj', p_c, selection_overlap(S))
    j = np.arange(n_sel)[None, :]
    blk_t = (t // L_SEL)[:, None]
    forced = (j == 0) | (j == blk_t) | (j == blk_t - 1)
    causal_blk = j * L_SEL <= t[:, None]
    score = jnp.where(forced, FORCED_SCORE, jnp.where(causal_blk, imp, -1.0))
    _, sel_idx = lax.top_k(score, n_top)

    n_qb = S // Q_BLOCK
    q_ch = q.reshape(B, n_qb, Q_BLOCK, N_KV_HEADS, GQA, HEAD_DIM).transpose(1, 0, 2, 3, 4, 5)
    idx_ch = sel_idx.reshape(B, N_KV_HEADS, n_qb, Q_BLOCK, n_top).transpose(2, 0, 1, 3, 4)
    ks_blk = k_sel.reshape(B, n_sel, L_SEL, N_KV_HEADS, HEAD_DIM).transpose(0, 3, 1, 2, 4)
    vs_blk = v_sel.reshape(B, n_sel, L_SEL, N_KV_HEADS, HEAD_DIM).transpose(0, 3, 1, 2, 4)
    pad = ((0, 0), (WINDOW, 0), (0, 0), (0, 0))
    kw_pad = jnp.pad(k_win, pad)
    vw_pad = jnp.pad(v_win, pad)
    rel_bias_g = rel_bias.reshape(NUM_BUCKETS, N_KV_HEADS, GQA)
    gather = jax.vmap(jax.vmap(lambda blocks, ids: blocks[ids]))
    h_ar = jnp.arange(N_KV_HEADS)[None, :, None, None]
    win_off = jnp.arange(WINDOW + Q_BLOCK) - WINDOW
    n_k = n_top * L_SEL

    def block_fn(args):
        c, qc, ic = args
        t_q = c * Q_BLOCK + jnp.arange(Q_BLOCK)
        kg = gather(ks_blk, ic).reshape(B, N_KV_HEADS, Q_BLOCK, n_k, HEAD_DIM)
        vg = gather(vs_blk, ic).reshape(B, N_KV_HEADS, Q_BLOCK, n_k, HEAD_DIM)
        pos = (ic[..., None] * L_SEL + jnp.arange(L_SEL)).reshape(B, N_KV_HEADS, Q_BLOCK, n_k)
        dist = t_q[None, None, :, None] - pos
        bias = rel_bias_g[t5_bucket(dist), h_ar].transpose(0, 1, 4, 2, 3)
        s = jnp.einsum('bqhgd,bhqkd->bhgqk', qc, kg).astype(jnp.float32) * scale + bias
        p = masked_softmax(s, (dist >= 0)[:, :, None])
        o_s = jnp.einsum('bhgqk,bhqkd->bqhgd', p.astype(vg.dtype), vg)
        kw = lax.dynamic_slice_in_dim(kw_pad, c * Q_BLOCK, WINDOW + Q_BLOCK, axis=1)
        vw = lax.dynamic_slice_in_dim(vw_pad, c * Q_BLOCK, WINDOW + Q_BLOCK, axis=1)
        pos_w = c * Q_BLOCK + win_off
        dist_w = t_q[:, None] - pos_w[None, :]
        valid_w = (dist_w >= 0) & (dist_w < WINDOW) & (pos_w >= 0)[None, :]
        bias_w = rel_bias[t5_bucket(dist_w)].reshape(Q_BLOCK, WINDOW + Q_BLOCK, N_KV_HEADS, GQA).transpose(2, 3, 0, 1)
        s_w = jnp.einsum('bqhgd,bkhd->bhgqk', qc, kw).astype(jnp.float32) * scale + bias_w
        p_w = masked_softmax(s_w, valid_w)
        o_w = jnp.einsum('bhgqk,bkhd->bqhgd', p_w.astype(vw.dtype), vw)
        return o_s, o_w

    o_sel, o_win = lax.map(block_fn, (jnp.arange(n_qb), q_ch, idx_ch))
    unblock = lambda o: o.transpose(1, 0, 2, 3, 4, 5).reshape(B, S, N_KV_HEADS, GQA, HEAD_DIM)
    o_sel, o_win = unblock(o_sel), unblock(o_win)

    g = jax.nn.sigmoid(gate_logits.astype(jnp.float32)).reshape(B, S, N_KV_HEADS, GQA, 3).astype(q.dtype)
    o = g[..., 0:1] * o_cmp + g[..., 1:2] * o_sel + g[..., 2:3] * o_win
    return o.reshape(B, S, D_ATTN)


def short_conv_mixer(b_gate, c_gate, h, conv_w, conv_b):
    u = c_gate * h
    y = lax.conv_general_dilated(u, conv_w[:, None, :].astype(u.dtype), window_strides=(1,),
                                 padding=[(CONV_WIDTH - 1, 0)],
                                 dimension_numbers=('NWC', 'WIO', 'NWC'),
                                 feature_group_count=D_CONV) + conv_b
    return b_gate * y


def peer_ffn(x, wq, subkeys, u_tab, v_tab):
    B, S, D = x.shape
    T = B * S
    xt = x.reshape(T, D)
    q = (xt @ wq).reshape(T, PEER_HEADS, 2, D_KEY_HALF)
    s = jnp.einsum('thcd,hcnd->thcn', q, subkeys).astype(jnp.float32)
    sv, si = lax.top_k(s, PEER_TOPK)
    cand = (sv[:, :, 0, :, None] + sv[:, :, 1, None, :]).reshape(T, PEER_HEADS, PEER_TOPK * PEER_TOPK)
    cv, ci = lax.top_k(cand, PEER_TOPK)
    i1 = jnp.take_along_axis(si[:, :, 0], ci // PEER_TOPK, axis=-1)
    i2 = jnp.take_along_axis(si[:, :, 1], ci % PEER_TOPK, axis=-1)
    experts = i1 * N_KEYS + i2
    gates = jax.nn.softmax(cv, axis=-1)
    n_tb = T // TOK_BLOCK

    def block_fn(args):
        xc, ec, gc = args
        u = u_tab[ec]
        a = jax.nn.gelu(jnp.einsum('thkd,td->thk', u, xc).astype(jnp.float32))
        w = (gc * a).astype(xc.dtype)
        return jnp.einsum('thk,thkd->td', w, v_tab[ec])

    out = lax.map(block_fn, (xt.reshape(n_tb, TOK_BLOCK, D),
                             experts.reshape(n_tb, TOK_BLOCK, PEER_HEADS, PEER_TOPK),
                             gates.reshape(n_tb, TOK_BLOCK, PEER_HEADS, PEER_TOPK)))
    return out.reshape(B, S, D)


def setup_inputs(seed: int = 0) -> dict:
    key = jax.random.key(seed)
    ks = jax.random.split(key, 20)
    nrm = lambda k, shp, sc: jax.random.normal(k, shp, jnp.float32) * sc
    return {
        'x': nrm(ks[0], (BATCH, SEQ, D_MODEL), 1.0),
        'attn_norm_w': 1.0 + nrm(ks[1], (DEPTH, D_MODEL), 0.02),
        'w_in': nrm(ks[2], (DEPTH, D_MODEL, D_IN_PROJ), D_MODEL ** -0.5),
        'w_cmp_k': nrm(ks[3], (DEPTH, L_CMP, HEAD_DIM, HEAD_DIM), (L_CMP * HEAD_DIM) ** -0.5),
        'w_cmp_v': nrm(ks[4], (DEPTH, L_CMP, HEAD_DIM, HEAD_DIM), (L_CMP * HEAD_DIM) ** -0.5),
        'cmp_pos': nrm(ks[5], (DEPTH, L_CMP, HEAD_DIM), 0.1),
        'conv_w': nrm(ks[6], (DEPTH, CONV_WIDTH, D_CONV), CONV_WIDTH ** -0.5),
        'conv_b': nrm(ks[7], (DEPTH, D_CONV), 0.02),
        'attn_group_norm_w': 1.0 + nrm(ks[8], (DEPTH, D_ATTN), 0.02),
        'conv_group_norm_w': 1.0 + nrm(ks[9], (DEPTH, D_CONV), 0.02),
        'w_out': nrm(ks[10], (DEPTH, D_MIX, D_MODEL), D_MIX ** -0.5),
        'rel_bias': nrm(ks[11], (NUM_BUCKETS, N_Q_HEADS), 0.2),
        'ffn_norm_w': 1.0 + nrm(ks[12], (DEPTH, D_MODEL), 0.02),
        'peer_wq': nrm(ks[13], (DEPTH, D_MODEL, PEER_HEADS * D_KEY), D_MODEL ** -0.5),
        'peer_subkeys': nrm(ks[14], (DEPTH, PEER_HEADS, 2, N_KEYS, D_KEY_HALF), D_KEY_HALF ** -0.5),
        'peer_u': nrm(ks[15], (DEPTH, N_EXPERTS, D_MODEL), D_MODEL ** -0.5),
        'peer_v': nrm(ks[16], (DEPTH, N_EXPERTS, D_MODEL), (PEER_HEADS * PEER_TOPK) ** -0.5),
        'final_norm_w': 1.0 + nrm(ks[17], (D_MODEL,), 0.02),
    }


def reference(x, attn_norm_w, w_in, w_cmp_k, w_cmp_v, cmp_pos, conv_w, conv_b,
              attn_group_norm_w, conv_group_norm_w, w_out, rel_bias, ffn_norm_w,
              peer_wq, peer_subkeys, peer_u, peer_v, final_norm_w):
    split_at = list(np.cumsum(IN_SPLITS)[:-1])
    h = x
    for l in range(DEPTH):
        xn = rmsnorm(h, attn_norm_w[l])
        proj = xn @ w_in[l]
        (q, k_c, v_c, k_s, v_s, k_w, v_w, gate_logits,
         b_gate, c_gate, h_conv) = jnp.split(proj, split_at, axis=-1)
        o_attn = nsa_mixer(q, k_c, v_c, k_s, v_s, k_w, v_w, gate_logits,
                           w_cmp_k[l], w_cmp_v[l], cmp_pos[l], rel_bias)
        o_conv = short_conv_mixer(b_gate, c_gate, h_conv, conv_w[l], conv_b[l])
        mixed = jnp.concatenate([group_rmsnorm(o_attn, attn_group_norm_w[l], N_Q_HEADS),
                                 group_rmsnorm(o_conv, conv_group_norm_w[l], CONV_GROUPS)], axis=-1)
        h = h + mixed @ w_out[l]
        h = h + peer_ffn(rmsnorm(h, ffn_norm_w[l]), peer_wq[l], peer_subkeys[l], peer_u[l], peer_v[l])
    return rmsnorm(h, final_norm_w)
```

```python
import functools
import math

import jax
import jax.numpy as jnp
import numpy as np
from jax import lax
from jax.experimental import pallas as pl
from jax.experimental.pallas import tpu as pltpu

F32 = jnp.float32
BF16 = jnp.bfloat16

HEAD_DIM = 128
N_KV_HEADS = 2
GQA = 4
N_Q_HEADS = N_KV_HEADS * GQA
L_CMP = 32
STRIDE_CMP = 16
L_SEL = 64
N_SEL = 16
WINDOW = 512
Q_BLOCK = 128
FORCED_SCORE = float(GQA + 1)
NUM_BUCKETS = 32
MAX_DISTANCE = 128
PEER_HEADS = 8
N_KEYS = 128
PEER_TOPK = 16
EPS = 1e-6
NEG_BIG = -1e30
SEL_OFF = -float(2 ** 30)
VMEM_LIMIT = 56 * 1024 * 1024


def _dot(a, b):
    return jnp.dot(a, b, preferred_element_type=F32)


def _dot_nt(a, b):
    return lax.dot_general(a, b, (((1,), (1,)), ((), ())), preferred_element_type=F32)


def _params(sem, vmem=VMEM_LIMIT):
    return pltpu.CompilerParams(dimension_semantics=sem, vmem_limit_bytes=vmem)


def _norm_matmul_kernel(x_ref, nw_ref, w_ref, o_ref, xn_ref):
    @pl.when(pl.program_id(1) == 0)
    def _():
        x = x_ref[...]
        y = x * lax.rsqrt(jnp.mean(x * x, axis=-1, keepdims=True) + EPS)
        xn_ref[...] = (y * nw_ref[...]).astype(BF16)

    o_ref[...] = _dot(xn_ref[...], w_ref[...]).astype(o_ref.dtype)


def norm_matmul(x, norm_w, w, out_dtype, tm, tn):
    T, D = x.shape
    N = w.shape[1]
    return pl.pallas_call(
        _norm_matmul_kernel,
        grid=(T // tm, N // tn),
        in_specs=[pl.BlockSpec((tm, D), lambda i, j: (i, 0)),
                  pl.BlockSpec((1, D), lambda i, j: (0, 0)),
                  pl.BlockSpec((D, tn), lambda i, j: (0, j))],
        out_specs=pl.BlockSpec((tm, tn), lambda i, j: (i, j)),
        out_shape=jax.ShapeDtypeStruct((T, N), out_dtype),
        scratch_shapes=[pltpu.VMEM((tm, D), BF16)],
        compiler_params=_params(("parallel", "arbitrary")),
    )(x, norm_w.reshape(1, D), w)


def _compress_kernel(k_ref, v_ref, pa_ref, pb_ref, wka_ref, wkb_ref, wva_ref, wvb_ref, kc_ref, vc_ref):
    def one(x_ref, wa_ref, wb_ref, o_ref):
        x = x_ref[...].astype(F32)
        a = _dot((x + pa_ref[...]).astype(BF16), wa_ref[...])
        b = _dot((x + pb_ref[...]).astype(BF16), wb_ref[...])
        n = b.shape[0]
        o_ref[...] = (a + pltpu.roll(b, n - 1, 0)).astype(o_ref.dtype)

    one(k_ref, wka_ref, wkb_ref, kc_ref)
    one(v_ref, wva_ref, wvb_ref, vc_ref)


def compress(k16, v16, pos_a, pos_b, wka, wkb, wva, wvb, B):
    R, C = k16.shape
    nb = R // B
    dkv = wka.shape[1]
    full = lambda shp: pl.BlockSpec(shp, lambda b: (0, 0))
    return pl.pallas_call(
        _compress_kernel,
        grid=(B,),
        in_specs=[pl.BlockSpec((nb, C), lambda b: (b, 0)),
                  pl.BlockSpec((nb, C), lambda b: (b, 0)),
                  full((1, C)), full((1, C)),
                  full((C, dkv)), full((C, dkv)), full((C, dkv)), full((C, dkv))],
        out_specs=[pl.BlockSpec((nb, dkv), lambda b: (b, 0))] * 2,
        out_shape=[jax.ShapeDtypeStruct((R, dkv), BF16)] * 2,
        compiler_params=_params(("parallel",)),
    )(k16, v16, pos_a, pos_b, wka, wkb, wva, wvb)


def _bias_from_dist(tab_row, dist):
    idx = jnp.clip(dist, 0, 127)
    w = tab_row.shape[1]
    tab = jnp.broadcast_to(tab_row, (idx.shape[0], w))
    parts = [jnp.take_along_axis(tab, idx[:, k:k + w], axis=1) for k in range(0, idx.shape[1], w)]
    return parts[0] if len(parts) == 1 else jnp.concatenate(parts, axis=1)


def _cmp_select_kernel(q_ref, kc_ref, vc_ref, tab_ref, ovt_ref, ocmp_ref, selb_ref, *, n_qb, scale):
    c = pl.program_id(0) % n_qb
    tq = q_ref.shape[0]
    n_c = kc_ref.shape[0]
    n_sel = ovt_ref.shape[0]
    t0 = c * tq
    t_col = t0 + lax.broadcasted_iota(jnp.int32, (tq, n_c), 0)
    n_row = lax.broadcasted_iota(jnp.int32, (tq, n_c), 1)
    dist = t_col - (n_row * STRIDE_CMP + (L_CMP - 1))
    valid = dist >= 0

    j_io = lax.broadcasted_iota(jnp.int32, (n_sel, tq), 0)
    t_io = t0 + lax.broadcasted_iota(jnp.int32, (n_sel, tq), 1)
    blk_t = t_io // L_SEL
    forced = (j_io == 0) | (j_io == blk_t) | (j_io == blk_t - 1)
    causal_blk = j_io * L_SEL <= t_io

    for h in range(N_KV_HEADS):
        kc = kc_ref[:, h * HEAD_DIM:(h + 1) * HEAD_DIM]
        vc = vc_ref[:, h * HEAD_DIM:(h + 1) * HEAD_DIM]
        psum = jnp.zeros((tq, n_c), F32)
        for g in range(GQA):
            hd = h * GQA + g
            qh = q_ref[:, hd * HEAD_DIM:(hd + 1) * HEAD_DIM]
            bias = _bias_from_dist(tab_ref[hd:hd + 1, :], dist)
            s = _dot_nt(qh, kc) * scale + bias
            s = jnp.where(valid, s, NEG_BIG)
            m = jnp.max(s, axis=-1, keepdims=True)
            e = jnp.where(valid, jnp.exp(s - m), 0.0)
            d = jnp.sum(e, axis=-1, keepdims=True)
            p = e / jnp.where(d > 0, d, 1.0)
            ocmp_ref[:, hd * HEAD_DIM:(hd + 1) * HEAD_DIM] = _dot(p.astype(BF16), vc)
            psum = psum + p
        p_hi = psum.astype(BF16)
        p_lo = (psum - p_hi.astype(F32)).astype(BF16)
        ovt = ovt_ref[...]
        imp = _dot_nt(ovt, p_hi) + _dot_nt(ovt, p_lo)
        score = jnp.where(forced, FORCED_SCORE, jnp.where(causal_blk, imp, -1.0))
        rank = jnp.zeros((n_sel, tq), F32)
        for jp in range(n_sel):
            row = score[jp:jp + 1, :]
            rank = rank + jnp.where(j_io > jp, jnp.where(row >= score, 1.0, 0.0),
                                    jnp.where(row > score, 1.0, 0.0))
        selb = jnp.where(rank < float(N_SEL), 0.0, SEL_OFF)
        if n_sel < HEAD_DIM:
            selb = jnp.concatenate([selb, jnp.zeros((HEAD_DIM - n_sel, tq), F32)], axis=0)
        selb_ref[:, h * HEAD_DIM:(h + 1) * HEAD_DIM] = selb.T.astype(BF16)


def cmp_select(qkv, kc, vc, tab, ovt, B, S):
    T = qkv.shape[0]
    tq = Q_BLOCK
    n_qb = S // tq
    n_c = kc.shape[0] // B
    dq = N_Q_HEADS * HEAD_DIM
    dkv = N_KV_HEADS * HEAD_DIM
    kern = functools.partial(_cmp_select_kernel, n_qb=n_qb, scale=HEAD_DIM ** -0.5)
    return pl.pallas_call(
        kern,
        grid=(T // tq,),
        in_specs=[pl.BlockSpec((tq, dq), lambda i: (i, 0)),
                  pl.BlockSpec((n_c, dkv), lambda i: (i // n_qb, 0)),
                  pl.BlockSpec((n_c, dkv), lambda i: (i // n_qb, 0)),
                  pl.BlockSpec(tab.shape, lambda i: (0, 0)),
                  pl.BlockSpec(ovt.shape, lambda i: (0, 0))],
        out_specs=[pl.BlockSpec((tq, dq), lambda i: (i, 0)),
                   pl.BlockSpec((tq, dkv), lambda i: (i, 0))],
        out_shape=[jax.ShapeDtypeStruct((T, dq), F32),
                   jax.ShapeDtypeStruct((T, dkv), BF16)],
        compiler_params=_params(("parallel",)),
    )(qkv, kc, vc, tab, ovt)


def _flash_step(q, k, v, bias, mask, m_ref, l_ref, acc_ref, scale):
    s = _dot_nt(q, k) * scale + bias
    if mask is not None:
        s = jnp.where(mask, s, NEG_BIG)
    m_old = m_ref[...]
    m_new = jnp.maximum(m_old, jnp.max(s, axis=-1, keepdims=True))
    alpha = jnp.exp(m_old - m_new)
    p = jnp.exp(s - m_new)
    l_ref[...] = alpha * l_ref[...] + jnp.sum(p, axis=-1, keepdims=True)
    acc_ref[...] = alpha * acc_ref[...] + _dot(p.astype(BF16), v)
    m_ref[...] = m_new


def _sel_win_kernel(q_ref, selb_ref, ks_ref, vs_ref, kw_ref, vw_ref, e2_ref, tab_ref, ocmp_ref,
                    gl_ref, gnw_ref, o_ref, m_ref, l_ref, acc_ref, *, n_qb, scale):
    c = pl.program_id(0) % n_qb
    tq = q_ref.shape[0]
    rows = GQA * tq
    i_io = lax.broadcasted_iota(jnp.int32, (rows, tq), 0) % tq
    j_io = lax.broadcasted_iota(jnp.int32, (rows, tq), 1)
    dij = i_io - j_io
    causal = dij >= 0
    sig = jax.nn.sigmoid(gl_ref[...])

    def ktile(ref, kt, h):
        return ref[pl.ds(pl.multiple_of(kt * tq, tq), tq), h * HEAD_DIM:(h + 1) * HEAD_DIM]

    def reset():
        m_ref[...] = jnp.full(m_ref.shape, NEG_BIG, F32)
        l_ref[...] = jnp.zeros(l_ref.shape, F32)
        acc_ref[...] = jnp.zeros(acc_ref.shape, F32)

    for h in range(N_KV_HEADS):
        q4 = jnp.concatenate([q_ref[:, (h * GQA + g) * HEAD_DIM:(h * GQA + g + 1) * HEAD_DIM]
                              for g in range(GQA)], axis=0)
        sb = selb_ref[:, h * HEAD_DIM:(h + 1) * HEAD_DIM]
        q_aug = jnp.concatenate([q4, jnp.concatenate([sb] * GQA, axis=0)], axis=1)
        tabs = [tab_ref[h * GQA + g:h * GQA + g + 1, :] for g in range(GQA)]
        tab4 = jnp.concatenate([jnp.broadcast_to(t, (tq, HEAD_DIM)) for t in tabs], axis=0)
        d0 = jnp.take_along_axis(tab4, jnp.clip(dij, 0, 127), axis=1)
        d1 = jnp.take_along_axis(tab4, jnp.clip(dij + tq, 0, 127), axis=1)
        far = tab4[:, HEAD_DIM - 1:HEAD_DIM]

        def k_aug(kt):
            return jnp.concatenate([ktile(ks_ref, kt, h), e2_ref[pl.ds(pl.multiple_of(kt * tq, tq), tq), :]], axis=1)

        reset()
        _flash_step(q_aug, k_aug(c), ktile(vs_ref, c, h), d0, causal, m_ref, l_ref, acc_ref, scale)

        @pl.when(c >= 1)
        def _():
            _flash_step(q_aug, k_aug(c - 1), ktile(vs_ref, c - 1, h), d1, None, m_ref, l_ref, acc_ref, scale)

        def body(i, carry):
            kt = c - 2 - i
            _flash_step(q_aug, k_aug(kt), ktile(vs_ref, kt, h), far, None, m_ref, l_ref, acc_ref, scale)
            return carry

        lax.fori_loop(0, jnp.maximum(c - 1, 0), body, 0)
        o_sel = acc_ref[...] / l_ref[...]

        reset()
        _flash_step(q4, ktile(kw_ref, c, h), ktile(vw_ref, c, h), d0, causal, m_ref, l_ref, acc_ref, scale)

        @pl.when(c >= 1)
        def _():
            _flash_step(q4, ktile(kw_ref, c - 1, h), ktile(vw_ref, c - 1, h), d1, None, m_ref, l_ref, acc_ref, scale)

        n_full = WINDOW // tq
        for back in range(2, n_full):
            @pl.when(c >= back)
            def _(back=back):
                _flash_step(q4, ktile(kw_ref, c - back, h), ktile(vw_ref, c - back, h), far, None,
                            m_ref, l_ref, acc_ref, scale)

        @pl.when(c >= n_full)
        def _():
            _flash_step(q4, ktile(kw_ref, c - n_full, h), ktile(vw_ref, c - n_full, h), far, dij < 0,
                        m_ref, l_ref, acc_ref, scale)

        o_win = acc_ref[...] / l_ref[...]

        for g in range(GQA):
            hd = h * GQA + g
            sl = slice(hd * HEAD_DIM, (hd + 1) * HEAD_DIM)
            rs = slice(g * tq, (g + 1) * tq)
            o = (sig[:, 3 * hd:3 * hd + 1] * ocmp_ref[:, sl]
                 + sig[:, 3 * hd + 1:3 * hd + 2] * o_sel[rs, :]
                 + sig[:, 3 * hd + 2:3 * hd + 3] * o_win[rs, :])
            y = o * lax.rsqrt(jnp.mean(o * o, axis=-1, keepdims=True) + EPS)
            o_ref[:, sl] = (y * gnw_ref[:, sl]).astype(o_ref.dtype)


def sel_win(qkv, selb, e2, tab, ocmp, gc, gnw, B, S, gate_blk):
    T = qkv.shape[0]
    tq = Q_BLOCK
    n_qb = S // tq
    dq = N_Q_HEADS * HEAD_DIM
    dkv = N_KV_HEADS * HEAD_DIM
    cb = dq // dkv
    kern = functools.partial(_sel_win_kernel, n_qb=n_qb, scale=HEAD_DIM ** -0.5)
    kv_spec = lambda col: pl.BlockSpec((S, dkv), lambda i, col=col: (i // n_qb, col))
    return pl.pallas_call(
        kern,
        grid=(T // tq,),
        in_specs=[pl.BlockSpec((tq, dq), lambda i: (i, 0)),
                  pl.BlockSpec((tq, dkv), lambda i: (i, 0)),
                  kv_spec(cb + 2), kv_spec(cb + 3), kv_spec(cb + 4), kv_spec(cb + 5),
                  pl.BlockSpec((S, HEAD_DIM), lambda i: (0, 0)),
                  pl.BlockSpec(tab.shape, lambda i: (0, 0)),
                  pl.BlockSpec((tq, dq), lambda i: (i, 0)),
                  pl.BlockSpec((tq, HEAD_DIM), lambda i: (i, gate_blk)),
                  pl.BlockSpec((1, dq), lambda i: (0, 0))],
        out_specs=pl.BlockSpec((tq, dq), lambda i: (i, 0)),
        out_shape=jax.ShapeDtypeStruct((T, dq), BF16),
        scratch_shapes=[pltpu.VMEM((GQA * tq, 1), F32), pltpu.VMEM((GQA * tq, 1), F32),
                        pltpu.VMEM((GQA * tq, HEAD_DIM), F32)],
        compiler_params=_params(("parallel",)),
    )(qkv, selb, qkv, qkv, qkv, qkv, e2, tab, ocmp, gc, gnw)


def _conv_kernel(b_ref, c_ref, h_ref, cp_ref, hp_ref, cw_ref, cb_ref, gnw_ref, o_ref, u_ref, *, tiles_per_seq):
    tm = b_ref.shape[0]
    first = (pl.program_id(0) % tiles_per_seq) == 0
    u_prev = cp_ref[...] * hp_ref[...]
    u_ref[0:8, :] = jnp.where(first, 0.0, u_prev)
    u = c_ref[...] * h_ref[...]
    u_ref[8:8 + tm, :] = u
    y = (cw_ref[0:1, :] * u_ref[6:6 + tm, :] + cw_ref[1:2, :] * u_ref[7:7 + tm, :]
         + cw_ref[2:3, :] * u + cb_ref[...])
    o = b_ref[...] * y
    n_groups = o.shape[1] // HEAD_DIM
    for g in range(n_groups):
        sl = slice(g * HEAD_DIM, (g + 1) * HEAD_DIM)
        og = o[:, sl]
        yg = og * lax.rsqrt(jnp.mean(og * og, axis=-1, keepdims=True) + EPS)
        o_ref[:, sl] = (yg * gnw_ref[:, sl]).astype(o_ref.dtype)


def conv_mixer(gc, conv_w, conv_b, gnw, S, tm):
    T = gc.shape[0]
    dc = conv_w.shape[1]
    tps = S // tm
    kern = functools.partial(_conv_kernel, tiles_per_seq=tps)
    prev = lambda col: pl.BlockSpec((8, dc), lambda i, col=col: (jnp.maximum(i * (tm // 8) - 1, 0), col))
    cur = lambda col: pl.BlockSpec((tm, dc), lambda i, col=col: (i, col))
    return pl.pallas_call(
        kern,
        grid=(T // tm,),
        in_specs=[cur(0), cur(1), cur(2), prev(1), prev(2),
                  pl.BlockSpec((8, dc), lambda i: (0, 0)),
                  pl.BlockSpec((1, dc), lambda i: (0, 0)),
                  pl.BlockSpec((1, dc), lambda i: (0, 0))],
        out_specs=pl.BlockSpec((tm, dc), lambda i: (i, 0)),
        out_shape=jax.ShapeDtypeStruct((T, dc), BF16),
        scratch_shapes=[pltpu.VMEM((tm + 8, dc), F32)],
        compiler_params=_params(("parallel",)),
    )(gc, gc, gc, gc, gc, conv_w, conv_b, gnw)


def _out_proj_kernel(ma_ref, mc_ref, wa_ref, wc_ref, x_ref, o_ref):
    o_ref[...] = x_ref[...] + _dot(ma_ref[...], wa_ref[...]) + _dot(mc_ref[...], wc_ref[...])


def out_proj(ma, mc, wa, wc, x, tm, tn):
    T, da = ma.shape
    dc = mc.shape[1]
    D = x.shape[1]
    return pl.pallas_call(
        _out_proj_kernel,
        grid=(T // tm, D // tn),
        in_specs=[pl.BlockSpec((tm, da), lambda i, j: (i, 0)),
                  pl.BlockSpec((tm, dc), lambda i, j: (i, 0)),
                  pl.BlockSpec((da, tn), lambda i, j: (0, j)),
                  pl.BlockSpec((dc, tn), lambda i, j: (0, j)),
                  pl.BlockSpec((tm, tn), lambda i, j: (i, j))],
        out_specs=pl.BlockSpec((tm, tn), lambda i, j: (i, j)),
        out_shape=jax.ShapeDtypeStruct((T, D), F32),
        compiler_params=_params(("parallel", "arbitrary")),
    )(ma, mc, wa, wc, x)


def _peer_route_kernel(q_ref, sk_ref, e_ref, g_ref, sv_ref, si_ref):
    tt = q_ref.shape[0]
    K = PEER_TOPK
    n_io = lax.broadcasted_iota(jnp.int32, (N_KEYS, tt), 0).astype(F32)
    r16 = lax.broadcasted_iota(jnp.int32, (16, tt), 0).astype(F32)
    r8 = lax.broadcasted_iota(jnp.int32, (8, tt), 0).astype(F32)
    ninf = -jnp.inf

    for h in range(PEER_HEADS):
        for c in range(2):
            col = (h * 2 + c) * HEAD_DIM
            s = _dot_nt(sk_ref[h * 2 + c], q_ref[:, col:col + HEAD_DIM])
            for k in range(K):
                m = jnp.max(s, axis=0, keepdims=True)
                idx = jnp.min(jnp.where(s == m, n_io, float(N_KEYS)), axis=0, keepdims=True)
                s = jnp.where(n_io == idx, ninf, s)
                sv_ref[c, k:k + 1, :] = m
                si_ref[c, k:k + 1, :] = idx
        sv0, sv1 = sv_ref[0], sv_ref[1]
        si0, si1 = si_ref[0], si_ref[1]
        cands, flats, exps = [], [], []
        for a, nb, rows in ((0, 16, 16), (1, 8, 8), (2, 5, 8), (3, 4, 8)):
            r_io = r16 if rows == 16 else r8
            val = sv0[a:a + 1, :] + sv1[0:rows, :]
            cands.append(jnp.where(r_io < nb, val, ninf))
            flats.append(a * 16.0 + r_io)
            exps.append(si0[a:a + 1, :] * float(N_KEYS) + si1[0:rows, :])
        for b, lo, hi, rows in ((0, 4, 16, 16), (1, 4, 8, 8), (2, 4, 5, 8)):
            r_io = r16 if rows == 16 else r8
            val = sv0[0:rows, :] + sv1[b:b + 1, :]
            cands.append(jnp.where((r_io >= lo) & (r_io < hi), val, ninf))
            flats.append(r_io * 16.0 + b)
            exps.append(si0[0:rows, :] * float(N_KEYS) + si1[b:b + 1, :])
        cand = jnp.concatenate(cands, axis=0)
        flat = jnp.concatenate(flats, axis=0)
        expt = jnp.concatenate(exps, axis=0)
        cvs, exs = [], []
        for k in range(K):
            m = jnp.max(cand, axis=0, keepdims=True)
            fsel = jnp.min(jnp.where(cand == m, flat, 1e9), axis=0, keepdims=True)
            hit = flat == fsel
            exs.append(jnp.max(jnp.where(hit, expt, -1.0), axis=0, keepdims=True))
            cand = jnp.where(hit, ninf, cand)
            cvs.append(m)
        cv = jnp.concatenate(cvs, axis=0)
        ex = jnp.concatenate(exs, axis=0)
        ev = jnp.exp(cv - jnp.max(cv, axis=0, keepdims=True))
        gates = ev / jnp.sum(ev, axis=0, keepdims=True)
        e_ref[0, h * K:(h + 1) * K, :] = ex.astype(jnp.int32)
        g_ref[0, h * K:(h + 1) * K, :] = gates


def peer_route(qp, subkeys):
    T = qp.shape[0]
    tt = 128
    P = PEER_HEADS * PEER_TOPK
    nt = T // tt
    return pl.pallas_call(
        _peer_route_kernel,
        grid=(nt,),
        in_specs=[pl.BlockSpec((tt, qp.shape[1]), lambda i: (i, 0)),
                  pl.BlockSpec(subkeys.shape, lambda i: (0, 0, 0))],
        out_specs=[pl.BlockSpec((1, P, tt), lambda i: (i, 0, 0))] * 2,
        out_shape=[jax.ShapeDtypeStruct((nt, P, tt), jnp.int32),
                   jax.ShapeDtypeStruct((nt, P, tt), F32)],
        scratch_shapes=[pltpu.VMEM((2, PEER_TOPK, tt), F32), pltpu.VMEM((2, PEER_TOPK, tt), F32)],
        compiler_params=_params(("parallel",)),
    )(qp, subkeys)


def _peer_gbuild_kernel(e_ref, g_ref, o_ref, i1_ref, i2_ref, gt_ref):
    tt = e_ref.shape[2]
    e = e_ref[0].T
    i1_ref[...] = e // N_KEYS
    i2_ref[...] = e % N_KEYS
    gt_ref[...] = g_ref[0].T
    P = e.shape[1]
    k_io = lax.broadcasted_iota(jnp.int32, (N_KEYS, P), 0)

    def body(t, carry):
        r1 = i1_ref[pl.ds(t, 1), :]
        r2 = i2_ref[pl.ds(t, 1), :]
        rg = gt_ref[pl.ds(t, 1), :]
        lhs = jnp.where(k_io == r1, rg, 0.0).astype(BF16)
        rhs = jnp.where(k_io == r2, 1.0, 0.0).astype(BF16)
        o_ref[t] = _dot_nt(lhs, rhs).astype(o_ref.dtype)
        return carry

    lax.fori_loop(0, tt, body, 0)


def peer_gbuild(ex, gates):
    nt, P, tt = ex.shape
    T = nt * tt
    return pl.pallas_call(
        _peer_gbuild_kernel,
        grid=(nt,),
        in_specs=[pl.BlockSpec((1, P, tt), lambda i: (i, 0, 0))] * 2,
        out_specs=pl.BlockSpec((tt, N_KEYS, N_KEYS), lambda i: (i, 0, 0)),
        out_shape=jax.ShapeDtypeStruct((T, N_KEYS, N_KEYS), BF16),
        scratch_shapes=[pltpu.VMEM((tt, P), jnp.int32), pltpu.VMEM((tt, P), jnp.int32),
                        pltpu.VMEM((tt, P), F32)],
        compiler_params=_params(("parallel",)),
    )(ex, gates)


def _peer_dense_kernel(h_ref, nw_ref, fw_ref, u_ref, v_ref, gm_ref, o_ref, hn_ref, acc_ref):
    j = pl.program_id(1)

    @pl.when(j == 0)
    def _():
        x = h_ref[...]
        y = x * lax.rsqrt(jnp.mean(x * x, axis=-1, keepdims=True) + EPS)
        hn_ref[...] = (y * nw_ref[...]).astype(BF16)
        acc_ref[...] = jnp.zeros(acc_ref.shape, F32)

    a = _dot_nt(hn_ref[...], u_ref[...])
    w = (gm_ref[...].astype(F32) * jax.nn.gelu(a)).astype(BF16)
    acc_ref[...] += _dot(w, v_ref[...])

    @pl.when(j == pl.num_programs(1) - 1)
    def _():
        x = h_ref[...] + acc_ref[...]
        y = x * lax.rsqrt(jnp.mean(x * x, axis=-1, keepdims=True) + EPS)
        o_ref[...] = y * fw_ref[...]


def peer_dense(h, ffn_nw, final_nw, u, v, gm, tt, ec):
    T, D = h.shape
    E = u.shape[0]
    return pl.pallas_call(
        _peer_dense_kernel,
        grid=(T // tt, E // ec),
        in_specs=[pl.BlockSpec((tt, D), lambda i, j: (i, 0)),
                  pl.BlockSpec((1, D), lambda i, j: (0, 0)),
                  pl.BlockSpec((1, D), lambda i, j: (0, 0)),
                  pl.BlockSpec((ec, D), lambda i, j: (j, 0)),
                  pl.BlockSpec((ec, D), lambda i, j: (j, 0)),
                  pl.BlockSpec((tt, ec), lambda i, j: (i, j))],
        out_specs=pl.BlockSpec((tt, D), lambda i, j: (i, 0)),
        out_shape=jax.ShapeDtypeStruct((T, D), F32),
        scratch_shapes=[pltpu.VMEM((tt, D), BF16), pltpu.VMEM((tt, D), F32)],
        compiler_params=_params(("parallel", "arbitrary")),
    )(h, ffn_nw.reshape(1, D), final_nw.reshape(1, D), u, v, gm)


def _t5_bucket_np(n_dist):
    d = np.arange(n_dist)
    max_exact = NUM_BUCKETS // 2
    nf = np.maximum(d, 1).astype(np.float64)
    large = max_exact + (np.log(nf / max_exact) / math.log(MAX_DISTANCE / max_exact)
                         * (NUM_BUCKETS - max_exact)).astype(np.int64)
    large = np.minimum(large, NUM_BUCKETS - 1)
    return np.where(d < max_exact, d, large).astype(np.int32)


def _overlap_t_np(S):
    n_c = (S - L_CMP) // STRIDE_CMP + 1
    n_sel = S // L_SEL
    pos = np.arange(n_c)[:, None] * STRIDE_CMP + np.arange(L_CMP)[None, :]
    m = np.zeros((n_c + 1, n_sel), np.float32)
    np.add.at(m, (np.repeat(np.arange(n_c), L_CMP), (pos // L_SEL).reshape(-1)), 1.0 / L_CMP)
    return np.ascontiguousarray(m.T)


def _block_onehot_np(S):
    e2 = np.zeros((S, HEAD_DIM), np.float32)
    e2[np.arange(S), np.arange(S) // L_SEL] = 1.0
    return e2


def nsa_conv_mix(xt, B, S, attn_norm_w, w_in, w_cmp_k, w_cmp_v, cmp_pos, conv_w, conv_b,
                 attn_gnw, conv_gnw, rel_bias, tm=512):
    T, D = xt.shape
    dq = N_Q_HEADS * HEAD_DIM
    dkv = N_KV_HEADS * HEAD_DIM
    n_attn = dq + 6 * dkv
    n_gate = 3 * N_Q_HEADS
    dc = (w_in.shape[1] - n_attn - n_gate) // 3
    w_attn = w_in[:, :n_attn].astype(BF16)
    w_gate = jnp.pad(w_in[:, n_attn:n_attn + n_gate], ((0, 0), (0, HEAD_DIM - n_gate)))
    w_gc = jnp.concatenate([w_in[:, n_attn + n_gate:], w_gate], axis=1).astype(BF16)

    qkv = norm_matmul(xt, attn_norm_w, w_attn, BF16, tm, 640 if n_attn % 640 == 0 else 128)
    gc = norm_matmul(xt, attn_norm_w, w_gc, F32, tm, 640 if w_gc.shape[1] % 640 == 0 else 128)

    n16 = S // STRIDE_CMP
    k16 = qkv[:, dq:dq + dkv].reshape(B * n16, STRIDE_CMP * dkv)
    v16 = qkv[:, dq + dkv:dq + 2 * dkv].reshape(B * n16, STRIDE_CMP * dkv)

    def wbig(w, lo):
        wl = w[lo:lo + STRIDE_CMP]
        eye = jnp.eye(N_KV_HEADS, dtype=w.dtype)
        return jnp.einsum('lde,hg->lhdge', wl, eye).reshape(STRIDE_CMP * dkv, dkv).astype(BF16)

    def posrow(lo):
        p = cmp_pos[lo:lo + STRIDE_CMP]
        return jnp.broadcast_to(p[:, None, :], (STRIDE_CMP, N_KV_HEADS, HEAD_DIM)).reshape(1, STRIDE_CMP * dkv)

    kc, vc = compress(k16, v16, posrow(0), posrow(STRIDE_CMP),
                      wbig(w_cmp_k, 0), wbig(w_cmp_k, STRIDE_CMP),
                      wbig(w_cmp_v, 0), wbig(w_cmp_v, STRIDE_CMP), B)

    tab = rel_bias[_t5_bucket_np(HEAD_DIM)].T
    ovt = jnp.asarray(_overlap_t_np(S), BF16)
    e2 = jnp.asarray(_block_onehot_np(S), BF16)

    ocmp, selb = cmp_select(qkv, kc, vc, tab, ovt, B, S)
    attn = sel_win(qkv, selb, e2, tab, ocmp, gc, attn_gnw.reshape(1, dq), B, S,
                   gate_blk=(3 * dc) // HEAD_DIM)
    cw8 = jnp.pad(conv_w, ((0, 8 - conv_w.shape[0]), (0, 0)))
    conv = conv_mixer(gc, cw8, conv_b.reshape(1, dc), conv_gnw.reshape(1, dc), S, tm)
    return attn, conv


def peer_block(h, ffn_nw, final_nw, peer_wq, peer_subkeys, peer_u, peer_v, tm=512, tt=512, ec=512):
    T, D = h.shape
    qp = norm_matmul(h, ffn_nw, peer_wq.astype(BF16), BF16, tm, 512)
    sk = peer_subkeys.reshape(PEER_HEADS * 2, N_KEYS, peer_subkeys.shape[-1]).astype(BF16)
    ex, gates = peer_route(qp, sk)
    gm = peer_gbuild(ex, gates).reshape(T, N_KEYS * N_KEYS)
    return peer_dense(h, ffn_nw, final_nw, peer_u.astype(BF16), peer_v.astype(BF16), gm, tt, ec)


def kernel(x, attn_norm_w, w_in, w_cmp_k, w_cmp_v, cmp_pos, conv_w, conv_b, attn_group_norm_w,
           conv_group_norm_w, w_out, rel_bias, ffn_norm_w, peer_wq, peer_subkeys, peer_u, peer_v,
           final_norm_w):
    B, S, D = x.shape
    T = B * S
    xt = x.reshape(T, D)
    attn, conv = nsa_conv_mix(xt, B, S, attn_norm_w[0], w_in[0], w_cmp_k[0], w_cmp_v[0], cmp_pos[0],
                              conv_w[0], conv_b[0], attn_group_norm_w[0], conv_group_norm_w[0], rel_bias)
    da = attn.shape[1]
    wo = w_out[0].astype(BF16)
    h = out_proj(attn, conv, wo[:da], wo[da:], xt, 512, 1024)
    out = peer_block(h, ffn_norm_w[0], final_norm_w, peer_wq[0], peer_subkeys[0], peer_u[0], peer_v[0])
    return out.reshape(B, S, D)
```

```python
import functools
import math

import jax
import jax.numpy as jnp
import numpy as np
from jax import lax
from jax.experimental import pallas as pl
from jax.experimental.pallas import tpu as pltpu

F32 = jnp.float32
BF16 = jnp.bfloat16

HEAD_DIM = 128
N_KV_HEADS = 2
GQA = 4
N_Q_HEADS = N_KV_HEADS * GQA
L_CMP = 32
STRIDE_CMP = 16
L_SEL = 64
N_SEL = 16
WINDOW = 512
Q_BLOCK = 128
FORCED_SCORE = float(GQA + 1)
NUM_BUCKETS = 32
MAX_DISTANCE = 128
PEER_HEADS = 8
N_KEYS = 128
PEER_TOPK = 16
EPS = 1e-6
NEG_BIG = -1e30
SEL_OFF = -float(2 ** 30)
VMEM_LIMIT = 56 * 1024 * 1024


def _dot(a, b):
    return jnp.dot(a, b, preferred_element_type=F32)


def _dot_nt(a, b):
    return lax.dot_general(a, b, (((1,), (1,)), ((), ())), preferred_element_type=F32)


def _params(sem, vmem=VMEM_LIMIT):
    return pltpu.CompilerParams(dimension_semantics=sem, vmem_limit_bytes=vmem)


def _norm_matmul_kernel(x_ref, nw_ref, w_ref, o_ref, xn_ref):
    @pl.when(pl.program_id(1) == 0)
    def _():
        x = x_ref[...]
        y = x * lax.rsqrt(jnp.mean(x * x, axis=-1, keepdims=True) + EPS)
        xn_ref[...] = (y * nw_ref[...]).astype(BF16)

    o_ref[...] = _dot(xn_ref[...], w_ref[...]).astype(o_ref.dtype)


def norm_matmul(x, norm_w, w, out_dtype, tm, tn):
    T, D = x.shape
    N = w.shape[1]
    return pl.pallas_call(
        _norm_matmul_kernel,
        grid=(T // tm, N // tn),
        in_specs=[pl.BlockSpec((tm, D), lambda i, j: (i, 0)),
                  pl.BlockSpec((1, D), lambda i, j: (0, 0)),
                  pl.BlockSpec((D, tn), lambda i, j: (0, j))],
        out_specs=pl.BlockSpec((tm, tn), lambda i, j: (i, j)),
        out_shape=jax.ShapeDtypeStruct((T, N), out_dtype),
        scratch_shapes=[pltpu.VMEM((tm, D), BF16)],
        compiler_params=_params(("parallel", "arbitrary")),
    )(x, norm_w.reshape(1, D), w)


def _compress_kernel(k_ref, v_ref, pa_ref, pb_ref, wka_ref, wkb_ref, wva_ref, wvb_ref, kc_ref, vc_ref):
    def one(x_ref, wa_ref, wb_ref, o_ref):
        x = x_ref[...].astype(F32)
        a = _dot((x + pa_ref[...]).astype(BF16), wa_ref[...])
        b = _dot((x + pb_ref[...]).astype(BF16), wb_ref[...])
        n = b.shape[0]
        o_ref[...] = (a + pltpu.roll(b, n - 1, 0)).astype(o_ref.dtype)

    one(k_ref, wka_ref, wkb_ref, kc_ref)
    one(v_ref, wva_ref, wvb_ref, vc_ref)


def compress(k16, v16, pos_a, pos_b, wka, wkb, wva, wvb, B):
    R, C = k16.shape
    nb = R // B
    dkv = wka.shape[1]
    full = lambda shp: pl.BlockSpec(shp, lambda b: (0, 0))
    return pl.pallas_call(
        _compress_kernel,
        grid=(B,),
        in_specs=[pl.BlockSpec((nb, C), lambda b: (b, 0)),
                  pl.BlockSpec((nb, C), lambda b: (b, 0)),
                  full((1, C)), full((1, C)),
                  full((C, dkv)), full((C, dkv)), full((C, dkv)), full((C, dkv))],
        out_specs=[pl.BlockSpec((nb, dkv), lambda b: (b, 0))] * 2,
        out_shape=[jax.ShapeDtypeStruct((R, dkv), BF16)] * 2,
        compiler_params=_params(("parallel",)),
    )(k16, v16, pos_a, pos_b, wka, wkb, wva, wvb)


def _bias_from_dist(tab_row, dist):
    idx = jnp.clip(dist, 0, 127)
    w = tab_row.shape[1]
    tab = jnp.broadcast_to(tab_row, (idx.shape[0], w))
    parts = [jnp.take_along_axis(tab, idx[:, k:k + w], axis=1) for k in range(0, idx.shape[1], w)]
    return parts[0] if len(parts) == 1 else jnp.concatenate(parts, axis=1)


def _cmp_select_kernel(q_ref, kc_ref, vc_ref, tab_ref, ovt_ref, ocmp_ref, selb_ref, *, n_qb, scale):
    c = pl.program_id(0) % n_qb
    tq = q_ref.shape[0]
    n_c = kc_ref.shape[0]
    n_sel = ovt_ref.shape[0]
    t0 = c * tq
    t_col = t0 + lax.broadcasted_iota(jnp.int32, (tq, n_c), 0)
    n_row = lax.broadcasted_iota(jnp.int32, (tq, n_c), 1)
    dist = t_col - (n_row * STRIDE_CMP + (L_CMP - 1))
    valid = dist >= 0

    j_io = lax.broadcasted_iota(jnp.int32, (n_sel, tq), 0)
    t_io = t0 + lax.broadcasted_iota(jnp.int32, (n_sel, tq), 1)
    blk_t = t_io // L_SEL
    forced = (j_io == 0) | (j_io == blk_t) | (j_io == blk_t - 1)
    causal_blk = j_io * L_SEL <= t_io

    for h in range(N_KV_HEADS):
        kc = kc_ref[:, h * HEAD_DIM:(h + 1) * HEAD_DIM]
        vc = vc_ref[:, h * HEAD_DIM:(h + 1) * HEAD_DIM]
        psum = jnp.zeros((tq, n_c), F32)
        for g in range(GQA):
            hd = h * GQA + g
            qh = q_ref[:, hd * HEAD_DIM:(hd + 1) * HEAD_DIM]
            bias = _bias_from_dist(tab_ref[hd:hd + 1, :], dist)
            s = _dot_nt(qh, kc) * scale + bias
            s = jnp.where(valid, s, NEG_BIG)
            m = jnp.max(s, axis=-1, keepdims=True)
            e = jnp.where(valid, jnp.exp(s - m), 0.0)
            d = jnp.sum(e, axis=-1, keepdims=True)
            p = e / jnp.where(d > 0, d, 1.0)
            ocmp_ref[:, hd * HEAD_DIM:(hd + 1) * HEAD_DIM] = _dot(p.astype(BF16), vc)
            psum = psum + p
        p_hi = psum.astype(BF16)
        p_lo = (psum - p_hi.astype(F32)).astype(BF16)
        ovt = ovt_ref[...]
        imp = _dot_nt(ovt, p_hi) + _dot_nt(ovt, p_lo)
        score = jnp.where(forced, FORCED_SCORE, jnp.where(causal_blk, imp, -1.0))
        rank = jnp.zeros((n_sel, tq), F32)
        for jp in range(n_sel):
            row = score[jp:jp + 1, :]
            rank = rank + jnp.where(j_io > jp, jnp.where(row >= score, 1.0, 0.0),
                                    jnp.where(row > score, 1.0, 0.0))
        selb = jnp.where(rank < float(N_SEL), 0.0, SEL_OFF)
        if n_sel < HEAD_DIM:
            selb = jnp.concatenate([selb, jnp.zeros((HEAD_DIM - n_sel, tq), F32)], axis=0)
        selb_ref[:, h * HEAD_DIM:(h + 1) * HEAD_DIM] = selb.T.astype(BF16)


def cmp_select(qkv, kc, vc, tab, ovt, B, S):
    T = qkv.shape[0]
    tq = Q_BLOCK
    n_qb = S // tq
    n_c = kc.shape[0] // B
    dq = N_Q_HEADS * HEAD_DIM
    dkv = N_KV_HEADS * HEAD_DIM
    kern = functools.partial(_cmp_select_kernel, n_qb=n_qb, scale=HEAD_DIM ** -0.5)
    return pl.pallas_call(
        kern,
        grid=(T // tq,),
        in_specs=[pl.BlockSpec((tq, dq), lambda i: (i, 0)),
                  pl.BlockSpec((n_c, dkv), lambda i: (i // n_qb, 0)),
                  pl.BlockSpec((n_c, dkv), lambda i: (i // n_qb, 0)),
                  pl.BlockSpec(tab.shape, lambda i: (0, 0)),
                  pl.BlockSpec(ovt.shape, lambda i: (0, 0))],
        out_specs=[pl.BlockSpec((tq, dq), lambda i: (i, 0)),
                   pl.BlockSpec((tq, dkv), lambda i: (i, 0))],
        out_shape=[jax.ShapeDtypeStruct((T, dq), F32),
                   jax.ShapeDtypeStruct((T, dkv), BF16)],
        compiler_params=_params(("parallel",)),
    )(qkv, kc, vc, tab, ovt)


def _flash_step(k, vt, q, bias, mask, m_ref, l_ref, acc_ref, scale):
    s = _dot_nt(k, q) * scale + bias
    if mask is not None:
        s = jnp.where(mask, s, NEG_BIG)
    m_old = m_ref[...]
    m_new = jnp.maximum(m_old, jnp.max(s, axis=0, keepdims=True))
    alpha = jnp.exp(m_old - m_new)
    p = jnp.exp(s - m_new)
    l_ref[...] = alpha * l_ref[...] + jnp.sum(p, axis=0, keepdims=True)
    acc_ref[...] = alpha * acc_ref[...] + _dot(vt, p.astype(BF16))
    m_ref[...] = m_new


def _sel_win_kernel(q_ref, selb_ref, ks_ref, vst_ref, kw_ref, vwt_ref, e2_ref, tab_ref, ocmp_ref,
                    gl_ref, gnw_ref, o_ref, m_ref, l_ref, acc_ref, *, n_qb, scale):
    c = pl.program_id(0) % n_qb
    tq = q_ref.shape[0]
    j_io = lax.broadcasted_iota(jnp.int32, (tq, tq), 0)
    i_io = lax.broadcasted_iota(jnp.int32, (tq, tq), 1)
    dij = i_io - j_io
    dij4 = jnp.concatenate([dij] * GQA, axis=1)
    causal = dij4 >= 0
    sig = jax.nn.sigmoid(gl_ref[...])

    def rows_of(ref, kt, h):
        return ref[pl.ds(pl.multiple_of(kt * tq, tq), tq), h * HEAD_DIM:(h + 1) * HEAD_DIM]

    def cols_of(ref, kt, h):
        return ref[h * HEAD_DIM:(h + 1) * HEAD_DIM, pl.ds(pl.multiple_of(kt * tq, tq), tq)]

    def reset():
        m_ref[...] = jnp.full(m_ref.shape, NEG_BIG, F32)
        l_ref[...] = jnp.zeros(l_ref.shape, F32)
        acc_ref[...] = jnp.zeros(acc_ref.shape, F32)

    def result():
        ot = acc_ref[...] / l_ref[...]
        return [ot[:, g * tq:(g + 1) * tq].T for g in range(GQA)]

    for h in range(N_KV_HEADS):
        q4 = jnp.concatenate([q_ref[:, (h * GQA + g) * HEAD_DIM:(h * GQA + g + 1) * HEAD_DIM]
                              for g in range(GQA)], axis=0)
        sb = selb_ref[:, h * HEAD_DIM:(h + 1) * HEAD_DIM]
        q_aug = jnp.concatenate([q4, jnp.concatenate([sb] * GQA, axis=0)], axis=1)
        tabs = [jnp.broadcast_to(tab_ref[h * GQA + g:h * GQA + g + 1, :], (tq, HEAD_DIM)) for g in range(GQA)]
        d0 = jnp.concatenate([jnp.take_along_axis(t, jnp.clip(dij, 0, 127), axis=1) for t in tabs], axis=1)
        d1 = jnp.concatenate([jnp.take_along_axis(t, jnp.clip(dij + tq, 0, 127), axis=1) for t in tabs], axis=1)
        far = jnp.concatenate([t[0:1, HEAD_DIM - 1:HEAD_DIM] + jnp.zeros((1, tq), F32) for t in tabs], axis=1)

        def k_aug(kt):
            return jnp.concatenate([rows_of(ks_ref, kt, h), e2_ref[pl.ds(pl.multiple_of(kt * tq, tq), tq), :]], axis=1)

        reset()
        _flash_step(k_aug(c), cols_of(vst_ref, c, h), q_aug, d0, causal, m_ref, l_ref, acc_ref, scale)

        @pl.when(c >= 1)
        def _():
            _flash_step(k_aug(c - 1), cols_of(vst_ref, c - 1, h), q_aug, d1, None, m_ref, l_ref, acc_ref, scale)

        def body(i, carry):
            kt = c - 2 - i
            _flash_step(k_aug(kt), cols_of(vst_ref, kt, h), q_aug, far, None, m_ref, l_ref, acc_ref, scale)
            return carry

        lax.fori_loop(0, jnp.maximum(c - 1, 0), body, 0)
        o_sel = result()

        reset()
        _flash_step(rows_of(kw_ref, c, h), cols_of(vwt_ref, c, h), q4, d0, causal, m_ref, l_ref, acc_ref, scale)

        @pl.when(c >= 1)
        def _():
            _flash_step(rows_of(kw_ref, c - 1, h), cols_of(vwt_ref, c - 1, h), q4, d1, None,
                        m_ref, l_ref, acc_ref, scale)

        n_full = WINDOW // tq
        for back in range(2, n_full):
            @pl.when(c >= back)
            def _(back=back):
                _flash_step(rows_of(kw_ref, c - back, h), cols_of(vwt_ref, c - back, h), q4, far, None,
                            m_ref, l_ref, acc_ref, scale)

        @pl.when(c >= n_full)
        def _():
            _flash_step(rows_of(kw_ref, c - n_full, h), cols_of(vwt_ref, c - n_full, h), q4, far, dij4 < 0,
                        m_ref, l_ref, acc_ref, scale)

        o_win = result()

        for g in range(GQA):
            hd = h * GQA + g
            sl = slice(hd * HEAD_DIM, (hd + 1) * HEAD_DIM)
            o = (sig[:, 3 * hd:3 * hd + 1] * ocmp_ref[:, sl]
                 + sig[:, 3 * hd + 1:3 * hd + 2] * o_sel[g]
                 + sig[:, 3 * hd + 2:3 * hd + 3] * o_win[g])
            y = o * lax.rsqrt(jnp.mean(o * o, axis=-1, keepdims=True) + EPS)
            o_ref[:, sl] = (y * gnw_ref[:, sl]).astype(o_ref.dtype)


def sel_win(qkv, vst, vwt, selb, e2, tab, ocmp, gc, gnw, B, S, gate_blk):
    T = qkv.shape[0]
    tq = Q_BLOCK
    n_qb = S // tq
    dq = N_Q_HEADS * HEAD_DIM
    dkv = N_KV_HEADS * HEAD_DIM
    cb = dq // dkv
    kern = functools.partial(_sel_win_kernel, n_qb=n_qb, scale=HEAD_DIM ** -0.5)
    k_spec = lambda col: pl.BlockSpec((S, dkv), lambda i, col=col: (i // n_qb, col))
    vt_spec = pl.BlockSpec((dkv, S), lambda i: (i // n_qb, 0))
    return pl.pallas_call(
        kern,
        grid=(T // tq,),
        in_specs=[pl.BlockSpec((tq, dq), lambda i: (i, 0)),
                  pl.BlockSpec((tq, dkv), lambda i: (i, 0)),
                  k_spec(cb + 2), vt_spec, k_spec(cb + 4), vt_spec,
                  pl.BlockSpec((S, HEAD_DIM), lambda i: (0, 0)),
                  pl.BlockSpec(tab.shape, lambda i: (0, 0)),
                  pl.BlockSpec((tq, dq), lambda i: (i, 0)),
                  pl.BlockSpec((tq, HEAD_DIM), lambda i: (i, gate_blk)),
                  pl.BlockSpec((1, dq), lambda i: (0, 0))],
        out_specs=pl.BlockSpec((tq, dq), lambda i: (i, 0)),
        out_shape=jax.ShapeDtypeStruct((T, dq), BF16),
        scratch_shapes=[pltpu.VMEM((1, GQA * tq), F32), pltpu.VMEM((1, GQA * tq), F32),
                        pltpu.VMEM((HEAD_DIM, GQA * tq), F32)],
        compiler_params=_params(("parallel",)),
    )(qkv, selb, qkv, vst, qkv, vwt, e2, tab, ocmp, gc, gnw)


def _conv_kernel(b_ref, c_ref, h_ref, cp_ref, hp_ref, cw_ref, cb_ref, gnw_ref, o_ref, u_ref, *, tiles_per_seq):
    tm = b_ref.shape[0]
    first = (pl.program_id(0) % tiles_per_seq) == 0
    u_prev = cp_ref[...] * hp_ref[...]
    u_ref[0:8, :] = jnp.where(first, 0.0, u_prev)
    u = c_ref[...] * h_ref[...]
    u_ref[8:8 + tm, :] = u
    y = (cw_ref[0:1, :] * u_ref[6:6 + tm, :] + cw_ref[1:2, :] * u_ref[7:7 + tm, :]
         + cw_ref[2:3, :] * u + cb_ref[...])
    o = b_ref[...] * y
    n_groups = o.shape[1] // HEAD_DIM
    for g in range(n_groups):
        sl = slice(g * HEAD_DIM, (g + 1) * HEAD_DIM)
        og = o[:, sl]
        yg = og * lax.rsqrt(jnp.mean(og * og, axis=-1, keepdims=True) + EPS)
        o_ref[:, sl] = (yg * gnw_ref[:, sl]).astype(o_ref.dtype)


def conv_mixer(gc, conv_w, conv_b, gnw, S, tm):
    T = gc.shape[0]
    dc = conv_w.shape[1]
    tps = S // tm
    kern = functools.partial(_conv_kernel, tiles_per_seq=tps)
    prev = lambda col: pl.BlockSpec((8, dc), lambda i, col=col: (jnp.maximum(i * (tm // 8) - 1, 0), col))
    cur = lambda col: pl.BlockSpec((tm, dc), lambda i, col=col: (i, col))
    return pl.pallas_call(
        kern,
        grid=(T // tm,),
        in_specs=[cur(0), cur(1), cur(2), prev(1), prev(2),
                  pl.BlockSpec((8, dc), lambda i: (0, 0)),
                  pl.BlockSpec((1, dc), lambda i: (0, 0)),
                  pl.BlockSpec((1, dc), lambda i: (0, 0))],
        out_specs=pl.BlockSpec((tm, dc), lambda i: (i, 0)),
        out_shape=jax.ShapeDtypeStruct((T, dc), BF16),
        scratch_shapes=[pltpu.VMEM((tm + 8, dc), F32)],
        compiler_params=_params(("parallel",)),
    )(gc, gc, gc, gc, gc, conv_w, conv_b, gnw)


def _out_proj_kernel(ma_ref, mc_ref, wa_ref, wc_ref, x_ref, o_ref):
    o_ref[...] = x_ref[...] + _dot(ma_ref[...], wa_ref[...]) + _dot(mc_ref[...], wc_ref[...])


def out_proj(ma, mc, wa, wc, x, tm, tn):
    T, da = ma.shape
    dc = mc.shape[1]
    D = x.shape[1]
    return pl.pallas_call(
        _out_proj_kernel,
        grid=(T // tm, D // tn),
        in_specs=[pl.BlockSpec((tm, da), lambda i, j: (i, 0)),
                  pl.BlockSpec((tm, dc), lambda i, j: (i, 0)),
                  pl.BlockSpec((da, tn), lambda i, j: (0, j)),
                  pl.BlockSpec((dc, tn), lambda i, j: (0, j)),
                  pl.BlockSpec((tm, tn), lambda i, j: (i, j))],
        out_specs=pl.BlockSpec((tm, tn), lambda i, j: (i, j)),
        out_shape=jax.ShapeDtypeStruct((T, D), F32),
        compiler_params=_params(("parallel", "arbitrary")),
    )(ma, mc, wa, wc, x)


def _peer_route_kernel(q_ref, sk_ref, e_ref, g_ref, sv_ref, si_ref):
    tt = q_ref.shape[0]
    K = PEER_TOPK
    n_io = lax.broadcasted_iota(jnp.int32, (N_KEYS, tt), 0).astype(F32)
    r16 = lax.broadcasted_iota(jnp.int32, (16, tt), 0).astype(F32)
    r8 = lax.broadcasted_iota(jnp.int32, (8, tt), 0).astype(F32)
    ninf = -jnp.inf

    for h in range(PEER_HEADS):
        for c in range(2):
            col = (h * 2 + c) * HEAD_DIM
            s = _dot_nt(sk_ref[h * 2 + c], q_ref[:, col:col + HEAD_DIM])
            for k in range(K):
                m = jnp.max(s, axis=0, keepdims=True)
                idx = jnp.min(jnp.where(s == m, n_io, float(N_KEYS)), axis=0, keepdims=True)
                s = jnp.where(n_io == idx, ninf, s)
                sv_ref[c, k:k + 1, :] = m
                si_ref[c, k:k + 1, :] = idx
        sv0, sv1 = sv_ref[0], sv_ref[1]
        si0, si1 = si_ref[0], si_ref[1]
        cands, flats, exps = [], [], []
        for a, nb, rows in ((0, 16, 16), (1, 8, 8), (2, 5, 8), (3, 4, 8)):
            r_io = r16 if rows == 16 else r8
            val = sv0[a:a + 1, :] + sv1[0:rows, :]
            cands.append(jnp.where(r_io < nb, val, ninf))
            flats.append(a * 16.0 + r_io)
            exps.append(si0[a:a + 1, :] * float(N_KEYS) + si1[0:rows, :])
        for b, lo, hi, rows in ((0, 4, 16, 16), (1, 4, 8, 8), (2, 4, 5, 8)):
            r_io = r16 if rows == 16 else r8
            val = sv0[0:rows, :] + sv1[b:b + 1, :]
            cands.append(jnp.where((r_io >= lo) & (r_io < hi), val, ninf))
            flats.append(r_io * 16.0 + b)
            exps.append(si0[0:rows, :] * float(N_KEYS) + si1[b:b + 1, :])
        cand = jnp.concatenate(cands, axis=0)
        flat = jnp.concatenate(flats, axis=0)
        expt = jnp.concatenate(exps, axis=0)
        cvs, exs = [], []
        for k in range(K):
            m = jnp.max(cand, axis=0, keepdims=True)
            fsel = jnp.min(jnp.where(cand == m, flat, 1e9), axis=0, keepdims=True)
            hit = flat == fsel
            exs.append(jnp.max(jnp.where(hit, expt, -1.0), axis=0, keepdims=True))
            cand = jnp.where(hit, ninf, cand)
            cvs.append(m)
        cv = jnp.concatenate(cvs, axis=0)
        ex = jnp.concatenate(exs, axis=0)
        ev = jnp.exp(cv - jnp.max(cv, axis=0, keepdims=True))
        gates = ev / jnp.sum(ev, axis=0, keepdims=True)
        e_ref[0, h * K:(h + 1) * K, :] = ex.astype(jnp.int32)
        g_ref[0, h * K:(h + 1) * K, :] = gates


def peer_route(qp, subkeys):
    T = qp.shape[0]
    tt = 128
    P = PEER_HEADS * PEER_TOPK
    nt = T // tt
    return pl.pallas_call(
        _peer_route_kernel,
        grid=(nt,),
        in_specs=[pl.BlockSpec((tt, qp.shape[1]), lambda i: (i, 0)),
                  pl.BlockSpec(subkeys.shape, lambda i: (0, 0, 0))],
        out_specs=[pl.BlockSpec((1, P, tt), lambda i: (i, 0, 0))] * 2,
        out_shape=[jax.ShapeDtypeStruct((nt, P, tt), jnp.int32),
                   jax.ShapeDtypeStruct((nt, P, tt), F32)],
        scratch_shapes=[pltpu.VMEM((2, PEER_TOPK, tt), F32), pltpu.VMEM((2, PEER_TOPK, tt), F32)],
        compiler_params=_params(("parallel",)),
    )(qp, subkeys)


def _peer_gbuild_kernel(e_ref, g_ref, o_ref, i1_ref, i2_ref, gt_ref, s_ref):
    tt = e_ref.shape[2]
    e = e_ref[0].T
    i1_ref[...] = e >> 7
    i2_ref[...] = e & (N_KEYS - 1)
    gt_ref[...] = g_ref[0].T
    P = e.shape[1]
    k_io = lax.broadcasted_iota(jnp.int32, (N_KEYS, P), 0)
    group = 8

    def body(tg, carry):
        base = pl.multiple_of(tg * group, group)
        r1s = i1_ref[pl.ds(base, group), :]
        r2s = i2_ref[pl.ds(base, group), :]
        rgs = gt_ref[pl.ds(base, group), :]
        for u in range(group):
            lhs = jnp.where(k_io == r1s[u:u + 1, :], rgs[u:u + 1, :], 0.0).astype(BF16)
            rhs = jnp.where(k_io == r2s[u:u + 1, :], 1.0, 0.0).astype(BF16)
            row0 = pl.multiple_of((base + u) * N_KEYS, N_KEYS)
            s_ref[pl.ds(row0, N_KEYS), :] = _dot_nt(lhs, rhs)
        return carry

    lax.fori_loop(0, tt // group, body, 0)
    for i1 in range(N_KEYS):
        o_ref[i1] = s_ref[pl.ds(i1, tt, stride=N_KEYS), :].astype(o_ref.dtype)


def peer_gbuild(ex, gates):
    nt, P, tt = ex.shape
    T = nt * tt
    return pl.pallas_call(
        _peer_gbuild_kernel,
        grid=(nt,),
        in_specs=[pl.BlockSpec((1, P, tt), lambda i: (i, 0, 0))] * 2,
        out_specs=pl.BlockSpec((N_KEYS, tt, N_KEYS), lambda i: (0, i, 0)),
        out_shape=jax.ShapeDtypeStruct((N_KEYS, T, N_KEYS), BF16),
        scratch_shapes=[pltpu.VMEM((tt, P), jnp.int32), pltpu.VMEM((tt, P), jnp.int32),
                        pltpu.VMEM((tt, P), F32), pltpu.VMEM((tt * N_KEYS, N_KEYS), F32)],
        compiler_params=_params(("parallel",)),
    )(ex, gates)


def _peer_dense_kernel(h_ref, nw_ref, fw_ref, u_ref, v_ref, gm_ref, o_ref, hn_ref, acc_ref):
    j = pl.program_id(1)

    @pl.when(j == 0)
    def _():
        x = h_ref[...]
        y = x * lax.rsqrt(jnp.mean(x * x, axis=-1, keepdims=True) + EPS)
        hn_ref[...] = (y * nw_ref[...]).astype(BF16)
        acc_ref[...] = jnp.zeros(acc_ref.shape, F32)

    a = _dot_nt(hn_ref[...], u_ref[...])
    gm = jnp.concatenate([gm_ref[k] for k in range(gm_ref.shape[0])], axis=1)
    w = (gm.astype(F32) * jax.nn.gelu(a)).astype(BF16)
    acc_ref[...] += _dot(w, v_ref[...])

    @pl.when(j == pl.num_programs(1) - 1)
    def _():
        x = h_ref[...] + acc_ref[...]
        y = x * lax.rsqrt(jnp.mean(x * x, axis=-1, keepdims=True) + EPS)
        o_ref[...] = y * fw_ref[...]


def peer_dense(h, ffn_nw, final_nw, u, v, gm, tt, ec):
    T, D = h.shape
    E = u.shape[0]
    return pl.pallas_call(
        _peer_dense_kernel,
        grid=(T // tt, E // ec),
        in_specs=[pl.BlockSpec((tt, D), lambda i, j: (i, 0)),
                  pl.BlockSpec((1, D), lambda i, j: (0, 0)),
                  pl.BlockSpec((1, D), lambda i, j: (0, 0)),
                  pl.BlockSpec((ec, D), lambda i, j: (j, 0)),
                  pl.BlockSpec((ec, D), lambda i, j: (j, 0)),
                  pl.BlockSpec((ec // N_KEYS, tt, N_KEYS), lambda i, j: (j, i, 0))],
        out_specs=pl.BlockSpec((tt, D), lambda i, j: (i, 0)),
        out_shape=jax.ShapeDtypeStruct((T, D), F32),
        scratch_shapes=[pltpu.VMEM((tt, D), BF16), pltpu.VMEM((tt, D), F32)],
        compiler_params=_params(("parallel", "arbitrary")),
    )(h, ffn_nw.reshape(1, D), final_nw.reshape(1, D), u, v, gm)


def _t5_bucket_np(n_dist):
    d = np.arange(n_dist)
    max_exact = NUM_BUCKETS // 2
    nf = np.maximum(d, 1).astype(np.float64)
    large = max_exact + (np.log(nf / max_exact) / math.log(MAX_DISTANCE / max_exact)
                         * (NUM_BUCKETS - max_exact)).astype(np.int64)
    large = np.minimum(large, NUM_BUCKETS - 1)
    return np.where(d < max_exact, d, large).astype(np.int32)


def _overlap_t_np(S):
    n_c = (S - L_CMP) // STRIDE_CMP + 1
    n_sel = S // L_SEL
    pos = np.arange(n_c)[:, None] * STRIDE_CMP + np.arange(L_CMP)[None, :]
    m = np.zeros((n_c + 1, n_sel), np.float32)
    np.add.at(m, (np.repeat(np.arange(n_c), L_CMP), (pos // L_SEL).reshape(-1)), 1.0 / L_CMP)
    return np.ascontiguousarray(m.T)


def _block_onehot_np(S):
    e2 = np.zeros((S, HEAD_DIM), np.float32)
    e2[np.arange(S), np.arange(S) // L_SEL] = 1.0
    return e2


def nsa_conv_mix(xt, B, S, attn_norm_w, w_in, w_cmp_k, w_cmp_v, cmp_pos, conv_w, conv_b,
                 attn_gnw, conv_gnw, rel_bias, tm=512):
    T, D = xt.shape
    dq = N_Q_HEADS * HEAD_DIM
    dkv = N_KV_HEADS * HEAD_DIM
    n_attn = dq + 6 * dkv
    n_gate = 3 * N_Q_HEADS
    dc = (w_in.shape[1] - n_attn - n_gate) // 3
    w_attn = w_in[:, :n_attn].astype(BF16)
    w_gate = jnp.pad(w_in[:, n_attn:n_attn + n_gate], ((0, 0), (0, HEAD_DIM - n_gate)))
    w_gc = jnp.concatenate([w_in[:, n_attn + n_gate:], w_gate], axis=1).astype(BF16)

    qkv = norm_matmul(xt, attn_norm_w, w_attn, BF16, tm, 640 if n_attn % 640 == 0 else 128)
    gc = norm_matmul(xt, attn_norm_w, w_gc, F32, tm, 640 if w_gc.shape[1] % 640 == 0 else 128)

    n16 = S // STRIDE_CMP
    k16 = qkv[:, dq:dq + dkv].reshape(B * n16, STRIDE_CMP * dkv)
    v16 = qkv[:, dq + dkv:dq + 2 * dkv].reshape(B * n16, STRIDE_CMP * dkv)

    def wbig(w, lo):
        wl = w[lo:lo + STRIDE_CMP]
        eye = jnp.eye(N_KV_HEADS, dtype=w.dtype)
        return jnp.einsum('lde,hg->lhdge', wl, eye).reshape(STRIDE_CMP * dkv, dkv).astype(BF16)

    def posrow(lo):
        p = cmp_pos[lo:lo + STRIDE_CMP]
        return jnp.broadcast_to(p[:, None, :], (STRIDE_CMP, N_KV_HEADS, HEAD_DIM)).reshape(1, STRIDE_CMP * dkv)

    kc, vc = compress(k16, v16, posrow(0), posrow(STRIDE_CMP),
                      wbig(w_cmp_k, 0), wbig(w_cmp_k, STRIDE_CMP),
                      wbig(w_cmp_v, 0), wbig(w_cmp_v, STRIDE_CMP), B)

    tab = rel_bias[_t5_bucket_np(HEAD_DIM)].T
    ovt = jnp.asarray(_overlap_t_np(S), BF16)
    e2 = jnp.asarray(_block_onehot_np(S), BF16)

    ocmp, selb = cmp_select(qkv, kc, vc, tab, ovt, B, S)
    def transposed(col):
        return qkv[:, col:col + dkv].reshape(B, S, dkv).transpose(0, 2, 1).reshape(B * dkv, S)

    attn = sel_win(qkv, transposed(dq + 3 * dkv), transposed(dq + 5 * dkv), selb, e2, tab, ocmp, gc,
                   attn_gnw.reshape(1, dq), B, S, gate_blk=(3 * dc) // HEAD_DIM)
    cw8 = jnp.pad(conv_w, ((0, 8 - conv_w.shape[0]), (0, 0)))
    conv = conv_mixer(gc, cw8, conv_b.reshape(1, dc), conv_gnw.reshape(1, dc), S, tm)
    return attn, conv


def peer_block(h, ffn_nw, final_nw, peer_wq, peer_subkeys, peer_u, peer_v, tm=512, tt=512, ec=512):
    T, D = h.shape
    qp = norm_matmul(h, ffn_nw, peer_wq.astype(BF16), BF16, tm, 512)
    sk = peer_subkeys.reshape(PEER_HEADS * 2, N_KEYS, peer_subkeys.shape[-1]).astype(BF16)
    ex, gates = peer_route(qp, sk)
    gm = peer_gbuild(ex, gates)
    return peer_dense(h, ffn_nw, final_nw, peer_u.astype(BF16), peer_v.astype(BF16), gm, tt, ec)


def kernel(x, attn_norm_w, w_in, w_cmp_k, w_cmp_v, cmp_pos, conv_w, conv_b, attn_group_norm_w,
           conv_group_norm_w, w_out, rel_bias, ffn_norm_w, peer_wq, peer_subkeys, peer_u, peer_v,
           final_norm_w):
    B, S, D = x.shape
    T = B * S
    xt = x.reshape(T, D)
    attn, conv = nsa_conv_mix(xt, B, S, attn_norm_w[0], w_in[0], w_cmp_k[0], w_cmp_v[0], cmp_pos[0],
                              conv_w[0], conv_b[0], attn_group_norm_w[0], conv_group_norm_w[0], rel_bias)
    da = attn.shape[1]
    wo = w_out[0].astype(BF16)
    h = out_proj(attn, conv, wo[:da], wo[da:], xt, 512, 1024)
    out = peer_block(h, ffn_norm_w[0], final_norm_w, peer_wq[0], peer_subkeys[0], peer_u[0], peer_v[0])
    return out.reshape(B, S, D)
```

```python
import functools
import math

import jax
import jax.numpy as jnp
import numpy as np
from jax import lax
from jax.experimental import pallas as pl
from jax.experimental.pallas import tpu as pltpu

F32 = jnp.float32
BF16 = jnp.bfloat16

HEAD_DIM = 128
N_KV_HEADS = 2
GQA = 4
N_Q_HEADS = N_KV_HEADS * GQA
L_CMP = 32
STRIDE_CMP = 16
L_SEL = 64
N_SEL = 16
WINDOW = 512
Q_BLOCK = 128
FORCED_SCORE = float(GQA + 1)
NUM_BUCKETS = 32
MAX_DISTANCE = 128
PEER_HEADS = 8
N_KEYS = 128
PEER_TOPK = 16
EPS = 1e-6
NEG_BIG = -1e30
SEL_OFF = -float(2 ** 30)
VMEM_LIMIT = 56 * 1024 * 1024


def _dot(a, b):
    return jnp.dot(a, b, preferred_element_type=F32)


def _dot_nt(a, b):
    return lax.dot_general(a, b, (((1,), (1,)), ((), ())), preferred_element_type=F32)


def _params(sem, vmem=VMEM_LIMIT):
    return pltpu.CompilerParams(dimension_semantics=sem, vmem_limit_bytes=vmem)


def _norm_matmul_kernel(x_ref, nw_ref, w_ref, o_ref, xn_ref):
    @pl.when(pl.program_id(1) == 0)
    def _():
        x = x_ref[...]
        y = x * lax.rsqrt(jnp.mean(x * x, axis=-1, keepdims=True) + EPS)
        xn_ref[...] = (y * nw_ref[...]).astype(BF16)

    o_ref[...] = _dot(xn_ref[...], w_ref[...]).astype(o_ref.dtype)


def norm_matmul(x, norm_w, w, out_dtype, tm, tn):
    T, D = x.shape
    N = w.shape[1]
    return pl.pallas_call(
        _norm_matmul_kernel,
        grid=(T // tm, N // tn),
        in_specs=[pl.BlockSpec((tm, D), lambda i, j: (i, 0)),
                  pl.BlockSpec((1, D), lambda i, j: (0, 0)),
                  pl.BlockSpec((D, tn), lambda i, j: (0, j))],
        out_specs=pl.BlockSpec((tm, tn), lambda i, j: (i, j)),
        out_shape=jax.ShapeDtypeStruct((T, N), out_dtype),
        scratch_shapes=[pltpu.VMEM((tm, D), BF16)],
        compiler_params=_params(("parallel", "arbitrary")),
    )(x, norm_w.reshape(1, D), w)


def _compress_kernel(k_ref, v_ref, pa_ref, pb_ref, wka_ref, wkb_ref, wva_ref, wvb_ref, kc_ref, vc_ref):
    def one(x_ref, wa_ref, wb_ref, o_ref):
        x = x_ref[...].astype(F32)
        a = _dot((x + pa_ref[...]).astype(BF16), wa_ref[...])
        b = _dot((x + pb_ref[...]).astype(BF16), wb_ref[...])
        n = b.shape[0]
        o_ref[...] = (a + pltpu.roll(b, n - 1, 0)).astype(o_ref.dtype)

    one(k_ref, wka_ref, wkb_ref, kc_ref)
    one(v_ref, wva_ref, wvb_ref, vc_ref)


def compress(k16, v16, pos_a, pos_b, wka, wkb, wva, wvb, B):
    R, C = k16.shape
    nb = R // B
    dkv = wka.shape[1]
    full = lambda shp: pl.BlockSpec(shp, lambda b: (0, 0))
    return pl.pallas_call(
        _compress_kernel,
        grid=(B,),
        in_specs=[pl.BlockSpec((nb, C), lambda b: (b, 0)),
                  pl.BlockSpec((nb, C), lambda b: (b, 0)),
                  full((1, C)), full((1, C)),
                  full((C, dkv)), full((C, dkv)), full((C, dkv)), full((C, dkv))],
        out_specs=[pl.BlockSpec((nb, dkv), lambda b: (b, 0))] * 2,
        out_shape=[jax.ShapeDtypeStruct((R, dkv), BF16)] * 2,
        compiler_params=_params(("parallel",)),
    )(k16, v16, pos_a, pos_b, wka, wkb, wva, wvb)


def _bias_from_dist(tab_row, dist):
    idx = jnp.clip(dist, 0, 127)
    w = tab_row.shape[1]
    tab = jnp.broadcast_to(tab_row, (idx.shape[0], w))
    parts = [jnp.take_along_axis(tab, idx[:, k:k + w], axis=1) for k in range(0, idx.shape[1], w)]
    return parts[0] if len(parts) == 1 else jnp.concatenate(parts, axis=1)


def _cmp_select_kernel(q_ref, kc_ref, vc_ref, tab_ref, ovt_ref, ocmp_ref, selb_ref, *, n_qb, scale):
    c = pl.program_id(0) % n_qb
    tq = q_ref.shape[0]
    n_c = kc_ref.shape[0]
    n_sel = ovt_ref.shape[0]
    t0 = c * tq
    t_col = t0 + lax.broadcasted_iota(jnp.int32, (tq, n_c), 0)
    n_row = lax.broadcasted_iota(jnp.int32, (tq, n_c), 1)
    dist = t_col - (n_row * STRIDE_CMP + (L_CMP - 1))
    valid = dist >= 0

    j_io = lax.broadcasted_iota(jnp.int32, (n_sel, tq), 0)
    t_io = t0 + lax.broadcasted_iota(jnp.int32, (n_sel, tq), 1)
    blk_t = t_io // L_SEL
    forced = (j_io == 0) | (j_io == blk_t) | (j_io == blk_t - 1)
    causal_blk = j_io * L_SEL <= t_io

    for h in range(N_KV_HEADS):
        kc = kc_ref[:, h * HEAD_DIM:(h + 1) * HEAD_DIM]
        vc = vc_ref[:, h * HEAD_DIM:(h + 1) * HEAD_DIM]
        psum = jnp.zeros((tq, n_c), F32)
        for g in range(GQA):
            hd = h * GQA + g
            qh = q_ref[:, hd * HEAD_DIM:(hd + 1) * HEAD_DIM]
            bias = _bias_from_dist(tab_ref[hd:hd + 1, :], dist)
            s = _dot_nt(qh, kc) * scale + bias
            s = jnp.where(valid, s, NEG_BIG)
            m = jnp.max(s, axis=-1, keepdims=True)
            e = jnp.where(valid, jnp.exp(s - m), 0.0)
            d = jnp.sum(e, axis=-1, keepdims=True)
            p = e / jnp.where(d > 0, d, 1.0)
            ocmp_ref[:, hd * HEAD_DIM:(hd + 1) * HEAD_DIM] = _dot(p.astype(BF16), vc)
            psum = psum + p
        p_hi = psum.astype(BF16)
        p_lo = (psum - p_hi.astype(F32)).astype(BF16)
        ovt = ovt_ref[...]
        imp = _dot_nt(ovt, p_hi) + _dot_nt(ovt, p_lo)
        score = jnp.where(forced, FORCED_SCORE, jnp.where(causal_blk, imp, -1.0))
        rank = jnp.zeros((n_sel, tq), F32)
        for jp in range(n_sel):
            row = score[jp:jp + 1, :]
            rank = rank + jnp.where(j_io > jp, jnp.where(row >= score, 1.0, 0.0),
                                    jnp.where(row > score, 1.0, 0.0))
        selb = jnp.where(rank < float(N_SEL), 0.0, SEL_OFF)
        if n_sel < HEAD_DIM:
            selb = jnp.concatenate([selb, jnp.zeros((HEAD_DIM - n_sel, tq), F32)], axis=0)
        selb_ref[:, h * HEAD_DIM:(h + 1) * HEAD_DIM] = selb.T.astype(BF16)


def cmp_select(qkv, kc, vc, tab, ovt, B, S):
    T = qkv.shape[0]
    tq = Q_BLOCK
    n_qb = S // tq
    n_c = kc.shape[0] // B
    dq = N_Q_HEADS * HEAD_DIM
    dkv = N_KV_HEADS * HEAD_DIM
    kern = functools.partial(_cmp_select_kernel, n_qb=n_qb, scale=HEAD_DIM ** -0.5)
    return pl.pallas_call(
        kern,
        grid=(T // tq,),
        in_specs=[pl.BlockSpec((tq, dq), lambda i: (i, 0)),
                  pl.BlockSpec((n_c, dkv), lambda i: (i // n_qb, 0)),
                  pl.BlockSpec((n_c, dkv), lambda i: (i // n_qb, 0)),
                  pl.BlockSpec(tab.shape, lambda i: (0, 0)),
                  pl.BlockSpec(ovt.shape, lambda i: (0, 0))],
        out_specs=[pl.BlockSpec((tq, dq), lambda i: (i, 0)),
                   pl.BlockSpec((tq, dkv), lambda i: (i, 0))],
        out_shape=[jax.ShapeDtypeStruct((T, dq), F32),
                   jax.ShapeDtypeStruct((T, dkv), BF16)],
        compiler_params=_params(("parallel",)),
    )(qkv, kc, vc, tab, ovt)


def _flash_chunk(k, vt, q, bias, m_ref, l_ref, acc_ref, scale):
    s = _dot_nt(k, q) * scale + bias
    m_old = m_ref[...]
    m_new = jnp.maximum(m_old, jnp.max(s, axis=0, keepdims=True))
    alpha = jnp.exp(m_old - m_new)
    p = jnp.exp(s - m_new)
    l_ref[...] = alpha * l_ref[...] + jnp.sum(p, axis=0, keepdims=True)
    acc_ref[...] = alpha * acc_ref[...] + _dot(vt, p.astype(BF16))
    m_ref[...] = m_new


def _sel_win_kernel(q_ref, selb_ref, ks_ref, vst_ref, kw_ref, vwt_ref, e2_ref, tab_ref, ocmp_ref,
                    gl_ref, gnw_ref, o_ref, m_ref, l_ref, acc_ref, *, n_qb, scale):
    c = pl.program_id(0) % n_qb
    tq = q_ref.shape[0]
    rows = GQA * tq
    n_back = WINDOW // tq
    near = WINDOW + tq
    j_io = lax.broadcasted_iota(jnp.int32, (tq, tq), 0)
    i_io = lax.broadcasted_iota(jnp.int32, (tq, tq), 1)
    dij = i_io - j_io
    dij4 = jnp.concatenate([dij] * GQA, axis=1)
    causal = dij4 >= 0
    in_seq = lax.broadcasted_iota(jnp.int32, (near, rows), 0) >= WINDOW - c * tq
    sig = jax.nn.sigmoid(gl_ref[...])
    near0 = pl.multiple_of(c * tq, tq)
    n_far = jnp.maximum(c - n_back, 0)
    n_full = n_far // n_back
    n_rem = n_far - n_full * n_back

    def hsl(h):
        return slice(h * HEAD_DIM, (h + 1) * HEAD_DIM)

    def far0(i):
        return pl.multiple_of(WINDOW + i * WINDOW, WINDOW)

    def reset():
        m_ref[...] = jnp.full(m_ref.shape, NEG_BIG, F32)
        l_ref[...] = jnp.zeros(l_ref.shape, F32)
        acc_ref[...] = jnp.zeros(acc_ref.shape, F32)

    def result():
        ot = acc_ref[...] / l_ref[...]
        return [ot[:, g * tq:(g + 1) * tq].T for g in range(GQA)]

    for h in range(N_KV_HEADS):
        q4 = jnp.concatenate([q_ref[:, (h * GQA + g) * HEAD_DIM:(h * GQA + g + 1) * HEAD_DIM]
                              for g in range(GQA)], axis=0)
        sb = selb_ref[:, hsl(h)]
        q_aug = jnp.concatenate([q4, jnp.concatenate([sb] * GQA, axis=0)], axis=1)
        tabs = [jnp.broadcast_to(tab_ref[h * GQA + g:h * GQA + g + 1, :], (tq, HEAD_DIM)) for g in range(GQA)]
        d0 = jnp.concatenate([jnp.take_along_axis(t, jnp.clip(dij, 0, 127), axis=1) for t in tabs], axis=1)
        d1 = jnp.concatenate([jnp.take_along_axis(t, jnp.clip(dij + tq, 0, 127), axis=1) for t in tabs], axis=1)
        far = jnp.concatenate([t[0:1, HEAD_DIM - 1:HEAD_DIM] + jnp.zeros((1, tq), F32) for t in tabs], axis=1)
        far_t = jnp.broadcast_to(far, (tq, rows))
        diag = jnp.where(causal, d0, NEG_BIG)
        bias_sel = jnp.concatenate([far_t] * (n_back - 1) + [d1, diag], axis=0)
        bias_win = jnp.concatenate([jnp.where(dij4 < 0, far_t, NEG_BIG)] + [far_t] * (n_back - 2) + [d1, diag],
                                   axis=0)
        bias_sel = jnp.where(in_seq, bias_sel, NEG_BIG)
        bias_win = jnp.where(in_seq, bias_win, NEG_BIG)

        reset()
        k_near = jnp.concatenate([ks_ref[pl.ds(near0, near), hsl(h)], e2_ref[pl.ds(near0, near), :]], axis=1)
        _flash_chunk(k_near, vst_ref[hsl(h), pl.ds(near0, near)], q_aug, bias_sel, m_ref, l_ref, acc_ref, scale)

        def far_chunk(i, bias):
            r0 = far0(i)
            k = jnp.concatenate([ks_ref[pl.ds(r0, WINDOW), hsl(h)], e2_ref[pl.ds(r0, WINDOW), :]], axis=1)
            _flash_chunk(k, vst_ref[hsl(h), pl.ds(r0, WINDOW)], q_aug, bias, m_ref, l_ref, acc_ref, scale)

        def body(i, carry):
            far_chunk(i, far)
            return carry

        lax.fori_loop(0, n_full, body, 0)

        @pl.when(n_rem > 0)
        def _():
            live = lax.broadcasted_iota(jnp.int32, (WINDOW, rows), 0) < n_rem * tq
            far_chunk(n_full, jnp.where(live, jnp.broadcast_to(far, (WINDOW, rows)), NEG_BIG))

        o_sel = result()

        reset()
        _flash_chunk(kw_ref[pl.ds(near0, near), hsl(h)], vwt_ref[hsl(h), pl.ds(near0, near)], q4, bias_win,
                     m_ref, l_ref, acc_ref, scale)
        o_win = result()

        for g in range(GQA):
            hd = h * GQA + g
            o = (sig[:, 3 * hd:3 * hd + 1] * ocmp_ref[:, hsl(hd)]
                 + sig[:, 3 * hd + 1:3 * hd + 2] * o_sel[g]
                 + sig[:, 3 * hd + 2:3 * hd + 3] * o_win[g])
            y = o * lax.rsqrt(jnp.mean(o * o, axis=-1, keepdims=True) + EPS)
            o_ref[:, hsl(hd)] = (y * gnw_ref[:, hsl(hd)]).astype(o_ref.dtype)


def sel_win(qkv, selb, ksp, vstp, kwp, vwtp, e2p, tab, ocmp, gc, gnw, B, S, gate_blk):
    T = qkv.shape[0]
    tq = Q_BLOCK
    n_qb = S // tq
    dq = N_Q_HEADS * HEAD_DIM
    dkv = N_KV_HEADS * HEAD_DIM
    SP = S + WINDOW
    kern = functools.partial(_sel_win_kernel, n_qb=n_qb, scale=HEAD_DIM ** -0.5)
    k_spec = pl.BlockSpec((SP, dkv), lambda i: (i // n_qb, 0))
    vt_spec = pl.BlockSpec((dkv, SP), lambda i: (i // n_qb, 0))
    return pl.pallas_call(
        kern,
        grid=(T // tq,),
        in_specs=[pl.BlockSpec((tq, dq), lambda i: (i, 0)),
                  pl.BlockSpec((tq, dkv), lambda i: (i, 0)),
                  k_spec, vt_spec, k_spec, vt_spec,
                  pl.BlockSpec((SP, HEAD_DIM), lambda i: (0, 0)),
                  pl.BlockSpec(tab.shape, lambda i: (0, 0)),
                  pl.BlockSpec((tq, dq), lambda i: (i, 0)),
                  pl.BlockSpec((tq, HEAD_DIM), lambda i: (i, gate_blk)),
                  pl.BlockSpec((1, dq), lambda i: (0, 0))],
        out_specs=pl.BlockSpec((tq, dq), lambda i: (i, 0)),
        out_shape=jax.ShapeDtypeStruct((T, dq), BF16),
        scratch_shapes=[pltpu.VMEM((1, GQA * tq), F32), pltpu.VMEM((1, GQA * tq), F32),
                        pltpu.VMEM((HEAD_DIM, GQA * tq), F32)],
        compiler_params=_params(("parallel",)),
    )(qkv, selb, ksp, vstp, kwp, vwtp, e2p, tab, ocmp, gc, gnw)


def _conv_kernel(b_ref, c_ref, h_ref, cp_ref, hp_ref, cw_ref, cb_ref, gnw_ref, o_ref, u_ref, *, tiles_per_seq):
    tm = b_ref.shape[0]
    first = (pl.program_id(0) % tiles_per_seq) == 0
    u_prev = cp_ref[...] * hp_ref[...]
    u_ref[0:8, :] = jnp.where(first, 0.0, u_prev)
    u = c_ref[...] * h_ref[...]
    u_ref[8:8 + tm, :] = u
    y = (cw_ref[0:1, :] * u_ref[6:6 + tm, :] + cw_ref[1:2, :] * u_ref[7:7 + tm, :]
         + cw_ref[2:3, :] * u + cb_ref[...])
    o = b_ref[...] * y
    n_groups = o.shape[1] // HEAD_DIM
    for g in range(n_groups):
        sl = slice(g * HEAD_DIM, (g + 1) * HEAD_DIM)
        og = o[:, sl]
        yg = og * lax.rsqrt(jnp.mean(og * og, axis=-1, keepdims=True) + EPS)
        o_ref[:, sl] = (yg * gnw_ref[:, sl]).astype(o_ref.dtype)


def conv_mixer(gc, conv_w, conv_b, gnw, S, tm):
    T = gc.shape[0]
    dc = conv_w.shape[1]
    tps = S // tm
    kern = functools.partial(_conv_kernel, tiles_per_seq=tps)
    prev = lambda col: pl.BlockSpec((8, dc), lambda i, col=col: (jnp.maximum(i * (tm // 8) - 1, 0), col))
    cur = lambda col: pl.BlockSpec((tm, dc), lambda i, col=col: (i, col))
    return pl.pallas_call(
        kern,
        grid=(T // tm,),
        in_specs=[cur(0), cur(1), cur(2), prev(1), prev(2),
                  pl.BlockSpec((8, dc), lambda i: (0, 0)),
                  pl.BlockSpec((1, dc), lambda i: (0, 0)),
                  pl.BlockSpec((1, dc), lambda i: (0, 0))],
        out_specs=pl.BlockSpec((tm, dc), lambda i: (i, 0)),
        out_shape=jax.ShapeDtypeStruct((T, dc), BF16),
        scratch_shapes=[pltpu.VMEM((tm + 8, dc), F32)],
        compiler_params=_params(("parallel",)),
    )(gc, gc, gc, gc, gc, conv_w, conv_b, gnw)


def _out_proj_kernel(ma_ref, mc_ref, wa_ref, wc_ref, x_ref, o_ref):
    o_ref[...] = x_ref[...] + _dot(ma_ref[...], wa_ref[...]) + _dot(mc_ref[...], wc_ref[...])


def out_proj(ma, mc, wa, wc, x, tm, tn):
    T, da = ma.shape
    dc = mc.shape[1]
    D = x.shape[1]
    return pl.pallas_call(
        _out_proj_kernel,
        grid=(T // tm, D // tn),
        in_specs=[pl.BlockSpec((tm, da), lambda i, j: (i, 0)),
                  pl.BlockSpec((tm, dc), lambda i, j: (i, 0)),
                  pl.BlockSpec((da, tn), lambda i, j: (0, j)),
                  pl.BlockSpec((dc, tn), lambda i, j: (0, j)),
                  pl.BlockSpec((tm, tn), lambda i, j: (i, j))],
        out_specs=pl.BlockSpec((tm, tn), lambda i, j: (i, j)),
        out_shape=jax.ShapeDtypeStruct((T, D), F32),
        compiler_params=_params(("parallel", "arbitrary")),
    )(ma, mc, wa, wc, x)


def _peer_route_kernel(q_ref, sk_ref, e_ref, g_ref, sv_ref, si_ref):
    tt = q_ref.shape[0]
    K = PEER_TOPK
    n_io = lax.broadcasted_iota(jnp.int32, (N_KEYS, tt), 0).astype(F32)
    r16 = lax.broadcasted_iota(jnp.int32, (16, tt), 0).astype(F32)
    r8 = lax.broadcasted_iota(jnp.int32, (8, tt), 0).astype(F32)
    ninf = -jnp.inf

    for h in range(PEER_HEADS):
        for c in range(2):
            col = (h * 2 + c) * HEAD_DIM
            s = _dot_nt(sk_ref[h * 2 + c], q_ref[:, col:col + HEAD_DIM])
            for k in range(K):
                m = jnp.max(s, axis=0, keepdims=True)
                idx = jnp.min(jnp.where(s == m, n_io, float(N_KEYS)), axis=0, keepdims=True)
                s = jnp.where(n_io == idx, ninf, s)
                sv_ref[c, k:k + 1, :] = m
                si_ref[c, k:k + 1, :] = idx
        sv0, sv1 = sv_ref[0], sv_ref[1]
        si0, si1 = si_ref[0], si_ref[1]
        cands, flats, exps = [], [], []
        for a, nb, rows in ((0, 16, 16), (1, 8, 8), (2, 5, 8), (3, 4, 8)):
            r_io = r16 if rows == 16 else r8
            val = sv0[a:a + 1, :] + sv1[0:rows, :]
            cands.append(jnp.where(r_io < nb, val, ninf))
            flats.append(a * 16.0 + r_io)
            exps.append(si0[a:a + 1, :] * float(N_KEYS) + si1[0:rows, :])
        for b, lo, hi, rows in ((0, 4, 16, 16), (1, 4, 8, 8), (2, 4, 5, 8)):
            r_io = r16 if rows == 16 else r8
            val = sv0[0:rows, :] + sv1[b:b + 1, :]
            cands.append(jnp.where((r_io >= lo) & (r_io < hi), val, ninf))
            flats.append(r_io * 16.0 + b)
            exps.append(si0[0:rows, :] * float(N_KEYS) + si1[b:b + 1, :])
        cand = jnp.concatenate(cands, axis=0)
        flat = jnp.concatenate(flats, axis=0)
        expt = jnp.concatenate(exps, axis=0)
        cvs, exs = [], []
        for k in range(K):
            m = jnp.max(cand, axis=0, keepdims=True)
            fsel = jnp.min(jnp.where(cand == m, flat, 1e9), axis=0, keepdims=True)
            hit = flat == fsel
            exs.append(jnp.max(jnp.where(hit, expt, -1.0), axis=0, keepdims=True))
            cand = jnp.where(hit, ninf, cand)
            cvs.append(m)
        cv = jnp.concatenate(cvs, axis=0)
        ex = jnp.concatenate(exs, axis=0)
        ev = jnp.exp(cv - jnp.max(cv, axis=0, keepdims=True))
        gates = ev / jnp.sum(ev, axis=0, keepdims=True)
        e_ref[0, h * K:(h + 1) * K, :] = ex.astype(jnp.int32)
        g_ref[0, h * K:(h + 1) * K, :] = gates


def peer_route(qp, subkeys):
    T = qp.shape[0]
    tt = 128
    P = PEER_HEADS * PEER_TOPK
    nt = T // tt
    return pl.pallas_call(
        _peer_route_kernel,
        grid=(nt,),
        in_specs=[pl.BlockSpec((tt, qp.shape[1]), lambda i: (i, 0)),
                  pl.BlockSpec(subkeys.shape, lambda i: (0, 0, 0))],
        out_specs=[pl.BlockSpec((1, P, tt), lambda i: (i, 0, 0))] * 2,
        out_shape=[jax.ShapeDtypeStruct((nt, P, tt), jnp.int32),
                   jax.ShapeDtypeStruct((nt, P, tt), F32)],
        scratch_shapes=[pltpu.VMEM((2, PEER_TOPK, tt), F32), pltpu.VMEM((2, PEER_TOPK, tt), F32)],
        compiler_params=_params(("parallel",)),
    )(qp, subkeys)


def _peer_gbuild_kernel(e_ref, g_ref, o_ref, i1_ref, i2_ref, gt_ref, s_ref):
    tt = e_ref.shape[2]
    e = e_ref[0].T
    i1_ref[...] = e >> 7
    i2_ref[...] = e & (N_KEYS - 1)
    gt_ref[...] = g_ref[0].T
    P = e.shape[1]
    k_io = lax.broadcasted_iota(jnp.int32, (N_KEYS, P), 0)
    group = 16
    sub = 8
    def body(tg, carry):
        base = pl.multiple_of(tg * group, group)
        r1s = i1_ref[pl.ds(base, group), :]
        r2s = i2_ref[pl.ds(base, group), :]
        rgs = gt_ref[pl.ds(base, group), :]
        for u in range(group):
            lhs = jnp.where(k_io == r1s[u:u + 1, :], rgs[u:u + 1, :], 0.0).astype(BF16)
            rhs = jnp.where(k_io == r2s[u:u + 1, :], 1.0, 0.0).astype(BF16)
            g_t = _dot_nt(lhs, rhs)
            for a in range(N_KEYS // sub):
                row0 = pl.multiple_of((a * tt + base + u) * sub, sub)
                s_ref[pl.ds(row0, sub), :] = g_t[a * sub:(a + 1) * sub, :]
        return carry

    lax.fori_loop(0, tt // group, body, 0)
    for i1 in range(N_KEYS):
        a, r = divmod(i1, sub)
        o_ref[i1] = s_ref[pl.ds(a * tt * sub + r, tt, stride=sub), :].astype(o_ref.dtype)


def peer_gbuild(ex, gates):
    nt, P, tt = ex.shape
    T = nt * tt
    return pl.pallas_call(
        _peer_gbuild_kernel,
        grid=(nt,),
        in_specs=[pl.BlockSpec((1, P, tt), lambda i: (i, 0, 0))] * 2,
        out_specs=pl.BlockSpec((N_KEYS, tt, N_KEYS), lambda i: (0, i, 0)),
        out_shape=jax.ShapeDtypeStruct((N_KEYS, T, N_KEYS), BF16),
        scratch_shapes=[pltpu.VMEM((tt, P), jnp.int32), pltpu.VMEM((tt, P), jnp.int32),
                        pltpu.VMEM((tt, P), F32), pltpu.VMEM((tt * N_KEYS, N_KEYS), F32)],
        compiler_params=_params(("parallel",)),
    )(ex, gates)


def _peer_dense_kernel(h_ref, nw_ref, fw_ref, u_ref, v_ref, gm_ref, o_ref, hn_ref, acc_ref):
    j = pl.program_id(1)

    @pl.when(j == 0)
    def _():
        x = h_ref[...]
        y = x * lax.rsqrt(jnp.mean(x * x, axis=-1, keepdims=True) + EPS)
        hn_ref[...] = (y * nw_ref[...]).astype(BF16)
        acc_ref[...] = jnp.zeros(acc_ref.shape, F32)

    a = _dot_nt(hn_ref[...], u_ref[...])
    gm = jnp.concatenate([gm_ref[k] for k in range(gm_ref.shape[0])], axis=1)
    w = (gm.astype(F32) * jax.nn.gelu(a)).astype(BF16)
    acc_ref[...] += _dot(w, v_ref[...])

    @pl.when(j == pl.num_programs(1) - 1)
    def _():
        x = h_ref[...] + acc_ref[...]
        y = x * lax.rsqrt(jnp.mean(x * x, axis=-1, keepdims=True) + EPS)
        o_ref[...] = y * fw_ref[...]


def peer_dense(h, ffn_nw, final_nw, u, v, gm, tt, ec):
    T, D = h.shape
    E = u.shape[0]
    return pl.pallas_call(
        _peer_dense_kernel,
        grid=(T // tt, E // ec),
        in_specs=[pl.BlockSpec((tt, D), lambda i, j: (i, 0)),
                  pl.BlockSpec((1, D), lambda i, j: (0, 0)),
                  pl.BlockSpec((1, D), lambda i, j: (0, 0)),
                  pl.BlockSpec((ec, D), lambda i, j: (j, 0)),
                  pl.BlockSpec((ec, D), lambda i, j: (j, 0)),
                  pl.BlockSpec((ec // N_KEYS, tt, N_KEYS), lambda i, j: (j, i, 0))],
        out_specs=pl.BlockSpec((tt, D), lambda i, j: (i, 0)),
        out_shape=jax.ShapeDtypeStruct((T, D), F32),
        scratch_shapes=[pltpu.VMEM((tt, D), BF16), pltpu.VMEM((tt, D), F32)],
        compiler_params=_params(("parallel", "arbitrary")),
    )(h, ffn_nw.reshape(1, D), final_nw.reshape(1, D), u, v, gm)


def _t5_bucket_np(n_dist):
    d = np.arange(n_dist)
    max_exact = NUM_BUCKETS // 2
    nf = np.maximum(d, 1).astype(np.float64)
    large = max_exact + (np.log(nf / max_exact) / math.log(MAX_DISTANCE / max_exact)
                         * (NUM_BUCKETS - max_exact)).astype(np.int64)
    large = np.minimum(large, NUM_BUCKETS - 1)
    return np.where(d < max_exact, d, large).astype(np.int32)


def _overlap_t_np(S):
    n_c = (S - L_CMP) // STRIDE_CMP + 1
    n_sel = S // L_SEL
    pos = np.arange(n_c)[:, None] * STRIDE_CMP + np.arange(L_CMP)[None, :]
    m = np.zeros((n_c + 1, n_sel), np.float32)
    np.add.at(m, (np.repeat(np.arange(n_c), L_CMP), (pos // L_SEL).reshape(-1)), 1.0 / L_CMP)
    return np.ascontiguousarray(m.T)


def _block_onehot_np(S):
    e2 = np.zeros((S, HEAD_DIM), np.float32)
    e2[np.arange(S), np.arange(S) // L_SEL] = 1.0
    return e2


def nsa_conv_mix(xt, B, S, attn_norm_w, w_in, w_cmp_k, w_cmp_v, cmp_pos, conv_w, conv_b,
                 attn_gnw, conv_gnw, rel_bias, tm=512):
    T, D = xt.shape
    dq = N_Q_HEADS * HEAD_DIM
    dkv = N_KV_HEADS * HEAD_DIM
    n_attn = dq + 6 * dkv
    n_gate = 3 * N_Q_HEADS
    dc = (w_in.shape[1] - n_attn - n_gate) // 3
    w_attn = w_in[:, :n_attn].astype(BF16)
    w_gate = jnp.pad(w_in[:, n_attn:n_attn + n_gate], ((0, 0), (0, HEAD_DIM - n_gate)))
    w_gc = jnp.concatenate([w_in[:, n_attn + n_gate:], w_gate], axis=1).astype(BF16)

    qkv = norm_matmul(xt, attn_norm_w, w_attn, BF16, tm, 640 if n_attn % 640 == 0 else 128)
    gc = norm_matmul(xt, attn_norm_w, w_gc, F32, tm, 640 if w_gc.shape[1] % 640 == 0 else 128)

    n16 = S // STRIDE_CMP
    k16 = qkv[:, dq:dq + dkv].reshape(B * n16, STRIDE_CMP * dkv)
    v16 = qkv[:, dq + dkv:dq + 2 * dkv].reshape(B * n16, STRIDE_CMP * dkv)

    def wbig(w, lo):
        wl = w[lo:lo + STRIDE_CMP]
        eye = jnp.eye(N_KV_HEADS, dtype=w.dtype)
        return jnp.einsum('lde,hg->lhdge', wl, eye).reshape(STRIDE_CMP * dkv, dkv).astype(BF16)

    def posrow(lo):
        p = cmp_pos[lo:lo + STRIDE_CMP]
        return jnp.broadcast_to(p[:, None, :], (STRIDE_CMP, N_KV_HEADS, HEAD_DIM)).reshape(1, STRIDE_CMP * dkv)

    kc, vc = compress(k16, v16, posrow(0), posrow(STRIDE_CMP),
                      wbig(w_cmp_k, 0), wbig(w_cmp_k, STRIDE_CMP),
                      wbig(w_cmp_v, 0), wbig(w_cmp_v, STRIDE_CMP), B)

    tab = rel_bias[_t5_bucket_np(HEAD_DIM)].T
    ovt = jnp.asarray(_overlap_t_np(S), BF16)
    e2 = jnp.asarray(_block_onehot_np(S), BF16)

    ocmp, selb = cmp_select(qkv, kc, vc, tab, ovt, B, S)
    def keys_padded(col):
        k = qkv[:, col:col + dkv].reshape(B, S, dkv)
        return jnp.pad(k, ((0, 0), (WINDOW, 0), (0, 0))).reshape(B * (WINDOW + S), dkv)

    def values_t_padded(col):
        v = qkv[:, col:col + dkv].reshape(B, S, dkv).transpose(0, 2, 1)
        return jnp.pad(v, ((0, 0), (0, 0), (WINDOW, 0))).reshape(B * dkv, WINDOW + S)

    e2p = jnp.pad(e2, ((WINDOW, 0), (0, 0)))
    attn = sel_win(qkv, selb, keys_padded(dq + 2 * dkv), values_t_padded(dq + 3 * dkv),
                   keys_padded(dq + 4 * dkv), values_t_padded(dq + 5 * dkv), e2p, tab, ocmp, gc,
                   attn_gnw.reshape(1, dq), B, S, gate_blk=(3 * dc) // HEAD_DIM)
    cw8 = jnp.pad(conv_w, ((0, 8 - conv_w.shape[0]), (0, 0)))
    conv = conv_mixer(gc, cw8, conv_b.reshape(1, dc), conv_gnw.reshape(1, dc), S, tm)
    return attn, conv


def peer_block(h, ffn_nw, final_nw, peer_wq, peer_subkeys, peer_u, peer_v, tm=512, tt=512, ec=1024):
    T, D = h.shape
    qp = norm_matmul(h, ffn_nw, peer_wq.astype(BF16), BF16, tm, 512)
    sk = peer_subkeys.reshape(PEER_HEADS * 2, N_KEYS, peer_subkeys.shape[-1]).astype(BF16)
    ex, gates = peer_route(qp, sk)
    gm = peer_gbuild(ex, gates)
    return peer_dense(h, ffn_nw, final_nw, peer_u.astype(BF16), peer_v.astype(BF16), gm, tt, ec)


def kernel(x, attn_norm_w, w_in, w_cmp_k, w_cmp_v, cmp_pos, conv_w, conv_b, attn_group_norm_w,
           conv_group_norm_w, w_out, rel_bias, ffn_norm_w, peer_wq, peer_subkeys, peer_u, peer_v,
           final_norm_w):
    B, S, D = x.shape
    T = B * S
    xt = x.reshape(T, D)
    attn, conv = nsa_conv_mix(xt, B, S, attn_norm_w[0], w_in[0], w_cmp_k[0], w_cmp_v[0], cmp_pos[0],
                              conv_w[0], conv_b[0], attn_group_norm_w[0], conv_group_norm_w[0], rel_bias)
    da = attn.shape[1]
    wo = w_out[0].astype(BF16)
    h = out_proj(attn, conv, wo[:da], wo[da:], xt, 512, 1024)
    out = peer_block(h, ffn_norm_w[0], final_norm_w, peer_wq[0], peer_subkeys[0], peer_u[0], peer_v[0])
    return out.reshape(B, S, D)
```

```python
import functools
import math

import jax
import jax.numpy as jnp
import numpy as np
from jax import lax
from jax.experimental import pallas as pl
from jax.experimental.pallas import tpu as pltpu

F32 = jnp.float32
BF16 = jnp.bfloat16

HEAD_DIM = 128
N_KV_HEADS = 2
GQA = 4
N_Q_HEADS = N_KV_HEADS * GQA
L_CMP = 32
STRIDE_CMP = 16
L_SEL = 64
N_SEL = 16
WINDOW = 512
Q_BLOCK = 128
FORCED_SCORE = float(GQA + 1)
NUM_BUCKETS = 32
MAX_DISTANCE = 128
PEER_HEADS = 8
N_KEYS = 128
PEER_TOPK = 16
EPS = 1e-6
NEG_BIG = -1e30
SEL_OFF = -float(2 ** 30)
VMEM_LIMIT = 56 * 1024 * 1024


def _dot(a, b):
    return jnp.dot(a, b, preferred_element_type=F32)


def _dot_nt(a, b):
    return lax.dot_general(a, b, (((1,), (1,)), ((), ())), preferred_element_type=F32)


def _params(sem, vmem=VMEM_LIMIT):
    return pltpu.CompilerParams(dimension_semantics=sem, vmem_limit_bytes=vmem)


def _norm_matmul_kernel(x_ref, nw_ref, w_ref, o_ref, xn_ref):
    @pl.when(pl.program_id(1) == 0)
    def _():
        x = x_ref[...]
        y = x * lax.rsqrt(jnp.mean(x * x, axis=-1, keepdims=True) + EPS)
        xn_ref[...] = (y * nw_ref[...]).astype(BF16)

    o_ref[...] = _dot(xn_ref[...], w_ref[...]).astype(o_ref.dtype)


def norm_matmul(x, norm_w, w, out_dtype, tm, tn):
    T, D = x.shape
    N = w.shape[1]
    return pl.pallas_call(
        _norm_matmul_kernel,
        grid=(T // tm, N // tn),
        in_specs=[pl.BlockSpec((tm, D), lambda i, j: (i, 0)),
                  pl.BlockSpec((1, D), lambda i, j: (0, 0)),
                  pl.BlockSpec((D, tn), lambda i, j: (0, j))],
        out_specs=pl.BlockSpec((tm, tn), lambda i, j: (i, j)),
        out_shape=jax.ShapeDtypeStruct((T, N), out_dtype),
        scratch_shapes=[pltpu.VMEM((tm, D), BF16)],
        compiler_params=_params(("parallel", "arbitrary")),
    )(x, norm_w.reshape(1, D), w)


def _compress_kernel(k_ref, v_ref, pa_ref, pb_ref, wka_ref, wkb_ref, wva_ref, wvb_ref, kc_ref, vc_ref):
    def one(x_ref, wa_ref, wb_ref, o_ref):
        x = x_ref[...].astype(F32)
        a = _dot((x + pa_ref[...]).astype(BF16), wa_ref[...])
        b = _dot((x + pb_ref[...]).astype(BF16), wb_ref[...])
        n = b.shape[0]
        o_ref[...] = (a + pltpu.roll(b, n - 1, 0)).astype(o_ref.dtype)

    one(k_ref, wka_ref, wkb_ref, kc_ref)
    one(v_ref, wva_ref, wvb_ref, vc_ref)


def compress(k16, v16, pos_a, pos_b, wka, wkb, wva, wvb, B):
    R, C = k16.shape
    nb = R // B
    dkv = wka.shape[1]
    full = lambda shp: pl.BlockSpec(shp, lambda b: (0, 0))
    return pl.pallas_call(
        _compress_kernel,
        grid=(B,),
        in_specs=[pl.BlockSpec((nb, C), lambda b: (b, 0)),
                  pl.BlockSpec((nb, C), lambda b: (b, 0)),
                  full((1, C)), full((1, C)),
                  full((C, dkv)), full((C, dkv)), full((C, dkv)), full((C, dkv))],
        out_specs=[pl.BlockSpec((nb, dkv), lambda b: (b, 0))] * 2,
        out_shape=[jax.ShapeDtypeStruct((R, dkv), BF16)] * 2,
        compiler_params=_params(("parallel",)),
    )(k16, v16, pos_a, pos_b, wka, wkb, wva, wvb)


def _bias_from_dist(tab_row, dist):
    idx = jnp.clip(dist, 0, 127)
    w = tab_row.shape[1]
    tab = jnp.broadcast_to(tab_row, (idx.shape[0], w))
    parts = [jnp.take_along_axis(tab, idx[:, k:k + w], axis=1) for k in range(0, idx.shape[1], w)]
    return parts[0] if len(parts) == 1 else jnp.concatenate(parts, axis=1)


def _cmp_select_kernel(q_ref, kc_ref, vc_ref, tab_ref, ovt_ref, ocmp_ref, selb_ref, *, n_qb, scale):
    c = pl.program_id(0) % n_qb
    tq = q_ref.shape[0]
    n_c = kc_ref.shape[0]
    n_sel = ovt_ref.shape[0]
    t0 = c * tq
    t_col = t0 + lax.broadcasted_iota(jnp.int32, (tq, n_c), 0)
    n_row = lax.broadcasted_iota(jnp.int32, (tq, n_c), 1)
    dist = t_col - (n_row * STRIDE_CMP + (L_CMP - 1))
    valid = dist >= 0

    j_io = lax.broadcasted_iota(jnp.int32, (n_sel, tq), 0)
    t_io = t0 + lax.broadcasted_iota(jnp.int32, (n_sel, tq), 1)
    blk_t = t_io // L_SEL
    forced = (j_io == 0) | (j_io == blk_t) | (j_io == blk_t - 1)
    causal_blk = j_io * L_SEL <= t_io

    for h in range(N_KV_HEADS):
        kc = kc_ref[:, h * HEAD_DIM:(h + 1) * HEAD_DIM]
        vc = vc_ref[:, h * HEAD_DIM:(h + 1) * HEAD_DIM]
        psum = jnp.zeros((tq, n_c), F32)
        for g in range(GQA):
            hd = h * GQA + g
            qh = q_ref[:, hd * HEAD_DIM:(hd + 1) * HEAD_DIM]
            bias = _bias_from_dist(tab_ref[hd:hd + 1, :], dist)
            s = _dot_nt(qh, kc) * scale + bias
            s = jnp.where(valid, s, NEG_BIG)
            m = jnp.max(s, axis=-1, keepdims=True)
            e = jnp.where(valid, jnp.exp(s - m), 0.0)
            d = jnp.sum(e, axis=-1, keepdims=True)
            p = e / jnp.where(d > 0, d, 1.0)
            ocmp_ref[:, hd * HEAD_DIM:(hd + 1) * HEAD_DIM] = _dot(p.astype(BF16), vc)
            psum = psum + p
        p_hi = psum.astype(BF16)
        p_lo = (psum - p_hi.astype(F32)).astype(BF16)
        ovt = ovt_ref[...]
        imp = _dot_nt(ovt, p_hi) + _dot_nt(ovt, p_lo)
        score = jnp.where(forced, FORCED_SCORE, jnp.where(causal_blk, imp, -1.0))
        rank = jnp.zeros((n_sel, tq), F32)
        for jp in range(n_sel):
            row = score[jp:jp + 1, :]
            rank = rank + jnp.where(j_io > jp, jnp.where(row >= score, 1.0, 0.0),
                                    jnp.where(row > score, 1.0, 0.0))
        selb = jnp.where(rank < float(N_SEL), 0.0, SEL_OFF)
        if n_sel < HEAD_DIM:
            selb = jnp.concatenate([selb, jnp.zeros((HEAD_DIM - n_sel, tq), F32)], axis=0)
        selb_ref[:, h * HEAD_DIM:(h + 1) * HEAD_DIM] = selb.T.astype(BF16)


def cmp_select(qkv, kc, vc, tab, ovt, B, S):
    T = qkv.shape[0]
    tq = Q_BLOCK
    n_qb = S // tq
    n_c = kc.shape[0] // B
    dq = N_Q_HEADS * HEAD_DIM
    dkv = N_KV_HEADS * HEAD_DIM
    kern = functools.partial(_cmp_select_kernel, n_qb=n_qb, scale=HEAD_DIM ** -0.5)
    return pl.pallas_call(
        kern,
        grid=(T // tq,),
        in_specs=[pl.BlockSpec((tq, dq), lambda i: (i, 0)),
                  pl.BlockSpec((n_c, dkv), lambda i: (i // n_qb, 0)),
                  pl.BlockSpec((n_c, dkv), lambda i: (i // n_qb, 0)),
                  pl.BlockSpec(tab.shape, lambda i: (0, 0)),
                  pl.BlockSpec(ovt.shape, lambda i: (0, 0))],
        out_specs=[pl.BlockSpec((tq, dq), lambda i: (i, 0)),
                   pl.BlockSpec((tq, dkv), lambda i: (i, 0))],
        out_shape=[jax.ShapeDtypeStruct((T, dq), F32),
                   jax.ShapeDtypeStruct((T, dkv), BF16)],
        compiler_params=_params(("parallel",)),
    )(qkv, kc, vc, tab, ovt)


def _flash_update(s_raw, vt, bias, m_ref, l_ref, acc_ref, scale):
    s = s_raw * scale + bias
    m_old = m_ref[...]
    m_new = jnp.maximum(m_old, jnp.max(s, axis=0, keepdims=True))
    alpha = jnp.exp(m_old - m_new)
    p = jnp.exp(s - m_new)
    l_ref[...] = alpha * l_ref[...] + jnp.sum(p, axis=0, keepdims=True)
    acc_ref[...] = alpha * acc_ref[...] + _dot(vt, p.astype(BF16))
    m_ref[...] = m_new


def _flash_chunk(k, vt, q, bias, m_ref, l_ref, acc_ref, scale):
    _flash_update(_dot_nt(k, q), vt, bias, m_ref, l_ref, acc_ref, scale)


def _sel_win_kernel(q_ref, selb_ref, ks_ref, vst_ref, kw_ref, vwt_ref, e2_ref, tab_ref, ocmp_ref,
                    gl_ref, gnw_ref, o_ref, m_ref, l_ref, acc_ref, s_ref, *, n_qb, scale):
    c = pl.program_id(0) % n_qb
    tq = q_ref.shape[0]
    rows = GQA * tq
    n_back = WINDOW // tq
    near = WINDOW + tq
    j_io = lax.broadcasted_iota(jnp.int32, (tq, tq), 0)
    i_io = lax.broadcasted_iota(jnp.int32, (tq, tq), 1)
    dij = i_io - j_io
    dij4 = jnp.concatenate([dij] * GQA, axis=1)
    causal = dij4 >= 0
    in_seq = lax.broadcasted_iota(jnp.int32, (near, rows), 0) >= WINDOW - c * tq
    sig = jax.nn.sigmoid(gl_ref[...])
    near0 = pl.multiple_of(c * tq, tq)
    n_far = jnp.maximum(c - n_back, 0)
    n_full = n_far // n_back
    n_rem = n_far - n_full * n_back

    def hsl(h):
        return slice(h * HEAD_DIM, (h + 1) * HEAD_DIM)

    def far0(i):
        return pl.multiple_of(WINDOW + i * WINDOW, WINDOW)

    def reset():
        m_ref[...] = jnp.full(m_ref.shape, NEG_BIG, F32)
        l_ref[...] = jnp.zeros(l_ref.shape, F32)
        acc_ref[...] = jnp.zeros(acc_ref.shape, F32)

    def result():
        ot = acc_ref[...] / l_ref[...]
        return [ot[:, g * tq:(g + 1) * tq].T for g in range(GQA)]

    for h in range(N_KV_HEADS):
        q4 = jnp.concatenate([q_ref[:, (h * GQA + g) * HEAD_DIM:(h * GQA + g + 1) * HEAD_DIM]
                              for g in range(GQA)], axis=0)
        sb = selb_ref[:, hsl(h)]
        q_aug = jnp.concatenate([q4, jnp.concatenate([sb] * GQA, axis=0)], axis=1)
        tabs = [jnp.broadcast_to(tab_ref[h * GQA + g:h * GQA + g + 1, :], (tq, HEAD_DIM)) for g in range(GQA)]
        d0 = jnp.concatenate([jnp.take_along_axis(t, jnp.clip(dij, 0, 127), axis=1) for t in tabs], axis=1)
        d1 = jnp.concatenate([jnp.take_along_axis(t, jnp.clip(dij + tq, 0, 127), axis=1) for t in tabs], axis=1)
        far = jnp.concatenate([t[0:1, HEAD_DIM - 1:HEAD_DIM] + jnp.zeros((1, tq), F32) for t in tabs], axis=1)
        far_t = jnp.broadcast_to(far, (tq, rows))
        diag = jnp.where(causal, d0, NEG_BIG)
        bias_sel = jnp.concatenate([far_t] * (n_back - 1) + [d1, diag], axis=0)
        bias_win = jnp.concatenate([jnp.where(dij4 < 0, far_t, NEG_BIG)] + [far_t] * (n_back - 2) + [d1, diag],
                                   axis=0)
        bias_sel = jnp.where(in_seq, bias_sel, NEG_BIG)
        bias_win = jnp.where(in_seq, bias_win, NEG_BIG)

        reset()
        k_near = jnp.concatenate([ks_ref[pl.ds(near0, near), hsl(h)], e2_ref[pl.ds(near0, near), :]], axis=1)
        _flash_chunk(k_near, vst_ref[hsl(h), pl.ds(near0, near)], q_aug, bias_sel, m_ref, l_ref, acc_ref, scale)

        def far_scores(i):
            r0 = far0(i)
            k = jnp.concatenate([ks_ref[pl.ds(r0, WINDOW), hsl(h)], e2_ref[pl.ds(r0, WINDOW), :]], axis=1)
            return _dot_nt(k, q_aug)

        def far_update(i, bias):
            _flash_update(s_ref[i % 2], vst_ref[hsl(h), pl.ds(far0(i), WINDOW)], bias, m_ref, l_ref, acc_ref, scale)

        s_ref[0] = far_scores(0)

        def body(i, carry):
            s_ref[(i + 1) % 2] = far_scores(i + 1)
            far_update(i, far)
            return carry

        lax.fori_loop(0, n_full, body, 0)

        @pl.when(n_rem > 0)
        def _():
            live = lax.broadcasted_iota(jnp.int32, (WINDOW, rows), 0) < n_rem * tq
            far_update(n_full, jnp.where(live, jnp.broadcast_to(far, (WINDOW, rows)), NEG_BIG))

        o_sel = result()

        reset()
        _flash_chunk(kw_ref[pl.ds(near0, near), hsl(h)], vwt_ref[hsl(h), pl.ds(near0, near)], q4, bias_win,
                     m_ref, l_ref, acc_ref, scale)
        o_win = result()

        for g in range(GQA):
            hd = h * GQA + g
            o = (sig[:, 3 * hd:3 * hd + 1] * ocmp_ref[:, hsl(hd)]
                 + sig[:, 3 * hd + 1:3 * hd + 2] * o_sel[g]
                 + sig[:, 3 * hd + 2:3 * hd + 3] * o_win[g])
            y = o * lax.rsqrt(jnp.mean(o * o, axis=-1, keepdims=True) + EPS)
            o_ref[:, hsl(hd)] = (y * gnw_ref[:, hsl(hd)]).astype(o_ref.dtype)


def sel_win(qkv, selb, ksp, vstp, kwp, vwtp, e2p, tab, ocmp, gl, gnw, B, S):
    T = qkv.shape[0]
    tq = Q_BLOCK
    n_qb = S // tq
    dq = N_Q_HEADS * HEAD_DIM
    dkv = N_KV_HEADS * HEAD_DIM
    SP = S + WINDOW
    kern = functools.partial(_sel_win_kernel, n_qb=n_qb, scale=HEAD_DIM ** -0.5)
    k_spec = pl.BlockSpec((SP, dkv), lambda i: (i // n_qb, 0))
    vt_spec = pl.BlockSpec((dkv, SP), lambda i: (i // n_qb, 0))
    return pl.pallas_call(
        kern,
        grid=(T // tq,),
        in_specs=[pl.BlockSpec((tq, dq), lambda i: (i, 0)),
                  pl.BlockSpec((tq, dkv), lambda i: (i, 0)),
                  k_spec, vt_spec, k_spec, vt_spec,
                  pl.BlockSpec((SP, HEAD_DIM), lambda i: (0, 0)),
                  pl.BlockSpec(tab.shape, lambda i: (0, 0)),
                  pl.BlockSpec((tq, dq), lambda i: (i, 0)),
                  pl.BlockSpec((tq, HEAD_DIM), lambda i: (i, 0)),
                  pl.BlockSpec((1, dq), lambda i: (0, 0))],
        out_specs=pl.BlockSpec((tq, dq), lambda i: (i, 0)),
        out_shape=jax.ShapeDtypeStruct((T, dq), BF16),
        scratch_shapes=[pltpu.VMEM((1, GQA * tq), F32), pltpu.VMEM((1, GQA * tq), F32),
                        pltpu.VMEM((HEAD_DIM, GQA * tq), F32), pltpu.VMEM((2, WINDOW, GQA * tq), F32)],
        compiler_params=_params(("parallel",)),
    )(qkv, selb, ksp, vstp, kwp, vwtp, e2p, tab, ocmp, gl, gnw)


def _conv_kernel(b_ref, c_ref, h_ref, cp_ref, hp_ref, cw_ref, cb_ref, gnw_ref, o_ref, u_ref, *, tiles_per_seq):
    tm = b_ref.shape[0]
    first = (pl.program_id(0) % tiles_per_seq) == 0
    u_prev = cp_ref[...] * hp_ref[...]
    u_ref[0:8, :] = jnp.where(first, 0.0, u_prev)
    u = c_ref[...] * h_ref[...]
    u_ref[8:8 + tm, :] = u
    y = (cw_ref[0:1, :] * u_ref[6:6 + tm, :] + cw_ref[1:2, :] * u_ref[7:7 + tm, :]
         + cw_ref[2:3, :] * u + cb_ref[...])
    o = b_ref[...] * y
    n_groups = o.shape[1] // HEAD_DIM
    for g in range(n_groups):
        sl = slice(g * HEAD_DIM, (g + 1) * HEAD_DIM)
        og = o[:, sl]
        yg = og * lax.rsqrt(jnp.mean(og * og, axis=-1, keepdims=True) + EPS)
        o_ref[:, sl] = (yg * gnw_ref[:, sl]).astype(o_ref.dtype)


def conv_mixer(gc, conv_w, conv_b, gnw, S, tm):
    T = gc.shape[0]
    dc = conv_w.shape[1]
    tps = S // tm
    kern = functools.partial(_conv_kernel, tiles_per_seq=tps)
    prev = lambda col: pl.BlockSpec((8, dc), lambda i, col=col: (jnp.maximum(i * (tm // 8) - 1, 0), col))
    cur = lambda col: pl.BlockSpec((tm, dc), lambda i, col=col: (i, col))
    return pl.pallas_call(
        kern,
        grid=(T // tm,),
        in_specs=[cur(0), cur(1), cur(2), prev(1), prev(2),
                  pl.BlockSpec((8, dc), lambda i: (0, 0)),
                  pl.BlockSpec((1, dc), lambda i: (0, 0)),
                  pl.BlockSpec((1, dc), lambda i: (0, 0))],
        out_specs=pl.BlockSpec((tm, dc), lambda i: (i, 0)),
        out_shape=jax.ShapeDtypeStruct((T, dc), BF16),
        scratch_shapes=[pltpu.VMEM((tm + 8, dc), F32)],
        compiler_params=_params(("parallel",)),
    )(gc, gc, gc, gc, gc, conv_w, conv_b, gnw)


def _out_proj_kernel(ma_ref, mc_ref, wa_ref, wc_ref, x_ref, o_ref):
    o_ref[...] = x_ref[...] + _dot(ma_ref[...], wa_ref[...]) + _dot(mc_ref[...], wc_ref[...])


def out_proj(ma, mc, wa, wc, x, tm, tn):
    T, da = ma.shape
    dc = mc.shape[1]
    D = x.shape[1]
    return pl.pallas_call(
        _out_proj_kernel,
        grid=(T // tm, D // tn),
        in_specs=[pl.BlockSpec((tm, da), lambda i, j: (i, 0)),
                  pl.BlockSpec((tm, dc), lambda i, j: (i, 0)),
                  pl.BlockSpec((da, tn), lambda i, j: (0, j)),
                  pl.BlockSpec((dc, tn), lambda i, j: (0, j)),
                  pl.BlockSpec((tm, tn), lambda i, j: (i, j))],
        out_specs=pl.BlockSpec((tm, tn), lambda i, j: (i, j)),
        out_shape=jax.ShapeDtypeStruct((T, D), F32),
        compiler_params=_params(("parallel", "arbitrary")),
    )(ma, mc, wa, wc, x)


def _peer_route_kernel(q_ref, sk_ref, e_ref, g_ref, sv_ref, si_ref):
    tt = q_ref.shape[0]
    K = PEER_TOPK
    n_io = lax.broadcasted_iota(jnp.int32, (N_KEYS, tt), 0).astype(F32)
    r16 = lax.broadcasted_iota(jnp.int32, (16, tt), 0).astype(F32)
    r8 = lax.broadcasted_iota(jnp.int32, (8, tt), 0).astype(F32)
    ninf = -jnp.inf

    for h in range(PEER_HEADS):
        for c in range(2):
            col = (h * 2 + c) * HEAD_DIM
            s = _dot_nt(sk_ref[h * 2 + c], q_ref[:, col:col + HEAD_DIM])
            for k in range(K):
                m = jnp.max(s, axis=0, keepdims=True)
                idx = jnp.min(jnp.where(s == m, n_io, float(N_KEYS)), axis=0, keepdims=True)
                s = jnp.where(n_io == idx, ninf, s)
                sv_ref[c, k:k + 1, :] = m
                si_ref[c, k:k + 1, :] = idx
        sv0, sv1 = sv_ref[0], sv_ref[1]
        si0, si1 = si_ref[0], si_ref[1]
        cands, flats, exps = [], [], []
        for a, nb, rows in ((0, 16, 16), (1, 8, 8), (2, 5, 8), (3, 4, 8)):
            r_io = r16 if rows == 16 else r8
            val = sv0[a:a + 1, :] + sv1[0:rows, :]
            cands.append(jnp.where(r_io < nb, val, ninf))
            flats.append(a * 16.0 + r_io)
            exps.append(si0[a:a + 1, :] * float(N_KEYS) + si1[0:rows, :])
        for b, lo, hi, rows in ((0, 4, 16, 16), (1, 4, 8, 8), (2, 4, 5, 8)):
            r_io = r16 if rows == 16 else r8
            val = sv0[0:rows, :] + sv1[b:b + 1, :]
            cands.append(jnp.where((r_io >= lo) & (r_io < hi), val, ninf))
            flats.append(r_io * 16.0 + b)
            exps.append(si0[0:rows, :] * float(N_KEYS) + si1[b:b + 1, :])
        cand = jnp.concatenate(cands, axis=0)
        flat = jnp.concatenate(flats, axis=0)
        expt = jnp.concatenate(exps, axis=0)
        cvs, exs = [], []
        for k in range(K):
            m = jnp.max(cand, axis=0, keepdims=True)
            fsel = jnp.min(jnp.where(cand == m, flat, 1e9), axis=0, keepdims=True)
            hit = flat == fsel
            exs.append(jnp.max(jnp.where(hit, expt, -1.0), axis=0, keepdims=True))
            cand = jnp.where(hit, ninf, cand)
            cvs.append(m)
        cv = jnp.concatenate(cvs, axis=0)
        ex = jnp.concatenate(exs, axis=0)
        ev = jnp.exp(cv - jnp.max(cv, axis=0, keepdims=True))
        gates = ev / jnp.sum(ev, axis=0, keepdims=True)
        e_ref[0, h * K:(h + 1) * K, :] = ex.astype(jnp.int32)
        g_ref[0, h * K:(h + 1) * K, :] = gates


def peer_route(qp, subkeys):
    T = qp.shape[0]
    tt = 128
    P = PEER_HEADS * PEER_TOPK
    nt = T // tt
    return pl.pallas_call(
        _peer_route_kernel,
        grid=(nt,),
        in_specs=[pl.BlockSpec((tt, qp.shape[1]), lambda i: (i, 0)),
                  pl.BlockSpec(subkeys.shape, lambda i: (0, 0, 0))],
        out_specs=[pl.BlockSpec((1, P, tt), lambda i: (i, 0, 0))] * 2,
        out_shape=[jax.ShapeDtypeStruct((nt, P, tt), jnp.int32),
                   jax.ShapeDtypeStruct((nt, P, tt), F32)],
        scratch_shapes=[pltpu.VMEM((2, PEER_TOPK, tt), F32), pltpu.VMEM((2, PEER_TOPK, tt), F32)],
        compiler_params=_params(("parallel",)),
    )(qp, subkeys)


def _peer_gbuild_kernel(e_ref, g_ref, o_ref, i1_ref, i2_ref, gt_ref, s_ref):
    tt = e_ref.shape[2]
    e = e_ref[0].T
    i1_ref[...] = e >> 7
    i2_ref[...] = e & (N_KEYS - 1)
    gt_ref[...] = g_ref[0].T
    P = e.shape[1]
    k_io = lax.broadcasted_iota(jnp.int32, (N_KEYS, P), 0)
    group = 16
    sub = 8
    def body(tg, carry):
        base = pl.multiple_of(tg * group, group)
        r1s = i1_ref[pl.ds(base, group), :]
        r2s = i2_ref[pl.ds(base, group), :]
        rgs = gt_ref[pl.ds(base, group), :]
        for u in range(group):
            lhs = jnp.where(k_io == r1s[u:u + 1, :], rgs[u:u + 1, :], 0.0).astype(BF16)
            rhs = jnp.where(k_io == r2s[u:u + 1, :], 1.0, 0.0).astype(BF16)
            g_t = _dot_nt(lhs, rhs)
            for a in range(N_KEYS // sub):
                row0 = pl.multiple_of((a * tt + base + u) * sub, sub)
                s_ref[pl.ds(row0, sub), :] = g_t[a * sub:(a + 1) * sub, :]
        return carry

    lax.fori_loop(0, tt // group, body, 0)
    for i1 in range(N_KEYS):
        a, r = divmod(i1, sub)
        o_ref[i1] = s_ref[pl.ds(a * tt * sub + r, tt, stride=sub), :].astype(o_ref.dtype)


def peer_gbuild(ex, gates):
    nt, P, tt = ex.shape
    T = nt * tt
    return pl.pallas_call(
        _peer_gbuild_kernel,
        grid=(nt,),
        in_specs=[pl.BlockSpec((1, P, tt), lambda i: (i, 0, 0))] * 2,
        out_specs=pl.BlockSpec((N_KEYS, tt, N_KEYS), lambda i: (0, i, 0)),
        out_shape=jax.ShapeDtypeStruct((N_KEYS, T, N_KEYS), BF16),
        scratch_shapes=[pltpu.VMEM((tt, P), jnp.int32), pltpu.VMEM((tt, P), jnp.int32),
                        pltpu.VMEM((tt, P), F32), pltpu.VMEM((tt * N_KEYS, N_KEYS), F32)],
        compiler_params=_params(("parallel",)),
    )(ex, gates)


def _peer_dense_kernel(h_ref, nw_ref, fw_ref, u_ref, v_ref, gm_ref, o_ref, hn_ref, acc_ref, w_ref):
    j = pl.program_id(1)
    n_chunks = pl.num_programs(1) - 1

    def gated_scores():
        a = _dot_nt(hn_ref[...], u_ref[...])
        gm = jnp.concatenate([gm_ref[k] for k in range(gm_ref.shape[0])], axis=1)
        return (gm.astype(F32) * jax.nn.gelu(a)).astype(BF16)

    @pl.when(j == 0)
    def _():
        x = h_ref[...]
        y = x * lax.rsqrt(jnp.mean(x * x, axis=-1, keepdims=True) + EPS)
        hn_ref[...] = (y * nw_ref[...]).astype(BF16)
        acc_ref[...] = jnp.zeros(acc_ref.shape, F32)
        w_ref[...] = gated_scores()

    @pl.when((j > 0) & (j < n_chunks))
    def _():
        w_prev = w_ref[...]
        acc_ref[...] += _dot(w_prev, v_ref[...])
        w_ref[...] = gated_scores()

    @pl.when(j == n_chunks)
    def _():
        x = h_ref[...] + acc_ref[...] + _dot(w_ref[...], v_ref[...])
        y = x * lax.rsqrt(jnp.mean(x * x, axis=-1, keepdims=True) + EPS)
        o_ref[...] = y * fw_ref[...]


def peer_dense(h, ffn_nw, final_nw, u, v, gm, tt, ec):
    T, D = h.shape
    E = u.shape[0]
    n_chunks = E // ec
    last = n_chunks - 1
    return pl.pallas_call(
        _peer_dense_kernel,
        grid=(T // tt, n_chunks + 1),
        in_specs=[pl.BlockSpec((tt, D), lambda i, j: (i, 0)),
                  pl.BlockSpec((1, D), lambda i, j: (0, 0)),
                  pl.BlockSpec((1, D), lambda i, j: (0, 0)),
                  pl.BlockSpec((ec, D), lambda i, j: (jnp.minimum(j, last), 0)),
                  pl.BlockSpec((ec, D), lambda i, j: (jnp.maximum(j - 1, 0), 0)),
                  pl.BlockSpec((ec // N_KEYS, tt, N_KEYS), lambda i, j: (jnp.minimum(j, last), i, 0))],
        out_specs=pl.BlockSpec((tt, D), lambda i, j: (i, 0)),
        out_shape=jax.ShapeDtypeStruct((T, D), F32),
        scratch_shapes=[pltpu.VMEM((tt, D), BF16), pltpu.VMEM((tt, D), F32), pltpu.VMEM((tt, ec), BF16)],
        compiler_params=_params(("parallel", "arbitrary")),
    )(h, ffn_nw.reshape(1, D), final_nw.reshape(1, D), u, v, gm)


def _t5_bucket_np(n_dist):
    d = np.arange(n_dist)
    max_exact = NUM_BUCKETS // 2
    nf = np.maximum(d, 1).astype(np.float64)
    large = max_exact + (np.log(nf / max_exact) / math.log(MAX_DISTANCE / max_exact)
                         * (NUM_BUCKETS - max_exact)).astype(np.int64)
    large = np.minimum(large, NUM_BUCKETS - 1)
    return np.where(d < max_exact, d, large).astype(np.int32)


def _overlap_t_np(S):
    n_c = (S - L_CMP) // STRIDE_CMP + 1
    n_sel = S // L_SEL
    pos = np.arange(n_c)[:, None] * STRIDE_CMP + np.arange(L_CMP)[None, :]
    m = np.zeros((n_c + 1, n_sel), np.float32)
    np.add.at(m, (np.repeat(np.arange(n_c), L_CMP), (pos // L_SEL).reshape(-1)), 1.0 / L_CMP)
    return np.ascontiguousarray(m.T)


def _block_onehot_np(S):
    e2 = np.zeros((S, HEAD_DIM), np.float32)
    e2[np.arange(S), np.arange(S) // L_SEL] = 1.0
    return e2


def nsa_conv_mix(xt, B, S, attn_norm_w, w_in, w_cmp_k, w_cmp_v, cmp_pos, conv_w, conv_b,
                 attn_gnw, conv_gnw, rel_bias, tm=512):
    T, D = xt.shape
    dq = N_Q_HEADS * HEAD_DIM
    dkv = N_KV_HEADS * HEAD_DIM
    n_attn = dq + 6 * dkv
    n_gate = 3 * N_Q_HEADS
    dc = (w_in.shape[1] - n_attn - n_gate) // 3
    w_attn = w_in[:, :n_attn].astype(BF16)
    w_gate = jnp.pad(w_in[:, n_attn:n_attn + n_gate], ((0, 0), (0, HEAD_DIM - n_gate))).astype(BF16)
    w_conv = w_in[:, n_attn + n_gate:].astype(BF16)

    qkv = norm_matmul(xt, attn_norm_w, w_attn, BF16, tm, n_attn // 2)
    gl = norm_matmul(xt, attn_norm_w, w_gate, F32, tm, HEAD_DIM)
    gc = norm_matmul(xt, attn_norm_w, w_conv, F32, tm, dc)

    n16 = S // STRIDE_CMP
    k16 = qkv[:, dq:dq + dkv].reshape(B * n16, STRIDE_CMP * dkv)
    v16 = qkv[:, dq + dkv:dq + 2 * dkv].reshape(B * n16, STRIDE_CMP * dkv)

    def wbig(w, lo):
        wl = w[lo:lo + STRIDE_CMP]
        eye = jnp.eye(N_KV_HEADS, dtype=w.dtype)
        return jnp.einsum('lde,hg->lhdge', wl, eye).reshape(STRIDE_CMP * dkv, dkv).astype(BF16)

    def posrow(lo):
        p = cmp_pos[lo:lo + STRIDE_CMP]
        return jnp.broadcast_to(p[:, None, :], (STRIDE_CMP, N_KV_HEADS, HEAD_DIM)).reshape(1, STRIDE_CMP * dkv)

    kc, vc = compress(k16, v16, posrow(0), posrow(STRIDE_CMP),
                      wbig(w_cmp_k, 0), wbig(w_cmp_k, STRIDE_CMP),
                      wbig(w_cmp_v, 0), wbig(w_cmp_v, STRIDE_CMP), B)

    tab = rel_bias[_t5_bucket_np(HEAD_DIM)].T
    ovt = jnp.asarray(_overlap_t_np(S), BF16)
    e2 = jnp.asarray(_block_onehot_np(S), BF16)

    ocmp, selb = cmp_select(qkv, kc, vc, tab, ovt, B, S)
    def keys_padded(col):
        k = qkv[:, col:col + dkv].reshape(B, S, dkv)
        return jnp.pad(k, ((0, 0), (WINDOW, 0), (0, 0))).reshape(B * (WINDOW + S), dkv)

    def values_t_padded(col):
        v = qkv[:, col:col + dkv].reshape(B, S, dkv).transpose(0, 2, 1)
        return jnp.pad(v, ((0, 0), (0, 0), (WINDOW, 0))).reshape(B * dkv, WINDOW + S)

    e2p = jnp.pad(e2, ((WINDOW, 0), (0, 0)))
    attn = sel_win(qkv, selb, keys_padded(dq + 2 * dkv), values_t_padded(dq + 3 * dkv),
                   keys_padded(dq + 4 * dkv), values_t_padded(dq + 5 * dkv), e2p, tab, ocmp, gl,
                   attn_gnw.reshape(1, dq), B, S)
    cw8 = jnp.pad(conv_w, ((0, 8 - conv_w.shape[0]), (0, 0)))
    conv = conv_mixer(gc, cw8, conv_b.reshape(1, dc), conv_gnw.reshape(1, dc), S, tm)
    return attn, conv


def peer_block(h, ffn_nw, final_nw, peer_wq, peer_subkeys, peer_u, peer_v, tm=512, tt=512, ec=1024):
    T, D = h.shape
    qp = norm_matmul(h, ffn_nw, peer_wq.astype(BF16), BF16, tm, 1024)
    sk = peer_subkeys.reshape(PEER_HEADS * 2, N_KEYS, peer_subkeys.shape[-1]).astype(BF16)
    ex, gates = peer_route(qp, sk)
    gm = peer_gbuild(ex, gates)
    return peer_dense(h, ffn_nw, final_nw, peer_u.astype(BF16), peer_v.astype(BF16), gm, tt, ec)


def kernel(x, attn_norm_w, w_in, w_cmp_k, w_cmp_v, cmp_pos, conv_w, conv_b, attn_group_norm_w,
           conv_group_norm_w, w_out, rel_bias, ffn_norm_w, peer_wq, peer_subkeys, peer_u, peer_v,
           final_norm_w):
    B, S, D = x.shape
    T = B * S
    xt = x.reshape(T, D)
    attn, conv = nsa_conv_mix(xt, B, S, attn_norm_w[0], w_in[0], w_cmp_k[0], w_cmp_v[0], cmp_pos[0],
                              conv_w[0], conv_b[0], attn_group_norm_w[0], conv_group_norm_w[0], rel_bias)
    da = attn.shape[1]
    wo = w_out[0].astype(BF16)
    h = out_proj(attn, conv, wo[:da], wo[da:], xt, 512, 1024)
    out = peer_block(h, ffn_norm_w[0], final_norm_w, peer_wq[0], peer_subkeys[0], peer_u[0], peer_v[0])
    return out.reshape(B, S, D)
```

```python
import functools
import math

import jax
import jax.numpy as jnp
import numpy as np
from jax import lax
from jax.experimental import pallas as pl
from jax.experimental.pallas import tpu as pltpu

F32 = jnp.float32
BF16 = jnp.bfloat16

HEAD_DIM = 128
N_KV_HEADS = 2
GQA = 4
N_Q_HEADS = N_KV_HEADS * GQA
L_CMP = 32
STRIDE_CMP = 16
L_SEL = 64
N_SEL = 16
WINDOW = 512
Q_BLOCK = 128
FORCED_SCORE = float(GQA + 1)
NUM_BUCKETS = 32
MAX_DISTANCE = 128
PEER_HEADS = 8
N_KEYS = 128
PEER_TOPK = 16
EPS = 1e-6
NEG_BIG = -1e30
SEL_OFF = -float(2 ** 30)
VMEM_LIMIT = 56 * 1024 * 1024


def _dot(a, b):
    return jnp.dot(a, b, preferred_element_type=F32)


def _dot_nt(a, b):
    return lax.dot_general(a, b, (((1,), (1,)), ((), ())), preferred_element_type=F32)


def _params(sem, vmem=VMEM_LIMIT):
    return pltpu.CompilerParams(dimension_semantics=sem, vmem_limit_bytes=vmem)


def _norm_matmul_kernel(x_ref, nw_ref, w_ref, o_ref, xn_ref):
    @pl.when(pl.program_id(1) == 0)
    def _():
        x = x_ref[...]
        y = x * lax.rsqrt(jnp.mean(x * x, axis=-1, keepdims=True) + EPS)
        xn_ref[...] = (y * nw_ref[...]).astype(BF16)

    o_ref[...] = _dot(xn_ref[...], w_ref[...]).astype(o_ref.dtype)


def norm_matmul(x, norm_w, w, out_dtype, tm, tn):
    T, D = x.shape
    N = w.shape[1]
    return pl.pallas_call(
        _norm_matmul_kernel,
        grid=(T // tm, N // tn),
        in_specs=[pl.BlockSpec((tm, D), lambda i, j: (i, 0)),
                  pl.BlockSpec((1, D), lambda i, j: (0, 0)),
                  pl.BlockSpec((D, tn), lambda i, j: (0, j))],
        out_specs=pl.BlockSpec((tm, tn), lambda i, j: (i, j)),
        out_shape=jax.ShapeDtypeStruct((T, N), out_dtype),
        scratch_shapes=[pltpu.VMEM((tm, D), BF16)],
        compiler_params=_params(("parallel", "arbitrary")),
    )(x, norm_w.reshape(1, D), w)


def _compress_kernel(k_ref, v_ref, pa_ref, pb_ref, wka_ref, wkb_ref, wva_ref, wvb_ref, kc_ref, vc_ref):
    def one(x_ref, wa_ref, wb_ref, o_ref):
        x = x_ref[...].astype(F32)
        a = _dot((x + pa_ref[...]).astype(BF16), wa_ref[...])
        b = _dot((x + pb_ref[...]).astype(BF16), wb_ref[...])
        n = b.shape[0]
        o_ref[...] = (a + pltpu.roll(b, n - 1, 0)).astype(o_ref.dtype)

    one(k_ref, wka_ref, wkb_ref, kc_ref)
    one(v_ref, wva_ref, wvb_ref, vc_ref)


def compress(k16, v16, pos_a, pos_b, wka, wkb, wva, wvb, B):
    R, C = k16.shape
    nb = R // B
    dkv = wka.shape[1]
    full = lambda shp: pl.BlockSpec(shp, lambda b: (0, 0))
    return pl.pallas_call(
        _compress_kernel,
        grid=(B,),
        in_specs=[pl.BlockSpec((nb, C), lambda b: (b, 0)),
                  pl.BlockSpec((nb, C), lambda b: (b, 0)),
                  full((1, C)), full((1, C)),
                  full((C, dkv)), full((C, dkv)), full((C, dkv)), full((C, dkv))],
        out_specs=[pl.BlockSpec((nb, dkv), lambda b: (b, 0))] * 2,
        out_shape=[jax.ShapeDtypeStruct((R, dkv), BF16)] * 2,
        compiler_params=_params(("parallel",)),
    )(k16, v16, pos_a, pos_b, wka, wkb, wva, wvb)


def _bias_from_dist(tab_row, dist):
    idx = jnp.clip(dist, 0, 127)
    w = tab_row.shape[1]
    tab = jnp.broadcast_to(tab_row, (idx.shape[0], w))
    parts = [jnp.take_along_axis(tab, idx[:, k:k + w], axis=1) for k in range(0, idx.shape[1], w)]
    return parts[0] if len(parts) == 1 else jnp.concatenate(parts, axis=1)


def _cmp_select_kernel(q_ref, kc_ref, vc_ref, tab_ref, ovt_ref, ocmp_ref, selb_ref, *, n_qb, scale):
    c = pl.program_id(0) % n_qb
    tq = q_ref.shape[0]
    n_c = kc_ref.shape[0]
    n_sel = ovt_ref.shape[0]
    t0 = c * tq
    t_col = t0 + lax.broadcasted_iota(jnp.int32, (tq, n_c), 0)
    n_row = lax.broadcasted_iota(jnp.int32, (tq, n_c), 1)
    dist = t_col - (n_row * STRIDE_CMP + (L_CMP - 1))
    valid = dist >= 0

    j_io = lax.broadcasted_iota(jnp.int32, (n_sel, tq), 0)
    t_io = t0 + lax.broadcasted_iota(jnp.int32, (n_sel, tq), 1)
    blk_t = t_io // L_SEL
    forced = (j_io == 0) | (j_io == blk_t) | (j_io == blk_t - 1)
    causal_blk = j_io * L_SEL <= t_io

    for h in range(N_KV_HEADS):
        kc = kc_ref[:, h * HEAD_DIM:(h + 1) * HEAD_DIM]
        vc = vc_ref[:, h * HEAD_DIM:(h + 1) * HEAD_DIM]
        psum = jnp.zeros((tq, n_c), F32)
        for g in range(GQA):
            hd = h * GQA + g
            qh = q_ref[:, hd * HEAD_DIM:(hd + 1) * HEAD_DIM]
            bias = _bias_from_dist(tab_ref[hd:hd + 1, :], dist)
            s = _dot_nt(qh, kc) * scale + bias
            s = jnp.where(valid, s, NEG_BIG)
            m = jnp.max(s, axis=-1, keepdims=True)
            e = jnp.where(valid, jnp.exp(s - m), 0.0)
            d = jnp.sum(e, axis=-1, keepdims=True)
            p = e / jnp.where(d > 0, d, 1.0)
            ocmp_ref[:, hd * HEAD_DIM:(hd + 1) * HEAD_DIM] = _dot(p.astype(BF16), vc)
            psum = psum + p
        p_hi = psum.astype(BF16)
        p_lo = (psum - p_hi.astype(F32)).astype(BF16)
        ovt = ovt_ref[...]
        imp = _dot_nt(ovt, p_hi) + _dot_nt(ovt, p_lo)
        score = jnp.where(forced, FORCED_SCORE, jnp.where(causal_blk, imp, -1.0))
        rank = jnp.zeros((n_sel, tq), F32)
        for jp in range(n_sel):
            row = score[jp:jp + 1, :]
            rank = rank + jnp.where(j_io > jp, jnp.where(row >= score, 1.0, 0.0),
                                    jnp.where(row > score, 1.0, 0.0))
        selb = jnp.where(rank < float(N_SEL), 0.0, SEL_OFF)
        if n_sel < HEAD_DIM:
            selb = jnp.concatenate([selb, jnp.zeros((HEAD_DIM - n_sel, tq), F32)], axis=0)
        selb_ref[:, h * HEAD_DIM:(h + 1) * HEAD_DIM] = selb.T.astype(BF16)


def cmp_select(qkv, kc, vc, tab, ovt, B, S):
    T = qkv.shape[0]
    tq = Q_BLOCK
    n_qb = S // tq
    n_c = kc.shape[0] // B
    dq = N_Q_HEADS * HEAD_DIM
    dkv = N_KV_HEADS * HEAD_DIM
    kern = functools.partial(_cmp_select_kernel, n_qb=n_qb, scale=HEAD_DIM ** -0.5)
    return pl.pallas_call(
        kern,
        grid=(T // tq,),
        in_specs=[pl.BlockSpec((tq, dq), lambda i: (i, 0)),
                  pl.BlockSpec((n_c, dkv), lambda i: (i // n_qb, 0)),
                  pl.BlockSpec((n_c, dkv), lambda i: (i // n_qb, 0)),
                  pl.BlockSpec(tab.shape, lambda i: (0, 0)),
                  pl.BlockSpec(ovt.shape, lambda i: (0, 0))],
        out_specs=[pl.BlockSpec((tq, dq), lambda i: (i, 0)),
                   pl.BlockSpec((tq, dkv), lambda i: (i, 0))],
        out_shape=[jax.ShapeDtypeStruct((T, dq), F32),
                   jax.ShapeDtypeStruct((T, dkv), BF16)],
        compiler_params=_params(("parallel",)),
    )(qkv, kc, vc, tab, ovt)


def _flash_chunk(k, vt, q, bias, m_ref, l_ref, acc_ref, scale):
    s = _dot_nt(k, q) * scale + bias
    m_old = m_ref[...]
    m_new = jnp.maximum(m_old, jnp.max(s, axis=0, keepdims=True))
    alpha = jnp.exp(m_old - m_new)
    p = jnp.exp(s - m_new)
    l_ref[...] = alpha * l_ref[...] + jnp.sum(p, axis=0, keepdims=True)
    acc_ref[...] = alpha * acc_ref[...] + _dot(vt, p.astype(BF16))
    m_ref[...] = m_new


def _sel_win_kernel(q_ref, selb_ref, ks_ref, vst_ref, kw_ref, vwt_ref, e2_ref, tab_ref, ocmp_ref,
                    gl_ref, gnw_ref, o_ref, m_ref, l_ref, acc_ref, *, n_qb, scale):
    c = pl.program_id(0) % n_qb
    tq = q_ref.shape[0]
    rows = GQA * tq
    n_back = WINDOW // tq
    near = WINDOW + tq
    j_io = lax.broadcasted_iota(jnp.int32, (tq, tq), 0)
    i_io = lax.broadcasted_iota(jnp.int32, (tq, tq), 1)
    dij = i_io - j_io
    dij4 = jnp.concatenate([dij] * GQA, axis=1)
    causal = dij4 >= 0
    in_seq = lax.broadcasted_iota(jnp.int32, (near, rows), 0) >= WINDOW - c * tq
    sig = jax.nn.sigmoid(gl_ref[...])
    near0 = pl.multiple_of(c * tq, tq)
    n_far = jnp.maximum(c - n_back, 0)
    n_full = n_far // n_back
    n_rem = n_far - n_full * n_back

    def hsl(h):
        return slice(h * HEAD_DIM, (h + 1) * HEAD_DIM)

    def far0(i):
        return pl.multiple_of(WINDOW + i * WINDOW, WINDOW)

    def reset():
        m_ref[...] = jnp.full(m_ref.shape, NEG_BIG, F32)
        l_ref[...] = jnp.zeros(l_ref.shape, F32)
        acc_ref[...] = jnp.zeros(acc_ref.shape, F32)

    def result():
        ot = acc_ref[...] / l_ref[...]
        return [ot[:, g * tq:(g + 1) * tq].T for g in range(GQA)]

    for h in range(N_KV_HEADS):
        q4 = jnp.concatenate([q_ref[:, (h * GQA + g) * HEAD_DIM:(h * GQA + g + 1) * HEAD_DIM]
                              for g in range(GQA)], axis=0)
        sb = selb_ref[:, hsl(h)]
        q_aug = jnp.concatenate([q4, jnp.concatenate([sb] * GQA, axis=0)], axis=1)
        tabs = [tab_ref[h * GQA + g:h * GQA + g + 1, :] for g in range(GQA)]
        d0 = jnp.concatenate([_bias_from_dist(t, dij) for t in tabs], axis=1)
        d1 = jnp.concatenate([_bias_from_dist(t, dij + tq) for t in tabs], axis=1)
        far = jnp.concatenate([t[:, HEAD_DIM - 1:HEAD_DIM] + jnp.zeros((1, tq), F32) for t in tabs], axis=1)
        far_t = jnp.broadcast_to(far, (tq, rows))
        diag = jnp.where(causal, d0, NEG_BIG)
        bias_sel = jnp.concatenate([far_t] * (n_back - 1) + [d1, diag], axis=0)
        bias_win = jnp.concatenate([jnp.where(dij4 < 0, far_t, NEG_BIG)] + [far_t] * (n_back - 2) + [d1, diag],
                                   axis=0)
        bias_sel = jnp.where(in_seq, bias_sel, NEG_BIG)
        bias_win = jnp.where(in_seq, bias_win, NEG_BIG)

        reset()
        k_near = jnp.concatenate([ks_ref[pl.ds(near0, near), hsl(h)], e2_ref[pl.ds(near0, near), :]], axis=1)
        _flash_chunk(k_near, vst_ref[hsl(h), pl.ds(near0, near)], q_aug, bias_sel, m_ref, l_ref, acc_ref, scale)

        def far_chunk(i, bias):
            r0 = far0(i)
            k = jnp.concatenate([ks_ref[pl.ds(r0, WINDOW), hsl(h)], e2_ref[pl.ds(r0, WINDOW), :]], axis=1)
            _flash_chunk(k, vst_ref[hsl(h), pl.ds(r0, WINDOW)], q_aug, bias, m_ref, l_ref, acc_ref, scale)

        def body(i, carry):
            far_chunk(i, far)
            return carry

        lax.fori_loop(0, n_full, body, 0)

        @pl.when(n_rem > 0)
        def _():
            live = lax.broadcasted_iota(jnp.int32, (WINDOW, rows), 0) < n_rem * tq
            far_chunk(n_full, jnp.where(live, jnp.broadcast_to(far, (WINDOW, rows)), NEG_BIG))

        o_sel = result()

        reset()
        _flash_chunk(kw_ref[pl.ds(near0, near), hsl(h)], vwt_ref[hsl(h), pl.ds(near0, near)], q4, bias_win,
                     m_ref, l_ref, acc_ref, scale)
        o_win = result()

        for g in range(GQA):
            hd = h * GQA + g
            o = (sig[:, 3 * hd:3 * hd + 1] * ocmp_ref[:, hsl(hd)]
                 + sig[:, 3 * hd + 1:3 * hd + 2] * o_sel[g]
                 + sig[:, 3 * hd + 2:3 * hd + 3] * o_win[g])
            y = o * lax.rsqrt(jnp.mean(o * o, axis=-1, keepdims=True) + EPS)
            o_ref[:, hsl(hd)] = (y * gnw_ref[:, hsl(hd)]).astype(o_ref.dtype)


def sel_win(qkv, selb, ksp, vstp, kwp, vwtp, e2p, tab, ocmp, gl, gnw, B, S, tq=2 * Q_BLOCK):
    T = qkv.shape[0]
    n_qb = S // tq
    dq = N_Q_HEADS * HEAD_DIM
    dkv = N_KV_HEADS * HEAD_DIM
    SP = S + WINDOW
    kern = functools.partial(_sel_win_kernel, n_qb=n_qb, scale=HEAD_DIM ** -0.5)
    k_spec = pl.BlockSpec((SP, dkv), lambda i: (i // n_qb, 0))
    vt_spec = pl.BlockSpec((dkv, SP), lambda i: (i // n_qb, 0))
    return pl.pallas_call(
        kern,
        grid=(T // tq,),
        in_specs=[pl.BlockSpec((tq, dq), lambda i: (i, 0)),
                  pl.BlockSpec((tq, dkv), lambda i: (i, 0)),
                  k_spec, vt_spec, k_spec, vt_spec,
                  pl.BlockSpec((SP, HEAD_DIM), lambda i: (0, 0)),
                  pl.BlockSpec(tab.shape, lambda i: (0, 0)),
                  pl.BlockSpec((tq, dq), lambda i: (i, 0)),
                  pl.BlockSpec((tq, HEAD_DIM), lambda i: (i, 0)),
                  pl.BlockSpec((1, dq), lambda i: (0, 0))],
        out_specs=pl.BlockSpec((tq, dq), lambda i: (i, 0)),
        out_shape=jax.ShapeDtypeStruct((T, dq), BF16),
        scratch_shapes=[pltpu.VMEM((1, GQA * tq), F32), pltpu.VMEM((1, GQA * tq), F32),
                        pltpu.VMEM((HEAD_DIM, GQA * tq), F32)],
        compiler_params=_params(("parallel",)),
    )(qkv, selb, ksp, vstp, kwp, vwtp, e2p, tab, ocmp, gl, gnw)


def _conv_kernel(b_ref, c_ref, h_ref, cp_ref, hp_ref, cw_ref, cb_ref, gnw_ref, o_ref, u_ref, *, tiles_per_seq):
    tm = b_ref.shape[0]
    first = (pl.program_id(0) % tiles_per_seq) == 0
    u_prev = cp_ref[...] * hp_ref[...]
    u_ref[0:8, :] = jnp.where(first, 0.0, u_prev)
    u = c_ref[...] * h_ref[...]
    u_ref[8:8 + tm, :] = u
    y = (cw_ref[0:1, :] * u_ref[6:6 + tm, :] + cw_ref[1:2, :] * u_ref[7:7 + tm, :]
         + cw_ref[2:3, :] * u + cb_ref[...])
    o = b_ref[...] * y
    n_groups = o.shape[1] // HEAD_DIM
    for g in range(n_groups):
        sl = slice(g * HEAD_DIM, (g + 1) * HEAD_DIM)
        og = o[:, sl]
        yg = og * lax.rsqrt(jnp.mean(og * og, axis=-1, keepdims=True) + EPS)
        o_ref[:, sl] = (yg * gnw_ref[:, sl]).astype(o_ref.dtype)


def conv_mixer(gc, conv_w, conv_b, gnw, S, tm):
    T = gc.shape[0]
    dc = conv_w.shape[1]
    tps = S // tm
    kern = functools.partial(_conv_kernel, tiles_per_seq=tps)
    prev = lambda col: pl.BlockSpec((8, dc), lambda i, col=col: (jnp.maximum(i * (tm // 8) - 1, 0), col))
    cur = lambda col: pl.BlockSpec((tm, dc), lambda i, col=col: (i, col))
    return pl.pallas_call(
        kern,
        grid=(T // tm,),
        in_specs=[cur(0), cur(1), cur(2), prev(1), prev(2),
                  pl.BlockSpec((8, dc), lambda i: (0, 0)),
                  pl.BlockSpec((1, dc), lambda i: (0, 0)),
                  pl.BlockSpec((1, dc), lambda i: (0, 0))],
        out_specs=pl.BlockSpec((tm, dc), lambda i: (i, 0)),
        out_shape=jax.ShapeDtypeStruct((T, dc), BF16),
        scratch_shapes=[pltpu.VMEM((tm + 8, dc), F32)],
        compiler_params=_params(("parallel",)),
    )(gc, gc, gc, gc, gc, conv_w, conv_b, gnw)


def _out_proj_kernel(ma_ref, mc_ref, wa_ref, wc_ref, x_ref, o_ref):
    o_ref[...] = x_ref[...] + _dot(ma_ref[...], wa_ref[...]) + _dot(mc_ref[...], wc_ref[...])


def out_proj(ma, mc, wa, wc, x, tm, tn):
    T, da = ma.shape
    dc = mc.shape[1]
    D = x.shape[1]
    return pl.pallas_call(
        _out_proj_kernel,
        grid=(T // tm, D // tn),
        in_specs=[pl.BlockSpec((tm, da), lambda i, j: (i, 0)),
                  pl.BlockSpec((tm, dc), lambda i, j: (i, 0)),
                  pl.BlockSpec((da, tn), lambda i, j: (0, j)),
                  pl.BlockSpec((dc, tn), lambda i, j: (0, j)),
                  pl.BlockSpec((tm, tn), lambda i, j: (i, j))],
        out_specs=pl.BlockSpec((tm, tn), lambda i, j: (i, j)),
        out_shape=jax.ShapeDtypeStruct((T, D), F32),
        compiler_params=_params(("parallel", "arbitrary")),
    )(ma, mc, wa, wc, x)


def _peer_route_kernel(q_ref, sk_ref, e_ref, g_ref, sv_ref, si_ref):
    tt = q_ref.shape[0]
    K = PEER_TOPK
    n_io = lax.broadcasted_iota(jnp.int32, (N_KEYS, tt), 0).astype(F32)
    r16 = lax.broadcasted_iota(jnp.int32, (16, tt), 0).astype(F32)
    r8 = lax.broadcasted_iota(jnp.int32, (8, tt), 0).astype(F32)
    ninf = -jnp.inf

    for h in range(PEER_HEADS):
        for c in range(2):
            col = (h * 2 + c) * HEAD_DIM
            s = _dot_nt(sk_ref[h * 2 + c], q_ref[:, col:col + HEAD_DIM])
            for k in range(K):
                m = jnp.max(s, axis=0, keepdims=True)
                idx = jnp.min(jnp.where(s == m, n_io, float(N_KEYS)), axis=0, keepdims=True)
                s = jnp.where(n_io == idx, ninf, s)
                sv_ref[c, k:k + 1, :] = m
                si_ref[c, k:k + 1, :] = idx
        sv0, sv1 = sv_ref[0], sv_ref[1]
        si0, si1 = si_ref[0], si_ref[1]
        cands, flats, exps = [], [], []
        for a, nb, rows in ((0, 16, 16), (1, 8, 8), (2, 5, 8), (3, 4, 8)):
            r_io = r16 if rows == 16 else r8
            val = sv0[a:a + 1, :] + sv1[0:rows, :]
            cands.append(jnp.where(r_io < nb, val, ninf))
            flats.append(a * 16.0 + r_io)
            exps.append(si0[a:a + 1, :] * float(N_KEYS) + si1[0:rows, :])
        for b, lo, hi, rows in ((0, 4, 16, 16), (1, 4, 8, 8), (2, 4, 5, 8)):
            r_io = r16 if rows == 16 else r8
            val = sv0[0:rows, :] + sv1[b:b + 1, :]
            cands.append(jnp.where((r_io >= lo) & (r_io < hi), val, ninf))
            flats.append(r_io * 16.0 + b)
            exps.append(si0[0:rows, :] * float(N_KEYS) + si1[b:b + 1, :])
        cand = jnp.concatenate(cands, axis=0)
        flat = jnp.concatenate(flats, axis=0)
        expt = jnp.concatenate(exps, axis=0)
        cvs, exs = [], []
        for k in range(K):
            m = jnp.max(cand, axis=0, keepdims=True)
            fsel = jnp.min(jnp.where(cand == m, flat, 1e9), axis=0, keepdims=True)
            hit = flat == fsel
            exs.append(jnp.max(jnp.where(hit, expt, -1.0), axis=0, keepdims=True))
            cand = jnp.where(hit, ninf, cand)
            cvs.append(m)
        cv = jnp.concatenate(cvs, axis=0)
        ex = jnp.concatenate(exs, axis=0)
        ev = jnp.exp(cv - jnp.max(cv, axis=0, keepdims=True))
        gates = ev / jnp.sum(ev, axis=0, keepdims=True)
        e_ref[0, h * K:(h + 1) * K, :] = ex.astype(jnp.int32)
        g_ref[0, h * K:(h + 1) * K, :] = gates


def peer_route(qp, subkeys):
    T = qp.shape[0]
    tt = 128
    P = PEER_HEADS * PEER_TOPK
    nt = T // tt
    return pl.pallas_call(
        _peer_route_kernel,
        grid=(nt,),
        in_specs=[pl.BlockSpec((tt, qp.shape[1]), lambda i: (i, 0)),
                  pl.BlockSpec(subkeys.shape, lambda i: (0, 0, 0))],
        out_specs=[pl.BlockSpec((1, P, tt), lambda i: (i, 0, 0))] * 2,
        out_shape=[jax.ShapeDtypeStruct((nt, P, tt), jnp.int32),
                   jax.ShapeDtypeStruct((nt, P, tt), F32)],
        scratch_shapes=[pltpu.VMEM((2, PEER_TOPK, tt), F32), pltpu.VMEM((2, PEER_TOPK, tt), F32)],
        compiler_params=_params(("parallel",)),
    )(qp, subkeys)


def _peer_gbuild_kernel(e_ref, g_ref, o_ref, i1_ref, i2_ref, gt_ref, s_ref):
    tt = e_ref.shape[2]
    e = e_ref[0].T
    i1_ref[...] = e >> 7
    i2_ref[...] = e & (N_KEYS - 1)
    gt_ref[...] = g_ref[0].T
    P = e.shape[1]
    k_io = lax.broadcasted_iota(jnp.int32, (N_KEYS, P), 0)
    group = 64
    sub = 8
    def body(tg, carry):
        base = pl.multiple_of(tg * group, group)
        r1s = i1_ref[pl.ds(base, group), :]
        r2s = i2_ref[pl.ds(base, group), :]
        rgs = gt_ref[pl.ds(base, group), :]
        for u in range(group):
            lhs = jnp.where(k_io == r1s[u:u + 1, :], rgs[u:u + 1, :], 0.0).astype(BF16)
            rhs = jnp.where(k_io == r2s[u:u + 1, :], 1.0, 0.0).astype(BF16)
            g_t = _dot_nt(lhs, rhs)
            for a in range(N_KEYS // sub):
                row0 = pl.multiple_of((a * tt + base + u) * sub, sub)
                s_ref[pl.ds(row0, sub), :] = g_t[a * sub:(a + 1) * sub, :]
        return carry

    lax.fori_loop(0, tt // group, body, 0)
    for i1 in range(N_KEYS):
        a, r = divmod(i1, sub)
        o_ref[i1] = s_ref[pl.ds(a * tt * sub + r, tt, stride=sub), :].astype(o_ref.dtype)


def peer_gbuild(ex, gates):
    nt, P, tt = ex.shape
    T = nt * tt
    return pl.pallas_call(
        _peer_gbuild_kernel,
        grid=(nt,),
        in_specs=[pl.BlockSpec((1, P, tt), lambda i: (i, 0, 0))] * 2,
        out_specs=pl.BlockSpec((N_KEYS, tt, N_KEYS), lambda i: (0, i, 0)),
        out_shape=jax.ShapeDtypeStruct((N_KEYS, T, N_KEYS), BF16),
        scratch_shapes=[pltpu.VMEM((tt, P), jnp.int32), pltpu.VMEM((tt, P), jnp.int32),
                        pltpu.VMEM((tt, P), F32), pltpu.VMEM((tt * N_KEYS, N_KEYS), F32)],
        compiler_params=_params(("parallel",)),
    )(ex, gates)


def _peer_dense_kernel(h_ref, nw_ref, fw_ref, u_ref, v_ref, gm_ref, o_ref, hn_ref, acc_ref):
    j = pl.program_id(1)

    @pl.when(j == 0)
    def _():
        x = h_ref[...]
        y = x * lax.rsqrt(jnp.mean(x * x, axis=-1, keepdims=True) + EPS)
        hn_ref[...] = (y * nw_ref[...]).astype(BF16)
        acc_ref[...] = jnp.zeros(acc_ref.shape, F32)

    a = _dot_nt(hn_ref[...], u_ref[...])
    gm = jnp.concatenate([gm_ref[k] for k in range(gm_ref.shape[0])], axis=1)
    w = (gm.astype(F32) * jax.nn.gelu(a)).astype(BF16)
    acc_ref[...] += _dot(w, v_ref[...])

    @pl.when(j == pl.num_programs(1) - 1)
    def _():
        x = h_ref[...] + acc_ref[...]
        y = x * lax.rsqrt(jnp.mean(x * x, axis=-1, keepdims=True) + EPS)
        o_ref[...] = y * fw_ref[...]


def peer_dense(h, ffn_nw, final_nw, u, v, gm, tt, ec):
    T, D = h.shape
    E = u.shape[0]
    return pl.pallas_call(
        _peer_dense_kernel,
        grid=(T // tt, E // ec),
        in_specs=[pl.BlockSpec((tt, D), lambda i, j: (i, 0)),
                  pl.BlockSpec((1, D), lambda i, j: (0, 0)),
                  pl.BlockSpec((1, D), lambda i, j: (0, 0)),
                  pl.BlockSpec((ec, D), lambda i, j: (j, 0)),
                  pl.BlockSpec((ec, D), lambda i, j: (j, 0)),
                  pl.BlockSpec((ec // N_KEYS, tt, N_KEYS), lambda i, j: (j, i, 0))],
        out_specs=pl.BlockSpec((tt, D), lambda i, j: (i, 0)),
        out_shape=jax.ShapeDtypeStruct((T, D), F32),
        scratch_shapes=[pltpu.VMEM((tt, D), BF16), pltpu.VMEM((tt, D), F32)],
        compiler_params=_params(("parallel", "arbitrary")),
    )(h, ffn_nw.reshape(1, D), final_nw.reshape(1, D), u, v, gm)


def _t5_bucket_np(n_dist):
    d = np.arange(n_dist)
    max_exact = NUM_BUCKETS // 2
    nf = np.maximum(d, 1).astype(np.float64)
    large = max_exact + (np.log(nf / max_exact) / math.log(MAX_DISTANCE / max_exact)
                         * (NUM_BUCKETS - max_exact)).astype(np.int64)
    large = np.minimum(large, NUM_BUCKETS - 1)
    return np.where(d < max_exact, d, large).astype(np.int32)


def _overlap_t_np(S):
    n_c = (S - L_CMP) // STRIDE_CMP + 1
    n_sel = S // L_SEL
    pos = np.arange(n_c)[:, None] * STRIDE_CMP + np.arange(L_CMP)[None, :]
    m = np.zeros((n_c + 1, n_sel), np.float32)
    np.add.at(m, (np.repeat(np.arange(n_c), L_CMP), (pos // L_SEL).reshape(-1)), 1.0 / L_CMP)
    return np.ascontiguousarray(m.T)


def _block_onehot_np(S):
    e2 = np.zeros((S, HEAD_DIM), np.float32)
    e2[np.arange(S), np.arange(S) // L_SEL] = 1.0
    return e2


def nsa_conv_mix(xt, B, S, attn_norm_w, w_in, w_cmp_k, w_cmp_v, cmp_pos, conv_w, conv_b,
                 attn_gnw, conv_gnw, rel_bias, tm=512):
    T, D = xt.shape
    dq = N_Q_HEADS * HEAD_DIM
    dkv = N_KV_HEADS * HEAD_DIM
    n_attn = dq + 6 * dkv
    n_gate = 3 * N_Q_HEADS
    dc = (w_in.shape[1] - n_attn - n_gate) // 3
    w_attn = w_in[:, :n_attn].astype(BF16)
    w_gate = jnp.pad(w_in[:, n_attn:n_attn + n_gate], ((0, 0), (0, HEAD_DIM - n_gate))).astype(BF16)
    w_conv = w_in[:, n_attn + n_gate:].astype(BF16)

    qkv = norm_matmul(xt, attn_norm_w, w_attn, BF16, tm, n_attn // 2)
    gl = norm_matmul(xt, attn_norm_w, w_gate, F32, tm, HEAD_DIM)
    gc = norm_matmul(xt, attn_norm_w, w_conv, F32, tm, dc)

    n16 = S // STRIDE_CMP
    k16 = qkv[:, dq:dq + dkv].reshape(B * n16, STRIDE_CMP * dkv)
    v16 = qkv[:, dq + dkv:dq + 2 * dkv].reshape(B * n16, STRIDE_CMP * dkv)

    def wbig(w, lo):
        wl = w[lo:lo + STRIDE_CMP]
        eye = jnp.eye(N_KV_HEADS, dtype=w.dtype)
        return jnp.einsum('lde,hg->lhdge', wl, eye).reshape(STRIDE_CMP * dkv, dkv).astype(BF16)

    def posrow(lo):
        p = cmp_pos[lo:lo + STRIDE_CMP]
        return jnp.broadcast_to(p[:, None, :], (STRIDE_CMP, N_KV_HEADS, HEAD_DIM)).reshape(1, STRIDE_CMP * dkv)

    kc, vc = compress(k16, v16, posrow(0), posrow(STRIDE_CMP),
                      wbig(w_cmp_k, 0), wbig(w_cmp_k, STRIDE_CMP),
                      wbig(w_cmp_v, 0), wbig(w_cmp_v, STRIDE_CMP), B)

    tab = rel_bias[_t5_bucket_np(HEAD_DIM)].T
    ovt = jnp.asarray(_overlap_t_np(S), BF16)
    e2 = jnp.asarray(_block_onehot_np(S), BF16)

    ocmp, selb = cmp_select(qkv, kc, vc, tab, ovt, B, S)
    def keys_padded(col):
        k = qkv[:, col:col + dkv].reshape(B, S, dkv)
        return jnp.pad(k, ((0, 0), (WINDOW, 0), (0, 0))).reshape(B * (WINDOW + S), dkv)

    def values_t_padded(col):
        v = qkv[:, col:col + dkv].reshape(B, S, dkv).transpose(0, 2, 1)
        return jnp.pad(v, ((0, 0), (0, 0), (WINDOW, 0))).reshape(B * dkv, WINDOW + S)

    e2p = jnp.pad(e2, ((WINDOW, 0), (0, 0)))
    attn = sel_win(qkv, selb, keys_padded(dq + 2 * dkv), values_t_padded(dq + 3 * dkv),
                   keys_padded(dq + 4 * dkv), values_t_padded(dq + 5 * dkv), e2p, tab, ocmp, gl,
                   attn_gnw.reshape(1, dq), B, S)
    cw8 = jnp.pad(conv_w, ((0, 8 - conv_w.shape[0]), (0, 0)))
    conv = conv_mixer(gc, cw8, conv_b.reshape(1, dc), conv_gnw.reshape(1, dc), S, tm)
    return attn, conv


def peer_block(h, ffn_nw, final_nw, peer_wq, peer_subkeys, peer_u, peer_v, tm=512, tt=512, ec=1024):
    T, D = h.shape
    qp = norm_matmul(h, ffn_nw, peer_wq.astype(BF16), BF16, tm, 1024)
    sk = peer_subkeys.reshape(PEER_HEADS * 2, N_KEYS, peer_subkeys.shape[-1]).astype(BF16)
    ex, gates = peer_route(qp, sk)
    gm = peer_gbuild(ex, gates)
    return peer_dense(h, ffn_nw, final_nw, peer_u.astype(BF16), peer_v.astype(BF16), gm, tt, ec)


def kernel(x, attn_norm_w, w_in, w_cmp_k, w_cmp_v, cmp_pos, conv_w, conv_b, attn_group_norm_w,
           conv_group_norm_w, w_out, rel_bias, ffn_norm_w, peer_wq, peer_subkeys, peer_u, peer_v,
           final_norm_w):
    B, S, D = x.shape
    T = B * S
    xt = x.reshape(T, D)
    attn, conv = nsa_conv_mix(xt, B, S, attn_norm_w[0], w_in[0], w_cmp_k[0], w_cmp_v[0], cmp_pos[0],
                              conv_w[0], conv_b[0], attn_group_norm_w[0], conv_group_norm_w[0], rel_bias)
    da = attn.shape[1]
    wo = w_out[0].astype(BF16)
    h = out_proj(attn, conv, wo[:da], wo[da:], xt, 512, 1024)
    out = peer_block(h, ffn_norm_w[0], final_norm_w, peer_wq[0], peer_subkeys[0], peer_u[0], peer_v[0])
    return out.reshape(B, S, D)
```

```python
import functools
import math

import jax
import jax.numpy as jnp
import numpy as np
from jax import lax
from jax.experimental import pallas as pl
from jax.experimental.pallas import tpu as pltpu

F32 = jnp.float32
BF16 = jnp.bfloat16

HEAD_DIM = 128
N_KV_HEADS = 2
GQA = 4
N_Q_HEADS = N_KV_HEADS * GQA
L_CMP = 32
STRIDE_CMP = 16
L_SEL = 64
N_SEL = 16
WINDOW = 512
Q_BLOCK = 128
FORCED_SCORE = float(GQA + 1)
NUM_BUCKETS = 32
MAX_DISTANCE = 128
PEER_HEADS = 8
N_KEYS = 128
PEER_TOPK = 16
EPS = 1e-6
NEG_BIG = -1e30
SEL_OFF = -float(2 ** 30)
VMEM_LIMIT = 56 * 1024 * 1024


def _dot(a, b):
    return jnp.dot(a, b, preferred_element_type=F32)


def _dot_nt(a, b):
    return lax.dot_general(a, b, (((1,), (1,)), ((), ())), preferred_element_type=F32)


def _params(sem, vmem=VMEM_LIMIT):
    return pltpu.CompilerParams(dimension_semantics=sem, vmem_limit_bytes=vmem)


def _norm_matmul_kernel(x_ref, nw_ref, w_ref, o_ref, xn_ref):
    @pl.when(pl.program_id(1) == 0)
    def _():
        x = x_ref[...]
        y = x * lax.rsqrt(jnp.mean(x * x, axis=-1, keepdims=True) + EPS)
        xn_ref[...] = (y * nw_ref[...]).astype(BF16)

    o_ref[...] = _dot(xn_ref[...], w_ref[...]).astype(o_ref.dtype)


def norm_matmul(x, norm_w, w, out_dtype, tm, tn):
    T, D = x.shape
    N = w.shape[1]
    return pl.pallas_call(
        _norm_matmul_kernel,
        grid=(T // tm, N // tn),
        in_specs=[pl.BlockSpec((tm, D), lambda i, j: (i, 0)),
                  pl.BlockSpec((1, D), lambda i, j: (0, 0)),
                  pl.BlockSpec((D, tn), lambda i, j: (0, j))],
        out_specs=pl.BlockSpec((tm, tn), lambda i, j: (i, j)),
        out_shape=jax.ShapeDtypeStruct((T, N), out_dtype),
        scratch_shapes=[pltpu.VMEM((tm, D), BF16)],
        compiler_params=_params(("parallel", "arbitrary")),
    )(x, norm_w.reshape(1, D), w)


def _in_proj_kernel(x_ref, nw_ref, wa_ref, wc_ref, wg_ref, oa_ref, oc_ref, og_ref, xn_ref, *, na, nc):
    j = pl.program_id(1)

    @pl.when(j == 0)
    def _():
        x = x_ref[...]
        y = x * lax.rsqrt(jnp.mean(x * x, axis=-1, keepdims=True) + EPS)
        xn_ref[...] = (y * nw_ref[...]).astype(BF16)

    @pl.when(j < na)
    def _():
        oa_ref[...] = _dot(xn_ref[...], wa_ref[...]).astype(oa_ref.dtype)

    @pl.when((j >= na) & (j < na + nc))
    def _():
        oc_ref[...] = _dot(xn_ref[...], wc_ref[...])

    @pl.when(j == na + nc)
    def _():
        og_ref[...] = _dot(xn_ref[...], wg_ref[...])


def in_proj(x, norm_w, w_attn, w_conv, w_gate, tm, tna, tnc):
    T, D = x.shape
    n_attn, n_conv, n_gate = w_attn.shape[1], w_conv.shape[1], w_gate.shape[1]
    na, nc = n_attn // tna, n_conv // tnc
    col_a = lambda i, j: jnp.minimum(j, na - 1)
    col_c = lambda i, j: jnp.clip(j - na, 0, nc - 1)
    kern = functools.partial(_in_proj_kernel, na=na, nc=nc)
    return pl.pallas_call(
        kern,
        grid=(T // tm, na + nc + 1),
        in_specs=[pl.BlockSpec((tm, D), lambda i, j: (i, 0)),
                  pl.BlockSpec((1, D), lambda i, j: (0, 0)),
                  pl.BlockSpec((D, tna), lambda i, j: (0, col_a(i, j))),
                  pl.BlockSpec((D, tnc), lambda i, j: (0, col_c(i, j))),
                  pl.BlockSpec((D, n_gate), lambda i, j: (0, 0))],
        out_specs=[pl.BlockSpec((tm, tna), lambda i, j: (i, col_a(i, j))),
                   pl.BlockSpec((tm, tnc), lambda i, j: (i, col_c(i, j))),
                   pl.BlockSpec((tm, n_gate), lambda i, j: (i, 0))],
        out_shape=[jax.ShapeDtypeStruct((T, n_attn), BF16),
                   jax.ShapeDtypeStruct((T, n_conv), F32),
                   jax.ShapeDtypeStruct((T, n_gate), F32)],
        scratch_shapes=[pltpu.VMEM((tm, D), BF16)],
        compiler_params=_params(("parallel", "arbitrary")),
    )(x, norm_w.reshape(1, D), w_attn, w_conv, w_gate)


def _compress_kernel(k_ref, v_ref, pa_ref, pb_ref, wka_ref, wkb_ref, wva_ref, wvb_ref, kc_ref, vc_ref):
    def one(x_ref, wa_ref, wb_ref, o_ref):
        x = x_ref[...].astype(F32)
        a = _dot((x + pa_ref[...]).astype(BF16), wa_ref[...])
        b = _dot((x + pb_ref[...]).astype(BF16), wb_ref[...])
        n = b.shape[0]
        o_ref[...] = (a + pltpu.roll(b, n - 1, 0)).astype(o_ref.dtype)

    one(k_ref, wka_ref, wkb_ref, kc_ref)
    one(v_ref, wva_ref, wvb_ref, vc_ref)


def compress(k16, v16, pos_a, pos_b, wka, wkb, wva, wvb, B):
    R, C = k16.shape
    nb = R // B
    dkv = wka.shape[1]
    full = lambda shp: pl.BlockSpec(shp, lambda b: (0, 0))
    return pl.pallas_call(
        _compress_kernel,
        grid=(B,),
        in_specs=[pl.BlockSpec((nb, C), lambda b: (b, 0)),
                  pl.BlockSpec((nb, C), lambda b: (b, 0)),
                  full((1, C)), full((1, C)),
                  full((C, dkv)), full((C, dkv)), full((C, dkv)), full((C, dkv))],
        out_specs=[pl.BlockSpec((nb, dkv), lambda b: (b, 0))] * 2,
        out_shape=[jax.ShapeDtypeStruct((R, dkv), BF16)] * 2,
        compiler_params=_params(("parallel",)),
    )(k16, v16, pos_a, pos_b, wka, wkb, wva, wvb)


def _bias_from_dist(tab_row, dist):
    idx = jnp.clip(dist, 0, 127)
    w = tab_row.shape[1]
    tab = jnp.broadcast_to(tab_row, (idx.shape[0], w))
    parts = [jnp.take_along_axis(tab, idx[:, k:k + w], axis=1) for k in range(0, idx.shape[1], w)]
    return parts[0] if len(parts) == 1 else jnp.concatenate(parts, axis=1)


def _cmp_select_kernel(q_ref, kc_ref, vc_ref, tab_ref, ovt_ref, ocmp_ref, selb_ref, *, n_qb, scale):
    c = pl.program_id(0) % n_qb
    tq = q_ref.shape[0]
    n_c = kc_ref.shape[0]
    n_sel = ovt_ref.shape[0]
    t0 = c * tq
    t_col = t0 + lax.broadcasted_iota(jnp.int32, (tq, n_c), 0)
    n_row = lax.broadcasted_iota(jnp.int32, (tq, n_c), 1)
    dist = t_col - (n_row * STRIDE_CMP + (L_CMP - 1))
    valid = dist >= 0

    j_io = lax.broadcasted_iota(jnp.int32, (n_sel, tq), 0)
    t_io = t0 + lax.broadcasted_iota(jnp.int32, (n_sel, tq), 1)
    blk_t = t_io // L_SEL
    forced = (j_io == 0) | (j_io == blk_t) | (j_io == blk_t - 1)
    causal_blk = j_io * L_SEL <= t_io

    for h in range(N_KV_HEADS):
        kc = kc_ref[:, h * HEAD_DIM:(h + 1) * HEAD_DIM]
        vc = vc_ref[:, h * HEAD_DIM:(h + 1) * HEAD_DIM]
        psum = jnp.zeros((tq, n_c), F32)
        for g in range(GQA):
            hd = h * GQA + g
            qh = q_ref[:, hd * HEAD_DIM:(hd + 1) * HEAD_DIM]
            bias = _bias_from_dist(tab_ref[hd:hd + 1, :], dist)
            s = _dot_nt(qh, kc) * scale + bias
            s = jnp.where(valid, s, NEG_BIG)
            m = jnp.max(s, axis=-1, keepdims=True)
            e = jnp.where(valid, jnp.exp(s - m), 0.0)
            d = jnp.sum(e, axis=-1, keepdims=True)
            p = e / jnp.where(d > 0, d, 1.0)
            ocmp_ref[:, hd * HEAD_DIM:(hd + 1) * HEAD_DIM] = _dot(p.astype(BF16), vc)
            psum = psum + p
        p_hi = psum.astype(BF16)
        p_lo = (psum - p_hi.astype(F32)).astype(BF16)
        ovt = ovt_ref[...]
        imp = _dot_nt(ovt, p_hi) + _dot_nt(ovt, p_lo)
        score = jnp.where(forced, FORCED_SCORE, jnp.where(causal_blk, imp, -1.0))
        rank = jnp.zeros((n_sel, tq), F32)
        for jp in range(n_sel):
            row = score[jp:jp + 1, :]
            rank = rank + jnp.where(j_io > jp, jnp.where(row >= score, 1.0, 0.0),
                                    jnp.where(row > score, 1.0, 0.0))
        selb = jnp.where(rank < float(N_SEL), 0.0, SEL_OFF)
        if n_sel < HEAD_DIM:
            selb = jnp.concatenate([selb, jnp.zeros((HEAD_DIM - n_sel, tq), F32)], axis=0)
        selb_ref[:, h * HEAD_DIM:(h + 1) * HEAD_DIM] = selb.T.astype(BF16)


def cmp_select(qkv, kc, vc, tab, ovt, B, S):
    T = qkv.shape[0]
    tq = Q_BLOCK
    n_qb = S // tq
    n_c = kc.shape[0] // B
    dq = N_Q_HEADS * HEAD_DIM
    dkv = N_KV_HEADS * HEAD_DIM
    kern = functools.partial(_cmp_select_kernel, n_qb=n_qb, scale=HEAD_DIM ** -0.5)
    return pl.pallas_call(
        kern,
        grid=(T // tq,),
        in_specs=[pl.BlockSpec((tq, dq), lambda i: (i, 0)),
                  pl.BlockSpec((n_c, dkv), lambda i: (i // n_qb, 0)),
                  pl.BlockSpec((n_c, dkv), lambda i: (i // n_qb, 0)),
                  pl.BlockSpec(tab.shape, lambda i: (0, 0)),
                  pl.BlockSpec(ovt.shape, lambda i: (0, 0))],
        out_specs=[pl.BlockSpec((tq, dq), lambda i: (i, 0)),
                   pl.BlockSpec((tq, dkv), lambda i: (i, 0))],
        out_shape=[jax.ShapeDtypeStruct((T, dq), F32),
                   jax.ShapeDtypeStruct((T, dkv), BF16)],
        compiler_params=_params(("parallel",)),
    )(qkv, kc, vc, tab, ovt)


def _flash_chunk(k, vt, q, bias, m_ref, l_ref, acc_ref, scale):
    s = _dot_nt(k, q) * scale + bias
    m_old = m_ref[...]
    m_new = jnp.maximum(m_old, jnp.max(s, axis=0, keepdims=True))
    alpha = jnp.exp(m_old - m_new)
    p = jnp.exp(s - m_new)
    l_ref[...] = alpha * l_ref[...] + jnp.sum(p, axis=0, keepdims=True)
    acc_ref[...] = alpha * acc_ref[...] + _dot(vt, p.astype(BF16))
    m_ref[...] = m_new


def _sel_win_kernel(q_ref, selb_ref, ks_ref, vst_ref, kw_ref, vwt_ref, e2_ref, tab_ref, ocmp_ref,
                    gl_ref, gnw_ref, o_ref, m_ref, l_ref, acc_ref, *, n_qb, scale):
    c = pl.program_id(0) % n_qb
    tq = q_ref.shape[0]
    rows = GQA * tq
    n_back = WINDOW // tq
    near = WINDOW + tq
    j_io = lax.broadcasted_iota(jnp.int32, (tq, tq), 0)
    i_io = lax.broadcasted_iota(jnp.int32, (tq, tq), 1)
    dij = i_io - j_io
    dij4 = jnp.concatenate([dij] * GQA, axis=1)
    causal = dij4 >= 0
    in_seq = lax.broadcasted_iota(jnp.int32, (near, rows), 0) >= WINDOW - c * tq
    sig = jax.nn.sigmoid(gl_ref[...])
    near0 = pl.multiple_of(c * tq, tq)
    n_far = jnp.maximum(c - n_back, 0)
    n_full = n_far // n_back
    n_rem = n_far - n_full * n_back

    def hsl(h):
        return slice(h * HEAD_DIM, (h + 1) * HEAD_DIM)

    def far0(i):
        return pl.multiple_of(WINDOW + i * WINDOW, WINDOW)

    def reset():
        m_ref[...] = jnp.full(m_ref.shape, NEG_BIG, F32)
        l_ref[...] = jnp.zeros(l_ref.shape, F32)
        acc_ref[...] = jnp.zeros(acc_ref.shape, F32)

    def result():
        ot = acc_ref[...] / l_ref[...]
        return [ot[:, g * tq:(g + 1) * tq].T for g in range(GQA)]

    for h in range(N_KV_HEADS):
        q4 = jnp.concatenate([q_ref[:, (h * GQA + g) * HEAD_DIM:(h * GQA + g + 1) * HEAD_DIM]
                              for g in range(GQA)], axis=0)
        sb = selb_ref[:, hsl(h)]
        q_aug = jnp.concatenate([q4, jnp.concatenate([sb] * GQA, axis=0)], axis=1)
        tabs = [tab_ref[h * GQA + g:h * GQA + g + 1, :] for g in range(GQA)]
        d0 = jnp.concatenate([_bias_from_dist(t, dij) for t in tabs], axis=1)
        d1 = jnp.concatenate([_bias_from_dist(t, dij + tq) for t in tabs], axis=1)
        far = jnp.concatenate([t[:, HEAD_DIM - 1:HEAD_DIM] + jnp.zeros((1, tq), F32) for t in tabs], axis=1)
        far_t = jnp.broadcast_to(far, (tq, rows))
        diag = jnp.where(causal, d0, NEG_BIG)
        bias_sel = jnp.concatenate([far_t] * (n_back - 1) + [d1, diag], axis=0)
        bias_win = jnp.concatenate([jnp.where(dij4 < 0, far_t, NEG_BIG)] + [far_t] * (n_back - 2) + [d1, diag],
                                   axis=0)
        bias_sel = jnp.where(in_seq, bias_sel, NEG_BIG)
        bias_win = jnp.where(in_seq, bias_win, NEG_BIG)

        reset()
        k_near = jnp.concatenate([ks_ref[pl.ds(near0, near), hsl(h)], e2_ref[pl.ds(near0, near), :]], axis=1)
        _flash_chunk(k_near, vst_ref[hsl(h), pl.ds(near0, near)], q_aug, bias_sel, m_ref, l_ref, acc_ref, scale)

        def far_chunk(i, bias):
            r0 = far0(i)
            k = jnp.concatenate([ks_ref[pl.ds(r0, WINDOW), hsl(h)], e2_ref[pl.ds(r0, WINDOW), :]], axis=1)
            _flash_chunk(k, vst_ref[hsl(h), pl.ds(r0, WINDOW)], q_aug, bias, m_ref, l_ref, acc_ref, scale)

        def body(i, carry):
            far_chunk(i, far)
            return carry

        lax.fori_loop(0, n_full, body, 0)

        @pl.when(n_rem > 0)
        def _():
            live = lax.broadcasted_iota(jnp.int32, (WINDOW, rows), 0) < n_rem * tq
            far_chunk(n_full, jnp.where(live, jnp.broadcast_to(far, (WINDOW, rows)), NEG_BIG))

        o_sel = result()

        reset()
        _flash_chunk(kw_ref[pl.ds(near0, near), hsl(h)], vwt_ref[hsl(h), pl.ds(near0, near)], q4, bias_win,
                     m_ref, l_ref, acc_ref, scale)
        o_win = result()

        for g in range(GQA):
            hd = h * GQA + g
            o = (sig[:, 3 * hd:3 * hd + 1] * ocmp_ref[:, hsl(hd)]
                 + sig[:, 3 * hd + 1:3 * hd + 2] * o_sel[g]
                 + sig[:, 3 * hd + 2:3 * hd + 3] * o_win[g])
            y = o * lax.rsqrt(jnp.mean(o * o, axis=-1, keepdims=True) + EPS)
            o_ref[:, hsl(hd)] = (y * gnw_ref[:, hsl(hd)]).astype(o_ref.dtype)


def sel_win(qkv, selb, ksp, vstp, kwp, vwtp, e2p, tab, ocmp, gl, gnw, B, S, tq=2 * Q_BLOCK):
    T = qkv.shape[0]
    n_qb = S // tq
    dq = N_Q_HEADS * HEAD_DIM
    dkv = N_KV_HEADS * HEAD_DIM
    SP = S + WINDOW
    kern = functools.partial(_sel_win_kernel, n_qb=n_qb, scale=HEAD_DIM ** -0.5)
    k_spec = pl.BlockSpec((SP, dkv), lambda i: (i // n_qb, 0))
    vt_spec = pl.BlockSpec((dkv, SP), lambda i: (i // n_qb, 0))
    return pl.pallas_call(
        kern,
        grid=(T // tq,),
        in_specs=[pl.BlockSpec((tq, dq), lambda i: (i, 0)),
                  pl.BlockSpec((tq, dkv), lambda i: (i, 0)),
                  k_spec, vt_spec, k_spec, vt_spec,
                  pl.BlockSpec((SP, HEAD_DIM), lambda i: (0, 0)),
                  pl.BlockSpec(tab.shape, lambda i: (0, 0)),
                  pl.BlockSpec((tq, dq), lambda i: (i, 0)),
                  pl.BlockSpec((tq, HEAD_DIM), lambda i: (i, 0)),
                  pl.BlockSpec((1, dq), lambda i: (0, 0))],
        out_specs=pl.BlockSpec((tq, dq), lambda i: (i, 0)),
        out_shape=jax.ShapeDtypeStruct((T, dq), BF16),
        scratch_shapes=[pltpu.VMEM((1, GQA * tq), F32), pltpu.VMEM((1, GQA * tq), F32),
                        pltpu.VMEM((HEAD_DIM, GQA * tq), F32)],
        compiler_params=_params(("parallel",)),
    )(qkv, selb, ksp, vstp, kwp, vwtp, e2p, tab, ocmp, gl, gnw)


def _conv_kernel(b_ref, c_ref, h_ref, cp_ref, hp_ref, cw_ref, cb_ref, gnw_ref, o_ref, u_ref, *, tiles_per_seq):
    tm = b_ref.shape[0]
    first = (pl.program_id(0) % tiles_per_seq) == 0
    u_prev = cp_ref[...] * hp_ref[...]
    u_ref[0:8, :] = jnp.where(first, 0.0, u_prev)
    u = c_ref[...] * h_ref[...]
    u_ref[8:8 + tm, :] = u
    y = (cw_ref[0:1, :] * u_ref[6:6 + tm, :] + cw_ref[1:2, :] * u_ref[7:7 + tm, :]
         + cw_ref[2:3, :] * u + cb_ref[...])
    o = b_ref[...] * y
    n_groups = o.shape[1] // HEAD_DIM
    for g in range(n_groups):
        sl = slice(g * HEAD_DIM, (g + 1) * HEAD_DIM)
        og = o[:, sl]
        yg = og * lax.rsqrt(jnp.mean(og * og, axis=-1, keepdims=True) + EPS)
        o_ref[:, sl] = (yg * gnw_ref[:, sl]).astype(o_ref.dtype)


def conv_mixer(gc, conv_w, conv_b, gnw, S, tm):
    T = gc.shape[0]
    dc = conv_w.shape[1]
    tps = S // tm
    kern = functools.partial(_conv_kernel, tiles_per_seq=tps)
    prev = lambda col: pl.BlockSpec((8, dc), lambda i, col=col: (jnp.maximum(i * (tm // 8) - 1, 0), col))
    cur = lambda col: pl.BlockSpec((tm, dc), lambda i, col=col: (i, col))
    return pl.pallas_call(
        kern,
        grid=(T // tm,),
        in_specs=[cur(0), cur(1), cur(2), prev(1), prev(2),
                  pl.BlockSpec((8, dc), lambda i: (0, 0)),
                  pl.BlockSpec((1, dc), lambda i: (0, 0)),
                  pl.BlockSpec((1, dc), lambda i: (0, 0))],
        out_specs=pl.BlockSpec((tm, dc), lambda i: (i, 0)),
        out_shape=jax.ShapeDtypeStruct((T, dc), BF16),
        scratch_shapes=[pltpu.VMEM((tm + 8, dc), F32)],
        compiler_params=_params(("parallel",)),
    )(gc, gc, gc, gc, gc, conv_w, conv_b, gnw)


def _out_proj_kernel(ma_ref, mc_ref, wa_ref, wc_ref, x_ref, o_ref):
    o_ref[...] = x_ref[...] + _dot(ma_ref[...], wa_ref[...]) + _dot(mc_ref[...], wc_ref[...])


def out_proj(ma, mc, wa, wc, x, tm, tn):
    T, da = ma.shape
    dc = mc.shape[1]
    D = x.shape[1]
    return pl.pallas_call(
        _out_proj_kernel,
        grid=(T // tm, D // tn),
        in_specs=[pl.BlockSpec((tm, da), lambda i, j: (i, 0)),
                  pl.BlockSpec((tm, dc), lambda i, j: (i, 0)),
                  pl.BlockSpec((da, tn), lambda i, j: (0, j)),
                  pl.BlockSpec((dc, tn), lambda i, j: (0, j)),
                  pl.BlockSpec((tm, tn), lambda i, j: (i, j))],
        out_specs=pl.BlockSpec((tm, tn), lambda i, j: (i, j)),
        out_shape=jax.ShapeDtypeStruct((T, D), F32),
        compiler_params=_params(("parallel", "arbitrary")),
    )(ma, mc, wa, wc, x)


def _peer_route_kernel(q_ref, sk_ref, u_ref, v_ref, e_ref, g_ref, ub_ref, vb_ref, sv_ref, si_ref):
    ub_ref[...] = u_ref[...].astype(BF16)
    vb_ref[...] = v_ref[...].astype(BF16)
    tt = q_ref.shape[0]
    K = PEER_TOPK
    n_io = lax.broadcasted_iota(jnp.int32, (N_KEYS, tt), 0).astype(F32)
    r16 = lax.broadcasted_iota(jnp.int32, (16, tt), 0).astype(F32)
    r8 = lax.broadcasted_iota(jnp.int32, (8, tt), 0).astype(F32)
    ninf = -jnp.inf

    for h in range(PEER_HEADS):
        for c in range(2):
            col = (h * 2 + c) * HEAD_DIM
            s = _dot_nt(sk_ref[h * 2 + c], q_ref[:, col:col + HEAD_DIM])
            for k in range(K):
                m = jnp.max(s, axis=0, keepdims=True)
                idx = jnp.min(jnp.where(s == m, n_io, float(N_KEYS)), axis=0, keepdims=True)
                s = jnp.where(n_io == idx, ninf, s)
                sv_ref[c, k:k + 1, :] = m
                si_ref[c, k:k + 1, :] = idx
        sv0, sv1 = sv_ref[0], sv_ref[1]
        si0, si1 = si_ref[0], si_ref[1]
        cands, flats, exps = [], [], []
        for a, nb, rows in ((0, 16, 16), (1, 8, 8), (2, 5, 8), (3, 4, 8)):
            r_io = r16 if rows == 16 else r8
            val = sv0[a:a + 1, :] + sv1[0:rows, :]
            cands.append(jnp.where(r_io < nb, val, ninf))
            flats.append(a * 16.0 + r_io)
            exps.append(si0[a:a + 1, :] * float(N_KEYS) + si1[0:rows, :])
        for b, lo, hi, rows in ((0, 4, 16, 16), (1, 4, 8, 8), (2, 4, 5, 8)):
            r_io = r16 if rows == 16 else r8
            val = sv0[0:rows, :] + sv1[b:b + 1, :]
            cands.append(jnp.where((r_io >= lo) & (r_io < hi), val, ninf))
            flats.append(r_io * 16.0 + b)
            exps.append(si0[0:rows, :] * float(N_KEYS) + si1[b:b + 1, :])
        cand = jnp.concatenate(cands, axis=0)
        flat = jnp.concatenate(flats, axis=0)
        expt = jnp.concatenate(exps, axis=0)
        cvs, exs = [], []
        for k in range(K):
            m = jnp.max(cand, axis=0, keepdims=True)
            fsel = jnp.min(jnp.where(cand == m, flat, 1e9), axis=0, keepdims=True)
            hit = flat == fsel
            exs.append(jnp.max(jnp.where(hit, expt, -1.0), axis=0, keepdims=True))
            cand = jnp.where(hit, ninf, cand)
            cvs.append(m)
        cv = jnp.concatenate(cvs, axis=0)
        ex = jnp.concatenate(exs, axis=0)
        ev = jnp.exp(cv - jnp.max(cv, axis=0, keepdims=True))
        gates = ev / jnp.sum(ev, axis=0, keepdims=True)
        e_ref[0, h * K:(h + 1) * K, :] = ex.astype(jnp.int32)
        g_ref[0, h * K:(h + 1) * K, :] = gates


def peer_route(qp, subkeys, u, v):
    T = qp.shape[0]
    tt = 128
    P = PEER_HEADS * PEER_TOPK
    nt = T // tt
    E, D = u.shape
    slab = E // nt
    assert slab * nt == E and slab % 16 == 0
    tab_spec = pl.BlockSpec((slab, D), lambda i: (i, 0))
    return pl.pallas_call(
        _peer_route_kernel,
        grid=(nt,),
        in_specs=[pl.BlockSpec((tt, qp.shape[1]), lambda i: (i, 0)),
                  pl.BlockSpec(subkeys.shape, lambda i: (0, 0, 0)),
                  tab_spec, tab_spec],
        out_specs=[pl.BlockSpec((1, P, tt), lambda i: (i, 0, 0))] * 2 + [tab_spec, tab_spec],
        out_shape=[jax.ShapeDtypeStruct((nt, P, tt), jnp.int32),
                   jax.ShapeDtypeStruct((nt, P, tt), F32),
                   jax.ShapeDtypeStruct((E, D), BF16), jax.ShapeDtypeStruct((E, D), BF16)],
        scratch_shapes=[pltpu.VMEM((2, PEER_TOPK, tt), F32), pltpu.VMEM((2, PEER_TOPK, tt), F32)],
        compiler_params=_params(("parallel",)),
    )(qp, subkeys, u, v)


def _peer_gbuild_kernel(e_ref, g_ref, o_ref, i1_ref, i2_ref, gt_ref, s_ref):
    tt = e_ref.shape[2]
    e = e_ref[0].T
    i1_ref[...] = e >> 7
    i2_ref[...] = e & (N_KEYS - 1)
    gt_ref[...] = g_ref[0].T
    P = e.shape[1]
    k_io = lax.broadcasted_iota(jnp.int32, (N_KEYS, P), 0)
    group = 64
    sub = 8
    def body(tg, carry):
        base = pl.multiple_of(tg * group, group)
        r1s = i1_ref[pl.ds(base, group), :]
        r2s = i2_ref[pl.ds(base, group), :]
        rgs = gt_ref[pl.ds(base, group), :]
        for u in range(group):
            lhs = jnp.where(k_io == r1s[u:u + 1, :], rgs[u:u + 1, :], 0.0).astype(BF16)
            rhs = jnp.where(k_io == r2s[u:u + 1, :], 1.0, 0.0).astype(BF16)
            g_t = _dot_nt(lhs, rhs)
            for a in range(N_KEYS // sub):
                row0 = pl.multiple_of((a * tt + base + u) * sub, sub)
                s_ref[pl.ds(row0, sub), :] = g_t[a * sub:(a + 1) * sub, :]
        return carry

    lax.fori_loop(0, tt // group, body, 0)
    for i1 in range(N_KEYS):
        a, r = divmod(i1, sub)
        o_ref[i1] = s_ref[pl.ds(a * tt * sub + r, tt, stride=sub), :].astype(o_ref.dtype)


def peer_gbuild(ex, gates):
    nt, P, tt = ex.shape
    T = nt * tt
    return pl.pallas_call(
        _peer_gbuild_kernel,
        grid=(nt,),
        in_specs=[pl.BlockSpec((1, P, tt), lambda i: (i, 0, 0))] * 2,
        out_specs=pl.BlockSpec((N_KEYS, tt, N_KEYS), lambda i: (0, i, 0)),
        out_shape=jax.ShapeDtypeStruct((N_KEYS, T, N_KEYS), BF16),
        scratch_shapes=[pltpu.VMEM((tt, P), jnp.int32), pltpu.VMEM((tt, P), jnp.int32),
                        pltpu.VMEM((tt, P), F32), pltpu.VMEM((tt * N_KEYS, N_KEYS), F32)],
        compiler_params=_params(("parallel",)),
    )(ex, gates)


def _peer_dense_kernel(h_ref, nw_ref, fw_ref, u_ref, v_ref, gm_ref, o_ref, hn_ref, acc_ref):
    j = pl.program_id(1)

    @pl.when(j == 0)
    def _():
        x = h_ref[...]
        y = x * lax.rsqrt(jnp.mean(x * x, axis=-1, keepdims=True) + EPS)
        hn_ref[...] = (y * nw_ref[...]).astype(BF16)
        acc_ref[...] = jnp.zeros(acc_ref.shape, F32)

    a = _dot_nt(hn_ref[...], u_ref[...])
    gm = jnp.concatenate([gm_ref[k] for k in range(gm_ref.shape[0])], axis=1)
    w = (gm.astype(F32) * jax.nn.gelu(a)).astype(BF16)
    acc_ref[...] += _dot(w, v_ref[...])

    @pl.when(j == pl.num_programs(1) - 1)
    def _():
        x = h_ref[...] + acc_ref[...]
        y = x * lax.rsqrt(jnp.mean(x * x, axis=-1, keepdims=True) + EPS)
        o_ref[...] = y * fw_ref[...]


def peer_dense(h, ffn_nw, final_nw, u, v, gm, tt, ec):
    T, D = h.shape
    E = u.shape[0]
    return pl.pallas_call(
        _peer_dense_kernel,
        grid=(T // tt, E // ec),
        in_specs=[pl.BlockSpec((tt, D), lambda i, j: (i, 0)),
                  pl.BlockSpec((1, D), lambda i, j: (0, 0)),
                  pl.BlockSpec((1, D), lambda i, j: (0, 0)),
                  pl.BlockSpec((ec, D), lambda i, j: (j, 0)),
                  pl.BlockSpec((ec, D), lambda i, j: (j, 0)),
                  pl.BlockSpec((ec // N_KEYS, tt, N_KEYS), lambda i, j: (j, i, 0))],
        out_specs=pl.BlockSpec((tt, D), lambda i, j: (i, 0)),
        out_shape=jax.ShapeDtypeStruct((T, D), F32),
        scratch_shapes=[pltpu.VMEM((tt, D), BF16), pltpu.VMEM((tt, D), F32)],
        compiler_params=_params(("parallel", "arbitrary")),
    )(h, ffn_nw.reshape(1, D), final_nw.reshape(1, D), u, v, gm)


def _t5_bucket_np(n_dist):
    d = np.arange(n_dist)
    max_exact = NUM_BUCKETS // 2
    nf = np.maximum(d, 1).astype(np.float64)
    large = max_exact + (np.log(nf / max_exact) / math.log(MAX_DISTANCE / max_exact)
                         * (NUM_BUCKETS - max_exact)).astype(np.int64)
    large = np.minimum(large, NUM_BUCKETS - 1)
    return np.where(d < max_exact, d, large).astype(np.int32)


def _overlap_t_np(S):
    n_c = (S - L_CMP) // STRIDE_CMP + 1
    n_sel = S // L_SEL
    pos = np.arange(n_c)[:, None] * STRIDE_CMP + np.arange(L_CMP)[None, :]
    m = np.zeros((n_c + 1, n_sel), np.float32)
    np.add.at(m, (np.repeat(np.arange(n_c), L_CMP), (pos // L_SEL).reshape(-1)), 1.0 / L_CMP)
    return np.ascontiguousarray(m.T)


def _block_onehot_np(S):
    e2 = np.zeros((S, HEAD_DIM), np.float32)
    e2[np.arange(S), np.arange(S) // L_SEL] = 1.0
    return e2


def nsa_conv_mix(xt, B, S, attn_norm_w, w_in, w_cmp_k, w_cmp_v, cmp_pos, conv_w, conv_b,
                 attn_gnw, conv_gnw, rel_bias, tm=512):
    T, D = xt.shape
    dq = N_Q_HEADS * HEAD_DIM
    dkv = N_KV_HEADS * HEAD_DIM
    n_attn = dq + 6 * dkv
    n_gate = 3 * N_Q_HEADS
    dc = (w_in.shape[1] - n_attn - n_gate) // 3
    w_attn = w_in[:, :n_attn].astype(BF16)
    w_gate = jnp.pad(w_in[:, n_attn:n_attn + n_gate], ((0, 0), (0, HEAD_DIM - n_gate))).astype(BF16)
    w_conv = w_in[:, n_attn + n_gate:].astype(BF16)

    qkv, gc, gl = in_proj(xt, attn_norm_w, w_attn, w_conv, w_gate, tm, n_attn // 2, dc)

    n16 = S // STRIDE_CMP
    k16 = qkv[:, dq:dq + dkv].reshape(B * n16, STRIDE_CMP * dkv)
    v16 = qkv[:, dq + dkv:dq + 2 * dkv].reshape(B * n16, STRIDE_CMP * dkv)

    def wbig(w, lo):
        wl = w[lo:lo + STRIDE_CMP]
        eye = jnp.eye(N_KV_HEADS, dtype=w.dtype)
        return jnp.einsum('lde,hg->lhdge', wl, eye).reshape(STRIDE_CMP * dkv, dkv).astype(BF16)

    def posrow(lo):
        p = cmp_pos[lo:lo + STRIDE_CMP]
        return jnp.broadcast_to(p[:, None, :], (STRIDE_CMP, N_KV_HEADS, HEAD_DIM)).reshape(1, STRIDE_CMP * dkv)

    kc, vc = compress(k16, v16, posrow(0), posrow(STRIDE_CMP),
                      wbig(w_cmp_k, 0), wbig(w_cmp_k, STRIDE_CMP),
                      wbig(w_cmp_v, 0), wbig(w_cmp_v, STRIDE_CMP), B)

    tab = rel_bias[_t5_bucket_np(HEAD_DIM)].T
    ovt = jnp.asarray(_overlap_t_np(S), BF16)
    e2 = jnp.asarray(_block_onehot_np(S), BF16)

    ocmp, selb = cmp_select(qkv, kc, vc, tab, ovt, B, S)
    def keys_padded(col):
        k = qkv[:, col:col + dkv].reshape(B, S, dkv)
        return jnp.pad(k, ((0, 0), (WINDOW, 0), (0, 0))).reshape(B * (WINDOW + S), dkv)

    def values_t_padded(col):
        v = qkv[:, col:col + dkv].reshape(B, S, dkv).transpose(0, 2, 1)
        return jnp.pad(v, ((0, 0), (0, 0), (WINDOW, 0))).reshape(B * dkv, WINDOW + S)

    e2p = jnp.pad(e2, ((WINDOW, 0), (0, 0)))
    attn = sel_win(qkv, selb, keys_padded(dq + 2 * dkv), values_t_padded(dq + 3 * dkv),
                   keys_padded(dq + 4 * dkv), values_t_padded(dq + 5 * dkv), e2p, tab, ocmp, gl,
                   attn_gnw.reshape(1, dq), B, S)
    cw8 = jnp.pad(conv_w, ((0, 8 - conv_w.shape[0]), (0, 0)))
    conv = conv_mixer(gc, cw8, conv_b.reshape(1, dc), conv_gnw.reshape(1, dc), S, tm)
    return attn, conv


def peer_block(h, ffn_nw, final_nw, peer_wq, peer_subkeys, peer_u, peer_v, tm=512, tt=512, ec=1024):
    T, D = h.shape
    qp = norm_matmul(h, ffn_nw, peer_wq.astype(BF16), BF16, tm, 1024)
    sk = peer_subkeys.reshape(PEER_HEADS * 2, N_KEYS, peer_subkeys.shape[-1]).astype(BF16)
    ex, gates, u_bf, v_bf = peer_route(qp, sk, peer_u, peer_v)
    gm = peer_gbuild(ex, gates)
    return peer_dense(h, ffn_nw, final_nw, u_bf, v_bf, gm, tt, ec)


def kernel(x, attn_norm_w, w_in, w_cmp_k, w_cmp_v, cmp_pos, conv_w, conv_b, attn_group_norm_w,
           conv_group_norm_w, w_out, rel_bias, ffn_norm_w, peer_wq, peer_subkeys, peer_u, peer_v,
           final_norm_w):
    B, S, D = x.shape
    T = B * S
    xt = x.reshape(T, D)
    attn, conv = nsa_conv_mix(xt, B, S, attn_norm_w[0], w_in[0], w_cmp_k[0], w_cmp_v[0], cmp_pos[0],
                              conv_w[0], conv_b[0], attn_group_norm_w[0], conv_group_norm_w[0], rel_bias)
    da = attn.shape[1]
    wo = w_out[0].astype(BF16)
    h = out_proj(attn, conv, wo[:da], wo[da:], xt, 512, 1024)
    out = peer_block(h, ffn_norm_w[0], final_norm_w, peer_wq[0], peer_subkeys[0], peer_u[0], peer_v[0])
    return out.reshape(B, S, D)
```

```python
import functools
import math

import jax
import jax.numpy as jnp
import numpy as np
from jax import lax
from jax.experimental import pallas as pl
from jax.experimental.pallas import tpu as pltpu

F32 = jnp.float32
BF16 = jnp.bfloat16

HEAD_DIM = 128
N_KV_HEADS = 2
GQA = 4
N_Q_HEADS = N_KV_HEADS * GQA
L_CMP = 32
STRIDE_CMP = 16
L_SEL = 64
N_SEL = 16
WINDOW = 512
Q_BLOCK = 128
FORCED_SCORE = float(GQA + 1)
NUM_BUCKETS = 32
MAX_DISTANCE = 128
PEER_HEADS = 8
N_KEYS = 128
PEER_TOPK = 16
EPS = 1e-6
NEG_BIG = -1e30
SEL_OFF = -float(2 ** 30)
VMEM_LIMIT = 56 * 1024 * 1024


def _dot(a, b):
    return jnp.dot(a, b, preferred_element_type=F32)


def _dot_nt(a, b):
    return lax.dot_general(a, b, (((1,), (1,)), ((), ())), preferred_element_type=F32)


def _params(sem, vmem=VMEM_LIMIT):
    return pltpu.CompilerParams(dimension_semantics=sem, vmem_limit_bytes=vmem)


def _norm_matmul_kernel(x_ref, nw_ref, w_ref, o_ref, xn_ref):
    @pl.when(pl.program_id(1) == 0)
    def _():
        x = x_ref[...]
        y = x * lax.rsqrt(jnp.mean(x * x, axis=-1, keepdims=True) + EPS)
        xn_ref[...] = (y * nw_ref[...]).astype(BF16)

    o_ref[...] = _dot(xn_ref[...], w_ref[...]).astype(o_ref.dtype)


def norm_matmul(x, norm_w, w, out_dtype, tm, tn):
    T, D = x.shape
    N = w.shape[1]
    return pl.pallas_call(
        _norm_matmul_kernel,
        grid=(T // tm, N // tn),
        in_specs=[pl.BlockSpec((tm, D), lambda i, j: (i, 0)),
                  pl.BlockSpec((1, D), lambda i, j: (0, 0)),
                  pl.BlockSpec((D, tn), lambda i, j: (0, j))],
        out_specs=pl.BlockSpec((tm, tn), lambda i, j: (i, j)),
        out_shape=jax.ShapeDtypeStruct((T, N), out_dtype),
        scratch_shapes=[pltpu.VMEM((tm, D), BF16)],
        compiler_params=_params(("parallel", "arbitrary")),
    )(x, norm_w.reshape(1, D), w)


def _compress_kernel(k_ref, v_ref, pa_ref, pb_ref, wka_ref, wkb_ref, wva_ref, wvb_ref, kc_ref, vc_ref):
    def one(x_ref, wa_ref, wb_ref, o_ref):
        x = x_ref[...].astype(F32)
        a = _dot((x + pa_ref[...]).astype(BF16), wa_ref[...])
        b = _dot((x + pb_ref[...]).astype(BF16), wb_ref[...])
        n = b.shape[0]
        o_ref[...] = (a + pltpu.roll(b, n - 1, 0)).astype(o_ref.dtype)

    one(k_ref, wka_ref, wkb_ref, kc_ref)
    one(v_ref, wva_ref, wvb_ref, vc_ref)


def compress(k16, v16, pos_a, pos_b, wka, wkb, wva, wvb, B):
    R, C = k16.shape
    nb = R // B
    dkv = wka.shape[1]
    full = lambda shp: pl.BlockSpec(shp, lambda b: (0, 0))
    return pl.pallas_call(
        _compress_kernel,
        grid=(B,),
        in_specs=[pl.BlockSpec((nb, C), lambda b: (b, 0)),
                  pl.BlockSpec((nb, C), lambda b: (b, 0)),
                  full((1, C)), full((1, C)),
                  full((C, dkv)), full((C, dkv)), full((C, dkv)), full((C, dkv))],
        out_specs=[pl.BlockSpec((nb, dkv), lambda b: (b, 0))] * 2,
        out_shape=[jax.ShapeDtypeStruct((R, dkv), BF16)] * 2,
        compiler_params=_params(("parallel",)),
    )(k16, v16, pos_a, pos_b, wka, wkb, wva, wvb)


def _bias_from_dist(tab_row, dist):
    idx = jnp.clip(dist, 0, 127)
    w = tab_row.shape[1]
    tab = jnp.broadcast_to(tab_row, (idx.shape[0], w))
    parts = [jnp.take_along_axis(tab, idx[:, k:k + w], axis=1) for k in range(0, idx.shape[1], w)]
    return parts[0] if len(parts) == 1 else jnp.concatenate(parts, axis=1)


def _cmp_select_kernel(q_ref, kc_ref, vc_ref, tab_ref, ovt_ref, ocmp_ref, selb_ref, *, n_qb, scale):
    c = pl.program_id(0) % n_qb
    tq = q_ref.shape[0]
    n_c = kc_ref.shape[0]
    n_sel = ovt_ref.shape[0]
    t0 = c * tq
    t_col = t0 + lax.broadcasted_iota(jnp.int32, (tq, n_c), 0)
    n_row = lax.broadcasted_iota(jnp.int32, (tq, n_c), 1)
    dist = t_col - (n_row * STRIDE_CMP + (L_CMP - 1))
    valid = dist >= 0

    j_io = lax.broadcasted_iota(jnp.int32, (n_sel, tq), 0)
    t_io = t0 + lax.broadcasted_iota(jnp.int32, (n_sel, tq), 1)
    blk_t = t_io // L_SEL
    forced = (j_io == 0) | (j_io == blk_t) | (j_io == blk_t - 1)
    causal_blk = j_io * L_SEL <= t_io

    for h in range(N_KV_HEADS):
        kc = kc_ref[:, h * HEAD_DIM:(h + 1) * HEAD_DIM]
        vc = vc_ref[:, h * HEAD_DIM:(h + 1) * HEAD_DIM]
        psum = jnp.zeros((tq, n_c), F32)
        for g in range(GQA):
            hd = h * GQA + g
            qh = q_ref[:, hd * HEAD_DIM:(hd + 1) * HEAD_DIM]
            bias = _bias_from_dist(tab_ref[hd:hd + 1, :], dist)
            s = _dot_nt(qh, kc) * scale + bias
            s = jnp.where(valid, s, NEG_BIG)
            m = jnp.max(s, axis=-1, keepdims=True)
            e = jnp.where(valid, jnp.exp(s - m), 0.0)
            d = jnp.sum(e, axis=-1, keepdims=True)
            p = e / jnp.where(d > 0, d, 1.0)
            ocmp_ref[:, hd * HEAD_DIM:(hd + 1) * HEAD_DIM] = _dot(p.astype(BF16), vc)
            psum = psum + p
        p_hi = psum.astype(BF16)
        p_lo = (psum - p_hi.astype(F32)).astype(BF16)
        ovt = ovt_ref[...]
        imp = _dot_nt(ovt, p_hi) + _dot_nt(ovt, p_lo)
        score = jnp.where(forced, FORCED_SCORE, jnp.where(causal_blk, imp, -1.0))
        rank = jnp.zeros((n_sel, tq), F32)
        for jp in range(n_sel):
            row = score[jp:jp + 1, :]
            rank = rank + jnp.where(j_io > jp, jnp.where(row >= score, 1.0, 0.0),
                                    jnp.where(row > score, 1.0, 0.0))
        selb = jnp.where(rank < float(N_SEL), 0.0, SEL_OFF)
        if n_sel < HEAD_DIM:
            selb = jnp.concatenate([selb, jnp.zeros((HEAD_DIM - n_sel, tq), F32)], axis=0)
        selb_ref[:, h * HEAD_DIM:(h + 1) * HEAD_DIM] = selb.T.astype(BF16)


def cmp_select(qkv, kc, vc, tab, ovt, B, S, tq=2 * Q_BLOCK):
    T = qkv.shape[0]
    n_qb = S // tq
    n_c = kc.shape[0] // B
    dq = N_Q_HEADS * HEAD_DIM
    dkv = N_KV_HEADS * HEAD_DIM
    kern = functools.partial(_cmp_select_kernel, n_qb=n_qb, scale=HEAD_DIM ** -0.5)
    return pl.pallas_call(
        kern,
        grid=(T // tq,),
        in_specs=[pl.BlockSpec((tq, dq), lambda i: (i, 0)),
                  pl.BlockSpec((n_c, dkv), lambda i: (i // n_qb, 0)),
                  pl.BlockSpec((n_c, dkv), lambda i: (i // n_qb, 0)),
                  pl.BlockSpec(tab.shape, lambda i: (0, 0)),
                  pl.BlockSpec(ovt.shape, lambda i: (0, 0))],
        out_specs=[pl.BlockSpec((tq, dq), lambda i: (i, 0)),
                   pl.BlockSpec((tq, dkv), lambda i: (i, 0))],
        out_shape=[jax.ShapeDtypeStruct((T, dq), F32),
                   jax.ShapeDtypeStruct((T, dkv), BF16)],
        compiler_params=_params(("parallel",)),
    )(qkv, kc, vc, tab, ovt)


def _flash_chunk(k, vt, q, bias, m_ref, l_ref, acc_ref, scale):
    s = _dot_nt(k, q) * scale + bias
    m_old = m_ref[...]
    m_new = jnp.maximum(m_old, jnp.max(s, axis=0, keepdims=True))
    alpha = jnp.exp(m_old - m_new)
    p = jnp.exp(s - m_new)
    l_ref[...] = alpha * l_ref[...] + jnp.sum(p, axis=0, keepdims=True)
    acc_ref[...] = alpha * acc_ref[...] + _dot(vt, p.astype(BF16))
    m_ref[...] = m_new


def _sel_win_kernel(q_ref, selb_ref, ks_ref, vst_ref, kw_ref, vwt_ref, e2_ref, tab_ref, ocmp_ref,
                    gl_ref, gnw_ref, o_ref, m_ref, l_ref, acc_ref, *, n_qb, scale):
    c = pl.program_id(0) % n_qb
    tq = q_ref.shape[0]
    rows = GQA * tq
    n_back = WINDOW // tq
    near = WINDOW + tq
    j_io = lax.broadcasted_iota(jnp.int32, (tq, tq), 0)
    i_io = lax.broadcasted_iota(jnp.int32, (tq, tq), 1)
    dij = i_io - j_io
    dij4 = jnp.concatenate([dij] * GQA, axis=1)
    causal = dij4 >= 0
    in_seq = lax.broadcasted_iota(jnp.int32, (near, rows), 0) >= WINDOW - c * tq
    sig = jax.nn.sigmoid(gl_ref[...])
    near0 = pl.multiple_of(c * tq, tq)
    n_far = jnp.maximum(c - n_back, 0)
    n_full = n_far // n_back
    n_rem = n_far - n_full * n_back

    def hsl(h):
        return slice(h * HEAD_DIM, (h + 1) * HEAD_DIM)

    def far0(i):
        return pl.multiple_of(WINDOW + i * WINDOW, WINDOW)

    def reset():
        m_ref[...] = jnp.full(m_ref.shape, NEG_BIG, F32)
        l_ref[...] = jnp.zeros(l_ref.shape, F32)
        acc_ref[...] = jnp.zeros(acc_ref.shape, F32)

    def result():
        ot = acc_ref[...] / l_ref[...]
        return [ot[:, g * tq:(g + 1) * tq].T for g in range(GQA)]

    for h in range(N_KV_HEADS):
        q4 = jnp.concatenate([q_ref[:, (h * GQA + g) * HEAD_DIM:(h * GQA + g + 1) * HEAD_DIM]
                              for g in range(GQA)], axis=0)
        sb = selb_ref[:, hsl(h)]
        q_aug = jnp.concatenate([q4, jnp.concatenate([sb] * GQA, axis=0)], axis=1)
        tabs = [tab_ref[h * GQA + g:h * GQA + g + 1, :] for g in range(GQA)]
        d0 = jnp.concatenate([_bias_from_dist(t, dij) for t in tabs], axis=1)
        d1 = jnp.concatenate([_bias_from_dist(t, dij + tq) for t in tabs], axis=1)
        far = jnp.concatenate([t[:, HEAD_DIM - 1:HEAD_DIM] + jnp.zeros((1, tq), F32) for t in tabs], axis=1)
        far_t = jnp.broadcast_to(far, (tq, rows))
        diag = jnp.where(causal, d0, NEG_BIG)
        bias_sel = jnp.concatenate([far_t] * (n_back - 1) + [d1, diag], axis=0)
        bias_win = jnp.concatenate([jnp.where(dij4 < 0, far_t, NEG_BIG)] + [far_t] * (n_back - 2) + [d1, diag],
                                   axis=0)
        bias_sel = jnp.where(in_seq, bias_sel, NEG_BIG)
        bias_win = jnp.where(in_seq, bias_win, NEG_BIG)

        reset()
        k_near = jnp.concatenate([ks_ref[pl.ds(near0, near), hsl(h)], e2_ref[pl.ds(near0, near), :]], axis=1)
        _flash_chunk(k_near, vst_ref[hsl(h), pl.ds(near0, near)], q_aug, bias_sel, m_ref, l_ref, acc_ref, scale)

        def far_chunk(i, bias):
            r0 = far0(i)
            k = jnp.concatenate([ks_ref[pl.ds(r0, WINDOW), hsl(h)], e2_ref[pl.ds(r0, WINDOW), :]], axis=1)
            _flash_chunk(k, vst_ref[hsl(h), pl.ds(r0, WINDOW)], q_aug, bias, m_ref, l_ref, acc_ref, scale)

        def body(i, carry):
            far_chunk(i, far)
            return carry

        lax.fori_loop(0, n_full, body, 0)

        @pl.when(n_rem > 0)
        def _():
            live = lax.broadcasted_iota(jnp.int32, (WINDOW, rows), 0) < n_rem * tq
            far_chunk(n_full, jnp.where(live, jnp.broadcast_to(far, (WINDOW, rows)), NEG_BIG))

        o_sel = result()

        reset()
        _flash_chunk(kw_ref[pl.ds(near0, near), hsl(h)], vwt_ref[hsl(h), pl.ds(near0, near)], q4, bias_win,
                     m_ref, l_ref, acc_ref, scale)
        o_win = result()

        for g in range(GQA):
            hd = h * GQA + g
            o = (sig[:, 3 * hd:3 * hd + 1] * ocmp_ref[:, hsl(hd)]
                 + sig[:, 3 * hd + 1:3 * hd + 2] * o_sel[g]
                 + sig[:, 3 * hd + 2:3 * hd + 3] * o_win[g])
            y = o * lax.rsqrt(jnp.mean(o * o, axis=-1, keepdims=True) + EPS)
            o_ref[:, hsl(hd)] = (y * gnw_ref[:, hsl(hd)]).astype(o_ref.dtype)


def sel_win(qkv, selb, ksp, vstp, kwp, vwtp, e2p, tab, ocmp, gl, gnw, B, S, tq=2 * Q_BLOCK):
    T = qkv.shape[0]
    n_qb = S // tq
    dq = N_Q_HEADS * HEAD_DIM
    dkv = N_KV_HEADS * HEAD_DIM
    SP = S + WINDOW
    kern = functools.partial(_sel_win_kernel, n_qb=n_qb, scale=HEAD_DIM ** -0.5)
    k_spec = pl.BlockSpec((SP, dkv), lambda i: (i // n_qb, 0))
    vt_spec = pl.BlockSpec((dkv, SP), lambda i: (i // n_qb, 0))
    return pl.pallas_call(
        kern,
        grid=(T // tq,),
        in_specs=[pl.BlockSpec((tq, dq), lambda i: (i, 0)),
                  pl.BlockSpec((tq, dkv), lambda i: (i, 0)),
                  k_spec, vt_spec, k_spec, vt_spec,
                  pl.BlockSpec((SP, HEAD_DIM), lambda i: (0, 0)),
                  pl.BlockSpec(tab.shape, lambda i: (0, 0)),
                  pl.BlockSpec((tq, dq), lambda i: (i, 0)),
                  pl.BlockSpec((tq, HEAD_DIM), lambda i: (i, 0)),
                  pl.BlockSpec((1, dq), lambda i: (0, 0))],
        out_specs=pl.BlockSpec((tq, dq), lambda i: (i, 0)),
        out_shape=jax.ShapeDtypeStruct((T, dq), BF16),
        scratch_shapes=[pltpu.VMEM((1, GQA * tq), F32), pltpu.VMEM((1, GQA * tq), F32),
                        pltpu.VMEM((HEAD_DIM, GQA * tq), F32)],
        compiler_params=_params(("parallel",)),
    )(qkv, selb, ksp, vstp, kwp, vwtp, e2p, tab, ocmp, gl, gnw)


def _conv_kernel(b_ref, c_ref, h_ref, cp_ref, hp_ref, cw_ref, cb_ref, gnw_ref, o_ref, u_ref, *, tiles_per_seq):
    tm = b_ref.shape[0]
    first = (pl.program_id(0) % tiles_per_seq) == 0
    u_prev = cp_ref[...] * hp_ref[...]
    u_ref[0:8, :] = jnp.where(first, 0.0, u_prev)
    u = c_ref[...] * h_ref[...]
    u_ref[8:8 + tm, :] = u
    y = (cw_ref[0:1, :] * u_ref[6:6 + tm, :] + cw_ref[1:2, :] * u_ref[7:7 + tm, :]
         + cw_ref[2:3, :] * u + cb_ref[...])
    o = b_ref[...] * y
    n_groups = o.shape[1] // HEAD_DIM
    for g in range(n_groups):
        sl = slice(g * HEAD_DIM, (g + 1) * HEAD_DIM)
        og = o[:, sl]
        yg = og * lax.rsqrt(jnp.mean(og * og, axis=-1, keepdims=True) + EPS)
        o_ref[:, sl] = (yg * gnw_ref[:, sl]).astype(o_ref.dtype)


def conv_mixer(gc, conv_w, conv_b, gnw, S, tm):
    T = gc.shape[0]
    dc = conv_w.shape[1]
    tps = S // tm
    kern = functools.partial(_conv_kernel, tiles_per_seq=tps)
    prev = lambda col: pl.BlockSpec((8, dc), lambda i, col=col: (jnp.maximum(i * (tm // 8) - 1, 0), col))
    cur = lambda col: pl.BlockSpec((tm, dc), lambda i, col=col: (i, col))
    return pl.pallas_call(
        kern,
        grid=(T // tm,),
        in_specs=[cur(0), cur(1), cur(2), prev(1), prev(2),
                  pl.BlockSpec((8, dc), lambda i: (0, 0)),
                  pl.BlockSpec((1, dc), lambda i: (0, 0)),
                  pl.BlockSpec((1, dc), lambda i: (0, 0))],
        out_specs=pl.BlockSpec((tm, dc), lambda i: (i, 0)),
        out_shape=jax.ShapeDtypeStruct((T, dc), BF16),
        scratch_shapes=[pltpu.VMEM((tm + 8, dc), F32)],
        compiler_params=_params(("parallel",)),
    )(gc, gc, gc, gc, gc, conv_w, conv_b, gnw)


def _out_proj_kernel(ma_ref, mc_ref, wa_ref, wc_ref, x_ref, o_ref):
    o_ref[...] = x_ref[...] + _dot(ma_ref[...], wa_ref[...]) + _dot(mc_ref[...], wc_ref[...])


def out_proj(ma, mc, wa, wc, x, tm, tn):
    T, da = ma.shape
    dc = mc.shape[1]
    D = x.shape[1]
    return pl.pallas_call(
        _out_proj_kernel,
        grid=(T // tm, D // tn),
        in_specs=[pl.BlockSpec((tm, da), lambda i, j: (i, 0)),
                  pl.BlockSpec((tm, dc), lambda i, j: (i, 0)),
                  pl.BlockSpec((da, tn), lambda i, j: (0, j)),
                  pl.BlockSpec((dc, tn), lambda i, j: (0, j)),
                  pl.BlockSpec((tm, tn), lambda i, j: (i, j))],
        out_specs=pl.BlockSpec((tm, tn), lambda i, j: (i, j)),
        out_shape=jax.ShapeDtypeStruct((T, D), F32),
        compiler_params=_params(("parallel", "arbitrary")),
    )(ma, mc, wa, wc, x)


def _peer_route_kernel(q_ref, sk_ref, u_ref, v_ref, e_ref, g_ref, ub_ref, vb_ref, sv_ref, si_ref):
    ub_ref[...] = u_ref[...].astype(BF16)
    vb_ref[...] = v_ref[...].astype(BF16)
    tt = q_ref.shape[0]
    K = PEER_TOPK
    n_io = lax.broadcasted_iota(jnp.int32, (N_KEYS, tt), 0).astype(F32)
    r16 = lax.broadcasted_iota(jnp.int32, (16, tt), 0).astype(F32)
    r8 = lax.broadcasted_iota(jnp.int32, (8, tt), 0).astype(F32)
    ninf = -jnp.inf

    for h in range(PEER_HEADS):
        for c in range(2):
            col = (h * 2 + c) * HEAD_DIM
            s = _dot_nt(sk_ref[h * 2 + c], q_ref[:, col:col + HEAD_DIM])
            for k in range(K):
                m = jnp.max(s, axis=0, keepdims=True)
                idx = jnp.min(jnp.where(s == m, n_io, float(N_KEYS)), axis=0, keepdims=True)
                s = jnp.where(n_io == idx, ninf, s)
                sv_ref[c, k:k + 1, :] = m
                si_ref[c, k:k + 1, :] = idx
        sv0, sv1 = sv_ref[0], sv_ref[1]
        si0, si1 = si_ref[0], si_ref[1]
        cands, flats, exps = [], [], []
        for a, nb, rows in ((0, 16, 16), (1, 8, 8), (2, 5, 8), (3, 4, 8)):
            r_io = r16 if rows == 16 else r8
            val = sv0[a:a + 1, :] + sv1[0:rows, :]
            cands.append(jnp.where(r_io < nb, val, ninf))
            flats.append(a * 16.0 + r_io)
            exps.append(si0[a:a + 1, :] * float(N_KEYS) + si1[0:rows, :])
        for b, lo, hi, rows in ((0, 4, 16, 16), (1, 4, 8, 8), (2, 4, 5, 8)):
            r_io = r16 if rows == 16 else r8
            val = sv0[0:rows, :] + sv1[b:b + 1, :]
            cands.append(jnp.where((r_io >= lo) & (r_io < hi), val, ninf))
            flats.append(r_io * 16.0 + b)
            exps.append(si0[0:rows, :] * float(N_KEYS) + si1[b:b + 1, :])
        cand = jnp.concatenate(cands, axis=0)
        flat = jnp.concatenate(flats, axis=0)
        expt = jnp.concatenate(exps, axis=0)
        cvs, exs = [], []
        for k in range(K):
            m = jnp.max(cand, axis=0, keepdims=True)
            fsel = jnp.min(jnp.where(cand == m, flat, 1e9), axis=0, keepdims=True)
            hit = flat == fsel
            exs.append(jnp.max(jnp.where(hit, expt, -1.0), axis=0, keepdims=True))
            cand = jnp.where(hit, ninf, cand)
            cvs.append(m)
        cv = jnp.concatenate(cvs, axis=0)
        ex = jnp.concatenate(exs, axis=0)
        ev = jnp.exp(cv - jnp.max(cv, axis=0, keepdims=True))
        gates = ev / jnp.sum(ev, axis=0, keepdims=True)
        e_ref[0, h * K:(h + 1) * K, :] = ex.astype(jnp.int32)
        g_ref[0, h * K:(h + 1) * K, :] = gates


def peer_route(qp, subkeys, u, v):
    T = qp.shape[0]
    tt = 128
    P = PEER_HEADS * PEER_TOPK
    nt = T // tt
    E, D = u.shape
    slab = E // nt
    assert slab * nt == E and slab % 16 == 0
    tab_spec = pl.BlockSpec((slab, D), lambda i: (i, 0))
    return pl.pallas_call(
        _peer_route_kernel,
        grid=(nt,),
        in_specs=[pl.BlockSpec((tt, qp.shape[1]), lambda i: (i, 0)),
                  pl.BlockSpec(subkeys.shape, lambda i: (0, 0, 0)),
                  tab_spec, tab_spec],
        out_specs=[pl.BlockSpec((1, P, tt), lambda i: (i, 0, 0))] * 2 + [tab_spec, tab_spec],
        out_shape=[jax.ShapeDtypeStruct((nt, P, tt), jnp.int32),
                   jax.ShapeDtypeStruct((nt, P, tt), F32),
                   jax.ShapeDtypeStruct((E, D), BF16), jax.ShapeDtypeStruct((E, D), BF16)],
        scratch_shapes=[pltpu.VMEM((2, PEER_TOPK, tt), F32), pltpu.VMEM((2, PEER_TOPK, tt), F32)],
        compiler_params=_params(("parallel",)),
    )(qp, subkeys, u, v)


def _peer_gbuild_kernel(e_ref, g_ref, o_ref, i1_ref, i2_ref, gt_ref, s_ref):
    tt = e_ref.shape[2]
    e = e_ref[0].T
    i1_ref[...] = e >> 7
    i2_ref[...] = e & (N_KEYS - 1)
    gt_ref[...] = g_ref[0].T
    P = e.shape[1]
    k_io = lax.broadcasted_iota(jnp.int32, (N_KEYS, P), 0)
    sub = 8
    half = tt // 2

    def build(t0):
        r1s = i1_ref[t0:t0 + half, :]
        r2s = i2_ref[t0:t0 + half, :]
        rgs = gt_ref[t0:t0 + half, :]
        for u in range(half):
            lhs = jnp.where(k_io == r1s[u:u + 1, :], rgs[u:u + 1, :], 0.0).astype(BF16)
            rhs = jnp.where(k_io == r2s[u:u + 1, :], 1.0, 0.0).astype(BF16)
            g_t = _dot_nt(lhs, rhs)
            for a in range(N_KEYS // sub):
                row0 = (a * tt + t0 + u) * sub
                s_ref[row0:row0 + sub, :] = g_t[a * sub:(a + 1) * sub, :]

    def regroup(t0):
        for i1 in range(N_KEYS):
            a, r = divmod(i1, sub)
            o_ref[i1, t0:t0 + half, :] = s_ref[pl.ds((a * tt + t0) * sub + r, half, stride=sub), :].astype(o_ref.dtype)

    build(0)
    build(half)
    regroup(0)
    regroup(half)


def peer_gbuild(ex, gates):
    nt, P, tt = ex.shape
    T = nt * tt
    return pl.pallas_call(
        _peer_gbuild_kernel,
        grid=(nt,),
        in_specs=[pl.BlockSpec((1, P, tt), lambda i: (i, 0, 0))] * 2,
        out_specs=pl.BlockSpec((N_KEYS, tt, N_KEYS), lambda i: (0, i, 0)),
        out_shape=jax.ShapeDtypeStruct((N_KEYS, T, N_KEYS), BF16),
        scratch_shapes=[pltpu.VMEM((tt, P), jnp.int32), pltpu.VMEM((tt, P), jnp.int32),
                        pltpu.VMEM((tt, P), F32), pltpu.VMEM((tt * N_KEYS, N_KEYS), F32)],
        compiler_params=_params(("parallel",)),
    )(ex, gates)


def _peer_dense_kernel(h_ref, nw_ref, fw_ref, u_ref, v_ref, gm_ref, o_ref, hn_ref, acc_ref):
    j = pl.program_id(1)

    @pl.when(j == 0)
    def _():
        x = h_ref[...]
        y = x * lax.rsqrt(jnp.mean(x * x, axis=-1, keepdims=True) + EPS)
        hn_ref[...] = (y * nw_ref[...]).astype(BF16)
        acc_ref[...] = jnp.zeros(acc_ref.shape, F32)

    a = _dot_nt(hn_ref[...], u_ref[...])
    gm = jnp.concatenate([gm_ref[k] for k in range(gm_ref.shape[0])], axis=1)
    w = (gm.astype(F32) * jax.nn.gelu(a)).astype(BF16)
    acc_ref[...] += _dot(w, v_ref[...])

    @pl.when(j == pl.num_programs(1) - 1)
    def _():
        x = h_ref[...] + acc_ref[...]
        y = x * lax.rsqrt(jnp.mean(x * x, axis=-1, keepdims=True) + EPS)
        o_ref[...] = y * fw_ref[...]


def peer_dense(h, ffn_nw, final_nw, u, v, gm, tt, ec):
    T, D = h.shape
    E = u.shape[0]
    return pl.pallas_call(
        _peer_dense_kernel,
        grid=(T // tt, E // ec),
        in_specs=[pl.BlockSpec((tt, D), lambda i, j: (i, 0)),
                  pl.BlockSpec((1, D), lambda i, j: (0, 0)),
                  pl.BlockSpec((1, D), lambda i, j: (0, 0)),
                  pl.BlockSpec((ec, D), lambda i, j: (j, 0)),
                  pl.BlockSpec((ec, D), lambda i, j: (j, 0)),
                  pl.BlockSpec((ec // N_KEYS, tt, N_KEYS), lambda i, j: (j, i, 0))],
        out_specs=pl.BlockSpec((tt, D), lambda i, j: (i, 0)),
        out_shape=jax.ShapeDtypeStruct((T, D), F32),
        scratch_shapes=[pltpu.VMEM((tt, D), BF16), pltpu.VMEM((tt, D), F32)],
        compiler_params=_params(("parallel", "arbitrary")),
    )(h, ffn_nw.reshape(1, D), final_nw.reshape(1, D), u, v, gm)


def _t5_bucket_np(n_dist):
    d = np.arange(n_dist)
    max_exact = NUM_BUCKETS // 2
    nf = np.maximum(d, 1).astype(np.float64)
    large = max_exact + (np.log(nf / max_exact) / math.log(MAX_DISTANCE / max_exact)
                         * (NUM_BUCKETS - max_exact)).astype(np.int64)
    large = np.minimum(large, NUM_BUCKETS - 1)
    return np.where(d < max_exact, d, large).astype(np.int32)


def _overlap_t_np(S):
    n_c = (S - L_CMP) // STRIDE_CMP + 1
    n_sel = S // L_SEL
    pos = np.arange(n_c)[:, None] * STRIDE_CMP + np.arange(L_CMP)[None, :]
    m = np.zeros((n_c + 1, n_sel), np.float32)
    np.add.at(m, (np.repeat(np.arange(n_c), L_CMP), (pos // L_SEL).reshape(-1)), 1.0 / L_CMP)
    return np.ascontiguousarray(m.T)


def _block_onehot_np(S):
    e2 = np.zeros((S, HEAD_DIM), np.float32)
    e2[np.arange(S), np.arange(S) // L_SEL] = 1.0
    return e2


def nsa_conv_mix(xt, B, S, attn_norm_w, w_in, w_cmp_k, w_cmp_v, cmp_pos, conv_w, conv_b,
                 attn_gnw, conv_gnw, rel_bias, tm=512):
    T, D = xt.shape
    dq = N_Q_HEADS * HEAD_DIM
    dkv = N_KV_HEADS * HEAD_DIM
    n_attn = dq + 6 * dkv
    n_gate = 3 * N_Q_HEADS
    dc = (w_in.shape[1] - n_attn - n_gate) // 3
    w_attn = w_in[:, :n_attn].astype(BF16)
    w_gate = jnp.pad(w_in[:, n_attn:n_attn + n_gate], ((0, 0), (0, HEAD_DIM - n_gate))).astype(BF16)
    w_conv = w_in[:, n_attn + n_gate:].astype(BF16)

    qkv = norm_matmul(xt, attn_norm_w, w_attn, BF16, tm, n_attn // 2)
    gl = norm_matmul(xt, attn_norm_w, w_gate, F32, tm, HEAD_DIM)
    gc = norm_matmul(xt, attn_norm_w, w_conv, F32, tm, dc)

    n16 = S // STRIDE_CMP
    k16 = qkv[:, dq:dq + dkv].reshape(B * n16, STRIDE_CMP * dkv)
    v16 = qkv[:, dq + dkv:dq + 2 * dkv].reshape(B * n16, STRIDE_CMP * dkv)

    def wbig(w, lo):
        wl = w[lo:lo + STRIDE_CMP]
        eye = jnp.eye(N_KV_HEADS, dtype=w.dtype)
        return jnp.einsum('lde,hg->lhdge', wl, eye).reshape(STRIDE_CMP * dkv, dkv).astype(BF16)

    def posrow(lo):
        p = cmp_pos[lo:lo + STRIDE_CMP]
        return jnp.broadcast_to(p[:, None, :], (STRIDE_CMP, N_KV_HEADS, HEAD_DIM)).reshape(1, STRIDE_CMP * dkv)

    kc, vc = compress(k16, v16, posrow(0), posrow(STRIDE_CMP),
                      wbig(w_cmp_k, 0), wbig(w_cmp_k, STRIDE_CMP),
                      wbig(w_cmp_v, 0), wbig(w_cmp_v, STRIDE_CMP), B)

    tab = rel_bias[_t5_bucket_np(HEAD_DIM)].T
    ovt = jnp.asarray(_overlap_t_np(S), BF16)
    e2 = jnp.asarray(_block_onehot_np(S), BF16)

    ocmp, selb = cmp_select(qkv, kc, vc, tab, ovt, B, S)
    def keys_padded(col):
        k = qkv[:, col:col + dkv].reshape(B, S, dkv)
        return jnp.pad(k, ((0, 0), (WINDOW, 0), (0, 0))).reshape(B * (WINDOW + S), dkv)

    def values_t_padded(col):
        v = qkv[:, col:col + dkv].reshape(B, S, dkv).transpose(0, 2, 1)
        return jnp.pad(v, ((0, 0), (0, 0), (WINDOW, 0))).reshape(B * dkv, WINDOW + S)

    e2p = jnp.pad(e2, ((WINDOW, 0), (0, 0)))
    attn = sel_win(qkv, selb, keys_padded(dq + 2 * dkv), values_t_padded(dq + 3 * dkv),
                   keys_padded(dq + 4 * dkv), values_t_padded(dq + 5 * dkv), e2p, tab, ocmp, gl,
                   attn_gnw.reshape(1, dq), B, S)
    cw8 = jnp.pad(conv_w, ((0, 8 - conv_w.shape[0]), (0, 0)))
    conv = conv_mixer(gc, cw8, conv_b.reshape(1, dc), conv_gnw.reshape(1, dc), S, tm)
    return attn, conv


def peer_block(h, ffn_nw, final_nw, peer_wq, peer_subkeys, peer_u, peer_v, tm=512, tt=512, ec=1024):
    T, D = h.shape
    qp = norm_matmul(h, ffn_nw, peer_wq.astype(BF16), BF16, tm, 1024)
    sk = peer_subkeys.reshape(PEER_HEADS * 2, N_KEYS, peer_subkeys.shape[-1]).astype(BF16)
    ex, gates, u_bf, v_bf = peer_route(qp, sk, peer_u, peer_v)
    gm = peer_gbuild(ex, gates)
    return peer_dense(h, ffn_nw, final_nw, u_bf, v_bf, gm, tt, ec)


def kernel(x, attn_norm_w, w_in, w_cmp_k, w_cmp_v, cmp_pos, conv_w, conv_b, attn_group_norm_w,
           conv_group_norm_w, w_out, rel_bias, ffn_norm_w, peer_wq, peer_subkeys, peer_u, peer_v,
           final_norm_w):
    B, S, D = x.shape
    T = B * S
    xt = x.reshape(T, D)
    attn, conv = nsa_conv_mix(xt, B, S, attn_norm_w[0], w_in[0], w_cmp_k[0], w_cmp_v[0], cmp_pos[0],
                              conv_w[0], conv_b[0], attn_group_norm_w[0], conv_group_norm_w[0], rel_bias)
    da = attn.shape[1]
    wo = w_out[0].astype(BF16)
    h = out_proj(attn, conv, wo[:da], wo[da:], xt, 512, 1024)
    out = peer_block(h, ffn_norm_w[0], final_norm_w, peer_wq[0], peer_subkeys[0], peer_u[0], peer_v[0])
    return out.reshape(B, S, D)
```

```python
import functools
import math

import jax
import jax.numpy as jnp
import numpy as np
from jax import lax
from jax.experimental import pallas as pl
from jax.experimental.pallas import tpu as pltpu

F32 = jnp.float32
BF16 = jnp.bfloat16

HEAD_DIM = 128
N_KV_HEADS = 2
GQA = 4
N_Q_HEADS = N_KV_HEADS * GQA
L_CMP = 32
STRIDE_CMP = 16
L_SEL = 64
N_SEL = 16
WINDOW = 512
Q_BLOCK = 128
FORCED_SCORE = float(GQA + 1)
NUM_BUCKETS = 32
MAX_DISTANCE = 128
PEER_HEADS = 8
N_KEYS = 128
PEER_TOPK = 16
EPS = 1e-6
NEG_BIG = -1e30
SEL_OFF = -float(2 ** 30)
LOG2E = math.log2(math.e)
VMEM_LIMIT = 56 * 1024 * 1024


def _dot(a, b):
    return jnp.dot(a, b, preferred_element_type=F32)


def _dot_nt(a, b):
    return lax.dot_general(a, b, (((1,), (1,)), ((), ())), preferred_element_type=F32)


def _params(sem, vmem=VMEM_LIMIT):
    return pltpu.CompilerParams(dimension_semantics=sem, vmem_limit_bytes=vmem)


def _norm_matmul_kernel(x_ref, nw_ref, w_ref, o_ref, xn_ref):
    @pl.when(pl.program_id(1) == 0)
    def _():
        x = x_ref[...]
        y = x * lax.rsqrt(jnp.mean(x * x, axis=-1, keepdims=True) + EPS)
        xn_ref[...] = (y * nw_ref[...]).astype(BF16)

    o_ref[...] = _dot(xn_ref[...], w_ref[...]).astype(o_ref.dtype)


def norm_matmul(x, norm_w, w, out_dtype, tm, tn):
    T, D = x.shape
    N = w.shape[1]
    return pl.pallas_call(
        _norm_matmul_kernel,
        grid=(T // tm, N // tn),
        in_specs=[pl.BlockSpec((tm, D), lambda i, j: (i, 0)),
                  pl.BlockSpec((1, D), lambda i, j: (0, 0)),
                  pl.BlockSpec((D, tn), lambda i, j: (0, j))],
        out_specs=pl.BlockSpec((tm, tn), lambda i, j: (i, j)),
        out_shape=jax.ShapeDtypeStruct((T, N), out_dtype),
        scratch_shapes=[pltpu.VMEM((tm, D), BF16)],
        compiler_params=_params(("parallel", "arbitrary")),
    )(x, norm_w.reshape(1, D), w)


def _compress_kernel(k_ref, v_ref, pa_ref, pb_ref, wka_ref, wkb_ref, wva_ref, wvb_ref, kc_ref, vc_ref):
    def one(x_ref, wa_ref, wb_ref, o_ref):
        x = x_ref[...].astype(F32)
        a = _dot((x + pa_ref[...]).astype(BF16), wa_ref[...])
        b = _dot((x + pb_ref[...]).astype(BF16), wb_ref[...])
        n = b.shape[0]
        o_ref[...] = (a + pltpu.roll(b, n - 1, 0)).astype(o_ref.dtype)

    one(k_ref, wka_ref, wkb_ref, kc_ref)
    one(v_ref, wva_ref, wvb_ref, vc_ref)


def compress(k16, v16, pos_a, pos_b, wka, wkb, wva, wvb, B):
    R, C = k16.shape
    nb = R // B
    dkv = wka.shape[1]
    full = lambda shp: pl.BlockSpec(shp, lambda b: (0, 0))
    return pl.pallas_call(
        _compress_kernel,
        grid=(B,),
        in_specs=[pl.BlockSpec((nb, C), lambda b: (b, 0)),
                  pl.BlockSpec((nb, C), lambda b: (b, 0)),
                  full((1, C)), full((1, C)),
                  full((C, dkv)), full((C, dkv)), full((C, dkv)), full((C, dkv))],
        out_specs=[pl.BlockSpec((nb, dkv), lambda b: (b, 0))] * 2,
        out_shape=[jax.ShapeDtypeStruct((R, dkv), BF16)] * 2,
        compiler_params=_params(("parallel",)),
    )(k16, v16, pos_a, pos_b, wka, wkb, wva, wvb)


def _bias_from_dist(tab_row, dist):
    idx = jnp.clip(dist, 0, 127)
    w = tab_row.shape[1]
    tab = jnp.broadcast_to(tab_row, (idx.shape[0], w))
    parts = [jnp.take_along_axis(tab, idx[:, k:k + w], axis=1) for k in range(0, idx.shape[1], w)]
    return parts[0] if len(parts) == 1 else jnp.concatenate(parts, axis=1)


def _cmp_select_kernel(q_ref, kc_ref, vc_ref, tab_ref, ovt_ref, ocmp_ref, selb_ref, *, n_qb, scale):
    c = pl.program_id(0) % n_qb
    tq = q_ref.shape[0]
    n_c = kc_ref.shape[0]
    n_sel = ovt_ref.shape[0]
    t0 = c * tq
    t_col = t0 + lax.broadcasted_iota(jnp.int32, (tq, n_c), 0)
    n_row = lax.broadcasted_iota(jnp.int32, (tq, n_c), 1)
    dist = t_col - (n_row * STRIDE_CMP + (L_CMP - 1))
    valid = dist >= 0

    j_io = lax.broadcasted_iota(jnp.int32, (n_sel, tq), 0)
    t_io = t0 + lax.broadcasted_iota(jnp.int32, (n_sel, tq), 1)
    blk_t = t_io // L_SEL
    forced = (j_io == 0) | (j_io == blk_t) | (j_io == blk_t - 1)
    causal_blk = j_io * L_SEL <= t_io

    for h in range(N_KV_HEADS):
        kc = kc_ref[:, h * HEAD_DIM:(h + 1) * HEAD_DIM]
        vc = vc_ref[:, h * HEAD_DIM:(h + 1) * HEAD_DIM]
        psum = jnp.zeros((tq, n_c), F32)
        for g in range(GQA):
            hd = h * GQA + g
            qh = q_ref[:, hd * HEAD_DIM:(hd + 1) * HEAD_DIM]
            bias = _bias_from_dist(tab_ref[hd:hd + 1, :], dist)
            s = _dot_nt(qh, kc) * scale + bias
            s = jnp.where(valid, s, NEG_BIG)
            m = jnp.max(s, axis=-1, keepdims=True)
            e = jnp.where(valid, jnp.exp(s - m), 0.0)
            d = jnp.sum(e, axis=-1, keepdims=True)
            p = e / jnp.where(d > 0, d, 1.0)
            ocmp_ref[:, hd * HEAD_DIM:(hd + 1) * HEAD_DIM] = _dot(p.astype(BF16), vc)
            psum = psum + p
        p_hi = psum.astype(BF16)
        p_lo = (psum - p_hi.astype(F32)).astype(BF16)
        ovt = ovt_ref[...]
        imp = _dot_nt(ovt, p_hi) + _dot_nt(ovt, p_lo)
        score = jnp.where(forced, FORCED_SCORE, jnp.where(causal_blk, imp, -1.0))
        rank = jnp.zeros((n_sel, tq), F32)
        for jp in range(n_sel):
            row = score[jp:jp + 1, :]
            rank = rank + jnp.where(j_io > jp, jnp.where(row >= score, 1.0, 0.0),
                                    jnp.where(row > score, 1.0, 0.0))
        selb = jnp.where(rank < float(N_SEL), 0.0, SEL_OFF)
        if n_sel < HEAD_DIM:
            selb = jnp.concatenate([selb, jnp.zeros((HEAD_DIM - n_sel, tq), F32)], axis=0)
        selb_ref[:, h * HEAD_DIM:(h + 1) * HEAD_DIM] = selb.T.astype(BF16)


def cmp_select(qkv, kc, vc, tab, ovt, B, S, tq=2 * Q_BLOCK):
    T = qkv.shape[0]
    n_qb = S // tq
    n_c = kc.shape[0] // B
    dq = N_Q_HEADS * HEAD_DIM
    dkv = N_KV_HEADS * HEAD_DIM
    kern = functools.partial(_cmp_select_kernel, n_qb=n_qb, scale=HEAD_DIM ** -0.5)
    return pl.pallas_call(
        kern,
        grid=(T // tq,),
        in_specs=[pl.BlockSpec((tq, dq), lambda i: (i, 0)),
                  pl.BlockSpec((n_c, dkv), lambda i: (i // n_qb, 0)),
                  pl.BlockSpec((n_c, dkv), lambda i: (i // n_qb, 0)),
                  pl.BlockSpec(tab.shape, lambda i: (0, 0)),
                  pl.BlockSpec(ovt.shape, lambda i: (0, 0))],
        out_specs=[pl.BlockSpec((tq, dq), lambda i: (i, 0)),
                   pl.BlockSpec((tq, dkv), lambda i: (i, 0))],
        out_shape=[jax.ShapeDtypeStruct((T, dq), F32),
                   jax.ShapeDtypeStruct((T, dkv), BF16)],
        compiler_params=_params(("parallel",)),
    )(qkv, kc, vc, tab, ovt)


def _flash_chunk(k, vt, q, bias, m_ref, l_ref, acc_ref, scale):
    s = _dot_nt(k, q) * scale + bias
    m_old = m_ref[...]
    m_new = jnp.maximum(m_old, jnp.max(s, axis=0, keepdims=True))
    alpha = jnp.exp2(m_old - m_new)
    p = jnp.exp2(s - m_new)
    l_ref[...] = alpha * l_ref[...] + jnp.sum(p, axis=0, keepdims=True)
    acc_ref[...] = alpha * acc_ref[...] + _dot(vt, p.astype(BF16))
    m_ref[...] = m_new


def _sel_win_kernel(q_ref, selb_ref, ks_ref, vst_ref, kw_ref, vwt_ref, e2_ref, tab_ref, ocmp_ref,
                    gl_ref, gnw_ref, o_ref, m_ref, l_ref, acc_ref, *, n_qb, scale):
    c = pl.program_id(0) % n_qb
    tq = q_ref.shape[0]
    rows = GQA * tq
    n_back = WINDOW // tq
    near = WINDOW + tq
    j_io = lax.broadcasted_iota(jnp.int32, (tq, tq), 0)
    i_io = lax.broadcasted_iota(jnp.int32, (tq, tq), 1)
    dij = i_io - j_io
    dij4 = jnp.concatenate([dij] * GQA, axis=1)
    causal = dij4 >= 0
    pad_col = jnp.where(lax.broadcasted_iota(jnp.int32, (rows, HEAD_DIM), 1) == HEAD_DIM - 1, 1.0, 0.0).astype(BF16)
    sig = jax.nn.sigmoid(gl_ref[...])
    near0 = pl.multiple_of(c * tq, tq)
    n_far = jnp.maximum(c - n_back, 0)
    n_full = n_far // n_back
    n_rem = n_far - n_full * n_back

    def hsl(h):
        return slice(h * HEAD_DIM, (h + 1) * HEAD_DIM)

    def far0(i):
        return pl.multiple_of(WINDOW + i * WINDOW, WINDOW)

    def reset():
        m_ref[...] = jnp.full(m_ref.shape, NEG_BIG, F32)
        l_ref[...] = jnp.zeros(l_ref.shape, F32)
        acc_ref[...] = jnp.zeros(acc_ref.shape, F32)

    def result():
        ot = acc_ref[...] / l_ref[...]
        return [ot[:, g * tq:(g + 1) * tq].T for g in range(GQA)]

    for h in range(N_KV_HEADS):
        q4 = jnp.concatenate([q_ref[:, (h * GQA + g) * HEAD_DIM:(h * GQA + g + 1) * HEAD_DIM]
                              for g in range(GQA)], axis=0)
        sb4 = jnp.concatenate([selb_ref[:, hsl(h)]] * GQA, axis=0)
        q_aug = jnp.concatenate([q4, sb4 + pad_col], axis=1)
        q_win = jnp.concatenate([q4, pad_col], axis=1)
        tabs = [tab_ref[h * GQA + g:h * GQA + g + 1, :] * LOG2E for g in range(GQA)]
        d0 = jnp.concatenate([_bias_from_dist(t, dij) for t in tabs], axis=1)
        d1 = jnp.concatenate([_bias_from_dist(t, dij + tq) for t in tabs], axis=1)
        far = jnp.concatenate([t[:, HEAD_DIM - 1:HEAD_DIM] + jnp.zeros((1, tq), F32) for t in tabs], axis=1)
        far_t = jnp.broadcast_to(far, (tq, rows))
        diag = jnp.where(causal, d0, NEG_BIG)
        bias_sel = jnp.concatenate([far_t] * (n_back - 1) + [d1, diag], axis=0)
        bias_win = jnp.concatenate([jnp.where(dij4 < 0, far_t, NEG_BIG)] + [far_t] * (n_back - 2) + [d1, diag],
                                   axis=0)

        reset()
        e2_near = e2_ref[pl.ds(near0, near), :]
        k_near = jnp.concatenate([ks_ref[pl.ds(near0, near), hsl(h)], e2_near], axis=1)
        _flash_chunk(k_near, vst_ref[hsl(h), pl.ds(near0, near)], q_aug, bias_sel, m_ref, l_ref, acc_ref, scale)

        def far_chunk(i, bias):
            r0 = far0(i)
            k = jnp.concatenate([ks_ref[pl.ds(r0, WINDOW), hsl(h)], e2_ref[pl.ds(r0, WINDOW), :]], axis=1)
            _flash_chunk(k, vst_ref[hsl(h), pl.ds(r0, WINDOW)], q_aug, bias, m_ref, l_ref, acc_ref, scale)

        def body(i, carry):
            far_chunk(i, far)
            return carry

        lax.fori_loop(0, n_full, body, 0)

        @pl.when(n_rem > 0)
        def _():
            live = lax.broadcasted_iota(jnp.int32, (WINDOW, rows), 0) < n_rem * tq
            far_chunk(n_full, jnp.where(live, jnp.broadcast_to(far, (WINDOW, rows)), NEG_BIG))

        o_sel = result()

        reset()
        kw_near = jnp.concatenate([kw_ref[pl.ds(near0, near), hsl(h)], e2_near], axis=1)
        _flash_chunk(kw_near, vwt_ref[hsl(h), pl.ds(near0, near)], q_win, bias_win, m_ref, l_ref, acc_ref, scale)
        o_win = result()

        for g in range(GQA):
            hd = h * GQA + g
            o = (sig[:, 3 * hd:3 * hd + 1] * ocmp_ref[:, hsl(hd)]
                 + sig[:, 3 * hd + 1:3 * hd + 2] * o_sel[g]
                 + sig[:, 3 * hd + 2:3 * hd + 3] * o_win[g])
            y = o * lax.rsqrt(jnp.mean(o * o, axis=-1, keepdims=True) + EPS)
            o_ref[:, hsl(hd)] = (y * gnw_ref[:, hsl(hd)]).astype(o_ref.dtype)


def sel_win(qkv, selb, ksp, vstp, kwp, vwtp, e2p, tab, ocmp, gl, gnw, B, S, tq=2 * Q_BLOCK):
    T = qkv.shape[0]
    n_qb = S // tq
    dq = N_Q_HEADS * HEAD_DIM
    dkv = N_KV_HEADS * HEAD_DIM
    SP = S + WINDOW
    kern = functools.partial(_sel_win_kernel, n_qb=n_qb, scale=HEAD_DIM ** -0.5 * LOG2E)
    k_spec = pl.BlockSpec((SP, dkv), lambda i: (i // n_qb, 0))
    vt_spec = pl.BlockSpec((dkv, SP), lambda i: (i // n_qb, 0))
    return pl.pallas_call(
        kern,
        grid=(T // tq,),
        in_specs=[pl.BlockSpec((tq, dq), lambda i: (i, 0)),
                  pl.BlockSpec((tq, dkv), lambda i: (i, 0)),
                  k_spec, vt_spec, k_spec, vt_spec,
                  pl.BlockSpec((SP, HEAD_DIM), lambda i: (0, 0)),
                  pl.BlockSpec(tab.shape, lambda i: (0, 0)),
                  pl.BlockSpec((tq, dq), lambda i: (i, 0)),
                  pl.BlockSpec((tq, HEAD_DIM), lambda i: (i, 0)),
                  pl.BlockSpec((1, dq), lambda i: (0, 0))],
        out_specs=pl.BlockSpec((tq, dq), lambda i: (i, 0)),
        out_shape=jax.ShapeDtypeStruct((T, dq), BF16),
        scratch_shapes=[pltpu.VMEM((1, GQA * tq), F32), pltpu.VMEM((1, GQA * tq), F32),
                        pltpu.VMEM((HEAD_DIM, GQA * tq), F32)],
        compiler_params=_params(("parallel",)),
    )(qkv, selb, ksp, vstp, kwp, vwtp, e2p, tab, ocmp, gl, gnw)


def _conv_kernel(b_ref, c_ref, h_ref, cp_ref, hp_ref, cw_ref, cb_ref, gnw_ref, o_ref, u_ref, *, tiles_per_seq):
    tm = b_ref.shape[0]
    first = (pl.program_id(0) % tiles_per_seq) == 0
    u_prev = cp_ref[...] * hp_ref[...]
    u_ref[0:8, :] = jnp.where(first, 0.0, u_prev)
    u = c_ref[...] * h_ref[...]
    u_ref[8:8 + tm, :] = u
    y = (cw_ref[0:1, :] * u_ref[6:6 + tm, :] + cw_ref[1:2, :] * u_ref[7:7 + tm, :]
         + cw_ref[2:3, :] * u + cb_ref[...])
    o = b_ref[...] * y
    n_groups = o.shape[1] // HEAD_DIM
    for g in range(n_groups):
        sl = slice(g * HEAD_DIM, (g + 1) * HEAD_DIM)
        og = o[:, sl]
        yg = og * lax.rsqrt(jnp.mean(og * og, axis=-1, keepdims=True) + EPS)
        o_ref[:, sl] = (yg * gnw_ref[:, sl]).astype(o_ref.dtype)


def conv_mixer(gc, conv_w, conv_b, gnw, S, tm):
    T = gc.shape[0]
    dc = conv_w.shape[1]
    tps = S // tm
    kern = functools.partial(_conv_kernel, tiles_per_seq=tps)
    prev = lambda col: pl.BlockSpec((8, dc), lambda i, col=col: (jnp.maximum(i * (tm // 8) - 1, 0), col))
    cur = lambda col: pl.BlockSpec((tm, dc), lambda i, col=col: (i, col))
    return pl.pallas_call(
        kern,
        grid=(T // tm,),
        in_specs=[cur(0), cur(1), cur(2), prev(1), prev(2),
                  pl.BlockSpec((8, dc), lambda i: (0, 0)),
                  pl.BlockSpec((1, dc), lambda i: (0, 0)),
                  pl.BlockSpec((1, dc), lambda i: (0, 0))],
        out_specs=pl.BlockSpec((tm, dc), lambda i: (i, 0)),
        out_shape=jax.ShapeDtypeStruct((T, dc), BF16),
        scratch_shapes=[pltpu.VMEM((tm + 8, dc), F32)],
        compiler_params=_params(("parallel",)),
    )(gc, gc, gc, gc, gc, conv_w, conv_b, gnw)


def _out_proj_kernel(ma_ref, mc_ref, wa_ref, wc_ref, x_ref, o_ref):
    o_ref[...] = x_ref[...] + _dot(ma_ref[...], wa_ref[...]) + _dot(mc_ref[...], wc_ref[...])


def out_proj(ma, mc, wa, wc, x, tm, tn):
    T, da = ma.shape
    dc = mc.shape[1]
    D = x.shape[1]
    return pl.pallas_call(
        _out_proj_kernel,
        grid=(T // tm, D // tn),
        in_specs=[pl.BlockSpec((tm, da), lambda i, j: (i, 0)),
                  pl.BlockSpec((tm, dc), lambda i, j: (i, 0)),
                  pl.BlockSpec((da, tn), lambda i, j: (0, j)),
                  pl.BlockSpec((dc, tn), lambda i, j: (0, j)),
                  pl.BlockSpec((tm, tn), lambda i, j: (i, j))],
        out_specs=pl.BlockSpec((tm, tn), lambda i, j: (i, j)),
        out_shape=jax.ShapeDtypeStruct((T, D), F32),
        compiler_params=_params(("parallel", "arbitrary")),
    )(ma, mc, wa, wc, x)


def _peer_route_kernel(q_ref, sk_ref, u_ref, v_ref, e_ref, g_ref, ub_ref, vb_ref, sv_ref, si_ref):
    ub_ref[...] = u_ref[...].astype(BF16)
    vb_ref[...] = v_ref[...].astype(BF16)
    tt = q_ref.shape[0]
    K = PEER_TOPK
    n_io = lax.broadcasted_iota(jnp.int32, (N_KEYS, tt), 0).astype(F32)
    r16 = lax.broadcasted_iota(jnp.int32, (16, tt), 0).astype(F32)
    r8 = lax.broadcasted_iota(jnp.int32, (8, tt), 0).astype(F32)
    ninf = -jnp.inf

    for h in range(PEER_HEADS):
        for c in range(2):
            col = (h * 2 + c) * HEAD_DIM
            s = _dot_nt(sk_ref[h * 2 + c], q_ref[:, col:col + HEAD_DIM])
            for k in range(K):
                m = jnp.max(s, axis=0, keepdims=True)
                idx = jnp.min(jnp.where(s == m, n_io, float(N_KEYS)), axis=0, keepdims=True)
                s = jnp.where(n_io == idx, ninf, s)
                sv_ref[c, k:k + 1, :] = m
                si_ref[c, k:k + 1, :] = idx
        sv0, sv1 = sv_ref[0], sv_ref[1]
        si0, si1 = si_ref[0], si_ref[1]
        cands, flats, exps = [], [], []
        for a, nb, rows in ((0, 16, 16), (1, 8, 8), (2, 5, 8), (3, 4, 8)):
            r_io = r16 if rows == 16 else r8
            val = sv0[a:a + 1, :] + sv1[0:rows, :]
            cands.append(jnp.where(r_io < nb, val, ninf))
            flats.append(a * 16.0 + r_io)
            exps.append(si0[a:a + 1, :] * float(N_KEYS) + si1[0:rows, :])
        for b, lo, hi, rows in ((0, 4, 16, 16), (1, 4, 8, 8), (2, 4, 5, 8)):
            r_io = r16 if rows == 16 else r8
            val = sv0[0:rows, :] + sv1[b:b + 1, :]
            cands.append(jnp.where((r_io >= lo) & (r_io < hi), val, ninf))
            flats.append(r_io * 16.0 + b)
            exps.append(si0[0:rows, :] * float(N_KEYS) + si1[b:b + 1, :])
        cand = jnp.concatenate(cands, axis=0)
        flat = jnp.concatenate(flats, axis=0)
        expt = jnp.concatenate(exps, axis=0)
        cvs, exs = [], []
        for k in range(K):
            m = jnp.max(cand, axis=0, keepdims=True)
            fsel = jnp.min(jnp.where(cand == m, flat, 1e9), axis=0, keepdims=True)
            hit = flat == fsel
            exs.append(jnp.max(jnp.where(hit, expt, -1.0), axis=0, keepdims=True))
            cand = jnp.where(hit, ninf, cand)
            cvs.append(m)
        cv = jnp.concatenate(cvs, axis=0)
        ex = jnp.concatenate(exs, axis=0)
        ev = jnp.exp(cv - jnp.max(cv, axis=0, keepdims=True))
        gates = ev / jnp.sum(ev, axis=0, keepdims=True)
        e_ref[0, h * K:(h + 1) * K, :] = ex.astype(jnp.int32)
        g_ref[0, h * K:(h + 1) * K, :] = gates


def peer_route(qp, subkeys, u, v):
    T = qp.shape[0]
    tt = 128
    P = PEER_HEADS * PEER_TOPK
    nt = T // tt
    E, D = u.shape
    slab = E // nt
    assert slab * nt == E and slab % 16 == 0
    tab_spec = pl.BlockSpec((slab, D), lambda i: (i, 0))
    return pl.pallas_call(
        _peer_route_kernel,
        grid=(nt,),
        in_specs=[pl.BlockSpec((tt, qp.shape[1]), lambda i: (i, 0)),
                  pl.BlockSpec(subkeys.shape, lambda i: (0, 0, 0)),
                  tab_spec, tab_spec],
        out_specs=[pl.BlockSpec((1, P, tt), lambda i: (i, 0, 0))] * 2 + [tab_spec, tab_spec],
        out_shape=[jax.ShapeDtypeStruct((nt, P, tt), jnp.int32),
                   jax.ShapeDtypeStruct((nt, P, tt), F32),
                   jax.ShapeDtypeStruct((E, D), BF16), jax.ShapeDtypeStruct((E, D), BF16)],
        scratch_shapes=[pltpu.VMEM((2, PEER_TOPK, tt), F32), pltpu.VMEM((2, PEER_TOPK, tt), F32)],
        compiler_params=_params(("parallel",)),
    )(qp, subkeys, u, v)


def _peer_gbuild_kernel(e_ref, g_ref, o_ref, i1_ref, i2_ref, gt_ref, s_ref):
    tt = e_ref.shape[2]
    e = e_ref[0].T
    i1_ref[...] = e >> 7
    i2_ref[...] = e & (N_KEYS - 1)
    gt_ref[...] = g_ref[0].T
    P = e.shape[1]
    k_io = lax.broadcasted_iota(jnp.int32, (N_KEYS, P), 0)
    sub = 8
    half = tt // 2

    def build(t0):
        r1s = i1_ref[t0:t0 + half, :]
        r2s = i2_ref[t0:t0 + half, :]
        rgs = gt_ref[t0:t0 + half, :]
        for u in range(half):
            lhs = jnp.where(k_io == r1s[u:u + 1, :], rgs[u:u + 1, :], 0.0).astype(BF16)
            rhs = jnp.where(k_io == r2s[u:u + 1, :], 1.0, 0.0).astype(BF16)
            g_t = _dot_nt(lhs, rhs)
            for a in range(N_KEYS // sub):
                row0 = (a * tt + t0 + u) * sub
                s_ref[row0:row0 + sub, :] = g_t[a * sub:(a + 1) * sub, :]

    def regroup(t0):
        for i1 in range(N_KEYS):
            a, r = divmod(i1, sub)
            o_ref[i1, t0:t0 + half, :] = s_ref[pl.ds((a * tt + t0) * sub + r, half, stride=sub), :].astype(o_ref.dtype)

    build(0)
    build(half)
    regroup(0)
    regroup(half)


def peer_gbuild(ex, gates):
    nt, P, tt = ex.shape
    T = nt * tt
    return pl.pallas_call(
        _peer_gbuild_kernel,
        grid=(nt,),
        in_specs=[pl.BlockSpec((1, P, tt), lambda i: (i, 0, 0))] * 2,
        out_specs=pl.BlockSpec((N_KEYS, tt, N_KEYS), lambda i: (0, i, 0)),
        out_shape=jax.ShapeDtypeStruct((N_KEYS, T, N_KEYS), BF16),
        scratch_shapes=[pltpu.VMEM((tt, P), jnp.int32), pltpu.VMEM((tt, P), jnp.int32),
                        pltpu.VMEM((tt, P), F32), pltpu.VMEM((tt * N_KEYS, N_KEYS), F32)],
        compiler_params=_params(("parallel",)),
    )(ex, gates)


def _peer_dense_kernel(h_ref, nw_ref, fw_ref, u_ref, v_ref, gm_ref, o_ref, hn_ref, acc_ref):
    j = pl.program_id(1)

    @pl.when(j == 0)
    def _():
        x = h_ref[...]
        y = x * lax.rsqrt(jnp.mean(x * x, axis=-1, keepdims=True) + EPS)
        hn_ref[...] = (y * nw_ref[...]).astype(BF16)
        acc_ref[...] = jnp.zeros(acc_ref.shape, F32)

    a = _dot_nt(hn_ref[...], u_ref[...])
    gm = jnp.concatenate([gm_ref[k] for k in range(gm_ref.shape[0])], axis=1)
    w = (gm.astype(F32) * jax.nn.gelu(a)).astype(BF16)
    acc_ref[...] += _dot(w, v_ref[...])

    @pl.when(j == pl.num_programs(1) - 1)
    def _():
        x = h_ref[...] + acc_ref[...]
        y = x * lax.rsqrt(jnp.mean(x * x, axis=-1, keepdims=True) + EPS)
        o_ref[...] = y * fw_ref[...]


def peer_dense(h, ffn_nw, final_nw, u, v, gm, tt, ec):
    T, D = h.shape
    E = u.shape[0]
    return pl.pallas_call(
        _peer_dense_kernel,
        grid=(T // tt, E // ec),
        in_specs=[pl.BlockSpec((tt, D), lambda i, j: (i, 0)),
                  pl.BlockSpec((1, D), lambda i, j: (0, 0)),
                  pl.BlockSpec((1, D), lambda i, j: (0, 0)),
                  pl.BlockSpec((ec, D), lambda i, j: (j, 0)),
                  pl.BlockSpec((ec, D), lambda i, j: (j, 0)),
                  pl.BlockSpec((ec // N_KEYS, tt, N_KEYS), lambda i, j: (j, i, 0))],
        out_specs=pl.BlockSpec((tt, D), lambda i, j: (i, 0)),
        out_shape=jax.ShapeDtypeStruct((T, D), F32),
        scratch_shapes=[pltpu.VMEM((tt, D), BF16), pltpu.VMEM((tt, D), F32)],
        compiler_params=_params(("parallel", "arbitrary")),
    )(h, ffn_nw.reshape(1, D), final_nw.reshape(1, D), u, v, gm)


def _t5_bucket_np(n_dist):
    d = np.arange(n_dist)
    max_exact = NUM_BUCKETS // 2
    nf = np.maximum(d, 1).astype(np.float64)
    large = max_exact + (np.log(nf / max_exact) / math.log(MAX_DISTANCE / max_exact)
                         * (NUM_BUCKETS - max_exact)).astype(np.int64)
    large = np.minimum(large, NUM_BUCKETS - 1)
    return np.where(d < max_exact, d, large).astype(np.int32)


def _overlap_t_np(S):
    n_c = (S - L_CMP) // STRIDE_CMP + 1
    n_sel = S // L_SEL
    pos = np.arange(n_c)[:, None] * STRIDE_CMP + np.arange(L_CMP)[None, :]
    m = np.zeros((n_c + 1, n_sel), np.float32)
    np.add.at(m, (np.repeat(np.arange(n_c), L_CMP), (pos // L_SEL).reshape(-1)), 1.0 / L_CMP)
    return np.ascontiguousarray(m.T)


def _block_onehot_padded_np(S):
    assert S // L_SEL < HEAD_DIM
    e2 = np.zeros((WINDOW + S, HEAD_DIM), np.float32)
    e2[WINDOW + np.arange(S), np.arange(S) // L_SEL] = 1.0
    e2[:WINDOW, HEAD_DIM - 1] = SEL_OFF
    return e2


def nsa_conv_mix(xt, B, S, attn_norm_w, w_in, w_cmp_k, w_cmp_v, cmp_pos, conv_w, conv_b,
                 attn_gnw, conv_gnw, rel_bias, tm=1024, tm_conv=512):
    T, D = xt.shape
    dq = N_Q_HEADS * HEAD_DIM
    dkv = N_KV_HEADS * HEAD_DIM
    n_attn = dq + 6 * dkv
    n_gate = 3 * N_Q_HEADS
    dc = (w_in.shape[1] - n_attn - n_gate) // 3
    w_attn = w_in[:, :n_attn].astype(BF16)
    w_gate = jnp.pad(w_in[:, n_attn:n_attn + n_gate], ((0, 0), (0, HEAD_DIM - n_gate))).astype(BF16)
    w_conv = w_in[:, n_attn + n_gate:].astype(BF16)

    qkv = norm_matmul(xt, attn_norm_w, w_attn, BF16, tm, n_attn // 2)
    gl = norm_matmul(xt, attn_norm_w, w_gate, F32, tm, HEAD_DIM)
    gc = norm_matmul(xt, attn_norm_w, w_conv, F32, tm, dc)

    n16 = S // STRIDE_CMP
    k16 = qkv[:, dq:dq + dkv].reshape(B * n16, STRIDE_CMP * dkv)
    v16 = qkv[:, dq + dkv:dq + 2 * dkv].reshape(B * n16, STRIDE_CMP * dkv)

    def wbig(w, lo):
        wl = w[lo:lo + STRIDE_CMP]
        eye = jnp.eye(N_KV_HEADS, dtype=w.dtype)
        return jnp.einsum('lde,hg->lhdge', wl, eye).reshape(STRIDE_CMP * dkv, dkv).astype(BF16)

    def posrow(lo):
        p = cmp_pos[lo:lo + STRIDE_CMP]
        return jnp.broadcast_to(p[:, None, :], (STRIDE_CMP, N_KV_HEADS, HEAD_DIM)).reshape(1, STRIDE_CMP * dkv)

    kc, vc = compress(k16, v16, posrow(0), posrow(STRIDE_CMP),
                      wbig(w_cmp_k, 0), wbig(w_cmp_k, STRIDE_CMP),
                      wbig(w_cmp_v, 0), wbig(w_cmp_v, STRIDE_CMP), B)

    tab = rel_bias[_t5_bucket_np(HEAD_DIM)].T
    ovt = jnp.asarray(_overlap_t_np(S), BF16)
    e2p = jnp.asarray(_block_onehot_padded_np(S), BF16)

    ocmp, selb = cmp_select(qkv, kc, vc, tab, ovt, B, S)
    def keys_padded(col):
        k = qkv[:, col:col + dkv].reshape(B, S, dkv)
        return jnp.pad(k, ((0, 0), (WINDOW, 0), (0, 0))).reshape(B * (WINDOW + S), dkv)

    def values_t_padded(col):
        v = qkv[:, col:col + dkv].reshape(B, S, dkv).transpose(0, 2, 1)
        return jnp.pad(v, ((0, 0), (0, 0), (WINDOW, 0))).reshape(B * dkv, WINDOW + S)

    attn = sel_win(qkv, selb, keys_padded(dq + 2 * dkv), values_t_padded(dq + 3 * dkv),
                   keys_padded(dq + 4 * dkv), values_t_padded(dq + 5 * dkv), e2p, tab, ocmp, gl,
                   attn_gnw.reshape(1, dq), B, S)
    cw8 = jnp.pad(conv_w, ((0, 8 - conv_w.shape[0]), (0, 0)))
    conv = conv_mixer(gc, cw8, conv_b.reshape(1, dc), conv_gnw.reshape(1, dc), S, tm_conv)
    return attn, conv


def peer_block(h, ffn_nw, final_nw, peer_wq, peer_subkeys, peer_u, peer_v, tm=1024, tt=512, ec=1024):
    T, D = h.shape
    qp = norm_matmul(h, ffn_nw, peer_wq.astype(BF16), BF16, tm, 1024)
    sk = peer_subkeys.reshape(PEER_HEADS * 2, N_KEYS, peer_subkeys.shape[-1]).astype(BF16)
    ex, gates, u_bf, v_bf = peer_route(qp, sk, peer_u, peer_v)
    gm = peer_gbuild(ex, gates)
    return peer_dense(h, ffn_nw, final_nw, u_bf, v_bf, gm, tt, ec)


def kernel(x, attn_norm_w, w_in, w_cmp_k, w_cmp_v, cmp_pos, conv_w, conv_b, attn_group_norm_w,
           conv_group_norm_w, w_out, rel_bias, ffn_norm_w, peer_wq, peer_subkeys, peer_u, peer_v,
           final_norm_w):
    B, S, D = x.shape
    T = B * S
    xt = x.reshape(T, D)
    attn, conv = nsa_conv_mix(xt, B, S, attn_norm_w[0], w_in[0], w_cmp_k[0], w_cmp_v[0], cmp_pos[0],
                              conv_w[0], conv_b[0], attn_group_norm_w[0], conv_group_norm_w[0], rel_bias)
    da = attn.shape[1]
    wo = w_out[0].astype(BF16)
    h = out_proj(attn, conv, wo[:da], wo[da:], xt, 1024, 1024)
    out = peer_block(h, ffn_norm_w[0], final_norm_w, peer_wq[0], peer_subkeys[0], peer_u[0], peer_v[0])
    return out.reshape(B, S, D)
```

```python
import functools
import math

import jax
import jax.numpy as jnp
import numpy as np
from jax import lax
from jax.experimental import pallas as pl
from jax.experimental.pallas import tpu as pltpu

F32 = jnp.float32
BF16 = jnp.bfloat16

HEAD_DIM = 128
N_KV_HEADS = 2
GQA = 4
N_Q_HEADS = N_KV_HEADS * GQA
L_CMP = 32
STRIDE_CMP = 16
L_SEL = 64
N_SEL = 16
WINDOW = 512
Q_BLOCK = 128
FORCED_SCORE = float(GQA + 1)
NUM_BUCKETS = 32
MAX_DISTANCE = 128
PEER_HEADS = 8
N_KEYS = 128
PEER_TOPK = 16
EPS = 1e-6
NEG_BIG = -1e30
SEL_OFF = -float(2 ** 30)
LOG2E = math.log2(math.e)
VMEM_LIMIT = 56 * 1024 * 1024


def _dot(a, b):
    return jnp.dot(a, b, preferred_element_type=F32)


def _dot_nt(a, b):
    return lax.dot_general(a, b, (((1,), (1,)), ((), ())), preferred_element_type=F32)


def _params(sem, vmem=VMEM_LIMIT):
    return pltpu.CompilerParams(dimension_semantics=sem, vmem_limit_bytes=vmem)


def _norm_matmul_kernel(x_ref, nw_ref, w_ref, o_ref, xn_ref):
    @pl.when(pl.program_id(1) == 0)
    def _():
        x = x_ref[...]
        y = x * lax.rsqrt(jnp.mean(x * x, axis=-1, keepdims=True) + EPS)
        xn_ref[...] = (y * nw_ref[...]).astype(BF16)

    o_ref[...] = _dot(xn_ref[...], w_ref[...]).astype(o_ref.dtype)


def norm_matmul(x, norm_w, w, out_dtype, tm, tn):
    T, D = x.shape
    N = w.shape[1]
    return pl.pallas_call(
        _norm_matmul_kernel,
        grid=(T // tm, N // tn),
        in_specs=[pl.BlockSpec((tm, D), lambda i, j: (i, 0)),
                  pl.BlockSpec((1, D), lambda i, j: (0, 0)),
                  pl.BlockSpec((D, tn), lambda i, j: (0, j))],
        out_specs=pl.BlockSpec((tm, tn), lambda i, j: (i, j)),
        out_shape=jax.ShapeDtypeStruct((T, N), out_dtype),
        scratch_shapes=[pltpu.VMEM((tm, D), BF16)],
        compiler_params=_params(("parallel", "arbitrary")),
    )(x, norm_w.reshape(1, D), w)


def _split_cast_kernel(w_ref, a_ref, g_ref, c_ref, *, n_attn, n_gate):
    w = w_ref[...]
    a_ref[...] = w[:, :n_attn].astype(BF16)
    g = w[:, n_attn:n_attn + HEAD_DIM]
    lane = lax.broadcasted_iota(jnp.int32, g.shape, 1)
    g_ref[...] = jnp.where(lane < n_gate, g, 0.0).astype(BF16)
    c_ref[...] = w[:, n_attn + n_gate:].astype(BF16)


def split_cast_weights(w, n_attn, n_gate, tr=256):
    D, N = w.shape
    n_conv = N - n_attn - n_gate
    kern = functools.partial(_split_cast_kernel, n_attn=n_attn, n_gate=n_gate)
    row = lambda i: (i, 0)
    return pl.pallas_call(
        kern,
        grid=(D // tr,),
        in_specs=[pl.BlockSpec((tr, N), row)],
        out_specs=[pl.BlockSpec((tr, n_attn), row), pl.BlockSpec((tr, HEAD_DIM), row),
                   pl.BlockSpec((tr, n_conv), row)],
        out_shape=[jax.ShapeDtypeStruct((D, n_attn), BF16), jax.ShapeDtypeStruct((D, HEAD_DIM), BF16),
                   jax.ShapeDtypeStruct((D, n_conv), BF16)],
        compiler_params=_params(("parallel",)),
    )(w)


def _compress_kernel(k_ref, v_ref, pa_ref, pb_ref, wka_ref, wkb_ref, wva_ref, wvb_ref, kc_ref, vc_ref):
    def one(x_ref, wa_ref, wb_ref, o_ref):
        x = x_ref[...].astype(F32)
        a = _dot((x + pa_ref[...]).astype(BF16), wa_ref[...])
        b = _dot((x + pb_ref[...]).astype(BF16), wb_ref[...])
        n = b.shape[0]
        o_ref[...] = (a + pltpu.roll(b, n - 1, 0)).astype(o_ref.dtype)

    one(k_ref, wka_ref, wkb_ref, kc_ref)
    one(v_ref, wva_ref, wvb_ref, vc_ref)


def compress(k16, v16, pos_a, pos_b, wka, wkb, wva, wvb, B):
    R, C = k16.shape
    nb = R // B
    dkv = wka.shape[1]
    full = lambda shp: pl.BlockSpec(shp, lambda b: (0, 0))
    return pl.pallas_call(
        _compress_kernel,
        grid=(B,),
        in_specs=[pl.BlockSpec((nb, C), lambda b: (b, 0)),
                  pl.BlockSpec((nb, C), lambda b: (b, 0)),
                  full((1, C)), full((1, C)),
                  full((C, dkv)), full((C, dkv)), full((C, dkv)), full((C, dkv))],
        out_specs=[pl.BlockSpec((nb, dkv), lambda b: (b, 0))] * 2,
        out_shape=[jax.ShapeDtypeStruct((R, dkv), BF16)] * 2,
        compiler_params=_params(("parallel",)),
    )(k16, v16, pos_a, pos_b, wka, wkb, wva, wvb)


def _bias_from_dist(tab_row, dist):
    idx = jnp.clip(dist, 0, 127)
    w = tab_row.shape[1]
    tab = jnp.broadcast_to(tab_row, (idx.shape[0], w))
    parts = [jnp.take_along_axis(tab, idx[:, k:k + w], axis=1) for k in range(0, idx.shape[1], w)]
    return parts[0] if len(parts) == 1 else jnp.concatenate(parts, axis=1)


def _cmp_select_kernel(q_ref, kc_ref, vc_ref, tab_ref, ovt_ref, ocmp_ref, selb_ref, *, n_qb, scale):
    c = pl.program_id(0) % n_qb
    tq = q_ref.shape[0]
    n_c = kc_ref.shape[0]
    n_sel = ovt_ref.shape[0]
    t0 = c * tq
    t_col = t0 + lax.broadcasted_iota(jnp.int32, (tq, n_c), 0)
    n_row = lax.broadcasted_iota(jnp.int32, (tq, n_c), 1)
    dist = t_col - (n_row * STRIDE_CMP + (L_CMP - 1))
    valid = dist >= 0

    j_io = lax.broadcasted_iota(jnp.int32, (n_sel, tq), 0)
    t_io = t0 + lax.broadcasted_iota(jnp.int32, (n_sel, tq), 1)
    blk_t = t_io // L_SEL
    forced = (j_io == 0) | (j_io == blk_t) | (j_io == blk_t - 1)
    causal_blk = j_io * L_SEL <= t_io

    for h in range(N_KV_HEADS):
        kc = kc_ref[:, h * HEAD_DIM:(h + 1) * HEAD_DIM]
        vc = vc_ref[:, h * HEAD_DIM:(h + 1) * HEAD_DIM]
        psum = jnp.zeros((tq, n_c), F32)
        for g in range(GQA):
            hd = h * GQA + g
            qh = q_ref[:, hd * HEAD_DIM:(hd + 1) * HEAD_DIM]
            bias = _bias_from_dist(tab_ref[hd:hd + 1, :], dist)
            s = _dot_nt(qh, kc) * scale + bias
            s = jnp.where(valid, s, NEG_BIG)
            m = jnp.max(s, axis=-1, keepdims=True)
            e = jnp.where(valid, jnp.exp(s - m), 0.0)
            d = jnp.sum(e, axis=-1, keepdims=True)
            p = e / jnp.where(d > 0, d, 1.0)
            ocmp_ref[:, hd * HEAD_DIM:(hd + 1) * HEAD_DIM] = _dot(p.astype(BF16), vc)
            psum = psum + p
        p_hi = psum.astype(BF16)
        p_lo = (psum - p_hi.astype(F32)).astype(BF16)
        ovt = ovt_ref[...]
        imp = _dot_nt(ovt, p_hi) + _dot_nt(ovt, p_lo)
        score = jnp.where(forced, FORCED_SCORE, jnp.where(causal_blk, imp, -1.0))
        rank = jnp.zeros((n_sel, tq), F32)
        for jp in range(n_sel):
            row = score[jp:jp + 1, :]
            rank = rank + jnp.where(j_io > jp, jnp.where(row >= score, 1.0, 0.0),
                                    jnp.where(row > score, 1.0, 0.0))
        selb = jnp.where(rank < float(N_SEL), 0.0, SEL_OFF)
        if n_sel < HEAD_DIM:
            selb = jnp.concatenate([selb, jnp.zeros((HEAD_DIM - n_sel, tq), F32)], axis=0)
        selb_ref[:, h * HEAD_DIM:(h + 1) * HEAD_DIM] = selb.T.astype(BF16)


def cmp_select(qkv, kc, vc, tab, ovt, B, S, tq=2 * Q_BLOCK):
    T = qkv.shape[0]
    n_qb = S // tq
    n_c = kc.shape[0] // B
    dq = N_Q_HEADS * HEAD_DIM
    dkv = N_KV_HEADS * HEAD_DIM
    kern = functools.partial(_cmp_select_kernel, n_qb=n_qb, scale=HEAD_DIM ** -0.5)
    return pl.pallas_call(
        kern,
        grid=(T // tq,),
        in_specs=[pl.BlockSpec((tq, dq), lambda i: (i, 0)),
                  pl.BlockSpec((n_c, dkv), lambda i: (i // n_qb, 0)),
                  pl.BlockSpec((n_c, dkv), lambda i: (i // n_qb, 0)),
                  pl.BlockSpec(tab.shape, lambda i: (0, 0)),
                  pl.BlockSpec(ovt.shape, lambda i: (0, 0))],
        out_specs=[pl.BlockSpec((tq, dq), lambda i: (i, 0)),
                   pl.BlockSpec((tq, dkv), lambda i: (i, 0))],
        out_shape=[jax.ShapeDtypeStruct((T, dq), F32),
                   jax.ShapeDtypeStruct((T, dkv), BF16)],
        compiler_params=_params(("parallel",)),
    )(qkv, kc, vc, tab, ovt)


def _flash_chunk(k, vt, q, bias, m_ref, l_ref, acc_ref, scale):
    s = _dot_nt(k, q) * scale + bias
    m_old = m_ref[...]
    m_new = jnp.maximum(m_old, jnp.max(s, axis=0, keepdims=True))
    alpha = jnp.exp2(m_old - m_new)
    p = jnp.exp2(s - m_new)
    l_ref[...] = alpha * l_ref[...] + jnp.sum(p, axis=0, keepdims=True)
    acc_ref[...] = alpha * acc_ref[...] + _dot(vt, p.astype(BF16))
    m_ref[...] = m_new


def _sel_win_kernel(q_ref, selb_ref, ks_ref, vst_ref, kw_ref, vwt_ref, e2_ref, tab_ref, ocmp_ref,
                    gl_ref, gnw_ref, o_ref, m_ref, l_ref, acc_ref, *, n_qb, scale):
    c = pl.program_id(0) % n_qb
    tq = q_ref.shape[0]
    rows = GQA * tq
    n_back = WINDOW // tq
    near = WINDOW + tq
    j_io = lax.broadcasted_iota(jnp.int32, (tq, tq), 0)
    i_io = lax.broadcasted_iota(jnp.int32, (tq, tq), 1)
    dij = i_io - j_io
    dij4 = jnp.concatenate([dij] * GQA, axis=1)
    causal = dij4 >= 0
    pad_col = jnp.where(lax.broadcasted_iota(jnp.int32, (rows, HEAD_DIM), 1) == HEAD_DIM - 1, 1.0, 0.0).astype(BF16)
    sig = jax.nn.sigmoid(gl_ref[...])
    near0 = pl.multiple_of(c * tq, tq)
    n_far = jnp.maximum(c - n_back, 0)
    n_full = n_far // n_back
    n_rem = n_far - n_full * n_back

    def hsl(h):
        return slice(h * HEAD_DIM, (h + 1) * HEAD_DIM)

    def far0(i):
        return pl.multiple_of(WINDOW + i * WINDOW, WINDOW)

    def reset():
        m_ref[...] = jnp.full(m_ref.shape, NEG_BIG, F32)
        l_ref[...] = jnp.zeros(l_ref.shape, F32)
        acc_ref[...] = jnp.zeros(acc_ref.shape, F32)

    def result():
        ot = acc_ref[...] / l_ref[...]
        return [ot[:, g * tq:(g + 1) * tq].T for g in range(GQA)]

    for h in range(N_KV_HEADS):
        q4 = jnp.concatenate([q_ref[:, (h * GQA + g) * HEAD_DIM:(h * GQA + g + 1) * HEAD_DIM]
                              for g in range(GQA)], axis=0)
        sb4 = jnp.concatenate([selb_ref[:, hsl(h)]] * GQA, axis=0)
        q_aug = jnp.concatenate([q4, sb4 + pad_col], axis=1)
        q_win = jnp.concatenate([q4, pad_col], axis=1)
        tabs = [tab_ref[h * GQA + g:h * GQA + g + 1, :] * LOG2E for g in range(GQA)]
        d0 = jnp.concatenate([_bias_from_dist(t, dij) for t in tabs], axis=1)
        d1 = jnp.concatenate([_bias_from_dist(t, dij + tq) for t in tabs], axis=1)
        far = jnp.concatenate([t[:, HEAD_DIM - 1:HEAD_DIM] + jnp.zeros((1, tq), F32) for t in tabs], axis=1)
        far_t = jnp.broadcast_to(far, (tq, rows))
        diag = jnp.where(causal, d0, NEG_BIG)
        bias_sel = jnp.concatenate([far_t] * (n_back - 1) + [d1, diag], axis=0)
        bias_win = jnp.concatenate([jnp.where(dij4 < 0, far_t, NEG_BIG)] + [far_t] * (n_back - 2) + [d1, diag],
                                   axis=0)

        reset()
        e2_near = e2_ref[pl.ds(near0, near), :]
        k_near = jnp.concatenate([ks_ref[pl.ds(near0, near), hsl(h)], e2_near], axis=1)
        _flash_chunk(k_near, vst_ref[hsl(h), pl.ds(near0, near)], q_aug, bias_sel, m_ref, l_ref, acc_ref, scale)

        def far_chunk(i, bias):
            r0 = far0(i)
            k = jnp.concatenate([ks_ref[pl.ds(r0, WINDOW), hsl(h)], e2_ref[pl.ds(r0, WINDOW), :]], axis=1)
            _flash_chunk(k, vst_ref[hsl(h), pl.ds(r0, WINDOW)], q_aug, bias, m_ref, l_ref, acc_ref, scale)

        def body(i, carry):
            far_chunk(i, far)
            return carry

        lax.fori_loop(0, n_full, body, 0)

        @pl.when(n_rem > 0)
        def _():
            if n_back == 2:
                r0 = far0(n_full)
                k = jnp.concatenate([ks_ref[pl.ds(r0, tq), hsl(h)], e2_ref[pl.ds(r0, tq), :]], axis=1)
                _flash_chunk(k, vst_ref[hsl(h), pl.ds(r0, tq)], q_aug, far, m_ref, l_ref, acc_ref, scale)
            else:
                live = lax.broadcasted_iota(jnp.int32, (WINDOW, rows), 0) < n_rem * tq
                far_chunk(n_full, jnp.where(live, jnp.broadcast_to(far, (WINDOW, rows)), NEG_BIG))

        o_sel = result()

        reset()
        kw_near = jnp.concatenate([kw_ref[pl.ds(near0, near), hsl(h)], e2_near], axis=1)
        _flash_chunk(kw_near, vwt_ref[hsl(h), pl.ds(near0, near)], q_win, bias_win, m_ref, l_ref, acc_ref, scale)
        o_win = result()

        for g in range(GQA):
            hd = h * GQA + g
            o = (sig[:, 3 * hd:3 * hd + 1] * ocmp_ref[:, hsl(hd)]
                 + sig[:, 3 * hd + 1:3 * hd + 2] * o_sel[g]
                 + sig[:, 3 * hd + 2:3 * hd + 3] * o_win[g])
            y = o * lax.rsqrt(jnp.mean(o * o, axis=-1, keepdims=True) + EPS)
            o_ref[:, hsl(hd)] = (y * gnw_ref[:, hsl(hd)]).astype(o_ref.dtype)


def sel_win(qkv, selb, ksp, vstp, kwp, vwtp, e2p, tab, ocmp, gl, gnw, B, S, tq=2 * Q_BLOCK):
    T = qkv.shape[0]
    n_qb = S // tq
    dq = N_Q_HEADS * HEAD_DIM
    dkv = N_KV_HEADS * HEAD_DIM
    SP = S + WINDOW
    kern = functools.partial(_sel_win_kernel, n_qb=n_qb, scale=HEAD_DIM ** -0.5 * LOG2E)
    k_spec = pl.BlockSpec((SP, dkv), lambda i: (i // n_qb, 0))
    vt_spec = pl.BlockSpec((dkv, SP), lambda i: (i // n_qb, 0))
    return pl.pallas_call(
        kern,
        grid=(T // tq,),
        in_specs=[pl.BlockSpec((tq, dq), lambda i: (i, 0)),
                  pl.BlockSpec((tq, dkv), lambda i: (i, 0)),
                  k_spec, vt_spec, k_spec, vt_spec,
                  pl.BlockSpec((SP, HEAD_DIM), lambda i: (0, 0)),
                  pl.BlockSpec(tab.shape, lambda i: (0, 0)),
                  pl.BlockSpec((tq, dq), lambda i: (i, 0)),
                  pl.BlockSpec((tq, HEAD_DIM), lambda i: (i, 0)),
                  pl.BlockSpec((1, dq), lambda i: (0, 0))],
        out_specs=pl.BlockSpec((tq, dq), lambda i: (i, 0)),
        out_shape=jax.ShapeDtypeStruct((T, dq), BF16),
        scratch_shapes=[pltpu.VMEM((1, GQA * tq), F32), pltpu.VMEM((1, GQA * tq), F32),
                        pltpu.VMEM((HEAD_DIM, GQA * tq), F32)],
        compiler_params=_params(("parallel",)),
    )(qkv, selb, ksp, vstp, kwp, vwtp, e2p, tab, ocmp, gl, gnw)


def _conv_kernel(b_ref, c_ref, h_ref, cp_ref, hp_ref, cw_ref, cb_ref, gnw_ref, o_ref, u_ref, *, tiles_per_seq):
    tm = b_ref.shape[0]
    first = (pl.program_id(0) % tiles_per_seq) == 0
    u_prev = cp_ref[...] * hp_ref[...]
    u_ref[0:8, :] = jnp.where(first, 0.0, u_prev)
    u = c_ref[...] * h_ref[...]
    u_ref[8:8 + tm, :] = u
    y = (cw_ref[0:1, :] * u_ref[6:6 + tm, :] + cw_ref[1:2, :] * u_ref[7:7 + tm, :]
         + cw_ref[2:3, :] * u + cb_ref[...])
    o = b_ref[...] * y
    n_groups = o.shape[1] // HEAD_DIM
    for g in range(n_groups):
        sl = slice(g * HEAD_DIM, (g + 1) * HEAD_DIM)
        og = o[:, sl]
        yg = og * lax.rsqrt(jnp.mean(og * og, axis=-1, keepdims=True) + EPS)
        o_ref[:, sl] = (yg * gnw_ref[:, sl]).astype(o_ref.dtype)


def conv_mixer(gc, conv_w, conv_b, gnw, S, tm):
    T = gc.shape[0]
    dc = conv_w.shape[1]
    tps = S // tm
    kern = functools.partial(_conv_kernel, tiles_per_seq=tps)
    prev = lambda col: pl.BlockSpec((8, dc), lambda i, col=col: (jnp.maximum(i * (tm // 8) - 1, 0), col))
    cur = lambda col: pl.BlockSpec((tm, dc), lambda i, col=col: (i, col))
    return pl.pallas_call(
        kern,
        grid=(T // tm,),
        in_specs=[cur(0), cur(1), cur(2), prev(1), prev(2),
                  pl.BlockSpec((8, dc), lambda i: (0, 0)),
                  pl.BlockSpec((1, dc), lambda i: (0, 0)),
                  pl.BlockSpec((1, dc), lambda i: (0, 0))],
        out_specs=pl.BlockSpec((tm, dc), lambda i: (i, 0)),
        out_shape=jax.ShapeDtypeStruct((T, dc), BF16),
        scratch_shapes=[pltpu.VMEM((tm + 8, dc), F32)],
        compiler_params=_params(("parallel",)),
    )(gc, gc, gc, gc, gc, conv_w, conv_b, gnw)


def _out_proj_kernel(ma_ref, mc_ref, wa_ref, wc_ref, x_ref, o_ref):
    o_ref[...] = x_ref[...] + _dot(ma_ref[...], wa_ref[...]) + _dot(mc_ref[...], wc_ref[...])


def out_proj(ma, mc, wa, wc, x, tm, tn):
    T, da = ma.shape
    dc = mc.shape[1]
    D = x.shape[1]
    return pl.pallas_call(
        _out_proj_kernel,
        grid=(T // tm, D // tn),
        in_specs=[pl.BlockSpec((tm, da), lambda i, j: (i, 0)),
                  pl.BlockSpec((tm, dc), lambda i, j: (i, 0)),
                  pl.BlockSpec((da, tn), lambda i, j: (0, j)),
                  pl.BlockSpec((dc, tn), lambda i, j: (0, j)),
                  pl.BlockSpec((tm, tn), lambda i, j: (i, j))],
        out_specs=pl.BlockSpec((tm, tn), lambda i, j: (i, j)),
        out_shape=jax.ShapeDtypeStruct((T, D), F32),
        compiler_params=_params(("parallel", "arbitrary")),
    )(ma, mc, wa, wc, x)


def _peer_route_kernel(q_ref, sk_ref, u_ref, v_ref, e_ref, g_ref, ub_ref, vb_ref, sv_ref, si_ref):
    ub_ref[...] = u_ref[...].astype(BF16)
    vb_ref[...] = v_ref[...].astype(BF16)
    tt = q_ref.shape[0]
    K = PEER_TOPK
    n_io = lax.broadcasted_iota(jnp.int32, (N_KEYS, tt), 0).astype(F32)
    r16 = lax.broadcasted_iota(jnp.int32, (16, tt), 0).astype(F32)
    r8 = lax.broadcasted_iota(jnp.int32, (8, tt), 0).astype(F32)
    ninf = -jnp.inf

    for h in range(PEER_HEADS):
        for c in range(2):
            col = (h * 2 + c) * HEAD_DIM
            s = _dot_nt(sk_ref[h * 2 + c], q_ref[:, col:col + HEAD_DIM])
            for k in range(K):
                m = jnp.max(s, axis=0, keepdims=True)
                idx = jnp.min(jnp.where(s == m, n_io, float(N_KEYS)), axis=0, keepdims=True)
                s = jnp.where(n_io == idx, ninf, s)
                sv_ref[c, k:k + 1, :] = m
                si_ref[c, k:k + 1, :] = idx
        sv0, sv1 = sv_ref[0], sv_ref[1]
        si0, si1 = si_ref[0], si_ref[1]
        cands, flats, exps = [], [], []
        for a, nb, rows in ((0, 16, 16), (1, 8, 8), (2, 5, 8), (3, 4, 8)):
            r_io = r16 if rows == 16 else r8
            val = sv0[a:a + 1, :] + sv1[0:rows, :]
            cands.append(jnp.where(r_io < nb, val, ninf))
            flats.append(a * 16.0 + r_io)
            exps.append(si0[a:a + 1, :] * float(N_KEYS) + si1[0:rows, :])
        for b, lo, hi, rows in ((0, 4, 16, 16), (1, 4, 8, 8), (2, 4, 5, 8)):
            r_io = r16 if rows == 16 else r8
            val = sv0[0:rows, :] + sv1[b:b + 1, :]
            cands.append(jnp.where((r_io >= lo) & (r_io < hi), val, ninf))
            flats.append(r_io * 16.0 + b)
            exps.append(si0[0:rows, :] * float(N_KEYS) + si1[b:b + 1, :])
        cand = jnp.concatenate(cands, axis=0)
        flat = jnp.concatenate(flats, axis=0)
        expt = jnp.concatenate(exps, axis=0)
        cvs, exs = [], []
        for k in range(K):
            m = jnp.max(cand, axis=0, keepdims=True)
            fsel = jnp.min(jnp.where(cand == m, flat, 1e9), axis=0, keepdims=True)
            hit = flat == fsel
            exs.append(jnp.max(jnp.where(hit, expt, -1.0), axis=0, keepdims=True))
            cand = jnp.where(hit, ninf, cand)
            cvs.append(m)
        cv = jnp.concatenate(cvs, axis=0)
        ex = jnp.concatenate(exs, axis=0)
        ev = jnp.exp(cv - jnp.max(cv, axis=0, keepdims=True))
        gates = ev / jnp.sum(ev, axis=0, keepdims=True)
        e_ref[0, h * K:(h + 1) * K, :] = ex.astype(jnp.int32)
        g_ref[0, h * K:(h + 1) * K, :] = gates


def peer_route(qp, subkeys, u, v):
    T = qp.shape[0]
    tt = 128
    P = PEER_HEADS * PEER_TOPK
    nt = T // tt
    E, D = u.shape
    slab = E // nt
    assert slab * nt == E and slab % 16 == 0
    tab_spec = pl.BlockSpec((slab, D), lambda i: (i, 0))
    return pl.pallas_call(
        _peer_route_kernel,
        grid=(nt,),
        in_specs=[pl.BlockSpec((tt, qp.shape[1]), lambda i: (i, 0)),
                  pl.BlockSpec(subkeys.shape, lambda i: (0, 0, 0)),
                  tab_spec, tab_spec],
        out_specs=[pl.BlockSpec((1, P, tt), lambda i: (i, 0, 0))] * 2 + [tab_spec, tab_spec],
        out_shape=[jax.ShapeDtypeStruct((nt, P, tt), jnp.int32),
                   jax.ShapeDtypeStruct((nt, P, tt), F32),
                   jax.ShapeDtypeStruct((E, D), BF16), jax.ShapeDtypeStruct((E, D), BF16)],
        scratch_shapes=[pltpu.VMEM((2, PEER_TOPK, tt), F32), pltpu.VMEM((2, PEER_TOPK, tt), F32)],
        compiler_params=_params(("parallel",)),
    )(qp, subkeys, u, v)


def _peer_gbuild_kernel(e_ref, g_ref, o_ref, i1_ref, i2_ref, gt_ref, s_ref):
    tt = e_ref.shape[2]
    e = e_ref[0].T
    i1_ref[...] = e >> 7
    i2_ref[...] = e & (N_KEYS - 1)
    gt_ref[...] = g_ref[0].T
    P = e.shape[1]
    k_io = lax.broadcasted_iota(jnp.int32, (N_KEYS, P), 0)
    sub = 8
    half = tt // 2

    def build(t0):
        r1s = i1_ref[t0:t0 + half, :]
        r2s = i2_ref[t0:t0 + half, :]
        rgs = gt_ref[t0:t0 + half, :]
        for u in range(half):
            lhs = jnp.where(k_io == r1s[u:u + 1, :], rgs[u:u + 1, :], 0.0).astype(BF16)
            rhs = jnp.where(k_io == r2s[u:u + 1, :], 1.0, 0.0).astype(BF16)
            g_t = _dot_nt(lhs, rhs)
            for a in range(N_KEYS // sub):
                row0 = (a * tt + t0 + u) * sub
                s_ref[row0:row0 + sub, :] = g_t[a * sub:(a + 1) * sub, :]

    def regroup(t0):
        for i1 in range(N_KEYS):
            a, r = divmod(i1, sub)
            o_ref[i1, t0:t0 + half, :] = s_ref[pl.ds((a * tt + t0) * sub + r, half, stride=sub), :].astype(o_ref.dtype)

    build(0)
    build(half)
    regroup(0)
    regroup(half)


def peer_gbuild(ex, gates):
    nt, P, tt = ex.shape
    T = nt * tt
    return pl.pallas_call(
        _peer_gbuild_kernel,
        grid=(nt,),
        in_specs=[pl.BlockSpec((1, P, tt), lambda i: (i, 0, 0))] * 2,
        out_specs=pl.BlockSpec((N_KEYS, tt, N_KEYS), lambda i: (0, i, 0)),
        out_shape=jax.ShapeDtypeStruct((N_KEYS, T, N_KEYS), BF16),
        scratch_shapes=[pltpu.VMEM((tt, P), jnp.int32), pltpu.VMEM((tt, P), jnp.int32),
                        pltpu.VMEM((tt, P), F32), pltpu.VMEM((tt * N_KEYS, N_KEYS), F32)],
        compiler_params=_params(("parallel",)),
    )(ex, gates)


def _peer_dense_kernel(h_ref, nw_ref, fw_ref, u_ref, v_ref, gm_ref, o_ref, hn_ref, acc_ref):
    j = pl.program_id(1)

    @pl.when(j == 0)
    def _():
        x = h_ref[...]
        y = x * lax.rsqrt(jnp.mean(x * x, axis=-1, keepdims=True) + EPS)
        hn_ref[...] = (y * nw_ref[...]).astype(BF16)
        acc_ref[...] = jnp.zeros(acc_ref.shape, F32)

    a = _dot_nt(hn_ref[...], u_ref[...])
    gm = jnp.concatenate([gm_ref[k] for k in range(gm_ref.shape[0])], axis=1)
    w = (gm.astype(F32) * jax.nn.gelu(a)).astype(BF16)
    acc_ref[...] += _dot(w, v_ref[...])

    @pl.when(j == pl.num_programs(1) - 1)
    def _():
        x = h_ref[...] + acc_ref[...]
        y = x * lax.rsqrt(jnp.mean(x * x, axis=-1, keepdims=True) + EPS)
        o_ref[...] = y * fw_ref[...]


def peer_dense(h, ffn_nw, final_nw, u, v, gm, tt, ec):
    T, D = h.shape
    E = u.shape[0]
    return pl.pallas_call(
        _peer_dense_kernel,
        grid=(T // tt, E // ec),
        in_specs=[pl.BlockSpec((tt, D), lambda i, j: (i, 0)),
                  pl.BlockSpec((1, D), lambda i, j: (0, 0)),
                  pl.BlockSpec((1, D), lambda i, j: (0, 0)),
                  pl.BlockSpec((ec, D), lambda i, j: (j, 0)),
                  pl.BlockSpec((ec, D), lambda i, j: (j, 0)),
                  pl.BlockSpec((ec // N_KEYS, tt, N_KEYS), lambda i, j: (j, i, 0))],
        out_specs=pl.BlockSpec((tt, D), lambda i, j: (i, 0)),
        out_shape=jax.ShapeDtypeStruct((T, D), F32),
        scratch_shapes=[pltpu.VMEM((tt, D), BF16), pltpu.VMEM((tt, D), F32)],
        compiler_params=_params(("parallel", "arbitrary")),
    )(h, ffn_nw.reshape(1, D), final_nw.reshape(1, D), u, v, gm)


def _t5_bucket_np(n_dist):
    d = np.arange(n_dist)
    max_exact = NUM_BUCKETS // 2
    nf = np.maximum(d, 1).astype(np.float64)
    large = max_exact + (np.log(nf / max_exact) / math.log(MAX_DISTANCE / max_exact)
                         * (NUM_BUCKETS - max_exact)).astype(np.int64)
    large = np.minimum(large, NUM_BUCKETS - 1)
    return np.where(d < max_exact, d, large).astype(np.int32)


def _overlap_t_np(S):
    n_c = (S - L_CMP) // STRIDE_CMP + 1
    n_sel = S // L_SEL
    pos = np.arange(n_c)[:, None] * STRIDE_CMP + np.arange(L_CMP)[None, :]
    m = np.zeros((n_c + 1, n_sel), np.float32)
    np.add.at(m, (np.repeat(np.arange(n_c), L_CMP), (pos // L_SEL).reshape(-1)), 1.0 / L_CMP)
    return np.ascontiguousarray(m.T)


def _block_onehot_padded_np(S):
    assert S // L_SEL < HEAD_DIM
    e2 = np.zeros((WINDOW + S, HEAD_DIM), np.float32)
    e2[WINDOW + np.arange(S), np.arange(S) // L_SEL] = 1.0
    e2[:WINDOW, HEAD_DIM - 1] = SEL_OFF
    return e2


def nsa_conv_mix(xt, B, S, attn_norm_w, w_in, w_cmp_k, w_cmp_v, cmp_pos, conv_w, conv_b,
                 attn_gnw, conv_gnw, rel_bias, tm=1024, tm_conv=512):
    T, D = xt.shape
    dq = N_Q_HEADS * HEAD_DIM
    dkv = N_KV_HEADS * HEAD_DIM
    n_attn = dq + 6 * dkv
    n_gate = 3 * N_Q_HEADS
    dc = (w_in.shape[1] - n_attn - n_gate) // 3
    w_attn, w_gate, w_conv = split_cast_weights(w_in, n_attn, n_gate)

    qkv = norm_matmul(xt, attn_norm_w, w_attn, BF16, tm, n_attn // 2)
    gl = norm_matmul(xt, attn_norm_w, w_gate, F32, tm, HEAD_DIM)
    gc = norm_matmul(xt, attn_norm_w, w_conv, F32, tm, dc)

    n16 = S // STRIDE_CMP
    k16 = qkv[:, dq:dq + dkv].reshape(B * n16, STRIDE_CMP * dkv)
    v16 = qkv[:, dq + dkv:dq + 2 * dkv].reshape(B * n16, STRIDE_CMP * dkv)

    def wbig(w, lo):
        wl = w[lo:lo + STRIDE_CMP]
        eye = jnp.eye(N_KV_HEADS, dtype=w.dtype)
        return jnp.einsum('lde,hg->lhdge', wl, eye).reshape(STRIDE_CMP * dkv, dkv).astype(BF16)

    def posrow(lo):
        p = cmp_pos[lo:lo + STRIDE_CMP]
        return jnp.broadcast_to(p[:, None, :], (STRIDE_CMP, N_KV_HEADS, HEAD_DIM)).reshape(1, STRIDE_CMP * dkv)

    kc, vc = compress(k16, v16, posrow(0), posrow(STRIDE_CMP),
                      wbig(w_cmp_k, 0), wbig(w_cmp_k, STRIDE_CMP),
                      wbig(w_cmp_v, 0), wbig(w_cmp_v, STRIDE_CMP), B)

    tab = rel_bias[_t5_bucket_np(HEAD_DIM)].T
    ovt = jnp.asarray(_overlap_t_np(S), BF16)
    e2p = jnp.asarray(_block_onehot_padded_np(S), BF16)

    ocmp, selb = cmp_select(qkv, kc, vc, tab, ovt, B, S)
    def keys_padded(col):
        k = qkv[:, col:col + dkv].reshape(B, S, dkv)
        return jnp.pad(k, ((0, 0), (WINDOW, 0), (0, 0))).reshape(B * (WINDOW + S), dkv)

    def values_t_padded(col):
        v = qkv[:, col:col + dkv].reshape(B, S, dkv).transpose(0, 2, 1)
        return jnp.pad(v, ((0, 0), (0, 0), (WINDOW, 0))).reshape(B * dkv, WINDOW + S)

    attn = sel_win(qkv, selb, keys_padded(dq + 2 * dkv), values_t_padded(dq + 3 * dkv),
                   keys_padded(dq + 4 * dkv), values_t_padded(dq + 5 * dkv), e2p, tab, ocmp, gl,
                   attn_gnw.reshape(1, dq), B, S)
    cw8 = jnp.pad(conv_w, ((0, 8 - conv_w.shape[0]), (0, 0)))
    conv = conv_mixer(gc, cw8, conv_b.reshape(1, dc), conv_gnw.reshape(1, dc), S, tm_conv)
    return attn, conv


def peer_block(h, ffn_nw, final_nw, peer_wq, peer_subkeys, peer_u, peer_v, tm=1024, tt=512, ec=1024):
    T, D = h.shape
    qp = norm_matmul(h, ffn_nw, peer_wq.astype(BF16), BF16, tm, 1024)
    sk = peer_subkeys.reshape(PEER_HEADS * 2, N_KEYS, peer_subkeys.shape[-1]).astype(BF16)
    ex, gates, u_bf, v_bf = peer_route(qp, sk, peer_u, peer_v)
    gm = peer_gbuild(ex, gates)
    return peer_dense(h, ffn_nw, final_nw, u_bf, v_bf, gm, tt, ec)


def kernel(x, attn_norm_w, w_in, w_cmp_k, w_cmp_v, cmp_pos, conv_w, conv_b, attn_group_norm_w,
           conv_group_norm_w, w_out, rel_bias, ffn_norm_w, peer_wq, peer_subkeys, peer_u, peer_v,
           final_norm_w):
    B, S, D = x.shape
    T = B * S
    xt = x.reshape(T, D)
    attn, conv = nsa_conv_mix(xt, B, S, attn_norm_w[0], w_in[0], w_cmp_k[0], w_cmp_v[0], cmp_pos[0],
                              conv_w[0], conv_b[0], attn_group_norm_w[0], conv_group_norm_w[0], rel_bias)
    da = attn.shape[1]
    wo = w_out[0].astype(BF16)
    h = out_proj(attn, conv, wo[:da], wo[da:], xt, 1024, 1024)
    out = peer_block(h, ffn_norm_w[0], final_norm_w, peer_wq[0], peer_subkeys[0], peer_u[0], peer_v[0])
    return out.reshape(B, S, D)
```

```python
import functools
import math

import jax
import jax.numpy as jnp
import numpy as np
from jax import lax
from jax.experimental import pallas as pl
from jax.experimental.pallas import tpu as pltpu

F32 = jnp.float32
BF16 = jnp.bfloat16

HEAD_DIM = 128
N_KV_HEADS = 2
GQA = 4
N_Q_HEADS = N_KV_HEADS * GQA
L_CMP = 32
STRIDE_CMP = 16
L_SEL = 64
N_SEL = 16
WINDOW = 512
Q_BLOCK = 128
FORCED_SCORE = float(GQA + 1)
NUM_BUCKETS = 32
MAX_DISTANCE = 128
PEER_HEADS = 8
N_KEYS = 128
PEER_TOPK = 16
EPS = 1e-6
NEG_BIG = -1e30
SEL_OFF = -float(2 ** 30)
LOG2E = math.log2(math.e)
VMEM_LIMIT = 56 * 1024 * 1024


def _dot(a, b):
    return jnp.dot(a, b, preferred_element_type=F32)


def _dot_nt(a, b):
    return lax.dot_general(a, b, (((1,), (1,)), ((), ())), preferred_element_type=F32)


def _params(sem, vmem=VMEM_LIMIT):
    return pltpu.CompilerParams(dimension_semantics=sem, vmem_limit_bytes=vmem)


def _norm_matmul_kernel(x_ref, nw_ref, w_ref, o_ref, xn_ref, *, w_is_nk):
    @pl.when(pl.program_id(1) == 0)
    def _():
        x = x_ref[...]
        y = x * lax.rsqrt(jnp.mean(x * x, axis=-1, keepdims=True) + EPS)
        xn_ref[...] = (y * nw_ref[...]).astype(BF16)

    dot = _dot_nt if w_is_nk else _dot
    o_ref[...] = dot(xn_ref[...], w_ref[...]).astype(o_ref.dtype)


def norm_matmul(x, norm_w, w, out_dtype, tm, tn, w_is_nk=False):
    T, D = x.shape
    N = w.shape[0] if w_is_nk else w.shape[1]
    w_spec = pl.BlockSpec((tn, D), lambda i, j: (j, 0)) if w_is_nk else pl.BlockSpec((D, tn), lambda i, j: (0, j))
    return pl.pallas_call(
        functools.partial(_norm_matmul_kernel, w_is_nk=w_is_nk),
        grid=(T // tm, N // tn),
        in_specs=[pl.BlockSpec((tm, D), lambda i, j: (i, 0)),
                  pl.BlockSpec((1, D), lambda i, j: (0, 0)),
                  w_spec],
        out_specs=pl.BlockSpec((tm, tn), lambda i, j: (i, j)),
        out_shape=jax.ShapeDtypeStruct((T, N), out_dtype),
        scratch_shapes=[pltpu.VMEM((tm, D), BF16)],
        compiler_params=_params(("parallel", "arbitrary")),
    )(x, norm_w.reshape(1, D), w)


def _cast_rows_kernel(w_ref, o_ref):
    o_ref[...] = w_ref[...].astype(o_ref.dtype)


def cast_rows(w, row0, n_rows, tr=256):
    K = w.shape[1]
    assert row0 % 8 == 0 and n_rows % tr == 0
    return pl.pallas_call(
        _cast_rows_kernel,
        grid=(n_rows // tr,),
        in_specs=[pl.BlockSpec((pl.Element(tr), pl.Element(K)), lambda i: (pl.multiple_of(row0 + i * tr, 8), 0))],
        out_specs=pl.BlockSpec((tr, K), lambda i: (i, 0)),
        out_shape=jax.ShapeDtypeStruct((n_rows, K), BF16),
        compiler_params=_params(("parallel",)),
    )(w)


def _compress_kernel(k_ref, v_ref, pa_ref, pb_ref, wka_ref, wkb_ref, wva_ref, wvb_ref, kc_ref, vc_ref):
    def one(x_ref, wa_ref, wb_ref, o_ref):
        x = x_ref[...].astype(F32)
        a = _dot((x + pa_ref[...]).astype(BF16), wa_ref[...])
        b = _dot((x + pb_ref[...]).astype(BF16), wb_ref[...])
        n = b.shape[0]
        o_ref[...] = (a + pltpu.roll(b, n - 1, 0)).astype(o_ref.dtype)

    one(k_ref, wka_ref, wkb_ref, kc_ref)
    one(v_ref, wva_ref, wvb_ref, vc_ref)


def compress(k16, v16, pos_a, pos_b, wka, wkb, wva, wvb, B):
    R, C = k16.shape
    nb = R // B
    dkv = wka.shape[1]
    full = lambda shp: pl.BlockSpec(shp, lambda b: (0, 0))
    return pl.pallas_call(
        _compress_kernel,
        grid=(B,),
        in_specs=[pl.BlockSpec((nb, C), lambda b: (b, 0)),
                  pl.BlockSpec((nb, C), lambda b: (b, 0)),
                  full((1, C)), full((1, C)),
                  full((C, dkv)), full((C, dkv)), full((C, dkv)), full((C, dkv))],
        out_specs=[pl.BlockSpec((nb, dkv), lambda b: (b, 0))] * 2,
        out_shape=[jax.ShapeDtypeStruct((R, dkv), BF16)] * 2,
        compiler_params=_params(("parallel",)),
    )(k16, v16, pos_a, pos_b, wka, wkb, wva, wvb)


def _bias_from_dist(tab_row, dist):
    idx = jnp.clip(dist, 0, 127)
    w = tab_row.shape[1]
    tab = jnp.broadcast_to(tab_row, (idx.shape[0], w))
    parts = [jnp.take_along_axis(tab, idx[:, k:k + w], axis=1) for k in range(0, idx.shape[1], w)]
    return parts[0] if len(parts) == 1 else jnp.concatenate(parts, axis=1)


def _cmp_select_kernel(q_ref, kc_ref, vc_ref, tab_ref, ovt_ref, ocmp_ref, selb_ref, *, n_qb, scale):
    c = pl.program_id(0) % n_qb
    tq = q_ref.shape[0]
    n_c = kc_ref.shape[0]
    n_sel = ovt_ref.shape[0]
    t0 = c * tq
    t_col = t0 + lax.broadcasted_iota(jnp.int32, (tq, n_c), 0)
    n_row = lax.broadcasted_iota(jnp.int32, (tq, n_c), 1)
    dist = t_col - (n_row * STRIDE_CMP + (L_CMP - 1))
    valid = dist >= 0

    j_io = lax.broadcasted_iota(jnp.int32, (n_sel, tq), 0)
    t_io = t0 + lax.broadcasted_iota(jnp.int32, (n_sel, tq), 1)
    blk_t = t_io // L_SEL
    forced = (j_io == 0) | (j_io == blk_t) | (j_io == blk_t - 1)
    causal_blk = j_io * L_SEL <= t_io

    for h in range(N_KV_HEADS):
        kc = kc_ref[:, h * HEAD_DIM:(h + 1) * HEAD_DIM]
        vc = vc_ref[:, h * HEAD_DIM:(h + 1) * HEAD_DIM]
        psum = jnp.zeros((tq, n_c), F32)
        for g in range(GQA):
            hd = h * GQA + g
            qh = q_ref[:, hd * HEAD_DIM:(hd + 1) * HEAD_DIM]
            bias = _bias_from_dist(tab_ref[hd:hd + 1, :], dist)
            s = _dot_nt(qh, kc) * scale + bias
            s = jnp.where(valid, s, NEG_BIG)
            m = jnp.max(s, axis=-1, keepdims=True)
            e = jnp.where(valid, jnp.exp(s - m), 0.0)
            d = jnp.sum(e, axis=-1, keepdims=True)
            p = e / jnp.where(d > 0, d, 1.0)
            ocmp_ref[:, hd * HEAD_DIM:(hd + 1) * HEAD_DIM] = _dot(p.astype(BF16), vc)
            psum = psum + p
        p_hi = psum.astype(BF16)
        p_lo = (psum - p_hi.astype(F32)).astype(BF16)
        ovt = ovt_ref[...]
        imp = _dot_nt(ovt, p_hi) + _dot_nt(ovt, p_lo)
        score = jnp.where(forced, FORCED_SCORE, jnp.where(causal_blk, imp, -1.0))
        rank = jnp.zeros((n_sel, tq), F32)
        for jp in range(n_sel):
            row = score[jp:jp + 1, :]
            rank = rank + jnp.where(j_io > jp, jnp.where(row >= score, 1.0, 0.0),
                                    jnp.where(row > score, 1.0, 0.0))
        selb = jnp.where(rank < float(N_SEL), 0.0, SEL_OFF)
        if n_sel < HEAD_DIM:
            selb = jnp.concatenate([selb, jnp.zeros((HEAD_DIM - n_sel, tq), F32)], axis=0)
        selb_ref[:, h * HEAD_DIM:(h + 1) * HEAD_DIM] = selb.T.astype(BF16)


def cmp_select(qkv, kc, vc, tab, ovt, B, S, tq=2 * Q_BLOCK):
    T = qkv.shape[0]
    n_qb = S // tq
    n_c = kc.shape[0] // B
    dq = N_Q_HEADS * HEAD_DIM
    dkv = N_KV_HEADS * HEAD_DIM
    kern = functools.partial(_cmp_select_kernel, n_qb=n_qb, scale=HEAD_DIM ** -0.5)
    return pl.pallas_call(
        kern,
        grid=(T // tq,),
        in_specs=[pl.BlockSpec((tq, dq), lambda i: (i, 0)),
                  pl.BlockSpec((n_c, dkv), lambda i: (i // n_qb, 0)),
                  pl.BlockSpec((n_c, dkv), lambda i: (i // n_qb, 0)),
                  pl.BlockSpec(tab.shape, lambda i: (0, 0)),
                  pl.BlockSpec(ovt.shape, lambda i: (0, 0))],
        out_specs=[pl.BlockSpec((tq, dq), lambda i: (i, 0)),
                   pl.BlockSpec((tq, dkv), lambda i: (i, 0))],
        out_shape=[jax.ShapeDtypeStruct((T, dq), F32),
                   jax.ShapeDtypeStruct((T, dkv), BF16)],
        compiler_params=_params(("parallel",)),
    )(qkv, kc, vc, tab, ovt)


def _flash_chunk(k, vt, q, bias, m_ref, l_ref, acc_ref, scale):
    s = _dot_nt(k, q) * scale + bias
    m_old = m_ref[...]
    m_new = jnp.maximum(m_old, jnp.max(s, axis=0, keepdims=True))
    alpha = jnp.exp2(m_old - m_new)
    p = jnp.exp2(s - m_new)
    l_ref[...] = alpha * l_ref[...] + jnp.sum(p, axis=0, keepdims=True)
    acc_ref[...] = alpha * acc_ref[...] + _dot(vt, p.astype(BF16))
    m_ref[...] = m_new


def _sel_win_kernel(q_ref, selb_ref, ks_ref, vst_ref, kw_ref, vwt_ref, e2_ref, tab_ref, ocmp_ref,
                    gl_ref, gnw_ref, o_ref, m_ref, l_ref, acc_ref, *, n_qb, scale):
    c = pl.program_id(0) % n_qb
    tq = q_ref.shape[0]
    rows = GQA * tq
    n_back = WINDOW // tq
    near = WINDOW + tq
    j_io = lax.broadcasted_iota(jnp.int32, (tq, tq), 0)
    i_io = lax.broadcasted_iota(jnp.int32, (tq, tq), 1)
    dij = i_io - j_io
    dij4 = jnp.concatenate([dij] * GQA, axis=1)
    causal = dij4 >= 0
    pad_col = jnp.where(lax.broadcasted_iota(jnp.int32, (rows, HEAD_DIM), 1) == HEAD_DIM - 1, 1.0, 0.0).astype(BF16)
    sig = jax.nn.sigmoid(gl_ref[...])
    near0 = pl.multiple_of(c * tq, tq)
    n_far = jnp.maximum(c - n_back, 0)
    n_full = n_far // n_back
    n_rem = n_far - n_full * n_back

    def hsl(h):
        return slice(h * HEAD_DIM, (h + 1) * HEAD_DIM)

    def far0(i):
        return pl.multiple_of(WINDOW + i * WINDOW, WINDOW)

    def reset():
        m_ref[...] = jnp.full(m_ref.shape, NEG_BIG, F32)
        l_ref[...] = jnp.zeros(l_ref.shape, F32)
        acc_ref[...] = jnp.zeros(acc_ref.shape, F32)

    def result():
        ot = acc_ref[...] / l_ref[...]
        return [ot[:, g * tq:(g + 1) * tq].T for g in range(GQA)]

    for h in range(N_KV_HEADS):
        q4 = jnp.concatenate([q_ref[:, (h * GQA + g) * HEAD_DIM:(h * GQA + g + 1) * HEAD_DIM]
                              for g in range(GQA)], axis=0)
        sb4 = jnp.concatenate([selb_ref[:, hsl(h)]] * GQA, axis=0)
        q_aug = jnp.concatenate([q4, sb4 + pad_col], axis=1)
        q_win = jnp.concatenate([q4, pad_col], axis=1)
        tabs = [tab_ref[h * GQA + g:h * GQA + g + 1, :] * LOG2E for g in range(GQA)]
        d0 = jnp.concatenate([_bias_from_dist(t, dij) for t in tabs], axis=1)
        d1 = jnp.concatenate([_bias_from_dist(t, dij + tq) for t in tabs], axis=1)
        far = jnp.concatenate([t[:, HEAD_DIM - 1:HEAD_DIM] + jnp.zeros((1, tq), F32) for t in tabs], axis=1)
        far_t = jnp.broadcast_to(far, (tq, rows))
        diag = jnp.where(causal, d0, NEG_BIG)
        bias_sel = jnp.concatenate([far_t] * (n_back - 1) + [d1, diag], axis=0)
        bias_win = jnp.concatenate([jnp.where(dij4 < 0, far_t, NEG_BIG)] + [far_t] * (n_back - 2) + [d1, diag],
                                   axis=0)

        reset()
        e2_near = e2_ref[pl.ds(near0, near), :]
        k_near = jnp.concatenate([ks_ref[pl.ds(near0, near), hsl(h)], e2_near], axis=1)
        _flash_chunk(k_near, vst_ref[hsl(h), pl.ds(near0, near)], q_aug, bias_sel, m_ref, l_ref, acc_ref, scale)

        def far_chunk(i, bias):
            r0 = far0(i)
            k = jnp.concatenate([ks_ref[pl.ds(r0, WINDOW), hsl(h)], e2_ref[pl.ds(r0, WINDOW), :]], axis=1)
            _flash_chunk(k, vst_ref[hsl(h), pl.ds(r0, WINDOW)], q_aug, bias, m_ref, l_ref, acc_ref, scale)

        def body(i, carry):
            far_chunk(i, far)
            return carry

        lax.fori_loop(0, n_full, body, 0)

        @pl.when(n_rem > 0)
        def _():
            if n_back == 2:
                r0 = far0(n_full)
                k = jnp.concatenate([ks_ref[pl.ds(r0, tq), hsl(h)], e2_ref[pl.ds(r0, tq), :]], axis=1)
                _flash_chunk(k, vst_ref[hsl(h), pl.ds(r0, tq)], q_aug, far, m_ref, l_ref, acc_ref, scale)
            else:
                live = lax.broadcasted_iota(jnp.int32, (WINDOW, rows), 0) < n_rem * tq
                far_chunk(n_full, jnp.where(live, jnp.broadcast_to(far, (WINDOW, rows)), NEG_BIG))

        o_sel = result()

        reset()
        kw_near = jnp.concatenate([kw_ref[pl.ds(near0, near), hsl(h)], e2_near], axis=1)
        _flash_chunk(kw_near, vwt_ref[hsl(h), pl.ds(near0, near)], q_win, bias_win, m_ref, l_ref, acc_ref, scale)
        o_win = result()

        for g in range(GQA):
            hd = h * GQA + g
            o = (sig[:, 3 * hd:3 * hd + 1] * ocmp_ref[:, hsl(hd)]
                 + sig[:, 3 * hd + 1:3 * hd + 2] * o_sel[g]
                 + sig[:, 3 * hd + 2:3 * hd + 3] * o_win[g])
            y = o * lax.rsqrt(jnp.mean(o * o, axis=-1, keepdims=True) + EPS)
            o_ref[:, hsl(hd)] = (y * gnw_ref[:, hsl(hd)]).astype(o_ref.dtype)


def sel_win(qkv, selb, ksp, vstp, kwp, vwtp, e2p, tab, ocmp, gl, gnw, B, S, tq=2 * Q_BLOCK):
    T = qkv.shape[0]
    n_qb = S // tq
    dq = N_Q_HEADS * HEAD_DIM
    dkv = N_KV_HEADS * HEAD_DIM
    SP = S + WINDOW
    kern = functools.partial(_sel_win_kernel, n_qb=n_qb, scale=HEAD_DIM ** -0.5 * LOG2E)
    k_spec = pl.BlockSpec((SP, dkv), lambda i: (i // n_qb, 0))
    vt_spec = pl.BlockSpec((dkv, SP), lambda i: (i // n_qb, 0))
    return pl.pallas_call(
        kern,
        grid=(T // tq,),
        in_specs=[pl.BlockSpec((tq, dq), lambda i: (i, 0)),
                  pl.BlockSpec((tq, dkv), lambda i: (i, 0)),
                  k_spec, vt_spec, k_spec, vt_spec,
                  pl.BlockSpec((SP, HEAD_DIM), lambda i: (0, 0)),
                  pl.BlockSpec(tab.shape, lambda i: (0, 0)),
                  pl.BlockSpec((tq, dq), lambda i: (i, 0)),
                  pl.BlockSpec((tq, HEAD_DIM), lambda i: (i, 0)),
                  pl.BlockSpec((1, dq), lambda i: (0, 0))],
        out_specs=pl.BlockSpec((tq, dq), lambda i: (i, 0)),
        out_shape=jax.ShapeDtypeStruct((T, dq), BF16),
        scratch_shapes=[pltpu.VMEM((1, GQA * tq), F32), pltpu.VMEM((1, GQA * tq), F32),
                        pltpu.VMEM((HEAD_DIM, GQA * tq), F32)],
        compiler_params=_params(("parallel",)),
    )(qkv, selb, ksp, vstp, kwp, vwtp, e2p, tab, ocmp, gl, gnw)


def _conv_kernel(b_ref, c_ref, h_ref, cp_ref, hp_ref, cw_ref, cb_ref, gnw_ref, o_ref, u_ref, *, tiles_per_seq):
    tm = b_ref.shape[0]
    first = (pl.program_id(0) % tiles_per_seq) == 0
    u_prev = cp_ref[...] * hp_ref[...]
    u_ref[0:8, :] = jnp.where(first, 0.0, u_prev)
    u = c_ref[...] * h_ref[...]
    u_ref[8:8 + tm, :] = u
    y = (cw_ref[0:1, :] * u_ref[6:6 + tm, :] + cw_ref[1:2, :] * u_ref[7:7 + tm, :]
         + cw_ref[2:3, :] * u + cb_ref[...])
    o = b_ref[...] * y
    n_groups = o.shape[1] // HEAD_DIM
    for g in range(n_groups):
        sl = slice(g * HEAD_DIM, (g + 1) * HEAD_DIM)
        og = o[:, sl]
        yg = og * lax.rsqrt(jnp.mean(og * og, axis=-1, keepdims=True) + EPS)
        o_ref[:, sl] = (yg * gnw_ref[:, sl]).astype(o_ref.dtype)


def conv_mixer(gc, conv_w, conv_b, gnw, S, tm):
    T = gc.shape[0]
    dc = conv_w.shape[1]
    tps = S // tm
    kern = functools.partial(_conv_kernel, tiles_per_seq=tps)
    prev = lambda col: pl.BlockSpec((8, dc), lambda i, col=col: (jnp.maximum(i * (tm // 8) - 1, 0), col))
    cur = lambda col: pl.BlockSpec((tm, dc), lambda i, col=col: (i, col))
    return pl.pallas_call(
        kern,
        grid=(T // tm,),
        in_specs=[cur(0), cur(1), cur(2), prev(1), prev(2),
                  pl.BlockSpec((8, dc), lambda i: (0, 0)),
                  pl.BlockSpec((1, dc), lambda i: (0, 0)),
                  pl.BlockSpec((1, dc), lambda i: (0, 0))],
        out_specs=pl.BlockSpec((tm, dc), lambda i: (i, 0)),
        out_shape=jax.ShapeDtypeStruct((T, dc), BF16),
        scratch_shapes=[pltpu.VMEM((tm + 8, dc), F32)],
        compiler_params=_params(("parallel",)),
    )(gc, gc, gc, gc, gc, conv_w, conv_b, gnw)


def _out_proj_kernel(ma_ref, mc_ref, wa_ref, wc_ref, x_ref, o_ref):
    o_ref[...] = x_ref[...] + _dot(ma_ref[...], wa_ref[...]) + _dot(mc_ref[...], wc_ref[...])


def out_proj(ma, mc, wa, wc, x, tm, tn):
    T, da = ma.shape
    dc = mc.shape[1]
    D = x.shape[1]
    return pl.pallas_call(
        _out_proj_kernel,
        grid=(T // tm, D // tn),
        in_specs=[pl.BlockSpec((tm, da), lambda i, j: (i, 0)),
                  pl.BlockSpec((tm, dc), lambda i, j: (i, 0)),
                  pl.BlockSpec((da, tn), lambda i, j: (0, j)),
                  pl.BlockSpec((dc, tn), lambda i, j: (0, j)),
                  pl.BlockSpec((tm, tn), lambda i, j: (i, j))],
        out_specs=pl.BlockSpec((tm, tn), lambda i, j: (i, j)),
        out_shape=jax.ShapeDtypeStruct((T, D), F32),
        compiler_params=_params(("parallel", "arbitrary")),
    )(ma, mc, wa, wc, x)


def _peer_route_kernel(q_ref, sk_ref, u_ref, v_ref, e_ref, g_ref, ub_ref, vb_ref, sv_ref, si_ref):
    ub_ref[...] = u_ref[...].astype(BF16)
    vb_ref[...] = v_ref[...].astype(BF16)
    tt = q_ref.shape[0]
    K = PEER_TOPK
    n_io = lax.broadcasted_iota(jnp.int32, (N_KEYS, tt), 0).astype(F32)
    r16 = lax.broadcasted_iota(jnp.int32, (16, tt), 0).astype(F32)
    r8 = lax.broadcasted_iota(jnp.int32, (8, tt), 0).astype(F32)
    ninf = -jnp.inf

    for h in range(PEER_HEADS):
        for c in range(2):
            col = (h * 2 + c) * HEAD_DIM
            s = _dot_nt(sk_ref[h * 2 + c], q_ref[:, col:col + HEAD_DIM])
            for k in range(K):
                m = jnp.max(s, axis=0, keepdims=True)
                idx = jnp.min(jnp.where(s == m, n_io, float(N_KEYS)), axis=0, keepdims=True)
                s = jnp.where(n_io == idx, ninf, s)
                sv_ref[c, k:k + 1, :] = m
                si_ref[c, k:k + 1, :] = idx
        sv0, sv1 = sv_ref[0], sv_ref[1]
        si0, si1 = si_ref[0], si_ref[1]
        cands, flats, exps = [], [], []
        for a, nb, rows in ((0, 16, 16), (1, 8, 8), (2, 5, 8), (3, 4, 8)):
            r_io = r16 if rows == 16 else r8
            val = sv0[a:a + 1, :] + sv1[0:rows, :]
            cands.append(jnp.where(r_io < nb, val, ninf))
            flats.append(a * 16.0 + r_io)
            exps.append(si0[a:a + 1, :] * float(N_KEYS) + si1[0:rows, :])
        for b, lo, hi, rows in ((0, 4, 16, 16), (1, 4, 8, 8), (2, 4, 5, 8)):
            r_io = r16 if rows == 16 else r8
            val = sv0[0:rows, :] + sv1[b:b + 1, :]
            cands.append(jnp.where((r_io >= lo) & (r_io < hi), val, ninf))
            flats.append(r_io * 16.0 + b)
            exps.append(si0[0:rows, :] * float(N_KEYS) + si1[b:b + 1, :])
        cand = jnp.concatenate(cands, axis=0)
        flat = jnp.concatenate(flats, axis=0)
        expt = jnp.concatenate(exps, axis=0)
        cvs, exs = [], []
        for k in range(K):
            m = jnp.max(cand, axis=0, keepdims=True)
            fsel = jnp.min(jnp.where(cand == m, flat, 1e9), axis=0, keepdims=True)
            hit = flat == fsel
            exs.append(jnp.max(jnp.where(hit, expt, -1.0), axis=0, keepdims=True))
            cand = jnp.where(hit, ninf, cand)
            cvs.append(m)
        cv = jnp.concatenate(cvs, axis=0)
        ex = jnp.concatenate(exs, axis=0)
        ev = jnp.exp(cv - jnp.max(cv, axis=0, keepdims=True))
        gates = ev / jnp.sum(ev, axis=0, keepdims=True)
        e_ref[0, h * K:(h + 1) * K, :] = ex.astype(jnp.int32)
        g_ref[0, h * K:(h + 1) * K, :] = gates


def peer_route(qp, subkeys, u, v):
    T = qp.shape[0]
    tt = 128
    P = PEER_HEADS * PEER_TOPK
    nt = T // tt
    E, D = u.shape
    slab = E // nt
    assert slab * nt == E and slab % 16 == 0
    tab_spec = pl.BlockSpec((slab, D), lambda i: (i, 0))
    return pl.pallas_call(
        _peer_route_kernel,
        grid=(nt,),
        in_specs=[pl.BlockSpec((tt, qp.shape[1]), lambda i: (i, 0)),
                  pl.BlockSpec(subkeys.shape, lambda i: (0, 0, 0)),
                  tab_spec, tab_spec],
        out_specs=[pl.BlockSpec((1, P, tt), lambda i: (i, 0, 0))] * 2 + [tab_spec, tab_spec],
        out_shape=[jax.ShapeDtypeStruct((nt, P, tt), jnp.int32),
                   jax.ShapeDtypeStruct((nt, P, tt), F32),
                   jax.ShapeDtypeStruct((E, D), BF16), jax.ShapeDtypeStruct((E, D), BF16)],
        scratch_shapes=[pltpu.VMEM((2, PEER_TOPK, tt), F32), pltpu.VMEM((2, PEER_TOPK, tt), F32)],
        compiler_params=_params(("parallel",)),
    )(qp, subkeys, u, v)


def _peer_gbuild_kernel(e_ref, g_ref, o_ref, i1_ref, i2_ref, gt_ref, s_ref):
    tt = e_ref.shape[2]
    e = e_ref[0].T
    i1_ref[...] = e >> 7
    i2_ref[...] = e & (N_KEYS - 1)
    gt_ref[...] = g_ref[0].T
    P = e.shape[1]
    k_io = lax.broadcasted_iota(jnp.int32, (N_KEYS, P), 0)
    sub = 8
    half = tt // 2

    def build(t0):
        r1s = i1_ref[t0:t0 + half, :]
        r2s = i2_ref[t0:t0 + half, :]
        rgs = gt_ref[t0:t0 + half, :]
        for u in range(half):
            lhs = jnp.where(k_io == r1s[u:u + 1, :], rgs[u:u + 1, :], 0.0).astype(BF16)
            rhs = jnp.where(k_io == r2s[u:u + 1, :], 1.0, 0.0).astype(BF16)
            g_t = _dot_nt(lhs, rhs)
            for a in range(N_KEYS // sub):
                row0 = (a * tt + t0 + u) * sub
                s_ref[row0:row0 + sub, :] = g_t[a * sub:(a + 1) * sub, :]

    def regroup(t0):
        for i1 in range(N_KEYS):
            a, r = divmod(i1, sub)
            o_ref[i1, t0:t0 + half, :] = s_ref[pl.ds((a * tt + t0) * sub + r, half, stride=sub), :].astype(o_ref.dtype)

    build(0)
    build(half)
    regroup(0)
    regroup(half)


def peer_gbuild(ex, gates):
    nt, P, tt = ex.shape
    T = nt * tt
    return pl.pallas_call(
        _peer_gbuild_kernel,
        grid=(nt,),
        in_specs=[pl.BlockSpec((1, P, tt), lambda i: (i, 0, 0))] * 2,
        out_specs=pl.BlockSpec((N_KEYS, tt, N_KEYS), lambda i: (0, i, 0)),
        out_shape=jax.ShapeDtypeStruct((N_KEYS, T, N_KEYS), BF16),
        scratch_shapes=[pltpu.VMEM((tt, P), jnp.int32), pltpu.VMEM((tt, P), jnp.int32),
                        pltpu.VMEM((tt, P), F32), pltpu.VMEM((tt * N_KEYS, N_KEYS), F32)],
        compiler_params=_params(("parallel",)),
    )(ex, gates)


def _peer_dense_kernel(h_ref, nw_ref, fw_ref, u_ref, v_ref, gm_ref, o_ref, hn_ref, acc_ref):
    j = pl.program_id(1)

    @pl.when(j == 0)
    def _():
        x = h_ref[...]
        y = x * lax.rsqrt(jnp.mean(x * x, axis=-1, keepdims=True) + EPS)
        hn_ref[...] = (y * nw_ref[...]).astype(BF16)
        acc_ref[...] = jnp.zeros(acc_ref.shape, F32)

    a = _dot_nt(hn_ref[...], u_ref[...])
    gm = jnp.concatenate([gm_ref[k] for k in range(gm_ref.shape[0])], axis=1)
    w = (gm.astype(F32) * jax.nn.gelu(a)).astype(BF16)
    acc_ref[...] += _dot(w, v_ref[...])

    @pl.when(j == pl.num_programs(1) - 1)
    def _():
        x = h_ref[...] + acc_ref[...]
        y = x * lax.rsqrt(jnp.mean(x * x, axis=-1, keepdims=True) + EPS)
        o_ref[...] = y * fw_ref[...]


def peer_dense(h, ffn_nw, final_nw, u, v, gm, tt, ec):
    T, D = h.shape
    E = u.shape[0]
    return pl.pallas_call(
        _peer_dense_kernel,
        grid=(T // tt, E // ec),
        in_specs=[pl.BlockSpec((tt, D), lambda i, j: (i, 0)),
                  pl.BlockSpec((1, D), lambda i, j: (0, 0)),
                  pl.BlockSpec((1, D), lambda i, j: (0, 0)),
                  pl.BlockSpec((ec, D), lambda i, j: (j, 0)),
                  pl.BlockSpec((ec, D), lambda i, j: (j, 0)),
                  pl.BlockSpec((ec // N_KEYS, tt, N_KEYS), lambda i, j: (j, i, 0))],
        out_specs=pl.BlockSpec((tt, D), lambda i, j: (i, 0)),
        out_shape=jax.ShapeDtypeStruct((T, D), F32),
        scratch_shapes=[pltpu.VMEM((tt, D), BF16), pltpu.VMEM((tt, D), F32)],
        compiler_params=_params(("parallel", "arbitrary")),
    )(h, ffn_nw.reshape(1, D), final_nw.reshape(1, D), u, v, gm)


def _t5_bucket_np(n_dist):
    d = np.arange(n_dist)
    max_exact = NUM_BUCKETS // 2
    nf = np.maximum(d, 1).astype(np.float64)
    large = max_exact + (np.log(nf / max_exact) / math.log(MAX_DISTANCE / max_exact)
                         * (NUM_BUCKETS - max_exact)).astype(np.int64)
    large = np.minimum(large, NUM_BUCKETS - 1)
    return np.where(d < max_exact, d, large).astype(np.int32)


def _overlap_t_np(S):
    n_c = (S - L_CMP) // STRIDE_CMP + 1
    n_sel = S // L_SEL
    pos = np.arange(n_c)[:, None] * STRIDE_CMP + np.arange(L_CMP)[None, :]
    m = np.zeros((n_c + 1, n_sel), np.float32)
    np.add.at(m, (np.repeat(np.arange(n_c), L_CMP), (pos // L_SEL).reshape(-1)), 1.0 / L_CMP)
    return np.ascontiguousarray(m.T)


def _block_onehot_padded_np(S):
    assert S // L_SEL < HEAD_DIM
    e2 = np.zeros((WINDOW + S, HEAD_DIM), np.float32)
    e2[WINDOW + np.arange(S), np.arange(S) // L_SEL] = 1.0
    e2[:WINDOW, HEAD_DIM - 1] = SEL_OFF
    return e2


def nsa_conv_mix(xt, B, S, attn_norm_w, w_in, w_cmp_k, w_cmp_v, cmp_pos, conv_w, conv_b,
                 attn_gnw, conv_gnw, rel_bias, tm=1024, tm_conv=512):
    T, D = xt.shape
    dq = N_Q_HEADS * HEAD_DIM
    dkv = N_KV_HEADS * HEAD_DIM
    n_attn = dq + 6 * dkv
    n_gate = 3 * N_Q_HEADS
    dc = (w_in.shape[1] - n_attn - n_gate) // 3
    w_t = w_in.T
    w_attn = cast_rows(w_t, 0, n_attn)
    w_conv = cast_rows(w_t, n_attn + n_gate, 3 * dc)
    w_gate = jnp.pad(w_t[n_attn:n_attn + n_gate], ((0, HEAD_DIM - n_gate), (0, 0))).astype(BF16)

    qkv = norm_matmul(xt, attn_norm_w, w_attn, BF16, tm, n_attn // 2, w_is_nk=True)
    gl = norm_matmul(xt, attn_norm_w, w_gate, F32, tm, HEAD_DIM, w_is_nk=True)
    gc = norm_matmul(xt, attn_norm_w, w_conv, F32, tm, dc, w_is_nk=True)

    n16 = S // STRIDE_CMP
    k16 = qkv[:, dq:dq + dkv].reshape(B * n16, STRIDE_CMP * dkv)
    v16 = qkv[:, dq + dkv:dq + 2 * dkv].reshape(B * n16, STRIDE_CMP * dkv)

    def wbig(w, lo):
        wl = w[lo:lo + STRIDE_CMP]
        eye = jnp.eye(N_KV_HEADS, dtype=w.dtype)
        return jnp.einsum('lde,hg->lhdge', wl, eye).reshape(STRIDE_CMP * dkv, dkv).astype(BF16)

    def posrow(lo):
        p = cmp_pos[lo:lo + STRIDE_CMP]
        return jnp.broadcast_to(p[:, None, :], (STRIDE_CMP, N_KV_HEADS, HEAD_DIM)).reshape(1, STRIDE_CMP * dkv)

    kc, vc = compress(k16, v16, posrow(0), posrow(STRIDE_CMP),
                      wbig(w_cmp_k, 0), wbig(w_cmp_k, STRIDE_CMP),
                      wbig(w_cmp_v, 0), wbig(w_cmp_v, STRIDE_CMP), B)

    tab = rel_bias[_t5_bucket_np(HEAD_DIM)].T
    ovt = jnp.asarray(_overlap_t_np(S), BF16)
    e2p = jnp.asarray(_block_onehot_padded_np(S), BF16)

    ocmp, selb = cmp_select(qkv, kc, vc, tab, ovt, B, S)
    def keys_padded(col):
        k = qkv[:, col:col + dkv].reshape(B, S, dkv)
        return jnp.pad(k, ((0, 0), (WINDOW, 0), (0, 0))).reshape(B * (WINDOW + S), dkv)

    def values_t_padded(col):
        v = qkv[:, col:col + dkv].reshape(B, S, dkv).transpose(0, 2, 1)
        return jnp.pad(v, ((0, 0), (0, 0), (WINDOW, 0))).reshape(B * dkv, WINDOW + S)

    attn = sel_win(qkv, selb, keys_padded(dq + 2 * dkv), values_t_padded(dq + 3 * dkv),
                   keys_padded(dq + 4 * dkv), values_t_padded(dq + 5 * dkv), e2p, tab, ocmp, gl,
                   attn_gnw.reshape(1, dq), B, S)
    cw8 = jnp.pad(conv_w, ((0, 8 - conv_w.shape[0]), (0, 0)))
    conv = conv_mixer(gc, cw8, conv_b.reshape(1, dc), conv_gnw.reshape(1, dc), S, tm_conv)
    return attn, conv


def peer_block(h, ffn_nw, final_nw, peer_wq, peer_subkeys, peer_u, peer_v, tm=1024, tt=512, ec=1024):
    T, D = h.shape
    qp = norm_matmul(h, ffn_nw, peer_wq.astype(BF16), BF16, tm, 1024)
    sk = peer_subkeys.reshape(PEER_HEADS * 2, N_KEYS, peer_subkeys.shape[-1]).astype(BF16)
    ex, gates, u_bf, v_bf = peer_route(qp, sk, peer_u, peer_v)
    gm = peer_gbuild(ex, gates)
    return peer_dense(h, ffn_nw, final_nw, u_bf, v_bf, gm, tt, ec)


def kernel(x, attn_norm_w, w_in, w_cmp_k, w_cmp_v, cmp_pos, conv_w, conv_b, attn_group_norm_w,
           conv_group_norm_w, w_out, rel_bias, ffn_norm_w, peer_wq, peer_subkeys, peer_u, peer_v,
           final_norm_w):
    B, S, D = x.shape
    T = B * S
    xt = x.reshape(T, D)
    attn, conv = nsa_conv_mix(xt, B, S, attn_norm_w[0], w_in[0], w_cmp_k[0], w_cmp_v[0], cmp_pos[0],
                              conv_w[0], conv_b[0], attn_group_norm_w[0], conv_group_norm_w[0], rel_bias)
    da = attn.shape[1]
    wo = w_out[0].astype(BF16)
    h = out_proj(attn, conv, wo[:da], wo[da:], xt, 1024, 1024)
    out = peer_block(h, ffn_norm_w[0], final_norm_w, peer_wq[0], peer_subkeys[0], peer_u[0], peer_v[0])
    return out.reshape(B, S, D)
```

```python
import functools
import math

import jax
import jax.numpy as jnp
import numpy as np
from jax import lax
from jax.experimental import pallas as pl
from jax.experimental.pallas import tpu as pltpu

F32 = jnp.float32
BF16 = jnp.bfloat16

HEAD_DIM = 128
N_KV_HEADS = 2
GQA = 4
N_Q_HEADS = N_KV_HEADS * GQA
L_CMP = 32
STRIDE_CMP = 16
L_SEL = 64
N_SEL = 16
WINDOW = 512
Q_BLOCK = 128
FORCED_SCORE = float(GQA + 1)
NUM_BUCKETS = 32
MAX_DISTANCE = 128
PEER_HEADS = 8
N_KEYS = 128
PEER_TOPK = 16
EPS = 1e-6
NEG_BIG = -1e30
SEL_OFF = -float(2 ** 30)
LOG2E = math.log2(math.e)
VMEM_LIMIT = 56 * 1024 * 1024


def _dot(a, b):
    return jnp.dot(a, b, preferred_element_type=F32)


def _dot_nt(a, b):
    return lax.dot_general(a, b, (((1,), (1,)), ((), ())), preferred_element_type=F32)


def _params(sem, vmem=VMEM_LIMIT):
    return pltpu.CompilerParams(dimension_semantics=sem, vmem_limit_bytes=vmem)


def _norm_matmul_kernel(x_ref, nw_ref, w_ref, o_ref, xn_ref, *, w_is_nk):
    @pl.when(pl.program_id(1) == 0)
    def _():
        x = x_ref[...]
        y = x * lax.rsqrt(jnp.mean(x * x, axis=-1, keepdims=True) + EPS)
        xn_ref[...] = (y * nw_ref[...]).astype(BF16)

    dot = _dot_nt if w_is_nk else _dot
    o_ref[...] = dot(xn_ref[...], w_ref[...]).astype(o_ref.dtype)


def norm_matmul(x, norm_w, w, out_dtype, tm, tn, w_is_nk=False):
    T, D = x.shape
    N = w.shape[0] if w_is_nk else w.shape[1]
    w_spec = pl.BlockSpec((tn, D), lambda i, j: (j, 0)) if w_is_nk else pl.BlockSpec((D, tn), lambda i, j: (0, j))
    return pl.pallas_call(
        functools.partial(_norm_matmul_kernel, w_is_nk=w_is_nk),
        grid=(T // tm, N // tn),
        in_specs=[pl.BlockSpec((tm, D), lambda i, j: (i, 0)),
                  pl.BlockSpec((1, D), lambda i, j: (0, 0)),
                  w_spec],
        out_specs=pl.BlockSpec((tm, tn), lambda i, j: (i, j)),
        out_shape=jax.ShapeDtypeStruct((T, N), out_dtype),
        scratch_shapes=[pltpu.VMEM((tm, D), BF16)],
        compiler_params=_params(("parallel", "arbitrary")),
    )(x, norm_w.reshape(1, D), w)


def _cast_rows_kernel(w_ref, o_ref):
    o_ref[...] = w_ref[...].astype(o_ref.dtype)


def cast_rows(w, row0, n_rows, tr=256):
    K = w.shape[1]
    assert row0 % 8 == 0 and n_rows % tr == 0
    return pl.pallas_call(
        _cast_rows_kernel,
        grid=(n_rows // tr,),
        in_specs=[pl.BlockSpec((pl.Element(tr), pl.Element(K)), lambda i: (pl.multiple_of(row0 + i * tr, 8), 0))],
        out_specs=pl.BlockSpec((tr, K), lambda i: (i, 0)),
        out_shape=jax.ShapeDtypeStruct((n_rows, K), BF16),
        compiler_params=_params(("parallel",)),
    )(w)


def _compress_kernel(k_ref, v_ref, pa_ref, pb_ref, wka_ref, wkb_ref, wva_ref, wvb_ref, kc_ref, vc_ref):
    def one(x_ref, wa_ref, wb_ref, o_ref):
        x = x_ref[...].astype(F32)
        a = _dot((x + pa_ref[...]).astype(BF16), wa_ref[...])
        b = _dot((x + pb_ref[...]).astype(BF16), wb_ref[...])
        n = b.shape[0]
        o_ref[...] = (a + pltpu.roll(b, n - 1, 0)).astype(o_ref.dtype)

    one(k_ref, wka_ref, wkb_ref, kc_ref)
    one(v_ref, wva_ref, wvb_ref, vc_ref)


def compress(k16, v16, pos_a, pos_b, wka, wkb, wva, wvb, B):
    R, C = k16.shape
    nb = R // B
    dkv = wka.shape[1]
    full = lambda shp: pl.BlockSpec(shp, lambda b: (0, 0))
    return pl.pallas_call(
        _compress_kernel,
        grid=(B,),
        in_specs=[pl.BlockSpec((nb, C), lambda b: (b, 0)),
                  pl.BlockSpec((nb, C), lambda b: (b, 0)),
                  full((1, C)), full((1, C)),
                  full((C, dkv)), full((C, dkv)), full((C, dkv)), full((C, dkv))],
        out_specs=[pl.BlockSpec((nb, dkv), lambda b: (b, 0))] * 2,
        out_shape=[jax.ShapeDtypeStruct((R, dkv), BF16)] * 2,
        compiler_params=_params(("parallel",)),
    )(k16, v16, pos_a, pos_b, wka, wkb, wva, wvb)


def _bias_from_dist(tab_row, dist):
    idx = jnp.clip(dist, 0, 127)
    w = tab_row.shape[1]
    tab = jnp.broadcast_to(tab_row, (idx.shape[0], w))
    parts = [jnp.take_along_axis(tab, idx[:, k:k + w], axis=1) for k in range(0, idx.shape[1], w)]
    return parts[0] if len(parts) == 1 else jnp.concatenate(parts, axis=1)


def _cmp_select_kernel(q_ref, kc_ref, vc_ref, tab_ref, ovt_ref, ocmp_ref, selb_ref, *, n_qb, scale):
    c = pl.program_id(0) % n_qb
    tq = q_ref.shape[0]
    n_c = kc_ref.shape[0]
    n_sel = ovt_ref.shape[0]
    t0 = c * tq
    t_col = t0 + lax.broadcasted_iota(jnp.int32, (tq, n_c), 0)
    n_row = lax.broadcasted_iota(jnp.int32, (tq, n_c), 1)
    dist = t_col - (n_row * STRIDE_CMP + (L_CMP - 1))
    valid = dist >= 0

    j_io = lax.broadcasted_iota(jnp.int32, (n_sel, tq), 0)
    t_io = t0 + lax.broadcasted_iota(jnp.int32, (n_sel, tq), 1)
    blk_t = t_io // L_SEL
    forced = (j_io == 0) | (j_io == blk_t) | (j_io == blk_t - 1)
    causal_blk = j_io * L_SEL <= t_io

    for h in range(N_KV_HEADS):
        kc = kc_ref[:, h * HEAD_DIM:(h + 1) * HEAD_DIM]
        vc = vc_ref[:, h * HEAD_DIM:(h + 1) * HEAD_DIM]
        psum = jnp.zeros((tq, n_c), F32)
        for g in range(GQA):
            hd = h * GQA + g
            qh = q_ref[:, hd * HEAD_DIM:(hd + 1) * HEAD_DIM]
            bias = _bias_from_dist(tab_ref[hd:hd + 1, :], dist)
            s = _dot_nt(qh, kc) * scale + bias
            s = jnp.where(valid, s, NEG_BIG)
            m = jnp.max(s, axis=-1, keepdims=True)
            e = jnp.where(valid, jnp.exp(s - m), 0.0)
            d = jnp.sum(e, axis=-1, keepdims=True)
            p = e / jnp.where(d > 0, d, 1.0)
            ocmp_ref[:, hd * HEAD_DIM:(hd + 1) * HEAD_DIM] = _dot(p.astype(BF16), vc)
            psum = psum + p
        p_hi = psum.astype(BF16)
        p_lo = (psum - p_hi.astype(F32)).astype(BF16)
        ovt = ovt_ref[...]
        imp = _dot_nt(ovt, p_hi) + _dot_nt(ovt, p_lo)
        score = jnp.where(forced, FORCED_SCORE, jnp.where(causal_blk, imp, -1.0))
        rank = jnp.zeros((n_sel, tq), F32)
        for jp in range(n_sel):
            row = score[jp:jp + 1, :]
            rank = rank + jnp.where(j_io > jp, jnp.where(row >= score, 1.0, 0.0),
                                    jnp.where(row > score, 1.0, 0.0))
        selb = jnp.where(rank < float(N_SEL), 0.0, SEL_OFF)
        if n_sel < HEAD_DIM:
            selb = jnp.concatenate([selb, jnp.zeros((HEAD_DIM - n_sel, tq), F32)], axis=0)
        selb_ref[:, h * HEAD_DIM:(h + 1) * HEAD_DIM] = selb.T.astype(BF16)


def cmp_select(qkv, kc, vc, tab, ovt, B, S, tq=2 * Q_BLOCK):
    T = qkv.shape[0]
    n_qb = S // tq
    n_c = kc.shape[0] // B
    dq = N_Q_HEADS * HEAD_DIM
    dkv = N_KV_HEADS * HEAD_DIM
    kern = functools.partial(_cmp_select_kernel, n_qb=n_qb, scale=HEAD_DIM ** -0.5)
    return pl.pallas_call(
        kern,
        grid=(T // tq,),
        in_specs=[pl.BlockSpec((tq, dq), lambda i: (i, 0)),
                  pl.BlockSpec((n_c, dkv), lambda i: (i // n_qb, 0)),
                  pl.BlockSpec((n_c, dkv), lambda i: (i // n_qb, 0)),
                  pl.BlockSpec(tab.shape, lambda i: (0, 0)),
                  pl.BlockSpec(ovt.shape, lambda i: (0, 0))],
        out_specs=[pl.BlockSpec((tq, dq), lambda i: (i, 0)),
                   pl.BlockSpec((tq, dkv), lambda i: (i, 0))],
        out_shape=[jax.ShapeDtypeStruct((T, dq), F32),
                   jax.ShapeDtypeStruct((T, dkv), BF16)],
        compiler_params=_params(("parallel",)),
    )(qkv, kc, vc, tab, ovt)


def _flash_chunk(k, vt, q, bias, m_ref, l_ref, acc_ref, scale):
    s = _dot_nt(k, q) * scale + bias
    m_old = m_ref[...]
    m_new = jnp.maximum(m_old, jnp.max(s, axis=0, keepdims=True))
    alpha = jnp.exp2(m_old - m_new)
    p = jnp.exp2(s - m_new)
    l_ref[...] = alpha * l_ref[...] + jnp.sum(p, axis=0, keepdims=True)
    acc_ref[...] = alpha * acc_ref[...] + _dot(vt, p.astype(BF16))
    m_ref[...] = m_new


def _sel_win_kernel(q_ref, selb_ref, ks_ref, vst_ref, kw_ref, vwt_ref, e2_ref, tab_ref, ocmp_ref,
                    gl_ref, gnw_ref, o_ref, m_ref, l_ref, acc_ref, *, n_qb, scale):
    c = pl.program_id(0) % n_qb
    tq = q_ref.shape[0]
    rows = GQA * tq
    n_back = WINDOW // tq
    near = WINDOW + tq
    j_io = lax.broadcasted_iota(jnp.int32, (tq, tq), 0)
    i_io = lax.broadcasted_iota(jnp.int32, (tq, tq), 1)
    dij = i_io - j_io
    dij4 = jnp.concatenate([dij] * GQA, axis=1)
    causal = dij4 >= 0
    pad_col = jnp.where(lax.broadcasted_iota(jnp.int32, (rows, HEAD_DIM), 1) == HEAD_DIM - 1, 1.0, 0.0).astype(BF16)
    sig = jax.nn.sigmoid(gl_ref[...])
    near0 = pl.multiple_of(c * tq, tq)
    n_far = jnp.maximum(c - n_back, 0)
    n_full = n_far // n_back
    n_rem = n_far - n_full * n_back

    def hsl(h):
        return slice(h * HEAD_DIM, (h + 1) * HEAD_DIM)

    def far0(i):
        return pl.multiple_of(WINDOW + i * WINDOW, WINDOW)

    def reset():
        m_ref[...] = jnp.full(m_ref.shape, NEG_BIG, F32)
        l_ref[...] = jnp.zeros(l_ref.shape, F32)
        acc_ref[...] = jnp.zeros(acc_ref.shape, F32)

    def result():
        ot = acc_ref[...] / l_ref[...]
        return [ot[:, g * tq:(g + 1) * tq].T for g in range(GQA)]

    for h in range(N_KV_HEADS):
        q4 = jnp.concatenate([q_ref[:, (h * GQA + g) * HEAD_DIM:(h * GQA + g + 1) * HEAD_DIM]
                              for g in range(GQA)], axis=0)
        sb4 = jnp.concatenate([selb_ref[:, hsl(h)]] * GQA, axis=0)
        q_aug = jnp.concatenate([q4, sb4 + pad_col], axis=1)
        q_win = jnp.concatenate([q4, pad_col], axis=1)
        tabs = [tab_ref[h * GQA + g:h * GQA + g + 1, :] * LOG2E for g in range(GQA)]
        d0 = jnp.concatenate([_bias_from_dist(t, dij) for t in tabs], axis=1)
        d1 = jnp.concatenate([_bias_from_dist(t, dij + tq) for t in tabs], axis=1)
        far = jnp.concatenate([t[:, HEAD_DIM - 1:HEAD_DIM] + jnp.zeros((1, tq), F32) for t in tabs], axis=1)
        far_t = jnp.broadcast_to(far, (tq, rows))
        diag = jnp.where(causal, d0, NEG_BIG)
        bias_sel = jnp.concatenate([far_t] * (n_back - 1) + [d1, diag], axis=0)
        bias_win = jnp.concatenate([jnp.where(dij4 < 0, far_t, NEG_BIG)] + [far_t] * (n_back - 2) + [d1, diag],
                                   axis=0)

        reset()
        e2_near = e2_ref[pl.ds(near0, near), :]
        k_near = jnp.concatenate([ks_ref[pl.ds(near0, near), hsl(h)], e2_near], axis=1)
        _flash_chunk(k_near, vst_ref[hsl(h), pl.ds(near0, near)], q_aug, bias_sel, m_ref, l_ref, acc_ref, scale)

        def far_chunk(i, bias):
            r0 = far0(i)
            k = jnp.concatenate([ks_ref[pl.ds(r0, WINDOW), hsl(h)], e2_ref[pl.ds(r0, WINDOW), :]], axis=1)
            _flash_chunk(k, vst_ref[hsl(h), pl.ds(r0, WINDOW)], q_aug, bias, m_ref, l_ref, acc_ref, scale)

        def body(i, carry):
            far_chunk(i, far)
            return carry

        lax.fori_loop(0, n_full, body, 0)

        @pl.when(n_rem > 0)
        def _():
            if n_back == 2:
                r0 = far0(n_full)
                k = jnp.concatenate([ks_ref[pl.ds(r0, tq), hsl(h)], e2_ref[pl.ds(r0, tq), :]], axis=1)
                _flash_chunk(k, vst_ref[hsl(h), pl.ds(r0, tq)], q_aug, far, m_ref, l_ref, acc_ref, scale)
            else:
                live = lax.broadcasted_iota(jnp.int32, (WINDOW, rows), 0) < n_rem * tq
                far_chunk(n_full, jnp.where(live, jnp.broadcast_to(far, (WINDOW, rows)), NEG_BIG))

        o_sel = result()

        reset()
        kw_near = jnp.concatenate([kw_ref[pl.ds(near0, near), hsl(h)], e2_near], axis=1)
        _flash_chunk(kw_near, vwt_ref[hsl(h), pl.ds(near0, near)], q_win, bias_win, m_ref, l_ref, acc_ref, scale)
        o_win = result()

        for g in range(GQA):
            hd = h * GQA + g
            o = (sig[:, 3 * hd:3 * hd + 1] * ocmp_ref[:, hsl(hd)]
                 + sig[:, 3 * hd + 1:3 * hd + 2] * o_sel[g]
                 + sig[:, 3 * hd + 2:3 * hd + 3] * o_win[g])
            y = o * lax.rsqrt(jnp.mean(o * o, axis=-1, keepdims=True) + EPS)
            o_ref[:, hsl(hd)] = (y * gnw_ref[:, hsl(hd)]).astype(o_ref.dtype)


def sel_win(qkv, selb, ksp, vstp, kwp, vwtp, e2p, tab, ocmp, gl, gnw, B, S, tq=2 * Q_BLOCK):
    T = qkv.shape[0]
    n_qb = S // tq
    dq = N_Q_HEADS * HEAD_DIM
    dkv = N_KV_HEADS * HEAD_DIM
    SP = S + WINDOW
    kern = functools.partial(_sel_win_kernel, n_qb=n_qb, scale=HEAD_DIM ** -0.5 * LOG2E)
    k_spec = pl.BlockSpec((SP, dkv), lambda i: (i // n_qb, 0))
    vt_spec = pl.BlockSpec((dkv, SP), lambda i: (i // n_qb, 0))
    return pl.pallas_call(
        kern,
        grid=(T // tq,),
        in_specs=[pl.BlockSpec((tq, dq), lambda i: (i, 0)),
                  pl.BlockSpec((tq, dkv), lambda i: (i, 0)),
                  k_spec, vt_spec, k_spec, vt_spec,
                  pl.BlockSpec((SP, HEAD_DIM), lambda i: (0, 0)),
                  pl.BlockSpec(tab.shape, lambda i: (0, 0)),
                  pl.BlockSpec((tq, dq), lambda i: (i, 0)),
                  pl.BlockSpec((tq, HEAD_DIM), lambda i: (i, 0)),
                  pl.BlockSpec((1, dq), lambda i: (0, 0))],
        out_specs=pl.BlockSpec((tq, dq), lambda i: (i, 0)),
        out_shape=jax.ShapeDtypeStruct((T, dq), BF16),
        scratch_shapes=[pltpu.VMEM((1, GQA * tq), F32), pltpu.VMEM((1, GQA * tq), F32),
                        pltpu.VMEM((HEAD_DIM, GQA * tq), F32)],
        compiler_params=_params(("parallel",)),
    )(qkv, selb, ksp, vstp, kwp, vwtp, e2p, tab, ocmp, gl, gnw)


def _conv_kernel(b_ref, c_ref, h_ref, cp_ref, hp_ref, cw_ref, cb_ref, gnw_ref, o_ref, u_ref, *, tiles_per_seq):
    tm = b_ref.shape[0]
    first = (pl.program_id(0) % tiles_per_seq) == 0
    u_prev = cp_ref[...] * hp_ref[...]
    u_ref[0:8, :] = jnp.where(first, 0.0, u_prev)
    u = c_ref[...] * h_ref[...]
    u_ref[8:8 + tm, :] = u
    y = (cw_ref[0:1, :] * u_ref[6:6 + tm, :] + cw_ref[1:2, :] * u_ref[7:7 + tm, :]
         + cw_ref[2:3, :] * u + cb_ref[...])
    o = b_ref[...] * y
    n_groups = o.shape[1] // HEAD_DIM
    for g in range(n_groups):
        sl = slice(g * HEAD_DIM, (g + 1) * HEAD_DIM)
        og = o[:, sl]
        yg = og * lax.rsqrt(jnp.mean(og * og, axis=-1, keepdims=True) + EPS)
        o_ref[:, sl] = (yg * gnw_ref[:, sl]).astype(o_ref.dtype)


def conv_mixer(gc, conv_w, conv_b, gnw, S, tm):
    T = gc.shape[0]
    dc = conv_w.shape[1]
    tps = S // tm
    kern = functools.partial(_conv_kernel, tiles_per_seq=tps)
    prev = lambda col: pl.BlockSpec((8, dc), lambda i, col=col: (jnp.maximum(i * (tm // 8) - 1, 0), col))
    cur = lambda col: pl.BlockSpec((tm, dc), lambda i, col=col: (i, col))
    return pl.pallas_call(
        kern,
        grid=(T // tm,),
        in_specs=[cur(0), cur(1), cur(2), prev(1), prev(2),
                  pl.BlockSpec((8, dc), lambda i: (0, 0)),
                  pl.BlockSpec((1, dc), lambda i: (0, 0)),
                  pl.BlockSpec((1, dc), lambda i: (0, 0))],
        out_specs=pl.BlockSpec((tm, dc), lambda i: (i, 0)),
        out_shape=jax.ShapeDtypeStruct((T, dc), BF16),
        scratch_shapes=[pltpu.VMEM((tm + 8, dc), F32)],
        compiler_params=_params(("parallel",)),
    )(gc, gc, gc, gc, gc, conv_w, conv_b, gnw)


def _out_proj_kernel(ma_ref, mc_ref, wa_ref, wc_ref, x_ref, o_ref):
    o_ref[...] = x_ref[...] + _dot(ma_ref[...], wa_ref[...]) + _dot(mc_ref[...], wc_ref[...])


def out_proj(ma, mc, wa, wc, x, tm, tn):
    T, da = ma.shape
    dc = mc.shape[1]
    D = x.shape[1]
    return pl.pallas_call(
        _out_proj_kernel,
        grid=(T // tm, D // tn),
        in_specs=[pl.BlockSpec((tm, da), lambda i, j: (i, 0)),
                  pl.BlockSpec((tm, dc), lambda i, j: (i, 0)),
                  pl.BlockSpec((da, tn), lambda i, j: (0, j)),
                  pl.BlockSpec((dc, tn), lambda i, j: (0, j)),
                  pl.BlockSpec((tm, tn), lambda i, j: (i, j))],
        out_specs=pl.BlockSpec((tm, tn), lambda i, j: (i, j)),
        out_shape=jax.ShapeDtypeStruct((T, D), F32),
        compiler_params=_params(("parallel", "arbitrary")),
    )(ma, mc, wa, wc, x)


def _peer_route_kernel(q_ref, sk_ref, u_ref, v_ref, e_ref, g_ref, ub_ref, vb_ref, sv_ref, si_ref):
    ub_ref[...] = u_ref[...].astype(BF16)
    vb_ref[...] = v_ref[...].astype(BF16)
    tt = q_ref.shape[0]
    K = PEER_TOPK
    sub = 8
    n_io = lax.broadcasted_iota(jnp.int32, (N_KEYS, tt), 0).astype(F32)
    r16 = lax.broadcasted_iota(jnp.int32, (16, tt), 0).astype(F32)
    r8 = lax.broadcasted_iota(jnp.int32, (8, tt), 0).astype(F32)
    ninf = -jnp.inf

    for h in range(PEER_HEADS):
        for c in range(2):
            col = (h * 2 + c) * HEAD_DIM
            s = _dot_nt(sk_ref[h * 2 + c], q_ref[:, col:col + HEAD_DIM])
            n_col = N_KEYS // sub
            vals = [s[j * sub:(j + 1) * sub, :] for j in range(n_col)]
            idxs = [n_io[j * sub:(j + 1) * sub, :] for j in range(n_col)]
            for rnd in range(n_col):
                for j in range(rnd % 2, n_col - 1, 2):
                    swap = vals[j + 1] > vals[j]
                    vals[j], vals[j + 1] = (jnp.where(swap, vals[j + 1], vals[j]),
                                            jnp.where(swap, vals[j], vals[j + 1]))
                    idxs[j], idxs[j + 1] = (jnp.where(swap, idxs[j + 1], idxs[j]),
                                            jnp.where(swap, idxs[j], idxs[j + 1]))
            for k in range(K):
                m = jnp.max(vals[0], axis=0, keepdims=True)
                idx = jnp.min(jnp.where(vals[0] == m, idxs[0], float(N_KEYS)), axis=0, keepdims=True)
                sv_ref[c, k:k + 1, :] = m
                si_ref[c, k:k + 1, :] = idx
                win = idxs[0] == idx
                for j in range(min(n_col, K) - 1 - k):
                    vals[j] = jnp.where(win, vals[j + 1], vals[j])
                    idxs[j] = jnp.where(win, idxs[j + 1], idxs[j])
        sv0, sv1 = sv_ref[0], sv_ref[1]
        si0, si1 = si_ref[0], si_ref[1]
        cands, flats, exps = [], [], []
        for a, nb, rows in ((0, 16, 16), (1, 8, 8), (2, 5, 8), (3, 4, 8)):
            r_io = r16 if rows == 16 else r8
            val = sv0[a:a + 1, :] + sv1[0:rows, :]
            cands.append(jnp.where(r_io < nb, val, ninf))
            flats.append(a * 16.0 + r_io)
            exps.append(si0[a:a + 1, :] * float(N_KEYS) + si1[0:rows, :])
        for b, lo, hi, rows in ((0, 4, 16, 16), (1, 4, 8, 8), (2, 4, 5, 8)):
            r_io = r16 if rows == 16 else r8
            val = sv0[0:rows, :] + sv1[b:b + 1, :]
            cands.append(jnp.where((r_io >= lo) & (r_io < hi), val, ninf))
            flats.append(r_io * 16.0 + b)
            exps.append(si0[0:rows, :] * float(N_KEYS) + si1[b:b + 1, :])
        cand = jnp.concatenate(cands, axis=0)
        flat = jnp.concatenate(flats, axis=0)
        expt = jnp.concatenate(exps, axis=0)
        cvs, exs = [], []
        for k in range(K):
            m = jnp.max(cand, axis=0, keepdims=True)
            fsel = jnp.min(jnp.where(cand == m, flat, 1e9), axis=0, keepdims=True)
            hit = flat == fsel
            exs.append(jnp.max(jnp.where(hit, expt, -1.0), axis=0, keepdims=True))
            cand = jnp.where(hit, ninf, cand)
            cvs.append(m)
        cv = jnp.concatenate(cvs, axis=0)
        ex = jnp.concatenate(exs, axis=0)
        ev = jnp.exp(cv - jnp.max(cv, axis=0, keepdims=True))
        gates = ev / jnp.sum(ev, axis=0, keepdims=True)
        e_ref[0, h * K:(h + 1) * K, :] = ex.astype(jnp.int32)
        g_ref[0, h * K:(h + 1) * K, :] = gates


def peer_route(qp, subkeys, u, v):
    T = qp.shape[0]
    tt = 128
    P = PEER_HEADS * PEER_TOPK
    nt = T // tt
    E, D = u.shape
    slab = E // nt
    assert slab * nt == E and slab % 16 == 0
    tab_spec = pl.BlockSpec((slab, D), lambda i: (i, 0))
    return pl.pallas_call(
        _peer_route_kernel,
        grid=(nt,),
        in_specs=[pl.BlockSpec((tt, qp.shape[1]), lambda i: (i, 0)),
                  pl.BlockSpec(subkeys.shape, lambda i: (0, 0, 0)),
                  tab_spec, tab_spec],
        out_specs=[pl.BlockSpec((1, P, tt), lambda i: (i, 0, 0))] * 2 + [tab_spec, tab_spec],
        out_shape=[jax.ShapeDtypeStruct((nt, P, tt), jnp.int32),
                   jax.ShapeDtypeStruct((nt, P, tt), F32),
                   jax.ShapeDtypeStruct((E, D), BF16), jax.ShapeDtypeStruct((E, D), BF16)],
        scratch_shapes=[pltpu.VMEM((2, PEER_TOPK, tt), F32), pltpu.VMEM((2, PEER_TOPK, tt), F32)],
        compiler_params=_params(("parallel",)),
    )(qp, subkeys, u, v)


def _peer_gbuild_kernel(e_ref, g_ref, o_ref, i1_ref, i2_ref, gt_ref, s_ref):
    tt = e_ref.shape[2]
    e = e_ref[0].T
    i1_ref[...] = e >> 7
    i2_ref[...] = e & (N_KEYS - 1)
    gt_ref[...] = g_ref[0].T
    P = e.shape[1]
    k_io = lax.broadcasted_iota(jnp.int32, (N_KEYS, P), 0)
    sub = 8
    half = tt // 2

    def build(t0):
        r1s = i1_ref[t0:t0 + half, :]
        r2s = i2_ref[t0:t0 + half, :]
        rgs = gt_ref[t0:t0 + half, :]
        for u in range(half):
            lhs = jnp.where(k_io == r1s[u:u + 1, :], rgs[u:u + 1, :], 0.0).astype(BF16)
            rhs = jnp.where(k_io == r2s[u:u + 1, :], 1.0, 0.0).astype(BF16)
            g_t = _dot_nt(lhs, rhs)
            for a in range(N_KEYS // sub):
                row0 = (a * tt + t0 + u) * sub
                s_ref[row0:row0 + sub, :] = g_t[a * sub:(a + 1) * sub, :]

    def regroup(t0):
        for i1 in range(N_KEYS):
            a, r = divmod(i1, sub)
            o_ref[i1, t0:t0 + half, :] = s_ref[pl.ds((a * tt + t0) * sub + r, half, stride=sub), :].astype(o_ref.dtype)

    build(0)
    build(half)
    regroup(0)
    regroup(half)


def peer_gbuild(ex, gates):
    nt, P, tt = ex.shape
    T = nt * tt
    return pl.pallas_call(
        _peer_gbuild_kernel,
        grid=(nt,),
        in_specs=[pl.BlockSpec((1, P, tt), lambda i: (i, 0, 0))] * 2,
        out_specs=pl.BlockSpec((N_KEYS, tt, N_KEYS), lambda i: (0, i, 0)),
        out_shape=jax.ShapeDtypeStruct((N_KEYS, T, N_KEYS), BF16),
        scratch_shapes=[pltpu.VMEM((tt, P), jnp.int32), pltpu.VMEM((tt, P), jnp.int32),
                        pltpu.VMEM((tt, P), F32), pltpu.VMEM((tt * N_KEYS, N_KEYS), F32)],
        compiler_params=_params(("parallel",)),
    )(ex, gates)


def _peer_dense_kernel(h_ref, nw_ref, fw_ref, u_ref, v_ref, gm_ref, o_ref, hn_ref, acc_ref):
    j = pl.program_id(1)

    @pl.when(j == 0)
    def _():
        x = h_ref[...]
        y = x * lax.rsqrt(jnp.mean(x * x, axis=-1, keepdims=True) + EPS)
        hn_ref[...] = (y * nw_ref[...]).astype(BF16)
        acc_ref[...] = jnp.zeros(acc_ref.shape, F32)

    a = _dot_nt(hn_ref[...], u_ref[...])
    gm = jnp.concatenate([gm_ref[k] for k in range(gm_ref.shape[0])], axis=1)
    w = (gm.astype(F32) * jax.nn.gelu(a)).astype(BF16)
    acc_ref[...] += _dot(w, v_ref[...])

    @pl.when(j == pl.num_programs(1) - 1)
    def _():
        x = h_ref[...] + acc_ref[...]
        y = x * lax.rsqrt(jnp.mean(x * x, axis=-1, keepdims=True) + EPS)
        o_ref[...] = y * fw_ref[...]


def peer_dense(h, ffn_nw, final_nw, u, v, gm, tt, ec):
    T, D = h.shape
    E = u.shape[0]
    return pl.pallas_call(
        _peer_dense_kernel,
        grid=(T // tt, E // ec),
        in_specs=[pl.BlockSpec((tt, D), lambda i, j: (i, 0)),
                  pl.BlockSpec((1, D), lambda i, j: (0, 0)),
                  pl.BlockSpec((1, D), lambda i, j: (0, 0)),
                  pl.BlockSpec((ec, D), lambda i, j: (j, 0)),
                  pl.BlockSpec((ec, D), lambda i, j: (j, 0)),
                  pl.BlockSpec((ec // N_KEYS, tt, N_KEYS), lambda i, j: (j, i, 0))],
        out_specs=pl.BlockSpec((tt, D), lambda i, j: (i, 0)),
        out_shape=jax.ShapeDtypeStruct((T, D), F32),
        scratch_shapes=[pltpu.VMEM((tt, D), BF16), pltpu.VMEM((tt, D), F32)],
        compiler_params=_params(("parallel", "arbitrary")),
    )(h, ffn_nw.reshape(1, D), final_nw.reshape(1, D), u, v, gm)


def _t5_bucket_np(n_dist):
    d = np.arange(n_dist)
    max_exact = NUM_BUCKETS // 2
    nf = np.maximum(d, 1).astype(np.float64)
    large = max_exact + (np.log(nf / max_exact) / math.log(MAX_DISTANCE / max_exact)
                         * (NUM_BUCKETS - max_exact)).astype(np.int64)
    large = np.minimum(large, NUM_BUCKETS - 1)
    return np.where(d < max_exact, d, large).astype(np.int32)


def _overlap_t_np(S):
    n_c = (S - L_CMP) // STRIDE_CMP + 1
    n_sel = S // L_SEL
    pos = np.arange(n_c)[:, None] * STRIDE_CMP + np.arange(L_CMP)[None, :]
    m = np.zeros((n_c + 1, n_sel), np.float32)
    np.add.at(m, (np.repeat(np.arange(n_c), L_CMP), (pos // L_SEL).reshape(-1)), 1.0 / L_CMP)
    return np.ascontiguousarray(m.T)


def _block_onehot_padded_np(S):
    assert S // L_SEL < HEAD_DIM
    e2 = np.zeros((WINDOW + S, HEAD_DIM), np.float32)
    e2[WINDOW + np.arange(S), np.arange(S) // L_SEL] = 1.0
    e2[:WINDOW, HEAD_DIM - 1] = SEL_OFF
    return e2


def nsa_conv_mix(xt, B, S, attn_norm_w, w_in, w_cmp_k, w_cmp_v, cmp_pos, conv_w, conv_b,
                 attn_gnw, conv_gnw, rel_bias, tm=1024, tm_conv=512):
    T, D = xt.shape
    dq = N_Q_HEADS * HEAD_DIM
    dkv = N_KV_HEADS * HEAD_DIM
    n_attn = dq + 6 * dkv
    n_gate = 3 * N_Q_HEADS
    dc = (w_in.shape[1] - n_attn - n_gate) // 3
    w_t = w_in.T
    w_attn = cast_rows(w_t, 0, n_attn)
    w_conv = cast_rows(w_t, n_attn + n_gate, 3 * dc)
    w_gate = jnp.pad(w_t[n_attn:n_attn + n_gate], ((0, HEAD_DIM - n_gate), (0, 0))).astype(BF16)

    qkv = norm_matmul(xt, attn_norm_w, w_attn, BF16, tm, n_attn // 2, w_is_nk=True)
    gl = norm_matmul(xt, attn_norm_w, w_gate, F32, tm, HEAD_DIM, w_is_nk=True)
    gc = norm_matmul(xt, attn_norm_w, w_conv, F32, tm, dc, w_is_nk=True)

    n16 = S // STRIDE_CMP
    k16 = qkv[:, dq:dq + dkv].reshape(B * n16, STRIDE_CMP * dkv)
    v16 = qkv[:, dq + dkv:dq + 2 * dkv].reshape(B * n16, STRIDE_CMP * dkv)

    def wbig(w, lo):
        wl = w[lo:lo + STRIDE_CMP]
        eye = jnp.eye(N_KV_HEADS, dtype=w.dtype)
        return jnp.einsum('lde,hg->lhdge', wl, eye).reshape(STRIDE_CMP * dkv, dkv).astype(BF16)

    def posrow(lo):
        p = cmp_pos[lo:lo + STRIDE_CMP]
        return jnp.broadcast_to(p[:, None, :], (STRIDE_CMP, N_KV_HEADS, HEAD_DIM)).reshape(1, STRIDE_CMP * dkv)

    kc, vc = compress(k16, v16, posrow(0), posrow(STRIDE_CMP),
                      wbig(w_cmp_k, 0), wbig(w_cmp_k, STRIDE_CMP),
                      wbig(w_cmp_v, 0), wbig(w_cmp_v, STRIDE_CMP), B)

    tab = rel_bias[_t5_bucket_np(HEAD_DIM)].T
    ovt = jnp.asarray(_overlap_t_np(S), BF16)
    e2p = jnp.asarray(_block_onehot_padded_np(S), BF16)

    ocmp, selb = cmp_select(qkv, kc, vc, tab, ovt, B, S)
    def keys_padded(col):
        k = qkv[:, col:col + dkv].reshape(B, S, dkv)
        return jnp.pad(k, ((0, 0), (WINDOW, 0), (0, 0))).reshape(B * (WINDOW + S), dkv)

    def values_t_padded(col):
        v = qkv[:, col:col + dkv].reshape(B, S, dkv).transpose(0, 2, 1)
        return jnp.pad(v, ((0, 0), (0, 0), (WINDOW, 0))).reshape(B * dkv, WINDOW + S)

    attn = sel_win(qkv, selb, keys_padded(dq + 2 * dkv), values_t_padded(dq + 3 * dkv),
                   keys_padded(dq + 4 * dkv), values_t_padded(dq + 5 * dkv), e2p, tab, ocmp, gl,
                   attn_gnw.reshape(1, dq), B, S)
    cw8 = jnp.pad(conv_w, ((0, 8 - conv_w.shape[0]), (0, 0)))
    conv = conv_mixer(gc, cw8, conv_b.reshape(1, dc), conv_gnw.reshape(1, dc), S, tm_conv)
    return attn, conv


def peer_block(h, ffn_nw, final_nw, peer_wq, peer_subkeys, peer_u, peer_v, tm=1024, tt=512, ec=1024):
    T, D = h.shape
    qp = norm_matmul(h, ffn_nw, peer_wq.astype(BF16), BF16, tm, 1024)
    sk = peer_subkeys.reshape(PEER_HEADS * 2, N_KEYS, peer_subkeys.shape[-1]).astype(BF16)
    ex, gates, u_bf, v_bf = peer_route(qp, sk, peer_u, peer_v)
    gm = peer_gbuild(ex, gates)
    return peer_dense(h, ffn_nw, final_nw, u_bf, v_bf, gm, tt, ec)


def kernel(x, attn_norm_w, w_in, w_cmp_k, w_cmp_v, cmp_pos, conv_w, conv_b, attn_group_norm_w,
           conv_group_norm_w, w_out, rel_bias, ffn_norm_w, peer_wq, peer_subkeys, peer_u, peer_v,
           final_norm_w):
    B, S, D = x.shape
    T = B * S
    xt = x.reshape(T, D)
    attn, conv = nsa_conv_mix(xt, B, S, attn_norm_w[0], w_in[0], w_cmp_k[0], w_cmp_v[0], cmp_pos[0],
                              conv_w[0], conv_b[0], attn_group_norm_w[0], conv_group_norm_w[0], rel_bias)
    da = attn.shape[1]
    wo = w_out[0].astype(BF16)
    h = out_proj(attn, conv, wo[:da], wo[da:], xt, 1024, 1024)
    out = peer_block(h, ffn_norm_w[0], final_norm_w, peer_wq[0], peer_subkeys[0], peer_u[0], peer_v[0])
    return out.reshape(B, S, D)
```

```python
import functools
import math

import jax
import jax.numpy as jnp
import numpy as np
from jax import lax
from jax.experimental import pallas as pl
from jax.experimental.pallas import tpu as pltpu

F32 = jnp.float32
BF16 = jnp.bfloat16

HEAD_DIM = 128
N_KV_HEADS = 2
GQA = 4
N_Q_HEADS = N_KV_HEADS * GQA
L_CMP = 32
STRIDE_CMP = 16
L_SEL = 64
N_SEL = 16
WINDOW = 512
Q_BLOCK = 128
FORCED_SCORE = float(GQA + 1)
NUM_BUCKETS = 32
MAX_DISTANCE = 128
PEER_HEADS = 8
N_KEYS = 128
PEER_TOPK = 16
EPS = 1e-6
NEG_BIG = -1e30
SEL_OFF = -float(2 ** 30)
LOG2E = math.log2(math.e)
HALF_KEYS = N_KEYS // 2
PACK_ROWS = 8
VMEM_LIMIT = 56 * 1024 * 1024


def _dot(a, b):
    return jnp.dot(a, b, preferred_element_type=F32)


def _dot_nt(a, b):
    return lax.dot_general(a, b, (((1,), (1,)), ((), ())), preferred_element_type=F32)


def _params(sem, vmem=VMEM_LIMIT):
    return pltpu.CompilerParams(dimension_semantics=sem, vmem_limit_bytes=vmem)


def _norm_matmul_kernel(x_ref, nw_ref, w_ref, o_ref, xn_ref, *, w_is_nk):
    @pl.when(pl.program_id(1) == 0)
    def _():
        x = x_ref[...]
        y = x * lax.rsqrt(jnp.mean(x * x, axis=-1, keepdims=True) + EPS)
        xn_ref[...] = (y * nw_ref[...]).astype(BF16)

    dot = _dot_nt if w_is_nk else _dot
    o_ref[...] = dot(xn_ref[...], w_ref[...]).astype(o_ref.dtype)


def norm_matmul(x, norm_w, w, out_dtype, tm, tn, w_is_nk=False):
    T, D = x.shape
    N = w.shape[0] if w_is_nk else w.shape[1]
    w_spec = pl.BlockSpec((tn, D), lambda i, j: (j, 0)) if w_is_nk else pl.BlockSpec((D, tn), lambda i, j: (0, j))
    return pl.pallas_call(
        functools.partial(_norm_matmul_kernel, w_is_nk=w_is_nk),
        grid=(T // tm, N // tn),
        in_specs=[pl.BlockSpec((tm, D), lambda i, j: (i, 0)),
                  pl.BlockSpec((1, D), lambda i, j: (0, 0)),
                  w_spec],
        out_specs=pl.BlockSpec((tm, tn), lambda i, j: (i, j)),
        out_shape=jax.ShapeDtypeStruct((T, N), out_dtype),
        scratch_shapes=[pltpu.VMEM((tm, D), BF16)],
        compiler_params=_params(("parallel", "arbitrary")),
    )(x, norm_w.reshape(1, D), w)


def _cast_rows_kernel(w_ref, o_ref):
    o_ref[...] = w_ref[...].astype(o_ref.dtype)


def cast_rows(w, row0, n_rows, tr=256):
    K = w.shape[1]
    assert row0 % 8 == 0 and n_rows % tr == 0
    return pl.pallas_call(
        _cast_rows_kernel,
        grid=(n_rows // tr,),
        in_specs=[pl.BlockSpec((pl.Element(tr), pl.Element(K)), lambda i: (pl.multiple_of(row0 + i * tr, 8), 0))],
        out_specs=pl.BlockSpec((tr, K), lambda i: (i, 0)),
        out_shape=jax.ShapeDtypeStruct((n_rows, K), BF16),
        compiler_params=_params(("parallel",)),
    )(w)


def _compress_kernel(k_ref, v_ref, pa_ref, pb_ref, wka_ref, wkb_ref, wva_ref, wvb_ref, kc_ref, vc_ref):
    def one(x_ref, wa_ref, wb_ref, o_ref):
        x = x_ref[...].astype(F32)
        a = _dot((x + pa_ref[...]).astype(BF16), wa_ref[...])
        b = _dot((x + pb_ref[...]).astype(BF16), wb_ref[...])
        n = b.shape[0]
        o_ref[...] = (a + pltpu.roll(b, n - 1, 0)).astype(o_ref.dtype)

    one(k_ref, wka_ref, wkb_ref, kc_ref)
    one(v_ref, wva_ref, wvb_ref, vc_ref)


def compress(k16, v16, pos_a, pos_b, wka, wkb, wva, wvb, B):
    R, C = k16.shape
    nb = R // B
    dkv = wka.shape[1]
    full = lambda shp: pl.BlockSpec(shp, lambda b: (0, 0))
    return pl.pallas_call(
        _compress_kernel,
        grid=(B,),
        in_specs=[pl.BlockSpec((nb, C), lambda b: (b, 0)),
                  pl.BlockSpec((nb, C), lambda b: (b, 0)),
                  full((1, C)), full((1, C)),
                  full((C, dkv)), full((C, dkv)), full((C, dkv)), full((C, dkv))],
        out_specs=[pl.BlockSpec((nb, dkv), lambda b: (b, 0))] * 2,
        out_shape=[jax.ShapeDtypeStruct((R, dkv), BF16)] * 2,
        compiler_params=_params(("parallel",)),
    )(k16, v16, pos_a, pos_b, wka, wkb, wva, wvb)


def _bias_from_dist(tab_row, dist):
    idx = jnp.clip(dist, 0, 127)
    w = tab_row.shape[1]
    tab = jnp.broadcast_to(tab_row, (idx.shape[0], w))
    parts = [jnp.take_along_axis(tab, idx[:, k:k + w], axis=1) for k in range(0, idx.shape[1], w)]
    return parts[0] if len(parts) == 1 else jnp.concatenate(parts, axis=1)


def _cmp_select_kernel(q_ref, kc_ref, vc_ref, tab_ref, ovt_ref, ocmp_ref, selb_ref, *, n_qb, scale):
    c = pl.program_id(0) % n_qb
    tq = q_ref.shape[0]
    n_c = kc_ref.shape[0]
    n_sel = ovt_ref.shape[0]
    t0 = c * tq
    t_col = t0 + lax.broadcasted_iota(jnp.int32, (tq, n_c), 0)
    n_row = lax.broadcasted_iota(jnp.int32, (tq, n_c), 1)
    dist = t_col - (n_row * STRIDE_CMP + (L_CMP - 1))
    valid = dist >= 0

    j_io = lax.broadcasted_iota(jnp.int32, (n_sel, tq), 0)
    t_io = t0 + lax.broadcasted_iota(jnp.int32, (n_sel, tq), 1)
    blk_t = t_io // L_SEL
    forced = (j_io == 0) | (j_io == blk_t) | (j_io == blk_t - 1)
    causal_blk = j_io * L_SEL <= t_io

    for h in range(N_KV_HEADS):
        kc = kc_ref[:, h * HEAD_DIM:(h + 1) * HEAD_DIM]
        vc = vc_ref[:, h * HEAD_DIM:(h + 1) * HEAD_DIM]
        psum = jnp.zeros((tq, n_c), F32)
        for g in range(GQA):
            hd = h * GQA + g
            qh = q_ref[:, hd * HEAD_DIM:(hd + 1) * HEAD_DIM]
            bias = _bias_from_dist(tab_ref[hd:hd + 1, :], dist)
            s = _dot_nt(qh, kc) * scale + bias
            s = jnp.where(valid, s, NEG_BIG)
            m = jnp.max(s, axis=-1, keepdims=True)
            e = jnp.where(valid, jnp.exp(s - m), 0.0)
            d = jnp.sum(e, axis=-1, keepdims=True)
            p = e / jnp.where(d > 0, d, 1.0)
            ocmp_ref[:, hd * HEAD_DIM:(hd + 1) * HEAD_DIM] = _dot(p.astype(BF16), vc)
            psum = psum + p
        p_hi = psum.astype(BF16)
        p_lo = (psum - p_hi.astype(F32)).astype(BF16)
        ovt = ovt_ref[...]
        imp = _dot_nt(ovt, p_hi) + _dot_nt(ovt, p_lo)
        score = jnp.where(forced, FORCED_SCORE, jnp.where(causal_blk, imp, -1.0))
        rank = jnp.zeros((n_sel, tq), F32)
        for jp in range(n_sel):
            row = score[jp:jp + 1, :]
            rank = rank + jnp.where(j_io > jp, jnp.where(row >= score, 1.0, 0.0),
                                    jnp.where(row > score, 1.0, 0.0))
        selb = jnp.where(rank < float(N_SEL), 0.0, SEL_OFF)
        if n_sel < HEAD_DIM:
            selb = jnp.concatenate([selb, jnp.zeros((HEAD_DIM - n_sel, tq), F32)], axis=0)
        selb_ref[:, h * HEAD_DIM:(h + 1) * HEAD_DIM] = selb.T.astype(BF16)


def cmp_select(qkv, kc, vc, tab, ovt, B, S, tq=2 * Q_BLOCK):
    T = qkv.shape[0]
    n_qb = S // tq
    n_c = kc.shape[0] // B
    dq = N_Q_HEADS * HEAD_DIM
    dkv = N_KV_HEADS * HEAD_DIM
    kern = functools.partial(_cmp_select_kernel, n_qb=n_qb, scale=HEAD_DIM ** -0.5)
    return pl.pallas_call(
        kern,
        grid=(T // tq,),
        in_specs=[pl.BlockSpec((tq, dq), lambda i: (i, 0)),
                  pl.BlockSpec((n_c, dkv), lambda i: (i // n_qb, 0)),
                  pl.BlockSpec((n_c, dkv), lambda i: (i // n_qb, 0)),
                  pl.BlockSpec(tab.shape, lambda i: (0, 0)),
                  pl.BlockSpec(ovt.shape, lambda i: (0, 0))],
        out_specs=[pl.BlockSpec((tq, dq), lambda i: (i, 0)),
                   pl.BlockSpec((tq, dkv), lambda i: (i, 0))],
        out_shape=[jax.ShapeDtypeStruct((T, dq), F32),
                   jax.ShapeDtypeStruct((T, dkv), BF16)],
        compiler_params=_params(("parallel",)),
    )(qkv, kc, vc, tab, ovt)


def _flash_chunk(k, vt, q, bias, m_ref, l_ref, acc_ref, scale):
    s = _dot_nt(k, q) * scale + bias
    m_old = m_ref[...]
    m_new = jnp.maximum(m_old, jnp.max(s, axis=0, keepdims=True))
    alpha = jnp.exp2(m_old - m_new)
    p = jnp.exp2(s - m_new)
    l_ref[...] = alpha * l_ref[...] + jnp.sum(p, axis=0, keepdims=True)
    acc_ref[...] = alpha * acc_ref[...] + _dot(vt, p.astype(BF16))
    m_ref[...] = m_new


def _sel_win_kernel(q_ref, selb_ref, ks_ref, vst_ref, kw_ref, vwt_ref, e2_ref, tab_ref, ocmp_ref,
                    gl_ref, gnw_ref, o_ref, m_ref, l_ref, acc_ref, *, n_qb, scale):
    c = pl.program_id(0) % n_qb
    tq = q_ref.shape[0]
    rows = GQA * tq
    n_back = WINDOW // tq
    near = WINDOW + tq
    j_io = lax.broadcasted_iota(jnp.int32, (tq, tq), 0)
    i_io = lax.broadcasted_iota(jnp.int32, (tq, tq), 1)
    dij = i_io - j_io
    dij4 = jnp.concatenate([dij] * GQA, axis=1)
    causal = dij4 >= 0
    pad_col = jnp.where(lax.broadcasted_iota(jnp.int32, (rows, HEAD_DIM), 1) == HEAD_DIM - 1, 1.0, 0.0).astype(BF16)
    sig = jax.nn.sigmoid(gl_ref[...])
    near0 = pl.multiple_of(c * tq, tq)
    n_far = jnp.maximum(c - n_back, 0)
    n_full = n_far // n_back
    n_rem = n_far - n_full * n_back

    def hsl(h):
        return slice(h * HEAD_DIM, (h + 1) * HEAD_DIM)

    def far0(i):
        return pl.multiple_of(WINDOW + i * WINDOW, WINDOW)

    def reset():
        m_ref[...] = jnp.full(m_ref.shape, NEG_BIG, F32)
        l_ref[...] = jnp.zeros(l_ref.shape, F32)
        acc_ref[...] = jnp.zeros(acc_ref.shape, F32)

    def result():
        ot = acc_ref[...] / l_ref[...]
        return [ot[:, g * tq:(g + 1) * tq].T for g in range(GQA)]

    for h in range(N_KV_HEADS):
        q4 = jnp.concatenate([q_ref[:, (h * GQA + g) * HEAD_DIM:(h * GQA + g + 1) * HEAD_DIM]
                              for g in range(GQA)], axis=0)
        sb4 = jnp.concatenate([selb_ref[:, hsl(h)]] * GQA, axis=0)
        q_aug = jnp.concatenate([q4, sb4 + pad_col], axis=1)
        q_win = jnp.concatenate([q4, pad_col], axis=1)
        tabs = [tab_ref[h * GQA + g:h * GQA + g + 1, :] * LOG2E for g in range(GQA)]
        d0 = jnp.concatenate([_bias_from_dist(t, dij) for t in tabs], axis=1)
        d1 = jnp.concatenate([_bias_from_dist(t, dij + tq) for t in tabs], axis=1)
        far = jnp.concatenate([t[:, HEAD_DIM - 1:HEAD_DIM] + jnp.zeros((1, tq), F32) for t in tabs], axis=1)
        far_t = jnp.broadcast_to(far, (tq, rows))
        diag = jnp.where(causal, d0, NEG_BIG)
        bias_sel = jnp.concatenate([far_t] * (n_back - 1) + [d1, diag], axis=0)
        bias_win = jnp.concatenate([jnp.where(dij4 < 0, far_t, NEG_BIG)] + [far_t] * (n_back - 2) + [d1, diag],
                                   axis=0)

        reset()
        e2_near = e2_ref[pl.ds(near0, near), :]
        k_near = jnp.concatenate([ks_ref[pl.ds(near0, near), hsl(h)], e2_near], axis=1)
        _flash_chunk(k_near, vst_ref[hsl(h), pl.ds(near0, near)], q_aug, bias_sel, m_ref, l_ref, acc_ref, scale)

        def far_chunk(i, bias):
            r0 = far0(i)
            k = jnp.concatenate([ks_ref[pl.ds(r0, WINDOW), hsl(h)], e2_ref[pl.ds(r0, WINDOW), :]], axis=1)
            _flash_chunk(k, vst_ref[hsl(h), pl.ds(r0, WINDOW)], q_aug, bias, m_ref, l_ref, acc_ref, scale)

        def body(i, carry):
            far_chunk(i, far)
            return carry

        lax.fori_loop(0, n_full, body, 0)

        @pl.when(n_rem > 0)
        def _():
            if n_back == 2:
                r0 = far0(n_full)
                k = jnp.concatenate([ks_ref[pl.ds(r0, tq), hsl(h)], e2_ref[pl.ds(r0, tq), :]], axis=1)
                _flash_chunk(k, vst_ref[hsl(h), pl.ds(r0, tq)], q_aug, far, m_ref, l_ref, acc_ref, scale)
            else:
                live = lax.broadcasted_iota(jnp.int32, (WINDOW, rows), 0) < n_rem * tq
                far_chunk(n_full, jnp.where(live, jnp.broadcast_to(far, (WINDOW, rows)), NEG_BIG))

        o_sel = result()

        reset()
        kw_near = jnp.concatenate([kw_ref[pl.ds(near0, near), hsl(h)], e2_near], axis=1)
        _flash_chunk(kw_near, vwt_ref[hsl(h), pl.ds(near0, near)], q_win, bias_win, m_ref, l_ref, acc_ref, scale)
        o_win = result()

        for g in range(GQA):
            hd = h * GQA + g
            o = (sig[:, 3 * hd:3 * hd + 1] * ocmp_ref[:, hsl(hd)]
                 + sig[:, 3 * hd + 1:3 * hd + 2] * o_sel[g]
                 + sig[:, 3 * hd + 2:3 * hd + 3] * o_win[g])
            y = o * lax.rsqrt(jnp.mean(o * o, axis=-1, keepdims=True) + EPS)
            o_ref[:, hsl(hd)] = (y * gnw_ref[:, hsl(hd)]).astype(o_ref.dtype)


def sel_win(qkv, selb, ksp, vstp, kwp, vwtp, e2p, tab, ocmp, gl, gnw, B, S, tq=2 * Q_BLOCK):
    T = qkv.shape[0]
    n_qb = S // tq
    dq = N_Q_HEADS * HEAD_DIM
    dkv = N_KV_HEADS * HEAD_DIM
    SP = S + WINDOW
    kern = functools.partial(_sel_win_kernel, n_qb=n_qb, scale=HEAD_DIM ** -0.5 * LOG2E)
    k_spec = pl.BlockSpec((SP, dkv), lambda i: (i // n_qb, 0))
    vt_spec = pl.BlockSpec((dkv, SP), lambda i: (i // n_qb, 0))
    return pl.pallas_call(
        kern,
        grid=(T // tq,),
        in_specs=[pl.BlockSpec((tq, dq), lambda i: (i, 0)),
                  pl.BlockSpec((tq, dkv), lambda i: (i, 0)),
                  k_spec, vt_spec, k_spec, vt_spec,
                  pl.BlockSpec((SP, HEAD_DIM), lambda i: (0, 0)),
                  pl.BlockSpec(tab.shape, lambda i: (0, 0)),
                  pl.BlockSpec((tq, dq), lambda i: (i, 0)),
                  pl.BlockSpec((tq, HEAD_DIM), lambda i: (i, 0)),
                  pl.BlockSpec((1, dq), lambda i: (0, 0))],
        out_specs=pl.BlockSpec((tq, dq), lambda i: (i, 0)),
        out_shape=jax.ShapeDtypeStruct((T, dq), BF16),
        scratch_shapes=[pltpu.VMEM((1, GQA * tq), F32), pltpu.VMEM((1, GQA * tq), F32),
                        pltpu.VMEM((HEAD_DIM, GQA * tq), F32)],
        compiler_params=_params(("parallel",)),
    )(qkv, selb, ksp, vstp, kwp, vwtp, e2p, tab, ocmp, gl, gnw)


def _conv_kernel(b_ref, c_ref, h_ref, cp_ref, hp_ref, cw_ref, cb_ref, gnw_ref, o_ref, u_ref, *, tiles_per_seq):
    tm = b_ref.shape[0]
    first = (pl.program_id(0) % tiles_per_seq) == 0
    u_prev = cp_ref[...] * hp_ref[...]
    u_ref[0:8, :] = jnp.where(first, 0.0, u_prev)
    u = c_ref[...] * h_ref[...]
    u_ref[8:8 + tm, :] = u
    y = (cw_ref[0:1, :] * u_ref[6:6 + tm, :] + cw_ref[1:2, :] * u_ref[7:7 + tm, :]
         + cw_ref[2:3, :] * u + cb_ref[...])
    o = b_ref[...] * y
    n_groups = o.shape[1] // HEAD_DIM
    for g in range(n_groups):
        sl = slice(g * HEAD_DIM, (g + 1) * HEAD_DIM)
        og = o[:, sl]
        yg = og * lax.rsqrt(jnp.mean(og * og, axis=-1, keepdims=True) + EPS)
        o_ref[:, sl] = (yg * gnw_ref[:, sl]).astype(o_ref.dtype)


def conv_mixer(gc, conv_w, conv_b, gnw, S, tm):
    T = gc.shape[0]
    dc = conv_w.shape[1]
    tps = S // tm
    kern = functools.partial(_conv_kernel, tiles_per_seq=tps)
    prev = lambda col: pl.BlockSpec((8, dc), lambda i, col=col: (jnp.maximum(i * (tm // 8) - 1, 0), col))
    cur = lambda col: pl.BlockSpec((tm, dc), lambda i, col=col: (i, col))
    return pl.pallas_call(
        kern,
        grid=(T // tm,),
        in_specs=[cur(0), cur(1), cur(2), prev(1), prev(2),
                  pl.BlockSpec((8, dc), lambda i: (0, 0)),
                  pl.BlockSpec((1, dc), lambda i: (0, 0)),
                  pl.BlockSpec((1, dc), lambda i: (0, 0))],
        out_specs=pl.BlockSpec((tm, dc), lambda i: (i, 0)),
        out_shape=jax.ShapeDtypeStruct((T, dc), BF16),
        scratch_shapes=[pltpu.VMEM((tm + 8, dc), F32)],
        compiler_params=_params(("parallel",)),
    )(gc, gc, gc, gc, gc, conv_w, conv_b, gnw)


def _out_proj_kernel(ma_ref, mc_ref, wa_ref, wc_ref, x_ref, o_ref):
    o_ref[...] = x_ref[...] + _dot(ma_ref[...], wa_ref[...]) + _dot(mc_ref[...], wc_ref[...])


def out_proj(ma, mc, wa, wc, x, tm, tn):
    T, da = ma.shape
    dc = mc.shape[1]
    D = x.shape[1]
    return pl.pallas_call(
        _out_proj_kernel,
        grid=(T // tm, D // tn),
        in_specs=[pl.BlockSpec((tm, da), lambda i, j: (i, 0)),
                  pl.BlockSpec((tm, dc), lambda i, j: (i, 0)),
                  pl.BlockSpec((da, tn), lambda i, j: (0, j)),
                  pl.BlockSpec((dc, tn), lambda i, j: (0, j)),
                  pl.BlockSpec((tm, tn), lambda i, j: (i, j))],
        out_specs=pl.BlockSpec((tm, tn), lambda i, j: (i, j)),
        out_shape=jax.ShapeDtypeStruct((T, D), F32),
        compiler_params=_params(("parallel", "arbitrary")),
    )(ma, mc, wa, wc, x)


def _peer_route_kernel(q_ref, sk_ref, u_ref, v_ref, e_ref, g_ref, ub_ref, vb_ref, sv_ref, si_ref):
    ub_ref[...] = u_ref[...].astype(BF16)
    vb_ref[...] = v_ref[...].astype(BF16)
    tt = q_ref.shape[0]
    K = PEER_TOPK
    sub = 8
    n_io = lax.broadcasted_iota(jnp.int32, (N_KEYS, tt), 0).astype(F32)
    r16 = lax.broadcasted_iota(jnp.int32, (16, tt), 0).astype(F32)
    r8 = lax.broadcasted_iota(jnp.int32, (8, tt), 0).astype(F32)
    ninf = -jnp.inf

    for h in range(PEER_HEADS):
        for c in range(2):
            col = (h * 2 + c) * HEAD_DIM
            s = _dot_nt(sk_ref[h * 2 + c], q_ref[:, col:col + HEAD_DIM])
            n_col = N_KEYS // sub
            vals = [s[j * sub:(j + 1) * sub, :] for j in range(n_col)]
            idxs = [n_io[j * sub:(j + 1) * sub, :] for j in range(n_col)]
            for rnd in range(n_col):
                for j in range(rnd % 2, n_col - 1, 2):
                    swap = vals[j + 1] > vals[j]
                    vals[j], vals[j + 1] = (jnp.where(swap, vals[j + 1], vals[j]),
                                            jnp.where(swap, vals[j], vals[j + 1]))
                    idxs[j], idxs[j + 1] = (jnp.where(swap, idxs[j + 1], idxs[j]),
                                            jnp.where(swap, idxs[j], idxs[j + 1]))
            for k in range(K):
                m = jnp.max(vals[0], axis=0, keepdims=True)
                idx = jnp.min(jnp.where(vals[0] == m, idxs[0], float(N_KEYS)), axis=0, keepdims=True)
                sv_ref[c, k:k + 1, :] = m
                si_ref[c, k:k + 1, :] = idx
                win = idxs[0] == idx
                for j in range(min(n_col, K) - 1 - k):
                    vals[j] = jnp.where(win, vals[j + 1], vals[j])
                    idxs[j] = jnp.where(win, idxs[j + 1], idxs[j])
        sv0, sv1 = sv_ref[0], sv_ref[1]
        si0, si1 = si_ref[0], si_ref[1]
        cands, flats, exps = [], [], []
        for a, nb, rows in ((0, 16, 16), (1, 8, 8), (2, 5, 8), (3, 4, 8)):
            r_io = r16 if rows == 16 else r8
            val = sv0[a:a + 1, :] + sv1[0:rows, :]
            cands.append(jnp.where(r_io < nb, val, ninf))
            flats.append(a * 16.0 + r_io)
            exps.append(si0[a:a + 1, :] * float(N_KEYS) + si1[0:rows, :])
        for b, lo, hi, rows in ((0, 4, 16, 16), (1, 4, 8, 8), (2, 4, 5, 8)):
            r_io = r16 if rows == 16 else r8
            val = sv0[0:rows, :] + sv1[b:b + 1, :]
            cands.append(jnp.where((r_io >= lo) & (r_io < hi), val, ninf))
            flats.append(r_io * 16.0 + b)
            exps.append(si0[0:rows, :] * float(N_KEYS) + si1[b:b + 1, :])
        cand = jnp.concatenate(cands, axis=0)
        flat = jnp.concatenate(flats, axis=0)
        expt = jnp.concatenate(exps, axis=0)
        cvs, exs = [], []
        for k in range(K):
            m = jnp.max(cand, axis=0, keepdims=True)
            fsel = jnp.min(jnp.where(cand == m, flat, 1e9), axis=0, keepdims=True)
            hit = flat == fsel
            exs.append(jnp.max(jnp.where(hit, expt, -1.0), axis=0, keepdims=True))
            cand = jnp.where(hit, ninf, cand)
            cvs.append(m)
        cv = jnp.concatenate(cvs, axis=0)
        ex = jnp.concatenate(exs, axis=0)
        ev = jnp.exp(cv - jnp.max(cv, axis=0, keepdims=True))
        gates = ev / jnp.sum(ev, axis=0, keepdims=True)
        e_ref[0, h * K:(h + 1) * K, :] = ex.astype(jnp.int32)
        g_ref[0, h * K:(h + 1) * K, :] = gates


def peer_route(qp, subkeys, u, v):
    T = qp.shape[0]
    tt = 128
    P = PEER_HEADS * PEER_TOPK
    nt = T // tt
    E, D = u.shape
    slab = E // nt
    assert slab * nt == E and slab % 16 == 0
    tab_spec = pl.BlockSpec((slab, D), lambda i: (i, 0))
    return pl.pallas_call(
        _peer_route_kernel,
        grid=(nt,),
        in_specs=[pl.BlockSpec((tt, qp.shape[1]), lambda i: (i, 0)),
                  pl.BlockSpec(subkeys.shape, lambda i: (0, 0, 0)),
                  tab_spec, tab_spec],
        out_specs=[pl.BlockSpec((1, P, tt), lambda i: (i, 0, 0))] * 2 + [tab_spec, tab_spec],
        out_shape=[jax.ShapeDtypeStruct((nt, P, tt), jnp.int32),
                   jax.ShapeDtypeStruct((nt, P, tt), F32),
                   jax.ShapeDtypeStruct((E, D), BF16), jax.ShapeDtypeStruct((E, D), BF16)],
        scratch_shapes=[pltpu.VMEM((2, PEER_TOPK, tt), F32), pltpu.VMEM((2, PEER_TOPK, tt), F32)],
        compiler_params=_params(("parallel",)),
    )(qp, subkeys, u, v)


def _peer_gbuild_kernel(e_ref, g_ref, o_ref, i1_ref, i2_ref, gt_ref):
    tt = e_ref.shape[2]
    e = e_ref[0].T
    i1_ref[...] = e >> 7
    i2_ref[...] = e & (N_KEYS - 1)
    gt_ref[...] = g_ref[0].T
    P = e.shape[1]
    k_io = lax.broadcasted_iota(jnp.int32, (N_KEYS, P), 0)
    group = 64

    def body(tg, carry):
        base = pl.multiple_of(tg * group, group)
        r1s = i1_ref[pl.ds(base, group), :]
        r2s = i2_ref[pl.ds(base, group), :]
        rgs = gt_ref[pl.ds(base, group), :]
        for u in range(group):
            lhs = jnp.where(k_io == r1s[u:u + 1, :], rgs[u:u + 1, :], 0.0).astype(BF16)
            rhs = jnp.where(k_io == r2s[u:u + 1, :], 1.0, 0.0).astype(BF16)
            g_t = _dot_nt(lhs, rhs).astype(BF16).astype(F32)
            lo = pltpu.bitcast(g_t[:HALF_KEYS], jnp.uint32) >> 16
            hi = pltpu.bitcast(g_t[HALF_KEYS:], jnp.uint32)
            words = hi | lo
            for a in range(HALF_KEYS // PACK_ROWS):
                o_ref[a, base + u] = words[a * PACK_ROWS:(a + 1) * PACK_ROWS, :]
        return carry

    lax.fori_loop(0, tt // group, body, 0)


def peer_gbuild(ex, gates):
    nt, P, tt = ex.shape
    T = nt * tt
    n_a = HALF_KEYS // PACK_ROWS
    return pl.pallas_call(
        _peer_gbuild_kernel,
        grid=(nt,),
        in_specs=[pl.BlockSpec((1, P, tt), lambda i: (i, 0, 0))] * 2,
        out_specs=pl.BlockSpec((n_a, tt, PACK_ROWS, N_KEYS), lambda i: (0, i, 0, 0)),
        out_shape=jax.ShapeDtypeStruct((n_a, T, PACK_ROWS, N_KEYS), jnp.uint32),
        scratch_shapes=[pltpu.VMEM((tt, P), jnp.int32), pltpu.VMEM((tt, P), jnp.int32),
                        pltpu.VMEM((tt, P), F32)],
        compiler_params=_params(("parallel",)),
    )(ex, gates)


def _peer_dense_kernel(h_ref, nw_ref, fw_ref, u_ref, v_ref, gm_ref, o_ref, hn_ref, acc_ref):
    j = pl.program_id(1)
    tt = h_ref.shape[0]

    @pl.when(j == 0)
    def _():
        x = h_ref[...]
        y = x * lax.rsqrt(jnp.mean(x * x, axis=-1, keepdims=True) + EPS)
        hn_ref[...] = (y * nw_ref[...]).astype(BF16)
        acc_ref[...] = jnp.zeros(acc_ref.shape, F32)

    a = _dot_nt(hn_ref[...], u_ref[...])
    shift = (16 * (1 - j % 2)).astype(jnp.uint32)
    gm = jnp.concatenate(
        [pltpu.bitcast((gm_ref[pl.ds(k, tt, stride=PACK_ROWS), :] << shift) & jnp.uint32(0xFFFF0000), F32)
         for k in range(PACK_ROWS)], axis=1)
    w = (gm * jax.nn.gelu(a)).astype(BF16)
    acc_ref[...] += _dot(w, v_ref[...])

    @pl.when(j == pl.num_programs(1) - 1)
    def _():
        x = h_ref[...] + acc_ref[...]
        y = x * lax.rsqrt(jnp.mean(x * x, axis=-1, keepdims=True) + EPS)
        o_ref[...] = y * fw_ref[...]


def peer_dense(h, ffn_nw, final_nw, u, v, gm_words, tt):
    T, D = h.shape
    E = u.shape[0]
    ec = PACK_ROWS * N_KEYS
    n_a = gm_words.shape[0]
    assert E == 2 * n_a * ec
    gm2 = gm_words.reshape(n_a, T * PACK_ROWS, N_KEYS)
    tab_spec = pl.BlockSpec((ec, D), lambda i, j: ((j % 2) * n_a + j // 2, 0))
    return pl.pallas_call(
        _peer_dense_kernel,
        grid=(T // tt, E // ec),
        in_specs=[pl.BlockSpec((tt, D), lambda i, j: (i, 0)),
                  pl.BlockSpec((1, D), lambda i, j: (0, 0)),
                  pl.BlockSpec((1, D), lambda i, j: (0, 0)),
                  tab_spec, tab_spec,
                  pl.BlockSpec((None, tt * PACK_ROWS, N_KEYS), lambda i, j: (j // 2, i, 0))],
        out_specs=pl.BlockSpec((tt, D), lambda i, j: (i, 0)),
        out_shape=jax.ShapeDtypeStruct((T, D), F32),
        scratch_shapes=[pltpu.VMEM((tt, D), BF16), pltpu.VMEM((tt, D), F32)],
        compiler_params=_params(("parallel", "arbitrary")),
    )(h, ffn_nw.reshape(1, D), final_nw.reshape(1, D), u, v, gm2)


def _t5_bucket_np(n_dist):
    d = np.arange(n_dist)
    max_exact = NUM_BUCKETS // 2
    nf = np.maximum(d, 1).astype(np.float64)
    large = max_exact + (np.log(nf / max_exact) / math.log(MAX_DISTANCE / max_exact)
                         * (NUM_BUCKETS - max_exact)).astype(np.int64)
    large = np.minimum(large, NUM_BUCKETS - 1)
    return np.where(d < max_exact, d, large).astype(np.int32)


def _overlap_t_np(S):
    n_c = (S - L_CMP) // STRIDE_CMP + 1
    n_sel = S // L_SEL
    pos = np.arange(n_c)[:, None] * STRIDE_CMP + np.arange(L_CMP)[None, :]
    m = np.zeros((n_c + 1, n_sel), np.float32)
    np.add.at(m, (np.repeat(np.arange(n_c), L_CMP), (pos // L_SEL).reshape(-1)), 1.0 / L_CMP)
    return np.ascontiguousarray(m.T)


def _block_onehot_padded_np(S):
    assert S // L_SEL < HEAD_DIM
    e2 = np.zeros((WINDOW + S, HEAD_DIM), np.float32)
    e2[WINDOW + np.arange(S), np.arange(S) // L_SEL] = 1.0
    e2[:WINDOW, HEAD_DIM - 1] = SEL_OFF
    return e2


def nsa_conv_mix(xt, B, S, attn_norm_w, w_in, w_cmp_k, w_cmp_v, cmp_pos, conv_w, conv_b,
                 attn_gnw, conv_gnw, rel_bias, tm=1024, tm_conv=512):
    T, D = xt.shape
    dq = N_Q_HEADS * HEAD_DIM
    dkv = N_KV_HEADS * HEAD_DIM
    n_attn = dq + 6 * dkv
    n_gate = 3 * N_Q_HEADS
    dc = (w_in.shape[1] - n_attn - n_gate) // 3
    w_t = w_in.T
    w_attn = cast_rows(w_t, 0, n_attn)
    w_conv = cast_rows(w_t, n_attn + n_gate, 3 * dc)
    w_gate = jnp.pad(w_t[n_attn:n_attn + n_gate], ((0, HEAD_DIM - n_gate), (0, 0))).astype(BF16)

    qkv = norm_matmul(xt, attn_norm_w, w_attn, BF16, tm, n_attn // 2, w_is_nk=True)
    gl = norm_matmul(xt, attn_norm_w, w_gate, F32, tm, HEAD_DIM, w_is_nk=True)
    gc = norm_matmul(xt, attn_norm_w, w_conv, F32, tm, dc, w_is_nk=True)

    n16 = S // STRIDE_CMP
    k16 = qkv[:, dq:dq + dkv].reshape(B * n16, STRIDE_CMP * dkv)
    v16 = qkv[:, dq + dkv:dq + 2 * dkv].reshape(B * n16, STRIDE_CMP * dkv)

    def wbig(w, lo):
        wl = w[lo:lo + STRIDE_CMP]
        eye = jnp.eye(N_KV_HEADS, dtype=w.dtype)
        return jnp.einsum('lde,hg->lhdge', wl, eye).reshape(STRIDE_CMP * dkv, dkv).astype(BF16)

    def posrow(lo):
        p = cmp_pos[lo:lo + STRIDE_CMP]
        return jnp.broadcast_to(p[:, None, :], (STRIDE_CMP, N_KV_HEADS, HEAD_DIM)).reshape(1, STRIDE_CMP * dkv)

    kc, vc = compress(k16, v16, posrow(0), posrow(STRIDE_CMP),
                      wbig(w_cmp_k, 0), wbig(w_cmp_k, STRIDE_CMP),
                      wbig(w_cmp_v, 0), wbig(w_cmp_v, STRIDE_CMP), B)

    tab = rel_bias[_t5_bucket_np(HEAD_DIM)].T
    ovt = jnp.asarray(_overlap_t_np(S), BF16)
    e2p = jnp.asarray(_block_onehot_padded_np(S), BF16)

    ocmp, selb = cmp_select(qkv, kc, vc, tab, ovt, B, S)
    def keys_padded(col):
        k = qkv[:, col:col + dkv].reshape(B, S, dkv)
        return jnp.pad(k, ((0, 0), (WINDOW, 0), (0, 0))).reshape(B * (WINDOW + S), dkv)

    def values_t_padded(col):
        v = qkv[:, col:col + dkv].reshape(B, S, dkv).transpose(0, 2, 1)
        return jnp.pad(v, ((0, 0), (0, 0), (WINDOW, 0))).reshape(B * dkv, WINDOW + S)

    attn = sel_win(qkv, selb, keys_padded(dq + 2 * dkv), values_t_padded(dq + 3 * dkv),
                   keys_padded(dq + 4 * dkv), values_t_padded(dq + 5 * dkv), e2p, tab, ocmp, gl,
                   attn_gnw.reshape(1, dq), B, S)
    cw8 = jnp.pad(conv_w, ((0, 8 - conv_w.shape[0]), (0, 0)))
    conv = conv_mixer(gc, cw8, conv_b.reshape(1, dc), conv_gnw.reshape(1, dc), S, tm_conv)
    return attn, conv


def peer_block(h, ffn_nw, final_nw, peer_wq, peer_subkeys, peer_u, peer_v, tm=1024, tt=512):
    T, D = h.shape
    qp = norm_matmul(h, ffn_nw, peer_wq.astype(BF16), BF16, tm, 1024)
    sk = peer_subkeys.reshape(PEER_HEADS * 2, N_KEYS, peer_subkeys.shape[-1]).astype(BF16)
    ex, gates, u_bf, v_bf = peer_route(qp, sk, peer_u, peer_v)
    gm = peer_gbuild(ex, gates)
    return peer_dense(h, ffn_nw, final_nw, u_bf, v_bf, gm, tt)


def kernel(x, attn_norm_w, w_in, w_cmp_k, w_cmp_v, cmp_pos, conv_w, conv_b, attn_group_norm_w,
           conv_group_norm_w, w_out, rel_bias, ffn_norm_w, peer_wq, peer_subkeys, peer_u, peer_v,
           final_norm_w):
    B, S, D = x.shape
    T = B * S
    xt = x.reshape(T, D)
    attn, conv = nsa_conv_mix(xt, B, S, attn_norm_w[0], w_in[0], w_cmp_k[0], w_cmp_v[0], cmp_pos[0],
                              conv_w[0], conv_b[0], attn_group_norm_w[0], conv_group_norm_w[0], rel_bias)
    da = attn.shape[1]
    wo = w_out[0].astype(BF16)
    h = out_proj(attn, conv, wo[:da], wo[da:], xt, 1024, 1024)
    out = peer_block(h, ffn_norm_w[0], final_norm_w, peer_wq[0], peer_subkeys[0], peer_u[0], peer_v[0])
    return out.reshape(B, S, D)
```

```python
import functools
import math

import jax
import jax.numpy as jnp
import numpy as np
from jax import lax
from jax.experimental import pallas as pl
from jax.experimental.pallas import tpu as pltpu

F32 = jnp.float32
BF16 = jnp.bfloat16

HEAD_DIM = 128
N_KV_HEADS = 2
GQA = 4
N_Q_HEADS = N_KV_HEADS * GQA
L_CMP = 32
STRIDE_CMP = 16
L_SEL = 64
N_SEL = 16
WINDOW = 512
Q_BLOCK = 128
FORCED_SCORE = float(GQA + 1)
NUM_BUCKETS = 32
MAX_DISTANCE = 128
PEER_HEADS = 8
N_KEYS = 128
PEER_TOPK = 16
EPS = 1e-6
NEG_BIG = -1e30
SEL_OFF = -float(2 ** 30)
LOG2E = math.log2(math.e)
HALF_KEYS = N_KEYS // 2
PACK_ROWS = 8
N_FIXED_A = 4
VMEM_LIMIT = 56 * 1024 * 1024


def _dot(a, b):
    return jnp.dot(a, b, preferred_element_type=F32)


def _dot_nt(a, b):
    return lax.dot_general(a, b, (((1,), (1,)), ((), ())), preferred_element_type=F32)


def _params(sem, vmem=VMEM_LIMIT):
    return pltpu.CompilerParams(dimension_semantics=sem, vmem_limit_bytes=vmem)


def _norm_matmul_kernel(x_ref, nw_ref, w_ref, o_ref, xn_ref, *, w_is_nk):
    @pl.when(pl.program_id(1) == 0)
    def _():
        x = x_ref[...]
        y = x * lax.rsqrt(jnp.mean(x * x, axis=-1, keepdims=True) + EPS)
        xn_ref[...] = (y * nw_ref[...]).astype(BF16)

    dot = _dot_nt if w_is_nk else _dot
    o_ref[...] = dot(xn_ref[...], w_ref[...]).astype(o_ref.dtype)


def norm_matmul(x, norm_w, w, out_dtype, tm, tn, w_is_nk=False):
    T, D = x.shape
    N = w.shape[0] if w_is_nk else w.shape[1]
    w_spec = pl.BlockSpec((tn, D), lambda i, j: (j, 0)) if w_is_nk else pl.BlockSpec((D, tn), lambda i, j: (0, j))
    return pl.pallas_call(
        functools.partial(_norm_matmul_kernel, w_is_nk=w_is_nk),
        grid=(T // tm, N // tn),
        in_specs=[pl.BlockSpec((tm, D), lambda i, j: (i, 0)),
                  pl.BlockSpec((1, D), lambda i, j: (0, 0)),
                  w_spec],
        out_specs=pl.BlockSpec((tm, tn), lambda i, j: (i, j)),
        out_shape=jax.ShapeDtypeStruct((T, N), out_dtype),
        scratch_shapes=[pltpu.VMEM((tm, D), BF16)],
        compiler_params=_params(("parallel", "arbitrary")),
    )(x, norm_w.reshape(1, D), w)


def _cast_rows_kernel(w_ref, o_ref):
    o_ref[...] = w_ref[...].astype(o_ref.dtype)


def cast_rows(w, row0, n_rows, tr=256):
    K = w.shape[1]
    assert row0 % 8 == 0 and n_rows % tr == 0
    return pl.pallas_call(
        _cast_rows_kernel,
        grid=(n_rows // tr,),
        in_specs=[pl.BlockSpec((pl.Element(tr), pl.Element(K)), lambda i: (pl.multiple_of(row0 + i * tr, 8), 0))],
        out_specs=pl.BlockSpec((tr, K), lambda i: (i, 0)),
        out_shape=jax.ShapeDtypeStruct((n_rows, K), BF16),
        compiler_params=_params(("parallel",)),
    )(w)


def _compress_kernel(k_ref, v_ref, pa_ref, pb_ref, wka_ref, wkb_ref, wva_ref, wvb_ref, kc_ref, vc_ref):
    def one(x_ref, wa_ref, wb_ref, o_ref):
        x = x_ref[...].astype(F32)
        a = _dot((x + pa_ref[...]).astype(BF16), wa_ref[...])
        b = _dot((x + pb_ref[...]).astype(BF16), wb_ref[...])
        n = b.shape[0]
        o_ref[...] = (a + pltpu.roll(b, n - 1, 0)).astype(o_ref.dtype)

    one(k_ref, wka_ref, wkb_ref, kc_ref)
    one(v_ref, wva_ref, wvb_ref, vc_ref)


def compress(k16, v16, pos_a, pos_b, wka, wkb, wva, wvb, B):
    R, C = k16.shape
    nb = R // B
    dkv = wka.shape[1]
    full = lambda shp: pl.BlockSpec(shp, lambda b: (0, 0))
    return pl.pallas_call(
        _compress_kernel,
        grid=(B,),
        in_specs=[pl.BlockSpec((nb, C), lambda b: (b, 0)),
                  pl.BlockSpec((nb, C), lambda b: (b, 0)),
                  full((1, C)), full((1, C)),
                  full((C, dkv)), full((C, dkv)), full((C, dkv)), full((C, dkv))],
        out_specs=[pl.BlockSpec((nb, dkv), lambda b: (b, 0))] * 2,
        out_shape=[jax.ShapeDtypeStruct((R, dkv), BF16)] * 2,
        compiler_params=_params(("parallel",)),
    )(k16, v16, pos_a, pos_b, wka, wkb, wva, wvb)


def _bias_from_dist(tab_row, dist):
    idx = jnp.clip(dist, 0, 127)
    w = tab_row.shape[1]
    tab = jnp.broadcast_to(tab_row, (idx.shape[0], w))
    parts = [jnp.take_along_axis(tab, idx[:, k:k + w], axis=1) for k in range(0, idx.shape[1], w)]
    return parts[0] if len(parts) == 1 else jnp.concatenate(parts, axis=1)


def _cmp_select_kernel(q_ref, kc_ref, vc_ref, tab_ref, ovt_ref, ocmp_ref, selb_ref, *, n_qb, scale):
    c = pl.program_id(0) % n_qb
    tq = q_ref.shape[0]
    n_c = kc_ref.shape[0]
    n_sel = ovt_ref.shape[0]
    t0 = c * tq
    t_col = t0 + lax.broadcasted_iota(jnp.int32, (tq, n_c), 0)
    n_row = lax.broadcasted_iota(jnp.int32, (tq, n_c), 1)
    dist = t_col - (n_row * STRIDE_CMP + (L_CMP - 1))
    valid = dist >= 0

    j_io = lax.broadcasted_iota(jnp.int32, (n_sel, tq), 0)
    t_io = t0 + lax.broadcasted_iota(jnp.int32, (n_sel, tq), 1)
    blk_t = t_io // L_SEL
    forced = (j_io == 0) | (j_io == blk_t) | (j_io == blk_t - 1)
    causal_blk = j_io * L_SEL <= t_io

    for h in range(N_KV_HEADS):
        kc = kc_ref[:, h * HEAD_DIM:(h + 1) * HEAD_DIM]
        vc = vc_ref[:, h * HEAD_DIM:(h + 1) * HEAD_DIM]
        psum = jnp.zeros((tq, n_c), F32)
        for g in range(GQA):
            hd = h * GQA + g
            qh = q_ref[:, hd * HEAD_DIM:(hd + 1) * HEAD_DIM]
            bias = _bias_from_dist(tab_ref[hd:hd + 1, :], dist)
            s = _dot_nt(qh, kc) * scale + bias
            s = jnp.where(valid, s, NEG_BIG)
            m = jnp.max(s, axis=-1, keepdims=True)
            e = jnp.where(valid, jnp.exp(s - m), 0.0)
            d = jnp.sum(e, axis=-1, keepdims=True)
            p = e / jnp.where(d > 0, d, 1.0)
            ocmp_ref[:, hd * HEAD_DIM:(hd + 1) * HEAD_DIM] = _dot(p.astype(BF16), vc)
            psum = psum + p
        p_hi = psum.astype(BF16)
        p_lo = (psum - p_hi.astype(F32)).astype(BF16)
        ovt = ovt_ref[...]
        imp = _dot_nt(ovt, p_hi) + _dot_nt(ovt, p_lo)
        score = jnp.where(forced, FORCED_SCORE, jnp.where(causal_blk, imp, -1.0))
        rank = jnp.zeros((n_sel, tq), F32)
        for jp in range(n_sel):
            row = score[jp:jp + 1, :]
            rank = rank + jnp.where(j_io > jp, jnp.where(row >= score, 1.0, 0.0),
                                    jnp.where(row > score, 1.0, 0.0))
        selb = jnp.where(rank < float(N_SEL), 0.0, SEL_OFF)
        if n_sel < HEAD_DIM:
            selb = jnp.concatenate([selb, jnp.zeros((HEAD_DIM - n_sel, tq), F32)], axis=0)
        selb_ref[:, h * HEAD_DIM:(h + 1) * HEAD_DIM] = selb.T.astype(BF16)


def cmp_select(qkv, kc, vc, tab, ovt, B, S, tq=4 * Q_BLOCK):
    T = qkv.shape[0]
    n_qb = S // tq
    n_c = kc.shape[0] // B
    dq = N_Q_HEADS * HEAD_DIM
    dkv = N_KV_HEADS * HEAD_DIM
    kern = functools.partial(_cmp_select_kernel, n_qb=n_qb, scale=HEAD_DIM ** -0.5)
    return pl.pallas_call(
        kern,
        grid=(T // tq,),
        in_specs=[pl.BlockSpec((tq, dq), lambda i: (i, 0)),
                  pl.BlockSpec((n_c, dkv), lambda i: (i // n_qb, 0)),
                  pl.BlockSpec((n_c, dkv), lambda i: (i // n_qb, 0)),
                  pl.BlockSpec(tab.shape, lambda i: (0, 0)),
                  pl.BlockSpec(ovt.shape, lambda i: (0, 0))],
        out_specs=[pl.BlockSpec((tq, dq), lambda i: (i, 0)),
                   pl.BlockSpec((tq, dkv), lambda i: (i, 0))],
        out_shape=[jax.ShapeDtypeStruct((T, dq), F32),
                   jax.ShapeDtypeStruct((T, dkv), BF16)],
        compiler_params=_params(("parallel",)),
    )(qkv, kc, vc, tab, ovt)


def _flash_chunk(k, vt, q, bias, m_ref, l_ref, acc_ref, scale):
    s = _dot_nt(k, q) * scale + bias
    m_old = m_ref[...]
    m_new = jnp.maximum(m_old, jnp.max(s, axis=0, keepdims=True))
    alpha = jnp.exp2(m_old - m_new)
    p = jnp.exp2(s - m_new)
    l_ref[...] = alpha * l_ref[...] + jnp.sum(p, axis=0, keepdims=True)
    acc_ref[...] = alpha * acc_ref[...] + _dot(vt, p.astype(BF16))
    m_ref[...] = m_new


def _sel_win_kernel(q_ref, selb_ref, ks_ref, vst_ref, kw_ref, vwt_ref, e2_ref, tab_ref, ocmp_ref,
                    gl_ref, gnw_ref, o_ref, m_ref, l_ref, acc_ref, *, n_qb, scale):
    c = pl.program_id(0) % n_qb
    tq = q_ref.shape[0]
    rows = GQA * tq
    n_back = WINDOW // tq
    near = WINDOW + tq
    j_io = lax.broadcasted_iota(jnp.int32, (tq, tq), 0)
    i_io = lax.broadcasted_iota(jnp.int32, (tq, tq), 1)
    dij = i_io - j_io
    dij4 = jnp.concatenate([dij] * GQA, axis=1)
    causal = dij4 >= 0
    pad_col = jnp.where(lax.broadcasted_iota(jnp.int32, (rows, HEAD_DIM), 1) == HEAD_DIM - 1, 1.0, 0.0).astype(BF16)
    sig = jax.nn.sigmoid(gl_ref[...])
    near0 = pl.multiple_of(c * tq, tq)
    n_far = jnp.maximum(c - n_back, 0)
    n_full = n_far // n_back
    n_rem = n_far - n_full * n_back

    def hsl(h):
        return slice(h * HEAD_DIM, (h + 1) * HEAD_DIM)

    def far0(i):
        return pl.multiple_of(WINDOW + i * WINDOW, WINDOW)

    def reset():
        m_ref[...] = jnp.full(m_ref.shape, NEG_BIG, F32)
        l_ref[...] = jnp.zeros(l_ref.shape, F32)
        acc_ref[...] = jnp.zeros(acc_ref.shape, F32)

    def result():
        ot = acc_ref[...] / l_ref[...]
        return [ot[:, g * tq:(g + 1) * tq].T for g in range(GQA)]

    for h in range(N_KV_HEADS):
        q4 = jnp.concatenate([q_ref[:, (h * GQA + g) * HEAD_DIM:(h * GQA + g + 1) * HEAD_DIM]
                              for g in range(GQA)], axis=0)
        sb4 = jnp.concatenate([selb_ref[:, hsl(h)]] * GQA, axis=0)
        q_aug = jnp.concatenate([q4, sb4 + pad_col], axis=1)
        q_win = jnp.concatenate([q4, pad_col], axis=1)
        tabs = [tab_ref[h * GQA + g:h * GQA + g + 1, :] * LOG2E for g in range(GQA)]
        d0 = jnp.concatenate([_bias_from_dist(t, dij) for t in tabs], axis=1)
        d1 = jnp.concatenate([_bias_from_dist(t, dij + tq) for t in tabs], axis=1)
        far = jnp.concatenate([t[:, HEAD_DIM - 1:HEAD_DIM] + jnp.zeros((1, tq), F32) for t in tabs], axis=1)
        far_t = jnp.broadcast_to(far, (tq, rows))
        diag = jnp.where(causal, d0, NEG_BIG)
        bias_sel = jnp.concatenate([far_t] * (n_back - 1) + [d1, diag], axis=0)
        bias_win = jnp.concatenate([jnp.where(dij4 < 0, far_t, NEG_BIG)] + [far_t] * (n_back - 2) + [d1, diag],
                                   axis=0)

        reset()
        e2_near = e2_ref[pl.ds(near0, near), :]
        k_near = jnp.concatenate([ks_ref[pl.ds(near0, near), hsl(h)], e2_near], axis=1)
        _flash_chunk(k_near, vst_ref[hsl(h), pl.ds(near0, near)], q_aug, bias_sel, m_ref, l_ref, acc_ref, scale)

        def far_chunk(i, bias):
            r0 = far0(i)
            k = jnp.concatenate([ks_ref[pl.ds(r0, WINDOW), hsl(h)], e2_ref[pl.ds(r0, WINDOW), :]], axis=1)
            _flash_chunk(k, vst_ref[hsl(h), pl.ds(r0, WINDOW)], q_aug, bias, m_ref, l_ref, acc_ref, scale)

        def body(i, carry):
            far_chunk(i, far)
            return carry

        lax.fori_loop(0, n_full, body, 0)

        @pl.when(n_rem > 0)
        def _():
            if n_back == 2:
                r0 = far0(n_full)
                k = jnp.concatenate([ks_ref[pl.ds(r0, tq), hsl(h)], e2_ref[pl.ds(r0, tq), :]], axis=1)
                _flash_chunk(k, vst_ref[hsl(h), pl.ds(r0, tq)], q_aug, far, m_ref, l_ref, acc_ref, scale)
            else:
                live = lax.broadcasted_iota(jnp.int32, (WINDOW, rows), 0) < n_rem * tq
                far_chunk(n_full, jnp.where(live, jnp.broadcast_to(far, (WINDOW, rows)), NEG_BIG))

        o_sel = result()

        reset()
        kw_near = jnp.concatenate([kw_ref[pl.ds(near0, near), hsl(h)], e2_near], axis=1)
        _flash_chunk(kw_near, vwt_ref[hsl(h), pl.ds(near0, near)], q_win, bias_win, m_ref, l_ref, acc_ref, scale)
        o_win = result()

        for g in range(GQA):
            hd = h * GQA + g
            o = (sig[:, 3 * hd:3 * hd + 1] * ocmp_ref[:, hsl(hd)]
                 + sig[:, 3 * hd + 1:3 * hd + 2] * o_sel[g]
                 + sig[:, 3 * hd + 2:3 * hd + 3] * o_win[g])
            y = o * lax.rsqrt(jnp.mean(o * o, axis=-1, keepdims=True) + EPS)
            o_ref[:, hsl(hd)] = (y * gnw_ref[:, hsl(hd)]).astype(o_ref.dtype)


def sel_win(qkv, selb, ksp, vstp, kwp, vwtp, e2p, tab, ocmp, gl, gnw, B, S, tq=2 * Q_BLOCK):
    T = qkv.shape[0]
    n_qb = S // tq
    dq = N_Q_HEADS * HEAD_DIM
    dkv = N_KV_HEADS * HEAD_DIM
    SP = S + WINDOW
    kern = functools.partial(_sel_win_kernel, n_qb=n_qb, scale=HEAD_DIM ** -0.5 * LOG2E)
    k_spec = pl.BlockSpec((SP, dkv), lambda i: (i // n_qb, 0))
    vt_spec = pl.BlockSpec((dkv, SP), lambda i: (i // n_qb, 0))
    return pl.pallas_call(
        kern,
        grid=(T // tq,),
        in_specs=[pl.BlockSpec((tq, dq), lambda i: (i, 0)),
                  pl.BlockSpec((tq, dkv), lambda i: (i, 0)),
                  k_spec, vt_spec, k_spec, vt_spec,
                  pl.BlockSpec((SP, HEAD_DIM), lambda i: (0, 0)),
                  pl.BlockSpec(tab.shape, lambda i: (0, 0)),
                  pl.BlockSpec((tq, dq), lambda i: (i, 0)),
                  pl.BlockSpec((tq, HEAD_DIM), lambda i: (i, 0)),
                  pl.BlockSpec((1, dq), lambda i: (0, 0))],
        out_specs=pl.BlockSpec((tq, dq), lambda i: (i, 0)),
        out_shape=jax.ShapeDtypeStruct((T, dq), BF16),
        scratch_shapes=[pltpu.VMEM((1, GQA * tq), F32), pltpu.VMEM((1, GQA * tq), F32),
                        pltpu.VMEM((HEAD_DIM, GQA * tq), F32)],
        compiler_params=_params(("parallel",)),
    )(qkv, selb, ksp, vstp, kwp, vwtp, e2p, tab, ocmp, gl, gnw)


def _conv_kernel(b_ref, c_ref, h_ref, cp_ref, hp_ref, cw_ref, cb_ref, gnw_ref, o_ref, u_ref, *, tiles_per_seq):
    tm = b_ref.shape[0]
    first = (pl.program_id(0) % tiles_per_seq) == 0
    u_prev = cp_ref[...] * hp_ref[...]
    u_ref[0:8, :] = jnp.where(first, 0.0, u_prev)
    u = c_ref[...] * h_ref[...]
    u_ref[8:8 + tm, :] = u
    y = (cw_ref[0:1, :] * u_ref[6:6 + tm, :] + cw_ref[1:2, :] * u_ref[7:7 + tm, :]
         + cw_ref[2:3, :] * u + cb_ref[...])
    o = b_ref[...] * y
    n_groups = o.shape[1] // HEAD_DIM
    for g in range(n_groups):
        sl = slice(g * HEAD_DIM, (g + 1) * HEAD_DIM)
        og = o[:, sl]
        yg = og * lax.rsqrt(jnp.mean(og * og, axis=-1, keepdims=True) + EPS)
        o_ref[:, sl] = (yg * gnw_ref[:, sl]).astype(o_ref.dtype)


def conv_mixer(gc, conv_w, conv_b, gnw, S, tm):
    T = gc.shape[0]
    dc = conv_w.shape[1]
    tps = S // tm
    kern = functools.partial(_conv_kernel, tiles_per_seq=tps)
    prev = lambda col: pl.BlockSpec((8, dc), lambda i, col=col: (jnp.maximum(i * (tm // 8) - 1, 0), col))
    cur = lambda col: pl.BlockSpec((tm, dc), lambda i, col=col: (i, col))
    return pl.pallas_call(
        kern,
        grid=(T // tm,),
        in_specs=[cur(0), cur(1), cur(2), prev(1), prev(2),
                  pl.BlockSpec((8, dc), lambda i: (0, 0)),
                  pl.BlockSpec((1, dc), lambda i: (0, 0)),
                  pl.BlockSpec((1, dc), lambda i: (0, 0))],
        out_specs=pl.BlockSpec((tm, dc), lambda i: (i, 0)),
        out_shape=jax.ShapeDtypeStruct((T, dc), BF16),
        scratch_shapes=[pltpu.VMEM((tm + 8, dc), F32)],
        compiler_params=_params(("parallel",)),
    )(gc, gc, gc, gc, gc, conv_w, conv_b, gnw)


def _out_proj_kernel(ma_ref, mc_ref, wa_ref, wc_ref, x_ref, o_ref):
    o_ref[...] = x_ref[...] + _dot(ma_ref[...], wa_ref[...]) + _dot(mc_ref[...], wc_ref[...])


def out_proj(ma, mc, wa, wc, x, tm, tn):
    T, da = ma.shape
    dc = mc.shape[1]
    D = x.shape[1]
    return pl.pallas_call(
        _out_proj_kernel,
        grid=(T // tm, D // tn),
        in_specs=[pl.BlockSpec((tm, da), lambda i, j: (i, 0)),
                  pl.BlockSpec((tm, dc), lambda i, j: (i, 0)),
                  pl.BlockSpec((da, tn), lambda i, j: (0, j)),
                  pl.BlockSpec((dc, tn), lambda i, j: (0, j)),
                  pl.BlockSpec((tm, tn), lambda i, j: (i, j))],
        out_specs=pl.BlockSpec((tm, tn), lambda i, j: (i, j)),
        out_shape=jax.ShapeDtypeStruct((T, D), F32),
        compiler_params=_params(("parallel", "arbitrary")),
    )(ma, mc, wa, wc, x)


def _peer_route_kernel(q_ref, sk_ref, u_ref, v_ref, e_ref, g_ref, ub_ref, vb_ref, sv_ref, si_ref):
    ub_ref[...] = u_ref[...].astype(BF16)
    vb_ref[...] = v_ref[...].astype(BF16)
    tt = q_ref.shape[0]
    K = PEER_TOPK
    sub = 8
    n_io = lax.broadcasted_iota(jnp.int32, (N_KEYS, tt), 0).astype(F32)
    r8 = lax.broadcasted_iota(jnp.int32, (sub, tt), 0).astype(F32)
    ninf = -jnp.inf
    lens = [K // (a + 1) for a in range(N_FIXED_A)]
    lens += [max(K // (b + 1) - N_FIXED_A, 0) for b in range(sub - N_FIXED_A)]
    assert sum(lens) == sum(K // (a + 1) for a in range(K)) and lens[-1] == 0
    fixed_a = r8 < float(N_FIXED_A)
    list_len = jnp.zeros((sub, tt), F32)
    for row, n in enumerate(lens):
        list_len = jnp.where(r8 == float(row), float(n), list_len)

    for h in range(PEER_HEADS):
        for c in range(2):
            col = (h * 2 + c) * HEAD_DIM
            s = _dot_nt(sk_ref[h * 2 + c], q_ref[:, col:col + HEAD_DIM])
            n_col = N_KEYS // sub
            vals = [s[j * sub:(j + 1) * sub, :] for j in range(n_col)]
            idxs = [n_io[j * sub:(j + 1) * sub, :] for j in range(n_col)]
            for rnd in range(n_col):
                for j in range(rnd % 2, n_col - 1, 2):
                    swap = vals[j + 1] > vals[j]
                    vals[j], vals[j + 1] = (jnp.where(swap, vals[j + 1], vals[j]),
                                            jnp.where(swap, vals[j], vals[j + 1]))
                    idxs[j], idxs[j + 1] = (jnp.where(swap, idxs[j + 1], idxs[j]),
                                            jnp.where(swap, idxs[j], idxs[j + 1]))
            for k in range(K):
                m = jnp.max(vals[0], axis=0, keepdims=True)
                idx = jnp.min(jnp.where(vals[0] == m, idxs[0], float(N_KEYS)), axis=0, keepdims=True)
                sv_ref[c, k:k + 1, :] = m
                si_ref[c, k:k + 1, :] = idx
                win = idxs[0] == idx
                for j in range(min(n_col, K) - 1 - k):
                    vals[j] = jnp.where(win, vals[j + 1], vals[j])
                    idxs[j] = jnp.where(win, idxs[j + 1], idxs[j])
        sv0, sv1 = sv_ref[0], sv_ref[1]
        si0, si1 = si_ref[0], si_ref[1]
        sv1_low = pltpu.roll(sv1[0:sub, :], N_FIXED_A, 0)
        sv0_top = sv0[0:sub, :]
        lv, lf = [], []
        for dep in range(K):
            a_dep = min(N_FIXED_A + dep, K - 1)
            val = (jnp.where(fixed_a, sv0_top, sv0[a_dep:a_dep + 1, :])
                   + jnp.where(fixed_a, sv1[dep:dep + 1, :], sv1_low))
            lv.append(jnp.where(list_len > dep, val, ninf))
            lf.append(jnp.where(fixed_a, r8 * float(K) + dep, (N_FIXED_A + dep) * float(K) + r8 - N_FIXED_A))
        cvs, fls = [], []
        for k in range(K):
            m = jnp.max(lv[0], axis=0, keepdims=True)
            fsel = jnp.min(jnp.where(lv[0] == m, lf[0], 1e9), axis=0, keepdims=True)
            win = lf[0] == fsel
            cvs.append(m)
            fls.append(fsel)
            for dep in range(K - 1 - k):
                lv[dep] = jnp.where(win, lv[dep + 1], lv[dep])
                lf[dep] = jnp.where(win, lf[dep + 1], lf[dep])
        cv = jnp.concatenate(cvs, axis=0)
        fl = jnp.concatenate(fls, axis=0)
        a_sel = jnp.floor(fl * (1.0 / K))
        b_sel = fl - a_sel * K
        i1 = jnp.zeros_like(fl)
        i2 = jnp.zeros_like(fl)
        for r in range(K):
            i1 = jnp.where(a_sel == r, si0[r:r + 1, :], i1)
            i2 = jnp.where(b_sel == r, si1[r:r + 1, :], i2)
        ex = i1 * float(N_KEYS) + i2
        ev = jnp.exp(cv - jnp.max(cv, axis=0, keepdims=True))
        gates = ev / jnp.sum(ev, axis=0, keepdims=True)
        e_ref[0, h * K:(h + 1) * K, :] = ex.astype(jnp.int32)
        g_ref[0, h * K:(h + 1) * K, :] = gates


def peer_route(qp, subkeys, u, v):
    T = qp.shape[0]
    tt = 128
    P = PEER_HEADS * PEER_TOPK
    nt = T // tt
    E, D = u.shape
    slab = E // nt
    assert slab * nt == E and slab % 16 == 0
    tab_spec = pl.BlockSpec((slab, D), lambda i: (i, 0))
    return pl.pallas_call(
        _peer_route_kernel,
        grid=(nt,),
        in_specs=[pl.BlockSpec((tt, qp.shape[1]), lambda i: (i, 0)),
                  pl.BlockSpec(subkeys.shape, lambda i: (0, 0, 0)),
                  tab_spec, tab_spec],
        out_specs=[pl.BlockSpec((1, P, tt), lambda i: (i, 0, 0))] * 2 + [tab_spec, tab_spec],
        out_shape=[jax.ShapeDtypeStruct((nt, P, tt), jnp.int32),
                   jax.ShapeDtypeStruct((nt, P, tt), F32),
                   jax.ShapeDtypeStruct((E, D), BF16), jax.ShapeDtypeStruct((E, D), BF16)],
        scratch_shapes=[pltpu.VMEM((2, PEER_TOPK, tt), F32), pltpu.VMEM((2, PEER_TOPK, tt), F32)],
        compiler_params=_params(("parallel",)),
    )(qp, subkeys, u, v)


def _peer_gbuild_kernel(e_ref, g_ref, o_ref, i1_ref, i2_ref, gt_ref):
    tt = e_ref.shape[2]
    e = e_ref[0].T
    i1_ref[...] = e >> 7
    i2_ref[...] = e & (N_KEYS - 1)
    gt_ref[...] = g_ref[0].T
    P = e.shape[1]
    k_io = lax.broadcasted_iota(jnp.int32, (N_KEYS, P), 0)
    group = 64

    def body(tg, carry):
        base = pl.multiple_of(tg * group, group)
        r1s = i1_ref[pl.ds(base, group), :]
        r2s = i2_ref[pl.ds(base, group), :]
        rgs = gt_ref[pl.ds(base, group), :]
        for u in range(group):
            lhs = jnp.where(k_io == r1s[u:u + 1, :], rgs[u:u + 1, :], 0.0).astype(BF16)
            rhs = jnp.where(k_io == r2s[u:u + 1, :], 1.0, 0.0).astype(BF16)
            g_t = _dot_nt(lhs, rhs).astype(BF16).astype(F32)
            lo = pltpu.bitcast(g_t[:HALF_KEYS], jnp.uint32) >> 16
            hi = pltpu.bitcast(g_t[HALF_KEYS:], jnp.uint32)
            words = hi | lo
            for a in range(HALF_KEYS // PACK_ROWS):
                o_ref[a, base + u] = words[a * PACK_ROWS:(a + 1) * PACK_ROWS, :]
        return carry

    lax.fori_loop(0, tt // group, body, 0)


def peer_gbuild(ex, gates):
    nt, P, tt = ex.shape
    T = nt * tt
    n_a = HALF_KEYS // PACK_ROWS
    return pl.pallas_call(
        _peer_gbuild_kernel,
        grid=(nt,),
        in_specs=[pl.BlockSpec((1, P, tt), lambda i: (i, 0, 0))] * 2,
        out_specs=pl.BlockSpec((n_a, tt, PACK_ROWS, N_KEYS), lambda i: (0, i, 0, 0)),
        out_shape=jax.ShapeDtypeStruct((n_a, T, PACK_ROWS, N_KEYS), jnp.uint32),
        scratch_shapes=[pltpu.VMEM((tt, P), jnp.int32), pltpu.VMEM((tt, P), jnp.int32),
                        pltpu.VMEM((tt, P), F32)],
        compiler_params=_params(("parallel",)),
    )(ex, gates)


def _peer_dense_kernel(h_ref, nw_ref, fw_ref, u_ref, v_ref, gm_ref, o_ref, hn_ref, acc_ref):
    j = pl.program_id(1)
    tt = h_ref.shape[0]

    @pl.when(j == 0)
    def _():
        x = h_ref[...]
        y = x * lax.rsqrt(jnp.mean(x * x, axis=-1, keepdims=True) + EPS)
        hn_ref[...] = (y * nw_ref[...]).astype(BF16)
        acc_ref[...] = jnp.zeros(acc_ref.shape, F32)

    a = _dot_nt(hn_ref[...], u_ref[...])
    shift = (16 * (1 - j % 2)).astype(jnp.uint32)
    gm = jnp.concatenate(
        [pltpu.bitcast((gm_ref[pl.ds(k, tt, stride=PACK_ROWS), :] << shift) & jnp.uint32(0xFFFF0000), F32)
         for k in range(PACK_ROWS)], axis=1)
    w = (gm * jax.nn.gelu(a)).astype(BF16)
    acc_ref[...] += _dot(w, v_ref[...])

    @pl.when(j == pl.num_programs(1) - 1)
    def _():
        x = h_ref[...] + acc_ref[...]
        y = x * lax.rsqrt(jnp.mean(x * x, axis=-1, keepdims=True) + EPS)
        o_ref[...] = y * fw_ref[...]


def peer_dense(h, ffn_nw, final_nw, u, v, gm_words, tt):
    T, D = h.shape
    E = u.shape[0]
    ec = PACK_ROWS * N_KEYS
    n_a = gm_words.shape[0]
    assert E == 2 * n_a * ec
    gm2 = gm_words.reshape(n_a, T * PACK_ROWS, N_KEYS)
    tab_spec = pl.BlockSpec((ec, D), lambda i, j: ((j % 2) * n_a + j // 2, 0))
    return pl.pallas_call(
        _peer_dense_kernel,
        grid=(T // tt, E // ec),
        in_specs=[pl.BlockSpec((tt, D), lambda i, j: (i, 0)),
                  pl.BlockSpec((1, D), lambda i, j: (0, 0)),
                  pl.BlockSpec((1, D), lambda i, j: (0, 0)),
                  tab_spec, tab_spec,
                  pl.BlockSpec((None, tt * PACK_ROWS, N_KEYS), lambda i, j: (j // 2, i, 0))],
        out_specs=pl.BlockSpec((tt, D), lambda i, j: (i, 0)),
        out_shape=jax.ShapeDtypeStruct((T, D), F32),
        scratch_shapes=[pltpu.VMEM((tt, D), BF16), pltpu.VMEM((tt, D), F32)],
        compiler_params=_params(("parallel", "arbitrary")),
    )(h, ffn_nw.reshape(1, D), final_nw.reshape(1, D), u, v, gm2)


def _t5_bucket_np(n_dist):
    d = np.arange(n_dist)
    max_exact = NUM_BUCKETS // 2
    nf = np.maximum(d, 1).astype(np.float64)
    large = max_exact + (np.log(nf / max_exact) / math.log(MAX_DISTANCE / max_exact)
                         * (NUM_BUCKETS - max_exact)).astype(np.int64)
    large = np.minimum(large, NUM_BUCKETS - 1)
    return np.where(d < max_exact, d, large).astype(np.int32)


def _overlap_t_np(S):
    n_c = (S - L_CMP) // STRIDE_CMP + 1
    n_sel = S // L_SEL
    pos = np.arange(n_c)[:, None] * STRIDE_CMP + np.arange(L_CMP)[None, :]
    m = np.zeros((n_c + 1, n_sel), np.float32)
    np.add.at(m, (np.repeat(np.arange(n_c), L_CMP), (pos // L_SEL).reshape(-1)), 1.0 / L_CMP)
    return np.ascontiguousarray(m.T)


def _block_onehot_padded_np(S):
    assert S // L_SEL < HEAD_DIM
    e2 = np.zeros((WINDOW + S, HEAD_DIM), np.float32)
    e2[WINDOW + np.arange(S), np.arange(S) // L_SEL] = 1.0
    e2[:WINDOW, HEAD_DIM - 1] = SEL_OFF
    return e2


def nsa_conv_mix(xt, B, S, attn_norm_w, w_in, w_cmp_k, w_cmp_v, cmp_pos, conv_w, conv_b,
                 attn_gnw, conv_gnw, rel_bias, tm=1024, tm_conv=512):
    T, D = xt.shape
    dq = N_Q_HEADS * HEAD_DIM
    dkv = N_KV_HEADS * HEAD_DIM
    n_attn = dq + 6 * dkv
    n_gate = 3 * N_Q_HEADS
    dc = (w_in.shape[1] - n_attn - n_gate) // 3
    w_t = w_in.T
    w_attn = cast_rows(w_t, 0, n_attn)
    w_conv = cast_rows(w_t, n_attn + n_gate, 3 * dc)
    w_gate = jnp.pad(w_t[n_attn:n_attn + n_gate], ((0, HEAD_DIM - n_gate), (0, 0))).astype(BF16)

    qkv = norm_matmul(xt, attn_norm_w, w_attn, BF16, tm, n_attn // 2, w_is_nk=True)
    gl = norm_matmul(xt, attn_norm_w, w_gate, F32, tm, HEAD_DIM, w_is_nk=True)
    gc = norm_matmul(xt, attn_norm_w, w_conv, F32, tm, dc, w_is_nk=True)

    n16 = S // STRIDE_CMP
    k16 = qkv[:, dq:dq + dkv].reshape(B * n16, STRIDE_CMP * dkv)
    v16 = qkv[:, dq + dkv:dq + 2 * dkv].reshape(B * n16, STRIDE_CMP * dkv)

    def wbig(w, lo):
        wl = w[lo:lo + STRIDE_CMP]
        eye = jnp.eye(N_KV_HEADS, dtype=w.dtype)
        return jnp.einsum('lde,hg->lhdge', wl, eye).reshape(STRIDE_CMP * dkv, dkv).astype(BF16)

    def posrow(lo):
        p = cmp_pos[lo:lo + STRIDE_CMP]
        return jnp.broadcast_to(p[:, None, :], (STRIDE_CMP, N_KV_HEADS, HEAD_DIM)).reshape(1, STRIDE_CMP * dkv)

    kc, vc = compress(k16, v16, posrow(0), posrow(STRIDE_CMP),
                      wbig(w_cmp_k, 0), wbig(w_cmp_k, STRIDE_CMP),
                      wbig(w_cmp_v, 0), wbig(w_cmp_v, STRIDE_CMP), B)

    tab = rel_bias[_t5_bucket_np(HEAD_DIM)].T
    ovt = jnp.asarray(_overlap_t_np(S), BF16)
    e2p = jnp.asarray(_block_onehot_padded_np(S), BF16)

    ocmp, selb = cmp_select(qkv, kc, vc, tab, ovt, B, S)
    def keys_padded(col):
        k = qkv[:, col:col + dkv].reshape(B, S, dkv)
        return jnp.pad(k, ((0, 0), (WINDOW, 0), (0, 0))).reshape(B * (WINDOW + S), dkv)

    def values_t_padded(col):
        v = qkv[:, col:col + dkv].reshape(B, S, dkv).transpose(0, 2, 1)
        return jnp.pad(v, ((0, 0), (0, 0), (WINDOW, 0))).reshape(B * dkv, WINDOW + S)

    attn = sel_win(qkv, selb, keys_padded(dq + 2 * dkv), values_t_padded(dq + 3 * dkv),
                   keys_padded(dq + 4 * dkv), values_t_padded(dq + 5 * dkv), e2p, tab, ocmp, gl,
                   attn_gnw.reshape(1, dq), B, S)
    cw8 = jnp.pad(conv_w, ((0, 8 - conv_w.shape[0]), (0, 0)))
    conv = conv_mixer(gc, cw8, conv_b.reshape(1, dc), conv_gnw.reshape(1, dc), S, tm_conv)
    return attn, conv


def peer_block(h, ffn_nw, final_nw, peer_wq, peer_subkeys, peer_u, peer_v, tm=1024, tt=512):
    T, D = h.shape
    qp = norm_matmul(h, ffn_nw, peer_wq.astype(BF16), BF16, tm, 1024)
    sk = peer_subkeys.reshape(PEER_HEADS * 2, N_KEYS, peer_subkeys.shape[-1]).astype(BF16)
    ex, gates, u_bf, v_bf = peer_route(qp, sk, peer_u, peer_v)
    gm = peer_gbuild(ex, gates)
    return peer_dense(h, ffn_nw, final_nw, u_bf, v_bf, gm, tt)


def kernel(x, attn_norm_w, w_in, w_cmp_k, w_cmp_v, cmp_pos, conv_w, conv_b, attn_group_norm_w,
           conv_group_norm_w, w_out, rel_bias, ffn_norm_w, peer_wq, peer_subkeys, peer_u, peer_v,
           final_norm_w):
    B, S, D = x.shape
    T = B * S
    xt = x.reshape(T, D)
    attn, conv = nsa_conv_mix(xt, B, S, attn_norm_w[0], w_in[0], w_cmp_k[0], w_cmp_v[0], cmp_pos[0],
                              conv_w[0], conv_b[0], attn_group_norm_w[0], conv_group_norm_w[0], rel_bias)
    da = attn.shape[1]
    wo = w_out[0].astype(BF16)
    h = out_proj(attn, conv, wo[:da], wo[da:], xt, 1024, 1024)
    out = peer_block(h, ffn_norm_w[0], final_norm_w, peer_wq[0], peer_subkeys[0], peer_u[0], peer_v[0])
    return out.reshape(B, S, D)
```

```python
import functools
import math

import jax
import jax.numpy as jnp
import numpy as np
from jax import lax
from jax.experimental import pallas as pl
from jax.experimental.pallas import tpu as pltpu

F32 = jnp.float32
BF16 = jnp.bfloat16

HEAD_DIM = 128
N_KV_HEADS = 2
GQA = 4
N_Q_HEADS = N_KV_HEADS * GQA
L_CMP = 32
STRIDE_CMP = 16
L_SEL = 64
N_SEL = 16
WINDOW = 512
Q_BLOCK = 128
FORCED_SCORE = float(GQA + 1)
NUM_BUCKETS = 32
MAX_DISTANCE = 128
PEER_HEADS = 8
N_KEYS = 128
PEER_TOPK = 16
EPS = 1e-6
NEG_BIG = -1e30
SEL_OFF = -float(2 ** 30)
LOG2E = math.log2(math.e)
HALF_KEYS = N_KEYS // 2
PACK_ROWS = 8
N_FIXED_A = 4
VMEM_LIMIT = 56 * 1024 * 1024


def _dot(a, b):
    return jnp.dot(a, b, preferred_element_type=F32)


def _dot_nt(a, b):
    return lax.dot_general(a, b, (((1,), (1,)), ((), ())), preferred_element_type=F32)


def _params(sem, vmem=VMEM_LIMIT):
    return pltpu.CompilerParams(dimension_semantics=sem, vmem_limit_bytes=vmem)


def _norm_matmul_kernel(x_ref, nw_ref, w_ref, o_ref, xn_ref, *, w_is_nk):
    @pl.when(pl.program_id(1) == 0)
    def _():
        x = x_ref[...]
        y = x * lax.rsqrt(jnp.mean(x * x, axis=-1, keepdims=True) + EPS)
        xn_ref[...] = (y * nw_ref[...]).astype(BF16)

    dot = _dot_nt if w_is_nk else _dot
    o_ref[...] = dot(xn_ref[...], w_ref[...]).astype(o_ref.dtype)


def norm_matmul(x, norm_w, w, out_dtype, tm, tn, w_is_nk=False):
    T, D = x.shape
    N = w.shape[0] if w_is_nk else w.shape[1]
    w_spec = pl.BlockSpec((tn, D), lambda i, j: (j, 0)) if w_is_nk else pl.BlockSpec((D, tn), lambda i, j: (0, j))
    return pl.pallas_call(
        functools.partial(_norm_matmul_kernel, w_is_nk=w_is_nk),
        grid=(T // tm, N // tn),
        in_specs=[pl.BlockSpec((tm, D), lambda i, j: (i, 0)),
                  pl.BlockSpec((1, D), lambda i, j: (0, 0)),
                  w_spec],
        out_specs=pl.BlockSpec((tm, tn), lambda i, j: (i, j)),
        out_shape=jax.ShapeDtypeStruct((T, N), out_dtype),
        scratch_shapes=[pltpu.VMEM((tm, D), BF16)],
        compiler_params=_params(("parallel", "arbitrary")),
    )(x, norm_w.reshape(1, D), w)


def _cast_rows_kernel(w_ref, o_ref):
    o_ref[...] = w_ref[...].astype(o_ref.dtype)


def cast_rows(w, row0, n_rows, tr=256):
    K = w.shape[1]
    assert row0 % 8 == 0 and n_rows % tr == 0
    return pl.pallas_call(
        _cast_rows_kernel,
        grid=(n_rows // tr,),
        in_specs=[pl.BlockSpec((pl.Element(tr), pl.Element(K)), lambda i: (pl.multiple_of(row0 + i * tr, 8), 0))],
        out_specs=pl.BlockSpec((tr, K), lambda i: (i, 0)),
        out_shape=jax.ShapeDtypeStruct((n_rows, K), BF16),
        compiler_params=_params(("parallel",)),
    )(w)


def _compress_kernel(k_ref, v_ref, pa_ref, pb_ref, wka_ref, wkb_ref, wva_ref, wvb_ref, kc_ref, vc_ref):
    def one(x_ref, wa_ref, wb_ref, o_ref):
        x = x_ref[...].astype(F32)
        a = _dot((x + pa_ref[...]).astype(BF16), wa_ref[...])
        b = _dot((x + pb_ref[...]).astype(BF16), wb_ref[...])
        n = b.shape[0]
        o_ref[...] = (a + pltpu.roll(b, n - 1, 0)).astype(o_ref.dtype)

    one(k_ref, wka_ref, wkb_ref, kc_ref)
    one(v_ref, wva_ref, wvb_ref, vc_ref)


def compress(k16, v16, pos_a, pos_b, wka, wkb, wva, wvb, B):
    R, C = k16.shape
    nb = R // B
    dkv = wka.shape[1]
    full = lambda shp: pl.BlockSpec(shp, lambda b: (0, 0))
    return pl.pallas_call(
        _compress_kernel,
        grid=(B,),
        in_specs=[pl.BlockSpec((nb, C), lambda b: (b, 0)),
                  pl.BlockSpec((nb, C), lambda b: (b, 0)),
                  full((1, C)), full((1, C)),
                  full((C, dkv)), full((C, dkv)), full((C, dkv)), full((C, dkv))],
        out_specs=[pl.BlockSpec((nb, dkv), lambda b: (b, 0))] * 2,
        out_shape=[jax.ShapeDtypeStruct((R, dkv), BF16)] * 2,
        compiler_params=_params(("parallel",)),
    )(k16, v16, pos_a, pos_b, wka, wkb, wva, wvb)


def _bias_from_dist(tab_row, dist):
    idx = jnp.clip(dist, 0, 127)
    w = tab_row.shape[1]
    tab = jnp.broadcast_to(tab_row, (idx.shape[0], w))
    parts = [jnp.take_along_axis(tab, idx[:, k:k + w], axis=1) for k in range(0, idx.shape[1], w)]
    return parts[0] if len(parts) == 1 else jnp.concatenate(parts, axis=1)


def _cmp_select_kernel(q_ref, kc_ref, vc_ref, tab_ref, ovt_ref, ocmp_ref, selb_ref, *, n_qb, scale):
    c = pl.program_id(0) % n_qb
    tq = q_ref.shape[0]
    n_c = kc_ref.shape[0]
    n_sel = ovt_ref.shape[0]
    t0 = c * tq
    t_col = t0 + lax.broadcasted_iota(jnp.int32, (tq, n_c), 0)
    n_row = lax.broadcasted_iota(jnp.int32, (tq, n_c), 1)
    dist = t_col - (n_row * STRIDE_CMP + (L_CMP - 1))
    valid = dist >= 0

    j_io = lax.broadcasted_iota(jnp.int32, (n_sel, tq), 0)
    t_io = t0 + lax.broadcasted_iota(jnp.int32, (n_sel, tq), 1)
    blk_t = t_io // L_SEL
    forced = (j_io == 0) | (j_io == blk_t) | (j_io == blk_t - 1)
    causal_blk = j_io * L_SEL <= t_io

    for h in range(N_KV_HEADS):
        kc = kc_ref[:, h * HEAD_DIM:(h + 1) * HEAD_DIM]
        vc = vc_ref[:, h * HEAD_DIM:(h + 1) * HEAD_DIM]
        psum = jnp.zeros((tq, n_c), F32)
        for g in range(GQA):
            hd = h * GQA + g
            qh = q_ref[:, hd * HEAD_DIM:(hd + 1) * HEAD_DIM]
            bias = _bias_from_dist(tab_ref[hd:hd + 1, :], dist)
            s = _dot_nt(qh, kc) * scale + bias
            s = jnp.where(valid, s, NEG_BIG)
            m = jnp.max(s, axis=-1, keepdims=True)
            e = jnp.where(valid, jnp.exp(s - m), 0.0)
            d = jnp.sum(e, axis=-1, keepdims=True)
            p = e / jnp.where(d > 0, d, 1.0)
            ocmp_ref[:, hd * HEAD_DIM:(hd + 1) * HEAD_DIM] = _dot(p.astype(BF16), vc)
            psum = psum + p
        p_hi = psum.astype(BF16)
        p_lo = (psum - p_hi.astype(F32)).astype(BF16)
        ovt = ovt_ref[...]
        imp = _dot_nt(ovt, p_hi) + _dot_nt(ovt, p_lo)
        score = jnp.where(forced, FORCED_SCORE, jnp.where(causal_blk, imp, -1.0))
        rank = jnp.zeros((n_sel, tq), F32)
        for jp in range(n_sel):
            row = score[jp:jp + 1, :]
            rank = rank + jnp.where(j_io > jp, jnp.where(row >= score, 1.0, 0.0),
                                    jnp.where(row > score, 1.0, 0.0))
        selb = jnp.where(rank < float(N_SEL), 0.0, SEL_OFF)
        if n_sel < HEAD_DIM:
            selb = jnp.concatenate([selb, jnp.zeros((HEAD_DIM - n_sel, tq), F32)], axis=0)
        selb_ref[:, h * HEAD_DIM:(h + 1) * HEAD_DIM] = selb.T.astype(BF16)


def cmp_select(qkv, kc, vc, tab, ovt, B, S, tq=4 * Q_BLOCK):
    T = qkv.shape[0]
    n_qb = S // tq
    n_c = kc.shape[0] // B
    dq = N_Q_HEADS * HEAD_DIM
    dkv = N_KV_HEADS * HEAD_DIM
    kern = functools.partial(_cmp_select_kernel, n_qb=n_qb, scale=HEAD_DIM ** -0.5)
    return pl.pallas_call(
        kern,
        grid=(T // tq,),
        in_specs=[pl.BlockSpec((tq, dq), lambda i: (i, 0)),
                  pl.BlockSpec((n_c, dkv), lambda i: (i // n_qb, 0)),
                  pl.BlockSpec((n_c, dkv), lambda i: (i // n_qb, 0)),
                  pl.BlockSpec(tab.shape, lambda i: (0, 0)),
                  pl.BlockSpec(ovt.shape, lambda i: (0, 0))],
        out_specs=[pl.BlockSpec((tq, dq), lambda i: (i, 0)),
                   pl.BlockSpec((tq, dkv), lambda i: (i, 0))],
        out_shape=[jax.ShapeDtypeStruct((T, dq), F32),
                   jax.ShapeDtypeStruct((T, dkv), BF16)],
        compiler_params=_params(("parallel",)),
    )(qkv, kc, vc, tab, ovt)


def _flash_update(s_raw, vt, bias, m_ref, l_ref, acc_ref, scale):
    s = s_raw * scale + bias
    m_old = m_ref[...]
    m_new = jnp.maximum(m_old, jnp.max(s, axis=0, keepdims=True))
    alpha = jnp.exp2(m_old - m_new)
    p = jnp.exp2(s - m_new)
    l_ref[...] = alpha * l_ref[...] + jnp.sum(p, axis=0, keepdims=True)
    acc_ref[...] = alpha * acc_ref[...] + _dot(vt, p.astype(BF16))
    m_ref[...] = m_new


def _flash_chunk(k, vt, q, bias, m_ref, l_ref, acc_ref, scale):
    _flash_update(_dot_nt(k, q), vt, bias, m_ref, l_ref, acc_ref, scale)


def _sel_win_kernel(q_ref, selb_ref, ks_ref, vst_ref, kw_ref, vwt_ref, e2_ref, tab_ref, ocmp_ref,
                    gl_ref, gnw_ref, o_ref, m_ref, l_ref, acc_ref, sa_ref, sb_ref, *, n_qb, scale):
    c = pl.program_id(0) % n_qb
    tq = q_ref.shape[0]
    rows = GQA * tq
    n_back = WINDOW // tq
    near = WINDOW + tq
    j_io = lax.broadcasted_iota(jnp.int32, (tq, tq), 0)
    i_io = lax.broadcasted_iota(jnp.int32, (tq, tq), 1)
    dij = i_io - j_io
    dij4 = jnp.concatenate([dij] * GQA, axis=1)
    causal = dij4 >= 0
    pad_col = jnp.where(lax.broadcasted_iota(jnp.int32, (rows, HEAD_DIM), 1) == HEAD_DIM - 1, 1.0, 0.0).astype(BF16)
    sig = jax.nn.sigmoid(gl_ref[...])
    near0 = pl.multiple_of(c * tq, tq)
    n_far = jnp.maximum(c - n_back, 0)
    n_full = n_far // n_back
    n_rem = n_far - n_full * n_back

    def hsl(h):
        return slice(h * HEAD_DIM, (h + 1) * HEAD_DIM)

    def far0(i):
        return pl.multiple_of(WINDOW + i * WINDOW, WINDOW)

    def reset():
        m_ref[...] = jnp.full(m_ref.shape, NEG_BIG, F32)
        l_ref[...] = jnp.zeros(l_ref.shape, F32)
        acc_ref[...] = jnp.zeros(acc_ref.shape, F32)

    def result():
        ot = acc_ref[...] / l_ref[...]
        return [ot[:, g * tq:(g + 1) * tq].T for g in range(GQA)]

    for h in range(N_KV_HEADS):
        q4 = jnp.concatenate([q_ref[:, (h * GQA + g) * HEAD_DIM:(h * GQA + g + 1) * HEAD_DIM]
                              for g in range(GQA)], axis=0)
        sb4 = jnp.concatenate([selb_ref[:, hsl(h)]] * GQA, axis=0)
        q_aug = jnp.concatenate([q4, sb4 + pad_col], axis=1)
        q_win = jnp.concatenate([q4, pad_col], axis=1)
        tabs = [tab_ref[h * GQA + g:h * GQA + g + 1, :] * LOG2E for g in range(GQA)]
        d0 = jnp.concatenate([_bias_from_dist(t, dij) for t in tabs], axis=1)
        d1 = jnp.concatenate([_bias_from_dist(t, dij + tq) for t in tabs], axis=1)
        far = jnp.concatenate([t[:, HEAD_DIM - 1:HEAD_DIM] + jnp.zeros((1, tq), F32) for t in tabs], axis=1)
        far_t = jnp.broadcast_to(far, (tq, rows))
        diag = jnp.where(causal, d0, NEG_BIG)
        bias_sel = jnp.concatenate([far_t] * (n_back - 1) + [d1, diag], axis=0)
        bias_win = jnp.concatenate([jnp.where(dij4 < 0, far_t, NEG_BIG)] + [far_t] * (n_back - 2) + [d1, diag],
                                   axis=0)

        reset()
        e2_near = e2_ref[pl.ds(near0, near), :]
        k_near = jnp.concatenate([ks_ref[pl.ds(near0, near), hsl(h)], e2_near], axis=1)
        _flash_chunk(k_near, vst_ref[hsl(h), pl.ds(near0, near)], q_aug, bias_sel, m_ref, l_ref, acc_ref, scale)

        def far_scores(i):
            r0 = far0(i)
            k = jnp.concatenate([ks_ref[pl.ds(r0, WINDOW), hsl(h)], e2_ref[pl.ds(r0, WINDOW), :]], axis=1)
            return _dot_nt(k, q_aug)

        def far_update(s_raw, i, bias):
            _flash_update(s_raw, vst_ref[hsl(h), pl.ds(far0(i), WINDOW)], bias, m_ref, l_ref, acc_ref, scale)

        sa_ref[...] = far_scores(0)

        def body(i, carry):
            sb_ref[...] = far_scores(2 * i + 1)
            far_update(sa_ref[...], 2 * i, far)
            sa_ref[...] = far_scores(2 * i + 2)
            far_update(sb_ref[...], 2 * i + 1, far)
            return carry

        lax.fori_loop(0, n_full // 2, body, 0)

        @pl.when(n_full % 2 == 1)
        def _():
            far_update(sa_ref[...], n_full - 1, far)

        @pl.when(n_rem > 0)
        def _():
            if n_back == 2:
                r0 = far0(n_full)
                k = jnp.concatenate([ks_ref[pl.ds(r0, tq), hsl(h)], e2_ref[pl.ds(r0, tq), :]], axis=1)
                _flash_chunk(k, vst_ref[hsl(h), pl.ds(r0, tq)], q_aug, far, m_ref, l_ref, acc_ref, scale)
            else:
                live = lax.broadcasted_iota(jnp.int32, (WINDOW, rows), 0) < n_rem * tq
                far_update(far_scores(n_full), n_full,
                           jnp.where(live, jnp.broadcast_to(far, (WINDOW, rows)), NEG_BIG))

        o_sel = result()

        reset()
        kw_near = jnp.concatenate([kw_ref[pl.ds(near0, near), hsl(h)], e2_near], axis=1)
        _flash_chunk(kw_near, vwt_ref[hsl(h), pl.ds(near0, near)], q_win, bias_win, m_ref, l_ref, acc_ref, scale)
        o_win = result()

        for g in range(GQA):
            hd = h * GQA + g
            o = (sig[:, 3 * hd:3 * hd + 1] * ocmp_ref[:, hsl(hd)]
                 + sig[:, 3 * hd + 1:3 * hd + 2] * o_sel[g]
                 + sig[:, 3 * hd + 2:3 * hd + 3] * o_win[g])
            y = o * lax.rsqrt(jnp.mean(o * o, axis=-1, keepdims=True) + EPS)
            o_ref[:, hsl(hd)] = (y * gnw_ref[:, hsl(hd)]).astype(o_ref.dtype)


def sel_win(qkv, selb, ksp, vstp, kwp, vwtp, e2p, tab, ocmp, gl, gnw, B, S, tq=2 * Q_BLOCK):
    T = qkv.shape[0]
    n_qb = S // tq
    dq = N_Q_HEADS * HEAD_DIM
    dkv = N_KV_HEADS * HEAD_DIM
    SP = S + WINDOW
    kern = functools.partial(_sel_win_kernel, n_qb=n_qb, scale=HEAD_DIM ** -0.5 * LOG2E)
    k_spec = pl.BlockSpec((SP, dkv), lambda i: (i // n_qb, 0))
    vt_spec = pl.BlockSpec((dkv, SP), lambda i: (i // n_qb, 0))
    return pl.pallas_call(
        kern,
        grid=(T // tq,),
        in_specs=[pl.BlockSpec((tq, dq), lambda i: (i, 0)),
                  pl.BlockSpec((tq, dkv), lambda i: (i, 0)),
                  k_spec, vt_spec, k_spec, vt_spec,
                  pl.BlockSpec((SP, HEAD_DIM), lambda i: (0, 0)),
                  pl.BlockSpec(tab.shape, lambda i: (0, 0)),
                  pl.BlockSpec((tq, dq), lambda i: (i, 0)),
                  pl.BlockSpec((tq, HEAD_DIM), lambda i: (i, 0)),
                  pl.BlockSpec((1, dq), lambda i: (0, 0))],
        out_specs=pl.BlockSpec((tq, dq), lambda i: (i, 0)),
        out_shape=jax.ShapeDtypeStruct((T, dq), BF16),
        scratch_shapes=[pltpu.VMEM((1, GQA * tq), F32), pltpu.VMEM((1, GQA * tq), F32),
                        pltpu.VMEM((HEAD_DIM, GQA * tq), F32),
                        pltpu.VMEM((WINDOW, GQA * tq), F32), pltpu.VMEM((WINDOW, GQA * tq), F32)],
        compiler_params=_params(("parallel",)),
    )(qkv, selb, ksp, vstp, kwp, vwtp, e2p, tab, ocmp, gl, gnw)


def _conv_kernel(b_ref, c_ref, h_ref, cp_ref, hp_ref, cw_ref, cb_ref, gnw_ref, o_ref, u_ref, *, tiles_per_seq):
    tm = b_ref.shape[0]
    first = (pl.program_id(0) % tiles_per_seq) == 0
    u_prev = cp_ref[...] * hp_ref[...]
    u_ref[0:8, :] = jnp.where(first, 0.0, u_prev)
    u = c_ref[...] * h_ref[...]
    u_ref[8:8 + tm, :] = u
    y = (cw_ref[0:1, :] * u_ref[6:6 + tm, :] + cw_ref[1:2, :] * u_ref[7:7 + tm, :]
         + cw_ref[2:3, :] * u + cb_ref[...])
    o = b_ref[...] * y
    n_groups = o.shape[1] // HEAD_DIM
    for g in range(n_groups):
        sl = slice(g * HEAD_DIM, (g + 1) * HEAD_DIM)
        og = o[:, sl]
        yg = og * lax.rsqrt(jnp.mean(og * og, axis=-1, keepdims=True) + EPS)
        o_ref[:, sl] = (yg * gnw_ref[:, sl]).astype(o_ref.dtype)


def conv_mixer(gc, conv_w, conv_b, gnw, S, tm):
    T = gc.shape[0]
    dc = conv_w.shape[1]
    tps = S // tm
    kern = functools.partial(_conv_kernel, tiles_per_seq=tps)
    prev = lambda col: pl.BlockSpec((8, dc), lambda i, col=col: (jnp.maximum(i * (tm // 8) - 1, 0), col))
    cur = lambda col: pl.BlockSpec((tm, dc), lambda i, col=col: (i, col))
    return pl.pallas_call(
        kern,
        grid=(T // tm,),
        in_specs=[cur(0), cur(1), cur(2), prev(1), prev(2),
                  pl.BlockSpec((8, dc), lambda i: (0, 0)),
                  pl.BlockSpec((1, dc), lambda i: (0, 0)),
                  pl.BlockSpec((1, dc), lambda i: (0, 0))],
        out_specs=pl.BlockSpec((tm, dc), lambda i: (i, 0)),
        out_shape=jax.ShapeDtypeStruct((T, dc), BF16),
        scratch_shapes=[pltpu.VMEM((tm + 8, dc), F32)],
        compiler_params=_params(("parallel",)),
    )(gc, gc, gc, gc, gc, conv_w, conv_b, gnw)


def _out_proj_kernel(ma_ref, mc_ref, wa_ref, wc_ref, x_ref, o_ref):
    o_ref[...] = x_ref[...] + _dot(ma_ref[...], wa_ref[...]) + _dot(mc_ref[...], wc_ref[...])


def out_proj(ma, mc, wa, wc, x, tm, tn):
    T, da = ma.shape
    dc = mc.shape[1]
    D = x.shape[1]
    return pl.pallas_call(
        _out_proj_kernel,
        grid=(T // tm, D // tn),
        in_specs=[pl.BlockSpec((tm, da), lambda i, j: (i, 0)),
                  pl.BlockSpec((tm, dc), lambda i, j: (i, 0)),
                  pl.BlockSpec((da, tn), lambda i, j: (0, j)),
                  pl.BlockSpec((dc, tn), lambda i, j: (0, j)),
                  pl.BlockSpec((tm, tn), lambda i, j: (i, j))],
        out_specs=pl.BlockSpec((tm, tn), lambda i, j: (i, j)),
        out_shape=jax.ShapeDtypeStruct((T, D), F32),
        compiler_params=_params(("parallel", "arbitrary")),
    )(ma, mc, wa, wc, x)


def _peer_route_kernel(q_ref, sk_ref, u_ref, v_ref, e_ref, g_ref, ub_ref, vb_ref, sv_ref, si_ref):
    ub_ref[...] = u_ref[...].astype(BF16)
    vb_ref[...] = v_ref[...].astype(BF16)
    tt = q_ref.shape[0]
    K = PEER_TOPK
    sub = 8
    n_io = lax.broadcasted_iota(jnp.int32, (N_KEYS, tt), 0).astype(F32)
    r8 = lax.broadcasted_iota(jnp.int32, (sub, tt), 0).astype(F32)
    ninf = -jnp.inf
    lens = [K // (a + 1) for a in range(N_FIXED_A)]
    lens += [max(K // (b + 1) - N_FIXED_A, 0) for b in range(sub - N_FIXED_A)]
    assert sum(lens) == sum(K // (a + 1) for a in range(K)) and lens[-1] == 0
    fixed_a = r8 < float(N_FIXED_A)
    list_len = jnp.zeros((sub, tt), F32)
    for row, n in enumerate(lens):
        list_len = jnp.where(r8 == float(row), float(n), list_len)

    for h in range(PEER_HEADS):
        for c in range(2):
            col = (h * 2 + c) * HEAD_DIM
            s = _dot_nt(sk_ref[h * 2 + c], q_ref[:, col:col + HEAD_DIM])
            n_col = N_KEYS // sub
            vals = [s[j * sub:(j + 1) * sub, :] for j in range(n_col)]
            idxs = [n_io[j * sub:(j + 1) * sub, :] for j in range(n_col)]
            for rnd in range(n_col):
                for j in range(rnd % 2, n_col - 1, 2):
                    swap = vals[j + 1] > vals[j]
                    vals[j], vals[j + 1] = (jnp.where(swap, vals[j + 1], vals[j]),
                                            jnp.where(swap, vals[j], vals[j + 1]))
                    idxs[j], idxs[j + 1] = (jnp.where(swap, idxs[j + 1], idxs[j]),
                                            jnp.where(swap, idxs[j], idxs[j + 1]))
            for k in range(K):
                m = jnp.max(vals[0], axis=0, keepdims=True)
                idx = jnp.min(jnp.where(vals[0] == m, idxs[0], float(N_KEYS)), axis=0, keepdims=True)
                sv_ref[c, k:k + 1, :] = m
                si_ref[c, k:k + 1, :] = idx
                win = idxs[0] == idx
                for j in range(min(n_col, K) - 1 - k):
                    vals[j] = jnp.where(win, vals[j + 1], vals[j])
                    idxs[j] = jnp.where(win, idxs[j + 1], idxs[j])
        sv0, sv1 = sv_ref[0], sv_ref[1]
        si0, si1 = si_ref[0], si_ref[1]
        sv1_low = pltpu.roll(sv1[0:sub, :], N_FIXED_A, 0)
        sv0_top = sv0[0:sub, :]
        lv, lf = [], []
        for dep in range(K):
            a_dep = min(N_FIXED_A + dep, K - 1)
            val = (jnp.where(fixed_a, sv0_top, sv0[a_dep:a_dep + 1, :])
                   + jnp.where(fixed_a, sv1[dep:dep + 1, :], sv1_low))
            lv.append(jnp.where(list_len > dep, val, ninf))
            lf.append(jnp.where(fixed_a, r8 * float(K) + dep, (N_FIXED_A + dep) * float(K) + r8 - N_FIXED_A))
        cvs, fls = [], []
        for k in range(K):
            m = jnp.max(lv[0], axis=0, keepdims=True)
            fsel = jnp.min(jnp.where(lv[0] == m, lf[0], 1e9), axis=0, keepdims=True)
            win = lf[0] == fsel
            cvs.append(m)
            fls.append(fsel)
            for dep in range(K - 1 - k):
                lv[dep] = jnp.where(win, lv[dep + 1], lv[dep])
                lf[dep] = jnp.where(win, lf[dep + 1], lf[dep])
        cv = jnp.concatenate(cvs, axis=0)
        fl = jnp.concatenate(fls, axis=0)
        a_sel = jnp.floor(fl * (1.0 / K))
        b_sel = fl - a_sel * K
        i1 = jnp.zeros_like(fl)
        i2 = jnp.zeros_like(fl)
        for r in range(K):
            i1 = jnp.where(a_sel == r, si0[r:r + 1, :], i1)
            i2 = jnp.where(b_sel == r, si1[r:r + 1, :], i2)
        ex = i1 * float(N_KEYS) + i2
        ev = jnp.exp(cv - jnp.max(cv, axis=0, keepdims=True))
        gates = ev / jnp.sum(ev, axis=0, keepdims=True)
        e_ref[0, h * K:(h + 1) * K, :] = ex.astype(jnp.int32)
        g_ref[0, h * K:(h + 1) * K, :] = gates


def peer_route(qp, subkeys, u, v):
    T = qp.shape[0]
    tt = 128
    P = PEER_HEADS * PEER_TOPK
    nt = T // tt
    E, D = u.shape
    slab = E // nt
    assert slab * nt == E and slab % 16 == 0
    tab_spec = pl.BlockSpec((slab, D), lambda i: (i, 0))
    return pl.pallas_call(
        _peer_route_kernel,
        grid=(nt,),
        in_specs=[pl.BlockSpec((tt, qp.shape[1]), lambda i: (i, 0)),
                  pl.BlockSpec(subkeys.shape, lambda i: (0, 0, 0)),
                  tab_spec, tab_spec],
        out_specs=[pl.BlockSpec((1, P, tt), lambda i: (i, 0, 0))] * 2 + [tab_spec, tab_spec],
        out_shape=[jax.ShapeDtypeStruct((nt, P, tt), jnp.int32),
                   jax.ShapeDtypeStruct((nt, P, tt), F32),
                   jax.ShapeDtypeStruct((E, D), BF16), jax.ShapeDtypeStruct((E, D), BF16)],
        scratch_shapes=[pltpu.VMEM((2, PEER_TOPK, tt), F32), pltpu.VMEM((2, PEER_TOPK, tt), F32)],
        compiler_params=_params(("parallel",)),
    )(qp, subkeys, u, v)


def _peer_gbuild_kernel(e_ref, g_ref, o_ref, i1_ref, i2_ref, gt_ref):
    tt = e_ref.shape[2]
    e = e_ref[0].T
    i1_ref[...] = e >> 7
    i2_ref[...] = e & (N_KEYS - 1)
    gt_ref[...] = g_ref[0].T
    P = e.shape[1]
    k_io = lax.broadcasted_iota(jnp.int32, (N_KEYS, P), 0)
    group = 64

    def body(tg, carry):
        base = pl.multiple_of(tg * group, group)
        r1s = i1_ref[pl.ds(base, group), :]
        r2s = i2_ref[pl.ds(base, group), :]
        rgs = gt_ref[pl.ds(base, group), :]
        for u in range(group):
            lhs = jnp.where(k_io == r1s[u:u + 1, :], rgs[u:u + 1, :], 0.0).astype(BF16)
            rhs = jnp.where(k_io == r2s[u:u + 1, :], 1.0, 0.0).astype(BF16)
            g_t = _dot_nt(lhs, rhs).astype(BF16).astype(F32)
            lo = pltpu.bitcast(g_t[:HALF_KEYS], jnp.uint32) >> 16
            hi = pltpu.bitcast(g_t[HALF_KEYS:], jnp.uint32)
            words = hi | lo
            for a in range(HALF_KEYS // PACK_ROWS):
                o_ref[a, base + u] = words[a * PACK_ROWS:(a + 1) * PACK_ROWS, :]
        return carry

    lax.fori_loop(0, tt // group, body, 0)


def peer_gbuild(ex, gates):
    nt, P, tt = ex.shape
    T = nt * tt
    n_a = HALF_KEYS // PACK_ROWS
    return pl.pallas_call(
        _peer_gbuild_kernel,
        grid=(nt,),
        in_specs=[pl.BlockSpec((1, P, tt), lambda i: (i, 0, 0))] * 2,
        out_specs=pl.BlockSpec((n_a, tt, PACK_ROWS, N_KEYS), lambda i: (0, i, 0, 0)),
        out_shape=jax.ShapeDtypeStruct((n_a, T, PACK_ROWS, N_KEYS), jnp.uint32),
        scratch_shapes=[pltpu.VMEM((tt, P), jnp.int32), pltpu.VMEM((tt, P), jnp.int32),
                        pltpu.VMEM((tt, P), F32)],
        compiler_params=_params(("parallel",)),
    )(ex, gates)


def _peer_dense_kernel(h_ref, nw_ref, fw_ref, u_ref, v_ref, gm_ref, o_ref, hn_ref, acc_ref):
    j = pl.program_id(1)
    tt = h_ref.shape[0]

    @pl.when(j == 0)
    def _():
        x = h_ref[...]
        y = x * lax.rsqrt(jnp.mean(x * x, axis=-1, keepdims=True) + EPS)
        hn_ref[...] = (y * nw_ref[...]).astype(BF16)
        acc_ref[...] = jnp.zeros(acc_ref.shape, F32)

    a = _dot_nt(hn_ref[...], u_ref[...])
    shift = (16 * (1 - j % 2)).astype(jnp.uint32)
    gm = jnp.concatenate(
        [pltpu.bitcast((gm_ref[pl.ds(k, tt, stride=PACK_ROWS), :] << shift) & jnp.uint32(0xFFFF0000), F32)
         for k in range(PACK_ROWS)], axis=1)
    w = (gm * jax.nn.gelu(a)).astype(BF16)
    acc_ref[...] += _dot(w, v_ref[...])

    @pl.when(j == pl.num_programs(1) - 1)
    def _():
        x = h_ref[...] + acc_ref[...]
        y = x * lax.rsqrt(jnp.mean(x * x, axis=-1, keepdims=True) + EPS)
        o_ref[...] = y * fw_ref[...]


def peer_dense(h, ffn_nw, final_nw, u, v, gm_words, tt):
    T, D = h.shape
    E = u.shape[0]
    ec = PACK_ROWS * N_KEYS
    n_a = gm_words.shape[0]
    assert E == 2 * n_a * ec
    gm2 = gm_words.reshape(n_a, T * PACK_ROWS, N_KEYS)
    tab_spec = pl.BlockSpec((ec, D), lambda i, j: ((j % 2) * n_a + j // 2, 0))
    return pl.pallas_call(
        _peer_dense_kernel,
        grid=(T // tt, E // ec),
        in_specs=[pl.BlockSpec((tt, D), lambda i, j: (i, 0)),
                  pl.BlockSpec((1, D), lambda i, j: (0, 0)),
                  pl.BlockSpec((1, D), lambda i, j: (0, 0)),
                  tab_spec, tab_spec,
                  pl.BlockSpec((None, tt * PACK_ROWS, N_KEYS), lambda i, j: (j // 2, i, 0))],
        out_specs=pl.BlockSpec((tt, D), lambda i, j: (i, 0)),
        out_shape=jax.ShapeDtypeStruct((T, D), F32),
        scratch_shapes=[pltpu.VMEM((tt, D), BF16), pltpu.VMEM((tt, D), F32)],
        compiler_params=_params(("parallel", "arbitrary")),
    )(h, ffn_nw.reshape(1, D), final_nw.reshape(1, D), u, v, gm2)


def _t5_bucket_np(n_dist):
    d = np.arange(n_dist)
    max_exact = NUM_BUCKETS // 2
    nf = np.maximum(d, 1).astype(np.float64)
    large = max_exact + (np.log(nf / max_exact) / math.log(MAX_DISTANCE / max_exact)
                         * (NUM_BUCKETS - max_exact)).astype(np.int64)
    large = np.minimum(large, NUM_BUCKETS - 1)
    return np.where(d < max_exact, d, large).astype(np.int32)


def _overlap_t_np(S):
    n_c = (S - L_CMP) // STRIDE_CMP + 1
    n_sel = S // L_SEL
    pos = np.arange(n_c)[:, None] * STRIDE_CMP + np.arange(L_CMP)[None, :]
    m = np.zeros((n_c + 1, n_sel), np.float32)
    np.add.at(m, (np.repeat(np.arange(n_c), L_CMP), (pos // L_SEL).reshape(-1)), 1.0 / L_CMP)
    return np.ascontiguousarray(m.T)


def _block_onehot_padded_np(S):
    assert S // L_SEL < HEAD_DIM
    e2 = np.zeros((WINDOW + S, HEAD_DIM), np.float32)
    e2[WINDOW + np.arange(S), np.arange(S) // L_SEL] = 1.0
    e2[:WINDOW, HEAD_DIM - 1] = SEL_OFF
    return e2


def nsa_conv_mix(xt, B, S, attn_norm_w, w_in, w_cmp_k, w_cmp_v, cmp_pos, conv_w, conv_b,
                 attn_gnw, conv_gnw, rel_bias, tm=1024, tm_conv=512):
    T, D = xt.shape
    dq = N_Q_HEADS * HEAD_DIM
    dkv = N_KV_HEADS * HEAD_DIM
    n_attn = dq + 6 * dkv
    n_gate = 3 * N_Q_HEADS
    dc = (w_in.shape[1] - n_attn - n_gate) // 3
    w_t = w_in.T
    w_attn = cast_rows(w_t, 0, n_attn)
    w_conv = cast_rows(w_t, n_attn + n_gate, 3 * dc)
    w_gate = jnp.pad(w_t[n_attn:n_attn + n_gate], ((0, HEAD_DIM - n_gate), (0, 0))).astype(BF16)

    qkv = norm_matmul(xt, attn_norm_w, w_attn, BF16, tm, n_attn // 2, w_is_nk=True)
    gl = norm_matmul(xt, attn_norm_w, w_gate, F32, tm, HEAD_DIM, w_is_nk=True)
    gc = norm_matmul(xt, attn_norm_w, w_conv, F32, tm, dc, w_is_nk=True)

    n16 = S // STRIDE_CMP
    k16 = qkv[:, dq:dq + dkv].reshape(B * n16, STRIDE_CMP * dkv)
    v16 = qkv[:, dq + dkv:dq + 2 * dkv].reshape(B * n16, STRIDE_CMP * dkv)

    def wbig(w, lo):
        wl = w[lo:lo + STRIDE_CMP]
        eye = jnp.eye(N_KV_HEADS, dtype=w.dtype)
        return jnp.einsum('lde,hg->lhdge', wl, eye).reshape(STRIDE_CMP * dkv, dkv).astype(BF16)

    def posrow(lo):
        p = cmp_pos[lo:lo + STRIDE_CMP]
        return jnp.broadcast_to(p[:, None, :], (STRIDE_CMP, N_KV_HEADS, HEAD_DIM)).reshape(1, STRIDE_CMP * dkv)

    kc, vc = compress(k16, v16, posrow(0), posrow(STRIDE_CMP),
                      wbig(w_cmp_k, 0), wbig(w_cmp_k, STRIDE_CMP),
                      wbig(w_cmp_v, 0), wbig(w_cmp_v, STRIDE_CMP), B)

    tab = rel_bias[_t5_bucket_np(HEAD_DIM)].T
    ovt = jnp.asarray(_overlap_t_np(S), BF16)
    e2p = jnp.asarray(_block_onehot_padded_np(S), BF16)

    ocmp, selb = cmp_select(qkv, kc, vc, tab, ovt, B, S)
    def keys_padded(col):
        k = qkv[:, col:col + dkv].reshape(B, S, dkv)
        return jnp.pad(k, ((0, 0), (WINDOW, 0), (0, 0))).reshape(B * (WINDOW + S), dkv)

    def values_t_padded(col):
        v = qkv[:, col:col + dkv].reshape(B, S, dkv).transpose(0, 2, 1)
        return jnp.pad(v, ((0, 0), (0, 0), (WINDOW, 0))).reshape(B * dkv, WINDOW + S)

    attn = sel_win(qkv, selb, keys_padded(dq + 2 * dkv), values_t_padded(dq + 3 * dkv),
                   keys_padded(dq + 4 * dkv), values_t_padded(dq + 5 * dkv), e2p, tab, ocmp, gl,
                   attn_gnw.reshape(1, dq), B, S)
    cw8 = jnp.pad(conv_w, ((0, 8 - conv_w.shape[0]), (0, 0)))
    conv = conv_mixer(gc, cw8, conv_b.reshape(1, dc), conv_gnw.reshape(1, dc), S, tm_conv)
    return attn, conv


def peer_block(h, ffn_nw, final_nw, peer_wq, peer_subkeys, peer_u, peer_v, tm=1024, tt=512):
    T, D = h.shape
    qp = norm_matmul(h, ffn_nw, peer_wq.astype(BF16), BF16, tm, 1024)
    sk = peer_subkeys.reshape(PEER_HEADS * 2, N_KEYS, peer_subkeys.shape[-1]).astype(BF16)
    ex, gates, u_bf, v_bf = peer_route(qp, sk, peer_u, peer_v)
    gm = peer_gbuild(ex, gates)
    return peer_dense(h, ffn_nw, final_nw, u_bf, v_bf, gm, tt)


def kernel(x, attn_norm_w, w_in, w_cmp_k, w_cmp_v, cmp_pos, conv_w, conv_b, attn_group_norm_w,
           conv_group_norm_w, w_out, rel_bias, ffn_norm_w, peer_wq, peer_subkeys, peer_u, peer_v,
           final_norm_w):
    B, S, D = x.shape
    T = B * S
    xt = x.reshape(T, D)
    attn, conv = nsa_conv_mix(xt, B, S, attn_norm_w[0], w_in[0], w_cmp_k[0], w_cmp_v[0], cmp_pos[0],
                              conv_w[0], conv_b[0], attn_group_norm_w[0], conv_group_norm_w[0], rel_bias)
    da = attn.shape[1]
    wo = w_out[0].astype(BF16)
    h = out_proj(attn, conv, wo[:da], wo[da:], xt, 1024, 1024)
    out = peer_block(h, ffn_norm_w[0], final_norm_w, peer_wq[0], peer_subkeys[0], peer_u[0], peer_v[0])
    return out.reshape(B, S, D)
```

```python
import functools
import math

import jax
import jax.numpy as jnp
import numpy as np
from jax import lax
from jax.experimental import pallas as pl
from jax.experimental.pallas import tpu as pltpu

F32 = jnp.float32
BF16 = jnp.bfloat16

HEAD_DIM = 128
N_KV_HEADS = 2
GQA = 4
N_Q_HEADS = N_KV_HEADS * GQA
L_CMP = 32
STRIDE_CMP = 16
L_SEL = 64
N_SEL = 16
WINDOW = 512
Q_BLOCK = 128
FORCED_SCORE = float(GQA + 1)
NUM_BUCKETS = 32
MAX_DISTANCE = 128
PEER_HEADS = 8
N_KEYS = 128
PEER_TOPK = 16
EPS = 1e-6
NEG_BIG = -1e30
SEL_OFF = -float(2 ** 30)
LOG2E = math.log2(math.e)
HALF_KEYS = N_KEYS // 2
PACK_ROWS = 8
N_FIXED_A = 4
VMEM_LIMIT = 56 * 1024 * 1024


def _dot(a, b):
    return jnp.dot(a, b, preferred_element_type=F32)


def _dot_nt(a, b):
    return lax.dot_general(a, b, (((1,), (1,)), ((), ())), preferred_element_type=F32)


def _params(sem, vmem=VMEM_LIMIT):
    return pltpu.CompilerParams(dimension_semantics=sem, vmem_limit_bytes=vmem)


def _norm_matmul_kernel(x_ref, nw_ref, w_ref, *rest, w_is_nk, has_side):
    if has_side:
        ws_ref, o_ref, os_ref, xn_ref = rest
    else:
        o_ref, xn_ref = rest

    @pl.when(pl.program_id(1) == 0)
    def _():
        x = x_ref[...]
        y = x * lax.rsqrt(jnp.mean(x * x, axis=-1, keepdims=True) + EPS)
        xn_ref[...] = (y * nw_ref[...]).astype(BF16)
        if has_side:
            os_ref[...] = _dot_nt(xn_ref[...], ws_ref[...])

    dot = _dot_nt if w_is_nk else _dot
    o_ref[...] = dot(xn_ref[...], w_ref[...]).astype(o_ref.dtype)


def norm_matmul(x, norm_w, w, out_dtype, tm, tn, w_is_nk=False, w_side=None):
    T, D = x.shape
    N = w.shape[0] if w_is_nk else w.shape[1]
    w_spec = pl.BlockSpec((tn, D), lambda i, j: (j, 0)) if w_is_nk else pl.BlockSpec((D, tn), lambda i, j: (0, j))
    in_specs = [pl.BlockSpec((tm, D), lambda i, j: (i, 0)), pl.BlockSpec((1, D), lambda i, j: (0, 0)), w_spec]
    out_specs = [pl.BlockSpec((tm, tn), lambda i, j: (i, j))]
    out_shape = [jax.ShapeDtypeStruct((T, N), out_dtype)]
    args = [x, norm_w.reshape(1, D), w]
    if w_side is not None:
        n_side = w_side.shape[0]
        in_specs.append(pl.BlockSpec((n_side, D), lambda i, j: (0, 0)))
        out_specs.append(pl.BlockSpec((tm, n_side), lambda i, j: (i, 0)))
        out_shape.append(jax.ShapeDtypeStruct((T, n_side), F32))
        args.append(w_side)
    out = pl.pallas_call(
        functools.partial(_norm_matmul_kernel, w_is_nk=w_is_nk, has_side=w_side is not None),
        grid=(T // tm, N // tn),
        in_specs=in_specs,
        out_specs=out_specs,
        out_shape=out_shape,
        scratch_shapes=[pltpu.VMEM((tm, D), BF16)],
        compiler_params=_params(("parallel", "arbitrary")),
    )(*args)
    return out if w_side is not None else out[0]


def _cast_rows_kernel(w_ref, o_ref):
    o_ref[...] = w_ref[...].astype(o_ref.dtype)


def cast_rows(w, row0, n_rows, tr=256):
    K = w.shape[1]
    assert row0 % 8 == 0 and n_rows % tr == 0
    return pl.pallas_call(
        _cast_rows_kernel,
        grid=(n_rows // tr,),
        in_specs=[pl.BlockSpec((pl.Element(tr), pl.Element(K)), lambda i: (pl.multiple_of(row0 + i * tr, 8), 0))],
        out_specs=pl.BlockSpec((tr, K), lambda i: (i, 0)),
        out_shape=jax.ShapeDtypeStruct((n_rows, K), BF16),
        compiler_params=_params(("parallel",)),
    )(w)


def _compress_kernel(k_ref, v_ref, pa_ref, pb_ref, wka_ref, wkb_ref, wva_ref, wvb_ref, kc_ref, vc_ref):
    def one(x_ref, wa_ref, wb_ref, o_ref):
        x = x_ref[...].astype(F32)
        a = _dot((x + pa_ref[...]).astype(BF16), wa_ref[...])
        b = _dot((x + pb_ref[...]).astype(BF16), wb_ref[...])
        n = b.shape[0]
        o_ref[...] = (a + pltpu.roll(b, n - 1, 0)).astype(o_ref.dtype)

    one(k_ref, wka_ref, wkb_ref, kc_ref)
    one(v_ref, wva_ref, wvb_ref, vc_ref)


def compress(k16, v16, pos_a, pos_b, wka, wkb, wva, wvb, B):
    R, C = k16.shape
    nb = R // B
    dkv = wka.shape[1]
    full = lambda shp: pl.BlockSpec(shp, lambda b: (0, 0))
    return pl.pallas_call(
        _compress_kernel,
        grid=(B,),
        in_specs=[pl.BlockSpec((nb, C), lambda b: (b, 0)),
                  pl.BlockSpec((nb, C), lambda b: (b, 0)),
                  full((1, C)), full((1, C)),
                  full((C, dkv)), full((C, dkv)), full((C, dkv)), full((C, dkv))],
        out_specs=[pl.BlockSpec((nb, dkv), lambda b: (b, 0))] * 2,
        out_shape=[jax.ShapeDtypeStruct((R, dkv), BF16)] * 2,
        compiler_params=_params(("parallel",)),
    )(k16, v16, pos_a, pos_b, wka, wkb, wva, wvb)


def _bias_from_dist(tab_row, dist):
    idx = jnp.clip(dist, 0, 127)
    w = tab_row.shape[1]
    tab = jnp.broadcast_to(tab_row, (idx.shape[0], w))
    parts = [jnp.take_along_axis(tab, idx[:, k:k + w], axis=1) for k in range(0, idx.shape[1], w)]
    return parts[0] if len(parts) == 1 else jnp.concatenate(parts, axis=1)


def _cmp_select_kernel(q_ref, kc_ref, vc_ref, tab_ref, ovt_ref, ocmp_ref, selb_ref, *, n_qb, scale):
    c = pl.program_id(0) % n_qb
    tq = q_ref.shape[0]
    n_c = kc_ref.shape[0]
    n_sel = ovt_ref.shape[0]
    t0 = c * tq
    t_col = t0 + lax.broadcasted_iota(jnp.int32, (tq, n_c), 0)
    n_row = lax.broadcasted_iota(jnp.int32, (tq, n_c), 1)
    dist = t_col - (n_row * STRIDE_CMP + (L_CMP - 1))
    valid = dist >= 0

    j_io = lax.broadcasted_iota(jnp.int32, (n_sel, tq), 0)
    t_io = t0 + lax.broadcasted_iota(jnp.int32, (n_sel, tq), 1)
    blk_t = t_io // L_SEL
    forced = (j_io == 0) | (j_io == blk_t) | (j_io == blk_t - 1)
    causal_blk = j_io * L_SEL <= t_io

    for h in range(N_KV_HEADS):
        kc = kc_ref[:, h * HEAD_DIM:(h + 1) * HEAD_DIM]
        vc = vc_ref[:, h * HEAD_DIM:(h + 1) * HEAD_DIM]
        psum = jnp.zeros((tq, n_c), F32)
        for g in range(GQA):
            hd = h * GQA + g
            qh = q_ref[:, hd * HEAD_DIM:(hd + 1) * HEAD_DIM]
            bias = _bias_from_dist(tab_ref[hd:hd + 1, :], dist)
            s = _dot_nt(qh, kc) * scale + bias
            s = jnp.where(valid, s, NEG_BIG)
            m = jnp.max(s, axis=-1, keepdims=True)
            e = jnp.where(valid, jnp.exp(s - m), 0.0)
            d = jnp.sum(e, axis=-1, keepdims=True)
            p = e / jnp.where(d > 0, d, 1.0)
            ocmp_ref[:, hd * HEAD_DIM:(hd + 1) * HEAD_DIM] = _dot(p.astype(BF16), vc)
            psum = psum + p
        p_hi = psum.astype(BF16)
        p_lo = (psum - p_hi.astype(F32)).astype(BF16)
        ovt = ovt_ref[...]
        imp = _dot_nt(ovt, p_hi) + _dot_nt(ovt, p_lo)
        score = jnp.where(forced, FORCED_SCORE, jnp.where(causal_blk, imp, -1.0))
        rank = jnp.zeros((n_sel, tq), F32)
        for jp in range(n_sel):
            row = score[jp:jp + 1, :]
            rank = rank + jnp.where(j_io > jp, jnp.where(row >= score, 1.0, 0.0),
                                    jnp.where(row > score, 1.0, 0.0))
        selb = jnp.where(rank < float(N_SEL), 0.0, SEL_OFF)
        if n_sel < HEAD_DIM:
            selb = jnp.concatenate([selb, jnp.zeros((HEAD_DIM - n_sel, tq), F32)], axis=0)
        selb_ref[:, h * HEAD_DIM:(h + 1) * HEAD_DIM] = selb.T.astype(BF16)


def cmp_select(qkv, kc, vc, tab, ovt, B, S, tq=4 * Q_BLOCK):
    T = qkv.shape[0]
    n_qb = S // tq
    n_c = kc.shape[0] // B
    dq = N_Q_HEADS * HEAD_DIM
    dkv = N_KV_HEADS * HEAD_DIM
    kern = functools.partial(_cmp_select_kernel, n_qb=n_qb, scale=HEAD_DIM ** -0.5)
    return pl.pallas_call(
        kern,
        grid=(T // tq,),
        in_specs=[pl.BlockSpec((tq, dq), lambda i: (i, 0)),
                  pl.BlockSpec((n_c, dkv), lambda i: (i // n_qb, 0)),
                  pl.BlockSpec((n_c, dkv), lambda i: (i // n_qb, 0)),
                  pl.BlockSpec(tab.shape, lambda i: (0, 0)),
                  pl.BlockSpec(ovt.shape, lambda i: (0, 0))],
        out_specs=[pl.BlockSpec((tq, dq), lambda i: (i, 0)),
                   pl.BlockSpec((tq, dkv), lambda i: (i, 0))],
        out_shape=[jax.ShapeDtypeStruct((T, dq), F32),
                   jax.ShapeDtypeStruct((T, dkv), BF16)],
        compiler_params=_params(("parallel",)),
    )(qkv, kc, vc, tab, ovt)


def _flash_update(s_raw, vt, bias, m_ref, l_ref, acc_ref, scale):
    s = s_raw * scale + bias
    m_old = m_ref[...]
    m_new = jnp.maximum(m_old, jnp.max(s, axis=0, keepdims=True))
    alpha = jnp.exp2(m_old - m_new)
    p = jnp.exp2(s - m_new)
    l_ref[...] = alpha * l_ref[...] + jnp.sum(p, axis=0, keepdims=True)
    acc_ref[...] = alpha * acc_ref[...] + _dot(vt, p.astype(BF16))
    m_ref[...] = m_new


def _flash_chunk(k, vt, q, bias, m_ref, l_ref, acc_ref, scale):
    _flash_update(_dot_nt(k, q), vt, bias, m_ref, l_ref, acc_ref, scale)


def _sel_win_kernel(q_ref, selb_ref, ks_ref, vst_ref, kw_ref, vwt_ref, e2_ref, tab_ref, ocmp_ref,
                    gl_ref, gnw_ref, o_ref, m_ref, l_ref, acc_ref, sa_ref, sb_ref, *, n_qb, scale):
    c = pl.program_id(0) % n_qb
    tq = q_ref.shape[0]
    rows = GQA * tq
    n_back = WINDOW // tq
    near = WINDOW + tq
    j_io = lax.broadcasted_iota(jnp.int32, (tq, tq), 0)
    i_io = lax.broadcasted_iota(jnp.int32, (tq, tq), 1)
    dij = i_io - j_io
    dij4 = jnp.concatenate([dij] * GQA, axis=1)
    causal = dij4 >= 0
    pad_col = jnp.where(lax.broadcasted_iota(jnp.int32, (rows, HEAD_DIM), 1) == HEAD_DIM - 1, 1.0, 0.0).astype(BF16)
    sig = jax.nn.sigmoid(gl_ref[...])
    near0 = pl.multiple_of(c * tq, tq)
    n_far = jnp.maximum(c - n_back, 0)
    n_full = n_far // n_back
    n_rem = n_far - n_full * n_back

    def hsl(h):
        return slice(h * HEAD_DIM, (h + 1) * HEAD_DIM)

    def far0(i):
        return pl.multiple_of(WINDOW + i * WINDOW, WINDOW)

    def reset():
        m_ref[...] = jnp.full(m_ref.shape, NEG_BIG, F32)
        l_ref[...] = jnp.zeros(l_ref.shape, F32)
        acc_ref[...] = jnp.zeros(acc_ref.shape, F32)

    def result():
        ot = acc_ref[...] / l_ref[...]
        return [ot[:, g * tq:(g + 1) * tq].T for g in range(GQA)]

    for h in range(N_KV_HEADS):
        q4 = jnp.concatenate([q_ref[:, (h * GQA + g) * HEAD_DIM:(h * GQA + g + 1) * HEAD_DIM]
                              for g in range(GQA)], axis=0)
        sb4 = jnp.concatenate([selb_ref[:, hsl(h)]] * GQA, axis=0)
        q_aug = jnp.concatenate([q4, sb4 + pad_col], axis=1)
        q_win = jnp.concatenate([q4, pad_col], axis=1)
        tabs = [tab_ref[h * GQA + g:h * GQA + g + 1, :] * LOG2E for g in range(GQA)]
        d0 = jnp.concatenate([_bias_from_dist(t, dij) for t in tabs], axis=1)
        d1 = jnp.concatenate([_bias_from_dist(t, dij + tq) for t in tabs], axis=1)
        far = jnp.concatenate([t[:, HEAD_DIM - 1:HEAD_DIM] + jnp.zeros((1, tq), F32) for t in tabs], axis=1)
        far_t = jnp.broadcast_to(far, (tq, rows))
        diag = jnp.where(causal, d0, NEG_BIG)
        bias_sel = jnp.concatenate([far_t] * (n_back - 1) + [d1, diag], axis=0)
        bias_win = jnp.concatenate([jnp.where(dij4 < 0, far_t, NEG_BIG)] + [far_t] * (n_back - 2) + [d1, diag],
                                   axis=0)

        reset()
        e2_near = e2_ref[pl.ds(near0, near), :]
        k_near = jnp.concatenate([ks_ref[pl.ds(near0, near), hsl(h)], e2_near], axis=1)
        _flash_chunk(k_near, vst_ref[hsl(h), pl.ds(near0, near)], q_aug, bias_sel, m_ref, l_ref, acc_ref, scale)

        def far_scores(i):
            r0 = far0(i)
            k = jnp.concatenate([ks_ref[pl.ds(r0, WINDOW), hsl(h)], e2_ref[pl.ds(r0, WINDOW), :]], axis=1)
            return _dot_nt(k, q_aug)

        def far_update(s_raw, i, bias):
            _flash_update(s_raw, vst_ref[hsl(h), pl.ds(far0(i), WINDOW)], bias, m_ref, l_ref, acc_ref, scale)

        sa_ref[...] = far_scores(0)

        def body(i, carry):
            sb_ref[...] = far_scores(2 * i + 1)
            far_update(sa_ref[...], 2 * i, far)
            sa_ref[...] = far_scores(2 * i + 2)
            far_update(sb_ref[...], 2 * i + 1, far)
            return carry

        lax.fori_loop(0, n_full // 2, body, 0)

        @pl.when(n_full % 2 == 1)
        def _():
            far_update(sa_ref[...], n_full - 1, far)

        @pl.when(n_rem > 0)
        def _():
            if n_back == 2:
                r0 = far0(n_full)
                k = jnp.concatenate([ks_ref[pl.ds(r0, tq), hsl(h)], e2_ref[pl.ds(r0, tq), :]], axis=1)
                _flash_chunk(k, vst_ref[hsl(h), pl.ds(r0, tq)], q_aug, far, m_ref, l_ref, acc_ref, scale)
            else:
                live = lax.broadcasted_iota(jnp.int32, (WINDOW, rows), 0) < n_rem * tq
                far_update(far_scores(n_full), n_full,
                           jnp.where(live, jnp.broadcast_to(far, (WINDOW, rows)), NEG_BIG))

        o_sel = result()

        reset()
        kw_near = jnp.concatenate([kw_ref[pl.ds(near0, near), hsl(h)], e2_near], axis=1)
        _flash_chunk(kw_near, vwt_ref[hsl(h), pl.ds(near0, near)], q_win, bias_win, m_ref, l_ref, acc_ref, scale)
        o_win = result()

        for g in range(GQA):
            hd = h * GQA + g
            o = (sig[:, 3 * hd:3 * hd + 1] * ocmp_ref[:, hsl(hd)]
                 + sig[:, 3 * hd + 1:3 * hd + 2] * o_sel[g]
                 + sig[:, 3 * hd + 2:3 * hd + 3] * o_win[g])
            y = o * lax.rsqrt(jnp.mean(o * o, axis=-1, keepdims=True) + EPS)
            o_ref[:, hsl(hd)] = (y * gnw_ref[:, hsl(hd)]).astype(o_ref.dtype)


def sel_win(qkv, selb, ksp, vstp, kwp, vwtp, e2p, tab, ocmp, gl, gnw, B, S, tq=2 * Q_BLOCK):
    T = qkv.shape[0]
    n_qb = S // tq
    dq = N_Q_HEADS * HEAD_DIM
    dkv = N_KV_HEADS * HEAD_DIM
    SP = S + WINDOW
    kern = functools.partial(_sel_win_kernel, n_qb=n_qb, scale=HEAD_DIM ** -0.5 * LOG2E)
    k_spec = pl.BlockSpec((SP, dkv), lambda i: (i // n_qb, 0))
    vt_spec = pl.BlockSpec((dkv, SP), lambda i: (i // n_qb, 0))
    return pl.pallas_call(
        kern,
        grid=(T // tq,),
        in_specs=[pl.BlockSpec((tq, dq), lambda i: (i, 0)),
                  pl.BlockSpec((tq, dkv), lambda i: (i, 0)),
                  k_spec, vt_spec, k_spec, vt_spec,
                  pl.BlockSpec((SP, HEAD_DIM), lambda i: (0, 0)),
                  pl.BlockSpec(tab.shape, lambda i: (0, 0)),
                  pl.BlockSpec((tq, dq), lambda i: (i, 0)),
                  pl.BlockSpec((tq, HEAD_DIM), lambda i: (i, 0)),
                  pl.BlockSpec((1, dq), lambda i: (0, 0))],
        out_specs=pl.BlockSpec((tq, dq), lambda i: (i, 0)),
        out_shape=jax.ShapeDtypeStruct((T, dq), BF16),
        scratch_shapes=[pltpu.VMEM((1, GQA * tq), F32), pltpu.VMEM((1, GQA * tq), F32),
                        pltpu.VMEM((HEAD_DIM, GQA * tq), F32),
                        pltpu.VMEM((WINDOW, GQA * tq), F32), pltpu.VMEM((WINDOW, GQA * tq), F32)],
        compiler_params=_params(("parallel",)),
    )(qkv, selb, ksp, vstp, kwp, vwtp, e2p, tab, ocmp, gl, gnw)


def _conv_kernel(b_ref, c_ref, h_ref, cp_ref, hp_ref, cw_ref, cb_ref, gnw_ref, o_ref, u_ref, *, tiles_per_seq):
    tm = b_ref.shape[0]
    first = (pl.program_id(0) % tiles_per_seq) == 0
    u_prev = cp_ref[...] * hp_ref[...]
    u_ref[0:8, :] = jnp.where(first, 0.0, u_prev)
    u = c_ref[...] * h_ref[...]
    u_ref[8:8 + tm, :] = u
    y = (cw_ref[0:1, :] * u_ref[6:6 + tm, :] + cw_ref[1:2, :] * u_ref[7:7 + tm, :]
         + cw_ref[2:3, :] * u + cb_ref[...])
    o = b_ref[...] * y
    n_groups = o.shape[1] // HEAD_DIM
    for g in range(n_groups):
        sl = slice(g * HEAD_DIM, (g + 1) * HEAD_DIM)
        og = o[:, sl]
        yg = og * lax.rsqrt(jnp.mean(og * og, axis=-1, keepdims=True) + EPS)
        o_ref[:, sl] = (yg * gnw_ref[:, sl]).astype(o_ref.dtype)


def conv_mixer(gc, conv_w, conv_b, gnw, S, tm):
    T = gc.shape[0]
    dc = conv_w.shape[1]
    tps = S // tm
    kern = functools.partial(_conv_kernel, tiles_per_seq=tps)
    prev = lambda col: pl.BlockSpec((8, dc), lambda i, col=col: (jnp.maximum(i * (tm // 8) - 1, 0), col))
    cur = lambda col: pl.BlockSpec((tm, dc), lambda i, col=col: (i, col))
    return pl.pallas_call(
        kern,
        grid=(T // tm,),
        in_specs=[cur(0), cur(1), cur(2), prev(1), prev(2),
                  pl.BlockSpec((8, dc), lambda i: (0, 0)),
                  pl.BlockSpec((1, dc), lambda i: (0, 0)),
                  pl.BlockSpec((1, dc), lambda i: (0, 0))],
        out_specs=pl.BlockSpec((tm, dc), lambda i: (i, 0)),
        out_shape=jax.ShapeDtypeStruct((T, dc), BF16),
        scratch_shapes=[pltpu.VMEM((tm + 8, dc), F32)],
        compiler_params=_params(("parallel",)),
    )(gc, gc, gc, gc, gc, conv_w, conv_b, gnw)


def _out_proj_kernel(ma_ref, mc_ref, wa_ref, wc_ref, x_ref, o_ref):
    o_ref[...] = x_ref[...] + _dot(ma_ref[...], wa_ref[...]) + _dot(mc_ref[...], wc_ref[...])


def out_proj(ma, mc, wa, wc, x, tm, tn):
    T, da = ma.shape
    dc = mc.shape[1]
    D = x.shape[1]
    return pl.pallas_call(
        _out_proj_kernel,
        grid=(T // tm, D // tn),
        in_specs=[pl.BlockSpec((tm, da), lambda i, j: (i, 0)),
                  pl.BlockSpec((tm, dc), lambda i, j: (i, 0)),
                  pl.BlockSpec((da, tn), lambda i, j: (0, j)),
                  pl.BlockSpec((dc, tn), lambda i, j: (0, j)),
                  pl.BlockSpec((tm, tn), lambda i, j: (i, j))],
        out_specs=pl.BlockSpec((tm, tn), lambda i, j: (i, j)),
        out_shape=jax.ShapeDtypeStruct((T, D), F32),
        compiler_params=_params(("parallel", "arbitrary")),
    )(ma, mc, wa, wc, x)


def _sort_network(n):
    pairs = []
    p = 1
    while p < n:
        k = p
        while k >= 1:
            for j in range(k % p, n - k, 2 * k):
                for i in range(min(k, n - j - k)):
                    if (i + j) // (2 * p) == (i + j + k) // (2 * p):
                        pairs.append((i + j, i + j + k))
            k //= 2
        p *= 2
    return pairs


def _peer_route_kernel(q_ref, sk_ref, u_ref, v_ref, e_ref, g_ref, ub_ref, vb_ref, sv_ref, si_ref):
    ub_ref[...] = u_ref[...].astype(BF16)
    vb_ref[...] = v_ref[...].astype(BF16)
    tt = q_ref.shape[0]
    K = PEER_TOPK
    sub = 8
    n_io = lax.broadcasted_iota(jnp.int32, (N_KEYS, tt), 0).astype(F32)
    r8 = lax.broadcasted_iota(jnp.int32, (sub, tt), 0).astype(F32)
    ninf = -jnp.inf
    lens = [K // (a + 1) for a in range(N_FIXED_A)]
    lens += [max(K // (b + 1) - N_FIXED_A, 0) for b in range(sub - N_FIXED_A)]
    assert sum(lens) == sum(K // (a + 1) for a in range(K)) and lens[-1] == 0
    fixed_a = r8 < float(N_FIXED_A)
    list_len = jnp.zeros((sub, tt), F32)
    for row, n in enumerate(lens):
        list_len = jnp.where(r8 == float(row), float(n), list_len)

    for h in range(PEER_HEADS):
        for c in range(2):
            col = (h * 2 + c) * HEAD_DIM
            s = _dot_nt(sk_ref[h * 2 + c], q_ref[:, col:col + HEAD_DIM])
            n_col = N_KEYS // sub
            vals = [s[j * sub:(j + 1) * sub, :] for j in range(n_col)]
            idxs = [n_io[j * sub:(j + 1) * sub, :] for j in range(n_col)]
            for lo, hi in _sort_network(n_col):
                swap = (vals[hi] > vals[lo]) | ((vals[hi] == vals[lo]) & (idxs[hi] < idxs[lo]))
                vals[lo], vals[hi] = jnp.where(swap, vals[hi], vals[lo]), jnp.where(swap, vals[lo], vals[hi])
                idxs[lo], idxs[hi] = jnp.where(swap, idxs[hi], idxs[lo]), jnp.where(swap, idxs[lo], idxs[hi])
            for k in range(K):
                m = jnp.max(vals[0], axis=0, keepdims=True)
                idx = jnp.min(jnp.where(vals[0] == m, idxs[0], float(N_KEYS)), axis=0, keepdims=True)
                sv_ref[c, k:k + 1, :] = m
                si_ref[c, k:k + 1, :] = idx
                win = idxs[0] == idx
                for j in range(min(n_col, K) - 1 - k):
                    vals[j] = jnp.where(win, vals[j + 1], vals[j])
                    idxs[j] = jnp.where(win, idxs[j + 1], idxs[j])
        sv0, sv1 = sv_ref[0], sv_ref[1]
        si0, si1 = si_ref[0], si_ref[1]
        sv1_low = pltpu.roll(sv1[0:sub, :], N_FIXED_A, 0)
        sv0_top = sv0[0:sub, :]
        lv, lf = [], []
        for dep in range(K):
            a_dep = min(N_FIXED_A + dep, K - 1)
            val = (jnp.where(fixed_a, sv0_top, sv0[a_dep:a_dep + 1, :])
                   + jnp.where(fixed_a, sv1[dep:dep + 1, :], sv1_low))
            lv.append(jnp.where(list_len > dep, val, ninf))
            lf.append(jnp.where(fixed_a, r8 * float(K) + dep, (N_FIXED_A + dep) * float(K) + r8 - N_FIXED_A))
        cvs, fls = [], []
        for k in range(K):
            m = jnp.max(lv[0], axis=0, keepdims=True)
            fsel = jnp.min(jnp.where(lv[0] == m, lf[0], 1e9), axis=0, keepdims=True)
            win = lf[0] == fsel
            cvs.append(m)
            fls.append(fsel)
            for dep in range(K - 1 - k):
                lv[dep] = jnp.where(win, lv[dep + 1], lv[dep])
                lf[dep] = jnp.where(win, lf[dep + 1], lf[dep])
        cv = jnp.concatenate(cvs, axis=0)
        fl = jnp.concatenate(fls, axis=0)
        a_sel = jnp.floor(fl * (1.0 / K))
        b_sel = fl - a_sel * K
        i1 = jnp.zeros_like(fl)
        i2 = jnp.zeros_like(fl)
        for r in range(K):
            i1 = jnp.where(a_sel == r, si0[r:r + 1, :], i1)
            i2 = jnp.where(b_sel == r, si1[r:r + 1, :], i2)
        ex = i1 * float(N_KEYS) + i2
        ev = jnp.exp(cv - jnp.max(cv, axis=0, keepdims=True))
        gates = ev / jnp.sum(ev, axis=0, keepdims=True)
        e_ref[0, h * K:(h + 1) * K, :] = ex.astype(jnp.int32)
        g_ref[0, h * K:(h + 1) * K, :] = gates


def peer_route(qp, subkeys, u, v):
    T = qp.shape[0]
    tt = 128
    P = PEER_HEADS * PEER_TOPK
    nt = T // tt
    E, D = u.shape
    slab = E // nt
    assert slab * nt == E and slab % 16 == 0
    tab_spec = pl.BlockSpec((slab, D), lambda i: (i, 0))
    return pl.pallas_call(
        _peer_route_kernel,
        grid=(nt,),
        in_specs=[pl.BlockSpec((tt, qp.shape[1]), lambda i: (i, 0)),
                  pl.BlockSpec(subkeys.shape, lambda i: (0, 0, 0)),
                  tab_spec, tab_spec],
        out_specs=[pl.BlockSpec((1, P, tt), lambda i: (i, 0, 0))] * 2 + [tab_spec, tab_spec],
        out_shape=[jax.ShapeDtypeStruct((nt, P, tt), jnp.int32),
                   jax.ShapeDtypeStruct((nt, P, tt), F32),
                   jax.ShapeDtypeStruct((E, D), BF16), jax.ShapeDtypeStruct((E, D), BF16)],
        scratch_shapes=[pltpu.VMEM((2, PEER_TOPK, tt), F32), pltpu.VMEM((2, PEER_TOPK, tt), F32)],
        compiler_params=_params(("parallel",)),
    )(qp, subkeys, u, v)


def _peer_gbuild_kernel(e_ref, g_ref, o_ref, i1_ref, i2_ref, gt_ref):
    tt = e_ref.shape[2]
    e = e_ref[0].T
    i1_ref[...] = e >> 7
    i2_ref[...] = e & (N_KEYS - 1)
    gt_ref[...] = g_ref[0].T
    P = e.shape[1]
    k_io = lax.broadcasted_iota(jnp.int32, (N_KEYS, P), 0)
    group = 64

    def body(tg, carry):
        base = pl.multiple_of(tg * group, group)
        r1s = i1_ref[pl.ds(base, group), :]
        r2s = i2_ref[pl.ds(base, group), :]
        rgs = gt_ref[pl.ds(base, group), :]
        for u in range(group):
            lhs = jnp.where(k_io == r1s[u:u + 1, :], rgs[u:u + 1, :], 0.0).astype(BF16)
            rhs = jnp.where(k_io == r2s[u:u + 1, :], 1.0, 0.0).astype(BF16)
            g_t = _dot_nt(lhs, rhs).astype(BF16).astype(F32)
            lo = pltpu.bitcast(g_t[:HALF_KEYS], jnp.uint32) >> 16
            hi = pltpu.bitcast(g_t[HALF_KEYS:], jnp.uint32)
            words = hi | lo
            for a in range(HALF_KEYS // PACK_ROWS):
                o_ref[a, base + u] = words[a * PACK_ROWS:(a + 1) * PACK_ROWS, :]
        return carry

    lax.fori_loop(0, tt // group, body, 0)


def peer_gbuild(ex, gates):
    nt, P, tt = ex.shape
    T = nt * tt
    n_a = HALF_KEYS // PACK_ROWS
    return pl.pallas_call(
        _peer_gbuild_kernel,
        grid=(nt,),
        in_specs=[pl.BlockSpec((1, P, tt), lambda i: (i, 0, 0))] * 2,
        out_specs=pl.BlockSpec((n_a, tt, PACK_ROWS, N_KEYS), lambda i: (0, i, 0, 0)),
        out_shape=jax.ShapeDtypeStruct((n_a, T, PACK_ROWS, N_KEYS), jnp.uint32),
        scratch_shapes=[pltpu.VMEM((tt, P), jnp.int32), pltpu.VMEM((tt, P), jnp.int32),
                        pltpu.VMEM((tt, P), F32)],
        compiler_params=_params(("parallel",)),
    )(ex, gates)


def _peer_dense_kernel(h_ref, nw_ref, fw_ref, u_ref, v_ref, gm_ref, o_ref, hn_ref, acc_ref):
    j = pl.program_id(1)
    tt = h_ref.shape[0]

    @pl.when(j == 0)
    def _():
        x = h_ref[...]
        y = x * lax.rsqrt(jnp.mean(x * x, axis=-1, keepdims=True) + EPS)
        hn_ref[...] = (y * nw_ref[...]).astype(BF16)
        acc_ref[...] = jnp.zeros(acc_ref.shape, F32)

    a = _dot_nt(hn_ref[...], u_ref[...])
    shift = (16 * (1 - j % 2)).astype(jnp.uint32)
    gm = jnp.concatenate(
        [pltpu.bitcast((gm_ref[pl.ds(k, tt, stride=PACK_ROWS), :] << shift) & jnp.uint32(0xFFFF0000), F32)
         for k in range(PACK_ROWS)], axis=1)
    w = (gm * jax.nn.gelu(a)).astype(BF16)
    acc_ref[...] += _dot(w, v_ref[...])

    @pl.when(j == pl.num_programs(1) - 1)
    def _():
        x = h_ref[...] + acc_ref[...]
        y = x * lax.rsqrt(jnp.mean(x * x, axis=-1, keepdims=True) + EPS)
        o_ref[...] = y * fw_ref[...]


def peer_dense(h, ffn_nw, final_nw, u, v, gm_words, tt):
    T, D = h.shape
    E = u.shape[0]
    ec = PACK_ROWS * N_KEYS
    n_a = gm_words.shape[0]
    assert E == 2 * n_a * ec
    gm2 = gm_words.reshape(n_a, T * PACK_ROWS, N_KEYS)
    tab_spec = pl.BlockSpec((ec, D), lambda i, j: ((j % 2) * n_a + j // 2, 0))
    return pl.pallas_call(
        _peer_dense_kernel,
        grid=(T // tt, E // ec),
        in_specs=[pl.BlockSpec((tt, D), lambda i, j: (i, 0)),
                  pl.BlockSpec((1, D), lambda i, j: (0, 0)),
                  pl.BlockSpec((1, D), lambda i, j: (0, 0)),
                  tab_spec, tab_spec,
                  pl.BlockSpec((None, tt * PACK_ROWS, N_KEYS), lambda i, j: (j // 2, i, 0))],
        out_specs=pl.BlockSpec((tt, D), lambda i, j: (i, 0)),
        out_shape=jax.ShapeDtypeStruct((T, D), F32),
        scratch_shapes=[pltpu.VMEM((tt, D), BF16), pltpu.VMEM((tt, D), F32)],
        compiler_params=_params(("parallel", "arbitrary")),
    )(h, ffn_nw.reshape(1, D), final_nw.reshape(1, D), u, v, gm2)


def _t5_bucket_np(n_dist):
    d = np.arange(n_dist)
    max_exact = NUM_BUCKETS // 2
    nf = np.maximum(d, 1).astype(np.float64)
    large = max_exact + (np.log(nf / max_exact) / math.log(MAX_DISTANCE / max_exact)
                         * (NUM_BUCKETS - max_exact)).astype(np.int64)
    large = np.minimum(large, NUM_BUCKETS - 1)
    return np.where(d < max_exact, d, large).astype(np.int32)


def _overlap_t_np(S):
    n_c = (S - L_CMP) // STRIDE_CMP + 1
    n_sel = S // L_SEL
    pos = np.arange(n_c)[:, None] * STRIDE_CMP + np.arange(L_CMP)[None, :]
    m = np.zeros((n_c + 1, n_sel), np.float32)
    np.add.at(m, (np.repeat(np.arange(n_c), L_CMP), (pos // L_SEL).reshape(-1)), 1.0 / L_CMP)
    return np.ascontiguousarray(m.T)


def _block_onehot_padded_np(S):
    assert S // L_SEL < HEAD_DIM
    e2 = np.zeros((WINDOW + S, HEAD_DIM), np.float32)
    e2[WINDOW + np.arange(S), np.arange(S) // L_SEL] = 1.0
    e2[:WINDOW, HEAD_DIM - 1] = SEL_OFF
    return e2


def nsa_conv_mix(xt, B, S, attn_norm_w, w_in, w_cmp_k, w_cmp_v, cmp_pos, conv_w, conv_b,
                 attn_gnw, conv_gnw, rel_bias, tm=1024, tm_conv=512):
    T, D = xt.shape
    dq = N_Q_HEADS * HEAD_DIM
    dkv = N_KV_HEADS * HEAD_DIM
    n_attn = dq + 6 * dkv
    n_gate = 3 * N_Q_HEADS
    dc = (w_in.shape[1] - n_attn - n_gate) // 3
    w_t = w_in.T
    w_attn = cast_rows(w_t, 0, n_attn)
    w_conv = cast_rows(w_t, n_attn + n_gate, 3 * dc)
    w_gate = jnp.pad(w_t[n_attn:n_attn + n_gate], ((0, HEAD_DIM - n_gate), (0, 0))).astype(BF16)

    qkv = norm_matmul(xt, attn_norm_w, w_attn, BF16, tm, n_attn // 2, w_is_nk=True)
    gc, gl = norm_matmul(xt, attn_norm_w, w_conv, F32, tm, dc, w_is_nk=True, w_side=w_gate)

    n16 = S // STRIDE_CMP
    k16 = qkv[:, dq:dq + dkv].reshape(B * n16, STRIDE_CMP * dkv)
    v16 = qkv[:, dq + dkv:dq + 2 * dkv].reshape(B * n16, STRIDE_CMP * dkv)

    def wbig(w, lo):
        wl = w[lo:lo + STRIDE_CMP]
        eye = jnp.eye(N_KV_HEADS, dtype=w.dtype)
        return jnp.einsum('lde,hg->lhdge', wl, eye).reshape(STRIDE_CMP * dkv, dkv).astype(BF16)

    def posrow(lo):
        p = cmp_pos[lo:lo + STRIDE_CMP]
        return jnp.broadcast_to(p[:, None, :], (STRIDE_CMP, N_KV_HEADS, HEAD_DIM)).reshape(1, STRIDE_CMP * dkv)

    kc, vc = compress(k16, v16, posrow(0), posrow(STRIDE_CMP),
                      wbig(w_cmp_k, 0), wbig(w_cmp_k, STRIDE_CMP),
                      wbig(w_cmp_v, 0), wbig(w_cmp_v, STRIDE_CMP), B)

    tab = rel_bias[_t5_bucket_np(HEAD_DIM)].T
    ovt = jnp.asarray(_overlap_t_np(S), BF16)
    e2p = jnp.asarray(_block_onehot_padded_np(S), BF16)

    ocmp, selb = cmp_select(qkv, kc, vc, tab, ovt, B, S)
    def keys_padded(col):
        k = qkv[:, col:col + dkv].reshape(B, S, dkv)
        return jnp.pad(k, ((0, 0), (WINDOW, 0), (0, 0))).reshape(B * (WINDOW + S), dkv)

    def values_t_padded(col):
        v = qkv[:, col:col + dkv].reshape(B, S, dkv).transpose(0, 2, 1)
        return jnp.pad(v, ((0, 0), (0, 0), (WINDOW, 0))).reshape(B * dkv, WINDOW + S)

    attn = sel_win(qkv, selb, keys_padded(dq + 2 * dkv), values_t_padded(dq + 3 * dkv),
                   keys_padded(dq + 4 * dkv), values_t_padded(dq + 5 * dkv), e2p, tab, ocmp, gl,
                   attn_gnw.reshape(1, dq), B, S)
    cw8 = jnp.pad(conv_w, ((0, 8 - conv_w.shape[0]), (0, 0)))
    conv = conv_mixer(gc, cw8, conv_b.reshape(1, dc), conv_gnw.reshape(1, dc), S, tm_conv)
    return attn, conv


def peer_block(h, ffn_nw, final_nw, peer_wq, peer_subkeys, peer_u, peer_v, tm=1024, tt=512):
    T, D = h.shape
    qp = norm_matmul(h, ffn_nw, peer_wq.astype(BF16), BF16, tm, 1024)
    sk = peer_subkeys.reshape(PEER_HEADS * 2, N_KEYS, peer_subkeys.shape[-1]).astype(BF16)
    ex, gates, u_bf, v_bf = peer_route(qp, sk, peer_u, peer_v)
    gm = peer_gbuild(ex, gates)
    return peer_dense(h, ffn_nw, final_nw, u_bf, v_bf, gm, tt)


def kernel(x, attn_norm_w, w_in, w_cmp_k, w_cmp_v, cmp_pos, conv_w, conv_b, attn_group_norm_w,
           conv_group_norm_w, w_out, rel_bias, ffn_norm_w, peer_wq, peer_subkeys, peer_u, peer_v,
           final_norm_w):
    B, S, D = x.shape
    T = B * S
    xt = x.reshape(T, D)
    attn, conv = nsa_conv_mix(xt, B, S, attn_norm_w[0], w_in[0], w_cmp_k[0], w_cmp_v[0], cmp_pos[0],
                              conv_w[0], conv_b[0], attn_group_norm_w[0], conv_group_norm_w[0], rel_bias)
    da = attn.shape[1]
    wo = w_out[0].astype(BF16)
    h = out_proj(attn, conv, wo[:da], wo[da:], xt, 1024, 1024)
    out = peer_block(h, ffn_norm_w[0], final_norm_w, peer_wq[0], peer_subkeys[0], peer_u[0], peer_v[0])
    return out.reshape(B, S, D)
```

```python
import functools
import math

import jax
import jax.numpy as jnp
import numpy as np
from jax import lax
from jax.experimental import pallas as pl
from jax.experimental.pallas import tpu as pltpu

F32 = jnp.float32
BF16 = jnp.bfloat16

HEAD_DIM = 128
N_KV_HEADS = 2
GQA = 4
N_Q_HEADS = N_KV_HEADS * GQA
L_CMP = 32
STRIDE_CMP = 16
L_SEL = 64
N_SEL = 16
WINDOW = 512
Q_BLOCK = 128
FORCED_SCORE = float(GQA + 1)
NUM_BUCKETS = 32
MAX_DISTANCE = 128
PEER_HEADS = 8
N_KEYS = 128
PEER_TOPK = 16
EPS = 1e-6
NEG_BIG = -1e30
SEL_OFF = -float(2 ** 30)
LOG2E = math.log2(math.e)
N_FIXED_A = 4

SUBLANES = 8
LANES = 128
MXU_DIM = 256
V7X_VMEM_BYTES = 64 * 1024 * 1024
VMEM_LIMIT = V7X_VMEM_BYTES * 7 // 8

HALF_KEYS = N_KEYS // 2
PACK_ROWS = SUBLANES

TM_PROJ = 4 * MXU_DIM
TN_OUT = 4 * MXU_DIM
TM_CONV = 2 * MXU_DIM
TQ_CMP = 4 * Q_BLOCK
TQ_SEL = 2 * Q_BLOCK
TT_ROUTE = LANES
TT_DENSE = 2 * MXU_DIM
CAST_ROWS = MXU_DIM


def _dot(a, b):
    return jnp.dot(a, b, preferred_element_type=F32)


def _dot_nt(a, b):
    return lax.dot_general(a, b, (((1,), (1,)), ((), ())), preferred_element_type=F32)


def _params(sem, vmem=VMEM_LIMIT):
    return pltpu.CompilerParams(dimension_semantics=sem, vmem_limit_bytes=vmem)


def _norm_matmul_kernel(x_ref, nw_ref, w_ref, *rest, w_is_nk, has_side):
    if has_side:
        ws_ref, o_ref, os_ref, xn_ref = rest
    else:
        o_ref, xn_ref = rest

    @pl.when(pl.program_id(1) == 0)
    def _():
        x = x_ref[...]
        y = x * lax.rsqrt(jnp.mean(x * x, axis=-1, keepdims=True) + EPS)
        xn_ref[...] = (y * nw_ref[...]).astype(BF16)
        if has_side:
            os_ref[...] = _dot_nt(xn_ref[...], ws_ref[...])

    dot = _dot_nt if w_is_nk else _dot
    o_ref[...] = dot(xn_ref[...], w_ref[...]).astype(o_ref.dtype)


def norm_matmul(x, norm_w, w, out_dtype, tm, tn, w_is_nk=False, w_side=None):
    T, D = x.shape
    N = w.shape[0] if w_is_nk else w.shape[1]
    w_spec = pl.BlockSpec((tn, D), lambda i, j: (j, 0)) if w_is_nk else pl.BlockSpec((D, tn), lambda i, j: (0, j))
    in_specs = [pl.BlockSpec((tm, D), lambda i, j: (i, 0)), pl.BlockSpec((1, D), lambda i, j: (0, 0)), w_spec]
    out_specs = [pl.BlockSpec((tm, tn), lambda i, j: (i, j))]
    out_shape = [jax.ShapeDtypeStruct((T, N), out_dtype)]
    args = [x, norm_w.reshape(1, D), w]
    if w_side is not None:
        n_side = w_side.shape[0]
        in_specs.append(pl.BlockSpec((n_side, D), lambda i, j: (0, 0)))
        out_specs.append(pl.BlockSpec((tm, n_side), lambda i, j: (i, 0)))
        out_shape.append(jax.ShapeDtypeStruct((T, n_side), F32))
        args.append(w_side)
    out = pl.pallas_call(
        functools.partial(_norm_matmul_kernel, w_is_nk=w_is_nk, has_side=w_side is not None),
        grid=(T // tm, N // tn),
        in_specs=in_specs,
        out_specs=out_specs,
        out_shape=out_shape,
        scratch_shapes=[pltpu.VMEM((tm, D), BF16)],
        compiler_params=_params(("parallel", "arbitrary")),
    )(*args)
    return out if w_side is not None else out[0]


def _cast_rows_kernel(w_ref, o_ref):
    o_ref[...] = w_ref[...].astype(o_ref.dtype)


def cast_rows(w, row0, n_rows, tr=CAST_ROWS):
    K = w.shape[1]
    assert row0 % SUBLANES == 0 and n_rows % tr == 0
    return pl.pallas_call(
        _cast_rows_kernel,
        grid=(n_rows // tr,),
        in_specs=[pl.BlockSpec((pl.Element(tr), pl.Element(K)), lambda i: (pl.multiple_of(row0 + i * tr, SUBLANES), 0))],
        out_specs=pl.BlockSpec((tr, K), lambda i: (i, 0)),
        out_shape=jax.ShapeDtypeStruct((n_rows, K), BF16),
        compiler_params=_params(("parallel",)),
    )(w)


def _compress_kernel(k_ref, v_ref, pa_ref, pb_ref, wka_ref, wkb_ref, wva_ref, wvb_ref, kc_ref, vc_ref):
    def one(x_ref, wa_ref, wb_ref, o_ref):
        x = x_ref[...].astype(F32)
        a = _dot((x + pa_ref[...]).astype(BF16), wa_ref[...])
        b = _dot((x + pb_ref[...]).astype(BF16), wb_ref[...])
        n = b.shape[0]
        o_ref[...] = (a + pltpu.roll(b, n - 1, 0)).astype(o_ref.dtype)

    one(k_ref, wka_ref, wkb_ref, kc_ref)
    one(v_ref, wva_ref, wvb_ref, vc_ref)


def compress(k16, v16, pos_a, pos_b, wka, wkb, wva, wvb, B):
    R, C = k16.shape
    nb = R // B
    dkv = wka.shape[1]
    full = lambda shp: pl.BlockSpec(shp, lambda b: (0, 0))
    return pl.pallas_call(
        _compress_kernel,
        grid=(B,),
        in_specs=[pl.BlockSpec((nb, C), lambda b: (b, 0)),
                  pl.BlockSpec((nb, C), lambda b: (b, 0)),
                  full((1, C)), full((1, C)),
                  full((C, dkv)), full((C, dkv)), full((C, dkv)), full((C, dkv))],
        out_specs=[pl.BlockSpec((nb, dkv), lambda b: (b, 0))] * 2,
        out_shape=[jax.ShapeDtypeStruct((R, dkv), BF16)] * 2,
        compiler_params=_params(("parallel",)),
    )(k16, v16, pos_a, pos_b, wka, wkb, wva, wvb)


def _bias_from_dist(tab_row, dist):
    idx = jnp.clip(dist, 0, 127)
    w = tab_row.shape[1]
    tab = jnp.broadcast_to(tab_row, (idx.shape[0], w))
    parts = [jnp.take_along_axis(tab, idx[:, k:k + w], axis=1) for k in range(0, idx.shape[1], w)]
    return parts[0] if len(parts) == 1 else jnp.concatenate(parts, axis=1)


def _cmp_select_kernel(q_ref, kc_ref, vc_ref, tab_ref, ovt_ref, ocmp_ref, selb_ref, *, n_qb, scale):
    c = pl.program_id(0) % n_qb
    tq = q_ref.shape[0]
    n_c = kc_ref.shape[0]
    n_sel = ovt_ref.shape[0]
    t0 = c * tq
    t_col = t0 + lax.broadcasted_iota(jnp.int32, (tq, n_c), 0)
    n_row = lax.broadcasted_iota(jnp.int32, (tq, n_c), 1)
    dist = t_col - (n_row * STRIDE_CMP + (L_CMP - 1))
    valid = dist >= 0

    j_io = lax.broadcasted_iota(jnp.int32, (n_sel, tq), 0)
    t_io = t0 + lax.broadcasted_iota(jnp.int32, (n_sel, tq), 1)
    blk_t = t_io // L_SEL
    forced = (j_io == 0) | (j_io == blk_t) | (j_io == blk_t - 1)
    causal_blk = j_io * L_SEL <= t_io

    for h in range(N_KV_HEADS):
        kc = kc_ref[:, h * HEAD_DIM:(h + 1) * HEAD_DIM]
        vc = vc_ref[:, h * HEAD_DIM:(h + 1) * HEAD_DIM]
        psum = jnp.zeros((tq, n_c), F32)
        for g in range(GQA):
            hd = h * GQA + g
            qh = q_ref[:, hd * HEAD_DIM:(hd + 1) * HEAD_DIM]
            bias = _bias_from_dist(tab_ref[hd:hd + 1, :], dist)
            s = _dot_nt(qh, kc) * scale + bias
            s = jnp.where(valid, s, NEG_BIG)
            m = jnp.max(s, axis=-1, keepdims=True)
            e = jnp.where(valid, jnp.exp(s - m), 0.0)
            d = jnp.sum(e, axis=-1, keepdims=True)
            p = e / jnp.where(d > 0, d, 1.0)
            ocmp_ref[:, hd * HEAD_DIM:(hd + 1) * HEAD_DIM] = _dot(p.astype(BF16), vc)
            psum = psum + p
        p_hi = psum.astype(BF16)
        p_lo = (psum - p_hi.astype(F32)).astype(BF16)
        ovt = ovt_ref[...]
        imp = _dot_nt(ovt, p_hi) + _dot_nt(ovt, p_lo)
        score = jnp.where(forced, FORCED_SCORE, jnp.where(causal_blk, imp, -1.0))
        rank = jnp.zeros((n_sel, tq), F32)
        for jp in range(n_sel):
            row = score[jp:jp + 1, :]
            rank = rank + jnp.where(j_io > jp, jnp.where(row >= score, 1.0, 0.0),
                                    jnp.where(row > score, 1.0, 0.0))
        selb = jnp.where(rank < float(N_SEL), 0.0, SEL_OFF)
        if n_sel < HEAD_DIM:
            selb = jnp.concatenate([selb, jnp.zeros((HEAD_DIM - n_sel, tq), F32)], axis=0)
        selb_ref[:, h * HEAD_DIM:(h + 1) * HEAD_DIM] = selb.T.astype(BF16)


def cmp_select(qkv, kc, vc, tab, ovt, B, S, tq=TQ_CMP):
    T = qkv.shape[0]
    n_qb = S // tq
    n_c = kc.shape[0] // B
    dq = N_Q_HEADS * HEAD_DIM
    dkv = N_KV_HEADS * HEAD_DIM
    kern = functools.partial(_cmp_select_kernel, n_qb=n_qb, scale=HEAD_DIM ** -0.5)
    return pl.pallas_call(
        kern,
        grid=(T // tq,),
        in_specs=[pl.BlockSpec((tq, dq), lambda i: (i, 0)),
                  pl.BlockSpec((n_c, dkv), lambda i: (i // n_qb, 0)),
                  pl.BlockSpec((n_c, dkv), lambda i: (i // n_qb, 0)),
                  pl.BlockSpec(tab.shape, lambda i: (0, 0)),
                  pl.BlockSpec(ovt.shape, lambda i: (0, 0))],
        out_specs=[pl.BlockSpec((tq, dq), lambda i: (i, 0)),
                   pl.BlockSpec((tq, dkv), lambda i: (i, 0))],
        out_shape=[jax.ShapeDtypeStruct((T, dq), F32),
                   jax.ShapeDtypeStruct((T, dkv), BF16)],
        compiler_params=_params(("parallel",)),
    )(qkv, kc, vc, tab, ovt)


def _flash_update(s_raw, vt, bias, m_ref, l_ref, acc_ref, scale):
    s = s_raw * scale + bias
    m_old = m_ref[...]
    m_new = jnp.maximum(m_old, jnp.max(s, axis=0, keepdims=True))
    alpha = jnp.exp2(m_old - m_new)
    p = jnp.exp2(s - m_new)
    l_ref[...] = alpha * l_ref[...] + jnp.sum(p, axis=0, keepdims=True)
    acc_ref[...] = alpha * acc_ref[...] + _dot(vt, p.astype(BF16))
    m_ref[...] = m_new


def _flash_chunk(k, vt, q, bias, m_ref, l_ref, acc_ref, scale):
    _flash_update(_dot_nt(k, q), vt, bias, m_ref, l_ref, acc_ref, scale)


def _sel_win_kernel(q_ref, selb_ref, ks_ref, vst_ref, kw_ref, vwt_ref, e2_ref, tab_ref, ocmp_ref,
                    gl_ref, gnw_ref, o_ref, m_ref, l_ref, acc_ref, sa_ref, sb_ref, *, n_qb, scale):
    c = pl.program_id(0) % n_qb
    tq = q_ref.shape[0]
    rows = GQA * tq
    n_back = WINDOW // tq
    near = WINDOW + tq
    j_io = lax.broadcasted_iota(jnp.int32, (tq, tq), 0)
    i_io = lax.broadcasted_iota(jnp.int32, (tq, tq), 1)
    dij = i_io - j_io
    dij4 = jnp.concatenate([dij] * GQA, axis=1)
    causal = dij4 >= 0
    pad_col = jnp.where(lax.broadcasted_iota(jnp.int32, (rows, HEAD_DIM), 1) == HEAD_DIM - 1, 1.0, 0.0).astype(BF16)
    sig = jax.nn.sigmoid(gl_ref[...])
    near0 = pl.multiple_of(c * tq, tq)
    n_far = jnp.maximum(c - n_back, 0)
    n_full = n_far // n_back
    n_rem = n_far - n_full * n_back

    def hsl(h):
        return slice(h * HEAD_DIM, (h + 1) * HEAD_DIM)

    def far0(i):
        return pl.multiple_of(WINDOW + i * WINDOW, WINDOW)

    def reset():
        m_ref[...] = jnp.full(m_ref.shape, NEG_BIG, F32)
        l_ref[...] = jnp.zeros(l_ref.shape, F32)
        acc_ref[...] = jnp.zeros(acc_ref.shape, F32)

    def result():
        ot = acc_ref[...] / l_ref[...]
        return [ot[:, g * tq:(g + 1) * tq].T for g in range(GQA)]

    for h in range(N_KV_HEADS):
        q4 = jnp.concatenate([q_ref[:, (h * GQA + g) * HEAD_DIM:(h * GQA + g + 1) * HEAD_DIM]
                              for g in range(GQA)], axis=0)
        sb4 = jnp.concatenate([selb_ref[:, hsl(h)]] * GQA, axis=0)
        q_aug = jnp.concatenate([q4, sb4 + pad_col], axis=1)
        q_win = jnp.concatenate([q4, pad_col], axis=1)
        tabs = [tab_ref[h * GQA + g:h * GQA + g + 1, :] * LOG2E for g in range(GQA)]
        d0 = jnp.concatenate([_bias_from_dist(t, dij) for t in tabs], axis=1)
        d1 = jnp.concatenate([_bias_from_dist(t, dij + tq) for t in tabs], axis=1)
        far = jnp.concatenate([t[:, HEAD_DIM - 1:HEAD_DIM] + jnp.zeros((1, tq), F32) for t in tabs], axis=1)
        far_t = jnp.broadcast_to(far, (tq, rows))
        diag = jnp.where(causal, d0, NEG_BIG)
        bias_sel = jnp.concatenate([far_t] * (n_back - 1) + [d1, diag], axis=0)
        bias_win = jnp.concatenate([jnp.where(dij4 < 0, far_t, NEG_BIG)] + [far_t] * (n_back - 2) + [d1, diag],
                                   axis=0)

        reset()
        e2_near = e2_ref[pl.ds(near0, near), :]
        k_near = jnp.concatenate([ks_ref[pl.ds(near0, near), hsl(h)], e2_near], axis=1)
        _flash_chunk(k_near, vst_ref[hsl(h), pl.ds(near0, near)], q_aug, bias_sel, m_ref, l_ref, acc_ref, scale)

        def far_scores(i):
            r0 = far0(i)
            k = jnp.concatenate([ks_ref[pl.ds(r0, WINDOW), hsl(h)], e2_ref[pl.ds(r0, WINDOW), :]], axis=1)
            return _dot_nt(k, q_aug)

        def far_update(s_raw, i, bias):
            _flash_update(s_raw, vst_ref[hsl(h), pl.ds(far0(i), WINDOW)], bias, m_ref, l_ref, acc_ref, scale)

        sa_ref[...] = far_scores(0)

        def body(i, carry):
            sb_ref[...] = far_scores(2 * i + 1)
            far_update(sa_ref[...], 2 * i, far)
            sa_ref[...] = far_scores(2 * i + 2)
            far_update(sb_ref[...], 2 * i + 1, far)
            return carry

        lax.fori_loop(0, n_full // 2, body, 0)

        @pl.when(n_full % 2 == 1)
        def _():
            far_update(sa_ref[...], n_full - 1, far)

        @pl.when(n_rem > 0)
        def _():
            if n_back == 2:
                r0 = far0(n_full)
                k = jnp.concatenate([ks_ref[pl.ds(r0, tq), hsl(h)], e2_ref[pl.ds(r0, tq), :]], axis=1)
                _flash_chunk(k, vst_ref[hsl(h), pl.ds(r0, tq)], q_aug, far, m_ref, l_ref, acc_ref, scale)
            else:
                live = lax.broadcasted_iota(jnp.int32, (WINDOW, rows), 0) < n_rem * tq
                far_update(far_scores(n_full), n_full,
                           jnp.where(live, jnp.broadcast_to(far, (WINDOW, rows)), NEG_BIG))

        o_sel = result()

        reset()
        kw_near = jnp.concatenate([kw_ref[pl.ds(near0, near), hsl(h)], e2_near], axis=1)
        _flash_chunk(kw_near, vwt_ref[hsl(h), pl.ds(near0, near)], q_win, bias_win, m_ref, l_ref, acc_ref, scale)
        o_win = result()

        for g in range(GQA):
            hd = h * GQA + g
            o = (sig[:, 3 * hd:3 * hd + 1] * ocmp_ref[:, hsl(hd)]
                 + sig[:, 3 * hd + 1:3 * hd + 2] * o_sel[g]
                 + sig[:, 3 * hd + 2:3 * hd + 3] * o_win[g])
            y = o * lax.rsqrt(jnp.mean(o * o, axis=-1, keepdims=True) + EPS)
            o_ref[:, hsl(hd)] = (y * gnw_ref[:, hsl(hd)]).astype(o_ref.dtype)


def sel_win(qkv, selb, ksp, vstp, kwp, vwtp, e2p, tab, ocmp, gl, gnw, B, S, tq=TQ_SEL):
    T = qkv.shape[0]
    n_qb = S // tq
    dq = N_Q_HEADS * HEAD_DIM
    dkv = N_KV_HEADS * HEAD_DIM
    SP = S + WINDOW
    kern = functools.partial(_sel_win_kernel, n_qb=n_qb, scale=HEAD_DIM ** -0.5 * LOG2E)
    k_spec = pl.BlockSpec((SP, dkv), lambda i: (i // n_qb, 0))
    vt_spec = pl.BlockSpec((dkv, SP), lambda i: (i // n_qb, 0))
    return pl.pallas_call(
        kern,
        grid=(T // tq,),
        in_specs=[pl.BlockSpec((tq, dq), lambda i: (i, 0)),
                  pl.BlockSpec((tq, dkv), lambda i: (i, 0)),
                  k_spec, vt_spec, k_spec, vt_spec,
                  pl.BlockSpec((SP, HEAD_DIM), lambda i: (0, 0)),
                  pl.BlockSpec(tab.shape, lambda i: (0, 0)),
                  pl.BlockSpec((tq, dq), lambda i: (i, 0)),
                  pl.BlockSpec((tq, HEAD_DIM), lambda i: (i, 0)),
                  pl.BlockSpec((1, dq), lambda i: (0, 0))],
        out_specs=pl.BlockSpec((tq, dq), lambda i: (i, 0)),
        out_shape=jax.ShapeDtypeStruct((T, dq), BF16),
        scratch_shapes=[pltpu.VMEM((1, GQA * tq), F32), pltpu.VMEM((1, GQA * tq), F32),
                        pltpu.VMEM((HEAD_DIM, GQA * tq), F32),
                        pltpu.VMEM((WINDOW, GQA * tq), F32), pltpu.VMEM((WINDOW, GQA * tq), F32)],
        compiler_params=_params(("parallel",)),
    )(qkv, selb, ksp, vstp, kwp, vwtp, e2p, tab, ocmp, gl, gnw)


def _conv_kernel(b_ref, c_ref, h_ref, cp_ref, hp_ref, cw_ref, cb_ref, gnw_ref, o_ref, u_ref, *, tiles_per_seq):
    tm = b_ref.shape[0]
    first = (pl.program_id(0) % tiles_per_seq) == 0
    u_prev = cp_ref[...] * hp_ref[...]
    u_ref[0:8, :] = jnp.where(first, 0.0, u_prev)
    u = c_ref[...] * h_ref[...]
    u_ref[8:8 + tm, :] = u
    y = (cw_ref[0:1, :] * u_ref[6:6 + tm, :] + cw_ref[1:2, :] * u_ref[7:7 + tm, :]
         + cw_ref[2:3, :] * u + cb_ref[...])
    o = b_ref[...] * y
    n_groups = o.shape[1] // HEAD_DIM
    for g in range(n_groups):
        sl = slice(g * HEAD_DIM, (g + 1) * HEAD_DIM)
        og = o[:, sl]
        yg = og * lax.rsqrt(jnp.mean(og * og, axis=-1, keepdims=True) + EPS)
        o_ref[:, sl] = (yg * gnw_ref[:, sl]).astype(o_ref.dtype)


def conv_mixer(gc, conv_w, conv_b, gnw, S, tm):
    T = gc.shape[0]
    dc = conv_w.shape[1]
    tps = S // tm
    kern = functools.partial(_conv_kernel, tiles_per_seq=tps)
    prev = lambda col: pl.BlockSpec((8, dc), lambda i, col=col: (jnp.maximum(i * (tm // 8) - 1, 0), col))
    cur = lambda col: pl.BlockSpec((tm, dc), lambda i, col=col: (i, col))
    return pl.pallas_call(
        kern,
        grid=(T // tm,),
        in_specs=[cur(0), cur(1), cur(2), prev(1), prev(2),
                  pl.BlockSpec((8, dc), lambda i: (0, 0)),
                  pl.BlockSpec((1, dc), lambda i: (0, 0)),
                  pl.BlockSpec((1, dc), lambda i: (0, 0))],
        out_specs=pl.BlockSpec((tm, dc), lambda i: (i, 0)),
        out_shape=jax.ShapeDtypeStruct((T, dc), BF16),
        scratch_shapes=[pltpu.VMEM((tm + 8, dc), F32)],
        compiler_params=_params(("parallel",)),
    )(gc, gc, gc, gc, gc, conv_w, conv_b, gnw)


def _out_proj_kernel(ma_ref, mc_ref, wa_ref, wc_ref, x_ref, o_ref):
    o_ref[...] = x_ref[...] + _dot(ma_ref[...], wa_ref[...]) + _dot(mc_ref[...], wc_ref[...])


def out_proj(ma, mc, wa, wc, x, tm, tn):
    T, da = ma.shape
    dc = mc.shape[1]
    D = x.shape[1]
    return pl.pallas_call(
        _out_proj_kernel,
        grid=(T // tm, D // tn),
        in_specs=[pl.BlockSpec((tm, da), lambda i, j: (i, 0)),
                  pl.BlockSpec((tm, dc), lambda i, j: (i, 0)),
                  pl.BlockSpec((da, tn), lambda i, j: (0, j)),
                  pl.BlockSpec((dc, tn), lambda i, j: (0, j)),
                  pl.BlockSpec((tm, tn), lambda i, j: (i, j))],
        out_specs=pl.BlockSpec((tm, tn), lambda i, j: (i, j)),
        out_shape=jax.ShapeDtypeStruct((T, D), F32),
        compiler_params=_params(("parallel", "arbitrary")),
    )(ma, mc, wa, wc, x)


def _sort_network(n):
    pairs = []
    p = 1
    while p < n:
        k = p
        while k >= 1:
            for j in range(k % p, n - k, 2 * k):
                for i in range(min(k, n - j - k)):
                    if (i + j) // (2 * p) == (i + j + k) // (2 * p):
                        pairs.append((i + j, i + j + k))
            k //= 2
        p *= 2
    return pairs


def _peer_route_kernel(q_ref, sk_ref, u_ref, v_ref, e_ref, g_ref, ub_ref, vb_ref, sv_ref, si_ref):
    ub_ref[...] = u_ref[...].astype(BF16)
    vb_ref[...] = v_ref[...].astype(BF16)
    tt = q_ref.shape[0]
    K = PEER_TOPK
    sub = SUBLANES
    n_io = lax.broadcasted_iota(jnp.int32, (N_KEYS, tt), 0).astype(F32)
    r8 = lax.broadcasted_iota(jnp.int32, (sub, tt), 0).astype(F32)
    ninf = -jnp.inf
    lens = [K // (a + 1) for a in range(N_FIXED_A)]
    lens += [max(K // (b + 1) - N_FIXED_A, 0) for b in range(sub - N_FIXED_A)]
    assert sum(lens) == sum(K // (a + 1) for a in range(K)) and lens[-1] == 0
    fixed_a = r8 < float(N_FIXED_A)
    list_len = jnp.zeros((sub, tt), F32)
    for row, n in enumerate(lens):
        list_len = jnp.where(r8 == float(row), float(n), list_len)

    for h in range(PEER_HEADS):
        for c in range(2):
            col = (h * 2 + c) * HEAD_DIM
            s = _dot_nt(sk_ref[h * 2 + c], q_ref[:, col:col + HEAD_DIM])
            n_col = N_KEYS // sub
            vals = [s[j * sub:(j + 1) * sub, :] for j in range(n_col)]
            idxs = [n_io[j * sub:(j + 1) * sub, :] for j in range(n_col)]
            for lo, hi in _sort_network(n_col):
                swap = (vals[hi] > vals[lo]) | ((vals[hi] == vals[lo]) & (idxs[hi] < idxs[lo]))
                vals[lo], vals[hi] = jnp.where(swap, vals[hi], vals[lo]), jnp.where(swap, vals[lo], vals[hi])
                idxs[lo], idxs[hi] = jnp.where(swap, idxs[hi], idxs[lo]), jnp.where(swap, idxs[lo], idxs[hi])
            for k in range(K):
                m = jnp.max(vals[0], axis=0, keepdims=True)
                idx = jnp.min(jnp.where(vals[0] == m, idxs[0], float(N_KEYS)), axis=0, keepdims=True)
                sv_ref[c, k:k + 1, :] = m
                si_ref[c, k:k + 1, :] = idx
                win = idxs[0] == idx
                for j in range(min(n_col, K) - 1 - k):
                    vals[j] = jnp.where(win, vals[j + 1], vals[j])
                    idxs[j] = jnp.where(win, idxs[j + 1], idxs[j])
        sv0, sv1 = sv_ref[0], sv_ref[1]
        si0, si1 = si_ref[0], si_ref[1]
        sv1_low = pltpu.roll(sv1[0:sub, :], N_FIXED_A, 0)
        sv0_top = sv0[0:sub, :]
        lv, lf = [], []
        for dep in range(K):
            a_dep = min(N_FIXED_A + dep, K - 1)
            val = (jnp.where(fixed_a, sv0_top, sv0[a_dep:a_dep + 1, :])
                   + jnp.where(fixed_a, sv1[dep:dep + 1, :], sv1_low))
            lv.append(jnp.where(list_len > dep, val, ninf))
            lf.append(jnp.where(fixed_a, r8 * float(K) + dep, (N_FIXED_A + dep) * float(K) + r8 - N_FIXED_A))
        cvs, fls = [], []
        for k in range(K):
            m = jnp.max(lv[0], axis=0, keepdims=True)
            fsel = jnp.min(jnp.where(lv[0] == m, lf[0], 1e9), axis=0, keepdims=True)
            win = lf[0] == fsel
            cvs.append(m)
            fls.append(fsel)
            for dep in range(K - 1 - k):
                lv[dep] = jnp.where(win, lv[dep + 1], lv[dep])
                lf[dep] = jnp.where(win, lf[dep + 1], lf[dep])
        cv = jnp.concatenate(cvs, axis=0)
        fl = jnp.concatenate(fls, axis=0)
        a_sel = jnp.floor(fl * (1.0 / K))
        b_sel = fl - a_sel * K
        i1 = jnp.zeros_like(fl)
        i2 = jnp.zeros_like(fl)
        for r in range(K):
            i1 = jnp.where(a_sel == r, si0[r:r + 1, :], i1)
            i2 = jnp.where(b_sel == r, si1[r:r + 1, :], i2)
        ex = i1 * float(N_KEYS) + i2
        ev = jnp.exp(cv - jnp.max(cv, axis=0, keepdims=True))
        gates = ev / jnp.sum(ev, axis=0, keepdims=True)
        e_ref[0, h * K:(h + 1) * K, :] = ex.astype(jnp.int32)
        g_ref[0, h * K:(h + 1) * K, :] = gates


def peer_route(qp, subkeys, u, v):
    T = qp.shape[0]
    tt = TT_ROUTE
    P = PEER_HEADS * PEER_TOPK
    nt = T // tt
    E, D = u.shape
    slab = E // nt
    assert slab * nt == E and slab % 16 == 0
    tab_spec = pl.BlockSpec((slab, D), lambda i: (i, 0))
    return pl.pallas_call(
        _peer_route_kernel,
        grid=(nt,),
        in_specs=[pl.BlockSpec((tt, qp.shape[1]), lambda i: (i, 0)),
                  pl.BlockSpec(subkeys.shape, lambda i: (0, 0, 0)),
                  tab_spec, tab_spec],
        out_specs=[pl.BlockSpec((1, P, tt), lambda i: (i, 0, 0))] * 2 + [tab_spec, tab_spec],
        out_shape=[jax.ShapeDtypeStruct((nt, P, tt), jnp.int32),
                   jax.ShapeDtypeStruct((nt, P, tt), F32),
                   jax.ShapeDtypeStruct((E, D), BF16), jax.ShapeDtypeStruct((E, D), BF16)],
        scratch_shapes=[pltpu.VMEM((2, PEER_TOPK, tt), F32), pltpu.VMEM((2, PEER_TOPK, tt), F32)],
        compiler_params=_params(("parallel",)),
    )(qp, subkeys, u, v)


def _peer_gbuild_kernel(e_ref, g_ref, o_ref, i1_ref, i2_ref, gt_ref):
    tt = e_ref.shape[2]
    e = e_ref[0].T
    i1_ref[...] = e >> 7
    i2_ref[...] = e & (N_KEYS - 1)
    gt_ref[...] = g_ref[0].T
    P = e.shape[1]
    k_io = lax.broadcasted_iota(jnp.int32, (N_KEYS, P), 0)
    group = 64

    def body(tg, carry):
        base = pl.multiple_of(tg * group, group)
        r1s = i1_ref[pl.ds(base, group), :]
        r2s = i2_ref[pl.ds(base, group), :]
        rgs = gt_ref[pl.ds(base, group), :]
        for u in range(group):
            lhs = jnp.where(k_io == r1s[u:u + 1, :], rgs[u:u + 1, :], 0.0).astype(BF16)
            rhs = jnp.where(k_io == r2s[u:u + 1, :], 1.0, 0.0).astype(BF16)
            g_t = _dot_nt(lhs, rhs).astype(BF16).astype(F32)
            lo = pltpu.bitcast(g_t[:HALF_KEYS], jnp.uint32) >> 16
            hi = pltpu.bitcast(g_t[HALF_KEYS:], jnp.uint32)
            words = hi | lo
            for a in range(HALF_KEYS // PACK_ROWS):
                o_ref[a, base + u] = words[a * PACK_ROWS:(a + 1) * PACK_ROWS, :]
        return carry

    lax.fori_loop(0, tt // group, body, 0)


def peer_gbuild(ex, gates):
    nt, P, tt = ex.shape
    T = nt * tt
    n_a = HALF_KEYS // PACK_ROWS
    return pl.pallas_call(
        _peer_gbuild_kernel,
        grid=(nt,),
        in_specs=[pl.BlockSpec((1, P, tt), lambda i: (i, 0, 0))] * 2,
        out_specs=pl.BlockSpec((n_a, tt, PACK_ROWS, N_KEYS), lambda i: (0, i, 0, 0)),
        out_shape=jax.ShapeDtypeStruct((n_a, T, PACK_ROWS, N_KEYS), jnp.uint32),
        scratch_shapes=[pltpu.VMEM((tt, P), jnp.int32), pltpu.VMEM((tt, P), jnp.int32),
                        pltpu.VMEM((tt, P), F32)],
        compiler_params=_params(("parallel",)),
    )(ex, gates)


def _peer_dense_kernel(h_ref, nw_ref, fw_ref, u_ref, v_ref, gm_ref, o_ref, hn_ref, acc_ref):
    j = pl.program_id(1)
    tt = h_ref.shape[0]

    @pl.when(j == 0)
    def _():
        x = h_ref[...]
        y = x * lax.rsqrt(jnp.mean(x * x, axis=-1, keepdims=True) + EPS)
        hn_ref[...] = (y * nw_ref[...]).astype(BF16)
        acc_ref[...] = jnp.zeros(acc_ref.shape, F32)

    a = _dot_nt(hn_ref[...], u_ref[...])
    shift = (16 * (1 - j % 2)).astype(jnp.uint32)
    gm = jnp.concatenate(
        [pltpu.bitcast((gm_ref[pl.ds(k, tt, stride=PACK_ROWS), :] << shift) & jnp.uint32(0xFFFF0000), F32)
         for k in range(PACK_ROWS)], axis=1)
    w = (gm * jax.nn.gelu(a)).astype(BF16)
    acc_ref[...] += _dot(w, v_ref[...])

    @pl.when(j == pl.num_programs(1) - 1)
    def _():
        x = h_ref[...] + acc_ref[...]
        y = x * lax.rsqrt(jnp.mean(x * x, axis=-1, keepdims=True) + EPS)
        o_ref[...] = y * fw_ref[...]


def peer_dense(h, ffn_nw, final_nw, u, v, gm_words, tt):
    T, D = h.shape
    E = u.shape[0]
    ec = PACK_ROWS * N_KEYS
    n_a = gm_words.shape[0]
    assert E == 2 * n_a * ec
    gm2 = gm_words.reshape(n_a, T * PACK_ROWS, N_KEYS)
    tab_spec = pl.BlockSpec((ec, D), lambda i, j: ((j % 2) * n_a + j // 2, 0))
    return pl.pallas_call(
        _peer_dense_kernel,
        grid=(T // tt, E // ec),
        in_specs=[pl.BlockSpec((tt, D), lambda i, j: (i, 0)),
                  pl.BlockSpec((1, D), lambda i, j: (0, 0)),
                  pl.BlockSpec((1, D), lambda i, j: (0, 0)),
                  tab_spec, tab_spec,
                  pl.BlockSpec((None, tt * PACK_ROWS, N_KEYS), lambda i, j: (j // 2, i, 0))],
        out_specs=pl.BlockSpec((tt, D), lambda i, j: (i, 0)),
        out_shape=jax.ShapeDtypeStruct((T, D), F32),
        scratch_shapes=[pltpu.VMEM((tt, D), BF16), pltpu.VMEM((tt, D), F32)],
        compiler_params=_params(("parallel", "arbitrary")),
    )(h, ffn_nw.reshape(1, D), final_nw.reshape(1, D), u, v, gm2)


def _t5_bucket_np(n_dist):
    d = np.arange(n_dist)
    max_exact = NUM_BUCKETS // 2
    nf = np.maximum(d, 1).astype(np.float64)
    large = max_exact + (np.log(nf / max_exact) / math.log(MAX_DISTANCE / max_exact)
                         * (NUM_BUCKETS - max_exact)).astype(np.int64)
    large = np.minimum(large, NUM_BUCKETS - 1)
    return np.where(d < max_exact, d, large).astype(np.int32)


def _overlap_t_np(S):
    n_c = (S - L_CMP) // STRIDE_CMP + 1
    n_sel = S // L_SEL
    pos = np.arange(n_c)[:, None] * STRIDE_CMP + np.arange(L_CMP)[None, :]
    m = np.zeros((n_c + 1, n_sel), np.float32)
    np.add.at(m, (np.repeat(np.arange(n_c), L_CMP), (pos // L_SEL).reshape(-1)), 1.0 / L_CMP)
    return np.ascontiguousarray(m.T)


def _block_onehot_padded_np(S):
    assert S // L_SEL < HEAD_DIM
    e2 = np.zeros((WINDOW + S, HEAD_DIM), np.float32)
    e2[WINDOW + np.arange(S), np.arange(S) // L_SEL] = 1.0
    e2[:WINDOW, HEAD_DIM - 1] = SEL_OFF
    return e2


def nsa_conv_mix(xt, B, S, attn_norm_w, w_in, w_cmp_k, w_cmp_v, cmp_pos, conv_w, conv_b,
                 attn_gnw, conv_gnw, rel_bias, tm=TM_PROJ, tm_conv=TM_CONV):
    T, D = xt.shape
    dq = N_Q_HEADS * HEAD_DIM
    dkv = N_KV_HEADS * HEAD_DIM
    n_attn = dq + 6 * dkv
    n_gate = 3 * N_Q_HEADS
    dc = (w_in.shape[1] - n_attn - n_gate) // 3
    w_t = w_in.T
    w_attn = cast_rows(w_t, 0, n_attn)
    w_conv = cast_rows(w_t, n_attn + n_gate, 3 * dc)
    w_gate = jnp.pad(w_t[n_attn:n_attn + n_gate], ((0, HEAD_DIM - n_gate), (0, 0))).astype(BF16)

    qkv = norm_matmul(xt, attn_norm_w, w_attn, BF16, tm, n_attn // 2, w_is_nk=True)
    gc, gl = norm_matmul(xt, attn_norm_w, w_conv, F32, tm, dc, w_is_nk=True, w_side=w_gate)

    n16 = S // STRIDE_CMP
    k16 = qkv[:, dq:dq + dkv].reshape(B * n16, STRIDE_CMP * dkv)
    v16 = qkv[:, dq + dkv:dq + 2 * dkv].reshape(B * n16, STRIDE_CMP * dkv)

    def wbig(w, lo):
        wl = w[lo:lo + STRIDE_CMP]
        eye = jnp.eye(N_KV_HEADS, dtype=w.dtype)
        return jnp.einsum('lde,hg->lhdge', wl, eye).reshape(STRIDE_CMP * dkv, dkv).astype(BF16)

    def posrow(lo):
        p = cmp_pos[lo:lo + STRIDE_CMP]
        return jnp.broadcast_to(p[:, None, :], (STRIDE_CMP, N_KV_HEADS, HEAD_DIM)).reshape(1, STRIDE_CMP * dkv)

    kc, vc = compress(k16, v16, posrow(0), posrow(STRIDE_CMP),
                      wbig(w_cmp_k, 0), wbig(w_cmp_k, STRIDE_CMP),
                      wbig(w_cmp_v, 0), wbig(w_cmp_v, STRIDE_CMP), B)

    tab = rel_bias[_t5_bucket_np(HEAD_DIM)].T
    ovt = jnp.asarray(_overlap_t_np(S), BF16)
    e2p = jnp.asarray(_block_onehot_padded_np(S), BF16)

    ocmp, selb = cmp_select(qkv, kc, vc, tab, ovt, B, S)
    def keys_padded(col):
        k = qkv[:, col:col + dkv].reshape(B, S, dkv)
        return jnp.pad(k, ((0, 0), (WINDOW, 0), (0, 0))).reshape(B * (WINDOW + S), dkv)

    def values_t_padded(col):
        v = qkv[:, col:col + dkv].reshape(B, S, dkv).transpose(0, 2, 1)
        return jnp.pad(v, ((0, 0), (0, 0), (WINDOW, 0))).reshape(B * dkv, WINDOW + S)

    attn = sel_win(qkv, selb, keys_padded(dq + 2 * dkv), values_t_padded(dq + 3 * dkv),
                   keys_padded(dq + 4 * dkv), values_t_padded(dq + 5 * dkv), e2p, tab, ocmp, gl,
                   attn_gnw.reshape(1, dq), B, S)
    cw8 = jnp.pad(conv_w, ((0, 8 - conv_w.shape[0]), (0, 0)))
    conv = conv_mixer(gc, cw8, conv_b.reshape(1, dc), conv_gnw.reshape(1, dc), S, tm_conv)
    return attn, conv


def peer_block(h, ffn_nw, final_nw, peer_wq, peer_subkeys, peer_u, peer_v, tm=TM_PROJ, tt=TT_DENSE):
    T, D = h.shape
    qp = norm_matmul(h, ffn_nw, peer_wq.astype(BF16), BF16, tm, TN_OUT)
    sk = peer_subkeys.reshape(PEER_HEADS * 2, N_KEYS, peer_subkeys.shape[-1]).astype(BF16)
    ex, gates, u_bf, v_bf = peer_route(qp, sk, peer_u, peer_v)
    gm = peer_gbuild(ex, gates)
    return peer_dense(h, ffn_nw, final_nw, u_bf, v_bf, gm, tt)


def kernel(x, attn_norm_w, w_in, w_cmp_k, w_cmp_v, cmp_pos, conv_w, conv_b, attn_group_norm_w,
           conv_group_norm_w, w_out, rel_bias, ffn_norm_w, peer_wq, peer_subkeys, peer_u, peer_v,
           final_norm_w):
    B, S, D = x.shape
    T = B * S
    xt = x.reshape(T, D)
    attn, conv = nsa_conv_mix(xt, B, S, attn_norm_w[0], w_in[0], w_cmp_k[0], w_cmp_v[0], cmp_pos[0],
                              conv_w[0], conv_b[0], attn_group_norm_w[0], conv_group_norm_w[0], rel_bias)
    da = attn.shape[1]
    wo = w_out[0].astype(BF16)
    h = out_proj(attn, conv, wo[:da], wo[da:], xt, TM_PROJ, TN_OUT)
    out = peer_block(h, ffn_norm_w[0], final_norm_w, peer_wq[0], peer_subkeys[0], peer_u[0], peer_v[0])
    return out.reshape(B, S, D)
```

```python
import functools
import math

import jax
import jax.numpy as jnp
import numpy as np
from jax import lax
from jax.experimental import pallas as pl
from jax.experimental.pallas import tpu as pltpu

F32 = jnp.float32
BF16 = jnp.bfloat16

HEAD_DIM = 128
N_KV_HEADS = 2
GQA = 4
N_Q_HEADS = N_KV_HEADS * GQA
L_CMP = 32
STRIDE_CMP = 16
L_SEL = 64
N_SEL = 16
WINDOW = 512
Q_BLOCK = 128
FORCED_SCORE = float(GQA + 1)
NUM_BUCKETS = 32
MAX_DISTANCE = 128
PEER_HEADS = 8
N_KEYS = 128
PEER_TOPK = 16
EPS = 1e-6
NEG_BIG = -1e30
SEL_OFF = -float(2 ** 30)
LOG2E = math.log2(math.e)
N_FIXED_A = 4

SUBLANES = 8
LANES = 128
MXU_DIM = 256
V7X_VMEM_BYTES = 64 * 1024 * 1024
VMEM_LIMIT = V7X_VMEM_BYTES * 7 // 8

HALF_KEYS = N_KEYS // 2
PACK_ROWS = SUBLANES

TM_PROJ = 4 * MXU_DIM
TN_OUT = 4 * MXU_DIM
TM_CONV = 2 * MXU_DIM
TQ_CMP = 4 * Q_BLOCK
TQ_SEL = 2 * Q_BLOCK
TT_ROUTE = LANES
TT_DENSE = 2 * MXU_DIM
CAST_ROWS = MXU_DIM


def _dot(a, b):
    return jnp.dot(a, b, preferred_element_type=F32)


def _dot_nt(a, b):
    return lax.dot_general(a, b, (((1,), (1,)), ((), ())), preferred_element_type=F32)


def _params(sem, vmem=VMEM_LIMIT):
    return pltpu.CompilerParams(dimension_semantics=sem, vmem_limit_bytes=vmem)


def _norm_matmul_kernel(x_ref, nw_ref, w_ref, *rest, w_is_nk, has_side):
    if has_side:
        ws_ref, o_ref, os_ref, xn_ref = rest
    else:
        o_ref, xn_ref = rest

    @pl.when(pl.program_id(1) == 0)
    def _():
        x = x_ref[...]
        y = x * lax.rsqrt(jnp.mean(x * x, axis=-1, keepdims=True) + EPS)
        xn_ref[...] = (y * nw_ref[...]).astype(BF16)
        if has_side:
            os_ref[...] = _dot_nt(xn_ref[...], ws_ref[...])

    dot = _dot_nt if w_is_nk else _dot
    o_ref[...] = dot(xn_ref[...], w_ref[...]).astype(o_ref.dtype)


def norm_matmul(x, norm_w, w, out_dtype, tm, tn, w_is_nk=False, w_side=None):
    T, D = x.shape
    N = w.shape[0] if w_is_nk else w.shape[1]
    w_spec = pl.BlockSpec((tn, D), lambda i, j: (j, 0)) if w_is_nk else pl.BlockSpec((D, tn), lambda i, j: (0, j))
    in_specs = [pl.BlockSpec((tm, D), lambda i, j: (i, 0)), pl.BlockSpec((1, D), lambda i, j: (0, 0)), w_spec]
    out_specs = [pl.BlockSpec((tm, tn), lambda i, j: (i, j))]
    out_shape = [jax.ShapeDtypeStruct((T, N), out_dtype)]
    args = [x, norm_w.reshape(1, D), w]
    if w_side is not None:
        n_side = w_side.shape[0]
        in_specs.append(pl.BlockSpec((n_side, D), lambda i, j: (0, 0)))
        out_specs.append(pl.BlockSpec((tm, n_side), lambda i, j: (i, 0)))
        out_shape.append(jax.ShapeDtypeStruct((T, n_side), F32))
        args.append(w_side)
    out = pl.pallas_call(
        functools.partial(_norm_matmul_kernel, w_is_nk=w_is_nk, has_side=w_side is not None),
        grid=(T // tm, N // tn),
        in_specs=in_specs,
        out_specs=out_specs,
        out_shape=out_shape,
        scratch_shapes=[pltpu.VMEM((tm, D), BF16)],
        compiler_params=_params(("parallel", "arbitrary")),
    )(*args)
    return out if w_side is not None else out[0]


def _cast_rows_kernel(w_ref, o_ref):
    o_ref[...] = w_ref[...].astype(o_ref.dtype)


def cast_rows(w, row0, n_rows, tr=CAST_ROWS):
    K = w.shape[1]
    assert row0 % SUBLANES == 0 and n_rows % tr == 0
    return pl.pallas_call(
        _cast_rows_kernel,
        grid=(n_rows // tr,),
        in_specs=[pl.BlockSpec((pl.Element(tr), pl.Element(K)), lambda i: (pl.multiple_of(row0 + i * tr, SUBLANES), 0))],
        out_specs=pl.BlockSpec((tr, K), lambda i: (i, 0)),
        out_shape=jax.ShapeDtypeStruct((n_rows, K), BF16),
        compiler_params=_params(("parallel",)),
    )(w)


def _compress_kernel(k_ref, v_ref, pa_ref, pb_ref, wka_ref, wkb_ref, wva_ref, wvb_ref, kc_ref, vc_ref):
    def one(x_ref, wa_ref, wb_ref, o_ref):
        x = x_ref[...].astype(F32)
        a = _dot((x + pa_ref[...]).astype(BF16), wa_ref[...])
        b = _dot((x + pb_ref[...]).astype(BF16), wb_ref[...])
        n = b.shape[0]
        o_ref[...] = (a + pltpu.roll(b, n - 1, 0)).astype(o_ref.dtype)

    one(k_ref, wka_ref, wkb_ref, kc_ref)
    one(v_ref, wva_ref, wvb_ref, vc_ref)


def compress(k16, v16, pos_a, pos_b, wka, wkb, wva, wvb, B):
    R, C = k16.shape
    nb = R // B
    dkv = wka.shape[1]
    full = lambda shp: pl.BlockSpec(shp, lambda b: (0, 0))
    return pl.pallas_call(
        _compress_kernel,
        grid=(B,),
        in_specs=[pl.BlockSpec((nb, C), lambda b: (b, 0)),
                  pl.BlockSpec((nb, C), lambda b: (b, 0)),
                  full((1, C)), full((1, C)),
                  full((C, dkv)), full((C, dkv)), full((C, dkv)), full((C, dkv))],
        out_specs=[pl.BlockSpec((nb, dkv), lambda b: (b, 0))] * 2,
        out_shape=[jax.ShapeDtypeStruct((R, dkv), BF16)] * 2,
        compiler_params=_params(("parallel",)),
    )(k16, v16, pos_a, pos_b, wka, wkb, wva, wvb)


def _bias_from_dist(tab_row, dist):
    idx = jnp.clip(dist, 0, 127)
    w = tab_row.shape[1]
    tab = jnp.broadcast_to(tab_row, (idx.shape[0], w))
    parts = [jnp.take_along_axis(tab, idx[:, k:k + w], axis=1) for k in range(0, idx.shape[1], w)]
    return parts[0] if len(parts) == 1 else jnp.concatenate(parts, axis=1)


def _cmp_select_kernel(q_ref, kc_ref, vc_ref, tab_ref, ovt_ref, *rest, n_qb, scale, n_cast):
    cast_in, (ocmp_ref, selb_ref), cast_out = rest[:n_cast], rest[n_cast:n_cast + 2], rest[n_cast + 2:]
    for src_ref, dst_ref in zip(cast_in, cast_out):
        dst_ref[...] = src_ref[...].astype(BF16)
    c = pl.program_id(0) % n_qb
    tq = q_ref.shape[0]
    n_c = kc_ref.shape[0]
    n_sel = ovt_ref.shape[0]
    t0 = c * tq
    t_col = t0 + lax.broadcasted_iota(jnp.int32, (tq, n_c), 0)
    n_row = lax.broadcasted_iota(jnp.int32, (tq, n_c), 1)
    dist = t_col - (n_row * STRIDE_CMP + (L_CMP - 1))
    valid = dist >= 0

    j_io = lax.broadcasted_iota(jnp.int32, (n_sel, tq), 0)
    t_io = t0 + lax.broadcasted_iota(jnp.int32, (n_sel, tq), 1)
    blk_t = t_io // L_SEL
    forced = (j_io == 0) | (j_io == blk_t) | (j_io == blk_t - 1)
    causal_blk = j_io * L_SEL <= t_io

    for h in range(N_KV_HEADS):
        kc = kc_ref[:, h * HEAD_DIM:(h + 1) * HEAD_DIM]
        vc = vc_ref[:, h * HEAD_DIM:(h + 1) * HEAD_DIM]
        psum = jnp.zeros((tq, n_c), F32)
        for g in range(GQA):
            hd = h * GQA + g
            qh = q_ref[:, hd * HEAD_DIM:(hd + 1) * HEAD_DIM]
            bias = _bias_from_dist(tab_ref[hd:hd + 1, :], dist)
            s = _dot_nt(qh, kc) * scale + bias
            s = jnp.where(valid, s, NEG_BIG)
            m = jnp.max(s, axis=-1, keepdims=True)
            e = jnp.where(valid, jnp.exp(s - m), 0.0)
            d = jnp.sum(e, axis=-1, keepdims=True)
            p = e / jnp.where(d > 0, d, 1.0)
            ocmp_ref[:, hd * HEAD_DIM:(hd + 1) * HEAD_DIM] = _dot(p.astype(BF16), vc)
            psum = psum + p
        p_hi = psum.astype(BF16)
        p_lo = (psum - p_hi.astype(F32)).astype(BF16)
        ovt = ovt_ref[...]
        imp = _dot_nt(ovt, p_hi) + _dot_nt(ovt, p_lo)
        score = jnp.where(forced, FORCED_SCORE, jnp.where(causal_blk, imp, -1.0))
        rank = jnp.zeros((n_sel, tq), F32)
        for jp in range(n_sel):
            row = score[jp:jp + 1, :]
            rank = rank + jnp.where(j_io > jp, jnp.where(row >= score, 1.0, 0.0),
                                    jnp.where(row > score, 1.0, 0.0))
        selb = jnp.where(rank < float(N_SEL), 0.0, SEL_OFF)
        if n_sel < HEAD_DIM:
            selb = jnp.concatenate([selb, jnp.zeros((HEAD_DIM - n_sel, tq), F32)], axis=0)
        selb_ref[:, h * HEAD_DIM:(h + 1) * HEAD_DIM] = selb.T.astype(BF16)


def cmp_select(qkv, kc, vc, tab, ovt, B, S, tq=TQ_CMP, cast_also=()):
    T = qkv.shape[0]
    n_qb = S // tq
    steps = T // tq
    n_c = kc.shape[0] // B
    dq = N_Q_HEADS * HEAD_DIM
    dkv = N_KV_HEADS * HEAD_DIM
    kern = functools.partial(_cmp_select_kernel, n_qb=n_qb, scale=HEAD_DIM ** -0.5, n_cast=len(cast_also))
    cast_specs = [pl.BlockSpec((w.shape[0] // steps, w.shape[1]), lambda i: (i, 0)) for w in cast_also]
    assert all(w.shape[0] % (2 * SUBLANES * steps) == 0 for w in cast_also)
    return pl.pallas_call(
        kern,
        grid=(steps,),
        in_specs=[pl.BlockSpec((tq, dq), lambda i: (i, 0)),
                  pl.BlockSpec((n_c, dkv), lambda i: (i // n_qb, 0)),
                  pl.BlockSpec((n_c, dkv), lambda i: (i // n_qb, 0)),
                  pl.BlockSpec(tab.shape, lambda i: (0, 0)),
                  pl.BlockSpec(ovt.shape, lambda i: (0, 0))] + cast_specs,
        out_specs=[pl.BlockSpec((tq, dq), lambda i: (i, 0)),
                   pl.BlockSpec((tq, dkv), lambda i: (i, 0))] + cast_specs,
        out_shape=[jax.ShapeDtypeStruct((T, dq), F32),
                   jax.ShapeDtypeStruct((T, dkv), BF16)] + [jax.ShapeDtypeStruct(w.shape, BF16) for w in cast_also],
        compiler_params=_params(("parallel",)),
    )(qkv, kc, vc, tab, ovt, *cast_also)


def _flash_update(s_raw, vt, bias, m_ref, l_ref, acc_ref, scale):
    s = s_raw * scale + bias
    m_old = m_ref[...]
    m_new = jnp.maximum(m_old, jnp.max(s, axis=0, keepdims=True))
    alpha = jnp.exp2(m_old - m_new)
    p = jnp.exp2(s - m_new)
    l_ref[...] = alpha * l_ref[...] + jnp.sum(p, axis=0, keepdims=True)
    acc_ref[...] = alpha * acc_ref[...] + _dot(vt, p.astype(BF16))
    m_ref[...] = m_new


def _flash_chunk(k, vt, q, bias, m_ref, l_ref, acc_ref, scale):
    _flash_update(_dot_nt(k, q), vt, bias, m_ref, l_ref, acc_ref, scale)


def _sel_win_kernel(q_ref, selb_ref, ks_ref, vst_ref, kw_ref, vwt_ref, e2_ref, tab_ref, ocmp_ref,
                    gl_ref, gnw_ref, o_ref, m_ref, l_ref, acc_ref, sa_ref, sb_ref, *, n_qb, scale):
    c = pl.program_id(0) % n_qb
    tq = q_ref.shape[0]
    rows = GQA * tq
    n_back = WINDOW // tq
    near = WINDOW + tq
    j_io = lax.broadcasted_iota(jnp.int32, (tq, tq), 0)
    i_io = lax.broadcasted_iota(jnp.int32, (tq, tq), 1)
    dij = i_io - j_io
    dij4 = jnp.concatenate([dij] * GQA, axis=1)
    causal = dij4 >= 0
    pad_col = jnp.where(lax.broadcasted_iota(jnp.int32, (rows, HEAD_DIM), 1) == HEAD_DIM - 1, 1.0, 0.0).astype(BF16)
    sig = jax.nn.sigmoid(gl_ref[...])
    near0 = pl.multiple_of(c * tq, tq)
    n_far = jnp.maximum(c - n_back, 0)
    n_full = n_far // n_back
    n_rem = n_far - n_full * n_back

    def hsl(h):
        return slice(h * HEAD_DIM, (h + 1) * HEAD_DIM)

    def far0(i):
        return pl.multiple_of(WINDOW + i * WINDOW, WINDOW)

    def reset():
        m_ref[...] = jnp.full(m_ref.shape, NEG_BIG, F32)
        l_ref[...] = jnp.zeros(l_ref.shape, F32)
        acc_ref[...] = jnp.zeros(acc_ref.shape, F32)

    def result():
        ot = acc_ref[...] / l_ref[...]
        return [ot[:, g * tq:(g + 1) * tq].T for g in range(GQA)]

    for h in range(N_KV_HEADS):
        q4 = jnp.concatenate([q_ref[:, (h * GQA + g) * HEAD_DIM:(h * GQA + g + 1) * HEAD_DIM]
                              for g in range(GQA)], axis=0)
        sb4 = jnp.concatenate([selb_ref[:, hsl(h)]] * GQA, axis=0)
        q_aug = jnp.concatenate([q4, sb4 + pad_col], axis=1)
        q_win = jnp.concatenate([q4, pad_col], axis=1)
        tabs = [tab_ref[h * GQA + g:h * GQA + g + 1, :] * LOG2E for g in range(GQA)]
        d0 = jnp.concatenate([_bias_from_dist(t, dij) for t in tabs], axis=1)
        d1 = jnp.concatenate([_bias_from_dist(t, dij + tq) for t in tabs], axis=1)
        far = jnp.concatenate([t[:, HEAD_DIM - 1:HEAD_DIM] + jnp.zeros((1, tq), F32) for t in tabs], axis=1)
        far_t = jnp.broadcast_to(far, (tq, rows))
        diag = jnp.where(causal, d0, NEG_BIG)
        bias_sel = jnp.concatenate([far_t] * (n_back - 1) + [d1, diag], axis=0)
        bias_win = jnp.concatenate([jnp.where(dij4 < 0, far_t, NEG_BIG)] + [far_t] * (n_back - 2) + [d1, diag],
                                   axis=0)

        reset()
        e2_near = e2_ref[pl.ds(near0, near), :]
        k_near = jnp.concatenate([ks_ref[pl.ds(near0, near), hsl(h)], e2_near], axis=1)
        _flash_chunk(k_near, vst_ref[hsl(h), pl.ds(near0, near)], q_aug, bias_sel, m_ref, l_ref, acc_ref, scale)

        def far_scores(i):
            r0 = far0(i)
            k = jnp.concatenate([ks_ref[pl.ds(r0, WINDOW), hsl(h)], e2_ref[pl.ds(r0, WINDOW), :]], axis=1)
            return _dot_nt(k, q_aug)

        def far_update(s_raw, i, bias):
            _flash_update(s_raw, vst_ref[hsl(h), pl.ds(far0(i), WINDOW)], bias, m_ref, l_ref, acc_ref, scale)

        sa_ref[...] = far_scores(0)

        def body(i, carry):
            sb_ref[...] = far_scores(2 * i + 1)
            far_update(sa_ref[...], 2 * i, far)
            sa_ref[...] = far_scores(2 * i + 2)
            far_update(sb_ref[...], 2 * i + 1, far)
            return carry

        lax.fori_loop(0, n_full // 2, body, 0)

        @pl.when(n_full % 2 == 1)
        def _():
            far_update(sa_ref[...], n_full - 1, far)

        @pl.when(n_rem > 0)
        def _():
            if n_back == 2:
                r0 = far0(n_full)
                k = jnp.concatenate([ks_ref[pl.ds(r0, tq), hsl(h)], e2_ref[pl.ds(r0, tq), :]], axis=1)
                _flash_chunk(k, vst_ref[hsl(h), pl.ds(r0, tq)], q_aug, far, m_ref, l_ref, acc_ref, scale)
            else:
                live = lax.broadcasted_iota(jnp.int32, (WINDOW, rows), 0) < n_rem * tq
                far_update(far_scores(n_full), n_full,
                           jnp.where(live, jnp.broadcast_to(far, (WINDOW, rows)), NEG_BIG))

        o_sel = result()

        reset()
        kw_near = jnp.concatenate([kw_ref[pl.ds(near0, near), hsl(h)], e2_near], axis=1)
        _flash_chunk(kw_near, vwt_ref[hsl(h), pl.ds(near0, near)], q_win, bias_win, m_ref, l_ref, acc_ref, scale)
        o_win = result()

        for g in range(GQA):
            hd = h * GQA + g
            o = (sig[:, 3 * hd:3 * hd + 1] * ocmp_ref[:, hsl(hd)]
                 + sig[:, 3 * hd + 1:3 * hd + 2] * o_sel[g]
                 + sig[:, 3 * hd + 2:3 * hd + 3] * o_win[g])
            y = o * lax.rsqrt(jnp.mean(o * o, axis=-1, keepdims=True) + EPS)
            o_ref[:, hsl(hd)] = (y * gnw_ref[:, hsl(hd)]).astype(o_ref.dtype)


def sel_win(qkv, selb, ksp, vstp, kwp, vwtp, e2p, tab, ocmp, gl, gnw, B, S, tq=TQ_SEL):
    T = qkv.shape[0]
    n_qb = S // tq
    dq = N_Q_HEADS * HEAD_DIM
    dkv = N_KV_HEADS * HEAD_DIM
    SP = S + WINDOW
    kern = functools.partial(_sel_win_kernel, n_qb=n_qb, scale=HEAD_DIM ** -0.5 * LOG2E)
    k_spec = pl.BlockSpec((SP, dkv), lambda i: (i // n_qb, 0))
    vt_spec = pl.BlockSpec((dkv, SP), lambda i: (i // n_qb, 0))
    return pl.pallas_call(
        kern,
        grid=(T // tq,),
        in_specs=[pl.BlockSpec((tq, dq), lambda i: (i, 0)),
                  pl.BlockSpec((tq, dkv), lambda i: (i, 0)),
                  k_spec, vt_spec, k_spec, vt_spec,
                  pl.BlockSpec((SP, HEAD_DIM), lambda i: (0, 0)),
                  pl.BlockSpec(tab.shape, lambda i: (0, 0)),
                  pl.BlockSpec((tq, dq), lambda i: (i, 0)),
                  pl.BlockSpec((tq, HEAD_DIM), lambda i: (i, 0)),
                  pl.BlockSpec((1, dq), lambda i: (0, 0))],
        out_specs=pl.BlockSpec((tq, dq), lambda i: (i, 0)),
        out_shape=jax.ShapeDtypeStruct((T, dq), BF16),
        scratch_shapes=[pltpu.VMEM((1, GQA * tq), F32), pltpu.VMEM((1, GQA * tq), F32),
                        pltpu.VMEM((HEAD_DIM, GQA * tq), F32),
                        pltpu.VMEM((WINDOW, GQA * tq), F32), pltpu.VMEM((WINDOW, GQA * tq), F32)],
        compiler_params=_params(("parallel",)),
    )(qkv, selb, ksp, vstp, kwp, vwtp, e2p, tab, ocmp, gl, gnw)


def _conv_kernel(b_ref, c_ref, h_ref, cp_ref, hp_ref, cw_ref, cb_ref, gnw_ref, o_ref, u_ref, *, tiles_per_seq):
    tm = b_ref.shape[0]
    first = (pl.program_id(0) % tiles_per_seq) == 0
    u_prev = cp_ref[...] * hp_ref[...]
    u_ref[0:8, :] = jnp.where(first, 0.0, u_prev)
    u = c_ref[...] * h_ref[...]
    u_ref[8:8 + tm, :] = u
    y = (cw_ref[0:1, :] * u_ref[6:6 + tm, :] + cw_ref[1:2, :] * u_ref[7:7 + tm, :]
         + cw_ref[2:3, :] * u + cb_ref[...])
    o = b_ref[...] * y
    n_groups = o.shape[1] // HEAD_DIM
    for g in range(n_groups):
        sl = slice(g * HEAD_DIM, (g + 1) * HEAD_DIM)
        og = o[:, sl]
        yg = og * lax.rsqrt(jnp.mean(og * og, axis=-1, keepdims=True) + EPS)
        o_ref[:, sl] = (yg * gnw_ref[:, sl]).astype(o_ref.dtype)


def conv_mixer(gc, conv_w, conv_b, gnw, S, tm):
    T = gc.shape[0]
    dc = conv_w.shape[1]
    tps = S // tm
    kern = functools.partial(_conv_kernel, tiles_per_seq=tps)
    prev = lambda col: pl.BlockSpec((8, dc), lambda i, col=col: (jnp.maximum(i * (tm // 8) - 1, 0), col))
    cur = lambda col: pl.BlockSpec((tm, dc), lambda i, col=col: (i, col))
    return pl.pallas_call(
        kern,
        grid=(T // tm,),
        in_specs=[cur(0), cur(1), cur(2), prev(1), prev(2),
                  pl.BlockSpec((8, dc), lambda i: (0, 0)),
                  pl.BlockSpec((1, dc), lambda i: (0, 0)),
                  pl.BlockSpec((1, dc), lambda i: (0, 0))],
        out_specs=pl.BlockSpec((tm, dc), lambda i: (i, 0)),
        out_shape=jax.ShapeDtypeStruct((T, dc), BF16),
        scratch_shapes=[pltpu.VMEM((tm + 8, dc), F32)],
        compiler_params=_params(("parallel",)),
    )(gc, gc, gc, gc, gc, conv_w, conv_b, gnw)


def _out_proj_kernel(ma_ref, mc_ref, wa_ref, wc_ref, x_ref, o_ref):
    o_ref[...] = x_ref[...] + _dot(ma_ref[...], wa_ref[...]) + _dot(mc_ref[...], wc_ref[...])


def out_proj(ma, mc, wa, wc, x, tm, tn):
    T, da = ma.shape
    dc = mc.shape[1]
    D = x.shape[1]
    return pl.pallas_call(
        _out_proj_kernel,
        grid=(T // tm, D // tn),
        in_specs=[pl.BlockSpec((tm, da), lambda i, j: (i, 0)),
                  pl.BlockSpec((tm, dc), lambda i, j: (i, 0)),
                  pl.BlockSpec((da, tn), lambda i, j: (0, j)),
                  pl.BlockSpec((dc, tn), lambda i, j: (0, j)),
                  pl.BlockSpec((tm, tn), lambda i, j: (i, j))],
        out_specs=pl.BlockSpec((tm, tn), lambda i, j: (i, j)),
        out_shape=jax.ShapeDtypeStruct((T, D), F32),
        compiler_params=_params(("parallel", "arbitrary")),
    )(ma, mc, wa, wc, x)


def _sort_network(n):
    pairs = []
    p = 1
    while p < n:
        k = p
        while k >= 1:
            for j in range(k % p, n - k, 2 * k):
                for i in range(min(k, n - j - k)):
                    if (i + j) // (2 * p) == (i + j + k) // (2 * p):
                        pairs.append((i + j, i + j + k))
            k //= 2
        p *= 2
    return pairs


def _peer_route_kernel(q_ref, sk_ref, u_ref, v_ref, e_ref, g_ref, ub_ref, vb_ref, sv_ref, si_ref):
    ub_ref[...] = u_ref[...].astype(BF16)
    vb_ref[...] = v_ref[...].astype(BF16)
    tt = q_ref.shape[0]
    K = PEER_TOPK
    sub = SUBLANES
    n_io = lax.broadcasted_iota(jnp.int32, (N_KEYS, tt), 0).astype(F32)
    r8 = lax.broadcasted_iota(jnp.int32, (sub, tt), 0).astype(F32)
    ninf = -jnp.inf
    lens = [K // (a + 1) for a in range(N_FIXED_A)]
    lens += [max(K // (b + 1) - N_FIXED_A, 0) for b in range(sub - N_FIXED_A)]
    assert sum(lens) == sum(K // (a + 1) for a in range(K)) and lens[-1] == 0
    fixed_a = r8 < float(N_FIXED_A)
    list_len = jnp.zeros((sub, tt), F32)
    for row, n in enumerate(lens):
        list_len = jnp.where(r8 == float(row), float(n), list_len)

    for h in range(PEER_HEADS):
        for c in range(2):
            col = (h * 2 + c) * HEAD_DIM
            s = _dot_nt(sk_ref[h * 2 + c], q_ref[:, col:col + HEAD_DIM])
            n_col = N_KEYS // sub
            vals = [s[j * sub:(j + 1) * sub, :] for j in range(n_col)]
            idxs = [n_io[j * sub:(j + 1) * sub, :] for j in range(n_col)]
            for lo, hi in _sort_network(n_col):
                swap = (vals[hi] > vals[lo]) | ((vals[hi] == vals[lo]) & (idxs[hi] < idxs[lo]))
                vals[lo], vals[hi] = jnp.where(swap, vals[hi], vals[lo]), jnp.where(swap, vals[lo], vals[hi])
                idxs[lo], idxs[hi] = jnp.where(swap, idxs[hi], idxs[lo]), jnp.where(swap, idxs[lo], idxs[hi])
            for k in range(K):
                m = jnp.max(vals[0], axis=0, keepdims=True)
                idx = jnp.min(jnp.where(vals[0] == m, idxs[0], float(N_KEYS)), axis=0, keepdims=True)
                sv_ref[c, k:k + 1, :] = m
                si_ref[c, k:k + 1, :] = idx
                win = idxs[0] == idx
                for j in range(min(n_col, K) - 1 - k):
                    vals[j] = jnp.where(win, vals[j + 1], vals[j])
                    idxs[j] = jnp.where(win, idxs[j + 1], idxs[j])
        sv0, sv1 = sv_ref[0], sv_ref[1]
        si0, si1 = si_ref[0], si_ref[1]
        sv1_low = pltpu.roll(sv1[0:sub, :], N_FIXED_A, 0)
        sv0_top = sv0[0:sub, :]
        lv, lf = [], []
        for dep in range(K):
            a_dep = min(N_FIXED_A + dep, K - 1)
            val = (jnp.where(fixed_a, sv0_top, sv0[a_dep:a_dep + 1, :])
                   + jnp.where(fixed_a, sv1[dep:dep + 1, :], sv1_low))
            lv.append(jnp.where(list_len > dep, val, ninf))
            lf.append(jnp.where(fixed_a, r8 * float(K) + dep, (N_FIXED_A + dep) * float(K) + r8 - N_FIXED_A))
        cvs, fls = [], []
        for k in range(K):
            m = jnp.max(lv[0], axis=0, keepdims=True)
            fsel = jnp.min(jnp.where(lv[0] == m, lf[0], 1e9), axis=0, keepdims=True)
            win = lf[0] == fsel
            cvs.append(m)
            fls.append(fsel)
            for dep in range(K - 1 - k):
                lv[dep] = jnp.where(win, lv[dep + 1], lv[dep])
                lf[dep] = jnp.where(win, lf[dep + 1], lf[dep])
        cv = jnp.concatenate(cvs, axis=0)
        fl = jnp.concatenate(fls, axis=0)
        a_sel = jnp.floor(fl * (1.0 / K))
        b_sel = fl - a_sel * K
        i1 = jnp.zeros_like(fl)
        i2 = jnp.zeros_like(fl)
        for r in range(K):
            i1 = jnp.where(a_sel == r, si0[r:r + 1, :], i1)
            i2 = jnp.where(b_sel == r, si1[r:r + 1, :], i2)
        ex = i1 * float(N_KEYS) + i2
        ev = jnp.exp(cv - jnp.max(cv, axis=0, keepdims=True))
        gates = ev / jnp.sum(ev, axis=0, keepdims=True)
        e_ref[0, h * K:(h + 1) * K, :] = ex.astype(jnp.int32)
        g_ref[0, h * K:(h + 1) * K, :] = gates


def peer_route(qp, subkeys, u, v):
    T = qp.shape[0]
    tt = TT_ROUTE
    P = PEER_HEADS * PEER_TOPK
    nt = T // tt
    E, D = u.shape
    slab = E // nt
    assert slab * nt == E and slab % 16 == 0
    tab_spec = pl.BlockSpec((slab, D), lambda i: (i, 0))
    return pl.pallas_call(
        _peer_route_kernel,
        grid=(nt,),
        in_specs=[pl.BlockSpec((tt, qp.shape[1]), lambda i: (i, 0)),
                  pl.BlockSpec(subkeys.shape, lambda i: (0, 0, 0)),
                  tab_spec, tab_spec],
        out_specs=[pl.BlockSpec((1, P, tt), lambda i: (i, 0, 0))] * 2 + [tab_spec, tab_spec],
        out_shape=[jax.ShapeDtypeStruct((nt, P, tt), jnp.int32),
                   jax.ShapeDtypeStruct((nt, P, tt), F32),
                   jax.ShapeDtypeStruct((E, D), BF16), jax.ShapeDtypeStruct((E, D), BF16)],
        scratch_shapes=[pltpu.VMEM((2, PEER_TOPK, tt), F32), pltpu.VMEM((2, PEER_TOPK, tt), F32)],
        compiler_params=_params(("parallel",)),
    )(qp, subkeys, u, v)


def _peer_gbuild_kernel(e_ref, g_ref, o_ref, i1_ref, i2_ref, gt_ref):
    tt = e_ref.shape[2]
    e = e_ref[0].T
    i1_ref[...] = e >> 7
    i2_ref[...] = e & (N_KEYS - 1)
    gt_ref[...] = g_ref[0].T
    P = e.shape[1]
    k_io = lax.broadcasted_iota(jnp.int32, (N_KEYS, P), 0)
    group = 64

    def body(tg, carry):
        base = pl.multiple_of(tg * group, group)
        r1s = i1_ref[pl.ds(base, group), :]
        r2s = i2_ref[pl.ds(base, group), :]
        rgs = gt_ref[pl.ds(base, group), :]
        for u in range(group):
            lhs = jnp.where(k_io == r1s[u:u + 1, :], rgs[u:u + 1, :], 0.0).astype(BF16)
            rhs = jnp.where(k_io == r2s[u:u + 1, :], 1.0, 0.0).astype(BF16)
            g_t = _dot_nt(lhs, rhs).astype(BF16).astype(F32)
            lo = pltpu.bitcast(g_t[:HALF_KEYS], jnp.uint32) >> 16
            hi = pltpu.bitcast(g_t[HALF_KEYS:], jnp.uint32)
            words = hi | lo
            for a in range(HALF_KEYS // PACK_ROWS):
                o_ref[a, base + u] = words[a * PACK_ROWS:(a + 1) * PACK_ROWS, :]
        return carry

    lax.fori_loop(0, tt // group, body, 0)


def peer_gbuild(ex, gates):
    nt, P, tt = ex.shape
    T = nt * tt
    n_a = HALF_KEYS // PACK_ROWS
    return pl.pallas_call(
        _peer_gbuild_kernel,
        grid=(nt,),
        in_specs=[pl.BlockSpec((1, P, tt), lambda i: (i, 0, 0))] * 2,
        out_specs=pl.BlockSpec((n_a, tt, PACK_ROWS, N_KEYS), lambda i: (0, i, 0, 0)),
        out_shape=jax.ShapeDtypeStruct((n_a, T, PACK_ROWS, N_KEYS), jnp.uint32),
        scratch_shapes=[pltpu.VMEM((tt, P), jnp.int32), pltpu.VMEM((tt, P), jnp.int32),
                        pltpu.VMEM((tt, P), F32)],
        compiler_params=_params(("parallel",)),
    )(ex, gates)


def _peer_dense_kernel(h_ref, nw_ref, fw_ref, u_ref, v_ref, gm_ref, o_ref, hn_ref, acc_ref):
    j = pl.program_id(1)
    tt = h_ref.shape[0]

    @pl.when(j == 0)
    def _():
        x = h_ref[...]
        y = x * lax.rsqrt(jnp.mean(x * x, axis=-1, keepdims=True) + EPS)
        hn_ref[...] = (y * nw_ref[...]).astype(BF16)
        acc_ref[...] = jnp.zeros(acc_ref.shape, F32)

    a = _dot_nt(hn_ref[...], u_ref[...])
    shift = (16 * (1 - j % 2)).astype(jnp.uint32)
    gm = jnp.concatenate(
        [pltpu.bitcast((gm_ref[pl.ds(k, tt, stride=PACK_ROWS), :] << shift) & jnp.uint32(0xFFFF0000), F32)
         for k in range(PACK_ROWS)], axis=1)
    w = (gm * jax.nn.gelu(a)).astype(BF16)
    acc_ref[...] += _dot(w, v_ref[...])

    @pl.when(j == pl.num_programs(1) - 1)
    def _():
        x = h_ref[...] + acc_ref[...]
        y = x * lax.rsqrt(jnp.mean(x * x, axis=-1, keepdims=True) + EPS)
        o_ref[...] = y * fw_ref[...]


def peer_dense(h, ffn_nw, final_nw, u, v, gm_words, tt):
    T, D = h.shape
    E = u.shape[0]
    ec = PACK_ROWS * N_KEYS
    n_a = gm_words.shape[0]
    assert E == 2 * n_a * ec
    gm2 = gm_words.reshape(n_a, T * PACK_ROWS, N_KEYS)
    tab_spec = pl.BlockSpec((ec, D), lambda i, j: ((j % 2) * n_a + j // 2, 0))
    return pl.pallas_call(
        _peer_dense_kernel,
        grid=(T // tt, E // ec),
        in_specs=[pl.BlockSpec((tt, D), lambda i, j: (i, 0)),
                  pl.BlockSpec((1, D), lambda i, j: (0, 0)),
                  pl.BlockSpec((1, D), lambda i, j: (0, 0)),
                  tab_spec, tab_spec,
                  pl.BlockSpec((None, tt * PACK_ROWS, N_KEYS), lambda i, j: (j // 2, i, 0))],
        out_specs=pl.BlockSpec((tt, D), lambda i, j: (i, 0)),
        out_shape=jax.ShapeDtypeStruct((T, D), F32),
        scratch_shapes=[pltpu.VMEM((tt, D), BF16), pltpu.VMEM((tt, D), F32)],
        compiler_params=_params(("parallel", "arbitrary")),
    )(h, ffn_nw.reshape(1, D), final_nw.reshape(1, D), u, v, gm2)


def _t5_bucket_np(n_dist):
    d = np.arange(n_dist)
    max_exact = NUM_BUCKETS // 2
    nf = np.maximum(d, 1).astype(np.float64)
    large = max_exact + (np.log(nf / max_exact) / math.log(MAX_DISTANCE / max_exact)
                         * (NUM_BUCKETS - max_exact)).astype(np.int64)
    large = np.minimum(large, NUM_BUCKETS - 1)
    return np.where(d < max_exact, d, large).astype(np.int32)


def _overlap_t_np(S):
    n_c = (S - L_CMP) // STRIDE_CMP + 1
    n_sel = S // L_SEL
    pos = np.arange(n_c)[:, None] * STRIDE_CMP + np.arange(L_CMP)[None, :]
    m = np.zeros((n_c + 1, n_sel), np.float32)
    np.add.at(m, (np.repeat(np.arange(n_c), L_CMP), (pos // L_SEL).reshape(-1)), 1.0 / L_CMP)
    return np.ascontiguousarray(m.T)


def _block_onehot_padded_np(S):
    assert S // L_SEL < HEAD_DIM
    e2 = np.zeros((WINDOW + S, HEAD_DIM), np.float32)
    e2[WINDOW + np.arange(S), np.arange(S) // L_SEL] = 1.0
    e2[:WINDOW, HEAD_DIM - 1] = SEL_OFF
    return e2


def nsa_conv_mix(xt, B, S, attn_norm_w, w_in, w_cmp_k, w_cmp_v, cmp_pos, conv_w, conv_b,
                 attn_gnw, conv_gnw, rel_bias, tm=TM_PROJ, tm_conv=TM_CONV, cast_also=()):
    T, D = xt.shape
    dq = N_Q_HEADS * HEAD_DIM
    dkv = N_KV_HEADS * HEAD_DIM
    n_attn = dq + 6 * dkv
    n_gate = 3 * N_Q_HEADS
    dc = (w_in.shape[1] - n_attn - n_gate) // 3
    w_t = w_in.T
    w_attn = cast_rows(w_t, 0, n_attn)
    w_conv = cast_rows(w_t, n_attn + n_gate, 3 * dc)
    w_gate = jnp.pad(w_t[n_attn:n_attn + n_gate], ((0, HEAD_DIM - n_gate), (0, 0))).astype(BF16)

    qkv = norm_matmul(xt, attn_norm_w, w_attn, BF16, tm, n_attn // 2, w_is_nk=True)
    gc, gl = norm_matmul(xt, attn_norm_w, w_conv, F32, tm, dc, w_is_nk=True, w_side=w_gate)

    n16 = S // STRIDE_CMP
    k16 = qkv[:, dq:dq + dkv].reshape(B * n16, STRIDE_CMP * dkv)
    v16 = qkv[:, dq + dkv:dq + 2 * dkv].reshape(B * n16, STRIDE_CMP * dkv)

    def wbig(w, lo):
        wl = w[lo:lo + STRIDE_CMP]
        eye = jnp.eye(N_KV_HEADS, dtype=w.dtype)
        return jnp.einsum('lde,hg->lhdge', wl, eye).reshape(STRIDE_CMP * dkv, dkv).astype(BF16)

    def posrow(lo):
        p = cmp_pos[lo:lo + STRIDE_CMP]
        return jnp.broadcast_to(p[:, None, :], (STRIDE_CMP, N_KV_HEADS, HEAD_DIM)).reshape(1, STRIDE_CMP * dkv)

    kc, vc = compress(k16, v16, posrow(0), posrow(STRIDE_CMP),
                      wbig(w_cmp_k, 0), wbig(w_cmp_k, STRIDE_CMP),
                      wbig(w_cmp_v, 0), wbig(w_cmp_v, STRIDE_CMP), B)

    tab = rel_bias[_t5_bucket_np(HEAD_DIM)].T
    ovt = jnp.asarray(_overlap_t_np(S), BF16)
    e2p = jnp.asarray(_block_onehot_padded_np(S), BF16)

    ocmp, selb, *casts = cmp_select(qkv, kc, vc, tab, ovt, B, S, cast_also=cast_also)
    def keys_padded(col):
        k = qkv[:, col:col + dkv].reshape(B, S, dkv)
        return jnp.pad(k, ((0, 0), (WINDOW, 0), (0, 0))).reshape(B * (WINDOW + S), dkv)

    def values_t_padded(col):
        v = qkv[:, col:col + dkv].reshape(B, S, dkv).transpose(0, 2, 1)
        return jnp.pad(v, ((0, 0), (0, 0), (WINDOW, 0))).reshape(B * dkv, WINDOW + S)

    attn = sel_win(qkv, selb, keys_padded(dq + 2 * dkv), values_t_padded(dq + 3 * dkv),
                   keys_padded(dq + 4 * dkv), values_t_padded(dq + 5 * dkv), e2p, tab, ocmp, gl,
                   attn_gnw.reshape(1, dq), B, S)
    cw8 = jnp.pad(conv_w, ((0, 8 - conv_w.shape[0]), (0, 0)))
    conv = conv_mixer(gc, cw8, conv_b.reshape(1, dc), conv_gnw.reshape(1, dc), S, tm_conv)
    return (attn, conv, *casts)


def peer_block(h, ffn_nw, final_nw, peer_wq, peer_subkeys, peer_u, peer_v, tm=TM_PROJ, tt=TT_DENSE):
    T, D = h.shape
    qp = norm_matmul(h, ffn_nw, peer_wq.astype(BF16), BF16, tm, TN_OUT)
    sk = peer_subkeys.reshape(PEER_HEADS * 2, N_KEYS, peer_subkeys.shape[-1]).astype(BF16)
    ex, gates, u_bf, v_bf = peer_route(qp, sk, peer_u, peer_v)
    gm = peer_gbuild(ex, gates)
    return peer_dense(h, ffn_nw, final_nw, u_bf, v_bf, gm, tt)


def kernel(x, attn_norm_w, w_in, w_cmp_k, w_cmp_v, cmp_pos, conv_w, conv_b, attn_group_norm_w,
           conv_group_norm_w, w_out, rel_bias, ffn_norm_w, peer_wq, peer_subkeys, peer_u, peer_v,
           final_norm_w):
    B, S, D = x.shape
    T = B * S
    xt = x.reshape(T, D)
    attn, conv, wo, wq = nsa_conv_mix(xt, B, S, attn_norm_w[0], w_in[0], w_cmp_k[0], w_cmp_v[0], cmp_pos[0],
                                      conv_w[0], conv_b[0], attn_group_norm_w[0], conv_group_norm_w[0], rel_bias,
                                      cast_also=(w_out[0], peer_wq[0]))
    da = attn.shape[1]
    h = out_proj(attn, conv, wo[:da], wo[da:], xt, TM_PROJ, TN_OUT)
    out = peer_block(h, ffn_norm_w[0], final_norm_w, wq, peer_subkeys[0], peer_u[0], peer_v[0])
    return out.reshape(B, S, D)
```

```python
import functools
import math

import jax
import jax.numpy as jnp
import numpy as np
from jax import lax
from jax.experimental import pallas as pl
from jax.experimental.pallas import tpu as pltpu

F32 = jnp.float32
BF16 = jnp.bfloat16

HEAD_DIM = 128
N_KV_HEADS = 2
GQA = 4
N_Q_HEADS = N_KV_HEADS * GQA
L_CMP = 32
STRIDE_CMP = 16
L_SEL = 64
N_SEL = 16
WINDOW = 512
Q_BLOCK = 128
FORCED_SCORE = float(GQA + 1)
NUM_BUCKETS = 32
MAX_DISTANCE = 128
PEER_HEADS = 8
N_KEYS = 128
PEER_TOPK = 16
EPS = 1e-6
NEG_BIG = -1e30
SEL_OFF = -float(2 ** 30)
LOG2E = math.log2(math.e)
N_FIXED_A = 4

SUBLANES = 8
LANES = 128
MXU_DIM = 256
V7X_VMEM_BYTES = 64 * 1024 * 1024
VMEM_LIMIT = V7X_VMEM_BYTES * 7 // 8

HALF_KEYS = N_KEYS // 2
PACK_ROWS = SUBLANES

TM_PROJ = 4 * MXU_DIM
TN_OUT = 4 * MXU_DIM
TM_CONV = 2 * MXU_DIM
TQ_CMP = 4 * Q_BLOCK
TQ_SEL = 2 * Q_BLOCK
TT_ROUTE = LANES
TT_DENSE = 2 * MXU_DIM
CAST_ROWS = MXU_DIM


def _dot(a, b):
    return jnp.dot(a, b, preferred_element_type=F32)


def _dot_nt(a, b):
    return lax.dot_general(a, b, (((1,), (1,)), ((), ())), preferred_element_type=F32)


def _params(sem, vmem=VMEM_LIMIT):
    return pltpu.CompilerParams(dimension_semantics=sem, vmem_limit_bytes=vmem)


def _norm_matmul_kernel(x_ref, nw_ref, w_ref, *rest, w_is_nk, has_side):
    if has_side:
        ws_ref, o_ref, os_ref, xn_ref = rest
    else:
        o_ref, xn_ref = rest

    @pl.when(pl.program_id(1) == 0)
    def _():
        x = x_ref[...]
        y = x * lax.rsqrt(jnp.mean(x * x, axis=-1, keepdims=True) + EPS)
        xn_ref[...] = (y * nw_ref[...]).astype(BF16)
        if has_side:
            os_ref[...] = _dot_nt(xn_ref[...], ws_ref[...])

    dot = _dot_nt if w_is_nk else _dot
    o_ref[...] = dot(xn_ref[...], w_ref[...]).astype(o_ref.dtype)


def norm_matmul(x, norm_w, w, out_dtype, tm, tn, w_is_nk=False, w_side=None):
    T, D = x.shape
    N = w.shape[0] if w_is_nk else w.shape[1]
    w_spec = pl.BlockSpec((tn, D), lambda i, j: (j, 0)) if w_is_nk else pl.BlockSpec((D, tn), lambda i, j: (0, j))
    in_specs = [pl.BlockSpec((tm, D), lambda i, j: (i, 0)), pl.BlockSpec((1, D), lambda i, j: (0, 0)), w_spec]
    out_specs = [pl.BlockSpec((tm, tn), lambda i, j: (i, j))]
    out_shape = [jax.ShapeDtypeStruct((T, N), out_dtype)]
    args = [x, norm_w.reshape(1, D), w]
    if w_side is not None:
        n_side = w_side.shape[0]
        in_specs.append(pl.BlockSpec((n_side, D), lambda i, j: (0, 0)))
        out_specs.append(pl.BlockSpec((tm, n_side), lambda i, j: (i, 0)))
        out_shape.append(jax.ShapeDtypeStruct((T, n_side), F32))
        args.append(w_side)
    out = pl.pallas_call(
        functools.partial(_norm_matmul_kernel, w_is_nk=w_is_nk, has_side=w_side is not None),
        grid=(T // tm, N // tn),
        in_specs=in_specs,
        out_specs=out_specs,
        out_shape=out_shape,
        scratch_shapes=[pltpu.VMEM((tm, D), BF16)],
        compiler_params=_params(("parallel", "arbitrary")),
    )(*args)
    return out if w_side is not None else out[0]


def _cast_rows_kernel(w_ref, o_ref):
    o_ref[...] = w_ref[...].astype(o_ref.dtype)


def cast_rows(w, row0, n_rows, tr=CAST_ROWS):
    K = w.shape[1]
    assert row0 % SUBLANES == 0 and n_rows % tr == 0
    return pl.pallas_call(
        _cast_rows_kernel,
        grid=(n_rows // tr,),
        in_specs=[pl.BlockSpec((pl.Element(tr), pl.Element(K)), lambda i: (pl.multiple_of(row0 + i * tr, SUBLANES), 0))],
        out_specs=pl.BlockSpec((tr, K), lambda i: (i, 0)),
        out_shape=jax.ShapeDtypeStruct((n_rows, K), BF16),
        compiler_params=_params(("parallel",)),
    )(w)


def _compress_kernel(k_ref, v_ref, pa_ref, pb_ref, wka_ref, wkb_ref, wva_ref, wvb_ref, kc_ref, vc_ref):
    def one(x_ref, wa_ref, wb_ref, o_ref):
        x = x_ref[...].astype(F32)
        a = _dot((x + pa_ref[...]).astype(BF16), wa_ref[...])
        b = _dot((x + pb_ref[...]).astype(BF16), wb_ref[...])
        n = b.shape[0]
        o_ref[...] = (a + pltpu.roll(b, n - 1, 0)).astype(o_ref.dtype)

    one(k_ref, wka_ref, wkb_ref, kc_ref)
    one(v_ref, wva_ref, wvb_ref, vc_ref)


def compress(k16, v16, pos_a, pos_b, wka, wkb, wva, wvb, B):
    R, C = k16.shape
    nb = R // B
    dkv = wka.shape[1]
    full = lambda shp: pl.BlockSpec(shp, lambda b: (0, 0))
    return pl.pallas_call(
        _compress_kernel,
        grid=(B,),
        in_specs=[pl.BlockSpec((nb, C), lambda b: (b, 0)),
                  pl.BlockSpec((nb, C), lambda b: (b, 0)),
                  full((1, C)), full((1, C)),
                  full((C, dkv)), full((C, dkv)), full((C, dkv)), full((C, dkv))],
        out_specs=[pl.BlockSpec((nb, dkv), lambda b: (b, 0))] * 2,
        out_shape=[jax.ShapeDtypeStruct((R, dkv), BF16)] * 2,
        compiler_params=_params(("parallel",)),
    )(k16, v16, pos_a, pos_b, wka, wkb, wva, wvb)


def _bias_from_dist(tab_row, dist):
    idx = jnp.clip(dist, 0, 127)
    w = tab_row.shape[1]
    tab = jnp.broadcast_to(tab_row, (idx.shape[0], w))
    parts = [jnp.take_along_axis(tab, idx[:, k:k + w], axis=1) for k in range(0, idx.shape[1], w)]
    return parts[0] if len(parts) == 1 else jnp.concatenate(parts, axis=1)


def _cmp_select_kernel(q_ref, kc_ref, vc_ref, tab_ref, ovt_ref, *rest, n_qb, scale, n_cast):
    cast_in, (ocmp_ref, selb_ref), cast_out = rest[:n_cast], rest[n_cast:n_cast + 2], rest[n_cast + 2:]
    for src_ref, dst_ref in zip(cast_in, cast_out):
        dst_ref[...] = src_ref[...].astype(BF16)
    c = pl.program_id(0) % n_qb
    tq = q_ref.shape[0]
    n_c = kc_ref.shape[0]
    n_sel = ovt_ref.shape[0]
    t0 = c * tq
    t_col = t0 + lax.broadcasted_iota(jnp.int32, (tq, n_c), 0)
    n_row = lax.broadcasted_iota(jnp.int32, (tq, n_c), 1)
    dist = t_col - (n_row * STRIDE_CMP + (L_CMP - 1))
    valid = dist >= 0

    j_io = lax.broadcasted_iota(jnp.int32, (n_sel, tq), 0)
    t_io = t0 + lax.broadcasted_iota(jnp.int32, (n_sel, tq), 1)
    blk_t = t_io // L_SEL
    forced = (j_io == 0) | (j_io == blk_t) | (j_io == blk_t - 1)
    causal_blk = j_io * L_SEL <= t_io

    for h in range(N_KV_HEADS):
        kc = kc_ref[:, h * HEAD_DIM:(h + 1) * HEAD_DIM]
        vc = vc_ref[:, h * HEAD_DIM:(h + 1) * HEAD_DIM]
        psum = jnp.zeros((tq, n_c), F32)
        for g in range(GQA):
            hd = h * GQA + g
            qh = q_ref[:, hd * HEAD_DIM:(hd + 1) * HEAD_DIM]
            bias = _bias_from_dist(tab_ref[hd:hd + 1, :], dist)
            s = _dot_nt(qh, kc) * scale + bias
            s = jnp.where(valid, s, NEG_BIG)
            m = jnp.max(s, axis=-1, keepdims=True)
            e = jnp.where(valid, jnp.exp(s - m), 0.0)
            d = jnp.sum(e, axis=-1, keepdims=True)
            p = e / jnp.where(d > 0, d, 1.0)
            ocmp_ref[:, hd * HEAD_DIM:(hd + 1) * HEAD_DIM] = _dot(p.astype(BF16), vc)
            psum = psum + p
        p_hi = psum.astype(BF16)
        p_lo = (psum - p_hi.astype(F32)).astype(BF16)
        ovt = ovt_ref[...]
        imp = _dot_nt(ovt, p_hi) + _dot_nt(ovt, p_lo)
        score = jnp.where(forced, FORCED_SCORE, jnp.where(causal_blk, imp, -1.0))
        rank = jnp.zeros((n_sel, tq), F32)
        for jp in range(n_sel):
            row = score[jp:jp + 1, :]
            rank = rank + jnp.where(j_io > jp, jnp.where(row >= score, 1.0, 0.0),
                                    jnp.where(row > score, 1.0, 0.0))
        selb = jnp.where(rank < float(N_SEL), 0.0, SEL_OFF)
        if n_sel < HEAD_DIM:
            selb = jnp.concatenate([selb, jnp.zeros((HEAD_DIM - n_sel, tq), F32)], axis=0)
        selb_ref[:, h * HEAD_DIM:(h + 1) * HEAD_DIM] = selb.T.astype(BF16)


def cmp_select(qkv, kc, vc, tab, ovt, B, S, tq=TQ_CMP, cast_also=()):
    T = qkv.shape[0]
    n_qb = S // tq
    steps = T // tq
    n_c = kc.shape[0] // B
    dq = N_Q_HEADS * HEAD_DIM
    dkv = N_KV_HEADS * HEAD_DIM
    kern = functools.partial(_cmp_select_kernel, n_qb=n_qb, scale=HEAD_DIM ** -0.5, n_cast=len(cast_also))
    cast_specs = [pl.BlockSpec((w.shape[0] // steps, w.shape[1]), lambda i: (i, 0)) for w in cast_also]
    assert all(w.shape[0] % (2 * SUBLANES * steps) == 0 for w in cast_also)
    return pl.pallas_call(
        kern,
        grid=(steps,),
        in_specs=[pl.BlockSpec((tq, dq), lambda i: (i, 0)),
                  pl.BlockSpec((n_c, dkv), lambda i: (i // n_qb, 0)),
                  pl.BlockSpec((n_c, dkv), lambda i: (i // n_qb, 0)),
                  pl.BlockSpec(tab.shape, lambda i: (0, 0)),
                  pl.BlockSpec(ovt.shape, lambda i: (0, 0))] + cast_specs,
        out_specs=[pl.BlockSpec((tq, dq), lambda i: (i, 0)),
                   pl.BlockSpec((tq, dkv), lambda i: (i, 0))] + cast_specs,
        out_shape=[jax.ShapeDtypeStruct((T, dq), F32),
                   jax.ShapeDtypeStruct((T, dkv), BF16)] + [jax.ShapeDtypeStruct(w.shape, BF16) for w in cast_also],
        compiler_params=_params(("parallel",)),
    )(qkv, kc, vc, tab, ovt, *cast_also)


def _flash_update(s_raw, vt, bias, m_ref, l_ref, acc_ref, scale):
    m_old = m_ref[...]
    if bias.shape[0] == 1:
        s = s_raw * scale
        m_new = jnp.maximum(m_old, jnp.max(s, axis=0, keepdims=True) + bias)
        p = jnp.exp2(s - (m_new - bias))
    else:
        s = s_raw * scale + bias
        m_new = jnp.maximum(m_old, jnp.max(s, axis=0, keepdims=True))
        p = jnp.exp2(s - m_new)
    alpha = jnp.exp2(m_old - m_new)
    l_ref[...] = alpha * l_ref[...] + jnp.sum(p, axis=0, keepdims=True)
    acc_ref[...] = alpha * acc_ref[...] + _dot(vt, p.astype(BF16))
    m_ref[...] = m_new


def _flash_chunk(k, vt, q, bias, m_ref, l_ref, acc_ref, scale):
    _flash_update(_dot_nt(k, q), vt, bias, m_ref, l_ref, acc_ref, scale)


def _sel_win_kernel(q_ref, selb_ref, ks_ref, vst_ref, kw_ref, vwt_ref, e2_ref, tab_ref, ocmp_ref,
                    gl_ref, gnw_ref, o_ref, m_ref, l_ref, acc_ref, sa_ref, sb_ref, *, n_qb, scale):
    c = pl.program_id(0) % n_qb
    tq = q_ref.shape[0]
    rows = GQA * tq
    n_back = WINDOW // tq
    near = WINDOW + tq
    j_io = lax.broadcasted_iota(jnp.int32, (tq, tq), 0)
    i_io = lax.broadcasted_iota(jnp.int32, (tq, tq), 1)
    dij = i_io - j_io
    dij4 = jnp.concatenate([dij] * GQA, axis=1)
    causal = dij4 >= 0
    pad_col = jnp.where(lax.broadcasted_iota(jnp.int32, (rows, HEAD_DIM), 1) == HEAD_DIM - 1, 1.0, 0.0).astype(BF16)
    sig = jax.nn.sigmoid(gl_ref[...])
    near0 = pl.multiple_of(c * tq, tq)
    n_far = jnp.maximum(c - n_back, 0)
    n_full = n_far // n_back
    n_rem = n_far - n_full * n_back

    def hsl(h):
        return slice(h * HEAD_DIM, (h + 1) * HEAD_DIM)

    def far0(i):
        return pl.multiple_of(WINDOW + i * WINDOW, WINDOW)

    def reset():
        m_ref[...] = jnp.full(m_ref.shape, NEG_BIG, F32)
        l_ref[...] = jnp.zeros(l_ref.shape, F32)
        acc_ref[...] = jnp.zeros(acc_ref.shape, F32)

    def result():
        ot = acc_ref[...] / l_ref[...]
        return [ot[:, g * tq:(g + 1) * tq].T for g in range(GQA)]

    for h in range(N_KV_HEADS):
        q4 = jnp.concatenate([q_ref[:, (h * GQA + g) * HEAD_DIM:(h * GQA + g + 1) * HEAD_DIM]
                              for g in range(GQA)], axis=0)
        sb4 = jnp.concatenate([selb_ref[:, hsl(h)]] * GQA, axis=0)
        q_aug = jnp.concatenate([q4, sb4 + pad_col], axis=1)
        q_win = jnp.concatenate([q4, pad_col], axis=1)
        tabs = [tab_ref[h * GQA + g:h * GQA + g + 1, :] * LOG2E for g in range(GQA)]
        d0 = jnp.concatenate([_bias_from_dist(t, dij) for t in tabs], axis=1)
        d1 = jnp.concatenate([_bias_from_dist(t, dij + tq) for t in tabs], axis=1)
        far = jnp.concatenate([t[:, HEAD_DIM - 1:HEAD_DIM] + jnp.zeros((1, tq), F32) for t in tabs], axis=1)
        far_t = jnp.broadcast_to(far, (tq, rows))
        diag = jnp.where(causal, d0, NEG_BIG)
        bias_sel = jnp.concatenate([far_t] * (n_back - 1) + [d1, diag], axis=0)
        bias_win = jnp.concatenate([jnp.where(dij4 < 0, far_t, NEG_BIG)] + [far_t] * (n_back - 2) + [d1, diag],
                                   axis=0)

        reset()
        e2_near = e2_ref[pl.ds(near0, near), :]
        k_near = jnp.concatenate([ks_ref[pl.ds(near0, near), hsl(h)], e2_near], axis=1)
        _flash_chunk(k_near, vst_ref[hsl(h), pl.ds(near0, near)], q_aug, bias_sel, m_ref, l_ref, acc_ref, scale)

        def far_scores(i):
            r0 = far0(i)
            k = jnp.concatenate([ks_ref[pl.ds(r0, WINDOW), hsl(h)], e2_ref[pl.ds(r0, WINDOW), :]], axis=1)
            return _dot_nt(k, q_aug)

        def far_update(s_raw, i, bias):
            _flash_update(s_raw, vst_ref[hsl(h), pl.ds(far0(i), WINDOW)], bias, m_ref, l_ref, acc_ref, scale)

        sa_ref[...] = far_scores(0)

        def body(i, carry):
            sb_ref[...] = far_scores(2 * i + 1)
            far_update(sa_ref[...], 2 * i, far)
            sa_ref[...] = far_scores(2 * i + 2)
            far_update(sb_ref[...], 2 * i + 1, far)
            return carry

        lax.fori_loop(0, n_full // 2, body, 0)

        @pl.when(n_full % 2 == 1)
        def _():
            far_update(sa_ref[...], n_full - 1, far)

        @pl.when(n_rem > 0)
        def _():
            if n_back == 2:
                r0 = far0(n_full)
                k = jnp.concatenate([ks_ref[pl.ds(r0, tq), hsl(h)], e2_ref[pl.ds(r0, tq), :]], axis=1)
                _flash_chunk(k, vst_ref[hsl(h), pl.ds(r0, tq)], q_aug, far, m_ref, l_ref, acc_ref, scale)
            else:
                live = lax.broadcasted_iota(jnp.int32, (WINDOW, rows), 0) < n_rem * tq
                far_update(far_scores(n_full), n_full,
                           jnp.where(live, jnp.broadcast_to(far, (WINDOW, rows)), NEG_BIG))

        o_sel = result()

        reset()
        kw_near = jnp.concatenate([kw_ref[pl.ds(near0, near), hsl(h)], e2_near], axis=1)
        _flash_chunk(kw_near, vwt_ref[hsl(h), pl.ds(near0, near)], q_win, bias_win, m_ref, l_ref, acc_ref, scale)
        o_win = result()

        for g in range(GQA):
            hd = h * GQA + g
            o = (sig[:, 3 * hd:3 * hd + 1] * ocmp_ref[:, hsl(hd)]
                 + sig[:, 3 * hd + 1:3 * hd + 2] * o_sel[g]
                 + sig[:, 3 * hd + 2:3 * hd + 3] * o_win[g])
            y = o * lax.rsqrt(jnp.mean(o * o, axis=-1, keepdims=True) + EPS)
            o_ref[:, hsl(hd)] = (y * gnw_ref[:, hsl(hd)]).astype(o_ref.dtype)


def sel_win(qkv, selb, ksp, vstp, kwp, vwtp, e2p, tab, ocmp, gl, gnw, B, S, tq=TQ_SEL):
    T = qkv.shape[0]
    n_qb = S // tq
    dq = N_Q_HEADS * HEAD_DIM
    dkv = N_KV_HEADS * HEAD_DIM
    SP = S + WINDOW
    kern = functools.partial(_sel_win_kernel, n_qb=n_qb, scale=HEAD_DIM ** -0.5 * LOG2E)
    k_spec = pl.BlockSpec((SP, dkv), lambda i: (i // n_qb, 0))
    vt_spec = pl.BlockSpec((dkv, SP), lambda i: (i // n_qb, 0))
    return pl.pallas_call(
        kern,
        grid=(T // tq,),
        in_specs=[pl.BlockSpec((tq, dq), lambda i: (i, 0)),
                  pl.BlockSpec((tq, dkv), lambda i: (i, 0)),
                  k_spec, vt_spec, k_spec, vt_spec,
                  pl.BlockSpec((SP, HEAD_DIM), lambda i: (0, 0)),
                  pl.BlockSpec(tab.shape, lambda i: (0, 0)),
                  pl.BlockSpec((tq, dq), lambda i: (i, 0)),
                  pl.BlockSpec((tq, HEAD_DIM), lambda i: (i, 0)),
                  pl.BlockSpec((1, dq), lambda i: (0, 0))],
        out_specs=pl.BlockSpec((tq, dq), lambda i: (i, 0)),
        out_shape=jax.ShapeDtypeStruct((T, dq), BF16),
        scratch_shapes=[pltpu.VMEM((1, GQA * tq), F32), pltpu.VMEM((1, GQA * tq), F32),
                        pltpu.VMEM((HEAD_DIM, GQA * tq), F32),
                        pltpu.VMEM((WINDOW, GQA * tq), F32), pltpu.VMEM((WINDOW, GQA * tq), F32)],
        compiler_params=_params(("parallel",)),
    )(qkv, selb, ksp, vstp, kwp, vwtp, e2p, tab, ocmp, gl, gnw)


def _conv_kernel(b_ref, c_ref, h_ref, cp_ref, hp_ref, cw_ref, cb_ref, gnw_ref, o_ref, u_ref, *, tiles_per_seq):
    tm = b_ref.shape[0]
    first = (pl.program_id(0) % tiles_per_seq) == 0
    u_prev = cp_ref[...] * hp_ref[...]
    u_ref[0:8, :] = jnp.where(first, 0.0, u_prev)
    u = c_ref[...] * h_ref[...]
    u_ref[8:8 + tm, :] = u
    y = (cw_ref[0:1, :] * u_ref[6:6 + tm, :] + cw_ref[1:2, :] * u_ref[7:7 + tm, :]
         + cw_ref[2:3, :] * u + cb_ref[...])
    o = b_ref[...] * y
    n_groups = o.shape[1] // HEAD_DIM
    for g in range(n_groups):
        sl = slice(g * HEAD_DIM, (g + 1) * HEAD_DIM)
        og = o[:, sl]
        yg = og * lax.rsqrt(jnp.mean(og * og, axis=-1, keepdims=True) + EPS)
        o_ref[:, sl] = (yg * gnw_ref[:, sl]).astype(o_ref.dtype)


def conv_mixer(gc, conv_w, conv_b, gnw, S, tm):
    T = gc.shape[0]
    dc = conv_w.shape[1]
    tps = S // tm
    kern = functools.partial(_conv_kernel, tiles_per_seq=tps)
    prev = lambda col: pl.BlockSpec((8, dc), lambda i, col=col: (jnp.maximum(i * (tm // 8) - 1, 0), col))
    cur = lambda col: pl.BlockSpec((tm, dc), lambda i, col=col: (i, col))
    return pl.pallas_call(
        kern,
        grid=(T // tm,),
        in_specs=[cur(0), cur(1), cur(2), prev(1), prev(2),
                  pl.BlockSpec((8, dc), lambda i: (0, 0)),
                  pl.BlockSpec((1, dc), lambda i: (0, 0)),
                  pl.BlockSpec((1, dc), lambda i: (0, 0))],
        out_specs=pl.BlockSpec((tm, dc), lambda i: (i, 0)),
        out_shape=jax.ShapeDtypeStruct((T, dc), BF16),
        scratch_shapes=[pltpu.VMEM((tm + 8, dc), F32)],
        compiler_params=_params(("parallel",)),
    )(gc, gc, gc, gc, gc, conv_w, conv_b, gnw)


def _out_proj_kernel(ma_ref, mc_ref, wa_ref, wc_ref, x_ref, o_ref):
    o_ref[...] = x_ref[...] + _dot(ma_ref[...], wa_ref[...]) + _dot(mc_ref[...], wc_ref[...])


def out_proj(ma, mc, wa, wc, x, tm, tn):
    T, da = ma.shape
    dc = mc.shape[1]
    D = x.shape[1]
    return pl.pallas_call(
        _out_proj_kernel,
        grid=(T // tm, D // tn),
        in_specs=[pl.BlockSpec((tm, da), lambda i, j: (i, 0)),
                  pl.BlockSpec((tm, dc), lambda i, j: (i, 0)),
                  pl.BlockSpec((da, tn), lambda i, j: (0, j)),
                  pl.BlockSpec((dc, tn), lambda i, j: (0, j)),
                  pl.BlockSpec((tm, tn), lambda i, j: (i, j))],
        out_specs=pl.BlockSpec((tm, tn), lambda i, j: (i, j)),
        out_shape=jax.ShapeDtypeStruct((T, D), F32),
        compiler_params=_params(("parallel", "arbitrary")),
    )(ma, mc, wa, wc, x)


def _sort_network(n):
    pairs = []
    p = 1
    while p < n:
        k = p
        while k >= 1:
            for j in range(k % p, n - k, 2 * k):
                for i in range(min(k, n - j - k)):
                    if (i + j) // (2 * p) == (i + j + k) // (2 * p):
                        pairs.append((i + j, i + j + k))
            k //= 2
        p *= 2
    return pairs


def _peer_route_kernel(q_ref, sk_ref, u_ref, v_ref, e_ref, g_ref, ub_ref, vb_ref, sv_ref, si_ref):
    ub_ref[...] = u_ref[...].astype(BF16)
    vb_ref[...] = v_ref[...].astype(BF16)
    tt = q_ref.shape[0]
    K = PEER_TOPK
    sub = SUBLANES
    n_io = lax.broadcasted_iota(jnp.int32, (N_KEYS, tt), 0).astype(F32)
    r8 = lax.broadcasted_iota(jnp.int32, (sub, tt), 0).astype(F32)
    ninf = -jnp.inf
    lens = [K // (a + 1) for a in range(N_FIXED_A)]
    lens += [max(K // (b + 1) - N_FIXED_A, 0) for b in range(sub - N_FIXED_A)]
    assert sum(lens) == sum(K // (a + 1) for a in range(K)) and lens[-1] == 0
    fixed_a = r8 < float(N_FIXED_A)
    list_len = jnp.zeros((sub, tt), F32)
    for row, n in enumerate(lens):
        list_len = jnp.where(r8 == float(row), float(n), list_len)

    for h in range(PEER_HEADS):
        for c in range(2):
            col = (h * 2 + c) * HEAD_DIM
            s = _dot_nt(sk_ref[h * 2 + c], q_ref[:, col:col + HEAD_DIM])
            n_col = N_KEYS // sub
            vals = [s[j * sub:(j + 1) * sub, :] for j in range(n_col)]
            idxs = [n_io[j * sub:(j + 1) * sub, :] for j in range(n_col)]
            for lo, hi in _sort_network(n_col):
                swap = (vals[hi] > vals[lo]) | ((vals[hi] == vals[lo]) & (idxs[hi] < idxs[lo]))
                vals[lo], vals[hi] = jnp.where(swap, vals[hi], vals[lo]), jnp.where(swap, vals[lo], vals[hi])
                idxs[lo], idxs[hi] = jnp.where(swap, idxs[hi], idxs[lo]), jnp.where(swap, idxs[lo], idxs[hi])
            for k in range(K):
                m = jnp.max(vals[0], axis=0, keepdims=True)
                idx = jnp.min(jnp.where(vals[0] == m, idxs[0], float(N_KEYS)), axis=0, keepdims=True)
                sv_ref[c, k:k + 1, :] = m
                si_ref[c, k:k + 1, :] = idx
                win = idxs[0] == idx
                for j in range(min(n_col, K) - 1 - k):
                    vals[j] = jnp.where(win, vals[j + 1], vals[j])
                    idxs[j] = jnp.where(win, idxs[j + 1], idxs[j])
        sv0, sv1 = sv_ref[0], sv_ref[1]
        si0, si1 = si_ref[0], si_ref[1]
        sv1_low = pltpu.roll(sv1[0:sub, :], N_FIXED_A, 0)
        sv0_top = sv0[0:sub, :]
        lv, lf = [], []
        for dep in range(K):
            a_dep = min(N_FIXED_A + dep, K - 1)
            val = (jnp.where(fixed_a, sv0_top, sv0[a_dep:a_dep + 1, :])
                   + jnp.where(fixed_a, sv1[dep:dep + 1, :], sv1_low))
            lv.append(jnp.where(list_len > dep, val, ninf))
            lf.append(jnp.where(fixed_a, r8 * float(K) + dep, (N_FIXED_A + dep) * float(K) + r8 - N_FIXED_A))
        cvs, fls = [], []
        for k in range(K):
            m = jnp.max(lv[0], axis=0, keepdims=True)
            fsel = jnp.min(jnp.where(lv[0] == m, lf[0], 1e9), axis=0, keepdims=True)
            win = lf[0] == fsel
            cvs.append(m)
            fls.append(fsel)
            for dep in range(K - 1 - k):
                lv[dep] = jnp.where(win, lv[dep + 1], lv[dep])
                lf[dep] = jnp.where(win, lf[dep + 1], lf[dep])
        cv = jnp.concatenate(cvs, axis=0)
        fl = jnp.concatenate(fls, axis=0)
        a_sel = jnp.floor(fl * (1.0 / K))
        b_sel = fl - a_sel * K
        i1 = jnp.zeros_like(fl)
        i2 = jnp.zeros_like(fl)
        for r in range(K):
            i1 = jnp.where(a_sel == r, si0[r:r + 1, :], i1)
            i2 = jnp.where(b_sel == r, si1[r:r + 1, :], i2)
        ex = i1 * float(N_KEYS) + i2
        ev = jnp.exp(cv - jnp.max(cv, axis=0, keepdims=True))
        gates = ev / jnp.sum(ev, axis=0, keepdims=True)
        e_ref[0, h * K:(h + 1) * K, :] = ex.astype(jnp.int32)
        g_ref[0, h * K:(h + 1) * K, :] = gates


def peer_route(qp, subkeys, u, v):
    T = qp.shape[0]
    tt = TT_ROUTE
    P = PEER_HEADS * PEER_TOPK
    nt = T // tt
    E, D = u.shape
    slab = E // nt
    assert slab * nt == E and slab % 16 == 0
    tab_spec = pl.BlockSpec((slab, D), lambda i: (i, 0))
    return pl.pallas_call(
        _peer_route_kernel,
        grid=(nt,),
        in_specs=[pl.BlockSpec((tt, qp.shape[1]), lambda i: (i, 0)),
                  pl.BlockSpec(subkeys.shape, lambda i: (0, 0, 0)),
                  tab_spec, tab_spec],
        out_specs=[pl.BlockSpec((1, P, tt), lambda i: (i, 0, 0))] * 2 + [tab_spec, tab_spec],
        out_shape=[jax.ShapeDtypeStruct((nt, P, tt), jnp.int32),
                   jax.ShapeDtypeStruct((nt, P, tt), F32),
                   jax.ShapeDtypeStruct((E, D), BF16), jax.ShapeDtypeStruct((E, D), BF16)],
        scratch_shapes=[pltpu.VMEM((2, PEER_TOPK, tt), F32), pltpu.VMEM((2, PEER_TOPK, tt), F32)],
        compiler_params=_params(("parallel",)),
    )(qp, subkeys, u, v)


def _peer_gbuild_kernel(e_ref, g_ref, o_ref, i1_ref, i2_ref, gt_ref):
    tt = e_ref.shape[2]
    e = e_ref[0].T
    i1_ref[...] = e >> 7
    i2_ref[...] = e & (N_KEYS - 1)
    gt_ref[...] = g_ref[0].T
    P = e.shape[1]
    k_io = lax.broadcasted_iota(jnp.int32, (N_KEYS, P), 0)
    group = 64

    def body(tg, carry):
        base = pl.multiple_of(tg * group, group)
        r1s = i1_ref[pl.ds(base, group), :]
        r2s = i2_ref[pl.ds(base, group), :]
        rgs = gt_ref[pl.ds(base, group), :]
        for u in range(group):
            lhs = jnp.where(k_io == r1s[u:u + 1, :], rgs[u:u + 1, :], 0.0).astype(BF16)
            rhs = jnp.where(k_io == r2s[u:u + 1, :], 1.0, 0.0).astype(BF16)
            g_t = _dot_nt(lhs, rhs).astype(BF16).astype(F32)
            lo = pltpu.bitcast(g_t[:HALF_KEYS], jnp.uint32) >> 16
            hi = pltpu.bitcast(g_t[HALF_KEYS:], jnp.uint32)
            words = hi | lo
            for a in range(HALF_KEYS // PACK_ROWS):
                o_ref[a, base + u] = words[a * PACK_ROWS:(a + 1) * PACK_ROWS, :]
        return carry

    lax.fori_loop(0, tt // group, body, 0)


def peer_gbuild(ex, gates):
    nt, P, tt = ex.shape
    T = nt * tt
    n_a = HALF_KEYS // PACK_ROWS
    return pl.pallas_call(
        _peer_gbuild_kernel,
        grid=(nt,),
        in_specs=[pl.BlockSpec((1, P, tt), lambda i: (i, 0, 0))] * 2,
        out_specs=pl.BlockSpec((n_a, tt, PACK_ROWS, N_KEYS), lambda i: (0, i, 0, 0)),
        out_shape=jax.ShapeDtypeStruct((n_a, T, PACK_ROWS, N_KEYS), jnp.uint32),
        scratch_shapes=[pltpu.VMEM((tt, P), jnp.int32), pltpu.VMEM((tt, P), jnp.int32),
                        pltpu.VMEM((tt, P), F32)],
        compiler_params=_params(("parallel",)),
    )(ex, gates)


def _peer_dense_kernel(h_ref, nw_ref, fw_ref, u_ref, v_ref, gm_ref, o_ref, hn_ref, acc_ref):
    j = pl.program_id(1)
    tt = h_ref.shape[0]

    @pl.when(j == 0)
    def _():
        x = h_ref[...]
        y = x * lax.rsqrt(jnp.mean(x * x, axis=-1, keepdims=True) + EPS)
        hn_ref[...] = (y * nw_ref[...]).astype(BF16)
        acc_ref[...] = jnp.zeros(acc_ref.shape, F32)

    a = _dot_nt(hn_ref[...], u_ref[...])
    shift = (16 * (1 - j % 2)).astype(jnp.uint32)
    gm = jnp.concatenate(
        [pltpu.bitcast((gm_ref[pl.ds(k, tt, stride=PACK_ROWS), :] << shift) & jnp.uint32(0xFFFF0000), F32)
         for k in range(PACK_ROWS)], axis=1)
    w = (gm * jax.nn.gelu(a)).astype(BF16)
    acc_ref[...] += _dot(w, v_ref[...])

    @pl.when(j == pl.num_programs(1) - 1)
    def _():
        x = h_ref[...] + acc_ref[...]
        y = x * lax.rsqrt(jnp.mean(x * x, axis=-1, keepdims=True) + EPS)
        o_ref[...] = y * fw_ref[...]


def peer_dense(h, ffn_nw, final_nw, u, v, gm_words, tt):
    T, D = h.shape
    E = u.shape[0]
    ec = PACK_ROWS * N_KEYS
    n_a = gm_words.shape[0]
    assert E == 2 * n_a * ec
    gm2 = gm_words.reshape(n_a, T * PACK_ROWS, N_KEYS)
    tab_spec = pl.BlockSpec((ec, D), lambda i, j: ((j % 2) * n_a + j // 2, 0))
    return pl.pallas_call(
        _peer_dense_kernel,
        grid=(T // tt, E // ec),
        in_specs=[pl.BlockSpec((tt, D), lambda i, j: (i, 0)),
                  pl.BlockSpec((1, D), lambda i, j: (0, 0)),
                  pl.BlockSpec((1, D), lambda i, j: (0, 0)),
                  tab_spec, tab_spec,
                  pl.BlockSpec((None, tt * PACK_ROWS, N_KEYS), lambda i, j: (j // 2, i, 0))],
        out_specs=pl.BlockSpec((tt, D), lambda i, j: (i, 0)),
        out_shape=jax.ShapeDtypeStruct((T, D), F32),
        scratch_shapes=[pltpu.VMEM((tt, D), BF16), pltpu.VMEM((tt, D), F32)],
        compiler_params=_params(("parallel", "arbitrary")),
    )(h, ffn_nw.reshape(1, D), final_nw.reshape(1, D), u, v, gm2)


def _t5_bucket_np(n_dist):
    d = np.arange(n_dist)
    max_exact = NUM_BUCKETS // 2
    nf = np.maximum(d, 1).astype(np.float64)
    large = max_exact + (np.log(nf / max_exact) / math.log(MAX_DISTANCE / max_exact)
                         * (NUM_BUCKETS - max_exact)).astype(np.int64)
    large = np.minimum(large, NUM_BUCKETS - 1)
    return np.where(d < max_exact, d, large).astype(np.int32)


def _overlap_t_np(S):
    n_c = (S - L_CMP) // STRIDE_CMP + 1
    n_sel = S // L_SEL
    pos = np.arange(n_c)[:, None] * STRIDE_CMP + np.arange(L_CMP)[None, :]
    m = np.zeros((n_c + 1, n_sel), np.float32)
    np.add.at(m, (np.repeat(np.arange(n_c), L_CMP), (pos // L_SEL).reshape(-1)), 1.0 / L_CMP)
    return np.ascontiguousarray(m.T)


def _block_onehot_padded_np(S):
    assert S // L_SEL < HEAD_DIM
    e2 = np.zeros((WINDOW + S, HEAD_DIM), np.float32)
    e2[WINDOW + np.arange(S), np.arange(S) // L_SEL] = 1.0
    e2[:WINDOW, HEAD_DIM - 1] = SEL_OFF
    return e2


def nsa_conv_mix(xt, B, S, attn_norm_w, w_in, w_cmp_k, w_cmp_v, cmp_pos, conv_w, conv_b,
                 attn_gnw, conv_gnw, rel_bias, tm=TM_PROJ, tm_conv=TM_CONV, cast_also=()):
    T, D = xt.shape
    dq = N_Q_HEADS * HEAD_DIM
    dkv = N_KV_HEADS * HEAD_DIM
    n_attn = dq + 6 * dkv
    n_gate = 3 * N_Q_HEADS
    dc = (w_in.shape[1] - n_attn - n_gate) // 3
    w_t = w_in.T
    w_attn = cast_rows(w_t, 0, n_attn)
    w_conv = cast_rows(w_t, n_attn + n_gate, 3 * dc)
    w_gate = jnp.pad(w_t[n_attn:n_attn + n_gate], ((0, HEAD_DIM - n_gate), (0, 0))).astype(BF16)

    qkv = norm_matmul(xt, attn_norm_w, w_attn, BF16, tm, n_attn // 2, w_is_nk=True)
    gc, gl = norm_matmul(xt, attn_norm_w, w_conv, F32, tm, dc, w_is_nk=True, w_side=w_gate)

    n16 = S // STRIDE_CMP
    k16 = qkv[:, dq:dq + dkv].reshape(B * n16, STRIDE_CMP * dkv)
    v16 = qkv[:, dq + dkv:dq + 2 * dkv].reshape(B * n16, STRIDE_CMP * dkv)

    def wbig(w, lo):
        wl = w[lo:lo + STRIDE_CMP]
        eye = jnp.eye(N_KV_HEADS, dtype=w.dtype)
        return jnp.einsum('lde,hg->lhdge', wl, eye).reshape(STRIDE_CMP * dkv, dkv).astype(BF16)

    def posrow(lo):
        p = cmp_pos[lo:lo + STRIDE_CMP]
        return jnp.broadcast_to(p[:, None, :], (STRIDE_CMP, N_KV_HEADS, HEAD_DIM)).reshape(1, STRIDE_CMP * dkv)

    kc, vc = compress(k16, v16, posrow(0), posrow(STRIDE_CMP),
                      wbig(w_cmp_k, 0), wbig(w_cmp_k, STRIDE_CMP),
                      wbig(w_cmp_v, 0), wbig(w_cmp_v, STRIDE_CMP), B)

    tab = rel_bias[_t5_bucket_np(HEAD_DIM)].T
    ovt = jnp.asarray(_overlap_t_np(S), BF16)
    e2p = jnp.asarray(_block_onehot_padded_np(S), BF16)

    ocmp, selb, *casts = cmp_select(qkv, kc, vc, tab, ovt, B, S, cast_also=cast_also)
    def keys_padded(col):
        k = qkv[:, col:col + dkv].reshape(B, S, dkv)
        return jnp.pad(k, ((0, 0), (WINDOW, 0), (0, 0))).reshape(B * (WINDOW + S), dkv)

    def values_t_padded(col):
        v = qkv[:, col:col + dkv].reshape(B, S, dkv).transpose(0, 2, 1)
        return jnp.pad(v, ((0, 0), (0, 0), (WINDOW, 0))).reshape(B * dkv, WINDOW + S)

    attn = sel_win(qkv, selb, keys_padded(dq + 2 * dkv), values_t_padded(dq + 3 * dkv),
                   keys_padded(dq + 4 * dkv), values_t_padded(dq + 5 * dkv), e2p, tab, ocmp, gl,
                   attn_gnw.reshape(1, dq), B, S)
    cw8 = jnp.pad(conv_w, ((0, 8 - conv_w.shape[0]), (0, 0)))
    conv = conv_mixer(gc, cw8, conv_b.reshape(1, dc), conv_gnw.reshape(1, dc), S, tm_conv)
    return (attn, conv, *casts)


def peer_block(h, ffn_nw, final_nw, peer_wq, peer_subkeys, peer_u, peer_v, tm=TM_PROJ, tt=TT_DENSE):
    T, D = h.shape
    qp = norm_matmul(h, ffn_nw, peer_wq.astype(BF16), BF16, tm, TN_OUT)
    sk = peer_subkeys.reshape(PEER_HEADS * 2, N_KEYS, peer_subkeys.shape[-1]).astype(BF16)
    ex, gates, u_bf, v_bf = peer_route(qp, sk, peer_u, peer_v)
    gm = peer_gbuild(ex, gates)
    return peer_dense(h, ffn_nw, final_nw, u_bf, v_bf, gm, tt)


def kernel(x, attn_norm_w, w_in, w_cmp_k, w_cmp_v, cmp_pos, conv_w, conv_b, attn_group_norm_w,
           conv_group_norm_w, w_out, rel_bias, ffn_norm_w, peer_wq, peer_subkeys, peer_u, peer_v,
           final_norm_w):
    B, S, D = x.shape
    T = B * S
    xt = x.reshape(T, D)
    attn, conv, wo, wq = nsa_conv_mix(xt, B, S, attn_norm_w[0], w_in[0], w_cmp_k[0], w_cmp_v[0], cmp_pos[0],
                                      conv_w[0], conv_b[0], attn_group_norm_w[0], conv_group_norm_w[0], rel_bias,
                                      cast_also=(w_out[0], peer_wq[0]))
    da = attn.shape[1]
    h = out_proj(attn, conv, wo[:da], wo[da:], xt, TM_PROJ, TN_OUT)
    out = peer_block(h, ffn_norm_w[0], final_norm_w, wq, peer_subkeys[0], peer_u[0], peer_v[0])
    return out.reshape(B, S, D)
```

```python
import functools
import math

import jax
import jax.numpy as jnp
import numpy as np
from jax import lax
from jax.experimental import pallas as pl
from jax.experimental.pallas import tpu as pltpu

F32 = jnp.float32
BF16 = jnp.bfloat16

HEAD_DIM = 128
N_KV_HEADS = 2
GQA = 4
N_Q_HEADS = N_KV_HEADS * GQA
L_CMP = 32
STRIDE_CMP = 16
L_SEL = 64
N_SEL = 16
WINDOW = 512
Q_BLOCK = 128
FORCED_SCORE = float(GQA + 1)
NUM_BUCKETS = 32
MAX_DISTANCE = 128
PEER_HEADS = 8
N_KEYS = 128
PEER_TOPK = 16
EPS = 1e-6
NEG_BIG = -1e30
SEL_OFF = -float(2 ** 30)
LOG2E = math.log2(math.e)
N_FIXED_A = 4

SUBLANES = 8
LANES = 128
MXU_DIM = 256
V7X_VMEM_BYTES = 64 * 1024 * 1024
VMEM_LIMIT = V7X_VMEM_BYTES * 7 // 8

HALF_KEYS = N_KEYS // 2
PACK_ROWS = SUBLANES

TM_PROJ = 4 * MXU_DIM
TN_OUT = 4 * MXU_DIM
TM_CONV = 2 * MXU_DIM
TQ_CMP = 4 * Q_BLOCK
TQ_SEL = 2 * Q_BLOCK
TT_ROUTE = LANES
TT_DENSE = 2 * MXU_DIM
CAST_ROWS = MXU_DIM


def _dot(a, b):
    return jnp.dot(a, b, preferred_element_type=F32)


def _dot_nt(a, b):
    return lax.dot_general(a, b, (((1,), (1,)), ((), ())), preferred_element_type=F32)


def _params(sem, vmem=VMEM_LIMIT):
    return pltpu.CompilerParams(dimension_semantics=sem, vmem_limit_bytes=vmem)


def _norm_matmul_kernel(x_ref, nw_ref, w_ref, *rest, w_is_nk, has_side):
    if has_side:
        ws_ref, o_ref, os_ref, xn_ref = rest
    else:
        o_ref, xn_ref = rest

    @pl.when(pl.program_id(1) == 0)
    def _():
        x = x_ref[...]
        y = x * lax.rsqrt(jnp.mean(x * x, axis=-1, keepdims=True) + EPS)
        xn_ref[...] = (y * nw_ref[...]).astype(BF16)
        if has_side:
            os_ref[...] = _dot_nt(xn_ref[...], ws_ref[...])

    dot = _dot_nt if w_is_nk else _dot
    o_ref[...] = dot(xn_ref[...], w_ref[...]).astype(o_ref.dtype)


def norm_matmul(x, norm_w, w, out_dtype, tm, tn, w_is_nk=False, w_side=None):
    T, D = x.shape
    N = w.shape[0] if w_is_nk else w.shape[1]
    w_spec = pl.BlockSpec((tn, D), lambda i, j: (j, 0)) if w_is_nk else pl.BlockSpec((D, tn), lambda i, j: (0, j))
    in_specs = [pl.BlockSpec((tm, D), lambda i, j: (i, 0)), pl.BlockSpec((1, D), lambda i, j: (0, 0)), w_spec]
    out_specs = [pl.BlockSpec((tm, tn), lambda i, j: (i, j))]
    out_shape = [jax.ShapeDtypeStruct((T, N), out_dtype)]
    args = [x, norm_w.reshape(1, D), w]
    if w_side is not None:
        n_side = w_side.shape[0]
        in_specs.append(pl.BlockSpec((n_side, D), lambda i, j: (0, 0)))
        out_specs.append(pl.BlockSpec((tm, n_side), lambda i, j: (i, 0)))
        out_shape.append(jax.ShapeDtypeStruct((T, n_side), F32))
        args.append(w_side)
    out = pl.pallas_call(
        functools.partial(_norm_matmul_kernel, w_is_nk=w_is_nk, has_side=w_side is not None),
        grid=(T // tm, N // tn),
        in_specs=in_specs,
        out_specs=out_specs,
        out_shape=out_shape,
        scratch_shapes=[pltpu.VMEM((tm, D), BF16)],
        compiler_params=_params(("parallel", "arbitrary")),
    )(*args)
    return out if w_side is not None else out[0]


def _cast_rows_kernel(w_ref, o_ref):
    o_ref[...] = w_ref[...].astype(o_ref.dtype)


def cast_rows(w, row0, n_rows, tr=CAST_ROWS):
    K = w.shape[1]
    assert row0 % SUBLANES == 0 and n_rows % tr == 0
    return pl.pallas_call(
        _cast_rows_kernel,
        grid=(n_rows // tr,),
        in_specs=[pl.BlockSpec((pl.Element(tr), pl.Element(K)), lambda i: (pl.multiple_of(row0 + i * tr, SUBLANES), 0))],
        out_specs=pl.BlockSpec((tr, K), lambda i: (i, 0)),
        out_shape=jax.ShapeDtypeStruct((n_rows, K), BF16),
        compiler_params=_params(("parallel",)),
    )(w)


def _compress_kernel(k_ref, v_ref, pa_ref, pb_ref, wka_ref, wkb_ref, wva_ref, wvb_ref, kc_ref, vc_ref):
    def one(x_ref, wa_ref, wb_ref, o_ref):
        x = x_ref[...].astype(F32)
        a = _dot((x + pa_ref[...]).astype(BF16), wa_ref[...])
        b = _dot((x + pb_ref[...]).astype(BF16), wb_ref[...])
        n = b.shape[0]
        o_ref[...] = (a + pltpu.roll(b, n - 1, 0)).astype(o_ref.dtype)

    one(k_ref, wka_ref, wkb_ref, kc_ref)
    one(v_ref, wva_ref, wvb_ref, vc_ref)


def compress(k16, v16, pos_a, pos_b, wka, wkb, wva, wvb, B):
    R, C = k16.shape
    nb = R // B
    dkv = wka.shape[1]
    full = lambda shp: pl.BlockSpec(shp, lambda b: (0, 0))
    return pl.pallas_call(
        _compress_kernel,
        grid=(B,),
        in_specs=[pl.BlockSpec((nb, C), lambda b: (b, 0)),
                  pl.BlockSpec((nb, C), lambda b: (b, 0)),
                  full((1, C)), full((1, C)),
                  full((C, dkv)), full((C, dkv)), full((C, dkv)), full((C, dkv))],
        out_specs=[pl.BlockSpec((nb, dkv), lambda b: (b, 0))] * 2,
        out_shape=[jax.ShapeDtypeStruct((R, dkv), BF16)] * 2,
        compiler_params=_params(("parallel",)),
    )(k16, v16, pos_a, pos_b, wka, wkb, wva, wvb)


def _bias_from_dist(tab_row, dist):
    idx = jnp.clip(dist, 0, 127)
    w = tab_row.shape[1]
    tab = jnp.broadcast_to(tab_row, (idx.shape[0], w))
    parts = [jnp.take_along_axis(tab, idx[:, k:k + w], axis=1) for k in range(0, idx.shape[1], w)]
    return parts[0] if len(parts) == 1 else jnp.concatenate(parts, axis=1)


def _cmp_select_kernel(q_ref, kc_ref, vc_ref, tab_ref, ovt_ref, *rest, n_qb, scale, n_cast):
    cast_in, (ocmp_ref, selb_ref), cast_out = rest[:n_cast], rest[n_cast:n_cast + 2], rest[n_cast + 2:]
    for src_ref, dst_ref in zip(cast_in, cast_out):
        dst_ref[...] = src_ref[...].astype(BF16)
    c = pl.program_id(0) % n_qb
    tq = q_ref.shape[0]
    n_c = kc_ref.shape[0]
    n_sel = ovt_ref.shape[0]
    t0 = c * tq
    t_col = t0 + lax.broadcasted_iota(jnp.int32, (tq, n_c), 0)
    n_row = lax.broadcasted_iota(jnp.int32, (tq, n_c), 1)
    dist = t_col - (n_row * STRIDE_CMP + (L_CMP - 1))
    valid = dist >= 0

    j_io = lax.broadcasted_iota(jnp.int32, (n_sel, tq), 0)
    t_io = t0 + lax.broadcasted_iota(jnp.int32, (n_sel, tq), 1)
    blk_t = t_io // L_SEL
    forced = (j_io == 0) | (j_io == blk_t) | (j_io == blk_t - 1)
    causal_blk = j_io * L_SEL <= t_io

    for h in range(N_KV_HEADS):
        kc = kc_ref[:, h * HEAD_DIM:(h + 1) * HEAD_DIM]
        vc = vc_ref[:, h * HEAD_DIM:(h + 1) * HEAD_DIM]
        psum = jnp.zeros((tq, n_c), F32)
        for g in range(GQA):
            hd = h * GQA + g
            qh = q_ref[:, hd * HEAD_DIM:(hd + 1) * HEAD_DIM]
            bias = _bias_from_dist(tab_ref[hd:hd + 1, :], dist)
            s = _dot_nt(qh, kc) * scale + bias
            s = jnp.where(valid, s, NEG_BIG)
            m = jnp.max(s, axis=-1, keepdims=True)
            e = jnp.where(valid, jnp.exp(s - m), 0.0)
            d = jnp.sum(e, axis=-1, keepdims=True)
            p = e / jnp.where(d > 0, d, 1.0)
            ocmp_ref[:, hd * HEAD_DIM:(hd + 1) * HEAD_DIM] = _dot(p.astype(BF16), vc)
            psum = psum + p
        p_hi = psum.astype(BF16)
        p_lo = (psum - p_hi.astype(F32)).astype(BF16)
        ovt = ovt_ref[...]
        imp = _dot_nt(ovt, p_hi) + _dot_nt(ovt, p_lo)
        score = jnp.where(forced, FORCED_SCORE, jnp.where(causal_blk, imp, -1.0))
        rank = jnp.zeros((n_sel, tq), F32)
        for jp in range(n_sel):
            row = score[jp:jp + 1, :]
            rank = rank + jnp.where(j_io > jp, jnp.where(row >= score, 1.0, 0.0),
                                    jnp.where(row > score, 1.0, 0.0))
        selb = jnp.where(rank < float(N_SEL), 0.0, SEL_OFF)
        if n_sel < HEAD_DIM:
            selb = jnp.concatenate([selb, jnp.zeros((HEAD_DIM - n_sel, tq), F32)], axis=0)
        selb_ref[:, h * HEAD_DIM:(h + 1) * HEAD_DIM] = selb.T.astype(BF16)


def cmp_select(qkv, kc, vc, tab, ovt, B, S, tq=TQ_CMP, cast_also=()):
    T = qkv.shape[0]
    n_qb = S // tq
    steps = T // tq
    n_c = kc.shape[0] // B
    dq = N_Q_HEADS * HEAD_DIM
    dkv = N_KV_HEADS * HEAD_DIM
    kern = functools.partial(_cmp_select_kernel, n_qb=n_qb, scale=HEAD_DIM ** -0.5, n_cast=len(cast_also))
    cast_specs = [pl.BlockSpec((w.shape[0] // steps, w.shape[1]), lambda i: (i, 0)) for w in cast_also]
    assert all(w.shape[0] % (2 * SUBLANES * steps) == 0 for w in cast_also)
    return pl.pallas_call(
        kern,
        grid=(steps,),
        in_specs=[pl.BlockSpec((tq, dq), lambda i: (i, 0)),
                  pl.BlockSpec((n_c, dkv), lambda i: (i // n_qb, 0)),
                  pl.BlockSpec((n_c, dkv), lambda i: (i // n_qb, 0)),
                  pl.BlockSpec(tab.shape, lambda i: (0, 0)),
                  pl.BlockSpec(ovt.shape, lambda i: (0, 0))] + cast_specs,
        out_specs=[pl.BlockSpec((tq, dq), lambda i: (i, 0)),
                   pl.BlockSpec((tq, dkv), lambda i: (i, 0))] + cast_specs,
        out_shape=[jax.ShapeDtypeStruct((T, dq), F32),
                   jax.ShapeDtypeStruct((T, dkv), BF16)] + [jax.ShapeDtypeStruct(w.shape, BF16) for w in cast_also],
        compiler_params=_params(("parallel",)),
    )(qkv, kc, vc, tab, ovt, *cast_also)


def _repack_kv_kernel(ks_ref, vs_ref, kw_ref, vw_ref, ksp_ref, vst_ref, kwp_ref, vwt_ref):
    first = pl.program_id(1) == 0

    @pl.when(first)
    def _():
        for ref in (ksp_ref, vst_ref, kwp_ref, vwt_ref):
            ref[...] = jnp.zeros(ref.shape, ref.dtype)

    @pl.when(jnp.logical_not(first))
    def _():
        ksp_ref[...] = ks_ref[...]
        kwp_ref[...] = kw_ref[...]
        vst_ref[...] = vs_ref[...].T
        vwt_ref[...] = vw_ref[...].T


def repack_kv(qkv, B, S, first_col_block):
    dkv = N_KV_HEADS * HEAD_DIM
    nt = S // WINDOW
    src = lambda col: pl.BlockSpec((WINDOW, dkv), lambda b, i, col=col: (b * nt + jnp.maximum(i - 1, 0), col))
    rows = pl.BlockSpec((WINDOW, dkv), lambda b, i: (b * (nt + 1) + i, 0))
    cols = pl.BlockSpec((dkv, WINDOW), lambda b, i: (b, i))
    return pl.pallas_call(
        _repack_kv_kernel,
        grid=(B, nt + 1),
        in_specs=[src(first_col_block + k) for k in range(4)],
        out_specs=[rows, cols, rows, cols],
        out_shape=[jax.ShapeDtypeStruct((B * (S + WINDOW), dkv), qkv.dtype),
                   jax.ShapeDtypeStruct((B * dkv, S + WINDOW), qkv.dtype)] * 2,
        compiler_params=_params(("parallel", "arbitrary")),
    )(qkv, qkv, qkv, qkv)


def _flash_update(s_raw, vt, bias, m_ref, l_ref, acc_ref, scale):
    m_old = m_ref[...]
    if bias.shape[0] == 1:
        s = s_raw * scale
        m_new = jnp.maximum(m_old, jnp.max(s, axis=0, keepdims=True) + bias)
        p = jnp.exp2(s - (m_new - bias))
    else:
        s = s_raw * scale + bias
        m_new = jnp.maximum(m_old, jnp.max(s, axis=0, keepdims=True))
        p = jnp.exp2(s - m_new)
    alpha = jnp.exp2(m_old - m_new)
    l_ref[...] = alpha * l_ref[...] + jnp.sum(p, axis=0, keepdims=True)
    acc_ref[...] = alpha * acc_ref[...] + _dot(vt, p.astype(BF16))
    m_ref[...] = m_new


def _flash_chunk(k, vt, q, bias, m_ref, l_ref, acc_ref, scale):
    _flash_update(_dot_nt(k, q), vt, bias, m_ref, l_ref, acc_ref, scale)


def _sel_win_kernel(q_ref, selb_ref, ks_ref, vst_ref, kw_ref, vwt_ref, e2_ref, tab_ref, ocmp_ref,
                    gl_ref, gnw_ref, o_ref, m_ref, l_ref, acc_ref, sa_ref, sb_ref, *, n_qb, scale):
    c = pl.program_id(0) % n_qb
    tq = q_ref.shape[0]
    rows = GQA * tq
    n_back = WINDOW // tq
    near = WINDOW + tq
    j_io = lax.broadcasted_iota(jnp.int32, (tq, tq), 0)
    i_io = lax.broadcasted_iota(jnp.int32, (tq, tq), 1)
    dij = i_io - j_io
    dij4 = jnp.concatenate([dij] * GQA, axis=1)
    causal = dij4 >= 0
    pad_col = jnp.where(lax.broadcasted_iota(jnp.int32, (rows, HEAD_DIM), 1) == HEAD_DIM - 1, 1.0, 0.0).astype(BF16)
    sig = jax.nn.sigmoid(gl_ref[...])
    near0 = pl.multiple_of(c * tq, tq)
    n_far = jnp.maximum(c - n_back, 0)
    n_full = n_far // n_back
    n_rem = n_far - n_full * n_back

    def hsl(h):
        return slice(h * HEAD_DIM, (h + 1) * HEAD_DIM)

    def far0(i):
        return pl.multiple_of(WINDOW + i * WINDOW, WINDOW)

    def reset():
        m_ref[...] = jnp.full(m_ref.shape, NEG_BIG, F32)
        l_ref[...] = jnp.zeros(l_ref.shape, F32)
        acc_ref[...] = jnp.zeros(acc_ref.shape, F32)

    def result():
        ot = acc_ref[...] / l_ref[...]
        return [ot[:, g * tq:(g + 1) * tq].T for g in range(GQA)]

    for h in range(N_KV_HEADS):
        q4 = jnp.concatenate([q_ref[:, (h * GQA + g) * HEAD_DIM:(h * GQA + g + 1) * HEAD_DIM]
                              for g in range(GQA)], axis=0)
        sb4 = jnp.concatenate([selb_ref[:, hsl(h)]] * GQA, axis=0)
        q_aug = jnp.concatenate([q4, sb4 + pad_col], axis=1)
        q_win = jnp.concatenate([q4, pad_col], axis=1)
        tabs = [tab_ref[h * GQA + g:h * GQA + g + 1, :] * LOG2E for g in range(GQA)]
        d0 = jnp.concatenate([_bias_from_dist(t, dij) for t in tabs], axis=1)
        d1 = jnp.concatenate([_bias_from_dist(t, dij + tq) for t in tabs], axis=1)
        far = jnp.concatenate([t[:, HEAD_DIM - 1:HEAD_DIM] + jnp.zeros((1, tq), F32) for t in tabs], axis=1)
        far_t = jnp.broadcast_to(far, (tq, rows))
        diag = jnp.where(causal, d0, NEG_BIG)
        bias_sel = jnp.concatenate([far_t] * (n_back - 1) + [d1, diag], axis=0)
        bias_win = jnp.concatenate([jnp.where(dij4 < 0, far_t, NEG_BIG)] + [far_t] * (n_back - 2) + [d1, diag],
                                   axis=0)

        reset()
        e2_near = e2_ref[pl.ds(near0, near), :]
        k_near = jnp.concatenate([ks_ref[pl.ds(near0, near), hsl(h)], e2_near], axis=1)
        _flash_chunk(k_near, vst_ref[hsl(h), pl.ds(near0, near)], q_aug, bias_sel, m_ref, l_ref, acc_ref, scale)

        def far_scores(i):
            r0 = far0(i)
            k = jnp.concatenate([ks_ref[pl.ds(r0, WINDOW), hsl(h)], e2_ref[pl.ds(r0, WINDOW), :]], axis=1)
            return _dot_nt(k, q_aug)

        def far_update(s_raw, i, bias):
            _flash_update(s_raw, vst_ref[hsl(h), pl.ds(far0(i), WINDOW)], bias, m_ref, l_ref, acc_ref, scale)

        sa_ref[...] = far_scores(0)

        def body(i, carry):
            sb_ref[...] = far_scores(2 * i + 1)
            far_update(sa_ref[...], 2 * i, far)
            sa_ref[...] = far_scores(2 * i + 2)
            far_update(sb_ref[...], 2 * i + 1, far)
            return carry

        lax.fori_loop(0, n_full // 2, body, 0)

        @pl.when(n_full % 2 == 1)
        def _():
            far_update(sa_ref[...], n_full - 1, far)

        @pl.when(n_rem > 0)
        def _():
            if n_back == 2:
                r0 = far0(n_full)
                k = jnp.concatenate([ks_ref[pl.ds(r0, tq), hsl(h)], e2_ref[pl.ds(r0, tq), :]], axis=1)
                _flash_chunk(k, vst_ref[hsl(h), pl.ds(r0, tq)], q_aug, far, m_ref, l_ref, acc_ref, scale)
            else:
                live = lax.broadcasted_iota(jnp.int32, (WINDOW, rows), 0) < n_rem * tq
                far_update(far_scores(n_full), n_full,
                           jnp.where(live, jnp.broadcast_to(far, (WINDOW, rows)), NEG_BIG))

        o_sel = result()

        reset()
        kw_near = jnp.concatenate([kw_ref[pl.ds(near0, near), hsl(h)], e2_near], axis=1)
        _flash_chunk(kw_near, vwt_ref[hsl(h), pl.ds(near0, near)], q_win, bias_win, m_ref, l_ref, acc_ref, scale)
        o_win = result()

        for g in range(GQA):
            hd = h * GQA + g
            o = (sig[:, 3 * hd:3 * hd + 1] * ocmp_ref[:, hsl(hd)]
                 + sig[:, 3 * hd + 1:3 * hd + 2] * o_sel[g]
                 + sig[:, 3 * hd + 2:3 * hd + 3] * o_win[g])
            y = o * lax.rsqrt(jnp.mean(o * o, axis=-1, keepdims=True) + EPS)
            o_ref[:, hsl(hd)] = (y * gnw_ref[:, hsl(hd)]).astype(o_ref.dtype)


def sel_win(qkv, selb, ksp, vstp, kwp, vwtp, e2p, tab, ocmp, gl, gnw, B, S, tq=TQ_SEL):
    T = qkv.shape[0]
    n_qb = S // tq
    dq = N_Q_HEADS * HEAD_DIM
    dkv = N_KV_HEADS * HEAD_DIM
    SP = S + WINDOW
    kern = functools.partial(_sel_win_kernel, n_qb=n_qb, scale=HEAD_DIM ** -0.5 * LOG2E)
    k_spec = pl.BlockSpec((SP, dkv), lambda i: (i // n_qb, 0))
    vt_spec = pl.BlockSpec((dkv, SP), lambda i: (i // n_qb, 0))
    return pl.pallas_call(
        kern,
        grid=(T // tq,),
        in_specs=[pl.BlockSpec((tq, dq), lambda i: (i, 0)),
                  pl.BlockSpec((tq, dkv), lambda i: (i, 0)),
                  k_spec, vt_spec, k_spec, vt_spec,
                  pl.BlockSpec((SP, HEAD_DIM), lambda i: (0, 0)),
                  pl.BlockSpec(tab.shape, lambda i: (0, 0)),
                  pl.BlockSpec((tq, dq), lambda i: (i, 0)),
                  pl.BlockSpec((tq, HEAD_DIM), lambda i: (i, 0)),
                  pl.BlockSpec((1, dq), lambda i: (0, 0))],
        out_specs=pl.BlockSpec((tq, dq), lambda i: (i, 0)),
        out_shape=jax.ShapeDtypeStruct((T, dq), BF16),
        scratch_shapes=[pltpu.VMEM((1, GQA * tq), F32), pltpu.VMEM((1, GQA * tq), F32),
                        pltpu.VMEM((HEAD_DIM, GQA * tq), F32),
                        pltpu.VMEM((WINDOW, GQA * tq), F32), pltpu.VMEM((WINDOW, GQA * tq), F32)],
        compiler_params=_params(("parallel",)),
    )(qkv, selb, ksp, vstp, kwp, vwtp, e2p, tab, ocmp, gl, gnw)


def _conv_kernel(b_ref, c_ref, h_ref, cp_ref, hp_ref, cw_ref, cb_ref, gnw_ref, o_ref, u_ref, *, tiles_per_seq):
    tm = b_ref.shape[0]
    first = (pl.program_id(0) % tiles_per_seq) == 0
    u_prev = cp_ref[...] * hp_ref[...]
    u_ref[0:8, :] = jnp.where(first, 0.0, u_prev)
    u = c_ref[...] * h_ref[...]
    u_ref[8:8 + tm, :] = u
    y = (cw_ref[0:1, :] * u_ref[6:6 + tm, :] + cw_ref[1:2, :] * u_ref[7:7 + tm, :]
         + cw_ref[2:3, :] * u + cb_ref[...])
    o = b_ref[...] * y
    n_groups = o.shape[1] // HEAD_DIM
    for g in range(n_groups):
        sl = slice(g * HEAD_DIM, (g + 1) * HEAD_DIM)
        og = o[:, sl]
        yg = og * lax.rsqrt(jnp.mean(og * og, axis=-1, keepdims=True) + EPS)
        o_ref[:, sl] = (yg * gnw_ref[:, sl]).astype(o_ref.dtype)


def conv_mixer(gc, conv_w, conv_b, gnw, S, tm):
    T = gc.shape[0]
    dc = conv_w.shape[1]
    tps = S // tm
    kern = functools.partial(_conv_kernel, tiles_per_seq=tps)
    prev = lambda col: pl.BlockSpec((8, dc), lambda i, col=col: (jnp.maximum(i * (tm // 8) - 1, 0), col))
    cur = lambda col: pl.BlockSpec((tm, dc), lambda i, col=col: (i, col))
    return pl.pallas_call(
        kern,
        grid=(T // tm,),
        in_specs=[cur(0), cur(1), cur(2), prev(1), prev(2),
                  pl.BlockSpec((8, dc), lambda i: (0, 0)),
                  pl.BlockSpec((1, dc), lambda i: (0, 0)),
                  pl.BlockSpec((1, dc), lambda i: (0, 0))],
        out_specs=pl.BlockSpec((tm, dc), lambda i: (i, 0)),
        out_shape=jax.ShapeDtypeStruct((T, dc), BF16),
        scratch_shapes=[pltpu.VMEM((tm + 8, dc), F32)],
        compiler_params=_params(("parallel",)),
    )(gc, gc, gc, gc, gc, conv_w, conv_b, gnw)


def _out_proj_kernel(ma_ref, mc_ref, wa_ref, wc_ref, x_ref, o_ref):
    o_ref[...] = x_ref[...] + _dot(ma_ref[...], wa_ref[...]) + _dot(mc_ref[...], wc_ref[...])


def out_proj(ma, mc, wa, wc, x, tm, tn):
    T, da = ma.shape
    dc = mc.shape[1]
    D = x.shape[1]
    return pl.pallas_call(
        _out_proj_kernel,
        grid=(T // tm, D // tn),
        in_specs=[pl.BlockSpec((tm, da), lambda i, j: (i, 0)),
                  pl.BlockSpec((tm, dc), lambda i, j: (i, 0)),
                  pl.BlockSpec((da, tn), lambda i, j: (0, j)),
                  pl.BlockSpec((dc, tn), lambda i, j: (0, j)),
                  pl.BlockSpec((tm, tn), lambda i, j: (i, j))],
        out_specs=pl.BlockSpec((tm, tn), lambda i, j: (i, j)),
        out_shape=jax.ShapeDtypeStruct((T, D), F32),
        compiler_params=_params(("parallel", "arbitrary")),
    )(ma, mc, wa, wc, x)


def _sort_network(n):
    pairs = []
    p = 1
    while p < n:
        k = p
        while k >= 1:
            for j in range(k % p, n - k, 2 * k):
                for i in range(min(k, n - j - k)):
                    if (i + j) // (2 * p) == (i + j + k) // (2 * p):
                        pairs.append((i + j, i + j + k))
            k //= 2
        p *= 2
    return pairs


def _peer_route_kernel(q_ref, sk_ref, u_ref, v_ref, e_ref, g_ref, ub_ref, vb_ref, sv_ref, si_ref):
    ub_ref[...] = u_ref[...].astype(BF16)
    vb_ref[...] = v_ref[...].astype(BF16)
    tt = q_ref.shape[0]
    K = PEER_TOPK
    sub = SUBLANES
    n_io = lax.broadcasted_iota(jnp.int32, (N_KEYS, tt), 0).astype(F32)
    r8 = lax.broadcasted_iota(jnp.int32, (sub, tt), 0).astype(F32)
    ninf = -jnp.inf
    lens = [K // (a + 1) for a in range(N_FIXED_A)]
    lens += [max(K // (b + 1) - N_FIXED_A, 0) for b in range(sub - N_FIXED_A)]
    assert sum(lens) == sum(K // (a + 1) for a in range(K)) and lens[-1] == 0
    fixed_a = r8 < float(N_FIXED_A)
    list_len = jnp.zeros((sub, tt), F32)
    for row, n in enumerate(lens):
        list_len = jnp.where(r8 == float(row), float(n), list_len)

    for h in range(PEER_HEADS):
        for c in range(2):
            col = (h * 2 + c) * HEAD_DIM
            s = _dot_nt(sk_ref[h * 2 + c], q_ref[:, col:col + HEAD_DIM])
            n_col = N_KEYS // sub
            vals = [s[j * sub:(j + 1) * sub, :] for j in range(n_col)]
            idxs = [n_io[j * sub:(j + 1) * sub, :] for j in range(n_col)]
            for lo, hi in _sort_network(n_col):
                swap = (vals[hi] > vals[lo]) | ((vals[hi] == vals[lo]) & (idxs[hi] < idxs[lo]))
                vals[lo], vals[hi] = jnp.where(swap, vals[hi], vals[lo]), jnp.where(swap, vals[lo], vals[hi])
                idxs[lo], idxs[hi] = jnp.where(swap, idxs[hi], idxs[lo]), jnp.where(swap, idxs[lo], idxs[hi])
            for k in range(K):
                m = jnp.max(vals[0], axis=0, keepdims=True)
                idx = jnp.min(jnp.where(vals[0] == m, idxs[0], float(N_KEYS)), axis=0, keepdims=True)
                sv_ref[c, k:k + 1, :] = m
                si_ref[c, k:k + 1, :] = idx
                win = idxs[0] == idx
                for j in range(min(n_col, K) - 1 - k):
                    vals[j] = jnp.where(win, vals[j + 1], vals[j])
                    idxs[j] = jnp.where(win, idxs[j + 1], idxs[j])
        sv0, sv1 = sv_ref[0], sv_ref[1]
        si0, si1 = si_ref[0], si_ref[1]
        sv1_low = pltpu.roll(sv1[0:sub, :], N_FIXED_A, 0)
        sv0_top = sv0[0:sub, :]
        lv, lf = [], []
        for dep in range(K):
            a_dep = min(N_FIXED_A + dep, K - 1)
            val = (jnp.where(fixed_a, sv0_top, sv0[a_dep:a_dep + 1, :])
                   + jnp.where(fixed_a, sv1[dep:dep + 1, :], sv1_low))
            lv.append(jnp.where(list_len > dep, val, ninf))
            lf.append(jnp.where(fixed_a, r8 * float(K) + dep, (N_FIXED_A + dep) * float(K) + r8 - N_FIXED_A))
        cvs, fls = [], []
        for k in range(K):
            m = jnp.max(lv[0], axis=0, keepdims=True)
            fsel = jnp.min(jnp.where(lv[0] == m, lf[0], 1e9), axis=0, keepdims=True)
            win = lf[0] == fsel
            cvs.append(m)
            fls.append(fsel)
            for dep in range(K - 1 - k):
                lv[dep] = jnp.where(win, lv[dep + 1], lv[dep])
                lf[dep] = jnp.where(win, lf[dep + 1], lf[dep])
        cv = jnp.concatenate(cvs, axis=0)
        fl = jnp.concatenate(fls, axis=0)
        a_sel = jnp.floor(fl * (1.0 / K))
        b_sel = fl - a_sel * K
        i1 = jnp.zeros_like(fl)
        i2 = jnp.zeros_like(fl)
        for r in range(K):
            i1 = jnp.where(a_sel == r, si0[r:r + 1, :], i1)
            i2 = jnp.where(b_sel == r, si1[r:r + 1, :], i2)
        ex = i1 * float(N_KEYS) + i2
        ev = jnp.exp(cv - jnp.max(cv, axis=0, keepdims=True))
        gates = ev / jnp.sum(ev, axis=0, keepdims=True)
        e_ref[0, h * K:(h + 1) * K, :] = ex.astype(jnp.int32)
        g_ref[0, h * K:(h + 1) * K, :] = gates


def peer_route(qp, subkeys, u, v):
    T = qp.shape[0]
    tt = TT_ROUTE
    P = PEER_HEADS * PEER_TOPK
    nt = T // tt
    E, D = u.shape
    slab = E // nt
    assert slab * nt == E and slab % 16 == 0
    tab_spec = pl.BlockSpec((slab, D), lambda i: (i, 0))
    return pl.pallas_call(
        _peer_route_kernel,
        grid=(nt,),
        in_specs=[pl.BlockSpec((tt, qp.shape[1]), lambda i: (i, 0)),
                  pl.BlockSpec(subkeys.shape, lambda i: (0, 0, 0)),
                  tab_spec, tab_spec],
        out_specs=[pl.BlockSpec((1, P, tt), lambda i: (i, 0, 0))] * 2 + [tab_spec, tab_spec],
        out_shape=[jax.ShapeDtypeStruct((nt, P, tt), jnp.int32),
                   jax.ShapeDtypeStruct((nt, P, tt), F32),
                   jax.ShapeDtypeStruct((E, D), BF16), jax.ShapeDtypeStruct((E, D), BF16)],
        scratch_shapes=[pltpu.VMEM((2, PEER_TOPK, tt), F32), pltpu.VMEM((2, PEER_TOPK, tt), F32)],
        compiler_params=_params(("parallel",)),
    )(qp, subkeys, u, v)


def _peer_gbuild_kernel(e_ref, g_ref, o_ref, i1_ref, i2_ref, gt_ref):
    tt = e_ref.shape[2]
    e = e_ref[0].T
    i1_ref[...] = e >> 7
    i2_ref[...] = e & (N_KEYS - 1)
    gt_ref[...] = g_ref[0].T
    P = e.shape[1]
    k_io = lax.broadcasted_iota(jnp.int32, (N_KEYS, P), 0)
    group = 64

    def body(tg, carry):
        base = pl.multiple_of(tg * group, group)
        r1s = i1_ref[pl.ds(base, group), :]
        r2s = i2_ref[pl.ds(base, group), :]
        rgs = gt_ref[pl.ds(base, group), :]
        for u in range(group):
            lhs = jnp.where(k_io == r1s[u:u + 1, :], rgs[u:u + 1, :], 0.0).astype(BF16)
            rhs = jnp.where(k_io == r2s[u:u + 1, :], 1.0, 0.0).astype(BF16)
            g_t = _dot_nt(lhs, rhs).astype(BF16).astype(F32)
            lo = pltpu.bitcast(g_t[:HALF_KEYS], jnp.uint32) >> 16
            hi = pltpu.bitcast(g_t[HALF_KEYS:], jnp.uint32)
            words = hi | lo
            for a in range(HALF_KEYS // PACK_ROWS):
                o_ref[a, base + u] = words[a * PACK_ROWS:(a + 1) * PACK_ROWS, :]
        return carry

    lax.fori_loop(0, tt // group, body, 0)


def peer_gbuild(ex, gates):
    nt, P, tt = ex.shape
    T = nt * tt
    n_a = HALF_KEYS // PACK_ROWS
    return pl.pallas_call(
        _peer_gbuild_kernel,
        grid=(nt,),
        in_specs=[pl.BlockSpec((1, P, tt), lambda i: (i, 0, 0))] * 2,
        out_specs=pl.BlockSpec((n_a, tt, PACK_ROWS, N_KEYS), lambda i: (0, i, 0, 0)),
        out_shape=jax.ShapeDtypeStruct((n_a, T, PACK_ROWS, N_KEYS), jnp.uint32),
        scratch_shapes=[pltpu.VMEM((tt, P), jnp.int32), pltpu.VMEM((tt, P), jnp.int32),
                        pltpu.VMEM((tt, P), F32)],
        compiler_params=_params(("parallel",)),
    )(ex, gates)


def _peer_dense_kernel(h_ref, nw_ref, fw_ref, u_ref, v_ref, gm_ref, o_ref, hn_ref, acc_ref):
    j = pl.program_id(1)
    tt = h_ref.shape[0]

    @pl.when(j == 0)
    def _():
        x = h_ref[...]
        y = x * lax.rsqrt(jnp.mean(x * x, axis=-1, keepdims=True) + EPS)
        hn_ref[...] = (y * nw_ref[...]).astype(BF16)
        acc_ref[...] = jnp.zeros(acc_ref.shape, F32)

    a = _dot_nt(hn_ref[...], u_ref[...])
    shift = (16 * (1 - j % 2)).astype(jnp.uint32)
    gm = jnp.concatenate(
        [pltpu.bitcast((gm_ref[pl.ds(k, tt, stride=PACK_ROWS), :] << shift) & jnp.uint32(0xFFFF0000), F32)
         for k in range(PACK_ROWS)], axis=1)
    w = (gm * jax.nn.gelu(a)).astype(BF16)
    acc_ref[...] += _dot(w, v_ref[...])

    @pl.when(j == pl.num_programs(1) - 1)
    def _():
        x = h_ref[...] + acc_ref[...]
        y = x * lax.rsqrt(jnp.mean(x * x, axis=-1, keepdims=True) + EPS)
        o_ref[...] = y * fw_ref[...]


def peer_dense(h, ffn_nw, final_nw, u, v, gm_words, tt):
    T, D = h.shape
    E = u.shape[0]
    ec = PACK_ROWS * N_KEYS
    n_a = gm_words.shape[0]
    assert E == 2 * n_a * ec
    gm2 = gm_words.reshape(n_a, T * PACK_ROWS, N_KEYS)
    tab_spec = pl.BlockSpec((ec, D), lambda i, j: ((j % 2) * n_a + j // 2, 0))
    return pl.pallas_call(
        _peer_dense_kernel,
        grid=(T // tt, E // ec),
        in_specs=[pl.BlockSpec((tt, D), lambda i, j: (i, 0)),
                  pl.BlockSpec((1, D), lambda i, j: (0, 0)),
                  pl.BlockSpec((1, D), lambda i, j: (0, 0)),
                  tab_spec, tab_spec,
                  pl.BlockSpec((None, tt * PACK_ROWS, N_KEYS), lambda i, j: (j // 2, i, 0))],
        out_specs=pl.BlockSpec((tt, D), lambda i, j: (i, 0)),
        out_shape=jax.ShapeDtypeStruct((T, D), F32),
        scratch_shapes=[pltpu.VMEM((tt, D), BF16), pltpu.VMEM((tt, D), F32)],
        compiler_params=_params(("parallel", "arbitrary")),
    )(h, ffn_nw.reshape(1, D), final_nw.reshape(1, D), u, v, gm2)


def _t5_bucket_np(n_dist):
    d = np.arange(n_dist)
    max_exact = NUM_BUCKETS // 2
    nf = np.maximum(d, 1).astype(np.float64)
    large = max_exact + (np.log(nf / max_exact) / math.log(MAX_DISTANCE / max_exact)
                         * (NUM_BUCKETS - max_exact)).astype(np.int64)
    large = np.minimum(large, NUM_BUCKETS - 1)
    return np.where(d < max_exact, d, large).astype(np.int32)


def _overlap_t_np(S):
    n_c = (S - L_CMP) // STRIDE_CMP + 1
    n_sel = S // L_SEL
    pos = np.arange(n_c)[:, None] * STRIDE_CMP + np.arange(L_CMP)[None, :]
    m = np.zeros((n_c + 1, n_sel), np.float32)
    np.add.at(m, (np.repeat(np.arange(n_c), L_CMP), (pos // L_SEL).reshape(-1)), 1.0 / L_CMP)
    return np.ascontiguousarray(m.T)


def _block_onehot_padded_np(S):
    assert S // L_SEL < HEAD_DIM
    e2 = np.zeros((WINDOW + S, HEAD_DIM), np.float32)
    e2[WINDOW + np.arange(S), np.arange(S) // L_SEL] = 1.0
    e2[:WINDOW, HEAD_DIM - 1] = SEL_OFF
    return e2


def nsa_conv_mix(xt, B, S, attn_norm_w, w_in, w_cmp_k, w_cmp_v, cmp_pos, conv_w, conv_b,
                 attn_gnw, conv_gnw, rel_bias, tm=TM_PROJ, tm_conv=TM_CONV, cast_also=()):
    T, D = xt.shape
    dq = N_Q_HEADS * HEAD_DIM
    dkv = N_KV_HEADS * HEAD_DIM
    n_attn = dq + 6 * dkv
    n_gate = 3 * N_Q_HEADS
    dc = (w_in.shape[1] - n_attn - n_gate) // 3
    w_t = w_in.T
    w_attn = cast_rows(w_t, 0, n_attn)
    w_conv = cast_rows(w_t, n_attn + n_gate, 3 * dc)
    w_gate = jnp.pad(w_t[n_attn:n_attn + n_gate], ((0, HEAD_DIM - n_gate), (0, 0))).astype(BF16)

    qkv = norm_matmul(xt, attn_norm_w, w_attn, BF16, tm, n_attn // 2, w_is_nk=True)
    gc, gl = norm_matmul(xt, attn_norm_w, w_conv, F32, tm, dc, w_is_nk=True, w_side=w_gate)

    n16 = S // STRIDE_CMP
    k16 = qkv[:, dq:dq + dkv].reshape(B * n16, STRIDE_CMP * dkv)
    v16 = qkv[:, dq + dkv:dq + 2 * dkv].reshape(B * n16, STRIDE_CMP * dkv)

    def wbig(w, lo):
        wl = w[lo:lo + STRIDE_CMP]
        eye = jnp.eye(N_KV_HEADS, dtype=w.dtype)
        return jnp.einsum('lde,hg->lhdge', wl, eye).reshape(STRIDE_CMP * dkv, dkv).astype(BF16)

    def posrow(lo):
        p = cmp_pos[lo:lo + STRIDE_CMP]
        return jnp.broadcast_to(p[:, None, :], (STRIDE_CMP, N_KV_HEADS, HEAD_DIM)).reshape(1, STRIDE_CMP * dkv)

    kc, vc = compress(k16, v16, posrow(0), posrow(STRIDE_CMP),
                      wbig(w_cmp_k, 0), wbig(w_cmp_k, STRIDE_CMP),
                      wbig(w_cmp_v, 0), wbig(w_cmp_v, STRIDE_CMP), B)

    tab = rel_bias[_t5_bucket_np(HEAD_DIM)].T
    ovt = jnp.asarray(_overlap_t_np(S), BF16)
    e2p = jnp.asarray(_block_onehot_padded_np(S), BF16)

    ocmp, selb, *casts = cmp_select(qkv, kc, vc, tab, ovt, B, S, cast_also=cast_also)
    ksp, vstp, kwp, vwtp = repack_kv(qkv, B, S, first_col_block=dq // dkv + 2)
    attn = sel_win(qkv, selb, ksp, vstp, kwp, vwtp, e2p, tab, ocmp, gl, attn_gnw.reshape(1, dq), B, S)
    cw8 = jnp.pad(conv_w, ((0, 8 - conv_w.shape[0]), (0, 0)))
    conv = conv_mixer(gc, cw8, conv_b.reshape(1, dc), conv_gnw.reshape(1, dc), S, tm_conv)
    return (attn, conv, *casts)


def peer_block(h, ffn_nw, final_nw, peer_wq, peer_subkeys, peer_u, peer_v, tm=TM_PROJ, tt=TT_DENSE):
    T, D = h.shape
    qp = norm_matmul(h, ffn_nw, peer_wq.astype(BF16), BF16, tm, TN_OUT)
    sk = peer_subkeys.reshape(PEER_HEADS * 2, N_KEYS, peer_subkeys.shape[-1]).astype(BF16)
    ex, gates, u_bf, v_bf = peer_route(qp, sk, peer_u, peer_v)
    gm = peer_gbuild(ex, gates)
    return peer_dense(h, ffn_nw, final_nw, u_bf, v_bf, gm, tt)


def kernel(x, attn_norm_w, w_in, w_cmp_k, w_cmp_v, cmp_pos, conv_w, conv_b, attn_group_norm_w,
           conv_group_norm_w, w_out, rel_bias, ffn_norm_w, peer_wq, peer_subkeys, peer_u, peer_v,
           final_norm_w):
    B, S, D = x.shape
    T = B * S
    xt = x.reshape(T, D)
    attn, conv, wo, wq = nsa_conv_mix(xt, B, S, attn_norm_w[0], w_in[0], w_cmp_k[0], w_cmp_v[0], cmp_pos[0],
                                      conv_w[0], conv_b[0], attn_group_norm_w[0], conv_group_norm_w[0], rel_bias,
                                      cast_also=(w_out[0], peer_wq[0]))
    da = attn.shape[1]
    h = out_proj(attn, conv, wo[:da], wo[da:], xt, TM_PROJ, TN_OUT)
    out = peer_block(h, ffn_norm_w[0], final_norm_w, wq, peer_subkeys[0], peer_u[0], peer_v[0])
    return out.reshape(B, S, D)
```

```python
import functools
import math

import jax
import jax.numpy as jnp
import numpy as np
from jax import lax
from jax.experimental import pallas as pl
from jax.experimental.pallas import tpu as pltpu

F32 = jnp.float32
BF16 = jnp.bfloat16

HEAD_DIM = 128
N_KV_HEADS = 2
GQA = 4
N_Q_HEADS = N_KV_HEADS * GQA
L_CMP = 32
STRIDE_CMP = 16
L_SEL = 64
N_SEL = 16
WINDOW = 512
Q_BLOCK = 128
FORCED_SCORE = float(GQA + 1)
NUM_BUCKETS = 32
MAX_DISTANCE = 128
PEER_HEADS = 8
N_KEYS = 128
PEER_TOPK = 16
EPS = 1e-6
NEG_BIG = -1e30
SEL_OFF = -float(2 ** 30)
LOG2E = math.log2(math.e)
N_FIXED_A = 4

SUBLANES = 8
LANES = 128
MXU_DIM = 256
V7X_VMEM_BYTES = 64 * 1024 * 1024
VMEM_LIMIT = V7X_VMEM_BYTES * 7 // 8

HALF_KEYS = N_KEYS // 2
PACK_ROWS = SUBLANES

TM_PROJ = 4 * MXU_DIM
TN_OUT = 4 * MXU_DIM
TM_CONV = 2 * MXU_DIM
TQ_CMP = 4 * Q_BLOCK
TQ_SEL = 2 * Q_BLOCK
TT_ROUTE = LANES
TT_DENSE = 2 * MXU_DIM
CAST_ROWS = MXU_DIM


def _dot(a, b):
    return jnp.dot(a, b, preferred_element_type=F32)


def _dot_nt(a, b):
    return lax.dot_general(a, b, (((1,), (1,)), ((), ())), preferred_element_type=F32)


def _params(sem, vmem=VMEM_LIMIT):
    return pltpu.CompilerParams(dimension_semantics=sem, vmem_limit_bytes=vmem)


def _norm_matmul_kernel(x_ref, nw_ref, w_ref, *rest, w_is_nk, has_side):
    if has_side:
        ws_ref, o_ref, os_ref, xn_ref = rest
    else:
        o_ref, xn_ref = rest

    @pl.when(pl.program_id(1) == 0)
    def _():
        x = x_ref[...]
        y = x * lax.rsqrt(jnp.mean(x * x, axis=-1, keepdims=True) + EPS)
        xn_ref[...] = (y * nw_ref[...]).astype(BF16)
        if has_side:
            os_ref[...] = _dot_nt(xn_ref[...], ws_ref[...])

    dot = _dot_nt if w_is_nk else _dot
    o_ref[...] = dot(xn_ref[...], w_ref[...]).astype(o_ref.dtype)


def norm_matmul(x, norm_w, w, out_dtype, tm, tn, w_is_nk=False, w_side=None):
    T, D = x.shape
    N = w.shape[0] if w_is_nk else w.shape[1]
    w_spec = pl.BlockSpec((tn, D), lambda i, j: (j, 0)) if w_is_nk else pl.BlockSpec((D, tn), lambda i, j: (0, j))
    in_specs = [pl.BlockSpec((tm, D), lambda i, j: (i, 0)), pl.BlockSpec((1, D), lambda i, j: (0, 0)), w_spec]
    out_specs = [pl.BlockSpec((tm, tn), lambda i, j: (i, j))]
    out_shape = [jax.ShapeDtypeStruct((T, N), out_dtype)]
    args = [x, norm_w.reshape(1, D), w]
    if w_side is not None:
        n_side = w_side.shape[0]
        in_specs.append(pl.BlockSpec((n_side, D), lambda i, j: (0, 0)))
        out_specs.append(pl.BlockSpec((tm, n_side), lambda i, j: (i, 0)))
        out_shape.append(jax.ShapeDtypeStruct((T, n_side), F32))
        args.append(w_side)
    out = pl.pallas_call(
        functools.partial(_norm_matmul_kernel, w_is_nk=w_is_nk, has_side=w_side is not None),
        grid=(T // tm, N // tn),
        in_specs=in_specs,
        out_specs=out_specs,
        out_shape=out_shape,
        scratch_shapes=[pltpu.VMEM((tm, D), BF16)],
        compiler_params=_params(("parallel", "arbitrary")),
    )(*args)
    return out if w_side is not None else out[0]


def _cast_rows_kernel(w_ref, o_ref):
    o_ref[...] = w_ref[...].astype(o_ref.dtype)


def cast_rows(w, row0, n_rows, tr=CAST_ROWS):
    K = w.shape[1]
    assert row0 % SUBLANES == 0 and n_rows % tr == 0
    return pl.pallas_call(
        _cast_rows_kernel,
        grid=(n_rows // tr,),
        in_specs=[pl.BlockSpec((pl.Element(tr), pl.Element(K)), lambda i: (pl.multiple_of(row0 + i * tr, SUBLANES), 0))],
        out_specs=pl.BlockSpec((tr, K), lambda i: (i, 0)),
        out_shape=jax.ShapeDtypeStruct((n_rows, K), BF16),
        compiler_params=_params(("parallel",)),
    )(w)


def _compress_kernel(k_ref, v_ref, pa_ref, pb_ref, wka_ref, wkb_ref, wva_ref, wvb_ref, kc_ref, vc_ref):
    dkv = kc_ref.shape[1]

    def one(x_ref, wa_ref, wb_ref, o_ref):
        for h in range(N_KV_HEADS):
            x = jnp.concatenate([x_ref[:, l * dkv + h * HEAD_DIM:l * dkv + (h + 1) * HEAD_DIM]
                                 for l in range(STRIDE_CMP)], axis=1).astype(F32)
            a = _dot((x + pa_ref[...]).astype(BF16), wa_ref[...])
            b = _dot((x + pb_ref[...]).astype(BF16), wb_ref[...])
            n = b.shape[0]
            o_ref[:, h * HEAD_DIM:(h + 1) * HEAD_DIM] = (a + pltpu.roll(b, n - 1, 0)).astype(o_ref.dtype)

    one(k_ref, wka_ref, wkb_ref, kc_ref)
    one(v_ref, wva_ref, wvb_ref, vc_ref)


def compress(k16, v16, pos_a, pos_b, wka, wkb, wva, wvb, B):
    R, C = k16.shape
    nb = R // B
    dkv = C // STRIDE_CMP
    full = lambda shp: pl.BlockSpec(shp, lambda b: (0, 0))
    return pl.pallas_call(
        _compress_kernel,
        grid=(B,),
        in_specs=[pl.BlockSpec((nb, C), lambda b: (b, 0)),
                  pl.BlockSpec((nb, C), lambda b: (b, 0)),
                  full(pos_a.shape), full(pos_b.shape),
                  full(wka.shape), full(wkb.shape), full(wva.shape), full(wvb.shape)],
        out_specs=[pl.BlockSpec((nb, dkv), lambda b: (b, 0))] * 2,
        out_shape=[jax.ShapeDtypeStruct((R, dkv), BF16)] * 2,
        compiler_params=_params(("parallel",)),
    )(k16, v16, pos_a, pos_b, wka, wkb, wva, wvb)


def _bias_from_dist(tab_row, dist):
    idx = jnp.clip(dist, 0, 127)
    w = tab_row.shape[1]
    tab = jnp.broadcast_to(tab_row, (idx.shape[0], w))
    parts = [jnp.take_along_axis(tab, idx[:, k:k + w], axis=1) for k in range(0, idx.shape[1], w)]
    return parts[0] if len(parts) == 1 else jnp.concatenate(parts, axis=1)


def _cmp_select_kernel(q_ref, kc_ref, vc_ref, tab_ref, ovt_ref, *rest, n_qb, scale, n_cast):
    cast_in, (ocmp_ref, selb_ref), cast_out = rest[:n_cast], rest[n_cast:n_cast + 2], rest[n_cast + 2:]
    for src_ref, dst_ref in zip(cast_in, cast_out):
        dst_ref[...] = src_ref[...].astype(BF16)
    c = pl.program_id(0) % n_qb
    tq = q_ref.shape[0]
    n_c = kc_ref.shape[0]
    n_sel = ovt_ref.shape[0]
    t0 = c * tq
    t_col = t0 + lax.broadcasted_iota(jnp.int32, (tq, n_c), 0)
    n_row = lax.broadcasted_iota(jnp.int32, (tq, n_c), 1)
    dist = t_col - (n_row * STRIDE_CMP + (L_CMP - 1))
    valid = dist >= 0

    j_io = lax.broadcasted_iota(jnp.int32, (n_sel, tq), 0)
    t_io = t0 + lax.broadcasted_iota(jnp.int32, (n_sel, tq), 1)
    blk_t = t_io // L_SEL
    forced = (j_io == 0) | (j_io == blk_t) | (j_io == blk_t - 1)
    causal_blk = j_io * L_SEL <= t_io

    for h in range(N_KV_HEADS):
        kc = kc_ref[:, h * HEAD_DIM:(h + 1) * HEAD_DIM]
        vc = vc_ref[:, h * HEAD_DIM:(h + 1) * HEAD_DIM]
        psum = jnp.zeros((tq, n_c), F32)
        for g in range(GQA):
            hd = h * GQA + g
            qh = q_ref[:, hd * HEAD_DIM:(hd + 1) * HEAD_DIM]
            bias = _bias_from_dist(tab_ref[hd:hd + 1, :], dist)
            s = _dot_nt(qh, kc) * scale + bias
            s = jnp.where(valid, s, NEG_BIG)
            m = jnp.max(s, axis=-1, keepdims=True)
            e = jnp.where(valid, jnp.exp(s - m), 0.0)
            d = jnp.sum(e, axis=-1, keepdims=True)
            p = e / jnp.where(d > 0, d, 1.0)
            ocmp_ref[:, hd * HEAD_DIM:(hd + 1) * HEAD_DIM] = _dot(p.astype(BF16), vc)
            psum = psum + p
        p_hi = psum.astype(BF16)
        p_lo = (psum - p_hi.astype(F32)).astype(BF16)
        ovt = ovt_ref[...]
        imp = _dot_nt(ovt, p_hi) + _dot_nt(ovt, p_lo)
        score = jnp.where(forced, FORCED_SCORE, jnp.where(causal_blk, imp, -1.0))
        rank = jnp.zeros((n_sel, tq), F32)
        for jp in range(n_sel):
            row = score[jp:jp + 1, :]
            rank = rank + jnp.where(j_io > jp, jnp.where(row >= score, 1.0, 0.0),
                                    jnp.where(row > score, 1.0, 0.0))
        selb = jnp.where(rank < float(N_SEL), 0.0, SEL_OFF)
        if n_sel < HEAD_DIM:
            selb = jnp.concatenate([selb, jnp.zeros((HEAD_DIM - n_sel, tq), F32)], axis=0)
        selb_ref[:, h * HEAD_DIM:(h + 1) * HEAD_DIM] = selb.T.astype(BF16)


def cmp_select(qkv, kc, vc, tab, ovt, B, S, tq=TQ_CMP, cast_also=()):
    T = qkv.shape[0]
    n_qb = S // tq
    steps = T // tq
    n_c = kc.shape[0] // B
    dq = N_Q_HEADS * HEAD_DIM
    dkv = N_KV_HEADS * HEAD_DIM
    kern = functools.partial(_cmp_select_kernel, n_qb=n_qb, scale=HEAD_DIM ** -0.5, n_cast=len(cast_also))
    cast_specs = [pl.BlockSpec((w.shape[0] // steps, w.shape[1]), lambda i: (i, 0)) for w in cast_also]
    assert all(w.shape[0] % (2 * SUBLANES * steps) == 0 for w in cast_also)
    return pl.pallas_call(
        kern,
        grid=(steps,),
        in_specs=[pl.BlockSpec((tq, dq), lambda i: (i, 0)),
                  pl.BlockSpec((n_c, dkv), lambda i: (i // n_qb, 0)),
                  pl.BlockSpec((n_c, dkv), lambda i: (i // n_qb, 0)),
                  pl.BlockSpec(tab.shape, lambda i: (0, 0)),
                  pl.BlockSpec(ovt.shape, lambda i: (0, 0))] + cast_specs,
        out_specs=[pl.BlockSpec((tq, dq), lambda i: (i, 0)),
                   pl.BlockSpec((tq, dkv), lambda i: (i, 0))] + cast_specs,
        out_shape=[jax.ShapeDtypeStruct((T, dq), F32),
                   jax.ShapeDtypeStruct((T, dkv), BF16)] + [jax.ShapeDtypeStruct(w.shape, BF16) for w in cast_also],
        compiler_params=_params(("parallel",)),
    )(qkv, kc, vc, tab, ovt, *cast_also)


def _repack_kv_kernel(ks_ref, vs_ref, kw_ref, vw_ref, ksp_ref, vst_ref, kwp_ref, vwt_ref):
    first = pl.program_id(1) == 0

    @pl.when(first)
    def _():
        for ref in (ksp_ref, vst_ref, kwp_ref, vwt_ref):
            ref[...] = jnp.zeros(ref.shape, ref.dtype)

    @pl.when(jnp.logical_not(first))
    def _():
        ksp_ref[...] = ks_ref[...]
        kwp_ref[...] = kw_ref[...]
        vst_ref[...] = vs_ref[...].T
        vwt_ref[...] = vw_ref[...].T


def repack_kv(qkv, B, S, first_col_block):
    dkv = N_KV_HEADS * HEAD_DIM
    nt = S // WINDOW
    src = lambda col: pl.BlockSpec((WINDOW, dkv), lambda b, i, col=col: (b * nt + jnp.maximum(i - 1, 0), col))
    rows = pl.BlockSpec((WINDOW, dkv), lambda b, i: (b * (nt + 1) + i, 0))
    cols = pl.BlockSpec((dkv, WINDOW), lambda b, i: (b, i))
    return pl.pallas_call(
        _repack_kv_kernel,
        grid=(B, nt + 1),
        in_specs=[src(first_col_block + k) for k in range(4)],
        out_specs=[rows, cols, rows, cols],
        out_shape=[jax.ShapeDtypeStruct((B * (S + WINDOW), dkv), qkv.dtype),
                   jax.ShapeDtypeStruct((B * dkv, S + WINDOW), qkv.dtype)] * 2,
        compiler_params=_params(("parallel", "arbitrary")),
    )(qkv, qkv, qkv, qkv)


def _flash_update(s_raw, vt, bias, m_ref, l_ref, acc_ref, scale):
    m_old = m_ref[...]
    if bias.shape[0] == 1:
        s = s_raw * scale
        m_new = jnp.maximum(m_old, jnp.max(s, axis=0, keepdims=True) + bias)
        p = jnp.exp2(s - (m_new - bias))
    else:
        s = s_raw * scale + bias
        m_new = jnp.maximum(m_old, jnp.max(s, axis=0, keepdims=True))
        p = jnp.exp2(s - m_new)
    alpha = jnp.exp2(m_old - m_new)
    l_ref[...] = alpha * l_ref[...] + jnp.sum(p, axis=0, keepdims=True)
    acc_ref[...] = alpha * acc_ref[...] + _dot(vt, p.astype(BF16))
    m_ref[...] = m_new


def _flash_chunk(k, vt, q, bias, m_ref, l_ref, acc_ref, scale):
    _flash_update(_dot_nt(k, q), vt, bias, m_ref, l_ref, acc_ref, scale)


def _sel_win_kernel(q_ref, selb_ref, ks_ref, vst_ref, kw_ref, vwt_ref, e2_ref, tab_ref, ocmp_ref,
                    gl_ref, gnw_ref, o_ref, m_ref, l_ref, acc_ref, sa_ref, sb_ref, *, n_qb, scale):
    c = pl.program_id(0) % n_qb
    tq = q_ref.shape[0]
    rows = GQA * tq
    n_back = WINDOW // tq
    near = WINDOW + tq
    j_io = lax.broadcasted_iota(jnp.int32, (tq, tq), 0)
    i_io = lax.broadcasted_iota(jnp.int32, (tq, tq), 1)
    dij = i_io - j_io
    dij4 = jnp.concatenate([dij] * GQA, axis=1)
    causal = dij4 >= 0
    pad_col = jnp.where(lax.broadcasted_iota(jnp.int32, (rows, HEAD_DIM), 1) == HEAD_DIM - 1, 1.0, 0.0).astype(BF16)
    sig = jax.nn.sigmoid(gl_ref[...])
    near0 = pl.multiple_of(c * tq, tq)
    n_far = jnp.maximum(c - n_back, 0)
    n_full = n_far // n_back
    n_rem = n_far - n_full * n_back

    def hsl(h):
        return slice(h * HEAD_DIM, (h + 1) * HEAD_DIM)

    def far0(i):
        return pl.multiple_of(WINDOW + i * WINDOW, WINDOW)

    def reset():
        m_ref[...] = jnp.full(m_ref.shape, NEG_BIG, F32)
        l_ref[...] = jnp.zeros(l_ref.shape, F32)
        acc_ref[...] = jnp.zeros(acc_ref.shape, F32)

    def result():
        ot = acc_ref[...] / l_ref[...]
        return [ot[:, g * tq:(g + 1) * tq].T for g in range(GQA)]

    for h in range(N_KV_HEADS):
        q4 = jnp.concatenate([q_ref[:, (h * GQA + g) * HEAD_DIM:(h * GQA + g + 1) * HEAD_DIM]
                              for g in range(GQA)], axis=0)
        sb4 = jnp.concatenate([selb_ref[:, hsl(h)]] * GQA, axis=0)
        q_aug = jnp.concatenate([q4, sb4 + pad_col], axis=1)
        q_win = jnp.concatenate([q4, pad_col], axis=1)
        tabs = [tab_ref[h * GQA + g:h * GQA + g + 1, :] * LOG2E for g in range(GQA)]
        d0 = jnp.concatenate([_bias_from_dist(t, dij) for t in tabs], axis=1)
        d1 = jnp.concatenate([_bias_from_dist(t, dij + tq) for t in tabs], axis=1)
        far = jnp.concatenate([t[:, HEAD_DIM - 1:HEAD_DIM] + jnp.zeros((1, tq), F32) for t in tabs], axis=1)
        far_t = jnp.broadcast_to(far, (tq, rows))
        diag = jnp.where(causal, d0, NEG_BIG)
        bias_sel = jnp.concatenate([far_t] * (n_back - 1) + [d1, diag], axis=0)
        bias_win = jnp.concatenate([jnp.where(dij4 < 0, far_t, NEG_BIG)] + [far_t] * (n_back - 2) + [d1, diag],
                                   axis=0)

        reset()
        e2_near = e2_ref[pl.ds(near0, near), :]
        k_near = jnp.concatenate([ks_ref[pl.ds(near0, near), hsl(h)], e2_near], axis=1)
        _flash_chunk(k_near, vst_ref[hsl(h), pl.ds(near0, near)], q_aug, bias_sel, m_ref, l_ref, acc_ref, scale)

        def far_scores(i):
            r0 = far0(i)
            k = jnp.concatenate([ks_ref[pl.ds(r0, WINDOW), hsl(h)], e2_ref[pl.ds(r0, WINDOW), :]], axis=1)
            return _dot_nt(k, q_aug)

        def far_update(s_raw, i, bias):
            _flash_update(s_raw, vst_ref[hsl(h), pl.ds(far0(i), WINDOW)], bias, m_ref, l_ref, acc_ref, scale)

        sa_ref[...] = far_scores(0)

        def body(i, carry):
            sb_ref[...] = far_scores(2 * i + 1)
            far_update(sa_ref[...], 2 * i, far)
            sa_ref[...] = far_scores(2 * i + 2)
            far_update(sb_ref[...], 2 * i + 1, far)
            return carry

        lax.fori_loop(0, n_full // 2, body, 0)

        @pl.when(n_full % 2 == 1)
        def _():
            far_update(sa_ref[...], n_full - 1, far)

        @pl.when(n_rem > 0)
        def _():
            if n_back == 2:
                r0 = far0(n_full)
                k = jnp.concatenate([ks_ref[pl.ds(r0, tq), hsl(h)], e2_ref[pl.ds(r0, tq), :]], axis=1)
                _flash_chunk(k, vst_ref[hsl(h), pl.ds(r0, tq)], q_aug, far, m_ref, l_ref, acc_ref, scale)
            else:
                live = lax.broadcasted_iota(jnp.int32, (WINDOW, rows), 0) < n_rem * tq
                far_update(far_scores(n_full), n_full,
                           jnp.where(live, jnp.broadcast_to(far, (WINDOW, rows)), NEG_BIG))

        o_sel = result()

        reset()
        kw_near = jnp.concatenate([kw_ref[pl.ds(near0, near), hsl(h)], e2_near], axis=1)
        _flash_chunk(kw_near, vwt_ref[hsl(h), pl.ds(near0, near)], q_win, bias_win, m_ref, l_ref, acc_ref, scale)
        o_win = result()

        for g in range(GQA):
            hd = h * GQA + g
            o = (sig[:, 3 * hd:3 * hd + 1] * ocmp_ref[:, hsl(hd)]
                 + sig[:, 3 * hd + 1:3 * hd + 2] * o_sel[g]
                 + sig[:, 3 * hd + 2:3 * hd + 3] * o_win[g])
            y = o * lax.rsqrt(jnp.mean(o * o, axis=-1, keepdims=True) + EPS)
            o_ref[:, hsl(hd)] = (y * gnw_ref[:, hsl(hd)]).astype(o_ref.dtype)


def sel_win(qkv, selb, ksp, vstp, kwp, vwtp, e2p, tab, ocmp, gl, gnw, B, S, tq=TQ_SEL):
    T = qkv.shape[0]
    n_qb = S // tq
    dq = N_Q_HEADS * HEAD_DIM
    dkv = N_KV_HEADS * HEAD_DIM
    SP = S + WINDOW
    kern = functools.partial(_sel_win_kernel, n_qb=n_qb, scale=HEAD_DIM ** -0.5 * LOG2E)
    k_spec = pl.BlockSpec((SP, dkv), lambda i: (i // n_qb, 0))
    vt_spec = pl.BlockSpec((dkv, SP), lambda i: (i // n_qb, 0))
    return pl.pallas_call(
        kern,
        grid=(T // tq,),
        in_specs=[pl.BlockSpec((tq, dq), lambda i: (i, 0)),
                  pl.BlockSpec((tq, dkv), lambda i: (i, 0)),
                  k_spec, vt_spec, k_spec, vt_spec,
                  pl.BlockSpec((SP, HEAD_DIM), lambda i: (0, 0)),
                  pl.BlockSpec(tab.shape, lambda i: (0, 0)),
                  pl.BlockSpec((tq, dq), lambda i: (i, 0)),
                  pl.BlockSpec((tq, HEAD_DIM), lambda i: (i, 0)),
                  pl.BlockSpec((1, dq), lambda i: (0, 0))],
        out_specs=pl.BlockSpec((tq, dq), lambda i: (i, 0)),
        out_shape=jax.ShapeDtypeStruct((T, dq), BF16),
        scratch_shapes=[pltpu.VMEM((1, GQA * tq), F32), pltpu.VMEM((1, GQA * tq), F32),
                        pltpu.VMEM((HEAD_DIM, GQA * tq), F32),
                        pltpu.VMEM((WINDOW, GQA * tq), F32), pltpu.VMEM((WINDOW, GQA * tq), F32)],
        compiler_params=_params(("parallel",)),
    )(qkv, selb, ksp, vstp, kwp, vwtp, e2p, tab, ocmp, gl, gnw)


def _conv_kernel(b_ref, c_ref, h_ref, cp_ref, hp_ref, cw_ref, cb_ref, gnw_ref, o_ref, u_ref, *, tiles_per_seq):
    tm = b_ref.shape[0]
    first = (pl.program_id(0) % tiles_per_seq) == 0
    u_prev = cp_ref[...] * hp_ref[...]
    u_ref[0:8, :] = jnp.where(first, 0.0, u_prev)
    u = c_ref[...] * h_ref[...]
    u_ref[8:8 + tm, :] = u
    y = (cw_ref[0:1, :] * u_ref[6:6 + tm, :] + cw_ref[1:2, :] * u_ref[7:7 + tm, :]
         + cw_ref[2:3, :] * u + cb_ref[...])
    o = b_ref[...] * y
    n_groups = o.shape[1] // HEAD_DIM
    for g in range(n_groups):
        sl = slice(g * HEAD_DIM, (g + 1) * HEAD_DIM)
        og = o[:, sl]
        yg = og * lax.rsqrt(jnp.mean(og * og, axis=-1, keepdims=True) + EPS)
        o_ref[:, sl] = (yg * gnw_ref[:, sl]).astype(o_ref.dtype)


def conv_mixer(gc, conv_w, conv_b, gnw, S, tm):
    T = gc.shape[0]
    dc = conv_w.shape[1]
    tps = S // tm
    kern = functools.partial(_conv_kernel, tiles_per_seq=tps)
    prev = lambda col: pl.BlockSpec((8, dc), lambda i, col=col: (jnp.maximum(i * (tm // 8) - 1, 0), col))
    cur = lambda col: pl.BlockSpec((tm, dc), lambda i, col=col: (i, col))
    return pl.pallas_call(
        kern,
        grid=(T // tm,),
        in_specs=[cur(0), cur(1), cur(2), prev(1), prev(2),
                  pl.BlockSpec((8, dc), lambda i: (0, 0)),
                  pl.BlockSpec((1, dc), lambda i: (0, 0)),
                  pl.BlockSpec((1, dc), lambda i: (0, 0))],
        out_specs=pl.BlockSpec((tm, dc), lambda i: (i, 0)),
        out_shape=jax.ShapeDtypeStruct((T, dc), BF16),
        scratch_shapes=[pltpu.VMEM((tm + 8, dc), F32)],
        compiler_params=_params(("parallel",)),
    )(gc, gc, gc, gc, gc, conv_w, conv_b, gnw)


def _out_proj_kernel(ma_ref, mc_ref, wa_ref, wc_ref, x_ref, o_ref):
    o_ref[...] = x_ref[...] + _dot(ma_ref[...], wa_ref[...]) + _dot(mc_ref[...], wc_ref[...])


def out_proj(ma, mc, wa, wc, x, tm, tn):
    T, da = ma.shape
    dc = mc.shape[1]
    D = x.shape[1]
    return pl.pallas_call(
        _out_proj_kernel,
        grid=(T // tm, D // tn),
        in_specs=[pl.BlockSpec((tm, da), lambda i, j: (i, 0)),
                  pl.BlockSpec((tm, dc), lambda i, j: (i, 0)),
                  pl.BlockSpec((da, tn), lambda i, j: (0, j)),
                  pl.BlockSpec((dc, tn), lambda i, j: (0, j)),
                  pl.BlockSpec((tm, tn), lambda i, j: (i, j))],
        out_specs=pl.BlockSpec((tm, tn), lambda i, j: (i, j)),
        out_shape=jax.ShapeDtypeStruct((T, D), F32),
        compiler_params=_params(("parallel", "arbitrary")),
    )(ma, mc, wa, wc, x)


def _sort_network(n):
    pairs = []
    p = 1
    while p < n:
        k = p
        while k >= 1:
            for j in range(k % p, n - k, 2 * k):
                for i in range(min(k, n - j - k)):
                    if (i + j) // (2 * p) == (i + j + k) // (2 * p):
                        pairs.append((i + j, i + j + k))
            k //= 2
        p *= 2
    return pairs


def _peer_route_kernel(q_ref, sk_ref, u_ref, v_ref, e_ref, g_ref, ub_ref, vb_ref, sv_ref, si_ref):
    ub_ref[...] = u_ref[...].astype(BF16)
    vb_ref[...] = v_ref[...].astype(BF16)
    tt = q_ref.shape[0]
    K = PEER_TOPK
    sub = SUBLANES
    n_io = lax.broadcasted_iota(jnp.int32, (N_KEYS, tt), 0).astype(F32)
    r8 = lax.broadcasted_iota(jnp.int32, (sub, tt), 0).astype(F32)
    ninf = -jnp.inf
    lens = [K // (a + 1) for a in range(N_FIXED_A)]
    lens += [max(K // (b + 1) - N_FIXED_A, 0) for b in range(sub - N_FIXED_A)]
    assert sum(lens) == sum(K // (a + 1) for a in range(K)) and lens[-1] == 0
    fixed_a = r8 < float(N_FIXED_A)
    list_len = jnp.zeros((sub, tt), F32)
    for row, n in enumerate(lens):
        list_len = jnp.where(r8 == float(row), float(n), list_len)

    for h in range(PEER_HEADS):
        for c in range(2):
            col = (h * 2 + c) * HEAD_DIM
            s = _dot_nt(sk_ref[h * 2 + c], q_ref[:, col:col + HEAD_DIM])
            n_col = N_KEYS // sub
            vals = [s[j * sub:(j + 1) * sub, :] for j in range(n_col)]
            idxs = [n_io[j * sub:(j + 1) * sub, :] for j in range(n_col)]
            for lo, hi in _sort_network(n_col):
                swap = (vals[hi] > vals[lo]) | ((vals[hi] == vals[lo]) & (idxs[hi] < idxs[lo]))
                vals[lo], vals[hi] = jnp.where(swap, vals[hi], vals[lo]), jnp.where(swap, vals[lo], vals[hi])
                idxs[lo], idxs[hi] = jnp.where(swap, idxs[hi], idxs[lo]), jnp.where(swap, idxs[lo], idxs[hi])
            for k in range(K):
                m = jnp.max(vals[0], axis=0, keepdims=True)
                idx = jnp.min(jnp.where(vals[0] == m, idxs[0], float(N_KEYS)), axis=0, keepdims=True)
                sv_ref[c, k:k + 1, :] = m
                si_ref[c, k:k + 1, :] = idx
                win = idxs[0] == idx
                for j in range(min(n_col, K) - 1 - k):
                    vals[j] = jnp.where(win, vals[j + 1], vals[j])
                    idxs[j] = jnp.where(win, idxs[j + 1], idxs[j])
        sv0, sv1 = sv_ref[0], sv_ref[1]
        si0, si1 = si_ref[0], si_ref[1]
        sv1_low = pltpu.roll(sv1[0:sub, :], N_FIXED_A, 0)
        sv0_top = sv0[0:sub, :]
        lv, lf = [], []
        for dep in range(K):
            a_dep = min(N_FIXED_A + dep, K - 1)
            val = (jnp.where(fixed_a, sv0_top, sv0[a_dep:a_dep + 1, :])
                   + jnp.where(fixed_a, sv1[dep:dep + 1, :], sv1_low))
            lv.append(jnp.where(list_len > dep, val, ninf))
            lf.append(jnp.where(fixed_a, r8 * float(K) + dep, (N_FIXED_A + dep) * float(K) + r8 - N_FIXED_A))
        cvs, fls = [], []
        for k in range(K):
            m = jnp.max(lv[0], axis=0, keepdims=True)
            fsel = jnp.min(jnp.where(lv[0] == m, lf[0], 1e9), axis=0, keepdims=True)
            win = lf[0] == fsel
            cvs.append(m)
            fls.append(fsel)
            for dep in range(K - 1 - k):
                lv[dep] = jnp.where(win, lv[dep + 1], lv[dep])
                lf[dep] = jnp.where(win, lf[dep + 1], lf[dep])
        cv = jnp.concatenate(cvs, axis=0)
        fl = jnp.concatenate(fls, axis=0)
        a_sel = jnp.floor(fl * (1.0 / K))
        b_sel = fl - a_sel * K
        i1 = jnp.zeros_like(fl)
        i2 = jnp.zeros_like(fl)
        for r in range(K):
            i1 = jnp.where(a_sel == r, si0[r:r + 1, :], i1)
            i2 = jnp.where(b_sel == r, si1[r:r + 1, :], i2)
        ex = i1 * float(N_KEYS) + i2
        ev = jnp.exp(cv - jnp.max(cv, axis=0, keepdims=True))
        gates = ev / jnp.sum(ev, axis=0, keepdims=True)
        e_ref[0, h * K:(h + 1) * K, :] = ex.astype(jnp.int32)
        g_ref[0, h * K:(h + 1) * K, :] = gates


def peer_route(qp, subkeys, u, v):
    T = qp.shape[0]
    tt = TT_ROUTE
    P = PEER_HEADS * PEER_TOPK
    nt = T // tt
    E, D = u.shape
    slab = E // nt
    assert slab * nt == E and slab % 16 == 0
    tab_spec = pl.BlockSpec((slab, D), lambda i: (i, 0))
    return pl.pallas_call(
        _peer_route_kernel,
        grid=(nt,),
        in_specs=[pl.BlockSpec((tt, qp.shape[1]), lambda i: (i, 0)),
                  pl.BlockSpec(subkeys.shape, lambda i: (0, 0, 0)),
                  tab_spec, tab_spec],
        out_specs=[pl.BlockSpec((1, P, tt), lambda i: (i, 0, 0))] * 2 + [tab_spec, tab_spec],
        out_shape=[jax.ShapeDtypeStruct((nt, P, tt), jnp.int32),
                   jax.ShapeDtypeStruct((nt, P, tt), F32),
                   jax.ShapeDtypeStruct((E, D), BF16), jax.ShapeDtypeStruct((E, D), BF16)],
        scratch_shapes=[pltpu.VMEM((2, PEER_TOPK, tt), F32), pltpu.VMEM((2, PEER_TOPK, tt), F32)],
        compiler_params=_params(("parallel",)),
    )(qp, subkeys, u, v)


def _peer_gbuild_kernel(e_ref, g_ref, o_ref, i1_ref, i2_ref, gt_ref):
    tt = e_ref.shape[2]
    e = e_ref[0].T
    i1_ref[...] = e >> 7
    i2_ref[...] = e & (N_KEYS - 1)
    gt_ref[...] = g_ref[0].T
    P = e.shape[1]
    k_io = lax.broadcasted_iota(jnp.int32, (N_KEYS, P), 0)
    group = 64

    def body(tg, carry):
        base = pl.multiple_of(tg * group, group)
        r1s = i1_ref[pl.ds(base, group), :]
        r2s = i2_ref[pl.ds(base, group), :]
        rgs = gt_ref[pl.ds(base, group), :]
        for u in range(group):
            lhs = jnp.where(k_io == r1s[u:u + 1, :], rgs[u:u + 1, :], 0.0).astype(BF16)
            rhs = jnp.where(k_io == r2s[u:u + 1, :], 1.0, 0.0).astype(BF16)
            g_t = _dot_nt(lhs, rhs).astype(BF16).astype(F32)
            lo = pltpu.bitcast(g_t[:HALF_KEYS], jnp.uint32) >> 16
            hi = pltpu.bitcast(g_t[HALF_KEYS:], jnp.uint32)
            words = hi | lo
            for a in range(HALF_KEYS // PACK_ROWS):
                o_ref[a, base + u] = words[a * PACK_ROWS:(a + 1) * PACK_ROWS, :]
        return carry

    lax.fori_loop(0, tt // group, body, 0)


def peer_gbuild(ex, gates):
    nt, P, tt = ex.shape
    T = nt * tt
    n_a = HALF_KEYS // PACK_ROWS
    return pl.pallas_call(
        _peer_gbuild_kernel,
        grid=(nt,),
        in_specs=[pl.BlockSpec((1, P, tt), lambda i: (i, 0, 0))] * 2,
        out_specs=pl.BlockSpec((n_a, tt, PACK_ROWS, N_KEYS), lambda i: (0, i, 0, 0)),
        out_shape=jax.ShapeDtypeStruct((n_a, T, PACK_ROWS, N_KEYS), jnp.uint32),
        scratch_shapes=[pltpu.VMEM((tt, P), jnp.int32), pltpu.VMEM((tt, P), jnp.int32),
                        pltpu.VMEM((tt, P), F32)],
        compiler_params=_params(("parallel",)),
    )(ex, gates)


def _peer_dense_kernel(h_ref, nw_ref, fw_ref, u_ref, v_ref, gm_ref, o_ref, hn_ref, acc_ref):
    j = pl.program_id(1)
    tt = h_ref.shape[0]

    @pl.when(j == 0)
    def _():
        x = h_ref[...]
        y = x * lax.rsqrt(jnp.mean(x * x, axis=-1, keepdims=True) + EPS)
        hn_ref[...] = (y * nw_ref[...]).astype(BF16)
        acc_ref[...] = jnp.zeros(acc_ref.shape, F32)

    a = _dot_nt(hn_ref[...], u_ref[...])
    shift = (16 * (1 - j % 2)).astype(jnp.uint32)
    gm = jnp.concatenate(
        [pltpu.bitcast((gm_ref[pl.ds(k, tt, stride=PACK_ROWS), :] << shift) & jnp.uint32(0xFFFF0000), F32)
         for k in range(PACK_ROWS)], axis=1)
    w = (gm * jax.nn.gelu(a)).astype(BF16)
    acc_ref[...] += _dot(w, v_ref[...])

    @pl.when(j == pl.num_programs(1) - 1)
    def _():
        x = h_ref[...] + acc_ref[...]
        y = x * lax.rsqrt(jnp.mean(x * x, axis=-1, keepdims=True) + EPS)
        o_ref[...] = y * fw_ref[...]


def peer_dense(h, ffn_nw, final_nw, u, v, gm_words, tt):
    T, D = h.shape
    E = u.shape[0]
    ec = PACK_ROWS * N_KEYS
    n_a = gm_words.shape[0]
    assert E == 2 * n_a * ec
    gm2 = gm_words.reshape(n_a, T * PACK_ROWS, N_KEYS)
    tab_spec = pl.BlockSpec((ec, D), lambda i, j: ((j % 2) * n_a + j // 2, 0))
    return pl.pallas_call(
        _peer_dense_kernel,
        grid=(T // tt, E // ec),
        in_specs=[pl.BlockSpec((tt, D), lambda i, j: (i, 0)),
                  pl.BlockSpec((1, D), lambda i, j: (0, 0)),
                  pl.BlockSpec((1, D), lambda i, j: (0, 0)),
                  tab_spec, tab_spec,
                  pl.BlockSpec((None, tt * PACK_ROWS, N_KEYS), lambda i, j: (j // 2, i, 0))],
        out_specs=pl.BlockSpec((tt, D), lambda i, j: (i, 0)),
        out_shape=jax.ShapeDtypeStruct((T, D), F32),
        scratch_shapes=[pltpu.VMEM((tt, D), BF16), pltpu.VMEM((tt, D), F32)],
        compiler_params=_params(("parallel", "arbitrary")),
    )(h, ffn_nw.reshape(1, D), final_nw.reshape(1, D), u, v, gm2)


def _t5_bucket_np(n_dist):
    d = np.arange(n_dist)
    max_exact = NUM_BUCKETS // 2
    nf = np.maximum(d, 1).astype(np.float64)
    large = max_exact + (np.log(nf / max_exact) / math.log(MAX_DISTANCE / max_exact)
                         * (NUM_BUCKETS - max_exact)).astype(np.int64)
    large = np.minimum(large, NUM_BUCKETS - 1)
    return np.where(d < max_exact, d, large).astype(np.int32)


def _overlap_t_np(S):
    n_c = (S - L_CMP) // STRIDE_CMP + 1
    n_sel = S // L_SEL
    pos = np.arange(n_c)[:, None] * STRIDE_CMP + np.arange(L_CMP)[None, :]
    m = np.zeros((n_c + 1, n_sel), np.float32)
    np.add.at(m, (np.repeat(np.arange(n_c), L_CMP), (pos // L_SEL).reshape(-1)), 1.0 / L_CMP)
    return np.ascontiguousarray(m.T)


def _block_onehot_padded_np(S):
    assert S // L_SEL < HEAD_DIM
    e2 = np.zeros((WINDOW + S, HEAD_DIM), np.float32)
    e2[WINDOW + np.arange(S), np.arange(S) // L_SEL] = 1.0
    e2[:WINDOW, HEAD_DIM - 1] = SEL_OFF
    return e2


def nsa_conv_mix(xt, B, S, attn_norm_w, w_in, w_cmp_k, w_cmp_v, cmp_pos, conv_w, conv_b,
                 attn_gnw, conv_gnw, rel_bias, tm=TM_PROJ, tm_conv=TM_CONV, cast_also=()):
    T, D = xt.shape
    dq = N_Q_HEADS * HEAD_DIM
    dkv = N_KV_HEADS * HEAD_DIM
    n_attn = dq + 6 * dkv
    n_gate = 3 * N_Q_HEADS
    dc = (w_in.shape[1] - n_attn - n_gate) // 3
    w_t = w_in.T
    w_attn = cast_rows(w_t, 0, n_attn)
    w_conv = cast_rows(w_t, n_attn + n_gate, 3 * dc)
    w_gate = jnp.pad(w_t[n_attn:n_attn + n_gate], ((0, HEAD_DIM - n_gate), (0, 0))).astype(BF16)

    qkv = norm_matmul(xt, attn_norm_w, w_attn, BF16, tm, n_attn // 2, w_is_nk=True)
    gc, gl = norm_matmul(xt, attn_norm_w, w_conv, F32, tm, dc, w_is_nk=True, w_side=w_gate)

    n16 = S // STRIDE_CMP
    k16 = qkv[:, dq:dq + dkv].reshape(B * n16, STRIDE_CMP * dkv)
    v16 = qkv[:, dq + dkv:dq + 2 * dkv].reshape(B * n16, STRIDE_CMP * dkv)

    def wflat(w, lo):
        return w[lo:lo + STRIDE_CMP].reshape(STRIDE_CMP * HEAD_DIM, HEAD_DIM).astype(BF16)

    def posrow(lo):
        return cmp_pos[lo:lo + STRIDE_CMP].reshape(1, STRIDE_CMP * HEAD_DIM)

    kc, vc = compress(k16, v16, posrow(0), posrow(STRIDE_CMP),
                      wflat(w_cmp_k, 0), wflat(w_cmp_k, STRIDE_CMP),
                      wflat(w_cmp_v, 0), wflat(w_cmp_v, STRIDE_CMP), B)

    tab = rel_bias[_t5_bucket_np(HEAD_DIM)].T
    ovt = jnp.asarray(_overlap_t_np(S), BF16)
    e2p = jnp.asarray(_block_onehot_padded_np(S), BF16)

    ocmp, selb, *casts = cmp_select(qkv, kc, vc, tab, ovt, B, S, cast_also=cast_also)
    ksp, vstp, kwp, vwtp = repack_kv(qkv, B, S, first_col_block=dq // dkv + 2)
    attn = sel_win(qkv, selb, ksp, vstp, kwp, vwtp, e2p, tab, ocmp, gl, attn_gnw.reshape(1, dq), B, S)
    cw8 = jnp.pad(conv_w, ((0, 8 - conv_w.shape[0]), (0, 0)))
    conv = conv_mixer(gc, cw8, conv_b.reshape(1, dc), conv_gnw.reshape(1, dc), S, tm_conv)
    return (attn, conv, *casts)


def peer_block(h, ffn_nw, final_nw, peer_wq, peer_subkeys, peer_u, peer_v, tm=TM_PROJ, tt=TT_DENSE):
    T, D = h.shape
    qp = norm_matmul(h, ffn_nw, peer_wq.astype(BF16), BF16, tm, TN_OUT)
    sk = peer_subkeys.reshape(PEER_HEADS * 2, N_KEYS, peer_subkeys.shape[-1]).astype(BF16)
    ex, gates, u_bf, v_bf = peer_route(qp, sk, peer_u, peer_v)
    gm = peer_gbuild(ex, gates)
    return peer_dense(h, ffn_nw, final_nw, u_bf, v_bf, gm, tt)


def kernel(x, attn_norm_w, w_in, w_cmp_k, w_cmp_v, cmp_pos, conv_w, conv_b, attn_group_norm_w,
           conv_group_norm_w, w_out, rel_bias, ffn_norm_w, peer_wq, peer_subkeys, peer_u, peer_v,
           final_norm_w):
    B, S, D = x.shape
    T = B * S
    xt = x.reshape(T, D)
    attn, conv, wo, wq = nsa_conv_mix(xt, B, S, attn_norm_w[0], w_in[0], w_cmp_k[0], w_cmp_v[0], cmp_pos[0],
                                      conv_w[0], conv_b[0], attn_group_norm_w[0], conv_group_norm_w[0], rel_bias,
                                      cast_also=(w_out[0], peer_wq[0]))
    da = attn.shape[1]
    h = out_proj(attn, conv, wo[:da], wo[da:], xt, TM_PROJ, TN_OUT)
    out = peer_block(h, ffn_norm_w[0], final_norm_w, wq, peer_subkeys[0], peer_u[0], peer_v[0])
    return out.reshape(B, S, D)
```

```python
import functools
import math

import jax
import jax.numpy as jnp
import numpy as np
from jax import lax
from jax.experimental import pallas as pl
from jax.experimental.pallas import tpu as pltpu

F32 = jnp.float32
BF16 = jnp.bfloat16

HEAD_DIM = 128
N_KV_HEADS = 2
GQA = 4
N_Q_HEADS = N_KV_HEADS * GQA
L_CMP = 32
STRIDE_CMP = 16
L_SEL = 64
N_SEL = 16
WINDOW = 512
Q_BLOCK = 128
FORCED_SCORE = float(GQA + 1)
NUM_BUCKETS = 32
MAX_DISTANCE = 128
PEER_HEADS = 8
N_KEYS = 128
PEER_TOPK = 16
EPS = 1e-6
NEG_BIG = -1e30
SEL_OFF = -float(2 ** 30)
LOG2E = math.log2(math.e)
N_FIXED_A = 4

SUBLANES = 8
LANES = 128
MXU_DIM = 256
V7X_VMEM_BYTES = 64 * 1024 * 1024
VMEM_LIMIT = V7X_VMEM_BYTES * 7 // 8

HALF_KEYS = N_KEYS // 2
PACK_ROWS = SUBLANES

TM_PROJ = 4 * MXU_DIM
TN_OUT = 4 * MXU_DIM
TM_CONV = 2 * MXU_DIM
TQ_CMP = 4 * Q_BLOCK
TQ_SEL = 2 * Q_BLOCK
TT_ROUTE = LANES
TT_DENSE = 2 * MXU_DIM
CAST_ROWS = MXU_DIM


def _dot(a, b):
    return jnp.dot(a, b, preferred_element_type=F32)


def _dot_nt(a, b):
    return lax.dot_general(a, b, (((1,), (1,)), ((), ())), preferred_element_type=F32)


def _params(sem, vmem=VMEM_LIMIT):
    return pltpu.CompilerParams(dimension_semantics=sem, vmem_limit_bytes=vmem)


def _norm_matmul_kernel(x_ref, nw_ref, w_ref, *rest, w_is_nk, has_side):
    if has_side:
        ws_ref, o_ref, os_ref, xn_ref = rest
    else:
        o_ref, xn_ref = rest

    @pl.when(pl.program_id(1) == 0)
    def _():
        x = x_ref[...]
        y = x * lax.rsqrt(jnp.mean(x * x, axis=-1, keepdims=True) + EPS)
        xn_ref[...] = (y * nw_ref[...]).astype(BF16)
        if has_side:
            os_ref[...] = _dot_nt(xn_ref[...], ws_ref[...])

    dot = _dot_nt if w_is_nk else _dot
    o_ref[...] = dot(xn_ref[...], w_ref[...]).astype(o_ref.dtype)


def norm_matmul(x, norm_w, w, out_dtype, tm, tn, w_is_nk=False, w_side=None):
    T, D = x.shape
    N = w.shape[0] if w_is_nk else w.shape[1]
    w_spec = pl.BlockSpec((tn, D), lambda i, j: (j, 0)) if w_is_nk else pl.BlockSpec((D, tn), lambda i, j: (0, j))
    in_specs = [pl.BlockSpec((tm, D), lambda i, j: (i, 0)), pl.BlockSpec((1, D), lambda i, j: (0, 0)), w_spec]
    out_specs = [pl.BlockSpec((tm, tn), lambda i, j: (i, j))]
    out_shape = [jax.ShapeDtypeStruct((T, N), out_dtype)]
    args = [x, norm_w.reshape(1, D), w]
    if w_side is not None:
        n_side = w_side.shape[0]
        in_specs.append(pl.BlockSpec((n_side, D), lambda i, j: (0, 0)))
        out_specs.append(pl.BlockSpec((tm, n_side), lambda i, j: (i, 0)))
        out_shape.append(jax.ShapeDtypeStruct((T, n_side), F32))
        args.append(w_side)
    out = pl.pallas_call(
        functools.partial(_norm_matmul_kernel, w_is_nk=w_is_nk, has_side=w_side is not None),
        grid=(T // tm, N // tn),
        in_specs=in_specs,
        out_specs=out_specs,
        out_shape=out_shape,
        scratch_shapes=[pltpu.VMEM((tm, D), BF16)],
        compiler_params=_params(("parallel", "arbitrary")),
    )(*args)
    return out if w_side is not None else out[0]


def _cast_rows_kernel(w_ref, o_ref):
    o_ref[...] = w_ref[...].astype(o_ref.dtype)


def cast_rows(w, row0, n_rows, tr=CAST_ROWS):
    K = w.shape[1]
    assert row0 % SUBLANES == 0 and n_rows % tr == 0
    return pl.pallas_call(
        _cast_rows_kernel,
        grid=(n_rows // tr,),
        in_specs=[pl.BlockSpec((pl.Element(tr), pl.Element(K)), lambda i: (pl.multiple_of(row0 + i * tr, SUBLANES), 0))],
        out_specs=pl.BlockSpec((tr, K), lambda i: (i, 0)),
        out_shape=jax.ShapeDtypeStruct((n_rows, K), BF16),
        compiler_params=_params(("parallel",)),
    )(w)


def _compress_kernel(k_ref, v_ref, pa_ref, pb_ref, wka_ref, wkb_ref, wva_ref, wvb_ref, kc_ref, vc_ref):
    dkv = kc_ref.shape[1]

    def one(x_ref, wa_ref, wb_ref, o_ref):
        for h in range(N_KV_HEADS):
            x = jnp.concatenate([x_ref[:, l * dkv + h * HEAD_DIM:l * dkv + (h + 1) * HEAD_DIM]
                                 for l in range(STRIDE_CMP)], axis=1).astype(F32)
            a = _dot((x + pa_ref[...]).astype(BF16), wa_ref[...])
            b = _dot((x + pb_ref[...]).astype(BF16), wb_ref[...])
            n = b.shape[0]
            o_ref[:, h * HEAD_DIM:(h + 1) * HEAD_DIM] = (a + pltpu.roll(b, n - 1, 0)).astype(o_ref.dtype)

    one(k_ref, wka_ref, wkb_ref, kc_ref)
    one(v_ref, wva_ref, wvb_ref, vc_ref)


def compress(k16, v16, pos_a, pos_b, wka, wkb, wva, wvb, B):
    R, C = k16.shape
    nb = R // B
    dkv = C // STRIDE_CMP
    full = lambda shp: pl.BlockSpec(shp, lambda b: (0, 0))
    return pl.pallas_call(
        _compress_kernel,
        grid=(B,),
        in_specs=[pl.BlockSpec((nb, C), lambda b: (b, 0)),
                  pl.BlockSpec((nb, C), lambda b: (b, 0)),
                  full(pos_a.shape), full(pos_b.shape),
                  full(wka.shape), full(wkb.shape), full(wva.shape), full(wvb.shape)],
        out_specs=[pl.BlockSpec((nb, dkv), lambda b: (b, 0))] * 2,
        out_shape=[jax.ShapeDtypeStruct((R, dkv), BF16)] * 2,
        compiler_params=_params(("parallel",)),
    )(k16, v16, pos_a, pos_b, wka, wkb, wva, wvb)


def _bias_from_dist(tab_row, dist):
    idx = jnp.clip(dist, 0, 127)
    w = tab_row.shape[1]
    tab = jnp.broadcast_to(tab_row, (idx.shape[0], w))
    parts = [jnp.take_along_axis(tab, idx[:, k:k + w], axis=1) for k in range(0, idx.shape[1], w)]
    return parts[0] if len(parts) == 1 else jnp.concatenate(parts, axis=1)


def _cmp_select_kernel(q_ref, kc_ref, vc_ref, tab_ref, ovt_ref, *rest, n_qb, scale, n_cast):
    cast_in, (ocmp_ref, selb_ref), cast_out = rest[:n_cast], rest[n_cast:n_cast + 2], rest[n_cast + 2:]
    for src_ref, dst_ref in zip(cast_in, cast_out):
        dst_ref[...] = src_ref[...].astype(BF16)
    c = pl.program_id(0) % n_qb
    tq = q_ref.shape[0]
    n_c = kc_ref.shape[0]
    n_sel = ovt_ref.shape[0]
    t0 = c * tq
    t_col = t0 + lax.broadcasted_iota(jnp.int32, (tq, n_c), 0)
    n_row = lax.broadcasted_iota(jnp.int32, (tq, n_c), 1)
    dist = t_col - (n_row * STRIDE_CMP + (L_CMP - 1))
    valid = dist >= 0

    j_io = lax.broadcasted_iota(jnp.int32, (n_sel, tq), 0)
    t_io = t0 + lax.broadcasted_iota(jnp.int32, (n_sel, tq), 1)
    blk_t = t_io // L_SEL
    forced = (j_io == 0) | (j_io == blk_t) | (j_io == blk_t - 1)
    causal_blk = j_io * L_SEL <= t_io

    for h in range(N_KV_HEADS):
        kc = kc_ref[:, h * HEAD_DIM:(h + 1) * HEAD_DIM]
        vc = vc_ref[:, h * HEAD_DIM:(h + 1) * HEAD_DIM]
        psum = jnp.zeros((tq, n_c), F32)
        for g in range(GQA):
            hd = h * GQA + g
            qh = q_ref[:, hd * HEAD_DIM:(hd + 1) * HEAD_DIM]
            bias = _bias_from_dist(tab_ref[hd:hd + 1, :], dist)
            s = _dot_nt(qh, kc) * scale + bias
            s = jnp.where(valid, s, NEG_BIG)
            m = jnp.max(s, axis=-1, keepdims=True)
            e = jnp.where(valid, jnp.exp(s - m), 0.0)
            d = jnp.sum(e, axis=-1, keepdims=True)
            p = e / jnp.where(d > 0, d, 1.0)
            ocmp_ref[:, hd * HEAD_DIM:(hd + 1) * HEAD_DIM] = _dot(p.astype(BF16), vc)
            psum = psum + p
        p_hi = psum.astype(BF16)
        p_lo = (psum - p_hi.astype(F32)).astype(BF16)
        ovt = ovt_ref[...]
        imp = _dot_nt(ovt, p_hi) + _dot_nt(ovt, p_lo)
        score = jnp.where(forced, FORCED_SCORE, jnp.where(causal_blk, imp, -1.0))
        rank = jnp.zeros((n_sel, tq), F32)
        for jp in range(n_sel):
            row = score[jp:jp + 1, :]
            rank = rank + jnp.where(j_io > jp, jnp.where(row >= score, 1.0, 0.0),
                                    jnp.where(row > score, 1.0, 0.0))
        selb = jnp.where(rank < float(N_SEL), 0.0, SEL_OFF)
        if n_sel < HEAD_DIM:
            selb = jnp.concatenate([selb, jnp.zeros((HEAD_DIM - n_sel, tq), F32)], axis=0)
        selb_ref[:, h * HEAD_DIM:(h + 1) * HEAD_DIM] = selb.T.astype(BF16)


def cmp_select(qkv, kc, vc, tab, ovt, B, S, tq=TQ_CMP, cast_also=()):
    T = qkv.shape[0]
    n_qb = S // tq
    steps = T // tq
    n_c = kc.shape[0] // B
    dq = N_Q_HEADS * HEAD_DIM
    dkv = N_KV_HEADS * HEAD_DIM
    kern = functools.partial(_cmp_select_kernel, n_qb=n_qb, scale=HEAD_DIM ** -0.5, n_cast=len(cast_also))
    cast_specs = [pl.BlockSpec((w.shape[0] // steps, w.shape[1]), lambda i: (i, 0)) for w in cast_also]
    assert all(w.shape[0] % (2 * SUBLANES * steps) == 0 for w in cast_also)
    return pl.pallas_call(
        kern,
        grid=(steps,),
        in_specs=[pl.BlockSpec((tq, dq), lambda i: (i, 0)),
                  pl.BlockSpec((n_c, dkv), lambda i: (i // n_qb, 0)),
                  pl.BlockSpec((n_c, dkv), lambda i: (i // n_qb, 0)),
                  pl.BlockSpec(tab.shape, lambda i: (0, 0)),
                  pl.BlockSpec(ovt.shape, lambda i: (0, 0))] + cast_specs,
        out_specs=[pl.BlockSpec((tq, dq), lambda i: (i, 0)),
                   pl.BlockSpec((tq, dkv), lambda i: (i, 0))] + cast_specs,
        out_shape=[jax.ShapeDtypeStruct((T, dq), F32),
                   jax.ShapeDtypeStruct((T, dkv), BF16)] + [jax.ShapeDtypeStruct(w.shape, BF16) for w in cast_also],
        compiler_params=_params(("parallel",)),
    )(qkv, kc, vc, tab, ovt, *cast_also)


def _repack_kv_kernel(ks_ref, vs_ref, kw_ref, vw_ref, ksp_ref, vst_ref, kwp_ref, vwt_ref):
    first = pl.program_id(1) == 0

    @pl.when(first)
    def _():
        for ref in (ksp_ref, vst_ref, kwp_ref, vwt_ref):
            ref[...] = jnp.zeros(ref.shape, ref.dtype)

    @pl.when(jnp.logical_not(first))
    def _():
        ksp_ref[...] = ks_ref[...]
        kwp_ref[...] = kw_ref[...]
        vst_ref[...] = vs_ref[...].T
        vwt_ref[...] = vw_ref[...].T


def repack_kv(qkv, B, S, first_col_block):
    dkv = N_KV_HEADS * HEAD_DIM
    nt = S // WINDOW
    src = lambda col: pl.BlockSpec((WINDOW, dkv), lambda b, i, col=col: (b * nt + jnp.maximum(i - 1, 0), col))
    rows = pl.BlockSpec((WINDOW, dkv), lambda b, i: (b * (nt + 1) + i, 0))
    cols = pl.BlockSpec((dkv, WINDOW), lambda b, i: (b, i))
    return pl.pallas_call(
        _repack_kv_kernel,
        grid=(B, nt + 1),
        in_specs=[src(first_col_block + k) for k in range(4)],
        out_specs=[rows, cols, rows, cols],
        out_shape=[jax.ShapeDtypeStruct((B * (S + WINDOW), dkv), qkv.dtype),
                   jax.ShapeDtypeStruct((B * dkv, S + WINDOW), qkv.dtype)] * 2,
        compiler_params=_params(("parallel", "arbitrary")),
    )(qkv, qkv, qkv, qkv)


def _flash_update(s_raw, vt, bias, m_ref, l_ref, acc_ref, scale):
    m_old = m_ref[...]
    if bias.shape[0] == 1:
        s = s_raw * scale
        m_new = jnp.maximum(m_old, jnp.max(s, axis=0, keepdims=True) + bias)
        p = jnp.exp2(s - (m_new - bias))
    else:
        s = s_raw * scale + bias
        m_new = jnp.maximum(m_old, jnp.max(s, axis=0, keepdims=True))
        p = jnp.exp2(s - m_new)
    alpha = jnp.exp2(m_old - m_new)
    l_ref[...] = alpha * l_ref[...] + jnp.sum(p, axis=0, keepdims=True)
    acc_ref[...] = alpha * acc_ref[...] + _dot(vt, p.astype(BF16))
    m_ref[...] = m_new


def _flash_chunk(k, vt, q, bias, m_ref, l_ref, acc_ref, scale):
    _flash_update(_dot_nt(k, q), vt, bias, m_ref, l_ref, acc_ref, scale)


def _sel_win_kernel(q_ref, selb_ref, ks_ref, vst_ref, kw_ref, vwt_ref, e2_ref, tab_ref, ocmp_ref,
                    gl_ref, gnw_ref, o_ref, m_ref, l_ref, acc_ref, sa_ref, sb_ref, *, n_qb, scale):
    c = pl.program_id(0) % n_qb
    tq = q_ref.shape[0]
    rows = GQA * tq
    n_back = WINDOW // tq
    near = WINDOW + tq
    j_io = lax.broadcasted_iota(jnp.int32, (tq, tq), 0)
    i_io = lax.broadcasted_iota(jnp.int32, (tq, tq), 1)
    dij = i_io - j_io
    dij4 = jnp.concatenate([dij] * GQA, axis=1)
    causal = dij4 >= 0
    pad_col = jnp.where(lax.broadcasted_iota(jnp.int32, (rows, HEAD_DIM), 1) == HEAD_DIM - 1, 1.0, 0.0).astype(BF16)
    sig = jax.nn.sigmoid(gl_ref[...])
    near0 = pl.multiple_of(c * tq, tq)
    n_far = jnp.maximum(c - n_back, 0)
    n_full = n_far // n_back
    n_rem = n_far - n_full * n_back

    def hsl(h):
        return slice(h * HEAD_DIM, (h + 1) * HEAD_DIM)

    def far0(i):
        return pl.multiple_of(WINDOW + i * WINDOW, WINDOW)

    def reset():
        m_ref[...] = jnp.full(m_ref.shape, NEG_BIG, F32)
        l_ref[...] = jnp.zeros(l_ref.shape, F32)
        acc_ref[...] = jnp.zeros(acc_ref.shape, F32)

    def result():
        ot = acc_ref[...] / l_ref[...]
        return [ot[:, g * tq:(g + 1) * tq].T for g in range(GQA)]

    for h in range(N_KV_HEADS):
        q4 = jnp.concatenate([q_ref[:, (h * GQA + g) * HEAD_DIM:(h * GQA + g + 1) * HEAD_DIM]
                              for g in range(GQA)], axis=0)
        sb4 = jnp.concatenate([selb_ref[:, hsl(h)]] * GQA, axis=0)
        q_aug = jnp.concatenate([q4, sb4 + pad_col], axis=1)
        q_win = jnp.concatenate([q4, pad_col], axis=1)
        tabs = [tab_ref[h * GQA + g:h * GQA + g + 1, :] * LOG2E for g in range(GQA)]
        d0 = jnp.concatenate([_bias_from_dist(t, dij) for t in tabs], axis=1)
        d1 = jnp.concatenate([_bias_from_dist(t, dij + tq) for t in tabs], axis=1)
        far = jnp.concatenate([t[:, HEAD_DIM - 1:HEAD_DIM] + jnp.zeros((1, tq), F32) for t in tabs], axis=1)
        far_t = jnp.broadcast_to(far, (tq, rows))
        diag = jnp.where(causal, d0, NEG_BIG)
        bias_sel = jnp.concatenate([far_t] * (n_back - 1) + [d1, diag], axis=0)
        bias_win = jnp.concatenate([jnp.where(dij4 < 0, far_t, NEG_BIG)] + [far_t] * (n_back - 2) + [d1, diag],
                                   axis=0)

        reset()
        e2_near = e2_ref[pl.ds(near0, near), :]
        k_near = jnp.concatenate([ks_ref[pl.ds(near0, near), hsl(h)], e2_near], axis=1)
        _flash_chunk(k_near, vst_ref[hsl(h), pl.ds(near0, near)], q_aug, bias_sel, m_ref, l_ref, acc_ref, scale)

        def far_scores(i):
            r0 = far0(i)
            k = jnp.concatenate([ks_ref[pl.ds(r0, WINDOW), hsl(h)], e2_ref[pl.ds(r0, WINDOW), :]], axis=1)
            return _dot_nt(k, q_aug)

        def far_update(s_raw, i, bias):
            _flash_update(s_raw, vst_ref[hsl(h), pl.ds(far0(i), WINDOW)], bias, m_ref, l_ref, acc_ref, scale)

        sa_ref[...] = far_scores(0)

        def body(i, carry):
            sb_ref[...] = far_scores(2 * i + 1)
            far_update(sa_ref[...], 2 * i, far)
            sa_ref[...] = far_scores(2 * i + 2)
            far_update(sb_ref[...], 2 * i + 1, far)
            return carry

        lax.fori_loop(0, n_full // 2, body, 0)

        @pl.when(n_full % 2 == 1)
        def _():
            far_update(sa_ref[...], n_full - 1, far)

        @pl.when(n_rem > 0)
        def _():
            if n_back == 2:
                r0 = far0(n_full)
                k = jnp.concatenate([ks_ref[pl.ds(r0, tq), hsl(h)], e2_ref[pl.ds(r0, tq), :]], axis=1)
                _flash_chunk(k, vst_ref[hsl(h), pl.ds(r0, tq)], q_aug, far, m_ref, l_ref, acc_ref, scale)
            else:
                live = lax.broadcasted_iota(jnp.int32, (WINDOW, rows), 0) < n_rem * tq
                far_update(far_scores(n_full), n_full,
                           jnp.where(live, jnp.broadcast_to(far, (WINDOW, rows)), NEG_BIG))

        o_sel = result()

        reset()
        kw_near = jnp.concatenate([kw_ref[pl.ds(near0, near), hsl(h)], e2_near], axis=1)
        _flash_chunk(kw_near, vwt_ref[hsl(h), pl.ds(near0, near)], q_win, bias_win, m_ref, l_ref, acc_ref, scale)
        o_win = result()

        for g in range(GQA):
            hd = h * GQA + g
            o = (sig[:, 3 * hd:3 * hd + 1] * ocmp_ref[:, hsl(hd)]
                 + sig[:, 3 * hd + 1:3 * hd + 2] * o_sel[g]
                 + sig[:, 3 * hd + 2:3 * hd + 3] * o_win[g])
            y = o * lax.rsqrt(jnp.mean(o * o, axis=-1, keepdims=True) + EPS)
            o_ref[:, hsl(hd)] = (y * gnw_ref[:, hsl(hd)]).astype(o_ref.dtype)


def sel_win(qkv, selb, ksp, vstp, kwp, vwtp, e2p, tab, ocmp, gl, gnw, B, S, tq=TQ_SEL):
    T = qkv.shape[0]
    n_qb = S // tq
    dq = N_Q_HEADS * HEAD_DIM
    dkv = N_KV_HEADS * HEAD_DIM
    SP = S + WINDOW
    kern = functools.partial(_sel_win_kernel, n_qb=n_qb, scale=HEAD_DIM ** -0.5 * LOG2E)
    k_spec = pl.BlockSpec((SP, dkv), lambda i: (i // n_qb, 0))
    vt_spec = pl.BlockSpec((dkv, SP), lambda i: (i // n_qb, 0))
    return pl.pallas_call(
        kern,
        grid=(T // tq,),
        in_specs=[pl.BlockSpec((tq, dq), lambda i: (i, 0)),
                  pl.BlockSpec((tq, dkv), lambda i: (i, 0)),
                  k_spec, vt_spec, k_spec, vt_spec,
                  pl.BlockSpec((SP, HEAD_DIM), lambda i: (0, 0)),
                  pl.BlockSpec(tab.shape, lambda i: (0, 0)),
                  pl.BlockSpec((tq, dq), lambda i: (i, 0)),
                  pl.BlockSpec((tq, HEAD_DIM), lambda i: (i, 0)),
                  pl.BlockSpec((1, dq), lambda i: (0, 0))],
        out_specs=pl.BlockSpec((tq, dq), lambda i: (i, 0)),
        out_shape=jax.ShapeDtypeStruct((T, dq), BF16),
        scratch_shapes=[pltpu.VMEM((1, GQA * tq), F32), pltpu.VMEM((1, GQA * tq), F32),
                        pltpu.VMEM((HEAD_DIM, GQA * tq), F32),
                        pltpu.VMEM((WINDOW, GQA * tq), F32), pltpu.VMEM((WINDOW, GQA * tq), F32)],
        compiler_params=_params(("parallel",)),
    )(qkv, selb, ksp, vstp, kwp, vwtp, e2p, tab, ocmp, gl, gnw)


def _conv_kernel(b_ref, c_ref, h_ref, cp_ref, hp_ref, cw_ref, cb_ref, gnw_ref, o_ref, u_ref, *, tiles_per_seq):
    tm = b_ref.shape[0]
    first = (pl.program_id(0) % tiles_per_seq) == 0
    u_prev = cp_ref[...] * hp_ref[...]
    u_ref[0:8, :] = jnp.where(first, 0.0, u_prev)
    u = c_ref[...] * h_ref[...]
    u_ref[8:8 + tm, :] = u
    y = (cw_ref[0:1, :] * u_ref[6:6 + tm, :] + cw_ref[1:2, :] * u_ref[7:7 + tm, :]
         + cw_ref[2:3, :] * u + cb_ref[...])
    o = b_ref[...] * y
    n_groups = o.shape[1] // HEAD_DIM
    for g in range(n_groups):
        sl = slice(g * HEAD_DIM, (g + 1) * HEAD_DIM)
        og = o[:, sl]
        yg = og * lax.rsqrt(jnp.mean(og * og, axis=-1, keepdims=True) + EPS)
        o_ref[:, sl] = (yg * gnw_ref[:, sl]).astype(o_ref.dtype)


def conv_mixer(gc, conv_w, conv_b, gnw, S, tm):
    T = gc.shape[0]
    dc = conv_w.shape[1]
    tps = S // tm
    kern = functools.partial(_conv_kernel, tiles_per_seq=tps)
    prev = lambda col: pl.BlockSpec((8, dc), lambda i, col=col: (jnp.maximum(i * (tm // 8) - 1, 0), col))
    cur = lambda col: pl.BlockSpec((tm, dc), lambda i, col=col: (i, col))
    return pl.pallas_call(
        kern,
        grid=(T // tm,),
        in_specs=[cur(0), cur(1), cur(2), prev(1), prev(2),
                  pl.BlockSpec((8, dc), lambda i: (0, 0)),
                  pl.BlockSpec((1, dc), lambda i: (0, 0)),
                  pl.BlockSpec((1, dc), lambda i: (0, 0))],
        out_specs=pl.BlockSpec((tm, dc), lambda i: (i, 0)),
        out_shape=jax.ShapeDtypeStruct((T, dc), BF16),
        scratch_shapes=[pltpu.VMEM((tm + 8, dc), F32)],
        compiler_params=_params(("parallel",)),
    )(gc, gc, gc, gc, gc, conv_w, conv_b, gnw)


def _out_proj_kernel(ma_ref, mc_ref, wa_ref, wc_ref, x_ref, o_ref):
    o_ref[...] = x_ref[...] + _dot(ma_ref[...], wa_ref[...]) + _dot(mc_ref[...], wc_ref[...])


def out_proj(ma, mc, w, x, tm, tn):
    T, da = ma.shape
    dc = mc.shape[1]
    D = x.shape[1]
    assert da == dc and w.shape[0] == da + dc
    return pl.pallas_call(
        _out_proj_kernel,
        grid=(T // tm, D // tn),
        in_specs=[pl.BlockSpec((tm, da), lambda i, j: (i, 0)),
                  pl.BlockSpec((tm, dc), lambda i, j: (i, 0)),
                  pl.BlockSpec((da, tn), lambda i, j: (0, j)),
                  pl.BlockSpec((dc, tn), lambda i, j: (1, j)),
                  pl.BlockSpec((tm, tn), lambda i, j: (i, j))],
        out_specs=pl.BlockSpec((tm, tn), lambda i, j: (i, j)),
        out_shape=jax.ShapeDtypeStruct((T, D), F32),
        compiler_params=_params(("parallel", "arbitrary")),
    )(ma, mc, w, w, x)


def _sort_network(n):
    pairs = []
    p = 1
    while p < n:
        k = p
        while k >= 1:
            for j in range(k % p, n - k, 2 * k):
                for i in range(min(k, n - j - k)):
                    if (i + j) // (2 * p) == (i + j + k) // (2 * p):
                        pairs.append((i + j, i + j + k))
            k //= 2
        p *= 2
    return pairs


def _peer_route_kernel(q_ref, sk_ref, u_ref, v_ref, e_ref, g_ref, ub_ref, vb_ref, sv_ref, si_ref):
    ub_ref[...] = u_ref[...].astype(BF16)
    vb_ref[...] = v_ref[...].astype(BF16)
    tt = q_ref.shape[0]
    K = PEER_TOPK
    sub = SUBLANES
    n_io = lax.broadcasted_iota(jnp.int32, (N_KEYS, tt), 0).astype(F32)
    r8 = lax.broadcasted_iota(jnp.int32, (sub, tt), 0).astype(F32)
    ninf = -jnp.inf
    lens = [K // (a + 1) for a in range(N_FIXED_A)]
    lens += [max(K // (b + 1) - N_FIXED_A, 0) for b in range(sub - N_FIXED_A)]
    assert sum(lens) == sum(K // (a + 1) for a in range(K)) and lens[-1] == 0
    fixed_a = r8 < float(N_FIXED_A)
    list_len = jnp.zeros((sub, tt), F32)
    for row, n in enumerate(lens):
        list_len = jnp.where(r8 == float(row), float(n), list_len)

    for h in range(PEER_HEADS):
        for c in range(2):
            col = (h * 2 + c) * HEAD_DIM
            s = _dot_nt(sk_ref[h * 2 + c], q_ref[:, col:col + HEAD_DIM])
            n_col = N_KEYS // sub
            vals = [s[j * sub:(j + 1) * sub, :] for j in range(n_col)]
            idxs = [n_io[j * sub:(j + 1) * sub, :] for j in range(n_col)]
            for lo, hi in _sort_network(n_col):
                swap = (vals[hi] > vals[lo]) | ((vals[hi] == vals[lo]) & (idxs[hi] < idxs[lo]))
                vals[lo], vals[hi] = jnp.where(swap, vals[hi], vals[lo]), jnp.where(swap, vals[lo], vals[hi])
                idxs[lo], idxs[hi] = jnp.where(swap, idxs[hi], idxs[lo]), jnp.where(swap, idxs[lo], idxs[hi])
            for k in range(K):
                m = jnp.max(vals[0], axis=0, keepdims=True)
                idx = jnp.min(jnp.where(vals[0] == m, idxs[0], float(N_KEYS)), axis=0, keepdims=True)
                sv_ref[c, k:k + 1, :] = m
                si_ref[c, k:k + 1, :] = idx
                win = idxs[0] == idx
                for j in range(min(n_col, K) - 1 - k):
                    vals[j] = jnp.where(win, vals[j + 1], vals[j])
                    idxs[j] = jnp.where(win, idxs[j + 1], idxs[j])
        sv0, sv1 = sv_ref[0], sv_ref[1]
        si0, si1 = si_ref[0], si_ref[1]
        sv1_low = pltpu.roll(sv1[0:sub, :], N_FIXED_A, 0)
        sv0_top = sv0[0:sub, :]
        lv, lf = [], []
        for dep in range(K):
            a_dep = min(N_FIXED_A + dep, K - 1)
            val = (jnp.where(fixed_a, sv0_top, sv0[a_dep:a_dep + 1, :])
                   + jnp.where(fixed_a, sv1[dep:dep + 1, :], sv1_low))
            lv.append(jnp.where(list_len > dep, val, ninf))
            lf.append(jnp.where(fixed_a, r8 * float(K) + dep, (N_FIXED_A + dep) * float(K) + r8 - N_FIXED_A))
        cvs, fls = [], []
        for k in range(K):
            m = jnp.max(lv[0], axis=0, keepdims=True)
            fsel = jnp.min(jnp.where(lv[0] == m, lf[0], 1e9), axis=0, keepdims=True)
            win = lf[0] == fsel
            cvs.append(m)
            fls.append(fsel)
            for dep in range(K - 1 - k):
                lv[dep] = jnp.where(win, lv[dep + 1], lv[dep])
                lf[dep] = jnp.where(win, lf[dep + 1], lf[dep])
        cv = jnp.concatenate(cvs, axis=0)
        fl = jnp.concatenate(fls, axis=0)
        a_sel = jnp.floor(fl * (1.0 / K))
        b_sel = fl - a_sel * K
        i1 = jnp.zeros_like(fl)
        i2 = jnp.zeros_like(fl)
        for r in range(K):
            i1 = jnp.where(a_sel == r, si0[r:r + 1, :], i1)
            i2 = jnp.where(b_sel == r, si1[r:r + 1, :], i2)
        ex = i1 * float(N_KEYS) + i2
        ev = jnp.exp(cv - jnp.max(cv, axis=0, keepdims=True))
        gates = ev / jnp.sum(ev, axis=0, keepdims=True)
        e_ref[0, h * K:(h + 1) * K, :] = ex.astype(jnp.int32)
        g_ref[0, h * K:(h + 1) * K, :] = gates


def peer_route(qp, subkeys, u, v):
    T = qp.shape[0]
    tt = TT_ROUTE
    P = PEER_HEADS * PEER_TOPK
    nt = T // tt
    E, D = u.shape
    slab = E // nt
    assert slab * nt == E and slab % 16 == 0
    tab_spec = pl.BlockSpec((slab, D), lambda i: (i, 0))
    return pl.pallas_call(
        _peer_route_kernel,
        grid=(nt,),
        in_specs=[pl.BlockSpec((tt, qp.shape[1]), lambda i: (i, 0)),
                  pl.BlockSpec(subkeys.shape, lambda i: (0, 0, 0)),
                  tab_spec, tab_spec],
        out_specs=[pl.BlockSpec((1, P, tt), lambda i: (i, 0, 0))] * 2 + [tab_spec, tab_spec],
        out_shape=[jax.ShapeDtypeStruct((nt, P, tt), jnp.int32),
                   jax.ShapeDtypeStruct((nt, P, tt), F32),
                   jax.ShapeDtypeStruct((E, D), BF16), jax.ShapeDtypeStruct((E, D), BF16)],
        scratch_shapes=[pltpu.VMEM((2, PEER_TOPK, tt), F32), pltpu.VMEM((2, PEER_TOPK, tt), F32)],
        compiler_params=_params(("parallel",)),
    )(qp, subkeys, u, v)


def _peer_gbuild_kernel(e_ref, g_ref, o_ref, i1_ref, i2_ref, gt_ref):
    tt = e_ref.shape[2]
    e = e_ref[0].T
    i1_ref[...] = e >> 7
    i2_ref[...] = e & (N_KEYS - 1)
    gt_ref[...] = g_ref[0].T
    P = e.shape[1]
    k_io = lax.broadcasted_iota(jnp.int32, (N_KEYS, P), 0)
    group = 64

    def body(tg, carry):
        base = pl.multiple_of(tg * group, group)
        r1s = i1_ref[pl.ds(base, group), :]
        r2s = i2_ref[pl.ds(base, group), :]
        rgs = gt_ref[pl.ds(base, group), :]
        for u in range(group):
            lhs = jnp.where(k_io == r1s[u:u + 1, :], rgs[u:u + 1, :], 0.0).astype(BF16)
            rhs = jnp.where(k_io == r2s[u:u + 1, :], 1.0, 0.0).astype(BF16)
            g_t = _dot_nt(lhs, rhs).astype(BF16).astype(F32)
            lo = pltpu.bitcast(g_t[:HALF_KEYS], jnp.uint32) >> 16
            hi = pltpu.bitcast(g_t[HALF_KEYS:], jnp.uint32)
            words = hi | lo
            for a in range(HALF_KEYS // PACK_ROWS):
                o_ref[a, base + u] = words[a * PACK_ROWS:(a + 1) * PACK_ROWS, :]
        return carry

    lax.fori_loop(0, tt // group, body, 0)


def peer_gbuild(ex, gates):
    nt, P, tt = ex.shape
    T = nt * tt
    n_a = HALF_KEYS // PACK_ROWS
    return pl.pallas_call(
        _peer_gbuild_kernel,
        grid=(nt,),
        in_specs=[pl.BlockSpec((1, P, tt), lambda i: (i, 0, 0))] * 2,
        out_specs=pl.BlockSpec((n_a, tt, PACK_ROWS, N_KEYS), lambda i: (0, i, 0, 0)),
        out_shape=jax.ShapeDtypeStruct((n_a, T, PACK_ROWS, N_KEYS), jnp.uint32),
        scratch_shapes=[pltpu.VMEM((tt, P), jnp.int32), pltpu.VMEM((tt, P), jnp.int32),
                        pltpu.VMEM((tt, P), F32)],
        compiler_params=_params(("parallel",)),
    )(ex, gates)


def _peer_dense_kernel(h_ref, nw_ref, fw_ref, u_ref, v_ref, gm_ref, o_ref, hn_ref, acc_ref):
    j = pl.program_id(1)
    tt = h_ref.shape[0]

    @pl.when(j == 0)
    def _():
        x = h_ref[...]
        y = x * lax.rsqrt(jnp.mean(x * x, axis=-1, keepdims=True) + EPS)
        hn_ref[...] = (y * nw_ref[...]).astype(BF16)
        acc_ref[...] = jnp.zeros(acc_ref.shape, F32)

    a = _dot_nt(hn_ref[...], u_ref[...])
    shift = (16 * (1 - j % 2)).astype(jnp.uint32)
    gm = jnp.concatenate(
        [pltpu.bitcast((gm_ref[pl.ds(k, tt, stride=PACK_ROWS), :] << shift) & jnp.uint32(0xFFFF0000), F32)
         for k in range(PACK_ROWS)], axis=1)
    w = (gm * jax.nn.gelu(a)).astype(BF16)
    acc_ref[...] += _dot(w, v_ref[...])

    @pl.when(j == pl.num_programs(1) - 1)
    def _():
        x = h_ref[...] + acc_ref[...]
        y = x * lax.rsqrt(jnp.mean(x * x, axis=-1, keepdims=True) + EPS)
        o_ref[...] = y * fw_ref[...]


def peer_dense(h, ffn_nw, final_nw, u, v, gm_words, tt):
    T, D = h.shape
    E = u.shape[0]
    ec = PACK_ROWS * N_KEYS
    n_a = gm_words.shape[0]
    assert E == 2 * n_a * ec
    gm2 = gm_words.reshape(n_a, T * PACK_ROWS, N_KEYS)
    tab_spec = pl.BlockSpec((ec, D), lambda i, j: ((j % 2) * n_a + j // 2, 0))
    return pl.pallas_call(
        _peer_dense_kernel,
        grid=(T // tt, E // ec),
        in_specs=[pl.BlockSpec((tt, D), lambda i, j: (i, 0)),
                  pl.BlockSpec((1, D), lambda i, j: (0, 0)),
                  pl.BlockSpec((1, D), lambda i, j: (0, 0)),
                  tab_spec, tab_spec,
                  pl.BlockSpec((None, tt * PACK_ROWS, N_KEYS), lambda i, j: (j // 2, i, 0))],
        out_specs=pl.BlockSpec((tt, D), lambda i, j: (i, 0)),
        out_shape=jax.ShapeDtypeStruct((T, D), F32),
        scratch_shapes=[pltpu.VMEM((tt, D), BF16), pltpu.VMEM((tt, D), F32)],
        compiler_params=_params(("parallel", "arbitrary")),
    )(h, ffn_nw.reshape(1, D), final_nw.reshape(1, D), u, v, gm2)


def _t5_bucket_np(n_dist):
    d = np.arange(n_dist)
    max_exact = NUM_BUCKETS // 2
    nf = np.maximum(d, 1).astype(np.float64)
    large = max_exact + (np.log(nf / max_exact) / math.log(MAX_DISTANCE / max_exact)
                         * (NUM_BUCKETS - max_exact)).astype(np.int64)
    large = np.minimum(large, NUM_BUCKETS - 1)
    return np.where(d < max_exact, d, large).astype(np.int32)


def _overlap_t_np(S):
    n_c = (S - L_CMP) // STRIDE_CMP + 1
    n_sel = S // L_SEL
    pos = np.arange(n_c)[:, None] * STRIDE_CMP + np.arange(L_CMP)[None, :]
    m = np.zeros((n_c + 1, n_sel), np.float32)
    np.add.at(m, (np.repeat(np.arange(n_c), L_CMP), (pos // L_SEL).reshape(-1)), 1.0 / L_CMP)
    return np.ascontiguousarray(m.T)


def _block_onehot_padded_np(S):
    assert S // L_SEL < HEAD_DIM
    e2 = np.zeros((WINDOW + S, HEAD_DIM), np.float32)
    e2[WINDOW + np.arange(S), np.arange(S) // L_SEL] = 1.0
    e2[:WINDOW, HEAD_DIM - 1] = SEL_OFF
    return e2


def nsa_conv_mix(xt, B, S, attn_norm_w, w_in, w_cmp_k, w_cmp_v, cmp_pos, conv_w, conv_b,
                 attn_gnw, conv_gnw, rel_bias, tm=TM_PROJ, tm_conv=TM_CONV, cast_also=()):
    T, D = xt.shape
    dq = N_Q_HEADS * HEAD_DIM
    dkv = N_KV_HEADS * HEAD_DIM
    n_attn = dq + 6 * dkv
    n_gate = 3 * N_Q_HEADS
    dc = (w_in.shape[1] - n_attn - n_gate) // 3
    w_t = w_in.T
    w_attn = cast_rows(w_t, 0, n_attn)
    w_conv = cast_rows(w_t, n_attn + n_gate, 3 * dc)
    w_gate = jnp.pad(w_t[n_attn:n_attn + n_gate], ((0, HEAD_DIM - n_gate), (0, 0))).astype(BF16)

    qkv = norm_matmul(xt, attn_norm_w, w_attn, BF16, tm, n_attn // 2, w_is_nk=True)
    gc, gl = norm_matmul(xt, attn_norm_w, w_conv, F32, tm, dc, w_is_nk=True, w_side=w_gate)

    n16 = S // STRIDE_CMP
    k16 = qkv[:, dq:dq + dkv].reshape(B * n16, STRIDE_CMP * dkv)
    v16 = qkv[:, dq + dkv:dq + 2 * dkv].reshape(B * n16, STRIDE_CMP * dkv)

    def wflat(w, lo):
        return w[lo:lo + STRIDE_CMP].reshape(STRIDE_CMP * HEAD_DIM, HEAD_DIM).astype(BF16)

    def posrow(lo):
        return cmp_pos[lo:lo + STRIDE_CMP].reshape(1, STRIDE_CMP * HEAD_DIM)

    kc, vc = compress(k16, v16, posrow(0), posrow(STRIDE_CMP),
                      wflat(w_cmp_k, 0), wflat(w_cmp_k, STRIDE_CMP),
                      wflat(w_cmp_v, 0), wflat(w_cmp_v, STRIDE_CMP), B)

    tab = rel_bias[_t5_bucket_np(HEAD_DIM)].T
    ovt = jnp.asarray(_overlap_t_np(S), BF16)
    e2p = jnp.asarray(_block_onehot_padded_np(S), BF16)

    ocmp, selb, *casts = cmp_select(qkv, kc, vc, tab, ovt, B, S, cast_also=cast_also)
    ksp, vstp, kwp, vwtp = repack_kv(qkv, B, S, first_col_block=dq // dkv + 2)
    attn = sel_win(qkv, selb, ksp, vstp, kwp, vwtp, e2p, tab, ocmp, gl, attn_gnw.reshape(1, dq), B, S)
    cw8 = jnp.pad(conv_w, ((0, 8 - conv_w.shape[0]), (0, 0)))
    conv = conv_mixer(gc, cw8, conv_b.reshape(1, dc), conv_gnw.reshape(1, dc), S, tm_conv)
    return (attn, conv, *casts)


def peer_block(h, ffn_nw, final_nw, peer_wq, peer_subkeys, peer_u, peer_v, tm=TM_PROJ, tt=TT_DENSE):
    T, D = h.shape
    qp = norm_matmul(h, ffn_nw, peer_wq.astype(BF16), BF16, tm, TN_OUT)
    sk = peer_subkeys.reshape(PEER_HEADS * 2, N_KEYS, peer_subkeys.shape[-1]).astype(BF16)
    ex, gates, u_bf, v_bf = peer_route(qp, sk, peer_u, peer_v)
    gm = peer_gbuild(ex, gates)
    return peer_dense(h, ffn_nw, final_nw, u_bf, v_bf, gm, tt)


def kernel(x, attn_norm_w, w_in, w_cmp_k, w_cmp_v, cmp_pos, conv_w, conv_b, attn_group_norm_w,
           conv_group_norm_w, w_out, rel_bias, ffn_norm_w, peer_wq, peer_subkeys, peer_u, peer_v,
           final_norm_w):
    B, S, D = x.shape
    T = B * S
    xt = x.reshape(T, D)
    attn, conv, wo, wq = nsa_conv_mix(xt, B, S, attn_norm_w[0], w_in[0], w_cmp_k[0], w_cmp_v[0], cmp_pos[0],
                                      conv_w[0], conv_b[0], attn_group_norm_w[0], conv_group_norm_w[0], rel_bias,
                                      cast_also=(w_out[0], peer_wq[0]))
    h = out_proj(attn, conv, wo, xt, TM_PROJ, TN_OUT)
    out = peer_block(h, ffn_norm_w[0], final_norm_w, wq, peer_subkeys[0], peer_u[0], peer_v[0])
    return out.reshape(B, S, D)
```

```python
import functools
import math

import jax
import jax.numpy as jnp
import numpy as np
from jax import lax
from jax.experimental import pallas as pl
from jax.experimental.pallas import tpu as pltpu

F32 = jnp.float32
BF16 = jnp.bfloat16

HEAD_DIM = 128
N_KV_HEADS = 2
GQA = 4
N_Q_HEADS = N_KV_HEADS * GQA
L_CMP = 32
STRIDE_CMP = 16
L_SEL = 64
N_SEL = 16
WINDOW = 512
Q_BLOCK = 128
FORCED_SCORE = float(GQA + 1)
NUM_BUCKETS = 32
MAX_DISTANCE = 128
PEER_HEADS = 8
N_KEYS = 128
PEER_TOPK = 16
EPS = 1e-6
NEG_BIG = -1e30
SEL_OFF = -float(2 ** 30)
LOG2E = math.log2(math.e)
N_FIXED_A = 4

SUBLANES = 8
LANES = 128
MXU_DIM = 256
V7X_VMEM_BYTES = 64 * 1024 * 1024
VMEM_LIMIT = V7X_VMEM_BYTES * 7 // 8

HALF_KEYS = N_KEYS // 2
PACK_ROWS = SUBLANES

TM_PROJ = 4 * MXU_DIM
TN_OUT = 4 * MXU_DIM
TM_CONV = 2 * MXU_DIM
TQ_CMP = 4 * Q_BLOCK
TQ_SEL = 2 * Q_BLOCK
TT_ROUTE = LANES
TT_DENSE = 2 * MXU_DIM
CAST_ROWS = MXU_DIM


def _dot(a, b):
    return jnp.dot(a, b, preferred_element_type=F32)


def _dot_nt(a, b):
    return lax.dot_general(a, b, (((1,), (1,)), ((), ())), preferred_element_type=F32)


def _params(sem, vmem=VMEM_LIMIT):
    return pltpu.CompilerParams(dimension_semantics=sem, vmem_limit_bytes=vmem)


def _norm_matmul_kernel(x_ref, nw_ref, w_ref, *rest, w_is_nk, has_side):
    if has_side:
        ws_ref, o_ref, os_ref, xn_ref = rest
    else:
        o_ref, xn_ref = rest

    @pl.when(pl.program_id(1) == 0)
    def _():
        x = x_ref[...]
        y = x * lax.rsqrt(jnp.mean(x * x, axis=-1, keepdims=True) + EPS)
        xn_ref[...] = (y * nw_ref[...]).astype(BF16)
        if has_side:
            os_ref[...] = _dot_nt(xn_ref[...], ws_ref[...])

    dot = _dot_nt if w_is_nk else _dot
    o_ref[...] = dot(xn_ref[...], w_ref[...]).astype(o_ref.dtype)


def norm_matmul(x, norm_w, w, out_dtype, tm, tn, w_is_nk=False, w_side=None):
    T, D = x.shape
    N = w.shape[0] if w_is_nk else w.shape[1]
    w_spec = pl.BlockSpec((tn, D), lambda i, j: (j, 0)) if w_is_nk else pl.BlockSpec((D, tn), lambda i, j: (0, j))
    in_specs = [pl.BlockSpec((tm, D), lambda i, j: (i, 0)), pl.BlockSpec((1, D), lambda i, j: (0, 0)), w_spec]
    out_specs = [pl.BlockSpec((tm, tn), lambda i, j: (i, j))]
    out_shape = [jax.ShapeDtypeStruct((T, N), out_dtype)]
    args = [x, norm_w.reshape(1, D), w]
    if w_side is not None:
        n_side = w_side.shape[0]
        in_specs.append(pl.BlockSpec((n_side, D), lambda i, j: (0, 0)))
        out_specs.append(pl.BlockSpec((tm, n_side), lambda i, j: (i, 0)))
        out_shape.append(jax.ShapeDtypeStruct((T, n_side), F32))
        args.append(w_side)
    out = pl.pallas_call(
        functools.partial(_norm_matmul_kernel, w_is_nk=w_is_nk, has_side=w_side is not None),
        grid=(T // tm, N // tn),
        in_specs=in_specs,
        out_specs=out_specs,
        out_shape=out_shape,
        scratch_shapes=[pltpu.VMEM((tm, D), BF16)],
        compiler_params=_params(("parallel", "arbitrary")),
    )(*args)
    return out if w_side is not None else out[0]


def _cast_rows_kernel(w_ref, o_ref):
    o_ref[...] = w_ref[...].astype(o_ref.dtype)


def cast_rows(w, row0, n_rows, tr=CAST_ROWS):
    K = w.shape[1]
    assert row0 % SUBLANES == 0 and n_rows % tr == 0
    return pl.pallas_call(
        _cast_rows_kernel,
        grid=(n_rows // tr,),
        in_specs=[pl.BlockSpec((pl.Element(tr), pl.Element(K)), lambda i: (pl.multiple_of(row0 + i * tr, SUBLANES), 0))],
        out_specs=pl.BlockSpec((tr, K), lambda i: (i, 0)),
        out_shape=jax.ShapeDtypeStruct((n_rows, K), BF16),
        compiler_params=_params(("parallel",)),
    )(w)


def _compress_kernel(k_ref, v_ref, pa_ref, pb_ref, wka_ref, wkb_ref, wva_ref, wvb_ref, kc_ref, vc_ref):
    dkv = kc_ref.shape[1]

    def one(x_ref, wa_ref, wb_ref, o_ref):
        for h in range(N_KV_HEADS):
            x = jnp.concatenate([x_ref[:, l * dkv + h * HEAD_DIM:l * dkv + (h + 1) * HEAD_DIM]
                                 for l in range(STRIDE_CMP)], axis=1).astype(F32)
            a = _dot((x + pa_ref[...]).astype(BF16), wa_ref[...])
            b = _dot((x + pb_ref[...]).astype(BF16), wb_ref[...])
            n = b.shape[0]
            o_ref[:, h * HEAD_DIM:(h + 1) * HEAD_DIM] = (a + pltpu.roll(b, n - 1, 0)).astype(o_ref.dtype)

    one(k_ref, wka_ref, wkb_ref, kc_ref)
    one(v_ref, wva_ref, wvb_ref, vc_ref)


def compress(k16, v16, pos_a, pos_b, wka, wkb, wva, wvb, B):
    R, C = k16.shape
    nb = R // B
    dkv = C // STRIDE_CMP
    full = lambda shp: pl.BlockSpec(shp, lambda b: (0, 0))
    return pl.pallas_call(
        _compress_kernel,
        grid=(B,),
        in_specs=[pl.BlockSpec((nb, C), lambda b: (b, 0)),
                  pl.BlockSpec((nb, C), lambda b: (b, 0)),
                  full(pos_a.shape), full(pos_b.shape),
                  full(wka.shape), full(wkb.shape), full(wva.shape), full(wvb.shape)],
        out_specs=[pl.BlockSpec((nb, dkv), lambda b: (b, 0))] * 2,
        out_shape=[jax.ShapeDtypeStruct((R, dkv), BF16)] * 2,
        compiler_params=_params(("parallel",)),
    )(k16, v16, pos_a, pos_b, wka, wkb, wva, wvb)


def _bias_from_dist(tab_row, dist):
    idx = jnp.clip(dist, 0, 127)
    w = tab_row.shape[1]
    tab = jnp.broadcast_to(tab_row, (idx.shape[0], w))
    parts = [jnp.take_along_axis(tab, idx[:, k:k + w], axis=1) for k in range(0, idx.shape[1], w)]
    return parts[0] if len(parts) == 1 else jnp.concatenate(parts, axis=1)


def _cmp_select_kernel(q_ref, kc_ref, vc_ref, tab_ref, ovt_ref, *rest, n_qb, scale, n_cast):
    cast_in, (ocmp_ref, selb_ref), cast_out = rest[:n_cast], rest[n_cast:n_cast + 2], rest[n_cast + 2:]
    for src_ref, dst_ref in zip(cast_in, cast_out):
        dst_ref[...] = src_ref[...].astype(BF16)
    c = pl.program_id(0) % n_qb
    tq = q_ref.shape[0]
    n_c = kc_ref.shape[0]
    n_sel = ovt_ref.shape[0]
    t0 = c * tq
    t_col = t0 + lax.broadcasted_iota(jnp.int32, (tq, n_c), 0)
    n_row = lax.broadcasted_iota(jnp.int32, (tq, n_c), 1)
    dist = t_col - (n_row * STRIDE_CMP + (L_CMP - 1))
    valid = dist >= 0

    j_io = lax.broadcasted_iota(jnp.int32, (n_sel, tq), 0)
    t_io = t0 + lax.broadcasted_iota(jnp.int32, (n_sel, tq), 1)
    blk_t = t_io // L_SEL
    forced = (j_io == 0) | (j_io == blk_t) | (j_io == blk_t - 1)
    causal_blk = j_io * L_SEL <= t_io

    for h in range(N_KV_HEADS):
        kc = kc_ref[:, h * HEAD_DIM:(h + 1) * HEAD_DIM]
        vc = vc_ref[:, h * HEAD_DIM:(h + 1) * HEAD_DIM]
        psum = jnp.zeros((tq, n_c), F32)
        for g in range(GQA):
            hd = h * GQA + g
            qh = q_ref[:, hd * HEAD_DIM:(hd + 1) * HEAD_DIM]
            bias = _bias_from_dist(tab_ref[hd:hd + 1, :], dist)
            s = _dot_nt(qh, kc) * scale + bias
            s = jnp.where(valid, s, NEG_BIG)
            m = jnp.max(s, axis=-1, keepdims=True)
            e = jnp.where(valid, jnp.exp(s - m), 0.0)
            d = jnp.sum(e, axis=-1, keepdims=True)
            p = e / jnp.where(d > 0, d, 1.0)
            ocmp_ref[:, hd * HEAD_DIM:(hd + 1) * HEAD_DIM] = _dot(p.astype(BF16), vc)
            psum = psum + p
        p_hi = psum.astype(BF16)
        p_lo = (psum - p_hi.astype(F32)).astype(BF16)
        ovt = ovt_ref[...]
        imp = _dot_nt(ovt, p_hi) + _dot_nt(ovt, p_lo)
        score = jnp.where(forced, FORCED_SCORE, jnp.where(causal_blk, imp, -1.0))
        rank = jnp.zeros((n_sel, tq), F32)
        for jp in range(n_sel):
            row = score[jp:jp + 1, :]
            rank = rank + jnp.where(j_io > jp, jnp.where(row >= score, 1.0, 0.0),
                                    jnp.where(row > score, 1.0, 0.0))
        selb = jnp.where(rank < float(N_SEL), 0.0, SEL_OFF)
        if n_sel < HEAD_DIM:
            selb = jnp.concatenate([selb, jnp.zeros((HEAD_DIM - n_sel, tq), F32)], axis=0)
        selb_ref[:, h * HEAD_DIM:(h + 1) * HEAD_DIM] = selb.T.astype(BF16)


def cmp_select(qkv, kc, vc, tab, ovt, B, S, tq=TQ_CMP, cast_also=()):
    T = qkv.shape[0]
    n_qb = S // tq
    steps = T // tq
    n_c = kc.shape[0] // B
    dq = N_Q_HEADS * HEAD_DIM
    dkv = N_KV_HEADS * HEAD_DIM
    kern = functools.partial(_cmp_select_kernel, n_qb=n_qb, scale=HEAD_DIM ** -0.5, n_cast=len(cast_also))
    cast_specs = [pl.BlockSpec((w.shape[0] // steps, w.shape[1]), lambda i: (i, 0)) for w in cast_also]
    assert all(w.shape[0] % (2 * SUBLANES * steps) == 0 for w in cast_also)
    return pl.pallas_call(
        kern,
        grid=(steps,),
        in_specs=[pl.BlockSpec((tq, dq), lambda i: (i, 0)),
                  pl.BlockSpec((n_c, dkv), lambda i: (i // n_qb, 0)),
                  pl.BlockSpec((n_c, dkv), lambda i: (i // n_qb, 0)),
                  pl.BlockSpec(tab.shape, lambda i: (0, 0)),
                  pl.BlockSpec(ovt.shape, lambda i: (0, 0))] + cast_specs,
        out_specs=[pl.BlockSpec((tq, dq), lambda i: (i, 0)),
                   pl.BlockSpec((tq, dkv), lambda i: (i, 0))] + cast_specs,
        out_shape=[jax.ShapeDtypeStruct((T, dq), F32),
                   jax.ShapeDtypeStruct((T, dkv), BF16)] + [jax.ShapeDtypeStruct(w.shape, BF16) for w in cast_also],
        compiler_params=_params(("parallel",)),
    )(qkv, kc, vc, tab, ovt, *cast_also)


def _repack_kv_kernel(ks_ref, vs_ref, kw_ref, vw_ref, ksp_ref, vst_ref, kwp_ref, vwt_ref):
    first = pl.program_id(1) == 0

    @pl.when(first)
    def _():
        for ref in (ksp_ref, vst_ref, kwp_ref, vwt_ref):
            ref[...] = jnp.zeros(ref.shape, ref.dtype)

    @pl.when(jnp.logical_not(first))
    def _():
        ksp_ref[...] = ks_ref[...]
        kwp_ref[...] = kw_ref[...]
        vst_ref[...] = vs_ref[...].T
        vwt_ref[...] = vw_ref[...].T


def repack_kv(qkv, B, S, first_col_block):
    dkv = N_KV_HEADS * HEAD_DIM
    nt = S // WINDOW
    src = lambda col: pl.BlockSpec((WINDOW, dkv), lambda b, i, col=col: (b * nt + jnp.maximum(i - 1, 0), col))
    rows = pl.BlockSpec((WINDOW, dkv), lambda b, i: (b * (nt + 1) + i, 0))
    cols = pl.BlockSpec((dkv, WINDOW), lambda b, i: (b, i))
    return pl.pallas_call(
        _repack_kv_kernel,
        grid=(B, nt + 1),
        in_specs=[src(first_col_block + k) for k in range(4)],
        out_specs=[rows, cols, rows, cols],
        out_shape=[jax.ShapeDtypeStruct((B * (S + WINDOW), dkv), qkv.dtype),
                   jax.ShapeDtypeStruct((B * dkv, S + WINDOW), qkv.dtype)] * 2,
        compiler_params=_params(("parallel", "arbitrary")),
    )(qkv, qkv, qkv, qkv)


def _flash_update(s_raw, vt, bias, m_ref, l_ref, acc_ref, scale):
    m_old = m_ref[...]
    if bias.shape[0] == 1:
        s = s_raw * scale
        m_new = jnp.maximum(m_old, jnp.max(s, axis=0, keepdims=True) + bias)
        p = jnp.exp2(s - (m_new - bias))
    else:
        s = s_raw * scale + bias
        m_new = jnp.maximum(m_old, jnp.max(s, axis=0, keepdims=True))
        p = jnp.exp2(s - m_new)
    alpha = jnp.exp2(m_old - m_new)
    l_ref[...] = alpha * l_ref[...] + jnp.sum(p, axis=0, keepdims=True)
    acc_ref[...] = alpha * acc_ref[...] + _dot(vt, p.astype(BF16))
    m_ref[...] = m_new


def _flash_chunk(k, vt, q, bias, m_ref, l_ref, acc_ref, scale):
    _flash_update(_dot_nt(k, q), vt, bias, m_ref, l_ref, acc_ref, scale)


def _sel_win_kernel(q_ref, selb_ref, ks_ref, vst_ref, kw_ref, vwt_ref, e2_ref, tab_ref, ocmp_ref,
                    gl_ref, gnw_ref, o_ref, m_ref, l_ref, acc_ref, sa_ref, sb_ref, *, n_qb, scale):
    c = pl.program_id(0) % n_qb
    tq = q_ref.shape[0]
    rows = GQA * tq
    n_back = WINDOW // tq
    near = WINDOW + tq
    j_io = lax.broadcasted_iota(jnp.int32, (tq, tq), 0)
    i_io = lax.broadcasted_iota(jnp.int32, (tq, tq), 1)
    dij = i_io - j_io
    dij4 = jnp.concatenate([dij] * GQA, axis=1)
    causal = dij4 >= 0
    pad_col = jnp.where(lax.broadcasted_iota(jnp.int32, (rows, HEAD_DIM), 1) == HEAD_DIM - 1, 1.0, 0.0).astype(BF16)
    sig = jax.nn.sigmoid(gl_ref[...])
    near0 = pl.multiple_of(c * tq, tq)
    n_far = jnp.maximum(c - n_back, 0)
    n_full = n_far // n_back
    n_rem = n_far - n_full * n_back

    def hsl(h):
        return slice(h * HEAD_DIM, (h + 1) * HEAD_DIM)

    def far0(i):
        return pl.multiple_of(WINDOW + i * WINDOW, WINDOW)

    def reset():
        m_ref[...] = jnp.full(m_ref.shape, NEG_BIG, F32)
        l_ref[...] = jnp.zeros(l_ref.shape, F32)
        acc_ref[...] = jnp.zeros(acc_ref.shape, F32)

    def result():
        ot = acc_ref[...] / l_ref[...]
        return [ot[:, g * tq:(g + 1) * tq].T for g in range(GQA)]

    for h in range(N_KV_HEADS):
        q4 = jnp.concatenate([q_ref[:, (h * GQA + g) * HEAD_DIM:(h * GQA + g + 1) * HEAD_DIM]
                              for g in range(GQA)], axis=0)
        sb4 = jnp.concatenate([selb_ref[:, hsl(h)]] * GQA, axis=0)
        q_aug = jnp.concatenate([q4, sb4 + pad_col], axis=1)
        q_win = jnp.concatenate([q4, pad_col], axis=1)
        tabs = [tab_ref[h * GQA + g:h * GQA + g + 1, :] * LOG2E for g in range(GQA)]
        d0 = jnp.concatenate([_bias_from_dist(t, dij) for t in tabs], axis=1)
        d1 = jnp.concatenate([_bias_from_dist(t, dij + tq) for t in tabs], axis=1)
        far = jnp.concatenate([t[:, HEAD_DIM - 1:HEAD_DIM] + jnp.zeros((1, tq), F32) for t in tabs], axis=1)
        far_t = jnp.broadcast_to(far, (tq, rows))
        diag = jnp.where(causal, d0, NEG_BIG)
        bias_sel = jnp.concatenate([far_t] * (n_back - 1) + [d1, diag], axis=0)
        bias_win = jnp.concatenate([jnp.where(dij4 < 0, far_t, NEG_BIG)] + [far_t] * (n_back - 2) + [d1, diag],
                                   axis=0)

        reset()
        e2_near = e2_ref[pl.ds(near0, near), :]
        k_near = jnp.concatenate([ks_ref[pl.ds(near0, near), hsl(h)], e2_near], axis=1)
        _flash_chunk(k_near, vst_ref[hsl(h), pl.ds(near0, near)], q_aug, bias_sel, m_ref, l_ref, acc_ref, scale)

        def far_scores(i):
            r0 = far0(i)
            k = jnp.concatenate([ks_ref[pl.ds(r0, WINDOW), hsl(h)], e2_ref[pl.ds(r0, WINDOW), :]], axis=1)
            return _dot_nt(k, q_aug)

        def far_update(s_raw, i, bias):
            _flash_update(s_raw, vst_ref[hsl(h), pl.ds(far0(i), WINDOW)], bias, m_ref, l_ref, acc_ref, scale)

        sa_ref[...] = far_scores(0)

        def body(i, carry):
            sb_ref[...] = far_scores(2 * i + 1)
            far_update(sa_ref[...], 2 * i, far)
            sa_ref[...] = far_scores(2 * i + 2)
            far_update(sb_ref[...], 2 * i + 1, far)
            return carry

        lax.fori_loop(0, n_full // 2, body, 0)

        @pl.when(n_full % 2 == 1)
        def _():
            far_update(sa_ref[...], n_full - 1, far)

        @pl.when(n_rem > 0)
        def _():
            if n_back == 2:
                r0 = far0(n_full)
                k = jnp.concatenate([ks_ref[pl.ds(r0, tq), hsl(h)], e2_ref[pl.ds(r0, tq), :]], axis=1)
                _flash_chunk(k, vst_ref[hsl(h), pl.ds(r0, tq)], q_aug, far, m_ref, l_ref, acc_ref, scale)
            else:
                live = lax.broadcasted_iota(jnp.int32, (WINDOW, rows), 0) < n_rem * tq
                far_update(far_scores(n_full), n_full,
                           jnp.where(live, jnp.broadcast_to(far, (WINDOW, rows)), NEG_BIG))

        o_sel = result()

        reset()
        kw_near = jnp.concatenate([kw_ref[pl.ds(near0, near), hsl(h)], e2_near], axis=1)
        _flash_chunk(kw_near, vwt_ref[hsl(h), pl.ds(near0, near)], q_win, bias_win, m_ref, l_ref, acc_ref, scale)
        o_win = result()

        for g in range(GQA):
            hd = h * GQA + g
            o = (sig[:, 3 * hd:3 * hd + 1] * ocmp_ref[:, hsl(hd)]
                 + sig[:, 3 * hd + 1:3 * hd + 2] * o_sel[g]
                 + sig[:, 3 * hd + 2:3 * hd + 3] * o_win[g])
            y = o * lax.rsqrt(jnp.mean(o * o, axis=-1, keepdims=True) + EPS)
            o_ref[:, hsl(hd)] = (y * gnw_ref[:, hsl(hd)]).astype(o_ref.dtype)


def sel_win(qkv, selb, ksp, vstp, kwp, vwtp, e2p, tab, ocmp, gl, gnw, B, S, tq=TQ_SEL):
    T = qkv.shape[0]
    n_qb = S // tq
    dq = N_Q_HEADS * HEAD_DIM
    dkv = N_KV_HEADS * HEAD_DIM
    SP = S + WINDOW
    kern = functools.partial(_sel_win_kernel, n_qb=n_qb, scale=HEAD_DIM ** -0.5 * LOG2E)
    k_spec = pl.BlockSpec((SP, dkv), lambda i: (i // n_qb, 0))
    vt_spec = pl.BlockSpec((dkv, SP), lambda i: (i // n_qb, 0))
    return pl.pallas_call(
        kern,
        grid=(T // tq,),
        in_specs=[pl.BlockSpec((tq, dq), lambda i: (i, 0)),
                  pl.BlockSpec((tq, dkv), lambda i: (i, 0)),
                  k_spec, vt_spec, k_spec, vt_spec,
                  pl.BlockSpec((SP, HEAD_DIM), lambda i: (0, 0)),
                  pl.BlockSpec(tab.shape, lambda i: (0, 0)),
                  pl.BlockSpec((tq, dq), lambda i: (i, 0)),
                  pl.BlockSpec((tq, HEAD_DIM), lambda i: (i, 0)),
                  pl.BlockSpec((1, dq), lambda i: (0, 0))],
        out_specs=pl.BlockSpec((tq, dq), lambda i: (i, 0)),
        out_shape=jax.ShapeDtypeStruct((T, dq), BF16),
        scratch_shapes=[pltpu.VMEM((1, GQA * tq), F32), pltpu.VMEM((1, GQA * tq), F32),
                        pltpu.VMEM((HEAD_DIM, GQA * tq), F32),
                        pltpu.VMEM((WINDOW, GQA * tq), F32), pltpu.VMEM((WINDOW, GQA * tq), F32)],
        compiler_params=_params(("parallel",)),
    )(qkv, selb, ksp, vstp, kwp, vwtp, e2p, tab, ocmp, gl, gnw)


def _conv_kernel(b_ref, c_ref, h_ref, cp_ref, hp_ref, cw_ref, cb_ref, gnw_ref, o_ref, u_ref, *, tiles_per_seq):
    tm = b_ref.shape[0]
    first = (pl.program_id(0) % tiles_per_seq) == 0
    u_prev = cp_ref[...] * hp_ref[...]
    u_ref[0:8, :] = jnp.where(first, 0.0, u_prev)
    u = c_ref[...] * h_ref[...]
    u_ref[8:8 + tm, :] = u
    y = (cw_ref[0:1, :] * u_ref[6:6 + tm, :] + cw_ref[1:2, :] * u_ref[7:7 + tm, :]
         + cw_ref[2:3, :] * u + cb_ref[...])
    o = b_ref[...] * y
    n_groups = o.shape[1] // HEAD_DIM
    for g in range(n_groups):
        sl = slice(g * HEAD_DIM, (g + 1) * HEAD_DIM)
        og = o[:, sl]
        yg = og * lax.rsqrt(jnp.mean(og * og, axis=-1, keepdims=True) + EPS)
        o_ref[:, sl] = (yg * gnw_ref[:, sl]).astype(o_ref.dtype)


def conv_mixer(gc, conv_w, conv_b, gnw, S, tm):
    T = gc.shape[0]
    dc = conv_w.shape[1]
    tps = S // tm
    kern = functools.partial(_conv_kernel, tiles_per_seq=tps)
    prev = lambda col: pl.BlockSpec((8, dc), lambda i, col=col: (jnp.maximum(i * (tm // 8) - 1, 0), col))
    cur = lambda col: pl.BlockSpec((tm, dc), lambda i, col=col: (i, col))
    return pl.pallas_call(
        kern,
        grid=(T // tm,),
        in_specs=[cur(0), cur(1), cur(2), prev(1), prev(2),
                  pl.BlockSpec((8, dc), lambda i: (0, 0)),
                  pl.BlockSpec((1, dc), lambda i: (0, 0)),
                  pl.BlockSpec((1, dc), lambda i: (0, 0))],
        out_specs=pl.BlockSpec((tm, dc), lambda i: (i, 0)),
        out_shape=jax.ShapeDtypeStruct((T, dc), BF16),
        scratch_shapes=[pltpu.VMEM((tm + 8, dc), F32)],
        compiler_params=_params(("parallel",)),
    )(gc, gc, gc, gc, gc, conv_w, conv_b, gnw)


def _out_proj_kernel(ma_ref, mc_ref, wa_ref, wc_ref, x_ref, o_ref):
    o_ref[...] = x_ref[...] + _dot(ma_ref[...], wa_ref[...]) + _dot(mc_ref[...], wc_ref[...])


def out_proj(ma, mc, w, x, tm, tn):
    T, da = ma.shape
    dc = mc.shape[1]
    D = x.shape[1]
    assert da == dc and w.shape[0] == da + dc
    return pl.pallas_call(
        _out_proj_kernel,
        grid=(T // tm, D // tn),
        in_specs=[pl.BlockSpec((tm, da), lambda i, j: (i, 0)),
                  pl.BlockSpec((tm, dc), lambda i, j: (i, 0)),
                  pl.BlockSpec((da, tn), lambda i, j: (0, j)),
                  pl.BlockSpec((dc, tn), lambda i, j: (1, j)),
                  pl.BlockSpec((tm, tn), lambda i, j: (i, j))],
        out_specs=pl.BlockSpec((tm, tn), lambda i, j: (i, j)),
        out_shape=jax.ShapeDtypeStruct((T, D), F32),
        compiler_params=_params(("parallel", "arbitrary")),
    )(ma, mc, w, w, x)


def _sort_network(n):
    pairs = []
    p = 1
    while p < n:
        k = p
        while k >= 1:
            for j in range(k % p, n - k, 2 * k):
                for i in range(min(k, n - j - k)):
                    if (i + j) // (2 * p) == (i + j + k) // (2 * p):
                        pairs.append((i + j, i + j + k))
            k //= 2
        p *= 2
    return pairs


def _peer_route_kernel(q_ref, sk_ref, u_ref, v_ref, e_ref, g_ref, ub_ref, vb_ref, sv_ref, si_ref):
    ub_ref[...] = u_ref[...].astype(BF16)
    vb_ref[...] = v_ref[...].astype(BF16)
    tt = q_ref.shape[0]
    K = PEER_TOPK
    sub = SUBLANES
    n_io = lax.broadcasted_iota(jnp.int32, (N_KEYS, tt), 0).astype(F32)
    r8 = lax.broadcasted_iota(jnp.int32, (sub, tt), 0).astype(F32)
    ninf = -jnp.inf
    lens = [K // (a + 1) for a in range(N_FIXED_A)]
    lens += [max(K // (b + 1) - N_FIXED_A, 0) for b in range(sub - N_FIXED_A)]
    assert sum(lens) == sum(K // (a + 1) for a in range(K)) and lens[-1] == 0
    fixed_a = r8 < float(N_FIXED_A)
    list_len = jnp.zeros((sub, tt), F32)
    for row, n in enumerate(lens):
        list_len = jnp.where(r8 == float(row), float(n), list_len)

    for h in range(PEER_HEADS):
        for c in range(2):
            col = (h * 2 + c) * HEAD_DIM
            s = _dot_nt(sk_ref[h * 2 + c], q_ref[:, col:col + HEAD_DIM])
            n_col = N_KEYS // sub
            vals = [s[j * sub:(j + 1) * sub, :] for j in range(n_col)]
            idxs = [n_io[j * sub:(j + 1) * sub, :] for j in range(n_col)]
            for lo, hi in _sort_network(n_col):
                swap = (vals[hi] > vals[lo]) | ((vals[hi] == vals[lo]) & (idxs[hi] < idxs[lo]))
                vals[lo], vals[hi] = jnp.where(swap, vals[hi], vals[lo]), jnp.where(swap, vals[lo], vals[hi])
                idxs[lo], idxs[hi] = jnp.where(swap, idxs[hi], idxs[lo]), jnp.where(swap, idxs[lo], idxs[hi])
            for k in range(K):
                m = jnp.max(vals[0], axis=0, keepdims=True)
                idx = jnp.min(jnp.where(vals[0] == m, idxs[0], float(N_KEYS)), axis=0, keepdims=True)
                sv_ref[c, k:k + 1, :] = m
                si_ref[c, k:k + 1, :] = idx
                win = idxs[0] == idx
                for j in range(min(n_col, K) - 1 - k):
                    vals[j] = jnp.where(win, vals[j + 1], vals[j])
                    idxs[j] = jnp.where(win, idxs[j + 1], idxs[j])
        sv0, sv1 = sv_ref[0], sv_ref[1]
        si0, si1 = si_ref[0], si_ref[1]
        sv1_low = pltpu.roll(sv1[0:sub, :], N_FIXED_A, 0)
        sv0_top = sv0[0:sub, :]
        lv, lf = [], []
        for dep in range(K):
            a_dep = min(N_FIXED_A + dep, K - 1)
            val = (jnp.where(fixed_a, sv0_top, sv0[a_dep:a_dep + 1, :])
                   + jnp.where(fixed_a, sv1[dep:dep + 1, :], sv1_low))
            lv.append(jnp.where(list_len > dep, val, ninf))
            lf.append(jnp.where(fixed_a, r8 * float(K) + dep, (N_FIXED_A + dep) * float(K) + r8 - N_FIXED_A))
        cvs, fls = [], []
        for k in range(K):
            m = jnp.max(lv[0], axis=0, keepdims=True)
            fsel = jnp.min(jnp.where(lv[0] == m, lf[0], 1e9), axis=0, keepdims=True)
            win = lf[0] == fsel
            cvs.append(m)
            fls.append(fsel)
            for dep in range(K - 1 - k):
                lv[dep] = jnp.where(win, lv[dep + 1], lv[dep])
                lf[dep] = jnp.where(win, lf[dep + 1], lf[dep])
        cv = jnp.concatenate(cvs, axis=0)
        fl = jnp.concatenate(fls, axis=0)
        a_sel = jnp.floor(fl * (1.0 / K))
        b_sel = fl - a_sel * K
        i1 = jnp.zeros_like(fl)
        i2 = jnp.zeros_like(fl)
        for r in range(K):
            i1 = jnp.where(a_sel == r, si0[r:r + 1, :], i1)
            i2 = jnp.where(b_sel == r, si1[r:r + 1, :], i2)
        ex = i1 * float(N_KEYS) + i2
        ev = jnp.exp(cv - jnp.max(cv, axis=0, keepdims=True))
        gates = ev / jnp.sum(ev, axis=0, keepdims=True)
        e_ref[0, h * K:(h + 1) * K, :] = ex.astype(jnp.int32)
        g_ref[0, h * K:(h + 1) * K, :] = gates


def peer_route(qp, subkeys, u, v):
    T = qp.shape[0]
    tt = TT_ROUTE
    P = PEER_HEADS * PEER_TOPK
    nt = T // tt
    E, D = u.shape
    slab = E // nt
    assert slab * nt == E and slab % 16 == 0
    tab_spec = pl.BlockSpec((slab, D), lambda i: (i, 0))
    return pl.pallas_call(
        _peer_route_kernel,
        grid=(nt,),
        in_specs=[pl.BlockSpec((tt, qp.shape[1]), lambda i: (i, 0)),
                  pl.BlockSpec(subkeys.shape, lambda i: (0, 0, 0)),
                  tab_spec, tab_spec],
        out_specs=[pl.BlockSpec((1, P, tt), lambda i: (i, 0, 0))] * 2 + [tab_spec, tab_spec],
        out_shape=[jax.ShapeDtypeStruct((nt, P, tt), jnp.int32),
                   jax.ShapeDtypeStruct((nt, P, tt), F32),
                   jax.ShapeDtypeStruct((E, D), BF16), jax.ShapeDtypeStruct((E, D), BF16)],
        scratch_shapes=[pltpu.VMEM((2, PEER_TOPK, tt), F32), pltpu.VMEM((2, PEER_TOPK, tt), F32)],
        compiler_params=_params(("parallel",)),
    )(qp, subkeys, u, v)


def _peer_gbuild_kernel(e_ref, g_ref, o_ref, i1_ref, i2_ref, gt_ref):
    tt = e_ref.shape[2]
    e = e_ref[0].T
    i1_ref[...] = e >> 7
    i2_ref[...] = e & (N_KEYS - 1)
    gt_ref[...] = g_ref[0].T
    P = e.shape[1]
    k_io = lax.broadcasted_iota(jnp.int32, (N_KEYS, P), 0)
    group = tt

    def body(tg, carry):
        base = pl.multiple_of(tg * group, group)
        r1s = i1_ref[pl.ds(base, group), :]
        r2s = i2_ref[pl.ds(base, group), :]
        rgs = gt_ref[pl.ds(base, group), :]
        for u in range(group):
            lhs = jnp.where(k_io == r1s[u:u + 1, :], rgs[u:u + 1, :], 0.0).astype(BF16)
            rhs = jnp.where(k_io == r2s[u:u + 1, :], 1.0, 0.0).astype(BF16)
            g_t = _dot_nt(lhs, rhs).astype(BF16).astype(F32)
            lo = pltpu.bitcast(g_t[:HALF_KEYS], jnp.uint32) >> 16
            hi = pltpu.bitcast(g_t[HALF_KEYS:], jnp.uint32)
            words = hi | lo
            for a in range(HALF_KEYS // PACK_ROWS):
                o_ref[a, base + u] = words[a * PACK_ROWS:(a + 1) * PACK_ROWS, :]
        return carry

    lax.fori_loop(0, tt // group, body, 0)


def peer_gbuild(ex, gates):
    nt, P, tt = ex.shape
    T = nt * tt
    n_a = HALF_KEYS // PACK_ROWS
    return pl.pallas_call(
        _peer_gbuild_kernel,
        grid=(nt,),
        in_specs=[pl.BlockSpec((1, P, tt), lambda i: (i, 0, 0))] * 2,
        out_specs=pl.BlockSpec((n_a, tt, PACK_ROWS, N_KEYS), lambda i: (0, i, 0, 0)),
        out_shape=jax.ShapeDtypeStruct((n_a, T, PACK_ROWS, N_KEYS), jnp.uint32),
        scratch_shapes=[pltpu.VMEM((tt, P), jnp.int32), pltpu.VMEM((tt, P), jnp.int32),
                        pltpu.VMEM((tt, P), F32)],
        compiler_params=_params(("parallel",)),
    )(ex, gates)


def _peer_dense_kernel(h_ref, nw_ref, fw_ref, u_ref, v_ref, gm_ref, o_ref, hn_ref, acc_ref):
    j = pl.program_id(1)
    tt = h_ref.shape[0]

    @pl.when(j == 0)
    def _():
        x = h_ref[...]
        y = x * lax.rsqrt(jnp.mean(x * x, axis=-1, keepdims=True) + EPS)
        hn_ref[...] = (y * nw_ref[...]).astype(BF16)
        acc_ref[...] = jnp.zeros(acc_ref.shape, F32)

    a = _dot_nt(hn_ref[...], u_ref[...])
    shift = (16 * (1 - j % 2)).astype(jnp.uint32)
    gm = jnp.concatenate(
        [pltpu.bitcast((gm_ref[pl.ds(k, tt, stride=PACK_ROWS), :] << shift) & jnp.uint32(0xFFFF0000), F32)
         for k in range(PACK_ROWS)], axis=1)
    w = (gm * jax.nn.gelu(a)).astype(BF16)
    acc_ref[...] += _dot(w, v_ref[...])

    @pl.when(j == pl.num_programs(1) - 1)
    def _():
        x = h_ref[...] + acc_ref[...]
        y = x * lax.rsqrt(jnp.mean(x * x, axis=-1, keepdims=True) + EPS)
        o_ref[...] = y * fw_ref[...]


def peer_dense(h, ffn_nw, final_nw, u, v, gm_words, tt):
    T, D = h.shape
    E = u.shape[0]
    ec = PACK_ROWS * N_KEYS
    n_a = gm_words.shape[0]
    assert E == 2 * n_a * ec
    gm2 = gm_words.reshape(n_a, T * PACK_ROWS, N_KEYS)
    tab_spec = pl.BlockSpec((ec, D), lambda i, j: ((j % 2) * n_a + j // 2, 0))
    return pl.pallas_call(
        _peer_dense_kernel,
        grid=(T // tt, E // ec),
        in_specs=[pl.BlockSpec((tt, D), lambda i, j: (i, 0)),
                  pl.BlockSpec((1, D), lambda i, j: (0, 0)),
                  pl.BlockSpec((1, D), lambda i, j: (0, 0)),
                  tab_spec, tab_spec,
                  pl.BlockSpec((None, tt * PACK_ROWS, N_KEYS), lambda i, j: (j // 2, i, 0))],
        out_specs=pl.BlockSpec((tt, D), lambda i, j: (i, 0)),
        out_shape=jax.ShapeDtypeStruct((T, D), F32),
        scratch_shapes=[pltpu.VMEM((tt, D), BF16), pltpu.VMEM((tt, D), F32)],
        compiler_params=_params(("parallel", "arbitrary")),
    )(h, ffn_nw.reshape(1, D), final_nw.reshape(1, D), u, v, gm2)


def _t5_bucket_np(n_dist):
    d = np.arange(n_dist)
    max_exact = NUM_BUCKETS // 2
    nf = np.maximum(d, 1).astype(np.float64)
    large = max_exact + (np.log(nf / max_exact) / math.log(MAX_DISTANCE / max_exact)
                         * (NUM_BUCKETS - max_exact)).astype(np.int64)
    large = np.minimum(large, NUM_BUCKETS - 1)
    return np.where(d < max_exact, d, large).astype(np.int32)


def _overlap_t_np(S):
    n_c = (S - L_CMP) // STRIDE_CMP + 1
    n_sel = S // L_SEL
    pos = np.arange(n_c)[:, None] * STRIDE_CMP + np.arange(L_CMP)[None, :]
    m = np.zeros((n_c + 1, n_sel), np.float32)
    np.add.at(m, (np.repeat(np.arange(n_c), L_CMP), (pos // L_SEL).reshape(-1)), 1.0 / L_CMP)
    return np.ascontiguousarray(m.T)


def _block_onehot_padded_np(S):
    assert S // L_SEL < HEAD_DIM
    e2 = np.zeros((WINDOW + S, HEAD_DIM), np.float32)
    e2[WINDOW + np.arange(S), np.arange(S) // L_SEL] = 1.0
    e2[:WINDOW, HEAD_DIM - 1] = SEL_OFF
    return e2


def nsa_conv_mix(xt, B, S, attn_norm_w, w_in, w_cmp_k, w_cmp_v, cmp_pos, conv_w, conv_b,
                 attn_gnw, conv_gnw, rel_bias, tm=TM_PROJ, tm_conv=TM_CONV, cast_also=()):
    T, D = xt.shape
    dq = N_Q_HEADS * HEAD_DIM
    dkv = N_KV_HEADS * HEAD_DIM
    n_attn = dq + 6 * dkv
    n_gate = 3 * N_Q_HEADS
    dc = (w_in.shape[1] - n_attn - n_gate) // 3
    w_t = w_in.T
    w_attn = cast_rows(w_t, 0, n_attn)
    w_conv = cast_rows(w_t, n_attn + n_gate, 3 * dc)
    w_gate = jnp.pad(w_t[n_attn:n_attn + n_gate], ((0, HEAD_DIM - n_gate), (0, 0))).astype(BF16)

    qkv = norm_matmul(xt, attn_norm_w, w_attn, BF16, tm, n_attn // 2, w_is_nk=True)
    gc, gl = norm_matmul(xt, attn_norm_w, w_conv, F32, tm, dc, w_is_nk=True, w_side=w_gate)

    n16 = S // STRIDE_CMP
    k16 = qkv[:, dq:dq + dkv].reshape(B * n16, STRIDE_CMP * dkv)
    v16 = qkv[:, dq + dkv:dq + 2 * dkv].reshape(B * n16, STRIDE_CMP * dkv)

    def wflat(w, lo):
        return w[lo:lo + STRIDE_CMP].reshape(STRIDE_CMP * HEAD_DIM, HEAD_DIM).astype(BF16)

    def posrow(lo):
        return cmp_pos[lo:lo + STRIDE_CMP].reshape(1, STRIDE_CMP * HEAD_DIM)

    kc, vc = compress(k16, v16, posrow(0), posrow(STRIDE_CMP),
                      wflat(w_cmp_k, 0), wflat(w_cmp_k, STRIDE_CMP),
                      wflat(w_cmp_v, 0), wflat(w_cmp_v, STRIDE_CMP), B)

    tab = rel_bias[_t5_bucket_np(HEAD_DIM)].T
    ovt = jnp.asarray(_overlap_t_np(S), BF16)
    e2p = jnp.asarray(_block_onehot_padded_np(S), BF16)

    ocmp, selb, *casts = cmp_select(qkv, kc, vc, tab, ovt, B, S, cast_also=cast_also)
    ksp, vstp, kwp, vwtp = repack_kv(qkv, B, S, first_col_block=dq // dkv + 2)
    attn = sel_win(qkv, selb, ksp, vstp, kwp, vwtp, e2p, tab, ocmp, gl, attn_gnw.reshape(1, dq), B, S)
    cw8 = jnp.pad(conv_w, ((0, 8 - conv_w.shape[0]), (0, 0)))
    conv = conv_mixer(gc, cw8, conv_b.reshape(1, dc), conv_gnw.reshape(1, dc), S, tm_conv)
    return (attn, conv, *casts)


def peer_block(h, ffn_nw, final_nw, peer_wq, peer_subkeys, peer_u, peer_v, tm=TM_PROJ, tt=TT_DENSE):
    T, D = h.shape
    qp = norm_matmul(h, ffn_nw, peer_wq.astype(BF16), BF16, tm, TN_OUT)
    sk = peer_subkeys.reshape(PEER_HEADS * 2, N_KEYS, peer_subkeys.shape[-1]).astype(BF16)
    ex, gates, u_bf, v_bf = peer_route(qp, sk, peer_u, peer_v)
    gm = peer_gbuild(ex, gates)
    return peer_dense(h, ffn_nw, final_nw, u_bf, v_bf, gm, tt)


def kernel(x, attn_norm_w, w_in, w_cmp_k, w_cmp_v, cmp_pos, conv_w, conv_b, attn_group_norm_w,
           conv_group_norm_w, w_out, rel_bias, ffn_norm_w, peer_wq, peer_subkeys, peer_u, peer_v,
           final_norm_w):
    B, S, D = x.shape
    T = B * S
    xt = x.reshape(T, D)
    attn, conv, wo, wq = nsa_conv_mix(xt, B, S, attn_norm_w[0], w_in[0], w_cmp_k[0], w_cmp_v[0], cmp_pos[0],
                                      conv_w[0], conv_b[0], attn_group_norm_w[0], conv_group_norm_w[0], rel_bias,
                                      cast_also=(w_out[0], peer_wq[0]))
    h = out_proj(attn, conv, wo, xt, TM_PROJ, TN_OUT)
    out = peer_block(h, ffn_norm_w[0], final_norm_w, wq, peer_subkeys[0], peer_u[0], peer_v[0])
    return out.reshape(B, S, D)
```

```python
import functools
import math

import jax
import jax.numpy as jnp
import numpy as np
from jax import lax
from jax.experimental import pallas as pl
from jax.experimental.pallas import tpu as pltpu

F32 = jnp.float32
BF16 = jnp.bfloat16

HEAD_DIM = 128
N_KV_HEADS = 2
GQA = 4
N_Q_HEADS = N_KV_HEADS * GQA
L_CMP = 32
STRIDE_CMP = 16
L_SEL = 64
N_SEL = 16
WINDOW = 512
Q_BLOCK = 128
FORCED_SCORE = float(GQA + 1)
NUM_BUCKETS = 32
MAX_DISTANCE = 128
PEER_HEADS = 8
N_KEYS = 128
PEER_TOPK = 16
EPS = 1e-6
NEG_BIG = -1e30
SEL_OFF = -float(2 ** 30)
LOG2E = math.log2(math.e)
N_FIXED_A = 4

SUBLANES = 8
LANES = 128
MXU_DIM = 256
V7X_VMEM_BYTES = 64 * 1024 * 1024
VMEM_LIMIT = V7X_VMEM_BYTES * 7 // 8

HALF_KEYS = N_KEYS // 2
PACK_ROWS = SUBLANES

TM_PROJ = 4 * MXU_DIM
TM_OUT = 2 * MXU_DIM
TM_CONV = 2 * MXU_DIM
TQ_CMP = 4 * Q_BLOCK
TQ_SEL = 2 * Q_BLOCK
TT_ROUTE = LANES
TT_DENSE = 2 * MXU_DIM
CAST_ROWS = MXU_DIM


def _dot(a, b):
    return jnp.dot(a, b, preferred_element_type=F32)


def _dot_nt(a, b):
    return lax.dot_general(a, b, (((1,), (1,)), ((), ())), preferred_element_type=F32)


def _params(sem, vmem=VMEM_LIMIT):
    return pltpu.CompilerParams(dimension_semantics=sem, vmem_limit_bytes=vmem)


def _norm_matmul_kernel(x_ref, nw_ref, w_ref, *rest, w_is_nk, has_side):
    if has_side:
        ws_ref, o_ref, os_ref, xn_ref = rest
    else:
        o_ref, xn_ref = rest

    @pl.when(pl.program_id(1) == 0)
    def _():
        x = x_ref[...]
        y = x * lax.rsqrt(jnp.mean(x * x, axis=-1, keepdims=True) + EPS)
        xn_ref[...] = (y * nw_ref[...]).astype(BF16)
        if has_side:
            os_ref[...] = _dot_nt(xn_ref[...], ws_ref[...])

    dot = _dot_nt if w_is_nk else _dot
    o_ref[...] = dot(xn_ref[...], w_ref[...]).astype(o_ref.dtype)


def norm_matmul(x, norm_w, w, out_dtype, tm, tn, w_is_nk=False, w_side=None):
    T, D = x.shape
    N = w.shape[0] if w_is_nk else w.shape[1]
    w_spec = pl.BlockSpec((tn, D), lambda i, j: (j, 0)) if w_is_nk else pl.BlockSpec((D, tn), lambda i, j: (0, j))
    in_specs = [pl.BlockSpec((tm, D), lambda i, j: (i, 0)), pl.BlockSpec((1, D), lambda i, j: (0, 0)), w_spec]
    out_specs = [pl.BlockSpec((tm, tn), lambda i, j: (i, j))]
    out_shape = [jax.ShapeDtypeStruct((T, N), out_dtype)]
    args = [x, norm_w.reshape(1, D), w]
    if w_side is not None:
        n_side = w_side.shape[0]
        in_specs.append(pl.BlockSpec((n_side, D), lambda i, j: (0, 0)))
        out_specs.append(pl.BlockSpec((tm, n_side), lambda i, j: (i, 0)))
        out_shape.append(jax.ShapeDtypeStruct((T, n_side), F32))
        args.append(w_side)
    out = pl.pallas_call(
        functools.partial(_norm_matmul_kernel, w_is_nk=w_is_nk, has_side=w_side is not None),
        grid=(T // tm, N // tn),
        in_specs=in_specs,
        out_specs=out_specs,
        out_shape=out_shape,
        scratch_shapes=[pltpu.VMEM((tm, D), BF16)],
        compiler_params=_params(("parallel", "arbitrary")),
    )(*args)
    return out if w_side is not None else out[0]


def _cast_rows_kernel(w_ref, o_ref):
    o_ref[...] = w_ref[...].astype(o_ref.dtype)


def cast_rows(w, row0, n_rows, tr=CAST_ROWS):
    K = w.shape[1]
    assert row0 % SUBLANES == 0 and n_rows % tr == 0
    return pl.pallas_call(
        _cast_rows_kernel,
        grid=(n_rows // tr,),
        in_specs=[pl.BlockSpec((pl.Element(tr), pl.Element(K)), lambda i: (pl.multiple_of(row0 + i * tr, SUBLANES), 0))],
        out_specs=pl.BlockSpec((tr, K), lambda i: (i, 0)),
        out_shape=jax.ShapeDtypeStruct((n_rows, K), BF16),
        compiler_params=_params(("parallel",)),
    )(w)


def _compress_kernel(k_ref, v_ref, pa_ref, pb_ref, wka_ref, wkb_ref, wva_ref, wvb_ref, kc_ref, vc_ref):
    dkv = kc_ref.shape[1]

    def one(x_ref, wa_ref, wb_ref, o_ref):
        for h in range(N_KV_HEADS):
            x = jnp.concatenate([x_ref[:, l * dkv + h * HEAD_DIM:l * dkv + (h + 1) * HEAD_DIM]
                                 for l in range(STRIDE_CMP)], axis=1).astype(F32)
            a = _dot((x + pa_ref[...]).astype(BF16), wa_ref[...])
            b = _dot((x + pb_ref[...]).astype(BF16), wb_ref[...])
            n = b.shape[0]
            o_ref[:, h * HEAD_DIM:(h + 1) * HEAD_DIM] = (a + pltpu.roll(b, n - 1, 0)).astype(o_ref.dtype)

    one(k_ref, wka_ref, wkb_ref, kc_ref)
    one(v_ref, wva_ref, wvb_ref, vc_ref)


def compress(k16, v16, pos_a, pos_b, wka, wkb, wva, wvb, B):
    R, C = k16.shape
    nb = R // B
    dkv = C // STRIDE_CMP
    full = lambda shp: pl.BlockSpec(shp, lambda b: (0, 0))
    return pl.pallas_call(
        _compress_kernel,
        grid=(B,),
        in_specs=[pl.BlockSpec((nb, C), lambda b: (b, 0)),
                  pl.BlockSpec((nb, C), lambda b: (b, 0)),
                  full(pos_a.shape), full(pos_b.shape),
                  full(wka.shape), full(wkb.shape), full(wva.shape), full(wvb.shape)],
        out_specs=[pl.BlockSpec((nb, dkv), lambda b: (b, 0))] * 2,
        out_shape=[jax.ShapeDtypeStruct((R, dkv), BF16)] * 2,
        compiler_params=_params(("parallel",)),
    )(k16, v16, pos_a, pos_b, wka, wkb, wva, wvb)


def _bias_from_dist(tab_row, dist):
    idx = jnp.clip(dist, 0, 127)
    w = tab_row.shape[1]
    tab = jnp.broadcast_to(tab_row, (idx.shape[0], w))
    parts = [jnp.take_along_axis(tab, idx[:, k:k + w], axis=1) for k in range(0, idx.shape[1], w)]
    return parts[0] if len(parts) == 1 else jnp.concatenate(parts, axis=1)


def _cmp_select_kernel(q_ref, kc_ref, vc_ref, tab_ref, ovt_ref, *rest, n_qb, scale, n_cast):
    cast_in, (ocmp_ref, selb_ref), cast_out = rest[:n_cast], rest[n_cast:n_cast + 2], rest[n_cast + 2:]
    for src_ref, dst_ref in zip(cast_in, cast_out):
        dst_ref[...] = src_ref[...].astype(BF16)
    c = pl.program_id(0) % n_qb
    tq = q_ref.shape[0]
    n_c = kc_ref.shape[0]
    n_sel = ovt_ref.shape[0]
    t0 = c * tq
    t_col = t0 + lax.broadcasted_iota(jnp.int32, (tq, n_c), 0)
    n_row = lax.broadcasted_iota(jnp.int32, (tq, n_c), 1)
    dist = t_col - (n_row * STRIDE_CMP + (L_CMP - 1))
    valid = dist >= 0

    j_io = lax.broadcasted_iota(jnp.int32, (n_sel, tq), 0)
    t_io = t0 + lax.broadcasted_iota(jnp.int32, (n_sel, tq), 1)
    blk_t = t_io // L_SEL
    forced = (j_io == 0) | (j_io == blk_t) | (j_io == blk_t - 1)
    causal_blk = j_io * L_SEL <= t_io

    for h in range(N_KV_HEADS):
        kc = kc_ref[:, h * HEAD_DIM:(h + 1) * HEAD_DIM]
        vc = vc_ref[:, h * HEAD_DIM:(h + 1) * HEAD_DIM]
        psum = jnp.zeros((tq, n_c), F32)
        for g in range(GQA):
            hd = h * GQA + g
            qh = q_ref[:, hd * HEAD_DIM:(hd + 1) * HEAD_DIM]
            bias = _bias_from_dist(tab_ref[hd:hd + 1, :], dist)
            s = _dot_nt(qh, kc) * scale + bias
            s = jnp.where(valid, s, NEG_BIG)
            m = jnp.max(s, axis=-1, keepdims=True)
            e = jnp.where(valid, jnp.exp(s - m), 0.0)
            d = jnp.sum(e, axis=-1, keepdims=True)
            p = e / jnp.where(d > 0, d, 1.0)
            ocmp_ref[:, hd * HEAD_DIM:(hd + 1) * HEAD_DIM] = _dot(p.astype(BF16), vc)
            psum = psum + p
        p_hi = psum.astype(BF16)
        p_lo = (psum - p_hi.astype(F32)).astype(BF16)
        ovt = ovt_ref[...]
        imp = _dot_nt(ovt, p_hi) + _dot_nt(ovt, p_lo)
        score = jnp.where(forced, FORCED_SCORE, jnp.where(causal_blk, imp, -1.0))
        rank = jnp.zeros((n_sel, tq), F32)
        for jp in range(n_sel):
            row = score[jp:jp + 1, :]
            rank = rank + jnp.where(j_io > jp, jnp.where(row >= score, 1.0, 0.0),
                                    jnp.where(row > score, 1.0, 0.0))
        selb = jnp.where(rank < float(N_SEL), 0.0, SEL_OFF)
        if n_sel < HEAD_DIM:
            selb = jnp.concatenate([selb, jnp.zeros((HEAD_DIM - n_sel, tq), F32)], axis=0)
        selb_ref[:, h * HEAD_DIM:(h + 1) * HEAD_DIM] = selb.T.astype(BF16)


def cmp_select(qkv, kc, vc, tab, ovt, B, S, tq=TQ_CMP, cast_also=()):
    T = qkv.shape[0]
    n_qb = S // tq
    steps = T // tq
    n_c = kc.shape[0] // B
    dq = N_Q_HEADS * HEAD_DIM
    dkv = N_KV_HEADS * HEAD_DIM
    kern = functools.partial(_cmp_select_kernel, n_qb=n_qb, scale=HEAD_DIM ** -0.5, n_cast=len(cast_also))
    cast_specs = [pl.BlockSpec((w.shape[0] // steps, w.shape[1]), lambda i: (i, 0)) for w in cast_also]
    assert all(w.shape[0] % (2 * SUBLANES * steps) == 0 for w in cast_also)
    return pl.pallas_call(
        kern,
        grid=(steps,),
        in_specs=[pl.BlockSpec((tq, dq), lambda i: (i, 0)),
                  pl.BlockSpec((n_c, dkv), lambda i: (i // n_qb, 0)),
                  pl.BlockSpec((n_c, dkv), lambda i: (i // n_qb, 0)),
                  pl.BlockSpec(tab.shape, lambda i: (0, 0)),
                  pl.BlockSpec(ovt.shape, lambda i: (0, 0))] + cast_specs,
        out_specs=[pl.BlockSpec((tq, dq), lambda i: (i, 0)),
                   pl.BlockSpec((tq, dkv), lambda i: (i, 0))] + cast_specs,
        out_shape=[jax.ShapeDtypeStruct((T, dq), F32),
                   jax.ShapeDtypeStruct((T, dkv), BF16)] + [jax.ShapeDtypeStruct(w.shape, BF16) for w in cast_also],
        compiler_params=_params(("parallel",)),
    )(qkv, kc, vc, tab, ovt, *cast_also)


def _repack_kv_kernel(ks_ref, vs_ref, kw_ref, vw_ref, ksp_ref, vst_ref, kwp_ref, vwt_ref):
    first = pl.program_id(1) == 0

    @pl.when(first)
    def _():
        for ref in (ksp_ref, vst_ref, kwp_ref, vwt_ref):
            ref[...] = jnp.zeros(ref.shape, ref.dtype)

    @pl.when(jnp.logical_not(first))
    def _():
        ksp_ref[...] = ks_ref[...]
        kwp_ref[...] = kw_ref[...]
        vst_ref[...] = vs_ref[...].T
        vwt_ref[...] = vw_ref[...].T


def repack_kv(qkv, B, S, first_col_block):
    dkv = N_KV_HEADS * HEAD_DIM
    nt = S // WINDOW
    src = lambda col: pl.BlockSpec((WINDOW, dkv), lambda b, i, col=col: (b * nt + jnp.maximum(i - 1, 0), col))
    rows = pl.BlockSpec((WINDOW, dkv), lambda b, i: (b * (nt + 1) + i, 0))
    cols = pl.BlockSpec((dkv, WINDOW), lambda b, i: (b, i))
    return pl.pallas_call(
        _repack_kv_kernel,
        grid=(B, nt + 1),
        in_specs=[src(first_col_block + k) for k in range(4)],
        out_specs=[rows, cols, rows, cols],
        out_shape=[jax.ShapeDtypeStruct((B * (S + WINDOW), dkv), qkv.dtype),
                   jax.ShapeDtypeStruct((B * dkv, S + WINDOW), qkv.dtype)] * 2,
        compiler_params=_params(("parallel", "arbitrary")),
    )(qkv, qkv, qkv, qkv)


def _flash_update(s_raw, vt, bias, m_ref, l_ref, acc_ref, scale):
    m_old = m_ref[...]
    if bias.shape[0] == 1:
        s = s_raw * scale
        m_new = jnp.maximum(m_old, jnp.max(s, axis=0, keepdims=True) + bias)
        p = jnp.exp2(s - (m_new - bias))
    else:
        s = s_raw * scale + bias
        m_new = jnp.maximum(m_old, jnp.max(s, axis=0, keepdims=True))
        p = jnp.exp2(s - m_new)
    alpha = jnp.exp2(m_old - m_new)
    l_ref[...] = alpha * l_ref[...] + jnp.sum(p, axis=0, keepdims=True)
    acc_ref[...] = alpha * acc_ref[...] + _dot(vt, p.astype(BF16))
    m_ref[...] = m_new


def _flash_chunk(k, vt, q, bias, m_ref, l_ref, acc_ref, scale):
    _flash_update(_dot_nt(k, q), vt, bias, m_ref, l_ref, acc_ref, scale)


def _sel_win_kernel(q_ref, selb_ref, ks_ref, vst_ref, kw_ref, vwt_ref, e2_ref, tab_ref, ocmp_ref,
                    gl_ref, gnw_ref, o_ref, m_ref, l_ref, acc_ref, sa_ref, sb_ref, *, n_qb, scale):
    c = pl.program_id(0) % n_qb
    tq = q_ref.shape[0]
    rows = GQA * tq
    n_back = WINDOW // tq
    near = WINDOW + tq
    j_io = lax.broadcasted_iota(jnp.int32, (tq, tq), 0)
    i_io = lax.broadcasted_iota(jnp.int32, (tq, tq), 1)
    dij = i_io - j_io
    dij4 = jnp.concatenate([dij] * GQA, axis=1)
    causal = dij4 >= 0
    pad_col = jnp.where(lax.broadcasted_iota(jnp.int32, (rows, HEAD_DIM), 1) == HEAD_DIM - 1, 1.0, 0.0).astype(BF16)
    sig = jax.nn.sigmoid(gl_ref[...])
    near0 = pl.multiple_of(c * tq, tq)
    n_far = jnp.maximum(c - n_back, 0)
    n_full = n_far // n_back
    n_rem = n_far - n_full * n_back

    def hsl(h):
        return slice(h * HEAD_DIM, (h + 1) * HEAD_DIM)

    def far0(i):
        return pl.multiple_of(WINDOW + i * WINDOW, WINDOW)

    def reset():
        m_ref[...] = jnp.full(m_ref.shape, NEG_BIG, F32)
        l_ref[...] = jnp.zeros(l_ref.shape, F32)
        acc_ref[...] = jnp.zeros(acc_ref.shape, F32)

    def result():
        ot = acc_ref[...] / l_ref[...]
        return [ot[:, g * tq:(g + 1) * tq].T for g in range(GQA)]

    for h in range(N_KV_HEADS):
        q4 = jnp.concatenate([q_ref[:, (h * GQA + g) * HEAD_DIM:(h * GQA + g + 1) * HEAD_DIM]
                              for g in range(GQA)], axis=0)
        sb4 = jnp.concatenate([selb_ref[:, hsl(h)]] * GQA, axis=0)
        q_aug = jnp.concatenate([q4, sb4 + pad_col], axis=1)
        q_win = jnp.concatenate([q4, pad_col], axis=1)
        tabs = [tab_ref[h * GQA + g:h * GQA + g + 1, :] * LOG2E for g in range(GQA)]
        d0 = jnp.concatenate([_bias_from_dist(t, dij) for t in tabs], axis=1)
        d1 = jnp.concatenate([_bias_from_dist(t, dij + tq) for t in tabs], axis=1)
        far = jnp.concatenate([t[:, HEAD_DIM - 1:HEAD_DIM] + jnp.zeros((1, tq), F32) for t in tabs], axis=1)
        far_t = jnp.broadcast_to(far, (tq, rows))
        diag = jnp.where(causal, d0, NEG_BIG)
        bias_sel = jnp.concatenate([far_t] * (n_back - 1) + [d1, diag], axis=0)
        bias_win = jnp.concatenate([jnp.where(dij4 < 0, far_t, NEG_BIG)] + [far_t] * (n_back - 2) + [d1, diag],
                                   axis=0)

        reset()
        e2_near = e2_ref[pl.ds(near0, near), :]
        k_near = jnp.concatenate([ks_ref[pl.ds(near0, near), hsl(h)], e2_near], axis=1)
        _flash_chunk(k_near, vst_ref[hsl(h), pl.ds(near0, near)], q_aug, bias_sel, m_ref, l_ref, acc_ref, scale)

        def far_scores(i):
            r0 = far0(i)
            k = jnp.concatenate([ks_ref[pl.ds(r0, WINDOW), hsl(h)], e2_ref[pl.ds(r0, WINDOW), :]], axis=1)
            return _dot_nt(k, q_aug)

        def far_update(s_raw, i, bias):
            _flash_update(s_raw, vst_ref[hsl(h), pl.ds(far0(i), WINDOW)], bias, m_ref, l_ref, acc_ref, scale)

        sa_ref[...] = far_scores(0)

        def body(i, carry):
            sb_ref[...] = far_scores(2 * i + 1)
            far_update(sa_ref[...], 2 * i, far)
            sa_ref[...] = far_scores(2 * i + 2)
            far_update(sb_ref[...], 2 * i + 1, far)
            return carry

        lax.fori_loop(0, n_full // 2, body, 0)

        @pl.when(n_full % 2 == 1)
        def _():
            far_update(sa_ref[...], n_full - 1, far)

        @pl.when(n_rem > 0)
        def _():
            if n_back == 2:
                r0 = far0(n_full)
                k = jnp.concatenate([ks_ref[pl.ds(r0, tq), hsl(h)], e2_ref[pl.ds(r0, tq), :]], axis=1)
                _flash_chunk(k, vst_ref[hsl(h), pl.ds(r0, tq)], q_aug, far, m_ref, l_ref, acc_ref, scale)
            else:
                live = lax.broadcasted_iota(jnp.int32, (WINDOW, rows), 0) < n_rem * tq
                far_update(far_scores(n_full), n_full,
                           jnp.where(live, jnp.broadcast_to(far, (WINDOW, rows)), NEG_BIG))

        o_sel = result()

        reset()
        kw_near = jnp.concatenate([kw_ref[pl.ds(near0, near), hsl(h)], e2_near], axis=1)
        _flash_chunk(kw_near, vwt_ref[hsl(h), pl.ds(near0, near)], q_win, bias_win, m_ref, l_ref, acc_ref, scale)
        o_win = result()

        for g in range(GQA):
            hd = h * GQA + g
            o = (sig[:, 3 * hd:3 * hd + 1] * ocmp_ref[:, hsl(hd)]
                 + sig[:, 3 * hd + 1:3 * hd + 2] * o_sel[g]
                 + sig[:, 3 * hd + 2:3 * hd + 3] * o_win[g])
            y = o * lax.rsqrt(jnp.mean(o * o, axis=-1, keepdims=True) + EPS)
            o_ref[:, hsl(hd)] = (y * gnw_ref[:, hsl(hd)]).astype(o_ref.dtype)


def sel_win(qkv, selb, ksp, vstp, kwp, vwtp, e2p, tab, ocmp, gl, gnw, B, S, tq=TQ_SEL):
    T = qkv.shape[0]
    n_qb = S // tq
    dq = N_Q_HEADS * HEAD_DIM
    dkv = N_KV_HEADS * HEAD_DIM
    SP = S + WINDOW
    kern = functools.partial(_sel_win_kernel, n_qb=n_qb, scale=HEAD_DIM ** -0.5 * LOG2E)
    k_spec = pl.BlockSpec((SP, dkv), lambda i: (i // n_qb, 0))
    vt_spec = pl.BlockSpec((dkv, SP), lambda i: (i // n_qb, 0))
    return pl.pallas_call(
        kern,
        grid=(T // tq,),
        in_specs=[pl.BlockSpec((tq, dq), lambda i: (i, 0)),
                  pl.BlockSpec((tq, dkv), lambda i: (i, 0)),
                  k_spec, vt_spec, k_spec, vt_spec,
                  pl.BlockSpec((SP, HEAD_DIM), lambda i: (0, 0)),
                  pl.BlockSpec(tab.shape, lambda i: (0, 0)),
                  pl.BlockSpec((tq, dq), lambda i: (i, 0)),
                  pl.BlockSpec((tq, HEAD_DIM), lambda i: (i, 0)),
                  pl.BlockSpec((1, dq), lambda i: (0, 0))],
        out_specs=pl.BlockSpec((tq, dq), lambda i: (i, 0)),
        out_shape=jax.ShapeDtypeStruct((T, dq), BF16),
        scratch_shapes=[pltpu.VMEM((1, GQA * tq), F32), pltpu.VMEM((1, GQA * tq), F32),
                        pltpu.VMEM((HEAD_DIM, GQA * tq), F32),
                        pltpu.VMEM((WINDOW, GQA * tq), F32), pltpu.VMEM((WINDOW, GQA * tq), F32)],
        compiler_params=_params(("parallel",)),
    )(qkv, selb, ksp, vstp, kwp, vwtp, e2p, tab, ocmp, gl, gnw)


def _conv_kernel(b_ref, c_ref, h_ref, cp_ref, hp_ref, cw_ref, cb_ref, gnw_ref, o_ref, u_ref, *, tiles_per_seq):
    tm = b_ref.shape[0]
    first = (pl.program_id(0) % tiles_per_seq) == 0
    u_prev = cp_ref[...] * hp_ref[...]
    u_ref[0:8, :] = jnp.where(first, 0.0, u_prev)
    u = c_ref[...] * h_ref[...]
    u_ref[8:8 + tm, :] = u
    y = (cw_ref[0:1, :] * u_ref[6:6 + tm, :] + cw_ref[1:2, :] * u_ref[7:7 + tm, :]
         + cw_ref[2:3, :] * u + cb_ref[...])
    o = b_ref[...] * y
    n_groups = o.shape[1] // HEAD_DIM
    for g in range(n_groups):
        sl = slice(g * HEAD_DIM, (g + 1) * HEAD_DIM)
        og = o[:, sl]
        yg = og * lax.rsqrt(jnp.mean(og * og, axis=-1, keepdims=True) + EPS)
        o_ref[:, sl] = (yg * gnw_ref[:, sl]).astype(o_ref.dtype)


def conv_mixer(gc, conv_w, conv_b, gnw, S, tm):
    T = gc.shape[0]
    dc = conv_w.shape[1]
    tps = S // tm
    kern = functools.partial(_conv_kernel, tiles_per_seq=tps)
    prev = lambda col: pl.BlockSpec((8, dc), lambda i, col=col: (jnp.maximum(i * (tm // 8) - 1, 0), col))
    cur = lambda col: pl.BlockSpec((tm, dc), lambda i, col=col: (i, col))
    return pl.pallas_call(
        kern,
        grid=(T // tm,),
        in_specs=[cur(0), cur(1), cur(2), prev(1), prev(2),
                  pl.BlockSpec((8, dc), lambda i: (0, 0)),
                  pl.BlockSpec((1, dc), lambda i: (0, 0)),
                  pl.BlockSpec((1, dc), lambda i: (0, 0))],
        out_specs=pl.BlockSpec((tm, dc), lambda i: (i, 0)),
        out_shape=jax.ShapeDtypeStruct((T, dc), BF16),
        scratch_shapes=[pltpu.VMEM((tm + 8, dc), F32)],
        compiler_params=_params(("parallel",)),
    )(gc, gc, gc, gc, gc, conv_w, conv_b, gnw)


def _out_proj_query_kernel(ma_ref, mc_ref, wa_ref, wc_ref, x_ref, nw_ref, wq_ref, h_ref, q_ref):
    h = x_ref[...] + _dot(ma_ref[...], wa_ref[...]) + _dot(mc_ref[...], wc_ref[...])
    h_ref[...] = h
    y = h * lax.rsqrt(jnp.mean(h * h, axis=-1, keepdims=True) + EPS)
    q_ref[...] = _dot((y * nw_ref[...]).astype(BF16), wq_ref[...]).astype(q_ref.dtype)


def out_proj_query(ma, mc, w, x, norm_w, wq, tm):
    T, da = ma.shape
    dc = mc.shape[1]
    D = x.shape[1]
    assert da == dc and w.shape == (da + dc, D) and wq.shape[0] == D
    resident = pl.Buffered(1)
    row = lambda i: (i, 0)
    return pl.pallas_call(
        _out_proj_query_kernel,
        grid=(T // tm,),
        in_specs=[pl.BlockSpec((tm, da), row),
                  pl.BlockSpec((tm, dc), row),
                  pl.BlockSpec((da, D), lambda i: (0, 0), pipeline_mode=resident),
                  pl.BlockSpec((dc, D), lambda i: (1, 0), pipeline_mode=resident),
                  pl.BlockSpec((tm, D), row),
                  pl.BlockSpec((1, D), lambda i: (0, 0)),
                  pl.BlockSpec(wq.shape, lambda i: (0, 0), pipeline_mode=resident)],
        out_specs=[pl.BlockSpec((tm, D), row), pl.BlockSpec((tm, wq.shape[1]), row)],
        out_shape=[jax.ShapeDtypeStruct((T, D), F32), jax.ShapeDtypeStruct((T, wq.shape[1]), BF16)],
        compiler_params=_params(("parallel",)),
    )(ma, mc, w, w, x, norm_w.reshape(1, D), wq)


def _sort_network(n):
    pairs = []
    p = 1
    while p < n:
        k = p
        while k >= 1:
            for j in range(k % p, n - k, 2 * k):
                for i in range(min(k, n - j - k)):
                    if (i + j) // (2 * p) == (i + j + k) // (2 * p):
                        pairs.append((i + j, i + j + k))
            k //= 2
        p *= 2
    return pairs


def _peer_route_kernel(q_ref, sk_ref, u_ref, v_ref, e_ref, g_ref, ub_ref, vb_ref, sv_ref, si_ref):
    ub_ref[...] = u_ref[...].astype(BF16)
    vb_ref[...] = v_ref[...].astype(BF16)
    tt = q_ref.shape[0]
    K = PEER_TOPK
    sub = SUBLANES
    n_io = lax.broadcasted_iota(jnp.int32, (N_KEYS, tt), 0).astype(F32)
    r8 = lax.broadcasted_iota(jnp.int32, (sub, tt), 0).astype(F32)
    ninf = -jnp.inf
    lens = [K // (a + 1) for a in range(N_FIXED_A)]
    lens += [max(K // (b + 1) - N_FIXED_A, 0) for b in range(sub - N_FIXED_A)]
    assert sum(lens) == sum(K // (a + 1) for a in range(K)) and lens[-1] == 0
    fixed_a = r8 < float(N_FIXED_A)
    list_len = jnp.zeros((sub, tt), F32)
    for row, n in enumerate(lens):
        list_len = jnp.where(r8 == float(row), float(n), list_len)

    for h in range(PEER_HEADS):
        for c in range(2):
            col = (h * 2 + c) * HEAD_DIM
            s = _dot_nt(sk_ref[h * 2 + c], q_ref[:, col:col + HEAD_DIM])
            n_col = N_KEYS // sub
            vals = [s[j * sub:(j + 1) * sub, :] for j in range(n_col)]
            idxs = [n_io[j * sub:(j + 1) * sub, :] for j in range(n_col)]
            for lo, hi in _sort_network(n_col):
                swap = (vals[hi] > vals[lo]) | ((vals[hi] == vals[lo]) & (idxs[hi] < idxs[lo]))
                vals[lo], vals[hi] = jnp.where(swap, vals[hi], vals[lo]), jnp.where(swap, vals[lo], vals[hi])
                idxs[lo], idxs[hi] = jnp.where(swap, idxs[hi], idxs[lo]), jnp.where(swap, idxs[lo], idxs[hi])
            for k in range(K):
                m = jnp.max(vals[0], axis=0, keepdims=True)
                idx = jnp.min(jnp.where(vals[0] == m, idxs[0], float(N_KEYS)), axis=0, keepdims=True)
                sv_ref[c, k:k + 1, :] = m
                si_ref[c, k:k + 1, :] = idx
                win = idxs[0] == idx
                for j in range(min(n_col, K) - 1 - k):
                    vals[j] = jnp.where(win, vals[j + 1], vals[j])
                    idxs[j] = jnp.where(win, idxs[j + 1], idxs[j])
        sv0, sv1 = sv_ref[0], sv_ref[1]
        si0, si1 = si_ref[0], si_ref[1]
        sv1_low = pltpu.roll(sv1[0:sub, :], N_FIXED_A, 0)
        sv0_top = sv0[0:sub, :]
        lv, lf = [], []
        for dep in range(K):
            a_dep = min(N_FIXED_A + dep, K - 1)
            val = (jnp.where(fixed_a, sv0_top, sv0[a_dep:a_dep + 1, :])
                   + jnp.where(fixed_a, sv1[dep:dep + 1, :], sv1_low))
            lv.append(jnp.where(list_len > dep, val, ninf))
            lf.append(jnp.where(fixed_a, r8 * float(K) + dep, (N_FIXED_A + dep) * float(K) + r8 - N_FIXED_A))
        cvs, fls = [], []
        for k in range(K):
            m = jnp.max(lv[0], axis=0, keepdims=True)
            fsel = jnp.min(jnp.where(lv[0] == m, lf[0], 1e9), axis=0, keepdims=True)
            win = lf[0] == fsel
            cvs.append(m)
            fls.append(fsel)
            for dep in range(K - 1 - k):
                lv[dep] = jnp.where(win, lv[dep + 1], lv[dep])
                lf[dep] = jnp.where(win, lf[dep + 1], lf[dep])
        cv = jnp.concatenate(cvs, axis=0)
        fl = jnp.concatenate(fls, axis=0)
        a_sel = jnp.floor(fl * (1.0 / K))
        b_sel = fl - a_sel * K
        i1 = jnp.zeros_like(fl)
        i2 = jnp.zeros_like(fl)
        for r in range(K):
            i1 = jnp.where(a_sel == r, si0[r:r + 1, :], i1)
            i2 = jnp.where(b_sel == r, si1[r:r + 1, :], i2)
        ex = i1 * float(N_KEYS) + i2
        ev = jnp.exp(cv - jnp.max(cv, axis=0, keepdims=True))
        gates = ev / jnp.sum(ev, axis=0, keepdims=True)
        e_ref[0, h * K:(h + 1) * K, :] = ex.astype(jnp.int32)
        g_ref[0, h * K:(h + 1) * K, :] = gates


def peer_route(qp, subkeys, u, v):
    T = qp.shape[0]
    tt = TT_ROUTE
    P = PEER_HEADS * PEER_TOPK
    nt = T // tt
    E, D = u.shape
    slab = E // nt
    assert slab * nt == E and slab % 16 == 0
    tab_spec = pl.BlockSpec((slab, D), lambda i: (i, 0))
    return pl.pallas_call(
        _peer_route_kernel,
        grid=(nt,),
        in_specs=[pl.BlockSpec((tt, qp.shape[1]), lambda i: (i, 0)),
                  pl.BlockSpec(subkeys.shape, lambda i: (0, 0, 0)),
                  tab_spec, tab_spec],
        out_specs=[pl.BlockSpec((1, P, tt), lambda i: (i, 0, 0))] * 2 + [tab_spec, tab_spec],
        out_shape=[jax.ShapeDtypeStruct((nt, P, tt), jnp.int32),
                   jax.ShapeDtypeStruct((nt, P, tt), F32),
                   jax.ShapeDtypeStruct((E, D), BF16), jax.ShapeDtypeStruct((E, D), BF16)],
        scratch_shapes=[pltpu.VMEM((2, PEER_TOPK, tt), F32), pltpu.VMEM((2, PEER_TOPK, tt), F32)],
        compiler_params=_params(("parallel",)),
    )(qp, subkeys, u, v)


def _peer_gbuild_kernel(e_ref, g_ref, o_ref, i1_ref, i2_ref, gt_ref):
    tt = e_ref.shape[2]
    e = e_ref[0].T
    i1_ref[...] = e >> 7
    i2_ref[...] = e & (N_KEYS - 1)
    gt_ref[...] = g_ref[0].T
    P = e.shape[1]
    k_io = lax.broadcasted_iota(jnp.int32, (N_KEYS, P), 0)
    group = tt

    def body(tg, carry):
        base = pl.multiple_of(tg * group, group)
        r1s = i1_ref[pl.ds(base, group), :]
        r2s = i2_ref[pl.ds(base, group), :]
        rgs = gt_ref[pl.ds(base, group), :]
        for u in range(group):
            lhs = jnp.where(k_io == r1s[u:u + 1, :], rgs[u:u + 1, :], 0.0).astype(BF16)
            rhs = jnp.where(k_io == r2s[u:u + 1, :], 1.0, 0.0).astype(BF16)
            g_t = _dot_nt(lhs, rhs).astype(BF16).astype(F32)
            lo = pltpu.bitcast(g_t[:HALF_KEYS], jnp.uint32) >> 16
            hi = pltpu.bitcast(g_t[HALF_KEYS:], jnp.uint32)
            words = hi | lo
            for a in range(HALF_KEYS // PACK_ROWS):
                o_ref[a, base + u] = words[a * PACK_ROWS:(a + 1) * PACK_ROWS, :]
        return carry

    lax.fori_loop(0, tt // group, body, 0)


def peer_gbuild(ex, gates):
    nt, P, tt = ex.shape
    T = nt * tt
    n_a = HALF_KEYS // PACK_ROWS
    return pl.pallas_call(
        _peer_gbuild_kernel,
        grid=(nt,),
        in_specs=[pl.BlockSpec((1, P, tt), lambda i: (i, 0, 0))] * 2,
        out_specs=pl.BlockSpec((n_a, tt, PACK_ROWS, N_KEYS), lambda i: (0, i, 0, 0)),
        out_shape=jax.ShapeDtypeStruct((n_a, T, PACK_ROWS, N_KEYS), jnp.uint32),
        scratch_shapes=[pltpu.VMEM((tt, P), jnp.int32), pltpu.VMEM((tt, P), jnp.int32),
                        pltpu.VMEM((tt, P), F32)],
        compiler_params=_params(("parallel",)),
    )(ex, gates)


def _peer_dense_kernel(h_ref, nw_ref, fw_ref, u_ref, v_ref, gm_ref, o_ref, hn_ref, acc_ref):
    j = pl.program_id(1)
    tt = h_ref.shape[0]

    @pl.when(j == 0)
    def _():
        x = h_ref[...]
        y = x * lax.rsqrt(jnp.mean(x * x, axis=-1, keepdims=True) + EPS)
        hn_ref[...] = (y * nw_ref[...]).astype(BF16)
        acc_ref[...] = jnp.zeros(acc_ref.shape, F32)

    a = _dot_nt(hn_ref[...], u_ref[...])
    shift = (16 * (1 - j % 2)).astype(jnp.uint32)
    gm = jnp.concatenate(
        [pltpu.bitcast((gm_ref[pl.ds(k, tt, stride=PACK_ROWS), :] << shift) & jnp.uint32(0xFFFF0000), F32)
         for k in range(PACK_ROWS)], axis=1)
    w = (gm * jax.nn.gelu(a)).astype(BF16)
    acc_ref[...] += _dot(w, v_ref[...])

    @pl.when(j == pl.num_programs(1) - 1)
    def _():
        x = h_ref[...] + acc_ref[...]
        y = x * lax.rsqrt(jnp.mean(x * x, axis=-1, keepdims=True) + EPS)
        o_ref[...] = y * fw_ref[...]


def peer_dense(h, ffn_nw, final_nw, u, v, gm_words, tt):
    T, D = h.shape
    E = u.shape[0]
    ec = PACK_ROWS * N_KEYS
    n_a = gm_words.shape[0]
    assert E == 2 * n_a * ec
    gm2 = gm_words.reshape(n_a, T * PACK_ROWS, N_KEYS)
    tab_spec = pl.BlockSpec((ec, D), lambda i, j: ((j % 2) * n_a + j // 2, 0))
    return pl.pallas_call(
        _peer_dense_kernel,
        grid=(T // tt, E // ec),
        in_specs=[pl.BlockSpec((tt, D), lambda i, j: (i, 0)),
                  pl.BlockSpec((1, D), lambda i, j: (0, 0)),
                  pl.BlockSpec((1, D), lambda i, j: (0, 0)),
                  tab_spec, tab_spec,
                  pl.BlockSpec((None, tt * PACK_ROWS, N_KEYS), lambda i, j: (j // 2, i, 0))],
        out_specs=pl.BlockSpec((tt, D), lambda i, j: (i, 0)),
        out_shape=jax.ShapeDtypeStruct((T, D), F32),
        scratch_shapes=[pltpu.VMEM((tt, D), BF16), pltpu.VMEM((tt, D), F32)],
        compiler_params=_params(("parallel", "arbitrary")),
    )(h, ffn_nw.reshape(1, D), final_nw.reshape(1, D), u, v, gm2)


def _t5_bucket_np(n_dist):
    d = np.arange(n_dist)
    max_exact = NUM_BUCKETS // 2
    nf = np.maximum(d, 1).astype(np.float64)
    large = max_exact + (np.log(nf / max_exact) / math.log(MAX_DISTANCE / max_exact)
                         * (NUM_BUCKETS - max_exact)).astype(np.int64)
    large = np.minimum(large, NUM_BUCKETS - 1)
    return np.where(d < max_exact, d, large).astype(np.int32)


def _overlap_t_np(S):
    n_c = (S - L_CMP) // STRIDE_CMP + 1
    n_sel = S // L_SEL
    pos = np.arange(n_c)[:, None] * STRIDE_CMP + np.arange(L_CMP)[None, :]
    m = np.zeros((n_c + 1, n_sel), np.float32)
    np.add.at(m, (np.repeat(np.arange(n_c), L_CMP), (pos // L_SEL).reshape(-1)), 1.0 / L_CMP)
    return np.ascontiguousarray(m.T)


def _block_onehot_padded_np(S):
    assert S // L_SEL < HEAD_DIM
    e2 = np.zeros((WINDOW + S, HEAD_DIM), np.float32)
    e2[WINDOW + np.arange(S), np.arange(S) // L_SEL] = 1.0
    e2[:WINDOW, HEAD_DIM - 1] = SEL_OFF
    return e2


def nsa_conv_mix(xt, B, S, attn_norm_w, w_in, w_cmp_k, w_cmp_v, cmp_pos, conv_w, conv_b,
                 attn_gnw, conv_gnw, rel_bias, tm=TM_PROJ, tm_conv=TM_CONV, cast_also=()):
    T, D = xt.shape
    dq = N_Q_HEADS * HEAD_DIM
    dkv = N_KV_HEADS * HEAD_DIM
    n_attn = dq + 6 * dkv
    n_gate = 3 * N_Q_HEADS
    dc = (w_in.shape[1] - n_attn - n_gate) // 3
    w_t = w_in.T
    w_attn = cast_rows(w_t, 0, n_attn)
    w_conv = cast_rows(w_t, n_attn + n_gate, 3 * dc)
    w_gate = jnp.pad(w_t[n_attn:n_attn + n_gate], ((0, HEAD_DIM - n_gate), (0, 0))).astype(BF16)

    qkv = norm_matmul(xt, attn_norm_w, w_attn, BF16, tm, n_attn // 2, w_is_nk=True)
    gc, gl = norm_matmul(xt, attn_norm_w, w_conv, F32, tm, dc, w_is_nk=True, w_side=w_gate)

    n16 = S // STRIDE_CMP
    k16 = qkv[:, dq:dq + dkv].reshape(B * n16, STRIDE_CMP * dkv)
    v16 = qkv[:, dq + dkv:dq + 2 * dkv].reshape(B * n16, STRIDE_CMP * dkv)

    def wflat(w, lo):
        return w[lo:lo + STRIDE_CMP].reshape(STRIDE_CMP * HEAD_DIM, HEAD_DIM).astype(BF16)

    def posrow(lo):
        return cmp_pos[lo:lo + STRIDE_CMP].reshape(1, STRIDE_CMP * HEAD_DIM)

    kc, vc = compress(k16, v16, posrow(0), posrow(STRIDE_CMP),
                      wflat(w_cmp_k, 0), wflat(w_cmp_k, STRIDE_CMP),
                      wflat(w_cmp_v, 0), wflat(w_cmp_v, STRIDE_CMP), B)

    tab = rel_bias[_t5_bucket_np(HEAD_DIM)].T
    ovt = jnp.asarray(_overlap_t_np(S), BF16)
    e2p = jnp.asarray(_block_onehot_padded_np(S), BF16)

    ocmp, selb, *casts = cmp_select(qkv, kc, vc, tab, ovt, B, S, cast_also=cast_also)
    ksp, vstp, kwp, vwtp = repack_kv(qkv, B, S, first_col_block=dq // dkv + 2)
    attn = sel_win(qkv, selb, ksp, vstp, kwp, vwtp, e2p, tab, ocmp, gl, attn_gnw.reshape(1, dq), B, S)
    cw8 = jnp.pad(conv_w, ((0, 8 - conv_w.shape[0]), (0, 0)))
    conv = conv_mixer(gc, cw8, conv_b.reshape(1, dc), conv_gnw.reshape(1, dc), S, tm_conv)
    return (attn, conv, *casts)


def peer_block(h, qp, ffn_nw, final_nw, peer_subkeys, peer_u, peer_v, tt=TT_DENSE):
    sk = peer_subkeys.reshape(PEER_HEADS * 2, N_KEYS, peer_subkeys.shape[-1]).astype(BF16)
    ex, gates, u_bf, v_bf = peer_route(qp, sk, peer_u, peer_v)
    gm = peer_gbuild(ex, gates)
    return peer_dense(h, ffn_nw, final_nw, u_bf, v_bf, gm, tt)


def kernel(x, attn_norm_w, w_in, w_cmp_k, w_cmp_v, cmp_pos, conv_w, conv_b, attn_group_norm_w,
           conv_group_norm_w, w_out, rel_bias, ffn_norm_w, peer_wq, peer_subkeys, peer_u, peer_v,
           final_norm_w):
    B, S, D = x.shape
    T = B * S
    xt = x.reshape(T, D)
    attn, conv, wo, wq = nsa_conv_mix(xt, B, S, attn_norm_w[0], w_in[0], w_cmp_k[0], w_cmp_v[0], cmp_pos[0],
                                      conv_w[0], conv_b[0], attn_group_norm_w[0], conv_group_norm_w[0], rel_bias,
                                      cast_also=(w_out[0], peer_wq[0]))
    h, qp = out_proj_query(attn, conv, wo, xt, ffn_norm_w[0], wq, TM_OUT)
    out = peer_block(h, qp, ffn_norm_w[0], final_norm_w, peer_subkeys[0], peer_u[0], peer_v[0])
    return out.reshape(B, S, D)
```

```python
import functools
import math

import jax
import jax.numpy as jnp
import numpy as np
from jax import lax
from jax.experimental import pallas as pl
from jax.experimental.pallas import tpu as pltpu

F32 = jnp.float32
BF16 = jnp.bfloat16

HEAD_DIM = 128
N_KV_HEADS = 2
GQA = 4
N_Q_HEADS = N_KV_HEADS * GQA
L_CMP = 32
STRIDE_CMP = 16
L_SEL = 64
N_SEL = 16
WINDOW = 512
Q_BLOCK = 128
FORCED_SCORE = float(GQA + 1)
NUM_BUCKETS = 32
MAX_DISTANCE = 128
PEER_HEADS = 8
N_KEYS = 128
PEER_TOPK = 16
EPS = 1e-6
NEG_BIG = -1e30
SEL_OFF = -float(2 ** 30)
LOG2E = math.log2(math.e)
N_FIXED_A = 4

SUBLANES = 8
LANES = 128
MXU_DIM = 256
V7X_VMEM_BYTES = 64 * 1024 * 1024
VMEM_LIMIT = V7X_VMEM_BYTES * 7 // 8

HALF_KEYS = N_KEYS // 2
PACK_ROWS = SUBLANES

TM_PROJ = MXU_DIM
TM_OUT = 2 * MXU_DIM
TM_CONV = 2 * MXU_DIM
TQ_CMP = 4 * Q_BLOCK
TQ_SEL = 2 * Q_BLOCK
TT_ROUTE = LANES
TT_DENSE = 2 * MXU_DIM
CAST_ROWS = MXU_DIM


def _dot(a, b):
    return jnp.dot(a, b, preferred_element_type=F32)


def _dot_nt(a, b):
    return lax.dot_general(a, b, (((1,), (1,)), ((), ())), preferred_element_type=F32)


def _params(sem, vmem=VMEM_LIMIT):
    return pltpu.CompilerParams(dimension_semantics=sem, vmem_limit_bytes=vmem)


def _in_proj_kernel(x_ref, nw_ref, wa_ref, wc_ref, wg_ref, oa_ref, oc_ref, og_ref):
    x = x_ref[...]
    y = x * lax.rsqrt(jnp.mean(x * x, axis=-1, keepdims=True) + EPS)
    xn = (y * nw_ref[...]).astype(BF16)
    oa_ref[...] = _dot_nt(xn, wa_ref[...]).astype(oa_ref.dtype)
    oc_ref[...] = _dot_nt(xn, wc_ref[...])
    og_ref[...] = _dot_nt(xn, wg_ref[...])


def in_proj(x, norm_w, w_attn, w_conv, w_gate, tm):
    T, D = x.shape
    resident = pl.Buffered(1)
    row = lambda i: (i, 0)
    fixed = lambda i: (0, 0)
    widths = [w.shape[0] for w in (w_attn, w_conv, w_gate)]
    return pl.pallas_call(
        _in_proj_kernel,
        grid=(T // tm,),
        in_specs=[pl.BlockSpec((tm, D), row), pl.BlockSpec((1, D), fixed)]
                 + [pl.BlockSpec(w.shape, fixed, pipeline_mode=resident) for w in (w_attn, w_conv, w_gate)],
        out_specs=[pl.BlockSpec((tm, n), row) for n in widths],
        out_shape=[jax.ShapeDtypeStruct((T, widths[0]), BF16), jax.ShapeDtypeStruct((T, widths[1]), F32),
                   jax.ShapeDtypeStruct((T, widths[2]), F32)],
        compiler_params=_params(("parallel",)),
    )(x, norm_w.reshape(1, D), w_attn, w_conv, w_gate)


def _cast_rows_kernel(w_ref, o_ref):
    o_ref[...] = w_ref[...].astype(o_ref.dtype)


def cast_rows(w, row0, n_rows, tr=CAST_ROWS):
    K = w.shape[1]
    assert row0 % SUBLANES == 0 and n_rows % tr == 0
    return pl.pallas_call(
        _cast_rows_kernel,
        grid=(n_rows // tr,),
        in_specs=[pl.BlockSpec((pl.Element(tr), pl.Element(K)), lambda i: (pl.multiple_of(row0 + i * tr, SUBLANES), 0))],
        out_specs=pl.BlockSpec((tr, K), lambda i: (i, 0)),
        out_shape=jax.ShapeDtypeStruct((n_rows, K), BF16),
        compiler_params=_params(("parallel",)),
    )(w)


def _compress_kernel(k_ref, v_ref, pa_ref, pb_ref, wka_ref, wkb_ref, wva_ref, wvb_ref, kc_ref, vc_ref):
    dkv = kc_ref.shape[1]

    def one(x_ref, wa_ref, wb_ref, o_ref):
        for h in range(N_KV_HEADS):
            x = jnp.concatenate([x_ref[:, l * dkv + h * HEAD_DIM:l * dkv + (h + 1) * HEAD_DIM]
                                 for l in range(STRIDE_CMP)], axis=1).astype(F32)
            a = _dot((x + pa_ref[...]).astype(BF16), wa_ref[...])
            b = _dot((x + pb_ref[...]).astype(BF16), wb_ref[...])
            n = b.shape[0]
            o_ref[:, h * HEAD_DIM:(h + 1) * HEAD_DIM] = (a + pltpu.roll(b, n - 1, 0)).astype(o_ref.dtype)

    one(k_ref, wka_ref, wkb_ref, kc_ref)
    one(v_ref, wva_ref, wvb_ref, vc_ref)


def compress(k16, v16, pos_a, pos_b, wka, wkb, wva, wvb, B):
    R, C = k16.shape
    nb = R // B
    dkv = C // STRIDE_CMP
    full = lambda shp: pl.BlockSpec(shp, lambda b: (0, 0))
    return pl.pallas_call(
        _compress_kernel,
        grid=(B,),
        in_specs=[pl.BlockSpec((nb, C), lambda b: (b, 0)),
                  pl.BlockSpec((nb, C), lambda b: (b, 0)),
                  full(pos_a.shape), full(pos_b.shape),
                  full(wka.shape), full(wkb.shape), full(wva.shape), full(wvb.shape)],
        out_specs=[pl.BlockSpec((nb, dkv), lambda b: (b, 0))] * 2,
        out_shape=[jax.ShapeDtypeStruct((R, dkv), BF16)] * 2,
        compiler_params=_params(("parallel",)),
    )(k16, v16, pos_a, pos_b, wka, wkb, wva, wvb)


def _bias_from_dist(tab_row, dist):
    idx = jnp.clip(dist, 0, 127)
    w = tab_row.shape[1]
    tab = jnp.broadcast_to(tab_row, (idx.shape[0], w))
    parts = [jnp.take_along_axis(tab, idx[:, k:k + w], axis=1) for k in range(0, idx.shape[1], w)]
    return parts[0] if len(parts) == 1 else jnp.concatenate(parts, axis=1)


def _cmp_select_kernel(q_ref, kc_ref, vc_ref, tab_ref, ovt_ref, *rest, n_qb, scale, n_cast):
    cast_in, (ocmp_ref, selb_ref), cast_out = rest[:n_cast], rest[n_cast:n_cast + 2], rest[n_cast + 2:]
    for src_ref, dst_ref in zip(cast_in, cast_out):
        dst_ref[...] = src_ref[...].astype(BF16)
    c = pl.program_id(0) % n_qb
    tq = q_ref.shape[0]
    n_c = kc_ref.shape[0]
    n_sel = ovt_ref.shape[0]
    t0 = c * tq
    t_col = t0 + lax.broadcasted_iota(jnp.int32, (tq, n_c), 0)
    n_row = lax.broadcasted_iota(jnp.int32, (tq, n_c), 1)
    dist = t_col - (n_row * STRIDE_CMP + (L_CMP - 1))
    valid = dist >= 0

    j_io = lax.broadcasted_iota(jnp.int32, (n_sel, tq), 0)
    t_io = t0 + lax.broadcasted_iota(jnp.int32, (n_sel, tq), 1)
    blk_t = t_io // L_SEL
    forced = (j_io == 0) | (j_io == blk_t) | (j_io == blk_t - 1)
    causal_blk = j_io * L_SEL <= t_io

    for h in range(N_KV_HEADS):
        kc = kc_ref[:, h * HEAD_DIM:(h + 1) * HEAD_DIM]
        vc = vc_ref[:, h * HEAD_DIM:(h + 1) * HEAD_DIM]
        psum = jnp.zeros((tq, n_c), F32)
        for g in range(GQA):
            hd = h * GQA + g
            qh = q_ref[:, hd * HEAD_DIM:(hd + 1) * HEAD_DIM]
            bias = _bias_from_dist(tab_ref[hd:hd + 1, :], dist)
            s = _dot_nt(qh, kc) * scale + bias
            s = jnp.where(valid, s, NEG_BIG)
            m = jnp.max(s, axis=-1, keepdims=True)
            e = jnp.where(valid, jnp.exp(s - m), 0.0)
            d = jnp.sum(e, axis=-1, keepdims=True)
            p = e / jnp.where(d > 0, d, 1.0)
            ocmp_ref[:, hd * HEAD_DIM:(hd + 1) * HEAD_DIM] = _dot(p.astype(BF16), vc)
            psum = psum + p
        p_hi = psum.astype(BF16)
        p_lo = (psum - p_hi.astype(F32)).astype(BF16)
        ovt = ovt_ref[...]
        imp = _dot_nt(ovt, p_hi) + _dot_nt(ovt, p_lo)
        score = jnp.where(forced, FORCED_SCORE, jnp.where(causal_blk, imp, -1.0))
        rank = jnp.zeros((n_sel, tq), F32)
        for jp in range(n_sel):
            row = score[jp:jp + 1, :]
            rank = rank + jnp.where(j_io > jp, jnp.where(row >= score, 1.0, 0.0),
                                    jnp.where(row > score, 1.0, 0.0))
        selb = jnp.where(rank < float(N_SEL), 0.0, SEL_OFF)
        if n_sel < HEAD_DIM:
            selb = jnp.concatenate([selb, jnp.zeros((HEAD_DIM - n_sel, tq), F32)], axis=0)
        selb_ref[:, h * HEAD_DIM:(h + 1) * HEAD_DIM] = selb.T.astype(BF16)


def cmp_select(qkv, kc, vc, tab, ovt, B, S, tq=TQ_CMP, cast_also=()):
    T = qkv.shape[0]
    n_qb = S // tq
    steps = T // tq
    n_c = kc.shape[0] // B
    dq = N_Q_HEADS * HEAD_DIM
    dkv = N_KV_HEADS * HEAD_DIM
    kern = functools.partial(_cmp_select_kernel, n_qb=n_qb, scale=HEAD_DIM ** -0.5, n_cast=len(cast_also))
    cast_specs = [pl.BlockSpec((w.shape[0] // steps, w.shape[1]), lambda i: (i, 0)) for w in cast_also]
    assert all(w.shape[0] % (2 * SUBLANES * steps) == 0 for w in cast_also)
    return pl.pallas_call(
        kern,
        grid=(steps,),
        in_specs=[pl.BlockSpec((tq, dq), lambda i: (i, 0)),
                  pl.BlockSpec((n_c, dkv), lambda i: (i // n_qb, 0)),
                  pl.BlockSpec((n_c, dkv), lambda i: (i // n_qb, 0)),
                  pl.BlockSpec(tab.shape, lambda i: (0, 0)),
                  pl.BlockSpec(ovt.shape, lambda i: (0, 0))] + cast_specs,
        out_specs=[pl.BlockSpec((tq, dq), lambda i: (i, 0)),
                   pl.BlockSpec((tq, dkv), lambda i: (i, 0))] + cast_specs,
        out_shape=[jax.ShapeDtypeStruct((T, dq), F32),
                   jax.ShapeDtypeStruct((T, dkv), BF16)] + [jax.ShapeDtypeStruct(w.shape, BF16) for w in cast_also],
        compiler_params=_params(("parallel",)),
    )(qkv, kc, vc, tab, ovt, *cast_also)


def _repack_kv_kernel(ks_ref, vs_ref, kw_ref, vw_ref, ksp_ref, vst_ref, kwp_ref, vwt_ref):
    first = pl.program_id(1) == 0

    @pl.when(first)
    def _():
        for ref in (ksp_ref, vst_ref, kwp_ref, vwt_ref):
            ref[...] = jnp.zeros(ref.shape, ref.dtype)

    @pl.when(jnp.logical_not(first))
    def _():
        ksp_ref[...] = ks_ref[...]
        kwp_ref[...] = kw_ref[...]
        vst_ref[...] = vs_ref[...].T
        vwt_ref[...] = vw_ref[...].T


def repack_kv(qkv, B, S, first_col_block):
    dkv = N_KV_HEADS * HEAD_DIM
    nt = S // WINDOW
    src = lambda col: pl.BlockSpec((WINDOW, dkv), lambda b, i, col=col: (b * nt + jnp.maximum(i - 1, 0), col))
    rows = pl.BlockSpec((WINDOW, dkv), lambda b, i: (b * (nt + 1) + i, 0))
    cols = pl.BlockSpec((dkv, WINDOW), lambda b, i: (b, i))
    return pl.pallas_call(
        _repack_kv_kernel,
        grid=(B, nt + 1),
        in_specs=[src(first_col_block + k) for k in range(4)],
        out_specs=[rows, cols, rows, cols],
        out_shape=[jax.ShapeDtypeStruct((B * (S + WINDOW), dkv), qkv.dtype),
                   jax.ShapeDtypeStruct((B * dkv, S + WINDOW), qkv.dtype)] * 2,
        compiler_params=_params(("parallel", "arbitrary")),
    )(qkv, qkv, qkv, qkv)


def _flash_update(s_raw, vt, bias, m_ref, l_ref, acc_ref, scale):
    m_old = m_ref[...]
    if bias.shape[0] == 1:
        s = s_raw * scale
        m_new = jnp.maximum(m_old, jnp.max(s, axis=0, keepdims=True) + bias)
        p = jnp.exp2(s - (m_new - bias))
    else:
        s = s_raw * scale + bias
        m_new = jnp.maximum(m_old, jnp.max(s, axis=0, keepdims=True))
        p = jnp.exp2(s - m_new)
    alpha = jnp.exp2(m_old - m_new)
    l_ref[...] = alpha * l_ref[...] + jnp.sum(p, axis=0, keepdims=True)
    acc_ref[...] = alpha * acc_ref[...] + _dot(vt, p.astype(BF16))
    m_ref[...] = m_new


def _flash_chunk(k, vt, q, bias, m_ref, l_ref, acc_ref, scale):
    _flash_update(_dot_nt(k, q), vt, bias, m_ref, l_ref, acc_ref, scale)


def _sel_win_kernel(q_ref, selb_ref, ks_ref, vst_ref, kw_ref, vwt_ref, e2_ref, tab_ref, ocmp_ref,
                    gl_ref, gnw_ref, o_ref, m_ref, l_ref, acc_ref, sa_ref, sb_ref, *, n_qb, scale):
    c = pl.program_id(0) % n_qb
    tq = q_ref.shape[0]
    rows = GQA * tq
    n_back = WINDOW // tq
    near = WINDOW + tq
    j_io = lax.broadcasted_iota(jnp.int32, (tq, tq), 0)
    i_io = lax.broadcasted_iota(jnp.int32, (tq, tq), 1)
    dij = i_io - j_io
    dij4 = jnp.concatenate([dij] * GQA, axis=1)
    causal = dij4 >= 0
    pad_col = jnp.where(lax.broadcasted_iota(jnp.int32, (rows, HEAD_DIM), 1) == HEAD_DIM - 1, 1.0, 0.0).astype(BF16)
    sig = jax.nn.sigmoid(gl_ref[...])
    near0 = pl.multiple_of(c * tq, tq)
    n_far = jnp.maximum(c - n_back, 0)
    n_full = n_far // n_back
    n_rem = n_far - n_full * n_back

    def hsl(h):
        return slice(h * HEAD_DIM, (h + 1) * HEAD_DIM)

    def far0(i):
        return pl.multiple_of(WINDOW + i * WINDOW, WINDOW)

    def reset():
        m_ref[...] = jnp.full(m_ref.shape, NEG_BIG, F32)
        l_ref[...] = jnp.zeros(l_ref.shape, F32)
        acc_ref[...] = jnp.zeros(acc_ref.shape, F32)

    def result():
        ot = acc_ref[...] / l_ref[...]
        return [ot[:, g * tq:(g + 1) * tq].T for g in range(GQA)]

    for h in range(N_KV_HEADS):
        q4 = jnp.concatenate([q_ref[:, (h * GQA + g) * HEAD_DIM:(h * GQA + g + 1) * HEAD_DIM]
                              for g in range(GQA)], axis=0)
        sb4 = jnp.concatenate([selb_ref[:, hsl(h)]] * GQA, axis=0)
        q_aug = jnp.concatenate([q4, sb4 + pad_col], axis=1)
        q_win = jnp.concatenate([q4, pad_col], axis=1)
        tabs = [tab_ref[h * GQA + g:h * GQA + g + 1, :] * LOG2E for g in range(GQA)]
        d0 = jnp.concatenate([_bias_from_dist(t, dij) for t in tabs], axis=1)
        d1 = jnp.concatenate([_bias_from_dist(t, dij + tq) for t in tabs], axis=1)
        far = jnp.concatenate([t[:, HEAD_DIM - 1:HEAD_DIM] + jnp.zeros((1, tq), F32) for t in tabs], axis=1)
        far_t = jnp.broadcast_to(far, (tq, rows))
        diag = jnp.where(causal, d0, NEG_BIG)
        bias_sel = jnp.concatenate([far_t] * (n_back - 1) + [d1, diag], axis=0)
        bias_win = jnp.concatenate([jnp.where(dij4 < 0, far_t, NEG_BIG)] + [far_t] * (n_back - 2) + [d1, diag],
                                   axis=0)

        reset()
        e2_near = e2_ref[pl.ds(near0, near), :]
        k_near = jnp.concatenate([ks_ref[pl.ds(near0, near), hsl(h)], e2_near], axis=1)
        _flash_chunk(k_near, vst_ref[hsl(h), pl.ds(near0, near)], q_aug, bias_sel, m_ref, l_ref, acc_ref, scale)

        def far_scores(i):
            r0 = far0(i)
            k = jnp.concatenate([ks_ref[pl.ds(r0, WINDOW), hsl(h)], e2_ref[pl.ds(r0, WINDOW), :]], axis=1)
            return _dot_nt(k, q_aug)

        def far_update(s_raw, i, bias):
            _flash_update(s_raw, vst_ref[hsl(h), pl.ds(far0(i), WINDOW)], bias, m_ref, l_ref, acc_ref, scale)

        sa_ref[...] = far_scores(0)

        def body(i, carry):
            sb_ref[...] = far_scores(2 * i + 1)
            far_update(sa_ref[...], 2 * i, far)
            sa_ref[...] = far_scores(2 * i + 2)
            far_update(sb_ref[...], 2 * i + 1, far)
            return carry

        lax.fori_loop(0, n_full // 2, body, 0)

        @pl.when(n_full % 2 == 1)
        def _():
            far_update(sa_ref[...], n_full - 1, far)

        @pl.when(n_rem > 0)
        def _():
            if n_back == 2:
                r0 = far0(n_full)
                k = jnp.concatenate([ks_ref[pl.ds(r0, tq), hsl(h)], e2_ref[pl.ds(r0, tq), :]], axis=1)
                _flash_chunk(k, vst_ref[hsl(h), pl.ds(r0, tq)], q_aug, far, m_ref, l_ref, acc_ref, scale)
            else:
                live = lax.broadcasted_iota(jnp.int32, (WINDOW, rows), 0) < n_rem * tq
                far_update(far_scores(n_full), n_full,
                           jnp.where(live, jnp.broadcast_to(far, (WINDOW, rows)), NEG_BIG))

        o_sel = result()

        reset()
        kw_near = jnp.concatenate([kw_ref[pl.ds(near0, near), hsl(h)], e2_near], axis=1)
        _flash_chunk(kw_near, vwt_ref[hsl(h), pl.ds(near0, near)], q_win, bias_win, m_ref, l_ref, acc_ref, scale)
        o_win = result()

        for g in range(GQA):
            hd = h * GQA + g
            o = (sig[:, 3 * hd:3 * hd + 1] * ocmp_ref[:, hsl(hd)]
                 + sig[:, 3 * hd + 1:3 * hd + 2] * o_sel[g]
                 + sig[:, 3 * hd + 2:3 * hd + 3] * o_win[g])
            y = o * lax.rsqrt(jnp.mean(o * o, axis=-1, keepdims=True) + EPS)
            o_ref[:, hsl(hd)] = (y * gnw_ref[:, hsl(hd)]).astype(o_ref.dtype)


def sel_win(qkv, selb, ksp, vstp, kwp, vwtp, e2p, tab, ocmp, gl, gnw, B, S, tq=TQ_SEL):
    T = qkv.shape[0]
    n_qb = S // tq
    dq = N_Q_HEADS * HEAD_DIM
    dkv = N_KV_HEADS * HEAD_DIM
    SP = S + WINDOW
    kern = functools.partial(_sel_win_kernel, n_qb=n_qb, scale=HEAD_DIM ** -0.5 * LOG2E)
    k_spec = pl.BlockSpec((SP, dkv), lambda i: (i // n_qb, 0))
    vt_spec = pl.BlockSpec((dkv, SP), lambda i: (i // n_qb, 0))
    return pl.pallas_call(
        kern,
        grid=(T // tq,),
        in_specs=[pl.BlockSpec((tq, dq), lambda i: (i, 0)),
                  pl.BlockSpec((tq, dkv), lambda i: (i, 0)),
                  k_spec, vt_spec, k_spec, vt_spec,
                  pl.BlockSpec((SP, HEAD_DIM), lambda i: (0, 0)),
                  pl.BlockSpec(tab.shape, lambda i: (0, 0)),
                  pl.BlockSpec((tq, dq), lambda i: (i, 0)),
                  pl.BlockSpec((tq, HEAD_DIM), lambda i: (i, 0)),
                  pl.BlockSpec((1, dq), lambda i: (0, 0))],
        out_specs=pl.BlockSpec((tq, dq), lambda i: (i, 0)),
        out_shape=jax.ShapeDtypeStruct((T, dq), BF16),
        scratch_shapes=[pltpu.VMEM((1, GQA * tq), F32), pltpu.VMEM((1, GQA * tq), F32),
                        pltpu.VMEM((HEAD_DIM, GQA * tq), F32),
                        pltpu.VMEM((WINDOW, GQA * tq), F32), pltpu.VMEM((WINDOW, GQA * tq), F32)],
        compiler_params=_params(("parallel",)),
    )(qkv, selb, ksp, vstp, kwp, vwtp, e2p, tab, ocmp, gl, gnw)


def _conv_kernel(b_ref, c_ref, h_ref, cp_ref, hp_ref, cw_ref, cb_ref, gnw_ref, o_ref, u_ref, *, tiles_per_seq):
    tm = b_ref.shape[0]
    first = (pl.program_id(0) % tiles_per_seq) == 0
    u_prev = cp_ref[...] * hp_ref[...]
    u_ref[0:8, :] = jnp.where(first, 0.0, u_prev)
    u = c_ref[...] * h_ref[...]
    u_ref[8:8 + tm, :] = u
    y = (cw_ref[0:1, :] * u_ref[6:6 + tm, :] + cw_ref[1:2, :] * u_ref[7:7 + tm, :]
         + cw_ref[2:3, :] * u + cb_ref[...])
    o = b_ref[...] * y
    n_groups = o.shape[1] // HEAD_DIM
    for g in range(n_groups):
        sl = slice(g * HEAD_DIM, (g + 1) * HEAD_DIM)
        og = o[:, sl]
        yg = og * lax.rsqrt(jnp.mean(og * og, axis=-1, keepdims=True) + EPS)
        o_ref[:, sl] = (yg * gnw_ref[:, sl]).astype(o_ref.dtype)


def conv_mixer(gc, conv_w, conv_b, gnw, S, tm):
    T = gc.shape[0]
    dc = conv_w.shape[1]
    tps = S // tm
    kern = functools.partial(_conv_kernel, tiles_per_seq=tps)
    prev = lambda col: pl.BlockSpec((8, dc), lambda i, col=col: (jnp.maximum(i * (tm // 8) - 1, 0), col))
    cur = lambda col: pl.BlockSpec((tm, dc), lambda i, col=col: (i, col))
    return pl.pallas_call(
        kern,
        grid=(T // tm,),
        in_specs=[cur(0), cur(1), cur(2), prev(1), prev(2),
                  pl.BlockSpec((8, dc), lambda i: (0, 0)),
                  pl.BlockSpec((1, dc), lambda i: (0, 0)),
                  pl.BlockSpec((1, dc), lambda i: (0, 0))],
        out_specs=pl.BlockSpec((tm, dc), lambda i: (i, 0)),
        out_shape=jax.ShapeDtypeStruct((T, dc), BF16),
        scratch_shapes=[pltpu.VMEM((tm + 8, dc), F32)],
        compiler_params=_params(("parallel",)),
    )(gc, gc, gc, gc, gc, conv_w, conv_b, gnw)


def _out_proj_query_kernel(ma_ref, mc_ref, wa_ref, wc_ref, x_ref, nw_ref, wq_ref, h_ref, q_ref):
    h = x_ref[...] + _dot(ma_ref[...], wa_ref[...]) + _dot(mc_ref[...], wc_ref[...])
    h_ref[...] = h
    y = h * lax.rsqrt(jnp.mean(h * h, axis=-1, keepdims=True) + EPS)
    q_ref[...] = _dot((y * nw_ref[...]).astype(BF16), wq_ref[...]).astype(q_ref.dtype)


def out_proj_query(ma, mc, w, x, norm_w, wq, tm):
    T, da = ma.shape
    dc = mc.shape[1]
    D = x.shape[1]
    assert da == dc and w.shape == (da + dc, D) and wq.shape[0] == D
    resident = pl.Buffered(1)
    row = lambda i: (i, 0)
    return pl.pallas_call(
        _out_proj_query_kernel,
        grid=(T // tm,),
        in_specs=[pl.BlockSpec((tm, da), row),
                  pl.BlockSpec((tm, dc), row),
                  pl.BlockSpec((da, D), lambda i: (0, 0), pipeline_mode=resident),
                  pl.BlockSpec((dc, D), lambda i: (1, 0), pipeline_mode=resident),
                  pl.BlockSpec((tm, D), row),
                  pl.BlockSpec((1, D), lambda i: (0, 0)),
                  pl.BlockSpec(wq.shape, lambda i: (0, 0), pipeline_mode=resident)],
        out_specs=[pl.BlockSpec((tm, D), row), pl.BlockSpec((tm, wq.shape[1]), row)],
        out_shape=[jax.ShapeDtypeStruct((T, D), F32), jax.ShapeDtypeStruct((T, wq.shape[1]), BF16)],
        compiler_params=_params(("parallel",)),
    )(ma, mc, w, w, x, norm_w.reshape(1, D), wq)


def _sort_network(n):
    pairs = []
    p = 1
    while p < n:
        k = p
        while k >= 1:
            for j in range(k % p, n - k, 2 * k):
                for i in range(min(k, n - j - k)):
                    if (i + j) // (2 * p) == (i + j + k) // (2 * p):
                        pairs.append((i + j, i + j + k))
            k //= 2
        p *= 2
    return pairs


def _peer_route_kernel(q_ref, sk_ref, u_ref, v_ref, e_ref, g_ref, ub_ref, vb_ref, sv_ref, si_ref):
    ub_ref[...] = u_ref[...].astype(BF16)
    vb_ref[...] = v_ref[...].astype(BF16)
    tt = q_ref.shape[0]
    K = PEER_TOPK
    sub = SUBLANES
    n_io = lax.broadcasted_iota(jnp.int32, (N_KEYS, tt), 0).astype(F32)
    r8 = lax.broadcasted_iota(jnp.int32, (sub, tt), 0).astype(F32)
    ninf = -jnp.inf
    lens = [K // (a + 1) for a in range(N_FIXED_A)]
    lens += [max(K // (b + 1) - N_FIXED_A, 0) for b in range(sub - N_FIXED_A)]
    assert sum(lens) == sum(K // (a + 1) for a in range(K)) and lens[-1] == 0
    fixed_a = r8 < float(N_FIXED_A)
    list_len = jnp.zeros((sub, tt), F32)
    for row, n in enumerate(lens):
        list_len = jnp.where(r8 == float(row), float(n), list_len)

    for h in range(PEER_HEADS):
        for c in range(2):
            col = (h * 2 + c) * HEAD_DIM
            s = _dot_nt(sk_ref[h * 2 + c], q_ref[:, col:col + HEAD_DIM])
            n_col = N_KEYS // sub
            vals = [s[j * sub:(j + 1) * sub, :] for j in range(n_col)]
            idxs = [n_io[j * sub:(j + 1) * sub, :] for j in range(n_col)]
            for lo, hi in _sort_network(n_col):
                swap = (vals[hi] > vals[lo]) | ((vals[hi] == vals[lo]) & (idxs[hi] < idxs[lo]))
                vals[lo], vals[hi] = jnp.where(swap, vals[hi], vals[lo]), jnp.where(swap, vals[lo], vals[hi])
                idxs[lo], idxs[hi] = jnp.where(swap, idxs[hi], idxs[lo]), jnp.where(swap, idxs[lo], idxs[hi])
            for k in range(K):
                m = jnp.max(vals[0], axis=0, keepdims=True)
                idx = jnp.min(jnp.where(vals[0] == m, idxs[0], float(N_KEYS)), axis=0, keepdims=True)
                sv_ref[c, k:k + 1, :] = m
                si_ref[c, k:k + 1, :] = idx
                win = idxs[0] == idx
                for j in range(min(n_col, K) - 1 - k):
                    vals[j] = jnp.where(win, vals[j + 1], vals[j])
                    idxs[j] = jnp.where(win, idxs[j + 1], idxs[j])
        sv0, sv1 = sv_ref[0], sv_ref[1]
        si0, si1 = si_ref[0], si_ref[1]
        sv1_low = pltpu.roll(sv1[0:sub, :], N_FIXED_A, 0)
        sv0_top = sv0[0:sub, :]
        lv, lf = [], []
        for dep in range(K):
            a_dep = min(N_FIXED_A + dep, K - 1)
            val = (jnp.where(fixed_a, sv0_top, sv0[a_dep:a_dep + 1, :])
                   + jnp.where(fixed_a, sv1[dep:dep + 1, :], sv1_low))
            lv.append(jnp.where(list_len > dep, val, ninf))
            lf.append(jnp.where(fixed_a, r8 * float(K) + dep, (N_FIXED_A + dep) * float(K) + r8 - N_FIXED_A))
        cvs, fls = [], []
        for k in range(K):
            m = jnp.max(lv[0], axis=0, keepdims=True)
            fsel = jnp.min(jnp.where(lv[0] == m, lf[0], 1e9), axis=0, keepdims=True)
            win = lf[0] == fsel
            cvs.append(m)
            fls.append(fsel)
            for dep in range(K - 1 - k):
                lv[dep] = jnp.where(win, lv[dep + 1], lv[dep])
                lf[dep] = jnp.where(win, lf[dep + 1], lf[dep])
        cv = jnp.concatenate(cvs, axis=0)
        fl = jnp.concatenate(fls, axis=0)
        a_sel = jnp.floor(fl * (1.0 / K))
        b_sel = fl - a_sel * K
        i1 = jnp.zeros_like(fl)
        i2 = jnp.zeros_like(fl)
        for r in range(K):
            i1 = jnp.where(a_sel == r, si0[r:r + 1, :], i1)
            i2 = jnp.where(b_sel == r, si1[r:r + 1, :], i2)
        ex = i1 * float(N_KEYS) + i2
        ev = jnp.exp(cv - jnp.max(cv, axis=0, keepdims=True))
        gates = ev / jnp.sum(ev, axis=0, keepdims=True)
        e_ref[0, h * K:(h + 1) * K, :] = ex.astype(jnp.int32)
        g_ref[0, h * K:(h + 1) * K, :] = gates


def peer_route(qp, subkeys, u, v):
    T = qp.shape[0]
    tt = TT_ROUTE
    P = PEER_HEADS * PEER_TOPK
    nt = T // tt
    E, D = u.shape
    slab = E // nt
    assert slab * nt == E and slab % 16 == 0
    tab_spec = pl.BlockSpec((slab, D), lambda i: (i, 0))
    return pl.pallas_call(
        _peer_route_kernel,
        grid=(nt,),
        in_specs=[pl.BlockSpec((tt, qp.shape[1]), lambda i: (i, 0)),
                  pl.BlockSpec(subkeys.shape, lambda i: (0, 0, 0)),
                  tab_spec, tab_spec],
        out_specs=[pl.BlockSpec((1, P, tt), lambda i: (i, 0, 0))] * 2 + [tab_spec, tab_spec],
        out_shape=[jax.ShapeDtypeStruct((nt, P, tt), jnp.int32),
                   jax.ShapeDtypeStruct((nt, P, tt), F32),
                   jax.ShapeDtypeStruct((E, D), BF16), jax.ShapeDtypeStruct((E, D), BF16)],
        scratch_shapes=[pltpu.VMEM((2, PEER_TOPK, tt), F32), pltpu.VMEM((2, PEER_TOPK, tt), F32)],
        compiler_params=_params(("parallel",)),
    )(qp, subkeys, u, v)


def _peer_gbuild_kernel(e_ref, g_ref, o_ref, i1_ref, i2_ref, gt_ref):
    tt = e_ref.shape[2]
    e = e_ref[0].T
    i1_ref[...] = e >> 7
    i2_ref[...] = e & (N_KEYS - 1)
    gt_ref[...] = g_ref[0].T
    P = e.shape[1]
    k_io = lax.broadcasted_iota(jnp.int32, (N_KEYS, P), 0)
    group = tt

    def body(tg, carry):
        base = pl.multiple_of(tg * group, group)
        r1s = i1_ref[pl.ds(base, group), :]
        r2s = i2_ref[pl.ds(base, group), :]
        rgs = gt_ref[pl.ds(base, group), :]
        for u in range(group):
            lhs = jnp.where(k_io == r1s[u:u + 1, :], rgs[u:u + 1, :], 0.0).astype(BF16)
            rhs = jnp.where(k_io == r2s[u:u + 1, :], 1.0, 0.0).astype(BF16)
            g_t = _dot_nt(lhs, rhs).astype(BF16).astype(F32)
            lo = pltpu.bitcast(g_t[:HALF_KEYS], jnp.uint32) >> 16
            hi = pltpu.bitcast(g_t[HALF_KEYS:], jnp.uint32)
            words = hi | lo
            for a in range(HALF_KEYS // PACK_ROWS):
                o_ref[a, base + u] = words[a * PACK_ROWS:(a + 1) * PACK_ROWS, :]
        return carry

    lax.fori_loop(0, tt // group, body, 0)


def peer_gbuild(ex, gates):
    nt, P, tt = ex.shape
    T = nt * tt
    n_a = HALF_KEYS // PACK_ROWS
    return pl.pallas_call(
        _peer_gbuild_kernel,
        grid=(nt,),
        in_specs=[pl.BlockSpec((1, P, tt), lambda i: (i, 0, 0))] * 2,
        out_specs=pl.BlockSpec((n_a, tt, PACK_ROWS, N_KEYS), lambda i: (0, i, 0, 0)),
        out_shape=jax.ShapeDtypeStruct((n_a, T, PACK_ROWS, N_KEYS), jnp.uint32),
        scratch_shapes=[pltpu.VMEM((tt, P), jnp.int32), pltpu.VMEM((tt, P), jnp.int32),
                        pltpu.VMEM((tt, P), F32)],
        compiler_params=_params(("parallel",)),
    )(ex, gates)


def _peer_dense_kernel(h_ref, nw_ref, fw_ref, u_ref, v_ref, gm_ref, o_ref, hn_ref, acc_ref):
    j = pl.program_id(1)
    tt = h_ref.shape[0]

    @pl.when(j == 0)
    def _():
        x = h_ref[...]
        y = x * lax.rsqrt(jnp.mean(x * x, axis=-1, keepdims=True) + EPS)
        hn_ref[...] = (y * nw_ref[...]).astype(BF16)
        acc_ref[...] = jnp.zeros(acc_ref.shape, F32)

    a = _dot_nt(hn_ref[...], u_ref[...])
    shift = (16 * (1 - j % 2)).astype(jnp.uint32)
    gm = jnp.concatenate(
        [pltpu.bitcast((gm_ref[pl.ds(k, tt, stride=PACK_ROWS), :] << shift) & jnp.uint32(0xFFFF0000), F32)
         for k in range(PACK_ROWS)], axis=1)
    w = (gm * jax.nn.gelu(a)).astype(BF16)
    acc_ref[...] += _dot(w, v_ref[...])

    @pl.when(j == pl.num_programs(1) - 1)
    def _():
        x = h_ref[...] + acc_ref[...]
        y = x * lax.rsqrt(jnp.mean(x * x, axis=-1, keepdims=True) + EPS)
        o_ref[...] = y * fw_ref[...]


def peer_dense(h, ffn_nw, final_nw, u, v, gm_words, tt):
    T, D = h.shape
    E = u.shape[0]
    ec = PACK_ROWS * N_KEYS
    n_a = gm_words.shape[0]
    assert E == 2 * n_a * ec
    gm2 = gm_words.reshape(n_a, T * PACK_ROWS, N_KEYS)
    tab_spec = pl.BlockSpec((ec, D), lambda i, j: ((j % 2) * n_a + j // 2, 0))
    return pl.pallas_call(
        _peer_dense_kernel,
        grid=(T // tt, E // ec),
        in_specs=[pl.BlockSpec((tt, D), lambda i, j: (i, 0)),
                  pl.BlockSpec((1, D), lambda i, j: (0, 0)),
                  pl.BlockSpec((1, D), lambda i, j: (0, 0)),
                  tab_spec, tab_spec,
                  pl.BlockSpec((None, tt * PACK_ROWS, N_KEYS), lambda i, j: (j // 2, i, 0))],
        out_specs=pl.BlockSpec((tt, D), lambda i, j: (i, 0)),
        out_shape=jax.ShapeDtypeStruct((T, D), F32),
        scratch_shapes=[pltpu.VMEM((tt, D), BF16), pltpu.VMEM((tt, D), F32)],
        compiler_params=_params(("parallel", "arbitrary")),
    )(h, ffn_nw.reshape(1, D), final_nw.reshape(1, D), u, v, gm2)


def _t5_bucket_np(n_dist):
    d = np.arange(n_dist)
    max_exact = NUM_BUCKETS // 2
    nf = np.maximum(d, 1).astype(np.float64)
    large = max_exact + (np.log(nf / max_exact) / math.log(MAX_DISTANCE / max_exact)
                         * (NUM_BUCKETS - max_exact)).astype(np.int64)
    large = np.minimum(large, NUM_BUCKETS - 1)
    return np.where(d < max_exact, d, large).astype(np.int32)


def _overlap_t_np(S):
    n_c = (S - L_CMP) // STRIDE_CMP + 1
    n_sel = S // L_SEL
    pos = np.arange(n_c)[:, None] * STRIDE_CMP + np.arange(L_CMP)[None, :]
    m = np.zeros((n_c + 1, n_sel), np.float32)
    np.add.at(m, (np.repeat(np.arange(n_c), L_CMP), (pos // L_SEL).reshape(-1)), 1.0 / L_CMP)
    return np.ascontiguousarray(m.T)


def _block_onehot_padded_np(S):
    assert S // L_SEL < HEAD_DIM
    e2 = np.zeros((WINDOW + S, HEAD_DIM), np.float32)
    e2[WINDOW + np.arange(S), np.arange(S) // L_SEL] = 1.0
    e2[:WINDOW, HEAD_DIM - 1] = SEL_OFF
    return e2


def nsa_conv_mix(xt, B, S, attn_norm_w, w_in, w_cmp_k, w_cmp_v, cmp_pos, conv_w, conv_b,
                 attn_gnw, conv_gnw, rel_bias, tm=TM_PROJ, tm_conv=TM_CONV, cast_also=()):
    T, D = xt.shape
    dq = N_Q_HEADS * HEAD_DIM
    dkv = N_KV_HEADS * HEAD_DIM
    n_attn = dq + 6 * dkv
    n_gate = 3 * N_Q_HEADS
    dc = (w_in.shape[1] - n_attn - n_gate) // 3
    w_t = w_in.T
    w_attn = cast_rows(w_t, 0, n_attn)
    w_conv = cast_rows(w_t, n_attn + n_gate, 3 * dc)
    w_gate = jnp.pad(w_t[n_attn:n_attn + n_gate], ((0, HEAD_DIM - n_gate), (0, 0))).astype(BF16)

    qkv, gc, gl = in_proj(xt, attn_norm_w, w_attn, w_conv, w_gate, tm)

    n16 = S // STRIDE_CMP
    k16 = qkv[:, dq:dq + dkv].reshape(B * n16, STRIDE_CMP * dkv)
    v16 = qkv[:, dq + dkv:dq + 2 * dkv].reshape(B * n16, STRIDE_CMP * dkv)

    def wflat(w, lo):
        return w[lo:lo + STRIDE_CMP].reshape(STRIDE_CMP * HEAD_DIM, HEAD_DIM).astype(BF16)

    def posrow(lo):
        return cmp_pos[lo:lo + STRIDE_CMP].reshape(1, STRIDE_CMP * HEAD_DIM)

    kc, vc = compress(k16, v16, posrow(0), posrow(STRIDE_CMP),
                      wflat(w_cmp_k, 0), wflat(w_cmp_k, STRIDE_CMP),
                      wflat(w_cmp_v, 0), wflat(w_cmp_v, STRIDE_CMP), B)

    tab = rel_bias[_t5_bucket_np(HEAD_DIM)].T
    ovt = jnp.asarray(_overlap_t_np(S), BF16)
    e2p = jnp.asarray(_block_onehot_padded_np(S), BF16)

    ocmp, selb, *casts = cmp_select(qkv, kc, vc, tab, ovt, B, S, cast_also=cast_also)
    ksp, vstp, kwp, vwtp = repack_kv(qkv, B, S, first_col_block=dq // dkv + 2)
    attn = sel_win(qkv, selb, ksp, vstp, kwp, vwtp, e2p, tab, ocmp, gl, attn_gnw.reshape(1, dq), B, S)
    cw8 = jnp.pad(conv_w, ((0, 8 - conv_w.shape[0]), (0, 0)))
    conv = conv_mixer(gc, cw8, conv_b.reshape(1, dc), conv_gnw.reshape(1, dc), S, tm_conv)
    return (attn, conv, *casts)


def peer_block(h, qp, ffn_nw, final_nw, peer_subkeys, peer_u, peer_v, tt=TT_DENSE):
    sk = peer_subkeys.reshape(PEER_HEADS * 2, N_KEYS, peer_subkeys.shape[-1]).astype(BF16)
    ex, gates, u_bf, v_bf = peer_route(qp, sk, peer_u, peer_v)
    gm = peer_gbuild(ex, gates)
    return peer_dense(h, ffn_nw, final_nw, u_bf, v_bf, gm, tt)


def kernel(x, attn_norm_w, w_in, w_cmp_k, w_cmp_v, cmp_pos, conv_w, conv_b, attn_group_norm_w,
           conv_group_norm_w, w_out, rel_bias, ffn_norm_w, peer_wq, peer_subkeys, peer_u, peer_v,
           final_norm_w):
    B, S, D = x.shape
    T = B * S
    xt = x.reshape(T, D)
    attn, conv, wo, wq = nsa_conv_mix(xt, B, S, attn_norm_w[0], w_in[0], w_cmp_k[0], w_cmp_v[0], cmp_pos[0],
                                      conv_w[0], conv_b[0], attn_group_norm_w[0], conv_group_norm_w[0], rel_bias,
                                      cast_also=(w_out[0], peer_wq[0]))
    h, qp = out_proj_query(attn, conv, wo, xt, ffn_norm_w[0], wq, TM_OUT)
    out = peer_block(h, qp, ffn_norm_w[0], final_norm_w, peer_subkeys[0], peer_u[0], peer_v[0])
    return out.reshape(B, S, D)
```

```python
import functools
import math

import jax
import jax.numpy as jnp
import numpy as np
from jax import lax
from jax.experimental import pallas as pl
from jax.experimental.pallas import tpu as pltpu

F32 = jnp.float32
BF16 = jnp.bfloat16

HEAD_DIM = 128
N_KV_HEADS = 2
GQA = 4
N_Q_HEADS = N_KV_HEADS * GQA
L_CMP = 32
STRIDE_CMP = 16
L_SEL = 64
N_SEL = 16
WINDOW = 512
Q_BLOCK = 128
FORCED_SCORE = float(GQA + 1)
NUM_BUCKETS = 32
MAX_DISTANCE = 128
PEER_HEADS = 8
N_KEYS = 128
PEER_TOPK = 16
EPS = 1e-6
NEG_BIG = -1e30
SEL_OFF = -float(2 ** 30)
LOG2E = math.log2(math.e)
N_FIXED_A = 4

SUBLANES = 8
LANES = 128
MXU_DIM = 256
V7X_VMEM_BYTES = 64 * 1024 * 1024
VMEM_LIMIT = V7X_VMEM_BYTES * 7 // 8

HALF_KEYS = N_KEYS // 2
PACK_ROWS = SUBLANES

TM_PROJ = MXU_DIM
TM_OUT = 2 * MXU_DIM
TM_CONV = 2 * MXU_DIM
TQ_CMP = 4 * Q_BLOCK
TQ_SEL = 2 * Q_BLOCK
TT_ROUTE = LANES
TT_DENSE = 2 * MXU_DIM
CAST_ROWS = MXU_DIM


def _dot(a, b):
    return jnp.dot(a, b, preferred_element_type=F32)


def _dot_nt(a, b):
    return lax.dot_general(a, b, (((1,), (1,)), ((), ())), preferred_element_type=F32)


def _params(sem, vmem=VMEM_LIMIT):
    return pltpu.CompilerParams(dimension_semantics=sem, vmem_limit_bytes=vmem)


def _in_proj_kernel(x_ref, nw_ref, wa_ref, wc_ref, wg_ref, oa_ref, oc_ref, og_ref):
    x = x_ref[...]
    y = x * lax.rsqrt(jnp.mean(x * x, axis=-1, keepdims=True) + EPS)
    xn = (y * nw_ref[...]).astype(BF16)
    oa_ref[...] = _dot_nt(xn, wa_ref[...]).astype(oa_ref.dtype)
    oc_ref[...] = _dot_nt(xn, wc_ref[...])
    og_ref[...] = _dot_nt(xn, wg_ref[...])


def in_proj(x, norm_w, w_attn, w_conv, w_gate, tm):
    T, D = x.shape
    resident = pl.Buffered(1)
    row = lambda i: (i, 0)
    fixed = lambda i: (0, 0)
    widths = [w.shape[0] for w in (w_attn, w_conv, w_gate)]
    return pl.pallas_call(
        _in_proj_kernel,
        grid=(T // tm,),
        in_specs=[pl.BlockSpec((tm, D), row), pl.BlockSpec((1, D), fixed)]
                 + [pl.BlockSpec(w.shape, fixed, pipeline_mode=resident) for w in (w_attn, w_conv, w_gate)],
        out_specs=[pl.BlockSpec((tm, n), row) for n in widths],
        out_shape=[jax.ShapeDtypeStruct((T, widths[0]), BF16), jax.ShapeDtypeStruct((T, widths[1]), F32),
                   jax.ShapeDtypeStruct((T, widths[2]), F32)],
        compiler_params=_params(("parallel",)),
    )(x, norm_w.reshape(1, D), w_attn, w_conv, w_gate)


def _cast_rows_kernel(w_ref, o_ref):
    o_ref[...] = w_ref[...].astype(o_ref.dtype)


def cast_rows(w, row0, n_rows, tr=CAST_ROWS):
    K = w.shape[1]
    assert row0 % SUBLANES == 0 and n_rows % tr == 0
    return pl.pallas_call(
        _cast_rows_kernel,
        grid=(n_rows // tr,),
        in_specs=[pl.BlockSpec((pl.Element(tr), pl.Element(K)), lambda i: (pl.multiple_of(row0 + i * tr, SUBLANES), 0))],
        out_specs=pl.BlockSpec((tr, K), lambda i: (i, 0)),
        out_shape=jax.ShapeDtypeStruct((n_rows, K), BF16),
        compiler_params=_params(("parallel",)),
    )(w)


def _compress_kernel(k_ref, v_ref, pa_ref, pb_ref, wka_ref, wkb_ref, wva_ref, wvb_ref, kc_ref, vc_ref):
    dkv = kc_ref.shape[1]

    def one(x_ref, wa_ref, wb_ref, o_ref):
        for h in range(N_KV_HEADS):
            x = jnp.concatenate([x_ref[:, l * dkv + h * HEAD_DIM:l * dkv + (h + 1) * HEAD_DIM]
                                 for l in range(STRIDE_CMP)], axis=1).astype(F32)
            a = _dot((x + pa_ref[...]).astype(BF16), wa_ref[...])
            b = _dot((x + pb_ref[...]).astype(BF16), wb_ref[...])
            n = b.shape[0]
            o_ref[:, h * HEAD_DIM:(h + 1) * HEAD_DIM] = (a + pltpu.roll(b, n - 1, 0)).astype(o_ref.dtype)

    one(k_ref, wka_ref, wkb_ref, kc_ref)
    one(v_ref, wva_ref, wvb_ref, vc_ref)


def compress(k16, v16, pos_a, pos_b, wka, wkb, wva, wvb, B):
    R, C = k16.shape
    nb = R // B
    dkv = C // STRIDE_CMP
    full = lambda shp: pl.BlockSpec(shp, lambda b: (0, 0))
    return pl.pallas_call(
        _compress_kernel,
        grid=(B,),
        in_specs=[pl.BlockSpec((nb, C), lambda b: (b, 0)),
                  pl.BlockSpec((nb, C), lambda b: (b, 0)),
                  full(pos_a.shape), full(pos_b.shape),
                  full(wka.shape), full(wkb.shape), full(wva.shape), full(wvb.shape)],
        out_specs=[pl.BlockSpec((nb, dkv), lambda b: (b, 0))] * 2,
        out_shape=[jax.ShapeDtypeStruct((R, dkv), BF16)] * 2,
        compiler_params=_params(("parallel",)),
    )(k16, v16, pos_a, pos_b, wka, wkb, wva, wvb)


def _bias_from_dist(tab_row, dist):
    idx = jnp.clip(dist, 0, 127)
    w = tab_row.shape[1]
    tab = jnp.broadcast_to(tab_row, (idx.shape[0], w))
    parts = [jnp.take_along_axis(tab, idx[:, k:k + w], axis=1) for k in range(0, idx.shape[1], w)]
    return parts[0] if len(parts) == 1 else jnp.concatenate(parts, axis=1)


def _cmp_select_kernel(q_ref, kc_ref, vc_ref, tab_ref, ovt_ref, *rest, n_qb, scale, n_cast):
    cast_in, (ocmp_ref, selb_ref), cast_out = rest[:n_cast], rest[n_cast:n_cast + 2], rest[n_cast + 2:]
    for src_ref, dst_ref in zip(cast_in, cast_out):
        dst_ref[...] = src_ref[...].astype(BF16)
    c = pl.program_id(0) % n_qb
    tq = q_ref.shape[0]
    n_c = kc_ref.shape[0]
    n_sel = ovt_ref.shape[0]
    t0 = c * tq
    t_col = t0 + lax.broadcasted_iota(jnp.int32, (tq, n_c), 0)
    n_row = lax.broadcasted_iota(jnp.int32, (tq, n_c), 1)
    dist = t_col - (n_row * STRIDE_CMP + (L_CMP - 1))
    valid = dist >= 0

    j_io = lax.broadcasted_iota(jnp.int32, (n_sel, tq), 0)
    t_io = t0 + lax.broadcasted_iota(jnp.int32, (n_sel, tq), 1)
    blk_t = t_io // L_SEL
    forced = (j_io == 0) | (j_io == blk_t) | (j_io == blk_t - 1)
    causal_blk = j_io * L_SEL <= t_io

    for h in range(N_KV_HEADS):
        kc = kc_ref[:, h * HEAD_DIM:(h + 1) * HEAD_DIM]
        vc = vc_ref[:, h * HEAD_DIM:(h + 1) * HEAD_DIM]
        psum = jnp.zeros((tq, n_c), F32)
        for g in range(GQA):
            hd = h * GQA + g
            qh = q_ref[:, hd * HEAD_DIM:(hd + 1) * HEAD_DIM]
            bias = _bias_from_dist(tab_ref[hd:hd + 1, :], dist)
            s = _dot_nt(qh, kc) * scale + bias
            s = jnp.where(valid, s, NEG_BIG)
            m = jnp.max(s, axis=-1, keepdims=True)
            e = jnp.where(valid, jnp.exp(s - m), 0.0)
            d = jnp.sum(e, axis=-1, keepdims=True)
            p = e / jnp.where(d > 0, d, 1.0)
            ocmp_ref[:, hd * HEAD_DIM:(hd + 1) * HEAD_DIM] = _dot(p.astype(BF16), vc)
            psum = psum + p
        p_hi = psum.astype(BF16)
        p_lo = (psum - p_hi.astype(F32)).astype(BF16)
        ovt = ovt_ref[...]
        imp = _dot_nt(ovt, p_hi) + _dot_nt(ovt, p_lo)
        score = jnp.where(forced, FORCED_SCORE, jnp.where(causal_blk, imp, -1.0))
        rank = jnp.zeros((n_sel, tq), F32)
        for jp in range(n_sel):
            row = score[jp:jp + 1, :]
            rank = rank + jnp.where(j_io > jp, jnp.where(row >= score, 1.0, 0.0),
                                    jnp.where(row > score, 1.0, 0.0))
        selb = jnp.where(rank < float(N_SEL), 0.0, SEL_OFF)
        if n_sel < HEAD_DIM:
            selb = jnp.concatenate([selb, jnp.zeros((HEAD_DIM - n_sel, tq), F32)], axis=0)
        selb_ref[:, h * HEAD_DIM:(h + 1) * HEAD_DIM] = selb.T.astype(BF16)


def cmp_select(qkv, kc, vc, tab, ovt, B, S, tq=TQ_CMP, cast_also=()):
    T = qkv.shape[0]
    n_qb = S // tq
    steps = T // tq
    n_c = kc.shape[0] // B
    dq = N_Q_HEADS * HEAD_DIM
    dkv = N_KV_HEADS * HEAD_DIM
    kern = functools.partial(_cmp_select_kernel, n_qb=n_qb, scale=HEAD_DIM ** -0.5, n_cast=len(cast_also))
    cast_specs = [pl.BlockSpec((w.shape[0] // steps, w.shape[1]), lambda i: (i, 0)) for w in cast_also]
    assert all(w.shape[0] % (2 * SUBLANES * steps) == 0 for w in cast_also)
    return pl.pallas_call(
        kern,
        grid=(steps,),
        in_specs=[pl.BlockSpec((tq, dq), lambda i: (i, 0)),
                  pl.BlockSpec((n_c, dkv), lambda i: (i // n_qb, 0)),
                  pl.BlockSpec((n_c, dkv), lambda i: (i // n_qb, 0)),
                  pl.BlockSpec(tab.shape, lambda i: (0, 0)),
                  pl.BlockSpec(ovt.shape, lambda i: (0, 0))] + cast_specs,
        out_specs=[pl.BlockSpec((tq, dq), lambda i: (i, 0)),
                   pl.BlockSpec((tq, dkv), lambda i: (i, 0))] + cast_specs,
        out_shape=[jax.ShapeDtypeStruct((T, dq), F32),
                   jax.ShapeDtypeStruct((T, dkv), BF16)] + [jax.ShapeDtypeStruct(w.shape, BF16) for w in cast_also],
        compiler_params=_params(("parallel",)),
    )(qkv, kc, vc, tab, ovt, *cast_also)


def _repack_kv_kernel(ks_ref, vs_ref, kw_ref, vw_ref, ksp_ref, vst_ref, kwp_ref, vwt_ref):
    first = pl.program_id(1) == 0

    @pl.when(first)
    def _():
        for ref in (ksp_ref, vst_ref, kwp_ref, vwt_ref):
            ref[...] = jnp.zeros(ref.shape, ref.dtype)

    @pl.when(jnp.logical_not(first))
    def _():
        ksp_ref[...] = ks_ref[...]
        kwp_ref[...] = kw_ref[...]
        vst_ref[...] = vs_ref[...].T
        vwt_ref[...] = vw_ref[...].T


def repack_kv(qkv, B, S, first_col_block):
    dkv = N_KV_HEADS * HEAD_DIM
    nt = S // WINDOW
    src = lambda col: pl.BlockSpec((WINDOW, dkv), lambda b, i, col=col: (b * nt + jnp.maximum(i - 1, 0), col))
    rows = pl.BlockSpec((WINDOW, dkv), lambda b, i: (b * (nt + 1) + i, 0))
    cols = pl.BlockSpec((dkv, WINDOW), lambda b, i: (b, i))
    return pl.pallas_call(
        _repack_kv_kernel,
        grid=(B, nt + 1),
        in_specs=[src(first_col_block + k) for k in range(4)],
        out_specs=[rows, cols, rows, cols],
        out_shape=[jax.ShapeDtypeStruct((B * (S + WINDOW), dkv), qkv.dtype),
                   jax.ShapeDtypeStruct((B * dkv, S + WINDOW), qkv.dtype)] * 2,
        compiler_params=_params(("parallel", "arbitrary")),
    )(qkv, qkv, qkv, qkv)


def _flash_update(s_raw, vt, bias, m_ref, l_ref, acc_ref, scale):
    m_old = m_ref[...]
    if bias.shape[0] == 1:
        s = s_raw * scale
        m_new = jnp.maximum(m_old, jnp.max(s, axis=0, keepdims=True) + bias)
        p = jnp.exp2(s - (m_new - bias))
    else:
        s = s_raw * scale + bias
        m_new = jnp.maximum(m_old, jnp.max(s, axis=0, keepdims=True))
        p = jnp.exp2(s - m_new)
    alpha = jnp.exp2(m_old - m_new)
    l_ref[...] = alpha * l_ref[...] + jnp.sum(p, axis=0, keepdims=True)
    acc_ref[...] = alpha * acc_ref[...] + _dot(vt, p.astype(BF16))
    m_ref[...] = m_new


def _flash_chunk(k, vt, q, bias, m_ref, l_ref, acc_ref, scale):
    _flash_update(_dot_nt(k, q), vt, bias, m_ref, l_ref, acc_ref, scale)


def _sel_win_kernel(q_ref, selb_ref, ks_ref, vst_ref, kw_ref, vwt_ref, e2_ref, tab_ref, ocmp_ref,
                    gl_ref, gnw_ref, o_ref, m_ref, l_ref, acc_ref, sa_ref, sb_ref, *, n_qb, scale):
    c = pl.program_id(0) % n_qb
    tq = q_ref.shape[0]
    rows = GQA * tq
    n_back = WINDOW // tq
    near = WINDOW + tq
    j_io = lax.broadcasted_iota(jnp.int32, (tq, tq), 0)
    i_io = lax.broadcasted_iota(jnp.int32, (tq, tq), 1)
    dij = i_io - j_io
    dij4 = jnp.concatenate([dij] * GQA, axis=1)
    causal = dij4 >= 0
    pad_col = jnp.where(lax.broadcasted_iota(jnp.int32, (rows, HEAD_DIM), 1) == HEAD_DIM - 1, 1.0, 0.0).astype(BF16)
    sig = jax.nn.sigmoid(gl_ref[...])
    near0 = pl.multiple_of(c * tq, tq)
    n_far = jnp.maximum(c - n_back, 0)
    n_full = n_far // n_back
    n_rem = n_far - n_full * n_back

    def hsl(h):
        return slice(h * HEAD_DIM, (h + 1) * HEAD_DIM)

    def far0(i):
        return pl.multiple_of(WINDOW + i * WINDOW, WINDOW)

    def reset():
        m_ref[...] = jnp.full(m_ref.shape, NEG_BIG, F32)
        l_ref[...] = jnp.zeros(l_ref.shape, F32)
        acc_ref[...] = jnp.zeros(acc_ref.shape, F32)

    def result():
        ot = acc_ref[...] / l_ref[...]
        return [ot[:, g * tq:(g + 1) * tq].T for g in range(GQA)]

    for h in range(N_KV_HEADS):
        q4 = jnp.concatenate([q_ref[:, (h * GQA + g) * HEAD_DIM:(h * GQA + g + 1) * HEAD_DIM]
                              for g in range(GQA)], axis=0)
        sb4 = jnp.concatenate([selb_ref[:, hsl(h)]] * GQA, axis=0)
        q_aug = jnp.concatenate([q4, sb4 + pad_col], axis=1)
        q_win = jnp.concatenate([q4, pad_col], axis=1)
        tabs = [tab_ref[h * GQA + g:h * GQA + g + 1, :] * LOG2E for g in range(GQA)]
        d0 = jnp.concatenate([_bias_from_dist(t, dij) for t in tabs], axis=1)
        d1 = jnp.concatenate([_bias_from_dist(t, dij + tq) for t in tabs], axis=1)
        far = jnp.concatenate([t[:, HEAD_DIM - 1:HEAD_DIM] + jnp.zeros((1, tq), F32) for t in tabs], axis=1)
        far_t = jnp.broadcast_to(far, (tq, rows))
        diag = jnp.where(causal, d0, NEG_BIG)
        bias_sel = jnp.concatenate([far_t] * (n_back - 1) + [d1, diag], axis=0)
        bias_win = jnp.concatenate([jnp.where(dij4 < 0, far_t, NEG_BIG)] + [far_t] * (n_back - 2) + [d1, diag],
                                   axis=0)

        reset()
        e2_near = e2_ref[pl.ds(near0, near), :]
        k_near = jnp.concatenate([ks_ref[pl.ds(near0, near), hsl(h)], e2_near], axis=1)
        _flash_chunk(k_near, vst_ref[hsl(h), pl.ds(near0, near)], q_aug, bias_sel, m_ref, l_ref, acc_ref, scale)

        def far_scores(i):
            r0 = far0(i)
            k = jnp.concatenate([ks_ref[pl.ds(r0, WINDOW), hsl(h)], e2_ref[pl.ds(r0, WINDOW), :]], axis=1)
            return _dot_nt(k, q_aug)

        def far_update(s_raw, i, bias):
            _flash_update(s_raw, vst_ref[hsl(h), pl.ds(far0(i), WINDOW)], bias, m_ref, l_ref, acc_ref, scale)

        sa_ref[...] = far_scores(0)

        def body(i, carry):
            sb_ref[...] = far_scores(2 * i + 1)
            far_update(sa_ref[...], 2 * i, far)
            sa_ref[...] = far_scores(2 * i + 2)
            far_update(sb_ref[...], 2 * i + 1, far)
            return carry

        lax.fori_loop(0, n_full // 2, body, 0)

        @pl.when(n_full % 2 == 1)
        def _():
            far_update(sa_ref[...], n_full - 1, far)

        @pl.when(n_rem > 0)
        def _():
            if n_back == 2:
                r0 = far0(n_full)
                k = jnp.concatenate([ks_ref[pl.ds(r0, tq), hsl(h)], e2_ref[pl.ds(r0, tq), :]], axis=1)
                _flash_chunk(k, vst_ref[hsl(h), pl.ds(r0, tq)], q_aug, far, m_ref, l_ref, acc_ref, scale)
            else:
                live = lax.broadcasted_iota(jnp.int32, (WINDOW, rows), 0) < n_rem * tq
                far_update(far_scores(n_full), n_full,
                           jnp.where(live, jnp.broadcast_to(far, (WINDOW, rows)), NEG_BIG))

        o_sel = result()

        reset()
        kw_near = jnp.concatenate([kw_ref[pl.ds(near0, near), hsl(h)], e2_near], axis=1)
        _flash_chunk(kw_near, vwt_ref[hsl(h), pl.ds(near0, near)], q_win, bias_win, m_ref, l_ref, acc_ref, scale)
        o_win = result()

        for g in range(GQA):
            hd = h * GQA + g
            o = (sig[:, 3 * hd:3 * hd + 1] * ocmp_ref[:, hsl(hd)]
                 + sig[:, 3 * hd + 1:3 * hd + 2] * o_sel[g]
                 + sig[:, 3 * hd + 2:3 * hd + 3] * o_win[g])
            y = o * lax.rsqrt(jnp.mean(o * o, axis=-1, keepdims=True) + EPS)
            o_ref[:, hsl(hd)] = (y * gnw_ref[:, hsl(hd)]).astype(o_ref.dtype)


def sel_win(qkv, selb, ksp, vstp, kwp, vwtp, e2p, tab, ocmp, gl, gnw, B, S, tq=TQ_SEL):
    T = qkv.shape[0]
    n_qb = S // tq
    dq = N_Q_HEADS * HEAD_DIM
    dkv = N_KV_HEADS * HEAD_DIM
    SP = S + WINDOW
    kern = functools.partial(_sel_win_kernel, n_qb=n_qb, scale=HEAD_DIM ** -0.5 * LOG2E)
    k_spec = pl.BlockSpec((SP, dkv), lambda i: (i // n_qb, 0))
    vt_spec = pl.BlockSpec((dkv, SP), lambda i: (i // n_qb, 0))
    return pl.pallas_call(
        kern,
        grid=(T // tq,),
        in_specs=[pl.BlockSpec((tq, dq), lambda i: (i, 0)),
                  pl.BlockSpec((tq, dkv), lambda i: (i, 0)),
                  k_spec, vt_spec, k_spec, vt_spec,
                  pl.BlockSpec((SP, HEAD_DIM), lambda i: (0, 0)),
                  pl.BlockSpec(tab.shape, lambda i: (0, 0)),
                  pl.BlockSpec((tq, dq), lambda i: (i, 0)),
                  pl.BlockSpec((tq, HEAD_DIM), lambda i: (i, 0)),
                  pl.BlockSpec((1, dq), lambda i: (0, 0))],
        out_specs=pl.BlockSpec((tq, dq), lambda i: (i, 0)),
        out_shape=jax.ShapeDtypeStruct((T, dq), BF16),
        scratch_shapes=[pltpu.VMEM((1, GQA * tq), F32), pltpu.VMEM((1, GQA * tq), F32),
                        pltpu.VMEM((HEAD_DIM, GQA * tq), F32),
                        pltpu.VMEM((WINDOW, GQA * tq), F32), pltpu.VMEM((WINDOW, GQA * tq), F32)],
        compiler_params=_params(("parallel",)),
    )(qkv, selb, ksp, vstp, kwp, vwtp, e2p, tab, ocmp, gl, gnw)


def _conv_kernel(b_ref, c_ref, h_ref, cp_ref, hp_ref, cw_ref, cb_ref, gnw_ref, o_ref, u_ref, *, tiles_per_seq):
    tm = b_ref.shape[0]
    first = (pl.program_id(0) % tiles_per_seq) == 0
    u_prev = cp_ref[...] * hp_ref[...]
    u_ref[0:8, :] = jnp.where(first, 0.0, u_prev)
    u = c_ref[...] * h_ref[...]
    u_ref[8:8 + tm, :] = u
    y = (cw_ref[0:1, :] * u_ref[6:6 + tm, :] + cw_ref[1:2, :] * u_ref[7:7 + tm, :]
         + cw_ref[2:3, :] * u + cb_ref[...])
    o = b_ref[...] * y
    n_groups = o.shape[1] // HEAD_DIM
    for g in range(n_groups):
        sl = slice(g * HEAD_DIM, (g + 1) * HEAD_DIM)
        og = o[:, sl]
        yg = og * lax.rsqrt(jnp.mean(og * og, axis=-1, keepdims=True) + EPS)
        o_ref[:, sl] = (yg * gnw_ref[:, sl]).astype(o_ref.dtype)


def conv_mixer(gc, conv_w, conv_b, gnw, S, tm):
    T = gc.shape[0]
    dc = conv_w.shape[1]
    tps = S // tm
    kern = functools.partial(_conv_kernel, tiles_per_seq=tps)
    prev = lambda col: pl.BlockSpec((8, dc), lambda i, col=col: (jnp.maximum(i * (tm // 8) - 1, 0), col))
    cur = lambda col: pl.BlockSpec((tm, dc), lambda i, col=col: (i, col))
    return pl.pallas_call(
        kern,
        grid=(T // tm,),
        in_specs=[cur(0), cur(1), cur(2), prev(1), prev(2),
                  pl.BlockSpec((8, dc), lambda i: (0, 0)),
                  pl.BlockSpec((1, dc), lambda i: (0, 0)),
                  pl.BlockSpec((1, dc), lambda i: (0, 0))],
        out_specs=pl.BlockSpec((tm, dc), lambda i: (i, 0)),
        out_shape=jax.ShapeDtypeStruct((T, dc), BF16),
        scratch_shapes=[pltpu.VMEM((tm + 8, dc), F32)],
        compiler_params=_params(("parallel",)),
    )(gc, gc, gc, gc, gc, conv_w, conv_b, gnw)


def _out_query_route_kernel(ma_ref, mc_ref, wa_ref, wc_ref, x_ref, nw_ref, wq_ref, sk_ref,
                            h_ref, e_ref, g_ref, q_ref, sv_ref, si_ref):
    @pl.when(pl.program_id(0) == 0)
    def _():
        q_ref[...] = jnp.zeros(q_ref.shape, q_ref.dtype)

    for t in range(q_ref.shape[0] // TT_ROUTE):
        _route_tokens(q_ref.at[pl.ds(t * TT_ROUTE, TT_ROUTE)], sk_ref, e_ref.at[pl.ds(t, 1)], g_ref.at[pl.ds(t, 1)],
                      sv_ref, si_ref)

    h = x_ref[...] + _dot(ma_ref[...], wa_ref[...]) + _dot(mc_ref[...], wc_ref[...])
    h_ref[...] = h
    y = h * lax.rsqrt(jnp.mean(h * h, axis=-1, keepdims=True) + EPS)
    q_ref[...] = _dot((y * nw_ref[...]).astype(BF16), wq_ref[...]).astype(q_ref.dtype)


def out_proj_query_route(ma, mc, w, x, norm_w, wq, subkeys, tm):
    T, da = ma.shape
    dc = mc.shape[1]
    D = x.shape[1]
    assert da == dc and w.shape == (da + dc, D) and wq.shape[0] == D and tm % TT_ROUTE == 0
    n = T // tm
    per = tm // TT_ROUTE
    P = PEER_HEADS * PEER_TOPK
    resident = pl.Buffered(1)
    cur = lambda i: (jnp.minimum(i, n - 1), 0)
    prev = lambda i: (jnp.maximum(i - 1, 0), 0, 0)
    return pl.pallas_call(
        _out_query_route_kernel,
        grid=(n + 1,),
        in_specs=[pl.BlockSpec((tm, da), cur),
                  pl.BlockSpec((tm, dc), cur),
                  pl.BlockSpec((da, D), lambda i: (0, 0), pipeline_mode=resident),
                  pl.BlockSpec((dc, D), lambda i: (1, 0), pipeline_mode=resident),
                  pl.BlockSpec((tm, D), cur),
                  pl.BlockSpec((1, D), lambda i: (0, 0)),
                  pl.BlockSpec(wq.shape, lambda i: (0, 0), pipeline_mode=resident),
                  pl.BlockSpec(subkeys.shape, lambda i: (0, 0, 0), pipeline_mode=resident)],
        out_specs=[pl.BlockSpec((tm, D), cur),
                   pl.BlockSpec((per, P, TT_ROUTE), prev), pl.BlockSpec((per, P, TT_ROUTE), prev)],
        out_shape=[jax.ShapeDtypeStruct((T, D), F32),
                   jax.ShapeDtypeStruct((T // TT_ROUTE, P, TT_ROUTE), jnp.int32),
                   jax.ShapeDtypeStruct((T // TT_ROUTE, P, TT_ROUTE), F32)],
        scratch_shapes=[pltpu.VMEM((tm, wq.shape[1]), BF16),
                        pltpu.VMEM((2, PEER_TOPK, TT_ROUTE), F32), pltpu.VMEM((2, PEER_TOPK, TT_ROUTE), F32)],
        compiler_params=_params(("arbitrary",)),
    )(ma, mc, w, w, x, norm_w.reshape(1, D), wq, subkeys)


def _sort_network(n):
    pairs = []
    p = 1
    while p < n:
        k = p
        while k >= 1:
            for j in range(k % p, n - k, 2 * k):
                for i in range(min(k, n - j - k)):
                    if (i + j) // (2 * p) == (i + j + k) // (2 * p):
                        pairs.append((i + j, i + j + k))
            k //= 2
        p *= 2
    return pairs


def _route_tokens(q_ref, sk_ref, e_ref, g_ref, sv_ref, si_ref):
    tt = q_ref.shape[0]
    K = PEER_TOPK
    sub = SUBLANES
    n_io = lax.broadcasted_iota(jnp.int32, (N_KEYS, tt), 0).astype(F32)
    r8 = lax.broadcasted_iota(jnp.int32, (sub, tt), 0).astype(F32)
    ninf = -jnp.inf
    lens = [K // (a + 1) for a in range(N_FIXED_A)]
    lens += [max(K // (b + 1) - N_FIXED_A, 0) for b in range(sub - N_FIXED_A)]
    assert sum(lens) == sum(K // (a + 1) for a in range(K)) and lens[-1] == 0
    fixed_a = r8 < float(N_FIXED_A)
    list_len = jnp.zeros((sub, tt), F32)
    for row, n in enumerate(lens):
        list_len = jnp.where(r8 == float(row), float(n), list_len)

    for h in range(PEER_HEADS):
        for c in range(2):
            col = (h * 2 + c) * HEAD_DIM
            s = _dot_nt(sk_ref[h * 2 + c], q_ref[:, col:col + HEAD_DIM])
            n_col = N_KEYS // sub
            vals = [s[j * sub:(j + 1) * sub, :] for j in range(n_col)]
            idxs = [n_io[j * sub:(j + 1) * sub, :] for j in range(n_col)]
            for lo, hi in _sort_network(n_col):
                swap = (vals[hi] > vals[lo]) | ((vals[hi] == vals[lo]) & (idxs[hi] < idxs[lo]))
                vals[lo], vals[hi] = jnp.where(swap, vals[hi], vals[lo]), jnp.where(swap, vals[lo], vals[hi])
                idxs[lo], idxs[hi] = jnp.where(swap, idxs[hi], idxs[lo]), jnp.where(swap, idxs[lo], idxs[hi])
            for k in range(K):
                m = jnp.max(vals[0], axis=0, keepdims=True)
                idx = jnp.min(jnp.where(vals[0] == m, idxs[0], float(N_KEYS)), axis=0, keepdims=True)
                sv_ref[c, k:k + 1, :] = m
                si_ref[c, k:k + 1, :] = idx
                win = idxs[0] == idx
                for j in range(min(n_col, K) - 1 - k):
                    vals[j] = jnp.where(win, vals[j + 1], vals[j])
                    idxs[j] = jnp.where(win, idxs[j + 1], idxs[j])
        sv0, sv1 = sv_ref[0], sv_ref[1]
        si0, si1 = si_ref[0], si_ref[1]
        sv1_low = pltpu.roll(sv1[0:sub, :], N_FIXED_A, 0)
        sv0_top = sv0[0:sub, :]
        lv, lf = [], []
        for dep in range(K):
            a_dep = min(N_FIXED_A + dep, K - 1)
            val = (jnp.where(fixed_a, sv0_top, sv0[a_dep:a_dep + 1, :])
                   + jnp.where(fixed_a, sv1[dep:dep + 1, :], sv1_low))
            lv.append(jnp.where(list_len > dep, val, ninf))
            lf.append(jnp.where(fixed_a, r8 * float(K) + dep, (N_FIXED_A + dep) * float(K) + r8 - N_FIXED_A))
        cvs, fls = [], []
        for k in range(K):
            m = jnp.max(lv[0], axis=0, keepdims=True)
            fsel = jnp.min(jnp.where(lv[0] == m, lf[0], 1e9), axis=0, keepdims=True)
            win = lf[0] == fsel
            cvs.append(m)
            fls.append(fsel)
            for dep in range(K - 1 - k):
                lv[dep] = jnp.where(win, lv[dep + 1], lv[dep])
                lf[dep] = jnp.where(win, lf[dep + 1], lf[dep])
        cv = jnp.concatenate(cvs, axis=0)
        fl = jnp.concatenate(fls, axis=0)
        a_sel = jnp.floor(fl * (1.0 / K))
        b_sel = fl - a_sel * K
        i1 = jnp.zeros_like(fl)
        i2 = jnp.zeros_like(fl)
        for r in range(K):
            i1 = jnp.where(a_sel == r, si0[r:r + 1, :], i1)
            i2 = jnp.where(b_sel == r, si1[r:r + 1, :], i2)
        ex = i1 * float(N_KEYS) + i2
        ev = jnp.exp(cv - jnp.max(cv, axis=0, keepdims=True))
        gates = ev / jnp.sum(ev, axis=0, keepdims=True)
        e_ref[0, h * K:(h + 1) * K, :] = ex.astype(jnp.int32)
        g_ref[0, h * K:(h + 1) * K, :] = gates


def _peer_gbuild_kernel(e_ref, g_ref, u_ref, v_ref, o_ref, ub_ref, vb_ref, i1_ref, i2_ref, gt_ref):
    ub_ref[...] = u_ref[...].astype(BF16)
    vb_ref[...] = v_ref[...].astype(BF16)
    tt = e_ref.shape[2]
    e = e_ref[0].T
    i1_ref[...] = e >> 7
    i2_ref[...] = e & (N_KEYS - 1)
    gt_ref[...] = g_ref[0].T
    P = e.shape[1]
    k_io = lax.broadcasted_iota(jnp.int32, (N_KEYS, P), 0)
    group = tt

    def body(tg, carry):
        base = pl.multiple_of(tg * group, group)
        r1s = i1_ref[pl.ds(base, group), :]
        r2s = i2_ref[pl.ds(base, group), :]
        rgs = gt_ref[pl.ds(base, group), :]
        for u in range(group):
            lhs = jnp.where(k_io == r1s[u:u + 1, :], rgs[u:u + 1, :], 0.0).astype(BF16)
            rhs = jnp.where(k_io == r2s[u:u + 1, :], 1.0, 0.0).astype(BF16)
            g_t = _dot_nt(lhs, rhs).astype(BF16).astype(F32)
            lo = pltpu.bitcast(g_t[:HALF_KEYS], jnp.uint32) >> 16
            hi = pltpu.bitcast(g_t[HALF_KEYS:], jnp.uint32)
            words = hi | lo
            for a in range(HALF_KEYS // PACK_ROWS):
                o_ref[a, base + u] = words[a * PACK_ROWS:(a + 1) * PACK_ROWS, :]
        return carry

    lax.fori_loop(0, tt // group, body, 0)


def peer_gbuild(ex, gates, u, v):
    nt, P, tt = ex.shape
    T = nt * tt
    n_a = HALF_KEYS // PACK_ROWS
    E, D = u.shape
    slab = E // nt
    assert slab * nt == E and slab % (2 * SUBLANES) == 0
    tab_spec = pl.BlockSpec((slab, D), lambda i: (i, 0))
    return pl.pallas_call(
        _peer_gbuild_kernel,
        grid=(nt,),
        in_specs=[pl.BlockSpec((1, P, tt), lambda i: (i, 0, 0))] * 2 + [tab_spec, tab_spec],
        out_specs=[pl.BlockSpec((n_a, tt, PACK_ROWS, N_KEYS), lambda i: (0, i, 0, 0)), tab_spec, tab_spec],
        out_shape=[jax.ShapeDtypeStruct((n_a, T, PACK_ROWS, N_KEYS), jnp.uint32),
                   jax.ShapeDtypeStruct((E, D), BF16), jax.ShapeDtypeStruct((E, D), BF16)],
        scratch_shapes=[pltpu.VMEM((tt, P), jnp.int32), pltpu.VMEM((tt, P), jnp.int32),
                        pltpu.VMEM((tt, P), F32)],
        compiler_params=_params(("parallel",)),
    )(ex, gates, u, v)


def _peer_dense_kernel(h_ref, nw_ref, fw_ref, u_ref, v_ref, gm_ref, o_ref, hn_ref, acc_ref):
    j = pl.program_id(1)
    tt = h_ref.shape[0]

    @pl.when(j == 0)
    def _():
        x = h_ref[...]
        y = x * lax.rsqrt(jnp.mean(x * x, axis=-1, keepdims=True) + EPS)
        hn_ref[...] = (y * nw_ref[...]).astype(BF16)
        acc_ref[...] = jnp.zeros(acc_ref.shape, F32)

    a = _dot_nt(hn_ref[...], u_ref[...])
    shift = (16 * (1 - j % 2)).astype(jnp.uint32)
    gm = jnp.concatenate(
        [pltpu.bitcast((gm_ref[pl.ds(k, tt, stride=PACK_ROWS), :] << shift) & jnp.uint32(0xFFFF0000), F32)
         for k in range(PACK_ROWS)], axis=1)
    w = (gm * jax.nn.gelu(a)).astype(BF16)
    acc_ref[...] += _dot(w, v_ref[...])

    @pl.when(j == pl.num_programs(1) - 1)
    def _():
        x = h_ref[...] + acc_ref[...]
        y = x * lax.rsqrt(jnp.mean(x * x, axis=-1, keepdims=True) + EPS)
        o_ref[...] = y * fw_ref[...]


def peer_dense(h, ffn_nw, final_nw, u, v, gm_words, tt):
    T, D = h.shape
    E = u.shape[0]
    ec = PACK_ROWS * N_KEYS
    n_a = gm_words.shape[0]
    assert E == 2 * n_a * ec
    gm2 = gm_words.reshape(n_a, T * PACK_ROWS, N_KEYS)
    tab_spec = pl.BlockSpec((ec, D), lambda i, j: ((j % 2) * n_a + j // 2, 0))
    return pl.pallas_call(
        _peer_dense_kernel,
        grid=(T // tt, E // ec),
        in_specs=[pl.BlockSpec((tt, D), lambda i, j: (i, 0)),
                  pl.BlockSpec((1, D), lambda i, j: (0, 0)),
                  pl.BlockSpec((1, D), lambda i, j: (0, 0)),
                  tab_spec, tab_spec,
                  pl.BlockSpec((None, tt * PACK_ROWS, N_KEYS), lambda i, j: (j // 2, i, 0))],
        out_specs=pl.BlockSpec((tt, D), lambda i, j: (i, 0)),
        out_shape=jax.ShapeDtypeStruct((T, D), F32),
        scratch_shapes=[pltpu.VMEM((tt, D), BF16), pltpu.VMEM((tt, D), F32)],
        compiler_params=_params(("parallel", "arbitrary")),
    )(h, ffn_nw.reshape(1, D), final_nw.reshape(1, D), u, v, gm2)


def _t5_bucket_np(n_dist):
    d = np.arange(n_dist)
    max_exact = NUM_BUCKETS // 2
    nf = np.maximum(d, 1).astype(np.float64)
    large = max_exact + (np.log(nf / max_exact) / math.log(MAX_DISTANCE / max_exact)
                         * (NUM_BUCKETS - max_exact)).astype(np.int64)
    large = np.minimum(large, NUM_BUCKETS - 1)
    return np.where(d < max_exact, d, large).astype(np.int32)


def _overlap_t_np(S):
    n_c = (S - L_CMP) // STRIDE_CMP + 1
    n_sel = S // L_SEL
    pos = np.arange(n_c)[:, None] * STRIDE_CMP + np.arange(L_CMP)[None, :]
    m = np.zeros((n_c + 1, n_sel), np.float32)
    np.add.at(m, (np.repeat(np.arange(n_c), L_CMP), (pos // L_SEL).reshape(-1)), 1.0 / L_CMP)
    return np.ascontiguousarray(m.T)


def _block_onehot_padded_np(S):
    assert S // L_SEL < HEAD_DIM
    e2 = np.zeros((WINDOW + S, HEAD_DIM), np.float32)
    e2[WINDOW + np.arange(S), np.arange(S) // L_SEL] = 1.0
    e2[:WINDOW, HEAD_DIM - 1] = SEL_OFF
    return e2


def nsa_conv_mix(xt, B, S, attn_norm_w, w_in, w_cmp_k, w_cmp_v, cmp_pos, conv_w, conv_b,
                 attn_gnw, conv_gnw, rel_bias, tm=TM_PROJ, tm_conv=TM_CONV, cast_also=()):
    T, D = xt.shape
    dq = N_Q_HEADS * HEAD_DIM
    dkv = N_KV_HEADS * HEAD_DIM
    n_attn = dq + 6 * dkv
    n_gate = 3 * N_Q_HEADS
    dc = (w_in.shape[1] - n_attn - n_gate) // 3
    w_t = w_in.T
    w_attn = cast_rows(w_t, 0, n_attn)
    w_conv = cast_rows(w_t, n_attn + n_gate, 3 * dc)
    w_gate = jnp.pad(w_t[n_attn:n_attn + n_gate], ((0, HEAD_DIM - n_gate), (0, 0))).astype(BF16)

    qkv, gc, gl = in_proj(xt, attn_norm_w, w_attn, w_conv, w_gate, tm)

    n16 = S // STRIDE_CMP
    k16 = qkv[:, dq:dq + dkv].reshape(B * n16, STRIDE_CMP * dkv)
    v16 = qkv[:, dq + dkv:dq + 2 * dkv].reshape(B * n16, STRIDE_CMP * dkv)

    def wflat(w, lo):
        return w[lo:lo + STRIDE_CMP].reshape(STRIDE_CMP * HEAD_DIM, HEAD_DIM).astype(BF16)

    def posrow(lo):
        return cmp_pos[lo:lo + STRIDE_CMP].reshape(1, STRIDE_CMP * HEAD_DIM)

    kc, vc = compress(k16, v16, posrow(0), posrow(STRIDE_CMP),
                      wflat(w_cmp_k, 0), wflat(w_cmp_k, STRIDE_CMP),
                      wflat(w_cmp_v, 0), wflat(w_cmp_v, STRIDE_CMP), B)

    tab = rel_bias[_t5_bucket_np(HEAD_DIM)].T
    ovt = jnp.asarray(_overlap_t_np(S), BF16)
    e2p = jnp.asarray(_block_onehot_padded_np(S), BF16)

    ocmp, selb, *casts = cmp_select(qkv, kc, vc, tab, ovt, B, S, cast_also=cast_also)
    ksp, vstp, kwp, vwtp = repack_kv(qkv, B, S, first_col_block=dq // dkv + 2)
    attn = sel_win(qkv, selb, ksp, vstp, kwp, vwtp, e2p, tab, ocmp, gl, attn_gnw.reshape(1, dq), B, S)
    cw8 = jnp.pad(conv_w, ((0, 8 - conv_w.shape[0]), (0, 0)))
    conv = conv_mixer(gc, cw8, conv_b.reshape(1, dc), conv_gnw.reshape(1, dc), S, tm_conv)
    return (attn, conv, *casts)


def peer_block(h, ex, gates, ffn_nw, final_nw, peer_u, peer_v, tt=TT_DENSE):
    gm, u_bf, v_bf = peer_gbuild(ex, gates, peer_u, peer_v)
    return peer_dense(h, ffn_nw, final_nw, u_bf, v_bf, gm, tt)


def kernel(x, attn_norm_w, w_in, w_cmp_k, w_cmp_v, cmp_pos, conv_w, conv_b, attn_group_norm_w,
           conv_group_norm_w, w_out, rel_bias, ffn_norm_w, peer_wq, peer_subkeys, peer_u, peer_v,
           final_norm_w):
    B, S, D = x.shape
    T = B * S
    xt = x.reshape(T, D)
    attn, conv, wo, wq = nsa_conv_mix(xt, B, S, attn_norm_w[0], w_in[0], w_cmp_k[0], w_cmp_v[0], cmp_pos[0],
                                      conv_w[0], conv_b[0], attn_group_norm_w[0], conv_group_norm_w[0], rel_bias,
                                      cast_also=(w_out[0], peer_wq[0]))
    sk = peer_subkeys[0].reshape(PEER_HEADS * 2, N_KEYS, peer_subkeys.shape[-1]).astype(BF16)
    h, ex, gates = out_proj_query_route(attn, conv, wo, xt, ffn_norm_w[0], wq, sk, TM_OUT)
    out = peer_block(h, ex, gates, ffn_norm_w[0], final_norm_w, peer_u[0], peer_v[0])
    return out.reshape(B, S, D)
```

```python
import functools
import math

import jax
import jax.numpy as jnp
import numpy as np
from jax import lax
from jax.experimental import pallas as pl
from jax.experimental.pallas import tpu as pltpu

F32 = jnp.float32
BF16 = jnp.bfloat16

HEAD_DIM = 128
N_KV_HEADS = 2
GQA = 4
N_Q_HEADS = N_KV_HEADS * GQA
L_CMP = 32
STRIDE_CMP = 16
L_SEL = 64
N_SEL = 16
WINDOW = 512
Q_BLOCK = 128
FORCED_SCORE = float(GQA + 1)
NUM_BUCKETS = 32
MAX_DISTANCE = 128
PEER_HEADS = 8
N_KEYS = 128
PEER_TOPK = 16
EPS = 1e-6
NEG_BIG = -1e30
SEL_OFF = -float(2 ** 30)
LOG2E = math.log2(math.e)
N_FIXED_A = 4

SUBLANES = 8
LANES = 128
MXU_DIM = 256
V7X_VMEM_BYTES = 64 * 1024 * 1024
VMEM_LIMIT = V7X_VMEM_BYTES * 7 // 8

HALF_KEYS = N_KEYS // 2
PACK_ROWS = SUBLANES

TM_PROJ = MXU_DIM
TM_OUT = 2 * MXU_DIM
TM_CONV = 2 * MXU_DIM
TQ_CMP = 4 * Q_BLOCK
TQ_SEL = 2 * Q_BLOCK
TT_ROUTE = LANES
TT_DENSE = 2 * MXU_DIM
CAST_ROWS = MXU_DIM


def _dot(a, b):
    return jnp.dot(a, b, preferred_element_type=F32)


def _dot_nt(a, b):
    return lax.dot_general(a, b, (((1,), (1,)), ((), ())), preferred_element_type=F32)


def _params(sem, vmem=VMEM_LIMIT):
    return pltpu.CompilerParams(dimension_semantics=sem, vmem_limit_bytes=vmem)


def _in_proj_kernel(x_ref, nw_ref, wa_ref, wc_ref, wg_ref, oa_ref, oc_ref, og_ref):
    x = x_ref[...]
    y = x * lax.rsqrt(jnp.mean(x * x, axis=-1, keepdims=True) + EPS)
    xn = (y * nw_ref[...]).astype(BF16)
    oa_ref[...] = _dot_nt(xn, wa_ref[...]).astype(oa_ref.dtype)
    oc_ref[...] = _dot_nt(xn, wc_ref[...])
    og_ref[...] = _dot_nt(xn, wg_ref[...])


def in_proj(x, norm_w, w_attn, w_conv, w_gate, tm):
    T, D = x.shape
    resident = pl.Buffered(1)
    row = lambda i: (i, 0)
    fixed = lambda i: (0, 0)
    widths = [w.shape[0] for w in (w_attn, w_conv, w_gate)]
    return pl.pallas_call(
        _in_proj_kernel,
        grid=(T // tm,),
        in_specs=[pl.BlockSpec((tm, D), row), pl.BlockSpec((1, D), fixed)]
                 + [pl.BlockSpec(w.shape, fixed, pipeline_mode=resident) for w in (w_attn, w_conv, w_gate)],
        out_specs=[pl.BlockSpec((tm, n), row) for n in widths],
        out_shape=[jax.ShapeDtypeStruct((T, widths[0]), BF16), jax.ShapeDtypeStruct((T, widths[1]), F32),
                   jax.ShapeDtypeStruct((T, widths[2]), F32)],
        compiler_params=_params(("parallel",)),
    )(x, norm_w.reshape(1, D), w_attn, w_conv, w_gate)


def _cast_rows_kernel(w_ref, o_ref):
    o_ref[...] = w_ref[...].astype(o_ref.dtype)


def cast_rows(w, row0, n_rows, tr=CAST_ROWS):
    K = w.shape[1]
    assert row0 % SUBLANES == 0 and n_rows % tr == 0
    return pl.pallas_call(
        _cast_rows_kernel,
        grid=(n_rows // tr,),
        in_specs=[pl.BlockSpec((pl.Element(tr), pl.Element(K)), lambda i: (pl.multiple_of(row0 + i * tr, SUBLANES), 0))],
        out_specs=pl.BlockSpec((tr, K), lambda i: (i, 0)),
        out_shape=jax.ShapeDtypeStruct((n_rows, K), BF16),
        compiler_params=_params(("parallel",)),
    )(w)


def _compress_kernel(k_ref, v_ref, pa_ref, pb_ref, wka_ref, wkb_ref, wva_ref, wvb_ref, kc_ref, vc_ref):
    dkv = kc_ref.shape[1]

    def one(x_ref, wa_ref, wb_ref, o_ref):
        for h in range(N_KV_HEADS):
            x = jnp.concatenate([x_ref[:, l * dkv + h * HEAD_DIM:l * dkv + (h + 1) * HEAD_DIM]
                                 for l in range(STRIDE_CMP)], axis=1).astype(F32)
            a = _dot((x + pa_ref[...]).astype(BF16), wa_ref[...])
            b = _dot((x + pb_ref[...]).astype(BF16), wb_ref[...])
            n = b.shape[0]
            o_ref[:, h * HEAD_DIM:(h + 1) * HEAD_DIM] = (a + pltpu.roll(b, n - 1, 0)).astype(o_ref.dtype)

    one(k_ref, wka_ref, wkb_ref, kc_ref)
    one(v_ref, wva_ref, wvb_ref, vc_ref)


def compress(k16, v16, pos_a, pos_b, wka, wkb, wva, wvb, B):
    R, C = k16.shape
    nb = R // B
    dkv = C // STRIDE_CMP
    full = lambda shp: pl.BlockSpec(shp, lambda b: (0, 0))
    return pl.pallas_call(
        _compress_kernel,
        grid=(B,),
        in_specs=[pl.BlockSpec((nb, C), lambda b: (b, 0)),
                  pl.BlockSpec((nb, C), lambda b: (b, 0)),
                  full(pos_a.shape), full(pos_b.shape),
                  full(wka.shape), full(wkb.shape), full(wva.shape), full(wvb.shape)],
        out_specs=[pl.BlockSpec((nb, dkv), lambda b: (b, 0))] * 2,
        out_shape=[jax.ShapeDtypeStruct((R, dkv), BF16)] * 2,
        compiler_params=_params(("parallel",)),
    )(k16, v16, pos_a, pos_b, wka, wkb, wva, wvb)


def _bias_from_dist(tab_row, dist):
    idx = jnp.clip(dist, 0, 127)
    w = tab_row.shape[1]
    tab = jnp.broadcast_to(tab_row, (idx.shape[0], w))
    parts = [jnp.take_along_axis(tab, idx[:, k:k + w], axis=1) for k in range(0, idx.shape[1], w)]
    return parts[0] if len(parts) == 1 else jnp.concatenate(parts, axis=1)


def _cmp_select_kernel(q_ref, kc_ref, vc_ref, tab_ref, ovt_ref, *rest, n_qb, scale, n_cast):
    cast_in, (ocmp_ref, selb_ref), cast_out = rest[:n_cast], rest[n_cast:n_cast + 2], rest[n_cast + 2:]
    for src_ref, dst_ref in zip(cast_in, cast_out):
        dst_ref[...] = src_ref[...].astype(BF16)
    c = pl.program_id(0) % n_qb
    tq = q_ref.shape[0]
    n_c = kc_ref.shape[0]
    n_sel = ovt_ref.shape[0]
    t0 = c * tq
    t_col = t0 + lax.broadcasted_iota(jnp.int32, (tq, n_c), 0)
    n_row = lax.broadcasted_iota(jnp.int32, (tq, n_c), 1)
    dist = t_col - (n_row * STRIDE_CMP + (L_CMP - 1))
    valid = dist >= 0

    j_io = lax.broadcasted_iota(jnp.int32, (n_sel, tq), 0)
    t_io = t0 + lax.broadcasted_iota(jnp.int32, (n_sel, tq), 1)
    blk_t = t_io // L_SEL
    forced = (j_io == 0) | (j_io == blk_t) | (j_io == blk_t - 1)
    causal_blk = j_io * L_SEL <= t_io

    for h in range(N_KV_HEADS):
        kc = kc_ref[:, h * HEAD_DIM:(h + 1) * HEAD_DIM]
        vc = vc_ref[:, h * HEAD_DIM:(h + 1) * HEAD_DIM]
        psum = jnp.zeros((tq, n_c), F32)
        for g in range(GQA):
            hd = h * GQA + g
            qh = q_ref[:, hd * HEAD_DIM:(hd + 1) * HEAD_DIM]
            bias = _bias_from_dist(tab_ref[hd:hd + 1, :], dist)
            s = _dot_nt(qh, kc) * scale + bias
            s = jnp.where(valid, s, NEG_BIG)
            m = jnp.max(s, axis=-1, keepdims=True)
            e = jnp.where(valid, jnp.exp(s - m), 0.0)
            d = jnp.sum(e, axis=-1, keepdims=True)
            p = e / jnp.where(d > 0, d, 1.0)
            ocmp_ref[:, hd * HEAD_DIM:(hd + 1) * HEAD_DIM] = _dot(p.astype(BF16), vc)
            psum = psum + p
        p_hi = psum.astype(BF16)
        p_lo = (psum - p_hi.astype(F32)).astype(BF16)
        ovt = ovt_ref[...]
        imp = _dot_nt(ovt, p_hi) + _dot_nt(ovt, p_lo)
        score = jnp.where(forced, FORCED_SCORE, jnp.where(causal_blk, imp, -1.0))
        rank = jnp.zeros((n_sel, tq), F32)
        for jp in range(n_sel):
            row = score[jp:jp + 1, :]
            rank = rank + jnp.where(j_io > jp, jnp.where(row >= score, 1.0, 0.0),
                                    jnp.where(row > score, 1.0, 0.0))
        selb = jnp.where(rank < float(N_SEL), 0.0, SEL_OFF)
        if n_sel < HEAD_DIM:
            selb = jnp.concatenate([selb, jnp.zeros((HEAD_DIM - n_sel, tq), F32)], axis=0)
        selb_ref[:, h * HEAD_DIM:(h + 1) * HEAD_DIM] = selb.T.astype(BF16)


def cmp_select(qkv, kc, vc, tab, ovt, B, S, tq=TQ_CMP, cast_also=()):
    T = qkv.shape[0]
    n_qb = S // tq
    steps = T // tq
    n_c = kc.shape[0] // B
    dq = N_Q_HEADS * HEAD_DIM
    dkv = N_KV_HEADS * HEAD_DIM
    kern = functools.partial(_cmp_select_kernel, n_qb=n_qb, scale=HEAD_DIM ** -0.5, n_cast=len(cast_also))
    cast_specs = [pl.BlockSpec((w.shape[0] // steps, w.shape[1]), lambda i: (i, 0)) for w in cast_also]
    assert all(w.shape[0] % (2 * SUBLANES * steps) == 0 for w in cast_also)
    return pl.pallas_call(
        kern,
        grid=(steps,),
        in_specs=[pl.BlockSpec((tq, dq), lambda i: (i, 0)),
                  pl.BlockSpec((n_c, dkv), lambda i: (i // n_qb, 0)),
                  pl.BlockSpec((n_c, dkv), lambda i: (i // n_qb, 0)),
                  pl.BlockSpec(tab.shape, lambda i: (0, 0)),
                  pl.BlockSpec(ovt.shape, lambda i: (0, 0))] + cast_specs,
        out_specs=[pl.BlockSpec((tq, dq), lambda i: (i, 0)),
                   pl.BlockSpec((tq, dkv), lambda i: (i, 0))] + cast_specs,
        out_shape=[jax.ShapeDtypeStruct((T, dq), F32),
                   jax.ShapeDtypeStruct((T, dkv), BF16)] + [jax.ShapeDtypeStruct(w.shape, BF16) for w in cast_also],
        compiler_params=_params(("parallel",)),
    )(qkv, kc, vc, tab, ovt, *cast_also)


def _repack_kv_kernel(ks_ref, vs_ref, kw_ref, vw_ref, ksp_ref, vst_ref, kwp_ref, vwt_ref):
    first = pl.program_id(1) == 0

    @pl.when(first)
    def _():
        for ref in (ksp_ref, vst_ref, kwp_ref, vwt_ref):
            ref[...] = jnp.zeros(ref.shape, ref.dtype)

    @pl.when(jnp.logical_not(first))
    def _():
        ksp_ref[...] = ks_ref[...]
        kwp_ref[...] = kw_ref[...]
        vst_ref[...] = vs_ref[...].T
        vwt_ref[...] = vw_ref[...].T


def repack_kv(qkv, B, S, first_col_block):
    dkv = N_KV_HEADS * HEAD_DIM
    nt = S // WINDOW
    src = lambda col: pl.BlockSpec((WINDOW, dkv), lambda b, i, col=col: (b * nt + jnp.maximum(i - 1, 0), col))
    rows = pl.BlockSpec((WINDOW, dkv), lambda b, i: (b * (nt + 1) + i, 0))
    cols = pl.BlockSpec((dkv, WINDOW), lambda b, i: (b, i))
    return pl.pallas_call(
        _repack_kv_kernel,
        grid=(B, nt + 1),
        in_specs=[src(first_col_block + k) for k in range(4)],
        out_specs=[rows, cols, rows, cols],
        out_shape=[jax.ShapeDtypeStruct((B * (S + WINDOW), dkv), qkv.dtype),
                   jax.ShapeDtypeStruct((B * dkv, S + WINDOW), qkv.dtype)] * 2,
        compiler_params=_params(("parallel", "arbitrary")),
    )(qkv, qkv, qkv, qkv)


def _flash_update(s_raw, vt, bias, m_ref, l_ref, acc_ref, scale):
    m_old = m_ref[...]
    if bias.shape[0] == 1:
        s = s_raw * scale
        m_new = jnp.maximum(m_old, jnp.max(s, axis=0, keepdims=True) + bias)
        p = jnp.exp2(s - (m_new - bias))
    else:
        s = s_raw * scale + bias
        m_new = jnp.maximum(m_old, jnp.max(s, axis=0, keepdims=True))
        p = jnp.exp2(s - m_new)
    alpha = jnp.exp2(m_old - m_new)
    l_ref[...] = alpha * l_ref[...] + jnp.sum(p, axis=0, keepdims=True)
    acc_ref[...] = alpha * acc_ref[...] + _dot(vt, p.astype(BF16))
    m_ref[...] = m_new


def _flash_chunk(k, vt, q, bias, m_ref, l_ref, acc_ref, scale):
    _flash_update(_dot_nt(k, q), vt, bias, m_ref, l_ref, acc_ref, scale)


def _sel_win_kernel(q_ref, selb_ref, ks_ref, vst_ref, kw_ref, vwt_ref, e2_ref, tab_ref, ocmp_ref,
                    gl_ref, gnw_ref, tab_in_ref, o_ref, tab_out_ref, m_ref, l_ref, acc_ref, sa_ref, sb_ref,
                    *, n_qb, scale):
    tab_out_ref[...] = tab_in_ref[...].astype(BF16)
    c = pl.program_id(0) % n_qb
    tq = q_ref.shape[0]
    rows = GQA * tq
    n_back = WINDOW // tq
    near = WINDOW + tq
    j_io = lax.broadcasted_iota(jnp.int32, (tq, tq), 0)
    i_io = lax.broadcasted_iota(jnp.int32, (tq, tq), 1)
    dij = i_io - j_io
    dij4 = jnp.concatenate([dij] * GQA, axis=1)
    causal = dij4 >= 0
    pad_col = jnp.where(lax.broadcasted_iota(jnp.int32, (rows, HEAD_DIM), 1) == HEAD_DIM - 1, 1.0, 0.0).astype(BF16)
    sig = jax.nn.sigmoid(gl_ref[...])
    near0 = pl.multiple_of(c * tq, tq)
    n_far = jnp.maximum(c - n_back, 0)
    n_full = n_far // n_back
    n_rem = n_far - n_full * n_back

    def hsl(h):
        return slice(h * HEAD_DIM, (h + 1) * HEAD_DIM)

    def far0(i):
        return pl.multiple_of(WINDOW + i * WINDOW, WINDOW)

    def reset():
        m_ref[...] = jnp.full(m_ref.shape, NEG_BIG, F32)
        l_ref[...] = jnp.zeros(l_ref.shape, F32)
        acc_ref[...] = jnp.zeros(acc_ref.shape, F32)

    def result():
        ot = acc_ref[...] / l_ref[...]
        return [ot[:, g * tq:(g + 1) * tq].T for g in range(GQA)]

    for h in range(N_KV_HEADS):
        q4 = jnp.concatenate([q_ref[:, (h * GQA + g) * HEAD_DIM:(h * GQA + g + 1) * HEAD_DIM]
                              for g in range(GQA)], axis=0)
        sb4 = jnp.concatenate([selb_ref[:, hsl(h)]] * GQA, axis=0)
        q_aug = jnp.concatenate([q4, sb4 + pad_col], axis=1)
        q_win = jnp.concatenate([q4, pad_col], axis=1)
        tabs = [tab_ref[h * GQA + g:h * GQA + g + 1, :] * LOG2E for g in range(GQA)]
        d0 = jnp.concatenate([_bias_from_dist(t, dij) for t in tabs], axis=1)
        d1 = jnp.concatenate([_bias_from_dist(t, dij + tq) for t in tabs], axis=1)
        far = jnp.concatenate([t[:, HEAD_DIM - 1:HEAD_DIM] + jnp.zeros((1, tq), F32) for t in tabs], axis=1)
        far_t = jnp.broadcast_to(far, (tq, rows))
        diag = jnp.where(causal, d0, NEG_BIG)
        bias_sel = jnp.concatenate([far_t] * (n_back - 1) + [d1, diag], axis=0)
        bias_win = jnp.concatenate([jnp.where(dij4 < 0, far_t, NEG_BIG)] + [far_t] * (n_back - 2) + [d1, diag],
                                   axis=0)

        reset()
        e2_near = e2_ref[pl.ds(near0, near), :]
        k_near = jnp.concatenate([ks_ref[pl.ds(near0, near), hsl(h)], e2_near], axis=1)
        _flash_chunk(k_near, vst_ref[hsl(h), pl.ds(near0, near)], q_aug, bias_sel, m_ref, l_ref, acc_ref, scale)

        def far_scores(i):
            r0 = far0(i)
            k = jnp.concatenate([ks_ref[pl.ds(r0, WINDOW), hsl(h)], e2_ref[pl.ds(r0, WINDOW), :]], axis=1)
            return _dot_nt(k, q_aug)

        def far_update(s_raw, i, bias):
            _flash_update(s_raw, vst_ref[hsl(h), pl.ds(far0(i), WINDOW)], bias, m_ref, l_ref, acc_ref, scale)

        sa_ref[...] = far_scores(0)

        def body(i, carry):
            sb_ref[...] = far_scores(2 * i + 1)
            far_update(sa_ref[...], 2 * i, far)
            sa_ref[...] = far_scores(2 * i + 2)
            far_update(sb_ref[...], 2 * i + 1, far)
            return carry

        lax.fori_loop(0, n_full // 2, body, 0)

        @pl.when(n_full % 2 == 1)
        def _():
            far_update(sa_ref[...], n_full - 1, far)

        @pl.when(n_rem > 0)
        def _():
            if n_back == 2:
                r0 = far0(n_full)
                k = jnp.concatenate([ks_ref[pl.ds(r0, tq), hsl(h)], e2_ref[pl.ds(r0, tq), :]], axis=1)
                _flash_chunk(k, vst_ref[hsl(h), pl.ds(r0, tq)], q_aug, far, m_ref, l_ref, acc_ref, scale)
            else:
                live = lax.broadcasted_iota(jnp.int32, (WINDOW, rows), 0) < n_rem * tq
                far_update(far_scores(n_full), n_full,
                           jnp.where(live, jnp.broadcast_to(far, (WINDOW, rows)), NEG_BIG))

        o_sel = result()

        reset()
        kw_near = jnp.concatenate([kw_ref[pl.ds(near0, near), hsl(h)], e2_near], axis=1)
        _flash_chunk(kw_near, vwt_ref[hsl(h), pl.ds(near0, near)], q_win, bias_win, m_ref, l_ref, acc_ref, scale)
        o_win = result()

        for g in range(GQA):
            hd = h * GQA + g
            o = (sig[:, 3 * hd:3 * hd + 1] * ocmp_ref[:, hsl(hd)]
                 + sig[:, 3 * hd + 1:3 * hd + 2] * o_sel[g]
                 + sig[:, 3 * hd + 2:3 * hd + 3] * o_win[g])
            y = o * lax.rsqrt(jnp.mean(o * o, axis=-1, keepdims=True) + EPS)
            o_ref[:, hsl(hd)] = (y * gnw_ref[:, hsl(hd)]).astype(o_ref.dtype)


def sel_win(qkv, selb, ksp, vstp, kwp, vwtp, e2p, tab, ocmp, gl, gnw, table, B, S, tq=TQ_SEL):
    T = qkv.shape[0]
    n_qb = S // tq
    dq = N_Q_HEADS * HEAD_DIM
    dkv = N_KV_HEADS * HEAD_DIM
    SP = S + WINDOW
    kern = functools.partial(_sel_win_kernel, n_qb=n_qb, scale=HEAD_DIM ** -0.5 * LOG2E)
    k_spec = pl.BlockSpec((SP, dkv), lambda i: (i // n_qb, 0))
    vt_spec = pl.BlockSpec((dkv, SP), lambda i: (i // n_qb, 0))
    steps = T // tq
    assert table.shape[0] % (2 * SUBLANES * steps) == 0
    slab_spec = pl.BlockSpec((table.shape[0] // steps, table.shape[1]), lambda i: (i, 0))
    return pl.pallas_call(
        kern,
        grid=(T // tq,),
        in_specs=[pl.BlockSpec((tq, dq), lambda i: (i, 0)),
                  pl.BlockSpec((tq, dkv), lambda i: (i, 0)),
                  k_spec, vt_spec, k_spec, vt_spec,
                  pl.BlockSpec((SP, HEAD_DIM), lambda i: (0, 0)),
                  pl.BlockSpec(tab.shape, lambda i: (0, 0)),
                  pl.BlockSpec((tq, dq), lambda i: (i, 0)),
                  pl.BlockSpec((tq, HEAD_DIM), lambda i: (i, 0)),
                  pl.BlockSpec((1, dq), lambda i: (0, 0)),
                  slab_spec],
        out_specs=[pl.BlockSpec((tq, dq), lambda i: (i, 0)), slab_spec],
        out_shape=[jax.ShapeDtypeStruct((T, dq), BF16), jax.ShapeDtypeStruct(table.shape, BF16)],
        scratch_shapes=[pltpu.VMEM((1, GQA * tq), F32), pltpu.VMEM((1, GQA * tq), F32),
                        pltpu.VMEM((HEAD_DIM, GQA * tq), F32),
                        pltpu.VMEM((WINDOW, GQA * tq), F32), pltpu.VMEM((WINDOW, GQA * tq), F32)],
        compiler_params=_params(("parallel",)),
    )(qkv, selb, ksp, vstp, kwp, vwtp, e2p, tab, ocmp, gl, gnw, table)


def _conv_kernel(b_ref, c_ref, h_ref, cp_ref, hp_ref, cw_ref, cb_ref, gnw_ref, o_ref, u_ref, *, tiles_per_seq):
    tm = b_ref.shape[0]
    first = (pl.program_id(0) % tiles_per_seq) == 0
    u_prev = cp_ref[...] * hp_ref[...]
    u_ref[0:8, :] = jnp.where(first, 0.0, u_prev)
    u = c_ref[...] * h_ref[...]
    u_ref[8:8 + tm, :] = u
    y = (cw_ref[0:1, :] * u_ref[6:6 + tm, :] + cw_ref[1:2, :] * u_ref[7:7 + tm, :]
         + cw_ref[2:3, :] * u + cb_ref[...])
    o = b_ref[...] * y
    n_groups = o.shape[1] // HEAD_DIM
    for g in range(n_groups):
        sl = slice(g * HEAD_DIM, (g + 1) * HEAD_DIM)
        og = o[:, sl]
        yg = og * lax.rsqrt(jnp.mean(og * og, axis=-1, keepdims=True) + EPS)
        o_ref[:, sl] = (yg * gnw_ref[:, sl]).astype(o_ref.dtype)


def conv_mixer(gc, conv_w, conv_b, gnw, S, tm):
    T = gc.shape[0]
    dc = conv_w.shape[1]
    tps = S // tm
    kern = functools.partial(_conv_kernel, tiles_per_seq=tps)
    prev = lambda col: pl.BlockSpec((8, dc), lambda i, col=col: (jnp.maximum(i * (tm // 8) - 1, 0), col))
    cur = lambda col: pl.BlockSpec((tm, dc), lambda i, col=col: (i, col))
    return pl.pallas_call(
        kern,
        grid=(T // tm,),
        in_specs=[cur(0), cur(1), cur(2), prev(1), prev(2),
                  pl.BlockSpec((8, dc), lambda i: (0, 0)),
                  pl.BlockSpec((1, dc), lambda i: (0, 0)),
                  pl.BlockSpec((1, dc), lambda i: (0, 0))],
        out_specs=pl.BlockSpec((tm, dc), lambda i: (i, 0)),
        out_shape=jax.ShapeDtypeStruct((T, dc), BF16),
        scratch_shapes=[pltpu.VMEM((tm + 8, dc), F32)],
        compiler_params=_params(("parallel",)),
    )(gc, gc, gc, gc, gc, conv_w, conv_b, gnw)


def _out_query_route_kernel(ma_ref, mc_ref, wa_ref, wc_ref, x_ref, nw_ref, wq_ref, sk_ref,
                            h_ref, e_ref, g_ref, q_ref, sv_ref, si_ref):
    @pl.when(pl.program_id(0) == 0)
    def _():
        q_ref[...] = jnp.zeros(q_ref.shape, q_ref.dtype)

    for t in range(q_ref.shape[0] // TT_ROUTE):
        _route_tokens(q_ref.at[pl.ds(t * TT_ROUTE, TT_ROUTE)], sk_ref, e_ref.at[pl.ds(t, 1)], g_ref.at[pl.ds(t, 1)],
                      sv_ref, si_ref)

    h = x_ref[...] + _dot(ma_ref[...], wa_ref[...]) + _dot(mc_ref[...], wc_ref[...])
    h_ref[...] = h
    y = h * lax.rsqrt(jnp.mean(h * h, axis=-1, keepdims=True) + EPS)
    q_ref[...] = _dot((y * nw_ref[...]).astype(BF16), wq_ref[...]).astype(q_ref.dtype)


def out_proj_query_route(ma, mc, w, x, norm_w, wq, subkeys, tm):
    T, da = ma.shape
    dc = mc.shape[1]
    D = x.shape[1]
    assert da == dc and w.shape == (da + dc, D) and wq.shape[0] == D and tm % TT_ROUTE == 0
    n = T // tm
    per = tm // TT_ROUTE
    P = PEER_HEADS * PEER_TOPK
    resident = pl.Buffered(1)
    cur = lambda i: (jnp.minimum(i, n - 1), 0)
    prev = lambda i: (jnp.maximum(i - 1, 0), 0, 0)
    return pl.pallas_call(
        _out_query_route_kernel,
        grid=(n + 1,),
        in_specs=[pl.BlockSpec((tm, da), cur),
                  pl.BlockSpec((tm, dc), cur),
                  pl.BlockSpec((da, D), lambda i: (0, 0), pipeline_mode=resident),
                  pl.BlockSpec((dc, D), lambda i: (1, 0), pipeline_mode=resident),
                  pl.BlockSpec((tm, D), cur),
                  pl.BlockSpec((1, D), lambda i: (0, 0)),
                  pl.BlockSpec(wq.shape, lambda i: (0, 0), pipeline_mode=resident),
                  pl.BlockSpec(subkeys.shape, lambda i: (0, 0, 0), pipeline_mode=resident)],
        out_specs=[pl.BlockSpec((tm, D), cur),
                   pl.BlockSpec((per, P, TT_ROUTE), prev), pl.BlockSpec((per, P, TT_ROUTE), prev)],
        out_shape=[jax.ShapeDtypeStruct((T, D), F32),
                   jax.ShapeDtypeStruct((T // TT_ROUTE, P, TT_ROUTE), jnp.int32),
                   jax.ShapeDtypeStruct((T // TT_ROUTE, P, TT_ROUTE), F32)],
        scratch_shapes=[pltpu.VMEM((tm, wq.shape[1]), BF16),
                        pltpu.VMEM((2, PEER_TOPK, TT_ROUTE), F32), pltpu.VMEM((2, PEER_TOPK, TT_ROUTE), F32)],
        compiler_params=_params(("arbitrary",)),
    )(ma, mc, w, w, x, norm_w.reshape(1, D), wq, subkeys)


def _sort_network(n):
    pairs = []
    p = 1
    while p < n:
        k = p
        while k >= 1:
            for j in range(k % p, n - k, 2 * k):
                for i in range(min(k, n - j - k)):
                    if (i + j) // (2 * p) == (i + j + k) // (2 * p):
                        pairs.append((i + j, i + j + k))
            k //= 2
        p *= 2
    return pairs


def _route_tokens(q_ref, sk_ref, e_ref, g_ref, sv_ref, si_ref):
    tt = q_ref.shape[0]
    K = PEER_TOPK
    sub = SUBLANES
    n_io = lax.broadcasted_iota(jnp.int32, (N_KEYS, tt), 0).astype(F32)
    r8 = lax.broadcasted_iota(jnp.int32, (sub, tt), 0).astype(F32)
    ninf = -jnp.inf
    lens = [K // (a + 1) for a in range(N_FIXED_A)]
    lens += [max(K // (b + 1) - N_FIXED_A, 0) for b in range(sub - N_FIXED_A)]
    assert sum(lens) == sum(K // (a + 1) for a in range(K)) and lens[-1] == 0
    fixed_a = r8 < float(N_FIXED_A)
    list_len = jnp.zeros((sub, tt), F32)
    for row, n in enumerate(lens):
        list_len = jnp.where(r8 == float(row), float(n), list_len)

    for h in range(PEER_HEADS):
        for c in range(2):
            col = (h * 2 + c) * HEAD_DIM
            s = _dot_nt(sk_ref[h * 2 + c], q_ref[:, col:col + HEAD_DIM])
            n_col = N_KEYS // sub
            vals = [s[j * sub:(j + 1) * sub, :] for j in range(n_col)]
            idxs = [n_io[j * sub:(j + 1) * sub, :] for j in range(n_col)]
            for lo, hi in _sort_network(n_col):
                swap = (vals[hi] > vals[lo]) | ((vals[hi] == vals[lo]) & (idxs[hi] < idxs[lo]))
                vals[lo], vals[hi] = jnp.where(swap, vals[hi], vals[lo]), jnp.where(swap, vals[lo], vals[hi])
                idxs[lo], idxs[hi] = jnp.where(swap, idxs[hi], idxs[lo]), jnp.where(swap, idxs[lo], idxs[hi])
            for k in range(K):
                m = jnp.max(vals[0], axis=0, keepdims=True)
                idx = jnp.min(jnp.where(vals[0] == m, idxs[0], float(N_KEYS)), axis=0, keepdims=True)
                sv_ref[c, k:k + 1, :] = m
                si_ref[c, k:k + 1, :] = idx
                win = idxs[0] == idx
                for j in range(min(n_col, K) - 1 - k):
                    vals[j] = jnp.where(win, vals[j + 1], vals[j])
                    idxs[j] = jnp.where(win, idxs[j + 1], idxs[j])
        sv0, sv1 = sv_ref[0], sv_ref[1]
        si0, si1 = si_ref[0], si_ref[1]
        sv1_low = pltpu.roll(sv1[0:sub, :], N_FIXED_A, 0)
        sv0_top = sv0[0:sub, :]
        lv, lf = [], []
        for dep in range(K):
            a_dep = min(N_FIXED_A + dep, K - 1)
            val = (jnp.where(fixed_a, sv0_top, sv0[a_dep:a_dep + 1, :])
                   + jnp.where(fixed_a, sv1[dep:dep + 1, :], sv1_low))
            lv.append(jnp.where(list_len > dep, val, ninf))
            lf.append(jnp.where(fixed_a, r8 * float(K) + dep, (N_FIXED_A + dep) * float(K) + r8 - N_FIXED_A))
        cvs, fls = [], []
        for k in range(K):
            m = jnp.max(lv[0], axis=0, keepdims=True)
            fsel = jnp.min(jnp.where(lv[0] == m, lf[0], 1e9), axis=0, keepdims=True)
            win = lf[0] == fsel
            cvs.append(m)
            fls.append(fsel)
            for dep in range(K - 1 - k):
                lv[dep] = jnp.where(win, lv[dep + 1], lv[dep])
                lf[dep] = jnp.where(win, lf[dep + 1], lf[dep])
        cv = jnp.concatenate(cvs, axis=0)
        fl = jnp.concatenate(fls, axis=0)
        a_sel = jnp.floor(fl * (1.0 / K))
        b_sel = fl - a_sel * K
        i1 = jnp.zeros_like(fl)
        i2 = jnp.zeros_like(fl)
        for r in range(K):
            i1 = jnp.where(a_sel == r, si0[r:r + 1, :], i1)
            i2 = jnp.where(b_sel == r, si1[r:r + 1, :], i2)
        ex = i1 * float(N_KEYS) + i2
        ev = jnp.exp(cv - jnp.max(cv, axis=0, keepdims=True))
        gates = ev / jnp.sum(ev, axis=0, keepdims=True)
        e_ref[0, h * K:(h + 1) * K, :] = ex.astype(jnp.int32)
        g_ref[0, h * K:(h + 1) * K, :] = gates


def _peer_gbuild_kernel(e_ref, g_ref, tab_in_ref, o_ref, tab_out_ref, i1_ref, i2_ref, gt_ref):
    tab_out_ref[...] = tab_in_ref[...].astype(BF16)
    tt = e_ref.shape[2]
    e = e_ref[0].T
    i1_ref[...] = e >> 7
    i2_ref[...] = e & (N_KEYS - 1)
    gt_ref[...] = g_ref[0].T
    P = e.shape[1]
    k_io = lax.broadcasted_iota(jnp.int32, (N_KEYS, P), 0)
    group = tt

    def body(tg, carry):
        base = pl.multiple_of(tg * group, group)
        r1s = i1_ref[pl.ds(base, group), :]
        r2s = i2_ref[pl.ds(base, group), :]
        rgs = gt_ref[pl.ds(base, group), :]
        for u in range(group):
            lhs = jnp.where(k_io == r1s[u:u + 1, :], rgs[u:u + 1, :], 0.0).astype(BF16)
            rhs = jnp.where(k_io == r2s[u:u + 1, :], 1.0, 0.0).astype(BF16)
            g_t = _dot_nt(lhs, rhs).astype(BF16).astype(F32)
            lo = pltpu.bitcast(g_t[:HALF_KEYS], jnp.uint32) >> 16
            hi = pltpu.bitcast(g_t[HALF_KEYS:], jnp.uint32)
            words = hi | lo
            for a in range(HALF_KEYS // PACK_ROWS):
                o_ref[a, base + u] = words[a * PACK_ROWS:(a + 1) * PACK_ROWS, :]
        return carry

    lax.fori_loop(0, tt // group, body, 0)


def peer_gbuild(ex, gates, table):
    nt, P, tt = ex.shape
    T = nt * tt
    n_a = HALF_KEYS // PACK_ROWS
    E, D = table.shape
    slab = E // nt
    assert slab * nt == E and slab % (2 * SUBLANES) == 0
    tab_spec = pl.BlockSpec((slab, D), lambda i: (i, 0))
    return pl.pallas_call(
        _peer_gbuild_kernel,
        grid=(nt,),
        in_specs=[pl.BlockSpec((1, P, tt), lambda i: (i, 0, 0))] * 2 + [tab_spec],
        out_specs=[pl.BlockSpec((n_a, tt, PACK_ROWS, N_KEYS), lambda i: (0, i, 0, 0)), tab_spec],
        out_shape=[jax.ShapeDtypeStruct((n_a, T, PACK_ROWS, N_KEYS), jnp.uint32),
                   jax.ShapeDtypeStruct((E, D), BF16)],
        scratch_shapes=[pltpu.VMEM((tt, P), jnp.int32), pltpu.VMEM((tt, P), jnp.int32),
                        pltpu.VMEM((tt, P), F32)],
        compiler_params=_params(("parallel",)),
    )(ex, gates, table)


def _peer_dense_kernel(h_ref, nw_ref, fw_ref, u_ref, v_ref, gm_ref, o_ref, hn_ref, acc_ref):
    j = pl.program_id(1)
    tt = h_ref.shape[0]

    @pl.when(j == 0)
    def _():
        x = h_ref[...]
        y = x * lax.rsqrt(jnp.mean(x * x, axis=-1, keepdims=True) + EPS)
        hn_ref[...] = (y * nw_ref[...]).astype(BF16)
        acc_ref[...] = jnp.zeros(acc_ref.shape, F32)

    a = _dot_nt(hn_ref[...], u_ref[...])
    shift = (16 * (1 - j % 2)).astype(jnp.uint32)
    gm = jnp.concatenate(
        [pltpu.bitcast((gm_ref[pl.ds(k, tt, stride=PACK_ROWS), :] << shift) & jnp.uint32(0xFFFF0000), F32)
         for k in range(PACK_ROWS)], axis=1)
    w = (gm * jax.nn.gelu(a)).astype(BF16)
    acc_ref[...] += _dot(w, v_ref[...])

    @pl.when(j == pl.num_programs(1) - 1)
    def _():
        x = h_ref[...] + acc_ref[...]
        y = x * lax.rsqrt(jnp.mean(x * x, axis=-1, keepdims=True) + EPS)
        o_ref[...] = y * fw_ref[...]


def peer_dense(h, ffn_nw, final_nw, u, v, gm_words, tt):
    T, D = h.shape
    E = u.shape[0]
    ec = PACK_ROWS * N_KEYS
    n_a = gm_words.shape[0]
    assert E == 2 * n_a * ec
    gm2 = gm_words.reshape(n_a, T * PACK_ROWS, N_KEYS)
    tab_spec = pl.BlockSpec((ec, D), lambda i, j: ((j % 2) * n_a + j // 2, 0))
    return pl.pallas_call(
        _peer_dense_kernel,
        grid=(T // tt, E // ec),
        in_specs=[pl.BlockSpec((tt, D), lambda i, j: (i, 0)),
                  pl.BlockSpec((1, D), lambda i, j: (0, 0)),
                  pl.BlockSpec((1, D), lambda i, j: (0, 0)),
                  tab_spec, tab_spec,
                  pl.BlockSpec((None, tt * PACK_ROWS, N_KEYS), lambda i, j: (j // 2, i, 0))],
        out_specs=pl.BlockSpec((tt, D), lambda i, j: (i, 0)),
        out_shape=jax.ShapeDtypeStruct((T, D), F32),
        scratch_shapes=[pltpu.VMEM((tt, D), BF16), pltpu.VMEM((tt, D), F32)],
        compiler_params=_params(("parallel", "arbitrary")),
    )(h, ffn_nw.reshape(1, D), final_nw.reshape(1, D), u, v, gm2)


def _t5_bucket_np(n_dist):
    d = np.arange(n_dist)
    max_exact = NUM_BUCKETS // 2
    nf = np.maximum(d, 1).astype(np.float64)
    large = max_exact + (np.log(nf / max_exact) / math.log(MAX_DISTANCE / max_exact)
                         * (NUM_BUCKETS - max_exact)).astype(np.int64)
    large = np.minimum(large, NUM_BUCKETS - 1)
    return np.where(d < max_exact, d, large).astype(np.int32)


def _overlap_t_np(S):
    n_c = (S - L_CMP) // STRIDE_CMP + 1
    n_sel = S // L_SEL
    pos = np.arange(n_c)[:, None] * STRIDE_CMP + np.arange(L_CMP)[None, :]
    m = np.zeros((n_c + 1, n_sel), np.float32)
    np.add.at(m, (np.repeat(np.arange(n_c), L_CMP), (pos // L_SEL).reshape(-1)), 1.0 / L_CMP)
    return np.ascontiguousarray(m.T)


def _block_onehot_padded_np(S):
    assert S // L_SEL < HEAD_DIM
    e2 = np.zeros((WINDOW + S, HEAD_DIM), np.float32)
    e2[WINDOW + np.arange(S), np.arange(S) // L_SEL] = 1.0
    e2[:WINDOW, HEAD_DIM - 1] = SEL_OFF
    return e2


def nsa_conv_mix(xt, B, S, attn_norm_w, w_in, w_cmp_k, w_cmp_v, cmp_pos, conv_w, conv_b,
                 attn_gnw, conv_gnw, rel_bias, tm=TM_PROJ, tm_conv=TM_CONV, cast_also=(), table=None):
    T, D = xt.shape
    dq = N_Q_HEADS * HEAD_DIM
    dkv = N_KV_HEADS * HEAD_DIM
    n_attn = dq + 6 * dkv
    n_gate = 3 * N_Q_HEADS
    dc = (w_in.shape[1] - n_attn - n_gate) // 3
    w_t = w_in.T
    w_attn = cast_rows(w_t, 0, n_attn)
    w_conv = cast_rows(w_t, n_attn + n_gate, 3 * dc)
    w_gate = jnp.pad(w_t[n_attn:n_attn + n_gate], ((0, HEAD_DIM - n_gate), (0, 0))).astype(BF16)

    qkv, gc, gl = in_proj(xt, attn_norm_w, w_attn, w_conv, w_gate, tm)

    n16 = S // STRIDE_CMP
    k16 = qkv[:, dq:dq + dkv].reshape(B * n16, STRIDE_CMP * dkv)
    v16 = qkv[:, dq + dkv:dq + 2 * dkv].reshape(B * n16, STRIDE_CMP * dkv)

    def wflat(w, lo):
        return w[lo:lo + STRIDE_CMP].reshape(STRIDE_CMP * HEAD_DIM, HEAD_DIM).astype(BF16)

    def posrow(lo):
        return cmp_pos[lo:lo + STRIDE_CMP].reshape(1, STRIDE_CMP * HEAD_DIM)

    kc, vc = compress(k16, v16, posrow(0), posrow(STRIDE_CMP),
                      wflat(w_cmp_k, 0), wflat(w_cmp_k, STRIDE_CMP),
                      wflat(w_cmp_v, 0), wflat(w_cmp_v, STRIDE_CMP), B)

    tab = rel_bias[_t5_bucket_np(HEAD_DIM)].T
    ovt = jnp.asarray(_overlap_t_np(S), BF16)
    e2p = jnp.asarray(_block_onehot_padded_np(S), BF16)

    ocmp, selb, *casts = cmp_select(qkv, kc, vc, tab, ovt, B, S, cast_also=cast_also)
    ksp, vstp, kwp, vwtp = repack_kv(qkv, B, S, first_col_block=dq // dkv + 2)
    attn, table_bf = sel_win(qkv, selb, ksp, vstp, kwp, vwtp, e2p, tab, ocmp, gl, attn_gnw.reshape(1, dq), table, B, S)
    cw8 = jnp.pad(conv_w, ((0, 8 - conv_w.shape[0]), (0, 0)))
    conv = conv_mixer(gc, cw8, conv_b.reshape(1, dc), conv_gnw.reshape(1, dc), S, tm_conv)
    return (attn, conv, table_bf, *casts)


def peer_block(h, ex, gates, ffn_nw, final_nw, u_bf, peer_v, tt=TT_DENSE):
    gm, v_bf = peer_gbuild(ex, gates, peer_v)
    return peer_dense(h, ffn_nw, final_nw, u_bf, v_bf, gm, tt)


def kernel(x, attn_norm_w, w_in, w_cmp_k, w_cmp_v, cmp_pos, conv_w, conv_b, attn_group_norm_w,
           conv_group_norm_w, w_out, rel_bias, ffn_norm_w, peer_wq, peer_subkeys, peer_u, peer_v,
           final_norm_w):
    B, S, D = x.shape
    T = B * S
    xt = x.reshape(T, D)
    attn, conv, u_bf, wo, wq = nsa_conv_mix(xt, B, S, attn_norm_w[0], w_in[0], w_cmp_k[0], w_cmp_v[0], cmp_pos[0],
                                            conv_w[0], conv_b[0], attn_group_norm_w[0], conv_group_norm_w[0],
                                            rel_bias, cast_also=(w_out[0], peer_wq[0]), table=peer_u[0])
    sk = peer_subkeys[0].reshape(PEER_HEADS * 2, N_KEYS, peer_subkeys.shape[-1]).astype(BF16)
    h, ex, gates = out_proj_query_route(attn, conv, wo, xt, ffn_norm_w[0], wq, sk, TM_OUT)
    out = peer_block(h, ex, gates, ffn_norm_w[0], final_norm_w, u_bf, peer_v[0])
    return out.reshape(B, S, D)
```

```python
import functools
import math

import jax
import jax.numpy as jnp
import numpy as np
from jax import lax
from jax.experimental import pallas as pl
from jax.experimental.pallas import tpu as pltpu

F32 = jnp.float32
BF16 = jnp.bfloat16

HEAD_DIM = 128
N_KV_HEADS = 2
GQA = 4
N_Q_HEADS = N_KV_HEADS * GQA
L_CMP = 32
STRIDE_CMP = 16
L_SEL = 64
N_SEL = 16
WINDOW = 512
Q_BLOCK = 128
FORCED_SCORE = float(GQA + 1)
NUM_BUCKETS = 32
MAX_DISTANCE = 128
PEER_HEADS = 8
N_KEYS = 128
PEER_TOPK = 16
EPS = 1e-6
NEG_BIG = -1e30
SEL_OFF = -float(2 ** 30)
LOG2E = math.log2(math.e)
N_FIXED_A = 4

SUBLANES = 8
LANES = 128
MXU_DIM = 256
V7X_VMEM_BYTES = 64 * 1024 * 1024
VMEM_LIMIT = V7X_VMEM_BYTES * 7 // 8

HALF_KEYS = N_KEYS // 2
PACK_ROWS = SUBLANES

TM_PROJ = MXU_DIM
TM_OUT = 2 * MXU_DIM
TM_CONV = 2 * MXU_DIM
TQ_CMP = 4 * Q_BLOCK
TQ_SEL = 2 * Q_BLOCK
TT_ROUTE = LANES
TT_DENSE = 2 * MXU_DIM
CAST_ROWS = MXU_DIM


def _dot(a, b):
    return jnp.dot(a, b, preferred_element_type=F32)


def _dot_nt(a, b):
    return lax.dot_general(a, b, (((1,), (1,)), ((), ())), preferred_element_type=F32)


def _params(sem, vmem=VMEM_LIMIT):
    return pltpu.CompilerParams(dimension_semantics=sem, vmem_limit_bytes=vmem)


def _in_proj_kernel(x_ref, nw_ref, wa_ref, wc_ref, wg_ref, oa_ref, oc_ref, og_ref):
    x = x_ref[...]
    y = x * lax.rsqrt(jnp.mean(x * x, axis=-1, keepdims=True) + EPS)
    xn = (y * nw_ref[...]).astype(BF16)
    oa_ref[...] = _dot_nt(xn, wa_ref[...]).astype(oa_ref.dtype)
    oc_ref[...] = _dot_nt(xn, wc_ref[...])
    og_ref[...] = _dot_nt(xn, wg_ref[...])


def in_proj(x, norm_w, w_attn, w_conv, w_gate, tm):
    T, D = x.shape
    resident = pl.Buffered(1)
    row = lambda i: (i, 0)
    fixed = lambda i: (0, 0)
    widths = [w.shape[0] for w in (w_attn, w_conv, w_gate)]
    return pl.pallas_call(
        _in_proj_kernel,
        grid=(T // tm,),
        in_specs=[pl.BlockSpec((tm, D), row), pl.BlockSpec((1, D), fixed)]
                 + [pl.BlockSpec(w.shape, fixed, pipeline_mode=resident) for w in (w_attn, w_conv, w_gate)],
        out_specs=[pl.BlockSpec((tm, n), row) for n in widths],
        out_shape=[jax.ShapeDtypeStruct((T, widths[0]), BF16), jax.ShapeDtypeStruct((T, widths[1]), F32),
                   jax.ShapeDtypeStruct((T, widths[2]), F32)],
        compiler_params=_params(("parallel",)),
    )(x, norm_w.reshape(1, D), w_attn, w_conv, w_gate)


def _cast_rows_kernel(w_ref, o_ref):
    o_ref[...] = w_ref[...].astype(o_ref.dtype)


def cast_rows(w, row0, n_rows, tr=CAST_ROWS):
    K = w.shape[1]
    assert row0 % SUBLANES == 0 and n_rows % tr == 0
    return pl.pallas_call(
        _cast_rows_kernel,
        grid=(n_rows // tr,),
        in_specs=[pl.BlockSpec((pl.Element(tr), pl.Element(K)), lambda i: (pl.multiple_of(row0 + i * tr, SUBLANES), 0))],
        out_specs=pl.BlockSpec((tr, K), lambda i: (i, 0)),
        out_shape=jax.ShapeDtypeStruct((n_rows, K), BF16),
        compiler_params=_params(("parallel",)),
    )(w)


def _compress_kernel(k_ref, v_ref, pa_ref, pb_ref, wka_ref, wkb_ref, wva_ref, wvb_ref, kc_ref, vc_ref):
    dkv = kc_ref.shape[1]

    def one(x_ref, wa_ref, wb_ref, o_ref):
        for h in range(N_KV_HEADS):
            x = jnp.concatenate([x_ref[:, l * dkv + h * HEAD_DIM:l * dkv + (h + 1) * HEAD_DIM]
                                 for l in range(STRIDE_CMP)], axis=1).astype(F32)
            a = _dot((x + pa_ref[...]).astype(BF16), wa_ref[...])
            b = _dot((x + pb_ref[...]).astype(BF16), wb_ref[...])
            n = b.shape[0]
            o_ref[:, h * HEAD_DIM:(h + 1) * HEAD_DIM] = (a + pltpu.roll(b, n - 1, 0)).astype(o_ref.dtype)

    one(k_ref, wka_ref, wkb_ref, kc_ref)
    one(v_ref, wva_ref, wvb_ref, vc_ref)


def compress(k16, v16, pos_a, pos_b, wka, wkb, wva, wvb, B):
    R, C = k16.shape
    nb = R // B
    dkv = C // STRIDE_CMP
    full = lambda shp: pl.BlockSpec(shp, lambda b: (0, 0))
    return pl.pallas_call(
        _compress_kernel,
        grid=(B,),
        in_specs=[pl.BlockSpec((nb, C), lambda b: (b, 0)),
                  pl.BlockSpec((nb, C), lambda b: (b, 0)),
                  full(pos_a.shape), full(pos_b.shape),
                  full(wka.shape), full(wkb.shape), full(wva.shape), full(wvb.shape)],
        out_specs=[pl.BlockSpec((nb, dkv), lambda b: (b, 0))] * 2,
        out_shape=[jax.ShapeDtypeStruct((R, dkv), BF16)] * 2,
        compiler_params=_params(("parallel",)),
    )(k16, v16, pos_a, pos_b, wka, wkb, wva, wvb)


def _bias_from_dist(tab_row, dist):
    idx = jnp.clip(dist, 0, 127)
    w = tab_row.shape[1]
    tab = jnp.broadcast_to(tab_row, (idx.shape[0], w))
    parts = [jnp.take_along_axis(tab, idx[:, k:k + w], axis=1) for k in range(0, idx.shape[1], w)]
    return parts[0] if len(parts) == 1 else jnp.concatenate(parts, axis=1)


def _cmp_select_kernel(q_ref, kc_ref, vc_ref, tab_ref, ovt_ref, *rest, n_qb, scale, n_cast):
    cast_in, (ocmp_ref, selb_ref), cast_out = rest[:n_cast], rest[n_cast:n_cast + 2], rest[n_cast + 2:]
    for src_ref, dst_ref in zip(cast_in, cast_out):
        dst_ref[...] = src_ref[...].astype(BF16)
    c = pl.program_id(0) % n_qb
    tq = q_ref.shape[0]
    n_c = kc_ref.shape[0]
    n_sel = ovt_ref.shape[0]
    t0 = c * tq
    t_col = t0 + lax.broadcasted_iota(jnp.int32, (tq, n_c), 0)
    n_row = lax.broadcasted_iota(jnp.int32, (tq, n_c), 1)
    dist = t_col - (n_row * STRIDE_CMP + (L_CMP - 1))
    valid = dist >= 0

    j_io = lax.broadcasted_iota(jnp.int32, (n_sel, tq), 0)
    t_io = t0 + lax.broadcasted_iota(jnp.int32, (n_sel, tq), 1)
    blk_t = t_io // L_SEL
    forced = (j_io == 0) | (j_io == blk_t) | (j_io == blk_t - 1)
    causal_blk = j_io * L_SEL <= t_io

    for h in range(N_KV_HEADS):
        kc = kc_ref[:, h * HEAD_DIM:(h + 1) * HEAD_DIM]
        vc = vc_ref[:, h * HEAD_DIM:(h + 1) * HEAD_DIM]
        psum = jnp.zeros((tq, n_c), F32)
        for g in range(GQA):
            hd = h * GQA + g
            qh = q_ref[:, hd * HEAD_DIM:(hd + 1) * HEAD_DIM]
            bias = _bias_from_dist(tab_ref[hd:hd + 1, :], dist)
            s = _dot_nt(qh, kc) * scale + bias
            s = jnp.where(valid, s, NEG_BIG)
            m = jnp.max(s, axis=-1, keepdims=True)
            e = jnp.where(valid, jnp.exp(s - m), 0.0)
            d = jnp.sum(e, axis=-1, keepdims=True)
            p = e / jnp.where(d > 0, d, 1.0)
            ocmp_ref[:, hd * HEAD_DIM:(hd + 1) * HEAD_DIM] = _dot(p.astype(BF16), vc)
            psum = psum + p
        p_hi = psum.astype(BF16)
        p_lo = (psum - p_hi.astype(F32)).astype(BF16)
        ovt = ovt_ref[...]
        imp = _dot_nt(ovt, p_hi) + _dot_nt(ovt, p_lo)
        score = jnp.where(forced, FORCED_SCORE, jnp.where(causal_blk, imp, -1.0))
        rank = jnp.zeros((n_sel, tq), F32)
        for jp in range(n_sel):
            row = score[jp:jp + 1, :]
            rank = rank + jnp.where(j_io > jp, jnp.where(row >= score, 1.0, 0.0),
                                    jnp.where(row > score, 1.0, 0.0))
        selb = jnp.where(rank < float(N_SEL), 0.0, SEL_OFF)
        if n_sel < HEAD_DIM:
            selb = jnp.concatenate([selb, jnp.zeros((HEAD_DIM - n_sel, tq), F32)], axis=0)
        selb_ref[:, h * HEAD_DIM:(h + 1) * HEAD_DIM] = selb.T.astype(BF16)


def cmp_select(qkv, kc, vc, tab, ovt, B, S, tq=TQ_CMP, cast_also=()):
    T = qkv.shape[0]
    n_qb = S // tq
    steps = T // tq
    n_c = kc.shape[0] // B
    dq = N_Q_HEADS * HEAD_DIM
    dkv = N_KV_HEADS * HEAD_DIM
    kern = functools.partial(_cmp_select_kernel, n_qb=n_qb, scale=HEAD_DIM ** -0.5, n_cast=len(cast_also))
    cast_specs = [pl.BlockSpec((w.shape[0] // steps, w.shape[1]), lambda i: (i, 0)) for w in cast_also]
    assert all(w.shape[0] % (2 * SUBLANES * steps) == 0 for w in cast_also)
    return pl.pallas_call(
        kern,
        grid=(steps,),
        in_specs=[pl.BlockSpec((tq, dq), lambda i: (i, 0)),
                  pl.BlockSpec((n_c, dkv), lambda i: (i // n_qb, 0)),
                  pl.BlockSpec((n_c, dkv), lambda i: (i // n_qb, 0)),
                  pl.BlockSpec(tab.shape, lambda i: (0, 0)),
                  pl.BlockSpec(ovt.shape, lambda i: (0, 0))] + cast_specs,
        out_specs=[pl.BlockSpec((tq, dq), lambda i: (i, 0)),
                   pl.BlockSpec((tq, dkv), lambda i: (i, 0))] + cast_specs,
        out_shape=[jax.ShapeDtypeStruct((T, dq), F32),
                   jax.ShapeDtypeStruct((T, dkv), BF16)] + [jax.ShapeDtypeStruct(w.shape, BF16) for w in cast_also],
        compiler_params=_params(("parallel",)),
    )(qkv, kc, vc, tab, ovt, *cast_also)


def _repack_kv_kernel(ks_ref, vs_ref, kw_ref, vw_ref, ksp_ref, vst_ref, kwp_ref, vwt_ref):
    first = pl.program_id(1) == 0

    @pl.when(first)
    def _():
        for ref in (ksp_ref, vst_ref, kwp_ref, vwt_ref):
            ref[...] = jnp.zeros(ref.shape, ref.dtype)

    @pl.when(jnp.logical_not(first))
    def _():
        ksp_ref[...] = ks_ref[...]
        kwp_ref[...] = kw_ref[...]
        vst_ref[...] = vs_ref[...].T
        vwt_ref[...] = vw_ref[...].T


def repack_kv(qkv, B, S, first_col_block):
    dkv = N_KV_HEADS * HEAD_DIM
    nt = S // WINDOW
    src = lambda col: pl.BlockSpec((WINDOW, dkv), lambda b, i, col=col: (b * nt + jnp.maximum(i - 1, 0), col))
    rows = pl.BlockSpec((WINDOW, dkv), lambda b, i: (b * (nt + 1) + i, 0))
    cols = pl.BlockSpec((dkv, WINDOW), lambda b, i: (b, i))
    return pl.pallas_call(
        _repack_kv_kernel,
        grid=(B, nt + 1),
        in_specs=[src(first_col_block + k) for k in range(4)],
        out_specs=[rows, cols, rows, cols],
        out_shape=[jax.ShapeDtypeStruct((B * (S + WINDOW), dkv), qkv.dtype),
                   jax.ShapeDtypeStruct((B * dkv, S + WINDOW), qkv.dtype)] * 2,
        compiler_params=_params(("parallel", "arbitrary")),
    )(qkv, qkv, qkv, qkv)


def _flash_update(s_raw, vt, bias, m_ref, l_ref, acc_ref, scale):
    m_old = m_ref[...]
    if bias.shape[0] == 1:
        s = s_raw * scale
        m_new = jnp.maximum(m_old, jnp.max(s, axis=0, keepdims=True) + bias)
        p = jnp.exp2(s - (m_new - bias))
    else:
        s = s_raw * scale + bias
        m_new = jnp.maximum(m_old, jnp.max(s, axis=0, keepdims=True))
        p = jnp.exp2(s - m_new)
    alpha = jnp.exp2(m_old - m_new)
    l_ref[...] = alpha * l_ref[...] + jnp.sum(p, axis=0, keepdims=True)
    acc_ref[...] = alpha * acc_ref[...] + _dot(vt, p.astype(BF16))
    m_ref[...] = m_new


def _flash_chunk(k, vt, q, bias, m_ref, l_ref, acc_ref, scale):
    _flash_update(_dot_nt(k, q), vt, bias, m_ref, l_ref, acc_ref, scale)


def _sel_win_kernel(q_ref, selb_ref, ks_ref, vst_ref, kw_ref, vwt_ref, e2_ref, tab_ref, ocmp_ref,
                    gl_ref, gnw_ref, tab_in_ref, o_ref, tab_out_ref, m_ref, l_ref, acc_ref, sa_ref, sb_ref,
                    *, n_qb, scale):
    tab_out_ref[...] = tab_in_ref[...].astype(BF16)
    c = pl.program_id(0) % n_qb
    tq = q_ref.shape[0]
    rows = GQA * tq
    n_back = WINDOW // tq
    near = WINDOW + tq
    j_io = lax.broadcasted_iota(jnp.int32, (tq, tq), 0)
    i_io = lax.broadcasted_iota(jnp.int32, (tq, tq), 1)
    dij = i_io - j_io
    dij4 = jnp.concatenate([dij] * GQA, axis=1)
    causal = dij4 >= 0
    pad_col = jnp.where(lax.broadcasted_iota(jnp.int32, (rows, HEAD_DIM), 1) == HEAD_DIM - 1, 1.0, 0.0).astype(BF16)
    sig = jax.nn.sigmoid(gl_ref[...])
    near0 = pl.multiple_of(c * tq, tq)
    n_far = jnp.maximum(c - n_back, 0)
    n_full = n_far // n_back
    n_rem = n_far - n_full * n_back

    def hsl(h):
        return slice(h * HEAD_DIM, (h + 1) * HEAD_DIM)

    def far0(i):
        return pl.multiple_of(WINDOW + i * WINDOW, WINDOW)

    def reset():
        m_ref[...] = jnp.full(m_ref.shape, NEG_BIG, F32)
        l_ref[...] = jnp.zeros(l_ref.shape, F32)
        acc_ref[...] = jnp.zeros(acc_ref.shape, F32)

    def result():
        ot = acc_ref[...] / l_ref[...]
        return [ot[:, g * tq:(g + 1) * tq].T for g in range(GQA)]

    for h in range(N_KV_HEADS):
        q4 = jnp.concatenate([q_ref[:, (h * GQA + g) * HEAD_DIM:(h * GQA + g + 1) * HEAD_DIM]
                              for g in range(GQA)], axis=0)
        sb4 = jnp.concatenate([selb_ref[:, hsl(h)]] * GQA, axis=0)
        q_aug = jnp.concatenate([q4, sb4 + pad_col], axis=1)
        q_win = jnp.concatenate([q4, pad_col], axis=1)
        tabs = [tab_ref[h * GQA + g:h * GQA + g + 1, :] * LOG2E for g in range(GQA)]
        d0 = jnp.concatenate([_bias_from_dist(t, dij) for t in tabs], axis=1)
        d1 = jnp.concatenate([_bias_from_dist(t, dij + tq) for t in tabs], axis=1)
        far = jnp.concatenate([t[:, HEAD_DIM - 1:HEAD_DIM] + jnp.zeros((1, tq), F32) for t in tabs], axis=1)
        far_t = jnp.broadcast_to(far, (tq, rows))
        diag = jnp.where(causal, d0, NEG_BIG)
        bias_sel = jnp.concatenate([far_t] * (n_back - 1) + [d1, diag], axis=0)
        bias_win = jnp.concatenate([jnp.where(dij4 < 0, far_t, NEG_BIG)] + [far_t] * (n_back - 2) + [d1, diag],
                                   axis=0)

        reset()
        e2_near = e2_ref[pl.ds(near0, near), :]
        k_near = jnp.concatenate([ks_ref[pl.ds(near0, near), hsl(h)], e2_near], axis=1)
        _flash_chunk(k_near, vst_ref[hsl(h), pl.ds(near0, near)], q_aug, bias_sel, m_ref, l_ref, acc_ref, scale)

        def far_scores(i):
            r0 = far0(i)
            k = jnp.concatenate([ks_ref[pl.ds(r0, WINDOW), hsl(h)], e2_ref[pl.ds(r0, WINDOW), :]], axis=1)
            return _dot_nt(k, q_aug)

        def far_update(s_raw, i, bias):
            _flash_update(s_raw, vst_ref[hsl(h), pl.ds(far0(i), WINDOW)], bias, m_ref, l_ref, acc_ref, scale)

        sa_ref[...] = far_scores(0)

        def body(i, carry):
            sb_ref[...] = far_scores(2 * i + 1)
            far_update(sa_ref[...], 2 * i, far)
            sa_ref[...] = far_scores(2 * i + 2)
            far_update(sb_ref[...], 2 * i + 1, far)
            return carry

        lax.fori_loop(0, n_full // 2, body, 0)

        @pl.when(n_full % 2 == 1)
        def _():
            far_update(sa_ref[...], n_full - 1, far)

        @pl.when(n_rem > 0)
        def _():
            if n_back == 2:
                r0 = far0(n_full)
                k = jnp.concatenate([ks_ref[pl.ds(r0, tq), hsl(h)], e2_ref[pl.ds(r0, tq), :]], axis=1)
                _flash_chunk(k, vst_ref[hsl(h), pl.ds(r0, tq)], q_aug, far, m_ref, l_ref, acc_ref, scale)
            else:
                live = lax.broadcasted_iota(jnp.int32, (WINDOW, rows), 0) < n_rem * tq
                far_update(far_scores(n_full), n_full,
                           jnp.where(live, jnp.broadcast_to(far, (WINDOW, rows)), NEG_BIG))

        o_sel = result()

        reset()
        kw_near = jnp.concatenate([kw_ref[pl.ds(near0, near), hsl(h)], e2_near], axis=1)
        _flash_chunk(kw_near, vwt_ref[hsl(h), pl.ds(near0, near)], q_win, bias_win, m_ref, l_ref, acc_ref, scale)
        o_win = result()

        for g in range(GQA):
            hd = h * GQA + g
            o = (sig[:, 3 * hd:3 * hd + 1] * ocmp_ref[:, hsl(hd)]
                 + sig[:, 3 * hd + 1:3 * hd + 2] * o_sel[g]
                 + sig[:, 3 * hd + 2:3 * hd + 3] * o_win[g])
            y = o * lax.rsqrt(jnp.mean(o * o, axis=-1, keepdims=True) + EPS)
            o_ref[:, hsl(hd)] = (y * gnw_ref[:, hsl(hd)]).astype(o_ref.dtype)


def sel_win(qkv, selb, ksp, vstp, kwp, vwtp, e2p, tab, ocmp, gl, gnw, table, B, S, tq=TQ_SEL):
    T = qkv.shape[0]
    n_qb = S // tq
    dq = N_Q_HEADS * HEAD_DIM
    dkv = N_KV_HEADS * HEAD_DIM
    SP = S + WINDOW
    kern = functools.partial(_sel_win_kernel, n_qb=n_qb, scale=HEAD_DIM ** -0.5 * LOG2E)
    k_spec = pl.BlockSpec((SP, dkv), lambda i: (i // n_qb, 0))
    vt_spec = pl.BlockSpec((dkv, SP), lambda i: (i // n_qb, 0))
    steps = T // tq
    assert table.shape[0] % (2 * SUBLANES * steps) == 0
    slab_spec = pl.BlockSpec((table.shape[0] // steps, table.shape[1]), lambda i: (i, 0))
    return pl.pallas_call(
        kern,
        grid=(T // tq,),
        in_specs=[pl.BlockSpec((tq, dq), lambda i: (i, 0)),
                  pl.BlockSpec((tq, dkv), lambda i: (i, 0)),
                  k_spec, vt_spec, k_spec, vt_spec,
                  pl.BlockSpec((SP, HEAD_DIM), lambda i: (0, 0)),
                  pl.BlockSpec(tab.shape, lambda i: (0, 0)),
                  pl.BlockSpec((tq, dq), lambda i: (i, 0)),
                  pl.BlockSpec((tq, HEAD_DIM), lambda i: (i, 0)),
                  pl.BlockSpec((1, dq), lambda i: (0, 0)),
                  slab_spec],
        out_specs=[pl.BlockSpec((tq, dq), lambda i: (i, 0)), slab_spec],
        out_shape=[jax.ShapeDtypeStruct((T, dq), BF16), jax.ShapeDtypeStruct(table.shape, BF16)],
        scratch_shapes=[pltpu.VMEM((1, GQA * tq), F32), pltpu.VMEM((1, GQA * tq), F32),
                        pltpu.VMEM((HEAD_DIM, GQA * tq), F32),
                        pltpu.VMEM((WINDOW, GQA * tq), F32), pltpu.VMEM((WINDOW, GQA * tq), F32)],
        compiler_params=_params(("parallel",)),
    )(qkv, selb, ksp, vstp, kwp, vwtp, e2p, tab, ocmp, gl, gnw, table)


def _conv_kernel(b_ref, c_ref, h_ref, cp_ref, hp_ref, cw_ref, cb_ref, gnw_ref, o_ref, u_ref, *, tiles_per_seq):
    tm = b_ref.shape[0]
    first = (pl.program_id(0) % tiles_per_seq) == 0
    u_prev = cp_ref[...] * hp_ref[...]
    u_ref[0:8, :] = jnp.where(first, 0.0, u_prev)
    u = c_ref[...] * h_ref[...]
    u_ref[8:8 + tm, :] = u
    y = (cw_ref[0:1, :] * u_ref[6:6 + tm, :] + cw_ref[1:2, :] * u_ref[7:7 + tm, :]
         + cw_ref[2:3, :] * u + cb_ref[...])
    o = b_ref[...] * y
    n_groups = o.shape[1] // HEAD_DIM
    for g in range(n_groups):
        sl = slice(g * HEAD_DIM, (g + 1) * HEAD_DIM)
        og = o[:, sl]
        yg = og * lax.rsqrt(jnp.mean(og * og, axis=-1, keepdims=True) + EPS)
        o_ref[:, sl] = (yg * gnw_ref[:, sl]).astype(o_ref.dtype)


def conv_mixer(gc, conv_w, conv_b, gnw, S, tm):
    T = gc.shape[0]
    dc = conv_w.shape[1]
    tps = S // tm
    kern = functools.partial(_conv_kernel, tiles_per_seq=tps)
    prev = lambda col: pl.BlockSpec((8, dc), lambda i, col=col: (jnp.maximum(i * (tm // 8) - 1, 0), col))
    cur = lambda col: pl.BlockSpec((tm, dc), lambda i, col=col: (i, col))
    return pl.pallas_call(
        kern,
        grid=(T // tm,),
        in_specs=[cur(0), cur(1), cur(2), prev(1), prev(2),
                  pl.BlockSpec((8, dc), lambda i: (0, 0)),
                  pl.BlockSpec((1, dc), lambda i: (0, 0)),
                  pl.BlockSpec((1, dc), lambda i: (0, 0))],
        out_specs=pl.BlockSpec((tm, dc), lambda i: (i, 0)),
        out_shape=jax.ShapeDtypeStruct((T, dc), BF16),
        scratch_shapes=[pltpu.VMEM((tm + 8, dc), F32)],
        compiler_params=_params(("parallel",)),
    )(gc, gc, gc, gc, gc, conv_w, conv_b, gnw)


def _out_query_route_kernel(ma_ref, mc_ref, wa_ref, wc_ref, x_ref, nw_ref, wq_ref, sk_ref,
                            h_ref, e_ref, g_ref, q_ref, sv_ref, si_ref):
    @pl.when(pl.program_id(0) == 0)
    def _():
        q_ref[...] = jnp.zeros(q_ref.shape, q_ref.dtype)

    for t in range(q_ref.shape[0] // TT_ROUTE):
        _route_tokens(q_ref.at[pl.ds(t * TT_ROUTE, TT_ROUTE)], sk_ref, e_ref.at[pl.ds(t, 1)], g_ref.at[pl.ds(t, 1)],
                      sv_ref, si_ref)

    h = x_ref[...] + _dot(ma_ref[...], wa_ref[...]) + _dot(mc_ref[...], wc_ref[...])
    h_ref[...] = h
    y = h * lax.rsqrt(jnp.mean(h * h, axis=-1, keepdims=True) + EPS)
    q_ref[...] = _dot((y * nw_ref[...]).astype(BF16), wq_ref[...]).astype(q_ref.dtype)


def out_proj_query_route(ma, mc, w, x, norm_w, wq, subkeys, tm):
    T, da = ma.shape
    dc = mc.shape[1]
    D = x.shape[1]
    assert da == dc and w.shape == (da + dc, D) and wq.shape[0] == D and tm % TT_ROUTE == 0
    n = T // tm
    per = tm // TT_ROUTE
    P = PEER_HEADS * PEER_TOPK
    resident = pl.Buffered(1)
    cur = lambda i: (jnp.minimum(i, n - 1), 0)
    prev = lambda i: (jnp.maximum(i - 1, 0), 0, 0)
    return pl.pallas_call(
        _out_query_route_kernel,
        grid=(n + 1,),
        in_specs=[pl.BlockSpec((tm, da), cur),
                  pl.BlockSpec((tm, dc), cur),
                  pl.BlockSpec((da, D), lambda i: (0, 0), pipeline_mode=resident),
                  pl.BlockSpec((dc, D), lambda i: (1, 0), pipeline_mode=resident),
                  pl.BlockSpec((tm, D), cur),
                  pl.BlockSpec((1, D), lambda i: (0, 0)),
                  pl.BlockSpec(wq.shape, lambda i: (0, 0), pipeline_mode=resident),
                  pl.BlockSpec(subkeys.shape, lambda i: (0, 0, 0), pipeline_mode=resident)],
        out_specs=[pl.BlockSpec((tm, D), cur),
                   pl.BlockSpec((per, P, TT_ROUTE), prev), pl.BlockSpec((per, P, TT_ROUTE), prev)],
        out_shape=[jax.ShapeDtypeStruct((T, D), F32),
                   jax.ShapeDtypeStruct((T // TT_ROUTE, P, TT_ROUTE), jnp.int32),
                   jax.ShapeDtypeStruct((T // TT_ROUTE, P, TT_ROUTE), F32)],
        scratch_shapes=[pltpu.VMEM((tm, wq.shape[1]), BF16),
                        pltpu.VMEM((2, PEER_TOPK, TT_ROUTE), F32), pltpu.VMEM((2, PEER_TOPK, TT_ROUTE), F32)],
        compiler_params=_params(("arbitrary",)),
    )(ma, mc, w, w, x, norm_w.reshape(1, D), wq, subkeys)


def _sort_network(n):
    pairs = []
    p = 1
    while p < n:
        k = p
        while k >= 1:
            for j in range(k % p, n - k, 2 * k):
                for i in range(min(k, n - j - k)):
                    if (i + j) // (2 * p) == (i + j + k) // (2 * p):
                        pairs.append((i + j, i + j + k))
            k //= 2
        p *= 2
    return pairs


def _route_tokens(q_ref, sk_ref, e_ref, g_ref, sv_ref, si_ref):
    tt = q_ref.shape[0]
    K = PEER_TOPK
    sub = SUBLANES
    n_io = lax.broadcasted_iota(jnp.int32, (N_KEYS, tt), 0).astype(F32)
    r8 = lax.broadcasted_iota(jnp.int32, (sub, tt), 0).astype(F32)
    ninf = -jnp.inf
    lens = [K // (a + 1) for a in range(N_FIXED_A)]
    lens += [max(K // (b + 1) - N_FIXED_A, 0) for b in range(sub - N_FIXED_A)]
    assert sum(lens) == sum(K // (a + 1) for a in range(K)) and lens[-1] == 0
    fixed_a = r8 < float(N_FIXED_A)
    list_len = jnp.zeros((sub, tt), F32)
    for row, n in enumerate(lens):
        list_len = jnp.where(r8 == float(row), float(n), list_len)

    for h in range(PEER_HEADS):
        for c in range(2):
            col = (h * 2 + c) * HEAD_DIM
            s = _dot_nt(sk_ref[h * 2 + c], q_ref[:, col:col + HEAD_DIM])
            n_col = N_KEYS // sub
            vals = [s[j * sub:(j + 1) * sub, :] for j in range(n_col)]
            idxs = [n_io[j * sub:(j + 1) * sub, :] for j in range(n_col)]
            for lo, hi in _sort_network(n_col):
                swap = (vals[hi] > vals[lo]) | ((vals[hi] == vals[lo]) & (idxs[hi] < idxs[lo]))
                vals[lo], vals[hi] = jnp.where(swap, vals[hi], vals[lo]), jnp.where(swap, vals[lo], vals[hi])
                idxs[lo], idxs[hi] = jnp.where(swap, idxs[hi], idxs[lo]), jnp.where(swap, idxs[lo], idxs[hi])
            for k in range(K):
                m = jnp.max(vals[0], axis=0, keepdims=True)
                idx = jnp.min(jnp.where(vals[0] == m, idxs[0], float(N_KEYS)), axis=0, keepdims=True)
                sv_ref[c, k:k + 1, :] = m
                si_ref[c, k:k + 1, :] = idx
                win = idxs[0] == idx
                for j in range(min(n_col, K) - 1 - k):
                    vals[j] = jnp.where(win, vals[j + 1], vals[j])
                    idxs[j] = jnp.where(win, idxs[j + 1], idxs[j])
        sv0, sv1 = sv_ref[0], sv_ref[1]
        si0, si1 = si_ref[0], si_ref[1]
        sv1_low = pltpu.roll(sv1[0:sub, :], N_FIXED_A, 0)
        sv0_top = sv0[0:sub, :]
        lv, lf = [], []
        for dep in range(K):
            a_dep = min(N_FIXED_A + dep, K - 1)
            val = (jnp.where(fixed_a, sv0_top, sv0[a_dep:a_dep + 1, :])
                   + jnp.where(fixed_a, sv1[dep:dep + 1, :], sv1_low))
            lv.append(jnp.where(list_len > dep, val, ninf))
            lf.append(jnp.where(fixed_a, r8 * float(K) + dep, (N_FIXED_A + dep) * float(K) + r8 - N_FIXED_A))
        cvs, fls = [], []
        for k in range(K):
            m = jnp.max(lv[0], axis=0, keepdims=True)
            fsel = jnp.min(jnp.where(lv[0] == m, lf[0], 1e9), axis=0, keepdims=True)
            win = lf[0] == fsel
            cvs.append(m)
            fls.append(fsel)
            for dep in range(K - 1 - k):
                lv[dep] = jnp.where(win, lv[dep + 1], lv[dep])
                lf[dep] = jnp.where(win, lf[dep + 1], lf[dep])
        cv = jnp.concatenate(cvs, axis=0)
        fl = jnp.concatenate(fls, axis=0)
        a_sel = jnp.floor(fl * (1.0 / K))
        b_sel = fl - a_sel * K
        i1 = jnp.zeros_like(fl)
        i2 = jnp.zeros_like(fl)
        for r in range(K):
            i1 = jnp.where(a_sel == r, si0[r:r + 1, :], i1)
            i2 = jnp.where(b_sel == r, si1[r:r + 1, :], i2)
        ex = i1 * float(N_KEYS) + i2
        ev = jnp.exp(cv - jnp.max(cv, axis=0, keepdims=True))
        gates = ev / jnp.sum(ev, axis=0, keepdims=True)
        e_ref[0, h * K:(h + 1) * K, :] = ex.astype(jnp.int32)
        g_ref[0, h * K:(h + 1) * K, :] = gates


def _peer_gbuild_kernel(e_ref, g_ref, o_ref, i1_ref, i2_ref, gt_ref):
    tt = e_ref.shape[2]
    e = e_ref[0].T
    i1_ref[...] = e >> 7
    i2_ref[...] = e & (N_KEYS - 1)
    gt_ref[...] = g_ref[0].T
    P = e.shape[1]
    k_io = lax.broadcasted_iota(jnp.int32, (N_KEYS, P), 0)
    group = tt

    def body(tg, carry):
        base = pl.multiple_of(tg * group, group)
        r1s = i1_ref[pl.ds(base, group), :]
        r2s = i2_ref[pl.ds(base, group), :]
        rgs = gt_ref[pl.ds(base, group), :]
        for u in range(group):
            lhs = jnp.where(k_io == r1s[u:u + 1, :], rgs[u:u + 1, :], 0.0).astype(BF16)
            rhs = jnp.where(k_io == r2s[u:u + 1, :], 1.0, 0.0).astype(BF16)
            g_t = _dot_nt(lhs, rhs).astype(BF16).astype(F32)
            lo = pltpu.bitcast(g_t[:HALF_KEYS], jnp.uint32) >> 16
            hi = pltpu.bitcast(g_t[HALF_KEYS:], jnp.uint32)
            words = hi | lo
            for a in range(HALF_KEYS // PACK_ROWS):
                o_ref[a, base + u] = words[a * PACK_ROWS:(a + 1) * PACK_ROWS, :]
        return carry

    lax.fori_loop(0, tt // group, body, 0)


def peer_gbuild(ex, gates):
    nt, P, tt = ex.shape
    T = nt * tt
    n_a = HALF_KEYS // PACK_ROWS
    return pl.pallas_call(
        _peer_gbuild_kernel,
        grid=(nt,),
        in_specs=[pl.BlockSpec((1, P, tt), lambda i: (i, 0, 0))] * 2,
        out_specs=pl.BlockSpec((n_a, tt, PACK_ROWS, N_KEYS), lambda i: (0, i, 0, 0)),
        out_shape=jax.ShapeDtypeStruct((n_a, T, PACK_ROWS, N_KEYS), jnp.uint32),
        scratch_shapes=[pltpu.VMEM((tt, P), jnp.int32), pltpu.VMEM((tt, P), jnp.int32),
                        pltpu.VMEM((tt, P), F32)],
        compiler_params=_params(("parallel",)),
    )(ex, gates)


def _peer_dense_kernel(h_ref, nw_ref, fw_ref, u_ref, v_ref, gm_ref, o_ref, hn_ref, acc_ref):
    j = pl.program_id(1)
    tt = h_ref.shape[0]

    @pl.when(j == 0)
    def _():
        x = h_ref[...]
        y = x * lax.rsqrt(jnp.mean(x * x, axis=-1, keepdims=True) + EPS)
        hn_ref[...] = (y * nw_ref[...]).astype(BF16)
        acc_ref[...] = jnp.zeros(acc_ref.shape, F32)

    a = _dot_nt(hn_ref[...], u_ref[...])
    shift = (16 * (1 - j % 2)).astype(jnp.uint32)
    gm = jnp.concatenate(
        [pltpu.bitcast((gm_ref[pl.ds(k, tt, stride=PACK_ROWS), :] << shift) & jnp.uint32(0xFFFF0000), F32)
         for k in range(PACK_ROWS)], axis=1)
    w = (gm * jax.nn.gelu(a)).astype(BF16)
    acc_ref[...] += _dot(w, v_ref[...])

    @pl.when(j == pl.num_programs(1) - 1)
    def _():
        x = h_ref[...] + acc_ref[...]
        y = x * lax.rsqrt(jnp.mean(x * x, axis=-1, keepdims=True) + EPS)
        o_ref[...] = y * fw_ref[...]


def peer_dense(h, ffn_nw, final_nw, u, v, gm_words, tt):
    T, D = h.shape
    E = u.shape[0]
    ec = PACK_ROWS * N_KEYS
    n_a = gm_words.shape[0]
    assert E == 2 * n_a * ec
    gm2 = gm_words.reshape(n_a, T * PACK_ROWS, N_KEYS)
    tab_spec = pl.BlockSpec((ec, D), lambda i, j: ((j % 2) * n_a + j // 2, 0))
    return pl.pallas_call(
        _peer_dense_kernel,
        grid=(T // tt, E // ec),
        in_specs=[pl.BlockSpec((tt, D), lambda i, j: (i, 0)),
                  pl.BlockSpec((1, D), lambda i, j: (0, 0)),
                  pl.BlockSpec((1, D), lambda i, j: (0, 0)),
                  tab_spec, tab_spec,
                  pl.BlockSpec((None, tt * PACK_ROWS, N_KEYS), lambda i, j: (j // 2, i, 0))],
        out_specs=pl.BlockSpec((tt, D), lambda i, j: (i, 0)),
        out_shape=jax.ShapeDtypeStruct((T, D), F32),
        scratch_shapes=[pltpu.VMEM((tt, D), BF16), pltpu.VMEM((tt, D), F32)],
        compiler_params=_params(("parallel", "arbitrary")),
    )(h, ffn_nw.reshape(1, D), final_nw.reshape(1, D), u, v, gm2)


def _t5_bucket_np(n_dist):
    d = np.arange(n_dist)
    max_exact = NUM_BUCKETS // 2
    nf = np.maximum(d, 1).astype(np.float64)
    large = max_exact + (np.log(nf / max_exact) / math.log(MAX_DISTANCE / max_exact)
                         * (NUM_BUCKETS - max_exact)).astype(np.int64)
    large = np.minimum(large, NUM_BUCKETS - 1)
    return np.where(d < max_exact, d, large).astype(np.int32)


def _overlap_t_np(S):
    n_c = (S - L_CMP) // STRIDE_CMP + 1
    n_sel = S // L_SEL
    pos = np.arange(n_c)[:, None] * STRIDE_CMP + np.arange(L_CMP)[None, :]
    m = np.zeros((n_c + 1, n_sel), np.float32)
    np.add.at(m, (np.repeat(np.arange(n_c), L_CMP), (pos // L_SEL).reshape(-1)), 1.0 / L_CMP)
    return np.ascontiguousarray(m.T)


def _block_onehot_padded_np(S):
    assert S // L_SEL < HEAD_DIM
    e2 = np.zeros((WINDOW + S, HEAD_DIM), np.float32)
    e2[WINDOW + np.arange(S), np.arange(S) // L_SEL] = 1.0
    e2[:WINDOW, HEAD_DIM - 1] = SEL_OFF
    return e2


def nsa_conv_mix(xt, B, S, attn_norm_w, w_in, w_cmp_k, w_cmp_v, cmp_pos, conv_w, conv_b,
                 attn_gnw, conv_gnw, rel_bias, tm=TM_PROJ, tm_conv=TM_CONV, cast_also=(), table=None):
    T, D = xt.shape
    dq = N_Q_HEADS * HEAD_DIM
    dkv = N_KV_HEADS * HEAD_DIM
    n_attn = dq + 6 * dkv
    n_gate = 3 * N_Q_HEADS
    dc = (w_in.shape[1] - n_attn - n_gate) // 3
    w_t = w_in.T
    w_attn = cast_rows(w_t, 0, n_attn)
    w_conv = cast_rows(w_t, n_attn + n_gate, 3 * dc)
    w_gate = jnp.pad(w_t[n_attn:n_attn + n_gate], ((0, HEAD_DIM - n_gate), (0, 0))).astype(BF16)

    qkv, gc, gl = in_proj(xt, attn_norm_w, w_attn, w_conv, w_gate, tm)

    n16 = S // STRIDE_CMP
    k16 = qkv[:, dq:dq + dkv].reshape(B * n16, STRIDE_CMP * dkv)
    v16 = qkv[:, dq + dkv:dq + 2 * dkv].reshape(B * n16, STRIDE_CMP * dkv)

    def wflat(w, lo):
        return w[lo:lo + STRIDE_CMP].reshape(STRIDE_CMP * HEAD_DIM, HEAD_DIM).astype(BF16)

    def posrow(lo):
        return cmp_pos[lo:lo + STRIDE_CMP].reshape(1, STRIDE_CMP * HEAD_DIM)

    kc, vc = compress(k16, v16, posrow(0), posrow(STRIDE_CMP),
                      wflat(w_cmp_k, 0), wflat(w_cmp_k, STRIDE_CMP),
                      wflat(w_cmp_v, 0), wflat(w_cmp_v, STRIDE_CMP), B)

    tab = rel_bias[_t5_bucket_np(HEAD_DIM)].T
    ovt = jnp.asarray(_overlap_t_np(S), BF16)
    e2p = jnp.asarray(_block_onehot_padded_np(S), BF16)

    ocmp, selb, *casts = cmp_select(qkv, kc, vc, tab, ovt, B, S, cast_also=cast_also)
    ksp, vstp, kwp, vwtp = repack_kv(qkv, B, S, first_col_block=dq // dkv + 2)
    attn, table_bf = sel_win(qkv, selb, ksp, vstp, kwp, vwtp, e2p, tab, ocmp, gl, attn_gnw.reshape(1, dq), table, B, S)
    cw8 = jnp.pad(conv_w, ((0, 8 - conv_w.shape[0]), (0, 0)))
    conv = conv_mixer(gc, cw8, conv_b.reshape(1, dc), conv_gnw.reshape(1, dc), S, tm_conv)
    return (attn, conv, table_bf, *casts)


def peer_block(h, ex, gates, ffn_nw, final_nw, u_bf, v_bf, tt=TT_DENSE):
    return peer_dense(h, ffn_nw, final_nw, u_bf, v_bf, peer_gbuild(ex, gates), tt)


def kernel(x, attn_norm_w, w_in, w_cmp_k, w_cmp_v, cmp_pos, conv_w, conv_b, attn_group_norm_w,
           conv_group_norm_w, w_out, rel_bias, ffn_norm_w, peer_wq, peer_subkeys, peer_u, peer_v,
           final_norm_w):
    B, S, D = x.shape
    T = B * S
    xt = x.reshape(T, D)
    attn, conv, u_bf, wo, wq, v_bf = nsa_conv_mix(
        xt, B, S, attn_norm_w[0], w_in[0], w_cmp_k[0], w_cmp_v[0], cmp_pos[0], conv_w[0], conv_b[0],
        attn_group_norm_w[0], conv_group_norm_w[0], rel_bias,
        cast_also=(w_out[0], peer_wq[0], peer_v[0]), table=peer_u[0])
    sk = peer_subkeys[0].reshape(PEER_HEADS * 2, N_KEYS, peer_subkeys.shape[-1]).astype(BF16)
    h, ex, gates = out_proj_query_route(attn, conv, wo, xt, ffn_norm_w[0], wq, sk, TM_OUT)
    out = peer_block(h, ex, gates, ffn_norm_w[0], final_norm_w, u_bf, v_bf)
    return out.reshape(B, S, D)
```
